```python
import math
import jax, jax.numpy as jnp
from jax import lax
import numpy as np

D_MODEL = 1024
BATCH = 4
SEQ = 4096
DEPTH = 2
DEC_BATCH = 8
DEC_SEQ = 32
PAST_LEN = 2048

CHUNK = 64
N_EVEN = (DEPTH + 1) // 2
N_ODD = DEPTH // 2
SSM_WIDTH = D_MODEL // 2
SSM_GROUP = 16
SSM_GROUPS = SSM_WIDTH // SSM_GROUP
SSM_STATE = 64
DT_MIN = 1e-3
DT_MAX = 1e-1
HEAD_DIM = 64
N_HEADS = (D_MODEL // 2) // HEAD_DIM
N_KV_HEADS = 2
GQ = N_HEADS // N_KV_HEADS
WINDOW = 128
WINDOW_CHUNKS = WINDOW // CHUNK
ROPE_THETA = 10000.0
EVEN_IN = SSM_WIDTH + (N_HEADS + 2 * N_KV_HEADS) * HEAD_DIM
CONV_WIDTH = 3
N_MEM = 256
X_HEADS = 4
X_HEAD_DIM = D_MODEL // X_HEADS
D_FF = 2816
EPS = 1e-6
NEG = -1e30

kernel_name = 'hybrid_s5_swa_shortconv_stream_step'

F32 = jnp.float32


def rms_norm(x, g):
    xf = x.astype(F32)
    y = xf * lax.rsqrt(jnp.mean(xf * xf, axis=-1, keepdims=True) + EPS)
    return (y * g.astype(F32)).astype(x.dtype)


def rope(t, pos):
    half = HEAD_DIM // 2
    inv = ROPE_THETA ** (-jnp.arange(half, dtype=F32) * 2.0 / HEAD_DIM)
    ang = pos[:, None] * inv[None, :]
    cos = jnp.cos(ang)[None, :, None, :]
    sin = jnp.sin(ang)[None, :, None, :]
    tf = t.astype(F32)
    t1, t2 = tf[..., :half], tf[..., half:]
    return jnp.concatenate([t1 * cos - t2 * sin, t2 * cos + t1 * sin], axis=-1).astype(t.dtype)


def causal_conv(x, prev, w):
    seq = x.shape[1]
    xp = jnp.concatenate([prev.astype(x.dtype), x], axis=1)
    y = w[0] * xp[:, 0:seq]
    for j in range(1, CONV_WIDTH):
        y = y + w[j] * xp[:, j:j + seq]
    return y, xp[:, -(CONV_WIDTH - 1):]


def _scan_op(e1, e2):
    a1, b1 = e1
    a2, b2 = e2
    return a1 * a2, a2 * b1 + b2


def s5_mixer(u, h0_re, h0_im, a_re, a_im, log_dt, b_re, b_im, c_re, c_im, d, w_glu):
    bsz, seq, _ = u.shape
    uf = u.astype(F32).reshape(bsz, seq, SSM_GROUPS, SSM_GROUP)
    lam = lax.complex(a_re.astype(F32), a_im.astype(F32))
    dt = jnp.exp(log_dt.astype(F32))
    a_bar = jnp.exp(lam * dt)
    b_bar = ((a_bar - 1.0) / lam)[..., None] * lax.complex(b_re.astype(F32), b_im.astype(F32))
    bu = jnp.einsum('blgh,gph->blgp', uf.astype(jnp.complex64), b_bar)
    h0 = lax.complex(h0_re.astype(F32), h0_im.astype(F32))
    bu = bu.at[:, 0].add(a_bar[None] * h0)
    a_seq = jnp.broadcast_to(a_bar, bu.shape)
    _, h = lax.associative_scan(_scan_op, (a_seq, bu), axis=1)
    c = lax.complex(c_re.astype(F32), c_im.astype(F32))
    y = jnp.real(jnp.einsum('blgp,ghp->blgh', h, c)) + d.astype(F32) * uf
    y = y.reshape(bsz, seq, SSM_WIDTH)
    g = jax.nn.gelu(y)
    out = g * jax.nn.sigmoid(g @ w_glu.astype(F32))
    h_last = h[:, -1]
    return out.astype(u.dtype), jnp.real(h_last), jnp.imag(h_last)


def swa_attention(q, k, v, past_k, past_v, sinks, start):
    bsz, seq = q.shape[0], q.shape[1]
    n_blk = -(-seq // CHUNK)
    l_pad = n_blk * CHUNK
    pad = l_pad - seq
    front = WINDOW - past_k.shape[1]

    def pad4(t, a, b):
        return jnp.pad(t, ((0, 0), (a, b), (0, 0), (0, 0)))

    q = pad4(q, 0, pad)
    k_all = jnp.concatenate([pad4(past_k.astype(k.dtype), front, 0), pad4(k, 0, pad)], axis=1)
    v_all = jnp.concatenate([pad4(past_v.astype(v.dtype), front, 0), pad4(v, 0, pad)], axis=1)
    n_key = WINDOW + l_pad
    idx = jnp.arange(n_blk)[:, None] * CHUNK + jnp.arange(WINDOW + CHUNK)[None, :]
    k_blk = k_all[:, idx]
    v_blk = v_all[:, idx]
    slot = jnp.arange(n_key)
    k_pos = start - WINDOW + slot
    k_ok = (k_pos >= 0) & (slot < WINDOW + seq)
    k_pos_b = k_pos[idx]
    k_ok_b = k_ok[idx]
    q_pos = start + jnp.arange(l_pad).reshape(n_blk, CHUNK)
    q_chunk = q_pos // CHUNK
    k_chunk = k_pos_b // CHUNK
    mask = (k_ok_b[:, None, :]
            & (k_chunk[:, None, :] <= q_chunk[:, :, None])
            & (k_chunk[:, None, :] >= q_chunk[:, :, None] - WINDOW_CHUNKS))
    q_blk = q.reshape(bsz, n_blk, CHUNK, N_KV_HEADS, GQ, HEAD_DIM)
    s = jnp.einsum('bnqkgd,bnskd->bnkgqs', q_blk, k_blk, preferred_element_type=F32) * (HEAD_DIM ** -0.5)
    s = jnp.where(mask[None, :, None, None], s, NEG)
    sink = jnp.broadcast_to(sinks.astype(F32).reshape(1, 1, N_KV_HEADS, GQ, 1, 1), s.shape[:-1] + (1,))
    p = jax.nn.softmax(jnp.concatenate([s, sink], axis=-1), axis=-1)[..., :-1]
    o = jnp.einsum('bnkgqs,bnskd->bnqkgd', p.astype(v.dtype), v_blk)
    return o.reshape(bsz, l_pad, N_HEADS * HEAD_DIM)[:, :seq]


def cross_attention(h, mem_k, mem_v, wq, wo):
    bsz, seq, _ = h.shape
    q = (h @ wq).reshape(bsz, seq, X_HEADS, X_HEAD_DIM)
    s = jnp.einsum('blhd,bmhd->bhlm', q, mem_k, preferred_element_type=F32) * (X_HEAD_DIM ** -0.5)
    p = jax.nn.softmax(s, axis=-1).astype(mem_v.dtype)
    o = jnp.einsum('bhlm,bmhd->blhd', p, mem_v).reshape(bsz, seq, D_MODEL)
    return o @ wo


def conv_ffn(h, prev, w_up, conv_w, w_down):
    up = h @ w_up
    gate, val = up[..., :D_FF], up[..., D_FF:]
    gate_c, new_prev = causal_conv(gate, prev, conv_w)
    return (jax.nn.silu(gate_c) * val) @ w_down, new_prev


def trunk(x, start, past_k, past_v, ssm_re, ssm_im, conv_mix_prev, conv_ffn_prev, mem_k, mem_v, p):
    bsz, seq, _ = x.shape
    pos = (start + jnp.arange(seq)).astype(F32)
    new_k, new_v, new_re, new_im, new_cm, new_cf = [], [], [], [], [], []
    o0 = SSM_WIDTH
    o1 = o0 + N_HEADS * HEAD_DIM
    o2 = o1 + N_KV_HEADS * HEAD_DIM
    for layer in range(DEPTH):
        h = rms_norm(x, p['norm_mix'][layer])
        if layer % 2 == 0:
            e = layer // 2
            proj = h @ p['w_in_even'][e]
            u = proj[..., :o0]
            q = rope(proj[..., o0:o1].reshape(bsz, seq, N_HEADS, HEAD_DIM), pos)
            k = rope(proj[..., o1:o2].reshape(bsz, seq, N_KV_HEADS, HEAD_DIM), pos)
            v = proj[..., o2:].reshape(bsz, seq, N_KV_HEADS, HEAD_DIM)
            y_a, h_re, h_im = s5_mixer(u, ssm_re[e], ssm_im[e], p['ssm_a_re'][e], p['ssm_a_im'][e],
                                       p['ssm_log_dt'][e], p['ssm_b_re'][e], p['ssm_b_im'][e],
                                       p['ssm_c_re'][e], p['ssm_c_im'][e], p['ssm_d'][e], p['w_glu'][e])
            y_b = swa_attention(q, k, v, past_k[e], past_v[e], p['attn_sinks'][e], start)
            x = x + jnp.concatenate([y_a, y_b], axis=-1) @ p['w_out_even'][e]
            new_k.append(k)
            new_v.append(v)
            new_re.append(h_re)
            new_im.append(h_im)
        else:
            o = layer // 2
            proj = h @ p['w_in_odd'][o]
            gate_b, gate_c, z = jnp.split(proj, 3, axis=-1)
            z_conv, cm = causal_conv(gate_c * z, conv_mix_prev[o], p['conv_mix_w'][o])
            x = x + (gate_b * z_conv) @ p['w_out_odd'][o]
            new_cm.append(cm)
        h = rms_norm(x, p['norm_xattn'][layer])
        x = x + cross_attention(h, mem_k[layer], mem_v[layer], p['xattn_wq'][layer], p['xattn_wo'][layer])
        h = rms_norm(x, p['norm_ffn'][layer])
        f, cf = conv_ffn(h, conv_ffn_prev[layer], p['ffn_w_up'][layer], p['ffn_conv_w'][layer], p['ffn_w_down'][layer])
        x = x + f
        new_cf.append(cf)
    y = rms_norm(x, p['norm_final'])
    return (y, jnp.stack(new_k), jnp.stack(new_v), jnp.stack(new_re), jnp.stack(new_im),
            jnp.stack(new_cm), jnp.stack(new_cf))


def setup_inputs(seed: int = 0) -> dict:
    key = jax.random.key(seed)
    ks = iter(jax.random.split(key, 48))

    def nrm(shape, scale=1.0):
        return scale * jax.random.normal(next(ks), shape, F32)

    ne, no = N_EVEN, N_ODD
    win_rows = min(WINDOW, PAST_LEN)
    cw = CONV_WIDTH - 1
    inp = {}
    inp['x_prompt'] = nrm((BATCH, SEQ, D_MODEL))
    inp['x_sample'] = nrm((DEC_BATCH, DEC_SEQ, D_MODEL))
    inp['mem_prompt'] = nrm((BATCH, N_MEM, D_MODEL))
    inp['cache_win_k'] = nrm((ne, DEC_BATCH, win_rows, N_KV_HEADS, HEAD_DIM))
    inp['cache_win_v'] = nrm((ne, DEC_BATCH, win_rows, N_KV_HEADS, HEAD_DIM))
    inp['state_ssm_re'] = nrm((ne, DEC_BATCH, SSM_GROUPS, SSM_STATE), 0.05)
    inp['state_ssm_im'] = nrm((ne, DEC_BATCH, SSM_GROUPS, SSM_STATE), 0.05)
    inp['state_conv_mix'] = nrm((no, DEC_BATCH, cw, D_MODEL))
    inp['state_conv_ffn'] = nrm((DEPTH, DEC_BATCH, cw, D_FF))
    inp['cache_mem_k'] = nrm((DEPTH, DEC_BATCH, N_MEM, X_HEADS, X_HEAD_DIM))
    inp['cache_mem_v'] = nrm((DEPTH, DEC_BATCH, N_MEM, X_HEADS, X_HEAD_DIM))
    inp['norm_mix'] = 1.0 + nrm((DEPTH, D_MODEL), 0.02)
    inp['norm_xattn'] = 1.0 + nrm((DEPTH, D_MODEL), 0.02)
    inp['norm_ffn'] = 1.0 + nrm((DEPTH, D_MODEL), 0.02)
    inp['norm_final'] = 1.0 + nrm((D_MODEL,), 0.02)
    inp['w_in_even'] = nrm((ne, D_MODEL, EVEN_IN), D_MODEL ** -0.5)
    inp['w_out_even'] = nrm((ne, D_MODEL, D_MODEL), D_MODEL ** -0.5)
    inp['ssm_a_re'] = -0.5 + nrm((ne, SSM_GROUPS, SSM_STATE), 0.01)
    inp['ssm_a_im'] = jnp.pi * jnp.arange(SSM_STATE, dtype=F32) + nrm((ne, SSM_GROUPS, SSM_STATE), 0.01)
    inp['ssm_log_dt'] = jax.random.uniform(next(ks), (ne, SSM_GROUPS, SSM_STATE), F32,
                                           math.log(DT_MIN), math.log(DT_MAX))
    inp['ssm_b_re'] = nrm((ne, SSM_GROUPS, SSM_STATE, SSM_GROUP), (2 * SSM_GROUP) ** -0.5)
    inp['ssm_b_im'] = nrm((ne, SSM_GROUPS, SSM_STATE, SSM_GROUP), (2 * SSM_GROUP) ** -0.5)
    inp['ssm_c_re'] = nrm((ne, SSM_GROUPS, SSM_GROUP, SSM_STATE), SSM_STATE ** -0.5)
    inp['ssm_c_im'] = nrm((ne, SSM_GROUPS, SSM_GROUP, SSM_STATE), SSM_STATE ** -0.5)
    inp['ssm_d'] = nrm((ne, SSM_GROUPS, SSM_GROUP))
    inp['w_glu'] = nrm((ne, SSM_WIDTH, SSM_WIDTH), SSM_WIDTH ** -0.5)
    inp['attn_sinks'] = nrm((ne, N_HEADS), 0.5)
    inp['w_in_odd'] = nrm((no, D_MODEL, 3 * D_MODEL), D_MODEL ** -0.5)
    inp['conv_mix_w'] = nrm((no, CONV_WIDTH, D_MODEL), CONV_WIDTH ** -0.5)
    inp['w_out_odd'] = nrm((no, D_MODEL, D_MODEL), D_MODEL ** -0.5)
    inp['xattn_wq'] = nrm((DEPTH, D_MODEL, D_MODEL), D_MODEL ** -0.5)
    inp['xattn_wkv'] = nrm((DEPTH, D_MODEL, 2 * D_MODEL), D_MODEL ** -0.5)
    inp['xattn_wo'] = nrm((DEPTH, D_MODEL, D_MODEL), D_MODEL ** -0.5)
    inp['ffn_w_up'] = nrm((DEPTH, D_MODEL, 2 * D_FF), D_MODEL ** -0.5)
    inp['ffn_conv_w'] = nrm((DEPTH, CONV_WIDTH, D_FF), CONV_WIDTH ** -0.5)
    inp['ffn_w_down'] = nrm((DEPTH, D_FF, D_MODEL), D_FF ** -0.5)
    return inp


def reference(x_prompt, x_sample, mem_prompt, cache_win_k, cache_win_v, state_ssm_re, state_ssm_im,
              state_conv_mix, state_conv_ffn, cache_mem_k, cache_mem_v, norm_mix, norm_xattn, norm_ffn,
              norm_final, w_in_even, w_out_even, ssm_a_re, ssm_a_im, ssm_log_dt, ssm_b_re, ssm_b_im,
              ssm_c_re, ssm_c_im, ssm_d, w_glu, attn_sinks, w_in_odd, conv_mix_w, w_out_odd, xattn_wq,
              xattn_wkv, xattn_wo, ffn_w_up, ffn_conv_w, ffn_w_down):
    params = dict(norm_mix=norm_mix, norm_xattn=norm_xattn, norm_ffn=norm_ffn, norm_final=norm_final,
                  w_in_even=w_in_even, w_out_even=w_out_even, ssm_a_re=ssm_a_re, ssm_a_im=ssm_a_im,
                  ssm_log_dt=ssm_log_dt, ssm_b_re=ssm_b_re, ssm_b_im=ssm_b_im, ssm_c_re=ssm_c_re,
                  ssm_c_im=ssm_c_im, ssm_d=ssm_d, w_glu=w_glu, attn_sinks=attn_sinks, w_in_odd=w_in_odd,
                  conv_mix_w=conv_mix_w, w_out_odd=w_out_odd, xattn_wq=xattn_wq, xattn_wo=xattn_wo,
                  ffn_w_up=ffn_w_up, ffn_conv_w=ffn_conv_w, ffn_w_down=ffn_w_down)
    dt = x_prompt.dtype
    bp = x_prompt.shape[0]
    cw = CONV_WIDTH - 1
    kv = jnp.einsum('bmd,lde->lbme', mem_prompt, xattn_wkv)
    mem_k_p = kv[..., :D_MODEL].reshape(DEPTH, bp, N_MEM, X_HEADS, X_HEAD_DIM)
    mem_v_p = kv[..., D_MODEL:].reshape(DEPTH, bp, N_MEM, X_HEADS, X_HEAD_DIM)
    zk = jnp.zeros((N_EVEN, bp, WINDOW, N_KV_HEADS, HEAD_DIM), dt)
    zs = jnp.zeros((N_EVEN, bp, SSM_GROUPS, SSM_STATE), F32)
    zcm = jnp.zeros((N_ODD, bp, cw, D_MODEL), dt)
    zcf = jnp.zeros((DEPTH, bp, cw, D_FF), dt)
    y_prompt, k_p, v_p, ssm_re_p, ssm_im_p, conv_mix_p, conv_ffn_p = trunk(
        x_prompt, 0, zk, zk, zs, zs, zcm, zcf, mem_k_p, mem_v_p, params)
    win_k_p = k_p[:, :, -WINDOW:]
    win_v_p = v_p[:, :, -WINDOW:]
    y_sample, win_k_s, win_v_s, ssm_re_s, ssm_im_s, conv_mix_s, conv_ffn_s = trunk(
        x_sample, PAST_LEN, cache_win_k, cache_win_v, state_ssm_re, state_ssm_im,
        state_conv_mix, state_conv_ffn, cache_mem_k, cache_mem_v, params)
    return (y_prompt, y_sample, win_k_p, win_v_p, ssm_re_p, ssm_im_p, conv_mix_p, conv_ffn_p,
            mem_k_p, mem_v_p, win_k_s, win_v_s, ssm_re_s, ssm_im_s, conv_mix_s, conv_ffn_s)
```

```python
import functools
import math

import numpy as np
import jax
import jax.numpy as jnp
from jax import lax
from jax.experimental import pallas as pl
from jax.experimental.pallas import tpu as pltpu

F32 = jnp.float32
BF16 = jnp.bfloat16

D_MODEL = 1024
DEPTH = 2
PAST_LEN = 2048
CHUNK = 64
SSM_WIDTH = 512
SSM_GROUP = 16
SSM_GROUPS = 32
SSM_STATE = 64
HEAD_DIM = 64
N_HEADS = 8
N_KV_HEADS = 2
GQ = N_HEADS // N_KV_HEADS
WINDOW = 128
ROPE_THETA = 10000.0
KV_WIDTH = N_KV_HEADS * HEAD_DIM
EVEN_IN = SSM_WIDTH + N_HEADS * HEAD_DIM + 2 * KV_WIDTH
CONV_WIDTH = 3
N_MEM = 256
X_HEADS = 4
X_HEAD_DIM = 256
D_FF = 2816
EPS = 1e-6
NEG = -1e30

LANES = 128
SUBLANES = 8
S5_T = 8
S5_TILES = SSM_WIDTH // LANES
S5_TILE_STATES = (LANES // SSM_GROUP) * SSM_STATE
S5_STATES = SSM_GROUPS * SSM_STATE
FF_CHUNK = 1408
VMEM_LIMIT = 56 * 1024 * 1024


def _const_spec(shape, index=None):
    idx = tuple(index) if index is not None else (0,) * len(shape)
    return pl.BlockSpec(shape, lambda *_: idx, pipeline_mode=pl.Buffered(1))


def _params(n_axes):
    return pltpu.CompilerParams(dimension_semantics=("arbitrary",) * n_axes,
                                vmem_limit_bytes=VMEM_LIMIT)


def _rms(x, g):
    return x * lax.rsqrt(jnp.mean(x * x, axis=-1, keepdims=True) + EPS) * g


def _dot(a, b):
    return jnp.dot(a, b, preferred_element_type=F32)


def _dot_t(a, b):
    return lax.dot_general(a, b, (((1,), (1,)), ((), ())), preferred_element_type=F32)


def _rope_table_body(cos_ref, sina_ref, sinb_ref, *, start, tl):
    i = pl.program_id(0)
    row = lax.broadcasted_iota(jnp.int32, (tl, LANES), 0)
    lane = lax.broadcasted_iota(jnp.int32, (tl, LANES), 1)
    pos = (start + i * tl + row).astype(F32)
    half = HEAD_DIM // 2
    freq = (lane[0:1] & (half - 1)).astype(F32)
    inv = jnp.exp(freq * (-2.0 / HEAD_DIM * math.log(ROPE_THETA)))
    ang = pos * inv
    c = jnp.cos(ang)
    s = jnp.sin(ang)
    first = (lane & (HEAD_DIM - 1)) < HEAD_DIM // 2
    cos_ref[...] = c
    sina_ref[...] = jnp.where(first, -s, 0.0)
    sinb_ref[...] = jnp.where(first, 0.0, s)


def _rope_tables(seq, start, tl):
    spec = pl.BlockSpec((tl, LANES), lambda i: (i, 0))
    return pl.pallas_call(
        functools.partial(_rope_table_body, start=start, tl=tl),
        grid=(seq // tl,),
        in_specs=[],
        out_specs=[spec, spec, spec],
        out_shape=[jax.ShapeDtypeStruct((seq, LANES), F32)] * 3,
        compiler_params=_params(1),
        name="rope_tables",
    )()


def _rope(t, cos, sina, sinb):
    n = t.shape[1]
    half = HEAD_DIM // 2
    return t * cos + pltpu.roll(t, n - half, 1) * sina + pltpu.roll(t, half, 1) * sinb


def _inproj_even_body(x_ref, g_ref, w_ref, cos_ref, sa_ref, sb_ref, u_ref, q_ref, k_ref, v_ref):
    h = _rms(x_ref[0], g_ref[...]).astype(BF16)
    proj = _dot(h, w_ref[0])
    o0 = SSM_WIDTH
    o1 = o0 + N_HEADS * HEAD_DIM
    o2 = o1 + KV_WIDTH
    cos, sa, sb = cos_ref[...], sa_ref[...], sb_ref[...]
    rep = (o1 - o0) // LANES
    cos_q = jnp.concatenate([cos] * rep, axis=1)
    sa_q = jnp.concatenate([sa] * rep, axis=1)
    sb_q = jnp.concatenate([sb] * rep, axis=1)
    u_ref[0] = proj[:, :o0]
    q_ref[0] = _rope(proj[:, o0:o1], cos_q, sa_q, sb_q).astype(BF16)
    k_ref[0] = _rope(proj[:, o1:o2], cos, sa, sb)
    v_ref[0] = proj[:, o2:]


def _inproj_even(x, g, w_in, tables, layer_e, tm):
    bsz, seq, _ = x.shape
    cos, sa, sb = tables
    tok = lambda w: pl.BlockSpec((1, tm, w), lambda b, i: (b, i, 0))
    tab = pl.BlockSpec((tm, LANES), lambda b, i: (i, 0))
    return pl.pallas_call(
        _inproj_even_body,
        grid=(bsz, seq // tm),
        in_specs=[tok(D_MODEL), _const_spec((1, D_MODEL)),
                  _const_spec((1, D_MODEL, EVEN_IN), (layer_e, 0, 0)), tab, tab, tab],
        out_specs=[tok(SSM_WIDTH), tok(N_HEADS * HEAD_DIM), tok(KV_WIDTH), tok(KV_WIDTH)],
        out_shape=[jax.ShapeDtypeStruct((bsz, seq, SSM_WIDTH), F32),
                   jax.ShapeDtypeStruct((bsz, seq, N_HEADS * HEAD_DIM), BF16),
                   jax.ShapeDtypeStruct((bsz, seq, KV_WIDTH), F32),
                   jax.ShapeDtypeStruct((bsz, seq, KV_WIDTH), F32)],
        compiler_params=_params(2),
        name="inproj_even",
    )(x, g, w_in, cos, sa, sb)


def _s5_prep_body(lr_ref, li_ref, ld_ref, bre_ref, bim_ref, lrc_ref, lic_ref, ldc_ref,
                  cre_ref, cim_ref, wcat_ref, e_ref):
    ts = S5_TILE_STATES
    tw = S5_T * LANES
    lr, li = lr_ref[0], li_ref[0]
    dt = jnp.exp(ld_ref[0])
    zr, zi = lr * dt, li * dt

    def pow_row(k):
        mag = jnp.exp(k * zr)
        return mag * jnp.cos(k * zi), mag * jnp.sin(k * zi)

    ar, ai = pow_row(1.0)
    nrm = lr * lr + li * li
    cbr = ((ar - 1.0) * lr + ai * li) / nrm
    cbi = (ai * lr - (ar - 1.0) * li) / nrm
    b_re, b_im = bre_ref[0], bim_ref[0]
    bbr = cbr * b_re - cbi * b_im
    bbi = cbr * b_im + cbi * b_re
    for s in range(S5_T):
        pr, pi = pow_row(float(S5_T - 1 - s))
        rows = slice(s * LANES, (s + 1) * LANES)
        wcat_ref[0, rows, tw:tw + ts] = (pr * bbr - pi * bbi).astype(BF16)
        wcat_ref[0, rows, tw + ts:tw + 2 * ts] = (pr * bbi + pi * bbr).astype(BF16)
        for t in range(s):
            wcat_ref[0, rows, t * LANES:(t + 1) * LANES] = jnp.zeros((LANES, LANES), BF16)
    dtc = jnp.exp(ldc_ref[0])
    zrc, zic = lrc_ref[0] * dtc, lic_ref[0] * dtc
    c_re, c_im = cre_ref[0], cim_ref[0]
    bb = jnp.concatenate([bbr, bbi], axis=1).astype(BF16)
    for k in range(S5_T + 1):
        mag = jnp.exp(float(k) * zrc)
        pr, pi = mag * jnp.cos(float(k) * zic), mag * jnp.sin(float(k) * zic)
        blk = jnp.concatenate([pr * c_re - pi * c_im, -(pr * c_im + pi * c_re)], axis=0).astype(BF16)
        if k >= 1:
            e_ref[0, :, (k - 1) * LANES:k * LANES] = blk
        if k < S5_T:
            lag = _dot(bb, blk).astype(BF16)
            for s in range(S5_T - k):
                t = s + k
                wcat_ref[0, s * LANES:(s + 1) * LANES, t * LANES:(t + 1) * LANES] = lag


def _s5_prep(a_re, a_im, log_dt, b_re, b_im, c_re, c_im):
    gpt = LANES // SSM_GROUP
    eye = jnp.eye(gpt, dtype=bool)

    def rows(p):
        return p.reshape(S5_TILES, 1, S5_TILE_STATES)

    def cols(p):
        return jnp.broadcast_to(p.reshape(S5_TILES, S5_TILE_STATES, 1), (S5_TILES, S5_TILE_STATES, LANES))

    def bmat(b):
        bt = b.reshape(S5_TILES, gpt, SSM_STATE, SSM_GROUP).transpose(0, 1, 3, 2)
        full = jnp.where(eye[None, :, None, :, None], bt[:, :, :, None, :], 0.0)
        return full.reshape(S5_TILES, LANES, S5_TILE_STATES)

    def cmat(c):
        ct = c.reshape(S5_TILES, gpt, SSM_GROUP, SSM_STATE).transpose(0, 1, 3, 2)
        full = jnp.where(eye[None, :, None, :, None], ct[:, :, :, None, :], 0.0)
        return full.reshape(S5_TILES, S5_TILE_STATES, LANES)

    row_spec = pl.BlockSpec((1, 1, S5_TILE_STATES), lambda j: (j, 0, 0))
    b_spec = pl.BlockSpec((1, LANES, S5_TILE_STATES), lambda j: (j, 0, 0))
    c_spec = pl.BlockSpec((1, S5_TILE_STATES, LANES), lambda j: (j, 0, 0))
    tw = S5_T * LANES
    return pl.pallas_call(
        _s5_prep_body,
        grid=(S5_TILES,),
        in_specs=[row_spec] * 3 + [b_spec] * 2 + [c_spec] * 5,
        out_specs=[pl.BlockSpec((1, tw, 2 * tw), lambda j: (j, 0, 0)),
                   pl.BlockSpec((1, 2 * S5_TILE_STATES, tw), lambda j: (j, 0, 0))],
        out_shape=[jax.ShapeDtypeStruct((S5_TILES, tw, 2 * tw), BF16),
                   jax.ShapeDtypeStruct((S5_TILES, 2 * S5_TILE_STATES, tw), BF16)],
        compiler_params=_params(1),
        name="s5_prep",
    )(rows(a_re), rows(a_im), rows(log_dt), bmat(b_re), bmat(b_im),
      cols(a_re), cols(a_im), cols(log_dt), cmat(c_re), cmat(c_im))


def _s5_pow_body(ex_ref, lr_ref, li_ref, ld_ref, o_ref):
    dt = jnp.exp(ld_ref[...])
    zr, zi = lr_ref[...] * dt, li_ref[...] * dt
    ex = ex_ref[...]
    mag = jnp.exp(ex * zr)
    o_ref[:, :S5_STATES] = mag * jnp.cos(ex * zi)
    o_ref[:, S5_STATES:] = mag * jnp.sin(ex * zi)


def _s5_pow_table(exponents, a_re, a_im, log_dt):
    n = len(exponents)
    ex = jnp.asarray(np.asarray(exponents, np.float32)[:, None])
    flat = lambda p: p.reshape(1, S5_STATES)
    return pl.pallas_call(
        _s5_pow_body,
        grid=(1,),
        in_specs=[pl.BlockSpec((n, 1), lambda i: (0, 0))] + [pl.BlockSpec((1, S5_STATES), lambda i: (0, 0))] * 3,
        out_specs=pl.BlockSpec((n, 2 * S5_STATES), lambda i: (0, 0)),
        out_shape=jax.ShapeDtypeStruct((n, 2 * S5_STATES), F32),
        compiler_params=_params(1),
        name="s5_pow_table",
    )(ex, flat(a_re), flat(a_im), flat(log_dt))


def _s5_body(u_ref, h0_ref, wcat_ref, e_ref, pow2_ref, powr_ref, d_ref, wglu_ref,
             y_ref, hout_ref, carry_ref, *, rows, rs, nsteps):
    ns = S5_STATES
    ts = S5_TILE_STATES
    tw = S5_T * LANES

    @pl.when(pl.program_id(1) == 0)
    def _():
        carry_ref[...] = h0_ref[0]

    u = u_ref[0]
    ub = u.astype(BF16)
    y_in, g_re, g_im = [], [], []
    for j in range(S5_TILES):
        xj = jnp.concatenate([ub[:, s * SSM_WIDTH + j * LANES: s * SSM_WIDTH + (j + 1) * LANES]
                              for s in range(S5_T)], axis=1)
        res = _dot(xj, wcat_ref[j])
        y_in.append(res[:, :tw])
        g_re.append(res[:, tw:tw + ts])
        g_im.append(res[:, tw + ts:])
    h_re = jnp.concatenate(g_re, axis=1)
    h_im = jnp.concatenate(g_im, axis=1)
    row = lax.broadcasted_iota(jnp.int32, (rows, 1), 0)
    for m in range(nsteps):
        k = (1 << m) * rs
        a_r, a_i = pow2_ref[m:m + 1, :ns], pow2_ref[m:m + 1, ns:]
        s_re = jnp.where(row >= k, pltpu.roll(h_re, k, 0), 0.0)
        s_im = jnp.where(row >= k, pltpu.roll(h_im, k, 0), 0.0)
        h_re, h_im = h_re + a_r * s_re - a_i * s_im, h_im + a_r * s_im + a_i * s_re
    cin = carry_ref[...]
    if rs == 1:
        c_re, c_im = cin[:, :ns], cin[:, ns:]
    else:
        c_re = jnp.concatenate([cin[:, :ns]] * (rows // rs), axis=0)
        c_im = jnp.concatenate([cin[:, ns:]] * (rows // rs), axis=0)
    p_r, p_i = powr_ref[:, :ns], powr_ref[:, ns:]
    h_re, h_im = h_re + p_r * c_re - p_i * c_im, h_im + p_r * c_im + p_i * c_re
    last = jnp.concatenate([h_re[rows - rs:], h_im[rows - rs:]], axis=1)
    carry_ref[...] = last
    hout_ref[0] = last
    prev_re = jnp.where(row >= rs, pltpu.roll(h_re, rs, 0), c_re)
    prev_im = jnp.where(row >= rs, pltpu.roll(h_im, rs, 0), c_im)
    y_tiles = []
    for j in range(S5_TILES):
        hb = jnp.concatenate([prev_re[:, j * ts:(j + 1) * ts], prev_im[:, j * ts:(j + 1) * ts]],
                             axis=1).astype(BF16)
        y_tiles.append(y_in[j] + _dot(hb, e_ref[j]))
    d = d_ref[...]
    wglu = wglu_ref[0]
    for t in range(S5_T):
        cols = slice(t * SSM_WIDTH, (t + 1) * SSM_WIDTH)
        yt = jnp.concatenate([y_tiles[j][:, t * LANES:(t + 1) * LANES] for j in range(S5_TILES)], axis=1)
        g = jax.nn.gelu(yt + d * u[:, cols])
        y_ref[0, :, cols] = (g * jax.nn.sigmoid(_dot(g.astype(BF16), wglu))).astype(BF16)


def _s5_mixer(u_rows, h0, wcat, e, pow2, powr, d, w_glu, layer_e, rows, rs, nsteps):
    nb, nrows, width = u_rows.shape
    blk = lambda w: pl.BlockSpec((1, rows, w), lambda b, i: (b, i, 0))
    state = pl.BlockSpec((1, rs, 2 * S5_STATES), lambda b, i: (b, 0, 0))
    return pl.pallas_call(
        functools.partial(_s5_body, rows=rows, rs=rs, nsteps=nsteps),
        grid=(nb, nrows // rows),
        in_specs=[blk(width), state, _const_spec(wcat.shape), _const_spec(e.shape),
                  _const_spec(pow2.shape), _const_spec(powr.shape), _const_spec((1, SSM_WIDTH)),
                  _const_spec((1, SSM_WIDTH, SSM_WIDTH), (layer_e, 0, 0))],
        out_specs=[blk(width), state],
        out_shape=[jax.ShapeDtypeStruct((nb, nrows, width), BF16),
                   jax.ShapeDtypeStruct((nb, rs, 2 * S5_STATES), F32)],
        scratch_shapes=[pltpu.VMEM((rs, 2 * S5_STATES), F32)],
        compiler_params=_params(2),
        name="s5_mixer",
    )(u_rows, h0, wcat, e, pow2, powr, d, w_glu)


def _swa_body(sink_ref, q_ref, kc_ref, vc_ref, kp_ref, vp_ref, pk_ref, pv_ref, o_ref, *, tq, qg, start):
    first = pl.program_id(1) == 0
    k_prev = jnp.where(first, pk_ref[0], kp_ref[0])
    v_prev = jnp.where(first, pv_ref[0], vp_ref[0])
    k_all = jnp.concatenate([k_prev, kc_ref[0]], axis=0)
    v_all = jnp.concatenate([v_prev, vc_ref[0]], axis=0)
    nk = WINDOW + qg
    lane = lax.broadcasted_iota(jnp.int32, (1, KV_WIDTH), 1)
    slot = lax.broadcasted_iota(jnp.int32, (1, nk), 1)
    qrow = lax.broadcasted_iota(jnp.int32, (2 * qg, 1), 0)
    shift = CHUNK.bit_length() - 1
    q_chunk = (qrow & (qg - 1)) >> shift
    k_chunk = (slot >> shift) - WINDOW // CHUNK
    visible = (k_chunk <= q_chunk) & (k_chunk >= q_chunk - WINDOW // CHUNK)
    cached_ok = slot >= jnp.where(first, WINDOW - start, 0)
    scale = HEAD_DIM ** -0.5
    placed = []
    for kh in range(N_KV_HEADS):
        own = (lane >> (HEAD_DIM.bit_length() - 1)) == kh
        kz, vz = jnp.where(own, k_all, 0.0), jnp.where(own, v_all, 0.0)
        kr, vr = pltpu.roll(kz, HEAD_DIM, 1), pltpu.roll(vz, HEAD_DIM, 1)
        k_lo, k_hi = (kz, kr) if kh == 0 else (kr, kz)
        v_lo, v_hi = (vz, vr) if kh == 0 else (vr, vz)
        placed.append(((k_lo.astype(BF16), v_lo.astype(BF16)), (k_hi.astype(BF16), v_hi.astype(BF16))))
    for gi in range(tq // qg):
        mask = visible & cached_ok if gi == 0 else visible
        key_rows = slice(gi * qg, gi * qg + nk)
        qgrp = q_ref[0, gi * qg:(gi + 1) * qg, :]
        outs = []
        for kh in range(N_KV_HEADS):
            base = kh * GQ * HEAD_DIM
            qs = jnp.concatenate([qgrp[:, base:base + LANES], qgrp[:, base + LANES:base + 2 * LANES]], axis=0)
            acc = None
            for par, (k_pl, v_pl) in enumerate(placed[kh]):
                kk, vv = k_pl[key_rows], v_pl[key_rows]
                s = jnp.where(mask, _dot_t(qs, kk) * scale, NEG)
                sink = jnp.where(qrow < qg, sink_ref[kh * GQ + par], sink_ref[kh * GQ + 2 + par])
                m = jnp.maximum(jnp.max(s, axis=1, keepdims=True), sink)
                p = jnp.exp(s - m)
                den = jnp.sum(p, axis=1, keepdims=True) + jnp.exp(sink - m)
                o = _dot(p.astype(BF16), vv) / den
                acc = o if acc is None else acc + o
            outs += [acc[:qg], acc[qg:]]
        o_ref[0, gi * qg:(gi + 1) * qg, :] = jnp.concatenate(outs, axis=1).astype(BF16)


def _swa(q, k, v, past_k, past_v, sinks, start, tq, qg):
    bsz, seq, _ = q.shape
    ntiles = seq // tq
    tok = lambda w: pl.BlockSpec((1, tq, w), lambda b, i: (b, i, 0))
    past = pl.BlockSpec((1, WINDOW, KV_WIDTH), lambda b, i: (b, 0, 0))
    if ntiles > 1:
        per = tq // WINDOW
        prev = pl.BlockSpec((1, WINDOW, KV_WIDTH), lambda b, i: (b, jnp.maximum(i * per - 1, 0), 0))
        k_prev, v_prev = k, v
    else:
        prev, k_prev, v_prev = past, past_k, past_v
    return pl.pallas_call(
        functools.partial(_swa_body, tq=tq, qg=qg, start=start),
        grid=(bsz, ntiles),
        in_specs=[pl.BlockSpec(memory_space=pltpu.SMEM), tok(N_HEADS * HEAD_DIM), tok(KV_WIDTH), tok(KV_WIDTH),
                  prev, prev, past, past],
        out_specs=tok(N_HEADS * HEAD_DIM),
        out_shape=jax.ShapeDtypeStruct((bsz, seq, N_HEADS * HEAD_DIM), BF16),
        compiler_params=_params(2),
        name="swa",
    )(sinks, q, k, v, k_prev, v_prev, past_k, past_v)


def _xattn(x1, g, wq, mem_k, mem_v, wo):
    h = _rms(x1, g).astype(BF16)
    q = _dot(h, wq).astype(BF16)
    scale = X_HEAD_DIM ** -0.5
    outs = []
    for hd in range(X_HEADS):
        cols = slice(hd * X_HEAD_DIM, (hd + 1) * X_HEAD_DIM)
        s = _dot_t(q[:, cols], mem_k[:, cols]) * scale
        p = jnp.exp(s - jnp.max(s, axis=1, keepdims=True))
        den = jnp.sum(p, axis=1, keepdims=True)
        outs.append(_dot(p.astype(BF16), mem_v[:, cols]) / den)
    o = jnp.concatenate(outs, axis=1).astype(BF16)
    return x1 + _dot(o, wo)


def _conv3(cur, carry, w):
    n = cur.shape[0]
    row = lax.broadcasted_iota(jnp.int32, (n, 1), 0)
    c1, c2 = carry[SUBLANES - 1:SUBLANES], carry[SUBLANES - 2:SUBLANES - 1]
    m1 = jnp.where(row == 0, c1, pltpu.roll(cur, 1, 0))
    m2 = jnp.where(row == 0, c2, jnp.where(row == 1, c1, pltpu.roll(cur, 2, 0)))
    return w[0:1] * m2 + w[1:2] * m1 + w[2:3] * cur


def _mix_even_xattn_body(x_ref, ya_ref, yb_ref, woa_ref, wob_ref, g_ref, wq_ref, mk_ref, mv_ref, wo_ref, o_ref):
    x1 = x_ref[0] + _dot(ya_ref[0], woa_ref[0]) + _dot(yb_ref[0], wob_ref[0])
    o_ref[0] = _xattn(x1, g_ref[...], wq_ref[0], mk_ref[0, 0].astype(BF16), mv_ref[0, 0].astype(BF16), wo_ref[0])


def _mix_even_xattn(x, ya, yb, w_out, g, wq, mem_k, mem_v, wo, layer, layer_e, tm):
    bsz, seq, _ = x.shape
    tok = lambda w: pl.BlockSpec((1, tm, w), lambda b, i: (b, i, 0))
    mem = pl.BlockSpec((1, 1, N_MEM, D_MODEL), lambda b, i: (layer, b, 0, 0))
    half = D_MODEL // 2
    sq = lambda: _const_spec((1, D_MODEL, D_MODEL), (layer, 0, 0))
    return pl.pallas_call(
        _mix_even_xattn_body,
        grid=(bsz, seq // tm),
        in_specs=[tok(D_MODEL), tok(half), tok(half),
                  _const_spec((1, half, D_MODEL), (layer_e, 0, 0)), _const_spec((1, half, D_MODEL), (layer_e, 1, 0)),
                  _const_spec((1, D_MODEL)), sq(), mem, mem, sq()],
        out_specs=tok(D_MODEL),
        out_shape=jax.ShapeDtypeStruct(x.shape, F32),
        compiler_params=_params(2),
        name="mix_even_xattn",
    )(x, ya, yb, w_out, w_out, g, wq, mem_k, mem_v, wo)


def _mix_odd_xattn_body(x_ref, gm_ref, win_ref, cw_ref, wout_ref, prev_ref, g_ref, wq_ref, mk_ref, mv_ref, wo_ref,
                        o_ref, st_ref, carry_ref, *, tm):
    @pl.when(pl.program_id(1) == 0)
    def _():
        carry_ref[SUBLANES - 2:SUBLANES, :] = prev_ref[0]

    x = x_ref[0]
    proj = _dot(_rms(x, gm_ref[...]).astype(BF16), win_ref[0])
    gate_b, gate_c, z = proj[:, :D_MODEL], proj[:, D_MODEL:2 * D_MODEL], proj[:, 2 * D_MODEL:]
    cz = gate_c * z
    z_conv = _conv3(cz, carry_ref[...], cw_ref[0])
    carry_ref[...] = cz[tm - SUBLANES:]
    st_ref[0] = cz[tm - (CONV_WIDTH - 1):]
    x1 = x + _dot((gate_b * z_conv).astype(BF16), wout_ref[0])
    o_ref[0] = _xattn(x1, g_ref[...], wq_ref[0], mk_ref[0, 0].astype(BF16), mv_ref[0, 0].astype(BF16), wo_ref[0])


def _mix_odd_xattn(x, gm, w_in, conv_w, w_out, prev, g, wq, mem_k, mem_v, wo, layer, layer_o, tm):
    bsz, seq, _ = x.shape
    tok = pl.BlockSpec((1, tm, D_MODEL), lambda b, i: (b, i, 0))
    mem = pl.BlockSpec((1, 1, N_MEM, D_MODEL), lambda b, i: (layer, b, 0, 0))
    st = pl.BlockSpec((1, CONV_WIDTH - 1, D_MODEL), lambda b, i: (b, 0, 0))
    sq = lambda l: _const_spec((1, D_MODEL, D_MODEL), (l, 0, 0))
    return pl.pallas_call(
        functools.partial(_mix_odd_xattn_body, tm=tm),
        grid=(bsz, seq // tm),
        in_specs=[tok, _const_spec((1, D_MODEL)), _const_spec((1, D_MODEL, 3 * D_MODEL), (layer_o, 0, 0)),
                  _const_spec((1, CONV_WIDTH, D_MODEL), (layer_o, 0, 0)), sq(layer_o), st,
                  _const_spec((1, D_MODEL)), sq(layer), mem, mem, sq(layer)],
        out_specs=[tok, st],
        out_shape=[jax.ShapeDtypeStruct(x.shape, F32),
                   jax.ShapeDtypeStruct((bsz, CONV_WIDTH - 1, D_MODEL), F32)],
        scratch_shapes=[pltpu.VMEM((SUBLANES, D_MODEL), F32)],
        compiler_params=_params(2),
        name="mix_odd_xattn",
    )(x, gm, w_in, conv_w, w_out, prev, g, wq, mem_k, mem_v, wo)


def _ffn_body(x_ref, g_ref, wup_ref, cw_ref, wdn_ref, prev_ref, gfin_ref, o_ref, st_ref, carry_ref, *, tm, final):
    @pl.when(pl.program_id(1) == 0)
    def _():
        carry_ref[SUBLANES - 2:SUBLANES, :] = prev_ref[0]

    x = x_ref[0]
    h = _rms(x, g_ref[...]).astype(BF16)
    acc = x
    for c in range(D_FF // FF_CHUNK):
        cols = slice(c * FF_CHUNK, (c + 1) * FF_CHUNK)
        gate = _dot(h, wup_ref[0, :, cols])
        val = _dot(h, wup_ref[0, :, D_FF + c * FF_CHUNK: D_FF + (c + 1) * FF_CHUNK])
        gate_c = _conv3(gate, carry_ref[:, cols], cw_ref[0, :, cols])
        carry_ref[:, cols] = gate[tm - SUBLANES:]
        st_ref[0, :, cols] = gate[tm - (CONV_WIDTH - 1):]
        act = gate_c * jax.nn.sigmoid(gate_c) * val
        acc = acc + _dot(act.astype(BF16), wdn_ref[0, cols, :])
    o_ref[0] = _rms(acc, gfin_ref[...]) if final else acc


def _ffn(x, g, w_up, conv_w, w_down, prev, g_final, layer, tm, final):
    bsz, seq, _ = x.shape
    tok = pl.BlockSpec((1, tm, D_MODEL), lambda b, i: (b, i, 0))
    st = pl.BlockSpec((1, CONV_WIDTH - 1, D_FF), lambda b, i: (b, 0, 0))
    return pl.pallas_call(
        functools.partial(_ffn_body, tm=tm, final=final),
        grid=(bsz, seq // tm),
        in_specs=[tok, _const_spec((1, D_MODEL)), _const_spec((1, D_MODEL, 2 * D_FF), (layer, 0, 0)),
                  _const_spec((1, CONV_WIDTH, D_FF), (layer, 0, 0)), _const_spec((1, D_FF, D_MODEL), (layer, 0, 0)),
                  st, _const_spec((1, D_MODEL))],
        out_specs=[tok, st],
        out_shape=[jax.ShapeDtypeStruct(x.shape, F32),
                   jax.ShapeDtypeStruct((bsz, CONV_WIDTH - 1, D_FF), F32)],
        scratch_shapes=[pltpu.VMEM((SUBLANES, D_FF), F32)],
        compiler_params=_params(2),
        name="conv_ffn",
    )(x, g, w_up, conv_w, w_down, prev, g_final)


def _mem_kv_body(m_ref, w_ref, k_ref, v_ref):
    kv = _dot(m_ref[0].astype(BF16), w_ref[0])
    k_ref[0, 0] = kv[:, :D_MODEL]
    v_ref[0, 0] = kv[:, D_MODEL:]


def _mem_kv(mem, w_kv):
    bsz = mem.shape[0]
    out = pl.BlockSpec((1, 1, N_MEM, D_MODEL), lambda l, b: (l, b, 0, 0))
    return pl.pallas_call(
        _mem_kv_body,
        grid=(DEPTH, bsz),
        in_specs=[pl.BlockSpec((1, N_MEM, D_MODEL), lambda l, b: (b, 0, 0)),
                  pl.BlockSpec((1, D_MODEL, 2 * D_MODEL), lambda l, b: (l, 0, 0))],
        out_specs=[out, out],
        out_shape=[jax.ShapeDtypeStruct((DEPTH, bsz, N_MEM, D_MODEL), F32)] * 2,
        compiler_params=_params(2),
        name="mem_kv",
    )(mem, w_kv)


def _trunk(x, start, past_k, past_v, ssm_re, ssm_im, conv_mix_prev, conv_ffn_prev, mem_k, mem_v, w, s5, cfg):
    bsz, seq, _ = x.shape
    tm, tq, qg = cfg["tm"], cfg["tq"], cfg["qg"]
    rows, rs, nsteps = cfg["rows"], cfg["rs"], cfg["nsteps"]
    nchunk = seq // S5_T
    tables = _rope_tables(seq, start, tm)
    u, q, k, v = _inproj_even(x, w["norm_mix"][0:1], w["w_in_even"], tables, 0, tm)
    h0 = jnp.concatenate([ssm_re[0].reshape(bsz, S5_STATES), ssm_im[0].reshape(bsz, S5_STATES)], axis=1)
    u_rows = u.reshape(bsz, nchunk, S5_T * SSM_WIDTH)
    if rs == 1:
        h0_rows = h0[:, None, :]
    else:
        u_rows = u_rows.transpose(1, 0, 2).reshape(1, nchunk * bsz, S5_T * SSM_WIDTH)
        h0_rows = h0[None]
    ya, h_last = _s5_mixer(u_rows, h0_rows, s5["wcat"], s5["e"], cfg["pow2"], cfg["powr"], w["ssm_d"],
                           w["w_glu"], 0, rows, rs, nsteps)
    if rs == 1:
        h_last = h_last[:, 0]
    else:
        ya = ya.reshape(nchunk, bsz, S5_T * SSM_WIDTH).transpose(1, 0, 2)
        h_last = h_last[0]
    ya = ya.reshape(bsz, seq, SSM_WIDTH)
    new_re = h_last[:, :S5_STATES].reshape(1, bsz, SSM_GROUPS, SSM_STATE)
    new_im = h_last[:, S5_STATES:].reshape(1, bsz, SSM_GROUPS, SSM_STATE)
    yb = _swa(q, k, v, past_k[0].reshape(bsz, WINDOW, KV_WIDTH), past_v[0].reshape(bsz, WINDOW, KV_WIDTH),
              w["attn_sinks"][0], start, tq, qg)
    x = _mix_even_xattn(x, ya, yb, w["w_out_even"], w["norm_xattn"][0:1], w["xattn_wq"], mem_k, mem_v,
                        w["xattn_wo"], 0, 0, tm)
    x, cf0 = _ffn(x, w["norm_ffn"][0:1], w["ffn_w_up"], w["ffn_conv_w"], w["ffn_w_down"], conv_ffn_prev[0],
                  w["norm_final"], 0, tm, False)
    x, cm = _mix_odd_xattn(x, w["norm_mix"][1:2], w["w_in_odd"], w["conv_mix_w"], w["w_out_odd"], conv_mix_prev[0],
                           w["norm_xattn"][1:2], w["xattn_wq"], mem_k, mem_v, w["xattn_wo"], 1, 0, tm)
    y, cf1 = _ffn(x, w["norm_ffn"][1:2], w["ffn_w_up"], w["ffn_conv_w"], w["ffn_w_down"], conv_ffn_prev[1],
                  w["norm_final"], 1, tm, True)
    new_k = k.reshape(1, bsz, seq, N_KV_HEADS, HEAD_DIM)
    new_v = v.reshape(1, bsz, seq, N_KV_HEADS, HEAD_DIM)
    return y, new_k, new_v, new_re, new_im, cm[None], jnp.stack([cf0, cf1])


def kernel(x_prompt, x_sample, mem_prompt, cache_win_k, cache_win_v, state_ssm_re, state_ssm_im, state_conv_mix, state_conv_ffn, cache_mem_k, cache_mem_v, norm_mix, norm_xattn, norm_ffn, norm_final, w_in_even, w_out_even, ssm_a_re, ssm_a_im, ssm_log_dt, ssm_b_re, ssm_b_im, ssm_c_re, ssm_c_im, ssm_d, w_glu, attn_sinks, w_in_odd, conv_mix_w, w_out_odd, xattn_wq, xattn_wkv, xattn_wo, ffn_w_up, ffn_conv_w, ffn_w_down):
    bp, seq_p, _ = x_prompt.shape
    bs, seq_s, _ = x_sample.shape
    w = dict(norm_mix=norm_mix, norm_xattn=norm_xattn, norm_ffn=norm_ffn, norm_final=norm_final.reshape(1, D_MODEL),
             w_in_even=w_in_even.astype(BF16), w_out_even=w_out_even.astype(BF16),
             ssm_d=ssm_d[0].reshape(1, SSM_WIDTH), w_glu=w_glu.astype(BF16), attn_sinks=attn_sinks,
             w_in_odd=w_in_odd.astype(BF16), conv_mix_w=conv_mix_w, w_out_odd=w_out_odd.astype(BF16),
             xattn_wq=xattn_wq.astype(BF16), xattn_wo=xattn_wo.astype(BF16),
             ffn_w_up=ffn_w_up.astype(BF16), ffn_conv_w=ffn_conv_w, ffn_w_down=ffn_w_down.astype(BF16))

    wcat, e = _s5_prep(ssm_a_re[0], ssm_a_im[0], ssm_log_dt[0], ssm_b_re[0], ssm_b_im[0], ssm_c_re[0], ssm_c_im[0])
    s5 = dict(wcat=wcat, e=e)
    rows_p, steps_p = 64, 6
    rows_s, steps_s = (seq_s // S5_T) * bs, int(math.log2(seq_s // S5_T))
    pad = lambda n: [1.0] * (SUBLANES - n)
    exps = ([float(S5_T << m) for m in range(steps_p)] + pad(steps_p)
            + [float(S5_T * (r + 1)) for r in range(rows_p)]
            + [float(S5_T << m) for m in range(steps_s)] + pad(steps_s)
            + [float(S5_T * (r // bs + 1)) for r in range(rows_s)])
    pows = _s5_pow_table(exps, ssm_a_re[0], ssm_a_im[0], ssm_log_dt[0])
    o1 = SUBLANES
    o2 = o1 + rows_p
    o3 = o2 + SUBLANES
    cfg_p = dict(tm=512, tq=256, qg=128, rows=rows_p, rs=1, nsteps=steps_p, pow2=pows[:o1], powr=pows[o1:o2])
    cfg_s = dict(tm=seq_s, tq=seq_s, qg=seq_s, rows=rows_s, rs=bs, nsteps=steps_s, pow2=pows[o2:o3], powr=pows[o3:])

    mem_k_p, mem_v_p = _mem_kv(mem_prompt, xattn_wkv.astype(BF16))
    cw = CONV_WIDTH - 1
    zk = jnp.zeros((1, bp, WINDOW, N_KV_HEADS, HEAD_DIM), F32)
    zs = jnp.zeros((1, bp, SSM_GROUPS, SSM_STATE), F32)
    zcm = jnp.zeros((1, bp, cw, D_MODEL), F32)
    zcf = jnp.zeros((DEPTH, bp, cw, D_FF), F32)
    y_p, k_p, v_p, re_p, im_p, cm_p, cf_p = _trunk(x_prompt, 0, zk, zk, zs, zs, zcm, zcf, mem_k_p, mem_v_p, w, s5, cfg_p)
    mem_k_s = cache_mem_k.reshape(DEPTH, bs, N_MEM, D_MODEL)
    mem_v_s = cache_mem_v.reshape(DEPTH, bs, N_MEM, D_MODEL)
    y_s, k_s, v_s, re_s, im_s, cm_s, cf_s = _trunk(x_sample, PAST_LEN, cache_win_k, cache_win_v, state_ssm_re,
                                                   state_ssm_im, state_conv_mix, state_conv_ffn, mem_k_s, mem_v_s,
                                                   w, s5, cfg_s)
    shape_mem = (DEPTH, bp, N_MEM, X_HEADS, X_HEAD_DIM)
    return (y_p, y_s, k_p[:, :, -WINDOW:], v_p[:, :, -WINDOW:], re_p, im_p, cm_p, cf_p,
            mem_k_p.reshape(shape_mem), mem_v_p.reshape(shape_mem), k_s, v_s, re_s, im_s, cm_s, cf_s)
```

```python
import functools
import math

import numpy as np
import jax
import jax.numpy as jnp
from jax import lax
from jax.experimental import pallas as pl
from jax.experimental.pallas import tpu as pltpu

F32 = jnp.float32
BF16 = jnp.bfloat16

D_MODEL = 1024
DEPTH = 2
PAST_LEN = 2048
CHUNK = 64
SSM_WIDTH = 512
SSM_GROUP = 16
SSM_GROUPS = 32
SSM_STATE = 64
HEAD_DIM = 64
N_HEADS = 8
N_KV_HEADS = 2
GQ = N_HEADS // N_KV_HEADS
WINDOW = 128
ROPE_THETA = 10000.0
KV_WIDTH = N_KV_HEADS * HEAD_DIM
EVEN_IN = SSM_WIDTH + N_HEADS * HEAD_DIM + 2 * KV_WIDTH
CONV_WIDTH = 3
N_MEM = 256
X_HEADS = 4
X_HEAD_DIM = 256
D_FF = 2816
EPS = 1e-6
NEG = -1e30

LANES = 128
SUBLANES = 8
S5_T = 8
S5_TILES = SSM_WIDTH // LANES
S5_TILE_STATES = (LANES // SSM_GROUP) * SSM_STATE
S5_STATES = SSM_GROUPS * SSM_STATE
S5_ROWS = 128
FF_CHUNK = 1408
VMEM_LIMIT = 56 * 1024 * 1024


def _const_spec(shape, index=None):
    idx = tuple(index) if index is not None else (0,) * len(shape)
    return pl.BlockSpec(shape, lambda *_: idx, pipeline_mode=pl.Buffered(1))


def _params(n_axes):
    return pltpu.CompilerParams(dimension_semantics=("arbitrary",) * n_axes,
                                vmem_limit_bytes=VMEM_LIMIT)


def _rms(x, g):
    return x * lax.rsqrt(jnp.mean(x * x, axis=-1, keepdims=True) + EPS) * g


def _dot(a, b):
    return jnp.dot(a, b, preferred_element_type=F32)


def _dot_t(a, b):
    return lax.dot_general(a, b, (((1,), (1,)), ((), ())), preferred_element_type=F32)


def _rope_table_body(cos_ref, sina_ref, sinb_ref, *, start, tl):
    i = pl.program_id(0)
    row = lax.broadcasted_iota(jnp.int32, (tl, LANES), 0)
    lane = lax.broadcasted_iota(jnp.int32, (tl, LANES), 1)
    pos = (start + i * tl + row).astype(F32)
    half = HEAD_DIM // 2
    freq = (lane[0:1] & (half - 1)).astype(F32)
    inv = jnp.exp(freq * (-2.0 / HEAD_DIM * math.log(ROPE_THETA)))
    ang = pos * inv
    c = jnp.cos(ang)
    s = jnp.sin(ang)
    first = (lane & (HEAD_DIM - 1)) < HEAD_DIM // 2
    cos_ref[...] = c
    sina_ref[...] = jnp.where(first, -s, 0.0)
    sinb_ref[...] = jnp.where(first, 0.0, s)


def _rope_tables(seq, start, tl):
    spec = pl.BlockSpec((tl, LANES), lambda i: (i, 0))
    return pl.pallas_call(
        functools.partial(_rope_table_body, start=start, tl=tl),
        grid=(seq // tl,),
        in_specs=[],
        out_specs=[spec, spec, spec],
        out_shape=[jax.ShapeDtypeStruct((seq, LANES), F32)] * 3,
        compiler_params=_params(1),
        name="rope_tables",
    )()


def _rope(t, cos, sina, sinb):
    n = t.shape[1]
    half = HEAD_DIM // 2
    return t * cos + pltpu.roll(t, n - half, 1) * sina + pltpu.roll(t, half, 1) * sinb


def _inproj_even_body(x_ref, g_ref, w_ref, cos_ref, sa_ref, sb_ref, u_ref, q_ref, k_ref, v_ref):
    h = _rms(x_ref[0], g_ref[...]).astype(BF16)
    proj = _dot(h, w_ref[0])
    o0 = SSM_WIDTH
    o1 = o0 + N_HEADS * HEAD_DIM
    o2 = o1 + KV_WIDTH
    cos, sa, sb = cos_ref[...], sa_ref[...], sb_ref[...]
    rep = (o1 - o0) // LANES
    cos_q = jnp.concatenate([cos] * rep, axis=1)
    sa_q = jnp.concatenate([sa] * rep, axis=1)
    sb_q = jnp.concatenate([sb] * rep, axis=1)
    u_ref[0] = proj[:, :o0]
    q_ref[0] = _rope(proj[:, o0:o1], cos_q, sa_q, sb_q).astype(BF16)
    k_ref[0] = _rope(proj[:, o1:o2], cos, sa, sb)
    v_ref[0] = proj[:, o2:]


def _inproj_even(x, g, w_in, tables, layer_e, tm):
    bsz, seq, _ = x.shape
    cos, sa, sb = tables
    tok = lambda w: pl.BlockSpec((1, tm, w), lambda b, i: (b, i, 0))
    tab = pl.BlockSpec((tm, LANES), lambda b, i: (i, 0))
    return pl.pallas_call(
        _inproj_even_body,
        grid=(bsz, seq // tm),
        in_specs=[tok(D_MODEL), _const_spec((1, D_MODEL)),
                  _const_spec((1, D_MODEL, EVEN_IN), (layer_e, 0, 0)), tab, tab, tab],
        out_specs=[tok(SSM_WIDTH), tok(N_HEADS * HEAD_DIM), tok(KV_WIDTH), tok(KV_WIDTH)],
        out_shape=[jax.ShapeDtypeStruct((bsz, seq, SSM_WIDTH), F32),
                   jax.ShapeDtypeStruct((bsz, seq, N_HEADS * HEAD_DIM), BF16),
                   jax.ShapeDtypeStruct((bsz, seq, KV_WIDTH), F32),
                   jax.ShapeDtypeStruct((bsz, seq, KV_WIDTH), F32)],
        compiler_params=_params(2),
        name="inproj_even",
    )(x, g, w_in, cos, sa, sb)


def _s5_prep_body(lr_ref, li_ref, ld_ref, bre_ref, bim_ref, lrc_ref, lic_ref, ldc_ref,
                  cre_ref, cim_ref, wg_ref, wy_ref):
    ts = S5_TILE_STATES
    tw = S5_T * LANES

    def step(z_r, z_i):
        mag = jnp.exp(z_r)
        a_r, a_i = mag * jnp.cos(z_i), mag * jnp.sin(z_i)
        return a_r, a_i, lambda p: (p[0] * a_r - p[1] * a_i, p[0] * a_i + p[1] * a_r)

    lr, li = lr_ref[0], li_ref[0]
    dt = jnp.exp(ld_ref[0])
    ar, ai, mul_a = step(lr * dt, li * dt)
    nrm = lr * lr + li * li
    cbr = ((ar - 1.0) * lr + ai * li) / nrm
    cbi = (ai * lr - (ar - 1.0) * li) / nrm
    b_re, b_im = bre_ref[0], bim_ref[0]
    bbr = cbr * b_re - cbi * b_im
    bbi = cbr * b_im + cbi * b_re
    p = (jnp.ones_like(ar), jnp.zeros_like(ar))
    for s in reversed(range(S5_T)):
        rows = slice(s * LANES, (s + 1) * LANES)
        wg_ref[0, rows, :ts] = (p[0] * bbr - p[1] * bbi).astype(BF16)
        wg_ref[0, rows, ts:] = (p[0] * bbi + p[1] * bbr).astype(BF16)
        for t in range(s):
            wy_ref[0, rows, t * LANES:(t + 1) * LANES] = jnp.zeros((LANES, LANES), BF16)
        p = mul_a(p)
    dtc = jnp.exp(ldc_ref[0])
    _, _, mul_ac = step(lrc_ref[0] * dtc, lic_ref[0] * dtc)
    c_re, c_im = cre_ref[0], cim_ref[0]
    bb = jnp.concatenate([bbr, bbi], axis=1).astype(BF16)
    pc = (jnp.ones_like(dtc), jnp.zeros_like(dtc))
    for k in range(S5_T + 1):
        blk = jnp.concatenate([pc[0] * c_re - pc[1] * c_im, -(pc[0] * c_im + pc[1] * c_re)], axis=0).astype(BF16)
        if k >= 1:
            wy_ref[0, tw:, (k - 1) * LANES:k * LANES] = blk
        if k < S5_T:
            lag = _dot(bb, blk).astype(BF16)
            for s in range(S5_T - k):
                t = s + k
                wy_ref[0, s * LANES:(s + 1) * LANES, t * LANES:(t + 1) * LANES] = lag
        pc = mul_ac(pc)


def _s5_prep(a_re, a_im, log_dt, b_re, b_im, c_re, c_im):
    gpt = LANES // SSM_GROUP
    eye = jnp.eye(gpt, dtype=bool)

    def rows(p):
        return p.reshape(S5_TILES, 1, S5_TILE_STATES)

    def cols(p):
        return jnp.broadcast_to(p.reshape(S5_TILES, S5_TILE_STATES, 1), (S5_TILES, S5_TILE_STATES, LANES))

    def bmat(b):
        bt = b.reshape(S5_TILES, gpt, SSM_STATE, SSM_GROUP).transpose(0, 1, 3, 2)
        full = jnp.where(eye[None, :, None, :, None], bt[:, :, :, None, :], 0.0)
        return full.reshape(S5_TILES, LANES, S5_TILE_STATES)

    def cmat(c):
        ct = c.reshape(S5_TILES, gpt, SSM_GROUP, SSM_STATE).transpose(0, 1, 3, 2)
        full = jnp.where(eye[None, :, None, :, None], ct[:, :, :, None, :], 0.0)
        return full.reshape(S5_TILES, S5_TILE_STATES, LANES)

    row_spec = pl.BlockSpec((1, 1, S5_TILE_STATES), lambda j: (j, 0, 0))
    b_spec = pl.BlockSpec((1, LANES, S5_TILE_STATES), lambda j: (j, 0, 0))
    c_spec = pl.BlockSpec((1, S5_TILE_STATES, LANES), lambda j: (j, 0, 0))
    tw = S5_T * LANES
    return pl.pallas_call(
        _s5_prep_body,
        grid=(S5_TILES,),
        in_specs=[row_spec] * 3 + [b_spec] * 2 + [c_spec] * 5,
        out_specs=[pl.BlockSpec((1, tw, 2 * S5_TILE_STATES), lambda j: (j, 0, 0)),
                   pl.BlockSpec((1, tw + 2 * S5_TILE_STATES, tw), lambda j: (j, 0, 0))],
        out_shape=[jax.ShapeDtypeStruct((S5_TILES, tw, 2 * S5_TILE_STATES), BF16),
                   jax.ShapeDtypeStruct((S5_TILES, tw + 2 * S5_TILE_STATES, tw), BF16)],
        compiler_params=_params(1),
        name="s5_prep",
    )(rows(a_re), rows(a_im), rows(log_dt), bmat(b_re), bmat(b_im),
      cols(a_re), cols(a_im), cols(log_dt), cmat(c_re), cmat(c_im))


def _s5_pow_body(ex_ref, lr_ref, li_ref, ld_ref, o_ref):
    dt = jnp.exp(ld_ref[...])
    zr, zi = lr_ref[...] * dt, li_ref[...] * dt
    ex = ex_ref[...]
    mag = jnp.exp(ex * zr)
    o_ref[:, :S5_STATES] = mag * jnp.cos(ex * zi)
    o_ref[:, S5_STATES:] = mag * jnp.sin(ex * zi)


def _s5_pow_table(exponents, a_re, a_im, log_dt):
    n = len(exponents)
    ex = jnp.asarray(np.asarray(exponents, np.float32)[:, None])
    flat = lambda p: p.reshape(1, S5_STATES)
    return pl.pallas_call(
        _s5_pow_body,
        grid=(1,),
        in_specs=[pl.BlockSpec((n, 1), lambda i: (0, 0))] + [pl.BlockSpec((1, S5_STATES), lambda i: (0, 0))] * 3,
        out_specs=pl.BlockSpec((n, 2 * S5_STATES), lambda i: (0, 0)),
        out_shape=jax.ShapeDtypeStruct((n, 2 * S5_STATES), F32),
        compiler_params=_params(1),
        name="s5_pow_table",
    )(ex, flat(a_re), flat(a_im), flat(log_dt))


def _s5_body(u_ref, h0_ref, wg_ref, wy_ref, pow2_ref, powr_ref, d_ref, wglu_ref,
             y_ref, hout_ref, carry_ref, us_ref, ys_ref, *, rows, seg, nsteps, hrows):
    ns = S5_STATES
    ts = S5_TILE_STATES

    @pl.when(pl.program_id(1) == 0)
    def _():
        carry_ref[...] = h0_ref[0]

    for j in range(S5_TILES):
        us_ref[j] = u_ref[0, :, j * LANES:(j + 1) * LANES]

    def chunk_cols(j, t):
        return us_ref[j, pl.ds(t, rows, stride=S5_T), :]

    xs, g_re, g_im = [], [], []
    for j in range(S5_TILES):
        xj = jnp.concatenate([chunk_cols(j, s) for s in range(S5_T)], axis=1).astype(BF16)
        g = _dot(xj, wg_ref[j])
        xs.append(xj)
        g_re.append(g[:, :ts])
        g_im.append(g[:, ts:])
    h_re = jnp.concatenate(g_re, axis=1)
    h_im = jnp.concatenate(g_im, axis=1)
    pos = lax.broadcasted_iota(jnp.int32, (rows, 1), 0) & (seg - 1)
    for m in range(nsteps):
        k = 1 << m
        a_r, a_i = pow2_ref[m:m + 1, :ns], pow2_ref[m:m + 1, ns:]
        s_re = jnp.where(pos >= k, pltpu.roll(h_re, k, 0), 0.0)
        s_im = jnp.where(pos >= k, pltpu.roll(h_im, k, 0), 0.0)
        h_re, h_im = h_re + a_r * s_re - a_i * s_im, h_im + a_r * s_im + a_i * s_re
    cin = carry_ref[...]
    c_re, c_im = cin[:, :ns], cin[:, ns:]
    p_r, p_i = powr_ref[:, :ns], powr_ref[:, ns:]
    h_re, h_im = h_re + p_r * c_re - p_i * c_im, h_im + p_r * c_im + p_i * c_re
    last = jnp.concatenate([h_re[rows - hrows:], h_im[rows - hrows:]], axis=1)
    carry_ref[...] = last
    hout_ref[0] = last
    prev_re = jnp.where(pos >= 1, pltpu.roll(h_re, 1, 0), c_re)
    prev_im = jnp.where(pos >= 1, pltpu.roll(h_im, 1, 0), c_im)
    y_tiles = []
    for j in range(S5_TILES):
        lhs = jnp.concatenate([xs[j], prev_re[:, j * ts:(j + 1) * ts].astype(BF16),
                               prev_im[:, j * ts:(j + 1) * ts].astype(BF16)], axis=1)
        y_tiles.append(_dot(lhs, wy_ref[j]))
    d = d_ref[...]
    wglu = wglu_ref[0]
    for t in range(S5_T):
        ut = jnp.concatenate([chunk_cols(j, t) for j in range(S5_TILES)], axis=1)
        yt = jnp.concatenate([y_tiles[j][:, t * LANES:(t + 1) * LANES] for j in range(S5_TILES)], axis=1)
        g = jax.nn.gelu(yt + d * ut)
        o = g * jax.nn.sigmoid(_dot(g.astype(BF16), wglu))
        for j in range(S5_TILES):
            ys_ref[j, pl.ds(t, rows, stride=S5_T), :] = o[:, j * LANES:(j + 1) * LANES]
    for j in range(S5_TILES):
        y_ref[0, :, j * LANES:(j + 1) * LANES] = ys_ref[j].astype(BF16)


def _s5_mixer(u, h0, wg, wy, pow2, powr, d, w_glu, layer_e, rows, seg, nsteps):
    nb, ntok, width = u.shape
    tb = rows * S5_T
    hrows = h0.shape[1]
    blk = pl.BlockSpec((1, tb, width), lambda b, i: (b, i, 0))
    state = pl.BlockSpec((1, hrows, 2 * S5_STATES), lambda b, i: (b, 0, 0))
    return pl.pallas_call(
        functools.partial(_s5_body, rows=rows, seg=seg, nsteps=nsteps, hrows=hrows),
        grid=(nb, ntok // tb),
        in_specs=[blk, state, _const_spec(wg.shape), _const_spec(wy.shape),
                  _const_spec(pow2.shape), _const_spec(powr.shape), _const_spec((1, SSM_WIDTH)),
                  _const_spec((1, SSM_WIDTH, SSM_WIDTH), (layer_e, 0, 0))],
        out_specs=[blk, state],
        out_shape=[jax.ShapeDtypeStruct((nb, ntok, width), BF16),
                   jax.ShapeDtypeStruct((nb, hrows, 2 * S5_STATES), F32)],
        scratch_shapes=[pltpu.VMEM((hrows, 2 * S5_STATES), F32),
                        pltpu.VMEM((S5_TILES, tb, LANES), F32), pltpu.VMEM((S5_TILES, tb, LANES), F32)],
        compiler_params=_params(2),
        name="s5_mixer",
    )(u, h0, wg, wy, pow2, powr, d, w_glu)


def _swa_body(sink_ref, q_ref, kc_ref, vc_ref, kp_ref, vp_ref, pk_ref, pv_ref, o_ref, *, tq, qg, start):
    first = pl.program_id(1) == 0
    k_prev = jnp.where(first, pk_ref[0], kp_ref[0])
    v_prev = jnp.where(first, pv_ref[0], vp_ref[0])
    k_all = jnp.concatenate([k_prev, kc_ref[0]], axis=0)
    v_all = jnp.concatenate([v_prev, vc_ref[0]], axis=0)
    nk = WINDOW + qg
    lane = lax.broadcasted_iota(jnp.int32, (1, KV_WIDTH), 1)
    slot = lax.broadcasted_iota(jnp.int32, (1, nk), 1)
    qrow = lax.broadcasted_iota(jnp.int32, (2 * qg, 1), 0)
    shift = CHUNK.bit_length() - 1
    q_chunk = (qrow & (qg - 1)) >> shift
    k_chunk = (slot >> shift) - WINDOW // CHUNK
    visible = (k_chunk <= q_chunk) & (k_chunk >= q_chunk - WINDOW // CHUNK)
    cached_ok = slot >= jnp.where(first, WINDOW - start, 0)
    scale = HEAD_DIM ** -0.5
    placed = []
    for kh in range(N_KV_HEADS):
        own = (lane >> (HEAD_DIM.bit_length() - 1)) == kh
        kz, vz = jnp.where(own, k_all, 0.0), jnp.where(own, v_all, 0.0)
        kr, vr = pltpu.roll(kz, HEAD_DIM, 1), pltpu.roll(vz, HEAD_DIM, 1)
        k_lo, k_hi = (kz, kr) if kh == 0 else (kr, kz)
        v_lo, v_hi = (vz, vr) if kh == 0 else (vr, vz)
        placed.append(((k_lo.astype(BF16), v_lo.astype(BF16)), (k_hi.astype(BF16), v_hi.astype(BF16))))
    for gi in range(tq // qg):
        mask = visible & cached_ok if gi == 0 else visible
        key_rows = slice(gi * qg, gi * qg + nk)
        qgrp = q_ref[0, gi * qg:(gi + 1) * qg, :]
        outs = []
        for kh in range(N_KV_HEADS):
            base = kh * GQ * HEAD_DIM
            qs = jnp.concatenate([qgrp[:, base:base + LANES], qgrp[:, base + LANES:base + 2 * LANES]], axis=0)
            acc = None
            for par, (k_pl, v_pl) in enumerate(placed[kh]):
                kk, vv = k_pl[key_rows], v_pl[key_rows]
                s = jnp.where(mask, _dot_t(qs, kk) * scale, NEG)
                sink = jnp.where(qrow < qg, sink_ref[kh * GQ + par], sink_ref[kh * GQ + 2 + par])
                m = jnp.maximum(jnp.max(s, axis=1, keepdims=True), sink)
                p = jnp.exp(s - m)
                den = jnp.sum(p, axis=1, keepdims=True) + jnp.exp(sink - m)
                o = _dot(p.astype(BF16), vv) / den
                acc = o if acc is None else acc + o
            outs += [acc[:qg], acc[qg:]]
        o_ref[0, gi * qg:(gi + 1) * qg, :] = jnp.concatenate(outs, axis=1).astype(BF16)


def _swa(q, k, v, past_k, past_v, sinks, start, tq, qg):
    bsz, seq, _ = q.shape
    ntiles = seq // tq
    tok = lambda w: pl.BlockSpec((1, tq, w), lambda b, i: (b, i, 0))
    past = pl.BlockSpec((1, WINDOW, KV_WIDTH), lambda b, i: (b, 0, 0))
    if ntiles > 1:
        per = tq // WINDOW
        prev = pl.BlockSpec((1, WINDOW, KV_WIDTH), lambda b, i: (b, jnp.maximum(i * per - 1, 0), 0))
        k_prev, v_prev = k, v
    else:
        prev, k_prev, v_prev = past, past_k, past_v
    return pl.pallas_call(
        functools.partial(_swa_body, tq=tq, qg=qg, start=start),
        grid=(bsz, ntiles),
        in_specs=[pl.BlockSpec(memory_space=pltpu.SMEM), tok(N_HEADS * HEAD_DIM), tok(KV_WIDTH), tok(KV_WIDTH),
                  prev, prev, past, past],
        out_specs=tok(N_HEADS * HEAD_DIM),
        out_shape=jax.ShapeDtypeStruct((bsz, seq, N_HEADS * HEAD_DIM), BF16),
        compiler_params=_params(2),
        name="swa",
    )(sinks, q, k, v, k_prev, v_prev, past_k, past_v)


def _xattn(x1, g, wq, mk_ref, mv_ref, wo):
    h = _rms(x1, g).astype(BF16)
    q = _dot(h, wq).astype(BF16)
    scale = X_HEAD_DIM ** -0.5
    outs = []
    for hd in range(X_HEADS):
        cols = slice(hd * X_HEAD_DIM, (hd + 1) * X_HEAD_DIM)
        s = _dot_t(q[:, cols], mk_ref[0, 0, :, hd, :].astype(BF16)) * scale
        p = jnp.exp(s - jnp.max(s, axis=1, keepdims=True))
        den = jnp.sum(p, axis=1, keepdims=True)
        outs.append(_dot(p.astype(BF16), mv_ref[0, 0, :, hd, :].astype(BF16)) / den)
    o = jnp.concatenate(outs, axis=1).astype(BF16)
    return x1 + _dot(o, wo)


def _conv3(cur, carry, w):
    n = cur.shape[0]
    row = lax.broadcasted_iota(jnp.int32, (n, 1), 0)
    c1, c2 = carry[SUBLANES - 1:SUBLANES], carry[SUBLANES - 2:SUBLANES - 1]
    m1 = jnp.where(row == 0, c1, pltpu.roll(cur, 1, 0))
    m2 = jnp.where(row == 0, c2, jnp.where(row == 1, c1, pltpu.roll(cur, 2, 0)))
    return w[0:1] * m2 + w[1:2] * m1 + w[2:3] * cur


def _mix_even_xattn_body(x_ref, ya_ref, yb_ref, woa_ref, wob_ref, g_ref, wq_ref, mk_ref, mv_ref, wo_ref, o_ref):
    x1 = x_ref[0] + _dot(ya_ref[0], woa_ref[0]) + _dot(yb_ref[0], wob_ref[0])
    o_ref[0] = _xattn(x1, g_ref[...], wq_ref[0], mk_ref, mv_ref, wo_ref[0])


def _mix_even_xattn(x, ya, yb, w_out, g, wq, mem_k, mem_v, wo, layer, layer_e, tm):
    bsz, seq, _ = x.shape
    tok = lambda w: pl.BlockSpec((1, tm, w), lambda b, i: (b, i, 0))
    mem = pl.BlockSpec((1, 1, N_MEM, X_HEADS, X_HEAD_DIM), lambda b, i: (layer, b, 0, 0, 0))
    half = D_MODEL // 2
    sq = lambda: _const_spec((1, D_MODEL, D_MODEL), (layer, 0, 0))
    return pl.pallas_call(
        _mix_even_xattn_body,
        grid=(bsz, seq // tm),
        in_specs=[tok(D_MODEL), tok(half), tok(half),
                  _const_spec((1, half, D_MODEL), (layer_e, 0, 0)), _const_spec((1, half, D_MODEL), (layer_e, 1, 0)),
                  _const_spec((1, D_MODEL)), sq(), mem, mem, sq()],
        out_specs=tok(D_MODEL),
        out_shape=jax.ShapeDtypeStruct(x.shape, F32),
        compiler_params=_params(2),
        name="mix_even_xattn",
    )(x, ya, yb, w_out, w_out, g, wq, mem_k, mem_v, wo)


def _mix_odd_xattn_body(x_ref, gm_ref, win_ref, cw_ref, wout_ref, prev_ref, g_ref, wq_ref, mk_ref, mv_ref, wo_ref,
                        o_ref, st_ref, carry_ref, *, tm):
    @pl.when(pl.program_id(1) == 0)
    def _():
        carry_ref[SUBLANES - 2:SUBLANES, :] = prev_ref[0]

    x = x_ref[0]
    proj = _dot(_rms(x, gm_ref[...]).astype(BF16), win_ref[0])
    gate_b, gate_c, z = proj[:, :D_MODEL], proj[:, D_MODEL:2 * D_MODEL], proj[:, 2 * D_MODEL:]
    cz = gate_c * z
    z_conv = _conv3(cz, carry_ref[...], cw_ref[0])
    carry_ref[...] = cz[tm - SUBLANES:]
    st_ref[0] = cz[tm - (CONV_WIDTH - 1):]
    x1 = x + _dot((gate_b * z_conv).astype(BF16), wout_ref[0])
    o_ref[0] = _xattn(x1, g_ref[...], wq_ref[0], mk_ref, mv_ref, wo_ref[0])


def _mix_odd_xattn(x, gm, w_in, conv_w, w_out, prev, g, wq, mem_k, mem_v, wo, layer, layer_o, tm):
    bsz, seq, _ = x.shape
    tok = pl.BlockSpec((1, tm, D_MODEL), lambda b, i: (b, i, 0))
    mem = pl.BlockSpec((1, 1, N_MEM, X_HEADS, X_HEAD_DIM), lambda b, i: (layer, b, 0, 0, 0))
    st = pl.BlockSpec((1, CONV_WIDTH - 1, D_MODEL), lambda b, i: (b, 0, 0))
    sq = lambda l: _const_spec((1, D_MODEL, D_MODEL), (l, 0, 0))
    return pl.pallas_call(
        functools.partial(_mix_odd_xattn_body, tm=tm),
        grid=(bsz, seq // tm),
        in_specs=[tok, _const_spec((1, D_MODEL)), _const_spec((1, D_MODEL, 3 * D_MODEL), (layer_o, 0, 0)),
                  _const_spec((1, CONV_WIDTH, D_MODEL), (layer_o, 0, 0)), sq(layer_o), st,
                  _const_spec((1, D_MODEL)), sq(layer), mem, mem, sq(layer)],
        out_specs=[tok, st],
        out_shape=[jax.ShapeDtypeStruct(x.shape, F32),
                   jax.ShapeDtypeStruct((bsz, CONV_WIDTH - 1, D_MODEL), F32)],
        scratch_shapes=[pltpu.VMEM((SUBLANES, D_MODEL), F32)],
        compiler_params=_params(2),
        name="mix_odd_xattn",
    )(x, gm, w_in, conv_w, w_out, prev, g, wq, mem_k, mem_v, wo)


def _ffn_body(x_ref, g_ref, wup_ref, cw_ref, wdn_ref, prev_ref, gfin_ref, o_ref, st_ref, carry_ref, *, tm, final):
    @pl.when(pl.program_id(1) == 0)
    def _():
        carry_ref[SUBLANES - 2:SUBLANES, :] = prev_ref[0]

    x = x_ref[0]
    h = _rms(x, g_ref[...]).astype(BF16)
    acc = x
    for c in range(D_FF // FF_CHUNK):
        cols = slice(c * FF_CHUNK, (c + 1) * FF_CHUNK)
        gate = _dot(h, wup_ref[0, :, cols])
        val = _dot(h, wup_ref[0, :, D_FF + c * FF_CHUNK: D_FF + (c + 1) * FF_CHUNK])
        gate_c = _conv3(gate, carry_ref[:, cols], cw_ref[0, :, cols])
        carry_ref[:, cols] = gate[tm - SUBLANES:]
        st_ref[0, :, cols] = gate[tm - (CONV_WIDTH - 1):]
        act = gate_c * jax.nn.sigmoid(gate_c) * val
        acc = acc + _dot(act.astype(BF16), wdn_ref[0, cols, :])
    o_ref[0] = _rms(acc, gfin_ref[...]) if final else acc


def _ffn(x, g, w_up, conv_w, w_down, prev, g_final, layer, tm, final):
    bsz, seq, _ = x.shape
    tok = pl.BlockSpec((1, tm, D_MODEL), lambda b, i: (b, i, 0))
    st = pl.BlockSpec((1, CONV_WIDTH - 1, D_FF), lambda b, i: (b, 0, 0))
    return pl.pallas_call(
        functools.partial(_ffn_body, tm=tm, final=final),
        grid=(bsz, seq // tm),
        in_specs=[tok, _const_spec((1, D_MODEL)), _const_spec((1, D_MODEL, 2 * D_FF), (layer, 0, 0)),
                  _const_spec((1, CONV_WIDTH, D_FF), (layer, 0, 0)), _const_spec((1, D_FF, D_MODEL), (layer, 0, 0)),
                  st, _const_spec((1, D_MODEL))],
        out_specs=[tok, st],
        out_shape=[jax.ShapeDtypeStruct(x.shape, F32),
                   jax.ShapeDtypeStruct((bsz, CONV_WIDTH - 1, D_FF), F32)],
        scratch_shapes=[pltpu.VMEM((SUBLANES, D_FF), F32)],
        compiler_params=_params(2),
        name="conv_ffn",
    )(x, g, w_up, conv_w, w_down, prev, g_final)


def _mem_kv_body(m_ref, w_ref, k_ref, v_ref):
    kv = _dot(m_ref[0].astype(BF16), w_ref[0])
    for hd in range(X_HEADS):
        k_ref[0, 0, :, hd, :] = kv[:, hd * X_HEAD_DIM:(hd + 1) * X_HEAD_DIM]
        v_ref[0, 0, :, hd, :] = kv[:, D_MODEL + hd * X_HEAD_DIM:D_MODEL + (hd + 1) * X_HEAD_DIM]


def _mem_kv(mem, w_kv):
    bsz = mem.shape[0]
    heads = pl.BlockSpec((1, 1, N_MEM, X_HEADS, X_HEAD_DIM), lambda l, b: (l, b, 0, 0, 0))
    return pl.pallas_call(
        _mem_kv_body,
        grid=(DEPTH, bsz),
        in_specs=[pl.BlockSpec((1, N_MEM, D_MODEL), lambda l, b: (b, 0, 0)),
                  pl.BlockSpec((1, D_MODEL, 2 * D_MODEL), lambda l, b: (l, 0, 0))],
        out_specs=[heads, heads],
        out_shape=[jax.ShapeDtypeStruct((DEPTH, bsz, N_MEM, X_HEADS, X_HEAD_DIM), F32)] * 2,
        compiler_params=_params(2),
        name="mem_kv",
    )(mem, w_kv)


def _trunk(x, start, past_k, past_v, ssm_re, ssm_im, conv_mix_prev, conv_ffn_prev, mem_k, mem_v, w, s5, cfg):
    bsz, seq, _ = x.shape
    tm, tq, qg = cfg["tm"], cfg["tq"], cfg["qg"]
    rows, seg, nsteps = cfg["rows"], cfg["seg"], cfg["nsteps"]
    tables = _rope_tables(seq, start, tm)
    u, q, k, v = _inproj_even(x, w["norm_mix"][0:1], w["w_in_even"], tables, 0, tm)
    h0 = jnp.concatenate([ssm_re[0].reshape(bsz, S5_STATES), ssm_im[0].reshape(bsz, S5_STATES)], axis=1)
    if cfg["one_block"]:
        ya, h_all = _s5_mixer(u.reshape(1, bsz * seq, SSM_WIDTH), jnp.repeat(h0, seg, axis=0)[None],
                              s5["wg"], s5["wy"], cfg["pow2"], cfg["powr"], w["ssm_d"], w["w_glu"], 0,
                              rows, seg, nsteps)
        ya = ya.reshape(bsz, seq, SSM_WIDTH)
        h_last = h_all[0, seg - 1::seg]
    else:
        ya, h_all = _s5_mixer(u, h0[:, None, :], s5["wg"], s5["wy"], cfg["pow2"], cfg["powr"], w["ssm_d"],
                              w["w_glu"], 0, rows, seg, nsteps)
        h_last = h_all[:, 0]
    new_re = h_last[:, :S5_STATES].reshape(1, bsz, SSM_GROUPS, SSM_STATE)
    new_im = h_last[:, S5_STATES:].reshape(1, bsz, SSM_GROUPS, SSM_STATE)
    yb = _swa(q, k, v, past_k[0].reshape(bsz, WINDOW, KV_WIDTH), past_v[0].reshape(bsz, WINDOW, KV_WIDTH),
              w["attn_sinks"][0], start, tq, qg)
    x = _mix_even_xattn(x, ya, yb, w["w_out_even"], w["norm_xattn"][0:1], w["xattn_wq"], mem_k, mem_v,
                        w["xattn_wo"], 0, 0, tm)
    x, cf0 = _ffn(x, w["norm_ffn"][0:1], w["ffn_w_up"], w["ffn_conv_w"], w["ffn_w_down"], conv_ffn_prev[0],
                  w["norm_final"], 0, tm, False)
    x, cm = _mix_odd_xattn(x, w["norm_mix"][1:2], w["w_in_odd"], w["conv_mix_w"], w["w_out_odd"], conv_mix_prev[0],
                           w["norm_xattn"][1:2], w["xattn_wq"], mem_k, mem_v, w["xattn_wo"], 1, 0, tm)
    y, cf1 = _ffn(x, w["norm_ffn"][1:2], w["ffn_w_up"], w["ffn_conv_w"], w["ffn_w_down"], conv_ffn_prev[1],
                  w["norm_final"], 1, tm, True)
    keep = min(seq, WINDOW)
    new_k = k[:, seq - keep:].reshape(1, bsz, keep, N_KV_HEADS, HEAD_DIM)
    new_v = v[:, seq - keep:].reshape(1, bsz, keep, N_KV_HEADS, HEAD_DIM)
    return y, new_k, new_v, new_re, new_im, cm[None], jnp.stack([cf0, cf1])


def kernel(x_prompt, x_sample, mem_prompt, cache_win_k, cache_win_v, state_ssm_re, state_ssm_im, state_conv_mix, state_conv_ffn, cache_mem_k, cache_mem_v, norm_mix, norm_xattn, norm_ffn, norm_final, w_in_even, w_out_even, ssm_a_re, ssm_a_im, ssm_log_dt, ssm_b_re, ssm_b_im, ssm_c_re, ssm_c_im, ssm_d, w_glu, attn_sinks, w_in_odd, conv_mix_w, w_out_odd, xattn_wq, xattn_wkv, xattn_wo, ffn_w_up, ffn_conv_w, ffn_w_down):
    bp, seq_p, _ = x_prompt.shape
    bs, seq_s, _ = x_sample.shape
    w = dict(norm_mix=norm_mix, norm_xattn=norm_xattn, norm_ffn=norm_ffn, norm_final=norm_final.reshape(1, D_MODEL),
             w_in_even=w_in_even.astype(BF16), w_out_even=w_out_even.astype(BF16),
             ssm_d=ssm_d[0].reshape(1, SSM_WIDTH), w_glu=w_glu.astype(BF16), attn_sinks=attn_sinks,
             w_in_odd=w_in_odd.astype(BF16), conv_mix_w=conv_mix_w, w_out_odd=w_out_odd.astype(BF16),
             xattn_wq=xattn_wq.astype(BF16), xattn_wo=xattn_wo.astype(BF16),
             ffn_w_up=ffn_w_up.astype(BF16), ffn_conv_w=ffn_conv_w, ffn_w_down=ffn_w_down.astype(BF16))

    wg, wy = _s5_prep(ssm_a_re[0], ssm_a_im[0], ssm_log_dt[0], ssm_b_re[0], ssm_b_im[0], ssm_c_re[0], ssm_c_im[0])
    s5 = dict(wg=wg, wy=wy)
    rows_p = seg_p = S5_ROWS
    seg_s = seq_s // S5_T
    rows_s = seg_s * bs
    steps_p, steps_s = int(math.log2(seg_p)), int(math.log2(seg_s))
    pad = lambda n: [1.0] * (-n % SUBLANES)
    exps = ([float(S5_T << m) for m in range(steps_p)] + pad(steps_p)
            + [float(S5_T * (r + 1)) for r in range(rows_p)]
            + [float(S5_T << m) for m in range(steps_s)] + pad(steps_s)
            + [float(S5_T * (r % seg_s + 1)) for r in range(rows_s)])
    pows = _s5_pow_table(exps, ssm_a_re[0], ssm_a_im[0], ssm_log_dt[0])
    o1 = steps_p + len(pad(steps_p))
    o2 = o1 + rows_p
    o3 = o2 + steps_s + len(pad(steps_s))
    cfg_p = dict(tm=512, tq=256, qg=128, rows=rows_p, seg=seg_p, nsteps=steps_p, one_block=False,
                 pow2=pows[:o1], powr=pows[o1:o2])
    cfg_s = dict(tm=seq_s, tq=seq_s, qg=seq_s, rows=rows_s, seg=seg_s, nsteps=steps_s, one_block=True,
                 pow2=pows[o2:o3], powr=pows[o3:])

    mem_k_p, mem_v_p = _mem_kv(mem_prompt, xattn_wkv.astype(BF16))
    cw = CONV_WIDTH - 1
    zk = jnp.zeros((1, bp, WINDOW, N_KV_HEADS, HEAD_DIM), F32)
    zs = jnp.zeros((1, bp, SSM_GROUPS, SSM_STATE), F32)
    zcm = jnp.zeros((1, bp, cw, D_MODEL), F32)
    zcf = jnp.zeros((DEPTH, bp, cw, D_FF), F32)
    y_p, k_p, v_p, re_p, im_p, cm_p, cf_p = _trunk(x_prompt, 0, zk, zk, zs, zs, zcm, zcf, mem_k_p, mem_v_p, w, s5, cfg_p)
    y_s, k_s, v_s, re_s, im_s, cm_s, cf_s = _trunk(x_sample, PAST_LEN, cache_win_k, cache_win_v, state_ssm_re,
                                                   state_ssm_im, state_conv_mix, state_conv_ffn, cache_mem_k,
                                                   cache_mem_v, w, s5, cfg_s)
    return (y_p, y_s, k_p, v_p, re_p, im_p, cm_p, cf_p, mem_k_p, mem_v_p, k_s, v_s, re_s, im_s, cm_s, cf_s)
```

```python
import functools
import math

import numpy as np
import jax
import jax.numpy as jnp
from jax import lax
from jax.experimental import pallas as pl
from jax.experimental.pallas import tpu as pltpu

F32 = jnp.float32
BF16 = jnp.bfloat16

D_MODEL = 1024
DEPTH = 2
PAST_LEN = 2048
CHUNK = 64
SSM_WIDTH = 512
SSM_GROUP = 16
SSM_GROUPS = 32
SSM_STATE = 64
HEAD_DIM = 64
N_HEADS = 8
N_KV_HEADS = 2
GQ = N_HEADS // N_KV_HEADS
WINDOW = 128
ROPE_THETA = 10000.0
KV_WIDTH = N_KV_HEADS * HEAD_DIM
EVEN_IN = SSM_WIDTH + N_HEADS * HEAD_DIM + 2 * KV_WIDTH
CONV_WIDTH = 3
N_MEM = 256
X_HEADS = 4
X_HEAD_DIM = 256
D_FF = 2816
EPS = 1e-6
NEG = -1e30

LANES = 128
SUBLANES = 8
S5_T = 8
S5_TILES = SSM_WIDTH // LANES
S5_TILE_STATES = (LANES // SSM_GROUP) * SSM_STATE
S5_STATES = SSM_GROUPS * SSM_STATE
S5_ROWS = 128
FF_CHUNK = 1408
VMEM_LIMIT = 56 * 1024 * 1024


def _const_spec(shape, index=None):
    idx = tuple(index) if index is not None else (0,) * len(shape)
    return pl.BlockSpec(shape, lambda *_: idx, pipeline_mode=pl.Buffered(1))


def _params(n_axes):
    return pltpu.CompilerParams(dimension_semantics=("arbitrary",) * n_axes,
                                vmem_limit_bytes=VMEM_LIMIT)


def _rms(x, g):
    return x * lax.rsqrt(jnp.mean(x * x, axis=-1, keepdims=True) + EPS) * g


def _dot(a, b):
    return jnp.dot(a, b, preferred_element_type=F32)


def _dot_t(a, b):
    return lax.dot_general(a, b, (((1,), (1,)), ((), ())), preferred_element_type=F32)


def _rope_table_body(cos_ref, sina_ref, sinb_ref, *, start, tl):
    i = pl.program_id(0)
    row = lax.broadcasted_iota(jnp.int32, (tl, LANES), 0)
    lane = lax.broadcasted_iota(jnp.int32, (tl, LANES), 1)
    pos = (start + i * tl + row).astype(F32)
    half = HEAD_DIM // 2
    freq = (lane[0:1] & (half - 1)).astype(F32)
    inv = jnp.exp(freq * (-2.0 / HEAD_DIM * math.log(ROPE_THETA)))
    ang = pos * inv
    c = jnp.cos(ang)
    s = jnp.sin(ang)
    first = (lane & (HEAD_DIM - 1)) < HEAD_DIM // 2
    cos_ref[...] = c
    sina_ref[...] = jnp.where(first, -s, 0.0)
    sinb_ref[...] = jnp.where(first, 0.0, s)


def _rope_tables(seq, start, tl):
    spec = pl.BlockSpec((tl, LANES), lambda i: (i, 0))
    return pl.pallas_call(
        functools.partial(_rope_table_body, start=start, tl=tl),
        grid=(seq // tl,),
        in_specs=[],
        out_specs=[spec, spec, spec],
        out_shape=[jax.ShapeDtypeStruct((seq, LANES), F32)] * 3,
        compiler_params=_params(1),
        name="rope_tables",
    )()


def _rope(t, cos, sina, sinb):
    n = t.shape[1]
    half = HEAD_DIM // 2
    return t * cos + pltpu.roll(t, n - half, 1) * sina + pltpu.roll(t, half, 1) * sinb


def _inproj_even_body(x_ref, g_ref, w_ref, cos_ref, sa_ref, sb_ref, u_ref, q_ref, k_ref, v_ref):
    h = _rms(x_ref[0], g_ref[...]).astype(BF16)
    proj = _dot(h, w_ref[0])
    o0 = SSM_WIDTH
    o1 = o0 + N_HEADS * HEAD_DIM
    o2 = o1 + KV_WIDTH
    cos, sa, sb = cos_ref[...], sa_ref[...], sb_ref[...]
    rep = (o1 - o0) // LANES
    cos_q = jnp.concatenate([cos] * rep, axis=1)
    sa_q = jnp.concatenate([sa] * rep, axis=1)
    sb_q = jnp.concatenate([sb] * rep, axis=1)
    u_ref[0] = proj[:, :o0]
    q_ref[0] = _rope(proj[:, o0:o1], cos_q, sa_q, sb_q).astype(BF16)
    k_ref[0] = _rope(proj[:, o1:o2], cos, sa, sb)
    v_ref[0] = proj[:, o2:]


def _inproj_even(x, g, w_in, tables, layer_e, tm):
    bsz, seq, _ = x.shape
    cos, sa, sb = tables
    tok = lambda w: pl.BlockSpec((1, tm, w), lambda b, i: (b, i, 0))
    tab = pl.BlockSpec((tm, LANES), lambda b, i: (i, 0))
    return pl.pallas_call(
        _inproj_even_body,
        grid=(bsz, seq // tm),
        in_specs=[tok(D_MODEL), _const_spec((1, D_MODEL)),
                  _const_spec((1, D_MODEL, EVEN_IN), (layer_e, 0, 0)), tab, tab, tab],
        out_specs=[tok(SSM_WIDTH), tok(N_HEADS * HEAD_DIM), tok(KV_WIDTH), tok(KV_WIDTH)],
        out_shape=[jax.ShapeDtypeStruct((bsz, seq, SSM_WIDTH), F32),
                   jax.ShapeDtypeStruct((bsz, seq, N_HEADS * HEAD_DIM), BF16),
                   jax.ShapeDtypeStruct((bsz, seq, KV_WIDTH), F32),
                   jax.ShapeDtypeStruct((bsz, seq, KV_WIDTH), F32)],
        compiler_params=_params(2),
        name="inproj_even",
    )(x, g, w_in, cos, sa, sb)


def _s5_prep_body(lr_ref, li_ref, ld_ref, bre_ref, bim_ref, lrc_ref, lic_ref, ldc_ref,
                  cre_ref, cim_ref, wg_ref, wy_ref):
    ts = S5_TILE_STATES
    tw = S5_T * LANES

    def step(z_r, z_i):
        mag = jnp.exp(z_r)
        a_r, a_i = mag * jnp.cos(z_i), mag * jnp.sin(z_i)
        return a_r, a_i, lambda p: (p[0] * a_r - p[1] * a_i, p[0] * a_i + p[1] * a_r)

    lr, li = lr_ref[0], li_ref[0]
    dt = jnp.exp(ld_ref[0])
    ar, ai, mul_a = step(lr * dt, li * dt)
    nrm = lr * lr + li * li
    cbr = ((ar - 1.0) * lr + ai * li) / nrm
    cbi = (ai * lr - (ar - 1.0) * li) / nrm
    b_re, b_im = bre_ref[0], bim_ref[0]
    bbr = cbr * b_re - cbi * b_im
    bbi = cbr * b_im + cbi * b_re
    p = (jnp.ones_like(ar), jnp.zeros_like(ar))
    for s in reversed(range(S5_T)):
        rows = slice(s * LANES, (s + 1) * LANES)
        wg_ref[0, rows, :ts] = (p[0] * bbr - p[1] * bbi).astype(BF16)
        wg_ref[0, rows, ts:] = (p[0] * bbi + p[1] * bbr).astype(BF16)
        for t in range(s):
            wy_ref[0, rows, t * LANES:(t + 1) * LANES] = jnp.zeros((LANES, LANES), BF16)
        p = mul_a(p)
    dtc = jnp.exp(ldc_ref[0])
    _, _, mul_ac = step(lrc_ref[0] * dtc, lic_ref[0] * dtc)
    c_re, c_im = cre_ref[0], cim_ref[0]
    bb = jnp.concatenate([bbr, bbi], axis=1).astype(BF16)
    pc = (jnp.ones_like(dtc), jnp.zeros_like(dtc))
    for k in range(S5_T + 1):
        blk = jnp.concatenate([pc[0] * c_re - pc[1] * c_im, -(pc[0] * c_im + pc[1] * c_re)], axis=0).astype(BF16)
        if k >= 1:
            wy_ref[0, tw:, (k - 1) * LANES:k * LANES] = blk
        if k < S5_T:
            lag = _dot(bb, blk).astype(BF16)
            for s in range(S5_T - k):
                t = s + k
                wy_ref[0, s * LANES:(s + 1) * LANES, t * LANES:(t + 1) * LANES] = lag
        pc = mul_ac(pc)


def _s5_prep(a_re, a_im, log_dt, b_re, b_im, c_re, c_im):
    gpt = LANES // SSM_GROUP
    eye = jnp.eye(gpt, dtype=bool)

    def rows(p):
        return p.reshape(S5_TILES, 1, S5_TILE_STATES)

    def cols(p):
        return jnp.broadcast_to(p.reshape(S5_TILES, S5_TILE_STATES, 1), (S5_TILES, S5_TILE_STATES, LANES))

    def bmat(b):
        bt = b.reshape(S5_TILES, gpt, SSM_STATE, SSM_GROUP).transpose(0, 1, 3, 2)
        full = jnp.where(eye[None, :, None, :, None], bt[:, :, :, None, :], 0.0)
        return full.reshape(S5_TILES, LANES, S5_TILE_STATES)

    def cmat(c):
        ct = c.reshape(S5_TILES, gpt, SSM_GROUP, SSM_STATE).transpose(0, 1, 3, 2)
        full = jnp.where(eye[None, :, None, :, None], ct[:, :, :, None, :], 0.0)
        return full.reshape(S5_TILES, S5_TILE_STATES, LANES)

    row_spec = pl.BlockSpec((1, 1, S5_TILE_STATES), lambda j: (j, 0, 0))
    b_spec = pl.BlockSpec((1, LANES, S5_TILE_STATES), lambda j: (j, 0, 0))
    c_spec = pl.BlockSpec((1, S5_TILE_STATES, LANES), lambda j: (j, 0, 0))
    tw = S5_T * LANES
    return pl.pallas_call(
        _s5_prep_body,
        grid=(S5_TILES,),
        in_specs=[row_spec] * 3 + [b_spec] * 2 + [c_spec] * 5,
        out_specs=[pl.BlockSpec((1, tw, 2 * S5_TILE_STATES), lambda j: (j, 0, 0)),
                   pl.BlockSpec((1, tw + 2 * S5_TILE_STATES, tw), lambda j: (j, 0, 0))],
        out_shape=[jax.ShapeDtypeStruct((S5_TILES, tw, 2 * S5_TILE_STATES), BF16),
                   jax.ShapeDtypeStruct((S5_TILES, tw + 2 * S5_TILE_STATES, tw), BF16)],
        compiler_params=_params(1),
        name="s5_prep",
    )(rows(a_re), rows(a_im), rows(log_dt), bmat(b_re), bmat(b_im),
      cols(a_re), cols(a_im), cols(log_dt), cmat(c_re), cmat(c_im))


def _s5_pow_body(ex_ref, lr_ref, li_ref, ld_ref, o_ref):
    dt = jnp.exp(ld_ref[...])
    zr, zi = lr_ref[...] * dt, li_ref[...] * dt
    ex = ex_ref[...]
    mag = jnp.exp(ex * zr)
    o_ref[:, :S5_STATES] = mag * jnp.cos(ex * zi)
    o_ref[:, S5_STATES:] = mag * jnp.sin(ex * zi)


def _s5_pow_table(exponents, a_re, a_im, log_dt):
    n = len(exponents)
    ex = jnp.asarray(np.asarray(exponents, np.float32)[:, None])
    flat = lambda p: p.reshape(1, S5_STATES)
    return pl.pallas_call(
        _s5_pow_body,
        grid=(1,),
        in_specs=[pl.BlockSpec((n, 1), lambda i: (0, 0))] + [pl.BlockSpec((1, S5_STATES), lambda i: (0, 0))] * 3,
        out_specs=pl.BlockSpec((n, 2 * S5_STATES), lambda i: (0, 0)),
        out_shape=jax.ShapeDtypeStruct((n, 2 * S5_STATES), F32),
        compiler_params=_params(1),
        name="s5_pow_table",
    )(ex, flat(a_re), flat(a_im), flat(log_dt))


def _s5_body(u_ref, h0_ref, wg_ref, wy_ref, pow2_ref, powr_ref, d_ref, wglu_ref,
             y_ref, hout_ref, carry_ref, us_ref, ys_ref, *, rows, seg, nsteps, hrows):
    ns = S5_STATES
    ts = S5_TILE_STATES

    @pl.when(pl.program_id(1) == 0)
    def _():
        carry_ref[...] = h0_ref[0]

    for j in range(S5_TILES):
        us_ref[j] = u_ref[0, :, j * LANES:(j + 1) * LANES]

    def chunk_cols(j, t):
        return us_ref[j, pl.ds(t, rows, stride=S5_T), :]

    xs, g_re, g_im = [], [], []
    for j in range(S5_TILES):
        xj = jnp.concatenate([chunk_cols(j, s) for s in range(S5_T)], axis=1).astype(BF16)
        g = _dot(xj, wg_ref[j])
        xs.append(xj)
        g_re.append(g[:, :ts])
        g_im.append(g[:, ts:])
    h_re = jnp.concatenate(g_re, axis=1)
    h_im = jnp.concatenate(g_im, axis=1)
    pos = lax.broadcasted_iota(jnp.int32, (rows, 1), 0) & (seg - 1)
    for m in range(nsteps):
        k = 1 << m
        a_r, a_i = pow2_ref[m:m + 1, :ns], pow2_ref[m:m + 1, ns:]
        s_re = jnp.where(pos >= k, pltpu.roll(h_re, k, 0), 0.0)
        s_im = jnp.where(pos >= k, pltpu.roll(h_im, k, 0), 0.0)
        h_re, h_im = h_re + a_r * s_re - a_i * s_im, h_im + a_r * s_im + a_i * s_re
    cin = carry_ref[...]
    c_re, c_im = cin[:, :ns], cin[:, ns:]
    p_r, p_i = powr_ref[:, :ns], powr_ref[:, ns:]
    h_re, h_im = h_re + p_r * c_re - p_i * c_im, h_im + p_r * c_im + p_i * c_re
    last = jnp.concatenate([h_re[rows - hrows:], h_im[rows - hrows:]], axis=1)
    carry_ref[...] = last
    hout_ref[0] = last
    prev_re = jnp.where(pos >= 1, pltpu.roll(h_re, 1, 0), c_re)
    prev_im = jnp.where(pos >= 1, pltpu.roll(h_im, 1, 0), c_im)
    y_tiles = []
    for j in range(S5_TILES):
        lhs = jnp.concatenate([xs[j], prev_re[:, j * ts:(j + 1) * ts].astype(BF16),
                               prev_im[:, j * ts:(j + 1) * ts].astype(BF16)], axis=1)
        y_tiles.append(_dot(lhs, wy_ref[j]))
    d = d_ref[...]
    wglu = wglu_ref[0]
    for t in range(S5_T):
        ut = jnp.concatenate([chunk_cols(j, t) for j in range(S5_TILES)], axis=1)
        yt = jnp.concatenate([y_tiles[j][:, t * LANES:(t + 1) * LANES] for j in range(S5_TILES)], axis=1)
        g = jax.nn.gelu(yt + d * ut)
        o = g * jax.nn.sigmoid(_dot(g.astype(BF16), wglu))
        for j in range(S5_TILES):
            ys_ref[j, pl.ds(t, rows, stride=S5_T), :] = o[:, j * LANES:(j + 1) * LANES]
    for j in range(S5_TILES):
        y_ref[0, :, j * LANES:(j + 1) * LANES] = ys_ref[j].astype(BF16)


def _s5_mixer(u, h0, wg, wy, pow2, powr, d, w_glu, layer_e, rows, seg, nsteps):
    nb, ntok, width = u.shape
    tb = rows * S5_T
    hrows = h0.shape[1]
    blk = pl.BlockSpec((1, tb, width), lambda b, i: (b, i, 0))
    state = pl.BlockSpec((1, hrows, 2 * S5_STATES), lambda b, i: (b, 0, 0))
    return pl.pallas_call(
        functools.partial(_s5_body, rows=rows, seg=seg, nsteps=nsteps, hrows=hrows),
        grid=(nb, ntok // tb),
        in_specs=[blk, state, _const_spec(wg.shape), _const_spec(wy.shape),
                  _const_spec(pow2.shape), _const_spec(powr.shape), _const_spec((1, SSM_WIDTH)),
                  _const_spec((1, SSM_WIDTH, SSM_WIDTH), (layer_e, 0, 0))],
        out_specs=[blk, state],
        out_shape=[jax.ShapeDtypeStruct((nb, ntok, width), BF16),
                   jax.ShapeDtypeStruct((nb, hrows, 2 * S5_STATES), F32)],
        scratch_shapes=[pltpu.VMEM((hrows, 2 * S5_STATES), F32),
                        pltpu.VMEM((S5_TILES, tb, LANES), F32), pltpu.VMEM((S5_TILES, tb, LANES), F32)],
        compiler_params=_params(2),
        name="s5_mixer",
    )(u, h0, wg, wy, pow2, powr, d, w_glu)


def _swa_body(sink_ref, q_ref, kc_ref, vc_ref, kp_ref, vp_ref, pk_ref, pv_ref, o_ref, *, tq, qg, start):
    first = pl.program_id(1) == 0
    k_prev = jnp.where(first, pk_ref[0], kp_ref[0])
    v_prev = jnp.where(first, pv_ref[0], vp_ref[0])
    k_all = jnp.concatenate([k_prev, kc_ref[0]], axis=0)
    v_all = jnp.concatenate([v_prev, vc_ref[0]], axis=0)
    nk = WINDOW + qg
    lane = lax.broadcasted_iota(jnp.int32, (1, KV_WIDTH), 1)
    slot = lax.broadcasted_iota(jnp.int32, (1, nk), 1)
    qrow = lax.broadcasted_iota(jnp.int32, (2 * qg, 1), 0)
    shift = CHUNK.bit_length() - 1
    q_chunk = (qrow & (qg - 1)) >> shift
    k_chunk = (slot >> shift) - WINDOW // CHUNK
    visible = (k_chunk <= q_chunk) & (k_chunk >= q_chunk - WINDOW // CHUNK)
    cached_ok = slot >= jnp.where(first, WINDOW - start, 0)
    scale = HEAD_DIM ** -0.5
    placed = []
    for kh in range(N_KV_HEADS):
        own = (lane >> (HEAD_DIM.bit_length() - 1)) == kh
        kz, vz = jnp.where(own, k_all, 0.0), jnp.where(own, v_all, 0.0)
        kr, vr = pltpu.roll(kz, HEAD_DIM, 1), pltpu.roll(vz, HEAD_DIM, 1)
        k_lo, k_hi = (kz, kr) if kh == 0 else (kr, kz)
        v_lo, v_hi = (vz, vr) if kh == 0 else (vr, vz)
        placed.append(((k_lo.astype(BF16), v_lo.astype(BF16)), (k_hi.astype(BF16), v_hi.astype(BF16))))
    for gi in range(tq // qg):
        mask = visible & cached_ok if gi == 0 else visible
        key_rows = slice(gi * qg, gi * qg + nk)
        qgrp = q_ref[0, gi * qg:(gi + 1) * qg, :]
        outs = []
        for kh in range(N_KV_HEADS):
            base = kh * GQ * HEAD_DIM
            qs = jnp.concatenate([qgrp[:, base:base + LANES], qgrp[:, base + LANES:base + 2 * LANES]], axis=0)
            acc = None
            for par, (k_pl, v_pl) in enumerate(placed[kh]):
                kk, vv = k_pl[key_rows], v_pl[key_rows]
                s = jnp.where(mask, _dot_t(qs, kk) * scale, NEG)
                sink = jnp.where(qrow < qg, sink_ref[kh * GQ + par], sink_ref[kh * GQ + 2 + par])
                m = jnp.maximum(jnp.max(s, axis=1, keepdims=True), sink)
                p = jnp.exp(s - m)
                den = jnp.sum(p, axis=1, keepdims=True) + jnp.exp(sink - m)
                o = _dot(p.astype(BF16), vv) / den
                acc = o if acc is None else acc + o
            outs += [acc[:qg], acc[qg:]]
        o_ref[0, gi * qg:(gi + 1) * qg, :] = jnp.concatenate(outs, axis=1).astype(BF16)


def _swa(q, k, v, past_k, past_v, sinks, start, tq, qg):
    bsz, seq, _ = q.shape
    ntiles = seq // tq
    tok = lambda w: pl.BlockSpec((1, tq, w), lambda b, i: (b, i, 0))
    past = pl.BlockSpec((1, WINDOW, KV_WIDTH), lambda b, i: (b, 0, 0))
    if ntiles > 1:
        per = tq // WINDOW
        prev = pl.BlockSpec((1, WINDOW, KV_WIDTH), lambda b, i: (b, jnp.maximum(i * per - 1, 0), 0))
        k_prev, v_prev = k, v
    else:
        prev, k_prev, v_prev = past, past_k, past_v
    return pl.pallas_call(
        functools.partial(_swa_body, tq=tq, qg=qg, start=start),
        grid=(bsz, ntiles),
        in_specs=[pl.BlockSpec(memory_space=pltpu.SMEM), tok(N_HEADS * HEAD_DIM), tok(KV_WIDTH), tok(KV_WIDTH),
                  prev, prev, past, past],
        out_specs=tok(N_HEADS * HEAD_DIM),
        out_shape=jax.ShapeDtypeStruct((bsz, seq, N_HEADS * HEAD_DIM), BF16),
        compiler_params=_params(2),
        name="swa",
    )(sinks, q, k, v, k_prev, v_prev, past_k, past_v)


def _stage_memory(mk_ref, mv_ref, mks_ref, mvs_ref):
    @pl.when(pl.program_id(1) == 0)
    def _():
        for hd in range(X_HEADS):
            cols = slice(hd * X_HEAD_DIM, (hd + 1) * X_HEAD_DIM)
            mks_ref[:, cols] = mk_ref[0, 0, :, hd, :].astype(BF16)
            mvs_ref[:, cols] = mv_ref[0, 0, :, hd, :].astype(BF16)


def _xattn(x1, g, wq, mks_ref, mvs_ref, wo):
    h = _rms(x1, g).astype(BF16)
    q = _dot(h, wq).astype(BF16)
    scale = X_HEAD_DIM ** -0.5
    outs = []
    for hd in range(X_HEADS):
        cols = slice(hd * X_HEAD_DIM, (hd + 1) * X_HEAD_DIM)
        s = _dot_t(q[:, cols], mks_ref[:, cols]) * scale
        p = jnp.exp(s - jnp.max(s, axis=1, keepdims=True))
        den = jnp.sum(p, axis=1, keepdims=True)
        outs.append(_dot(p.astype(BF16), mvs_ref[:, cols]) / den)
    o = jnp.concatenate(outs, axis=1).astype(BF16)
    return x1 + _dot(o, wo)


def _conv3(cur, carry, w):
    n = cur.shape[0]
    row = lax.broadcasted_iota(jnp.int32, (n, 1), 0)
    c1, c2 = carry[SUBLANES - 1:SUBLANES], carry[SUBLANES - 2:SUBLANES - 1]
    m1 = jnp.where(row == 0, c1, pltpu.roll(cur, 1, 0))
    m2 = jnp.where(row == 0, c2, jnp.where(row == 1, c1, pltpu.roll(cur, 2, 0)))
    return w[0:1] * m2 + w[1:2] * m1 + w[2:3] * cur


def _mix_even_xattn_body(x_ref, ya_ref, yb_ref, woa_ref, wob_ref, g_ref, wq_ref, mk_ref, mv_ref, wo_ref, o_ref,
                         mks_ref, mvs_ref):
    _stage_memory(mk_ref, mv_ref, mks_ref, mvs_ref)
    x1 = x_ref[0] + _dot(ya_ref[0], woa_ref[0]) + _dot(yb_ref[0], wob_ref[0])
    o_ref[0] = _xattn(x1, g_ref[...], wq_ref[0], mks_ref, mvs_ref, wo_ref[0])


def _mix_even_xattn(x, ya, yb, w_out, g, wq, mem_k, mem_v, wo, layer, layer_e, tm):
    bsz, seq, _ = x.shape
    tok = lambda w: pl.BlockSpec((1, tm, w), lambda b, i: (b, i, 0))
    mem = pl.BlockSpec((1, 1, N_MEM, X_HEADS, X_HEAD_DIM), lambda b, i: (layer, b, 0, 0, 0))
    half = D_MODEL // 2
    sq = lambda: _const_spec((1, D_MODEL, D_MODEL), (layer, 0, 0))
    return pl.pallas_call(
        _mix_even_xattn_body,
        grid=(bsz, seq // tm),
        in_specs=[tok(D_MODEL), tok(half), tok(half),
                  _const_spec((1, half, D_MODEL), (layer_e, 0, 0)), _const_spec((1, half, D_MODEL), (layer_e, 1, 0)),
                  _const_spec((1, D_MODEL)), sq(), mem, mem, sq()],
        out_specs=tok(D_MODEL),
        out_shape=jax.ShapeDtypeStruct(x.shape, F32),
        scratch_shapes=[pltpu.VMEM((N_MEM, D_MODEL), BF16)] * 2,
        compiler_params=_params(2),
        name="mix_even_xattn",
    )(x, ya, yb, w_out, w_out, g, wq, mem_k, mem_v, wo)


def _mix_odd_xattn_body(x_ref, gm_ref, win_ref, cw_ref, wout_ref, prev_ref, g_ref, wq_ref, mk_ref, mv_ref, wo_ref,
                        o_ref, st_ref, carry_ref, mks_ref, mvs_ref, *, tm):
    @pl.when(pl.program_id(1) == 0)
    def _():
        carry_ref[SUBLANES - 2:SUBLANES, :] = prev_ref[0]

    _stage_memory(mk_ref, mv_ref, mks_ref, mvs_ref)
    x = x_ref[0]
    proj = _dot(_rms(x, gm_ref[...]).astype(BF16), win_ref[0])
    gate_b, gate_c, z = proj[:, :D_MODEL], proj[:, D_MODEL:2 * D_MODEL], proj[:, 2 * D_MODEL:]
    cz = gate_c * z
    z_conv = _conv3(cz, carry_ref[...], cw_ref[0])
    carry_ref[...] = cz[tm - SUBLANES:]
    st_ref[0] = cz[tm - (CONV_WIDTH - 1):]
    x1 = x + _dot((gate_b * z_conv).astype(BF16), wout_ref[0])
    o_ref[0] = _xattn(x1, g_ref[...], wq_ref[0], mks_ref, mvs_ref, wo_ref[0])


def _mix_odd_xattn(x, gm, w_in, conv_w, w_out, prev, g, wq, mem_k, mem_v, wo, layer, layer_o, tm):
    bsz, seq, _ = x.shape
    tok = pl.BlockSpec((1, tm, D_MODEL), lambda b, i: (b, i, 0))
    mem = pl.BlockSpec((1, 1, N_MEM, X_HEADS, X_HEAD_DIM), lambda b, i: (layer, b, 0, 0, 0))
    st = pl.BlockSpec((1, CONV_WIDTH - 1, D_MODEL), lambda b, i: (b, 0, 0))
    sq = lambda l: _const_spec((1, D_MODEL, D_MODEL), (l, 0, 0))
    return pl.pallas_call(
        functools.partial(_mix_odd_xattn_body, tm=tm),
        grid=(bsz, seq // tm),
        in_specs=[tok, _const_spec((1, D_MODEL)), _const_spec((1, D_MODEL, 3 * D_MODEL), (layer_o, 0, 0)),
                  _const_spec((1, CONV_WIDTH, D_MODEL), (layer_o, 0, 0)), sq(layer_o), st,
                  _const_spec((1, D_MODEL)), sq(layer), mem, mem, sq(layer)],
        out_specs=[tok, st],
        out_shape=[jax.ShapeDtypeStruct(x.shape, F32),
                   jax.ShapeDtypeStruct((bsz, CONV_WIDTH - 1, D_MODEL), F32)],
        scratch_shapes=[pltpu.VMEM((SUBLANES, D_MODEL), F32)] + [pltpu.VMEM((N_MEM, D_MODEL), BF16)] * 2,
        compiler_params=_params(2),
        name="mix_odd_xattn",
    )(x, gm, w_in, conv_w, w_out, prev, g, wq, mem_k, mem_v, wo)


def _ffn_body(x_ref, g_ref, wup_ref, cw_ref, wdn_ref, prev_ref, gfin_ref, o_ref, st_ref, carry_ref, *, tm, final):
    @pl.when(pl.program_id(1) == 0)
    def _():
        carry_ref[SUBLANES - 2:SUBLANES, :] = prev_ref[0]

    x = x_ref[0]
    h = _rms(x, g_ref[...]).astype(BF16)
    acc = x
    for c in range(D_FF // FF_CHUNK):
        cols = slice(c * FF_CHUNK, (c + 1) * FF_CHUNK)
        gate = _dot(h, wup_ref[0, :, cols])
        val = _dot(h, wup_ref[0, :, D_FF + c * FF_CHUNK: D_FF + (c + 1) * FF_CHUNK])
        gate_c = _conv3(gate, carry_ref[:, cols], cw_ref[0, :, cols])
        carry_ref[:, cols] = gate[tm - SUBLANES:]
        st_ref[0, :, cols] = gate[tm - (CONV_WIDTH - 1):]
        act = gate_c * jax.nn.sigmoid(gate_c) * val
        acc = acc + _dot(act.astype(BF16), wdn_ref[0, cols, :])
    o_ref[0] = _rms(acc, gfin_ref[...]) if final else acc


def _ffn(x, g, w_up, conv_w, w_down, prev, g_final, layer, tm, final):
    bsz, seq, _ = x.shape
    tok = pl.BlockSpec((1, tm, D_MODEL), lambda b, i: (b, i, 0))
    st = pl.BlockSpec((1, CONV_WIDTH - 1, D_FF), lambda b, i: (b, 0, 0))
    return pl.pallas_call(
        functools.partial(_ffn_body, tm=tm, final=final),
        grid=(bsz, seq // tm),
        in_specs=[tok, _const_spec((1, D_MODEL)), _const_spec((1, D_MODEL, 2 * D_FF), (layer, 0, 0)),
                  _const_spec((1, CONV_WIDTH, D_FF), (layer, 0, 0)), _const_spec((1, D_FF, D_MODEL), (layer, 0, 0)),
                  st, _const_spec((1, D_MODEL))],
        out_specs=[tok, st],
        out_shape=[jax.ShapeDtypeStruct(x.shape, F32),
                   jax.ShapeDtypeStruct((bsz, CONV_WIDTH - 1, D_FF), F32)],
        scratch_shapes=[pltpu.VMEM((SUBLANES, D_FF), F32)],
        compiler_params=_params(2),
        name="conv_ffn",
    )(x, g, w_up, conv_w, w_down, prev, g_final)


def _mem_kv_body(m_ref, w_ref, k_ref, v_ref):
    kv = _dot(m_ref[0].astype(BF16), w_ref[0])
    for hd in range(X_HEADS):
        k_ref[0, 0, :, hd, :] = kv[:, hd * X_HEAD_DIM:(hd + 1) * X_HEAD_DIM]
        v_ref[0, 0, :, hd, :] = kv[:, D_MODEL + hd * X_HEAD_DIM:D_MODEL + (hd + 1) * X_HEAD_DIM]


def _mem_kv(mem, w_kv):
    bsz = mem.shape[0]
    heads = pl.BlockSpec((1, 1, N_MEM, X_HEADS, X_HEAD_DIM), lambda l, b: (l, b, 0, 0, 0))
    return pl.pallas_call(
        _mem_kv_body,
        grid=(DEPTH, bsz),
        in_specs=[pl.BlockSpec((1, N_MEM, D_MODEL), lambda l, b: (b, 0, 0)),
                  pl.BlockSpec((1, D_MODEL, 2 * D_MODEL), lambda l, b: (l, 0, 0))],
        out_specs=[heads, heads],
        out_shape=[jax.ShapeDtypeStruct((DEPTH, bsz, N_MEM, X_HEADS, X_HEAD_DIM), F32)] * 2,
        compiler_params=_params(2),
        name="mem_kv",
    )(mem, w_kv)


def _trunk(x, start, past_k, past_v, ssm_re, ssm_im, conv_mix_prev, conv_ffn_prev, mem_k, mem_v, w, s5, cfg):
    bsz, seq, _ = x.shape
    tm, tq, qg = cfg["tm"], cfg["tq"], cfg["qg"]
    rows, seg, nsteps = cfg["rows"], cfg["seg"], cfg["nsteps"]
    tables = _rope_tables(seq, start, tm)
    u, q, k, v = _inproj_even(x, w["norm_mix"][0:1], w["w_in_even"], tables, 0, tm)
    h0 = jnp.concatenate([ssm_re[0].reshape(bsz, S5_STATES), ssm_im[0].reshape(bsz, S5_STATES)], axis=1)
    if cfg["one_block"]:
        ya, h_all = _s5_mixer(u.reshape(1, bsz * seq, SSM_WIDTH), jnp.repeat(h0, seg, axis=0)[None],
                              s5["wg"], s5["wy"], cfg["pow2"], cfg["powr"], w["ssm_d"], w["w_glu"], 0,
                              rows, seg, nsteps)
        ya = ya.reshape(bsz, seq, SSM_WIDTH)
        h_last = h_all[0, seg - 1::seg]
    else:
        ya, h_all = _s5_mixer(u, h0[:, None, :], s5["wg"], s5["wy"], cfg["pow2"], cfg["powr"], w["ssm_d"],
                              w["w_glu"], 0, rows, seg, nsteps)
        h_last = h_all[:, 0]
    new_re = h_last[:, :S5_STATES].reshape(1, bsz, SSM_GROUPS, SSM_STATE)
    new_im = h_last[:, S5_STATES:].reshape(1, bsz, SSM_GROUPS, SSM_STATE)
    yb = _swa(q, k, v, past_k[0].reshape(bsz, WINDOW, KV_WIDTH), past_v[0].reshape(bsz, WINDOW, KV_WIDTH),
              w["attn_sinks"][0], start, tq, qg)
    x = _mix_even_xattn(x, ya, yb, w["w_out_even"], w["norm_xattn"][0:1], w["xattn_wq"], mem_k, mem_v,
                        w["xattn_wo"], 0, 0, tm)
    x, cf0 = _ffn(x, w["norm_ffn"][0:1], w["ffn_w_up"], w["ffn_conv_w"], w["ffn_w_down"], conv_ffn_prev[0],
                  w["norm_final"], 0, tm, False)
    x, cm = _mix_odd_xattn(x, w["norm_mix"][1:2], w["w_in_odd"], w["conv_mix_w"], w["w_out_odd"], conv_mix_prev[0],
                           w["norm_xattn"][1:2], w["xattn_wq"], mem_k, mem_v, w["xattn_wo"], 1, 0, tm)
    y, cf1 = _ffn(x, w["norm_ffn"][1:2], w["ffn_w_up"], w["ffn_conv_w"], w["ffn_w_down"], conv_ffn_prev[1],
                  w["norm_final"], 1, tm, True)
    keep = min(seq, WINDOW)
    new_k = k[:, seq - keep:].reshape(1, bsz, keep, N_KV_HEADS, HEAD_DIM)
    new_v = v[:, seq - keep:].reshape(1, bsz, keep, N_KV_HEADS, HEAD_DIM)
    return y, new_k, new_v, new_re, new_im, cm[None], jnp.stack([cf0, cf1])


def kernel(x_prompt, x_sample, mem_prompt, cache_win_k, cache_win_v, state_ssm_re, state_ssm_im, state_conv_mix, state_conv_ffn, cache_mem_k, cache_mem_v, norm_mix, norm_xattn, norm_ffn, norm_final, w_in_even, w_out_even, ssm_a_re, ssm_a_im, ssm_log_dt, ssm_b_re, ssm_b_im, ssm_c_re, ssm_c_im, ssm_d, w_glu, attn_sinks, w_in_odd, conv_mix_w, w_out_odd, xattn_wq, xattn_wkv, xattn_wo, ffn_w_up, ffn_conv_w, ffn_w_down):
    bp, seq_p, _ = x_prompt.shape
    bs, seq_s, _ = x_sample.shape
    w = dict(norm_mix=norm_mix, norm_xattn=norm_xattn, norm_ffn=norm_ffn, norm_final=norm_final.reshape(1, D_MODEL),
             w_in_even=w_in_even.astype(BF16), w_out_even=w_out_even.astype(BF16),
             ssm_d=ssm_d[0].reshape(1, SSM_WIDTH), w_glu=w_glu.astype(BF16), attn_sinks=attn_sinks,
             w_in_odd=w_in_odd.astype(BF16), conv_mix_w=conv_mix_w, w_out_odd=w_out_odd.astype(BF16),
             xattn_wq=xattn_wq.astype(BF16), xattn_wo=xattn_wo.astype(BF16),
             ffn_w_up=ffn_w_up.astype(BF16), ffn_conv_w=ffn_conv_w, ffn_w_down=ffn_w_down.astype(BF16))

    wg, wy = _s5_prep(ssm_a_re[0], ssm_a_im[0], ssm_log_dt[0], ssm_b_re[0], ssm_b_im[0], ssm_c_re[0], ssm_c_im[0])
    s5 = dict(wg=wg, wy=wy)
    rows_p = seg_p = S5_ROWS
    seg_s = seq_s // S5_T
    rows_s = seg_s * bs
    steps_p, steps_s = int(math.log2(seg_p)), int(math.log2(seg_s))
    pad = lambda n: [1.0] * (-n % SUBLANES)
    exps = ([float(S5_T << m) for m in range(steps_p)] + pad(steps_p)
            + [float(S5_T * (r + 1)) for r in range(rows_p)]
            + [float(S5_T << m) for m in range(steps_s)] + pad(steps_s)
            + [float(S5_T * (r % seg_s + 1)) for r in range(rows_s)])
    pows = _s5_pow_table(exps, ssm_a_re[0], ssm_a_im[0], ssm_log_dt[0])
    o1 = steps_p + len(pad(steps_p))
    o2 = o1 + rows_p
    o3 = o2 + steps_s + len(pad(steps_s))
    cfg_p = dict(tm=512, tq=256, qg=128, rows=rows_p, seg=seg_p, nsteps=steps_p, one_block=False,
                 pow2=pows[:o1], powr=pows[o1:o2])
    cfg_s = dict(tm=seq_s, tq=seq_s, qg=seq_s, rows=rows_s, seg=seg_s, nsteps=steps_s, one_block=True,
                 pow2=pows[o2:o3], powr=pows[o3:])

    mem_k_p, mem_v_p = _mem_kv(mem_prompt, xattn_wkv.astype(BF16))
    cw = CONV_WIDTH - 1
    zk = jnp.zeros((1, bp, WINDOW, N_KV_HEADS, HEAD_DIM), F32)
    zs = jnp.zeros((1, bp, SSM_GROUPS, SSM_STATE), F32)
    zcm = jnp.zeros((1, bp, cw, D_MODEL), F32)
    zcf = jnp.zeros((DEPTH, bp, cw, D_FF), F32)
    y_p, k_p, v_p, re_p, im_p, cm_p, cf_p = _trunk(x_prompt, 0, zk, zk, zs, zs, zcm, zcf, mem_k_p, mem_v_p, w, s5, cfg_p)
    y_s, k_s, v_s, re_s, im_s, cm_s, cf_s = _trunk(x_sample, PAST_LEN, cache_win_k, cache_win_v, state_ssm_re,
                                                   state_ssm_im, state_conv_mix, state_conv_ffn, cache_mem_k,
                                                   cache_mem_v, w, s5, cfg_s)
    return (y_p, y_s, k_p, v_p, re_p, im_p, cm_p, cf_p, mem_k_p, mem_v_p, k_s, v_s, re_s, im_s, cm_s, cf_s)
```

```python
import functools
import math

import numpy as np
import jax
import jax.numpy as jnp
from jax import lax
from jax.experimental import pallas as pl
from jax.experimental.pallas import tpu as pltpu

F32 = jnp.float32
BF16 = jnp.bfloat16

D_MODEL = 1024
DEPTH = 2
PAST_LEN = 2048
CHUNK = 64
SSM_WIDTH = 512
SSM_GROUP = 16
SSM_GROUPS = 32
SSM_STATE = 64
HEAD_DIM = 64
N_HEADS = 8
N_KV_HEADS = 2
GQ = N_HEADS // N_KV_HEADS
WINDOW = 128
ROPE_THETA = 10000.0
KV_WIDTH = N_KV_HEADS * HEAD_DIM
EVEN_IN = SSM_WIDTH + N_HEADS * HEAD_DIM + 2 * KV_WIDTH
CONV_WIDTH = 3
N_MEM = 256
X_HEADS = 4
X_HEAD_DIM = 256
D_FF = 2816
EPS = 1e-6
NEG = -1e30

LANES = 128
SUBLANES = 8
S5_T = 8
S5_TILES = SSM_WIDTH // LANES
S5_TILE_STATES = (LANES // SSM_GROUP) * SSM_STATE
S5_STATES = SSM_GROUPS * SSM_STATE
S5_SEG_CHUNKS = 16
MXU_DIM = 256
FF_CHUNK = 4 * MXU_DIM
VMEM_LIMIT = 56 * 1024 * 1024


def _const_spec(shape, index=None):
    idx = tuple(index) if index is not None else (0,) * len(shape)
    return pl.BlockSpec(shape, lambda *_: idx, pipeline_mode=pl.Buffered(1))


def _params(n_axes):
    return pltpu.CompilerParams(dimension_semantics=("arbitrary",) * n_axes,
                                vmem_limit_bytes=VMEM_LIMIT)


def _rms(x, g):
    return x * lax.rsqrt(jnp.mean(x * x, axis=-1, keepdims=True) + EPS) * g


def _dot(a, b):
    return jnp.dot(a, b, preferred_element_type=F32)


def _dot_t(a, b):
    return lax.dot_general(a, b, (((1,), (1,)), ((), ())), preferred_element_type=F32)


def _rope_table_body(cos_ref, sina_ref, sinb_ref, *, start, tl):
    i = pl.program_id(0)
    row = lax.broadcasted_iota(jnp.int32, (tl, LANES), 0)
    lane = lax.broadcasted_iota(jnp.int32, (tl, LANES), 1)
    pos = (start + i * tl + row).astype(F32)
    half = HEAD_DIM // 2
    freq = (lane[0:1] & (half - 1)).astype(F32)
    inv = jnp.exp(freq * (-2.0 / HEAD_DIM * math.log(ROPE_THETA)))
    ang = pos * inv
    c = jnp.cos(ang)
    s = jnp.sin(ang)
    first = (lane & (HEAD_DIM - 1)) < HEAD_DIM // 2
    cos_ref[...] = c
    sina_ref[...] = jnp.where(first, -s, 0.0)
    sinb_ref[...] = jnp.where(first, 0.0, s)


def _rope_tables(seq, start, tl):
    spec = pl.BlockSpec((tl, LANES), lambda i: (i, 0))
    return pl.pallas_call(
        functools.partial(_rope_table_body, start=start, tl=tl),
        grid=(seq // tl,),
        in_specs=[],
        out_specs=[spec, spec, spec],
        out_shape=[jax.ShapeDtypeStruct((seq, LANES), F32)] * 3,
        compiler_params=_params(1),
        name="rope_tables",
    )()


def _rope(t, cos, sina, sinb):
    n = t.shape[1]
    half = HEAD_DIM // 2
    return t * cos + pltpu.roll(t, n - half, 1) * sina + pltpu.roll(t, half, 1) * sinb


def _inproj_even_body(x_ref, g_ref, w_ref, cos_ref, sa_ref, sb_ref, u_ref, q_ref, k_ref, v_ref):
    h = _rms(x_ref[0], g_ref[...]).astype(BF16)
    proj = _dot(h, w_ref[0])
    o0 = SSM_WIDTH
    o1 = o0 + N_HEADS * HEAD_DIM
    o2 = o1 + KV_WIDTH
    cos, sa, sb = cos_ref[...], sa_ref[...], sb_ref[...]
    rep = (o1 - o0) // LANES
    cos_q = jnp.concatenate([cos] * rep, axis=1)
    sa_q = jnp.concatenate([sa] * rep, axis=1)
    sb_q = jnp.concatenate([sb] * rep, axis=1)
    u_ref[0] = proj[:, :o0]
    q_ref[0] = _rope(proj[:, o0:o1], cos_q, sa_q, sb_q).astype(BF16)
    k_ref[0] = _rope(proj[:, o1:o2], cos, sa, sb)
    v_ref[0] = proj[:, o2:]


def _inproj_even(x, g, w_in, tables, layer_e, tm):
    bsz, seq, _ = x.shape
    cos, sa, sb = tables
    tok = lambda w: pl.BlockSpec((1, tm, w), lambda b, i: (b, i, 0))
    tab = pl.BlockSpec((tm, LANES), lambda b, i: (i, 0))
    return pl.pallas_call(
        _inproj_even_body,
        grid=(bsz, seq // tm),
        in_specs=[tok(D_MODEL), _const_spec((1, D_MODEL)),
                  _const_spec((1, D_MODEL, EVEN_IN), (layer_e, 0, 0)), tab, tab, tab],
        out_specs=[tok(SSM_WIDTH), tok(N_HEADS * HEAD_DIM), tok(KV_WIDTH), tok(KV_WIDTH)],
        out_shape=[jax.ShapeDtypeStruct((bsz, seq, SSM_WIDTH), F32),
                   jax.ShapeDtypeStruct((bsz, seq, N_HEADS * HEAD_DIM), BF16),
                   jax.ShapeDtypeStruct((bsz, seq, KV_WIDTH), F32),
                   jax.ShapeDtypeStruct((bsz, seq, KV_WIDTH), F32)],
        compiler_params=_params(2),
        name="inproj_even",
    )(x, g, w_in, cos, sa, sb)


def _s5_prep_body(lr_ref, li_ref, ld_ref, bre_ref, bim_ref, lrc_ref, lic_ref, ldc_ref,
                  cre_ref, cim_ref, wg_ref, wy_ref):
    ts = S5_TILE_STATES
    tw = S5_T * LANES

    def step(z_r, z_i):
        mag = jnp.exp(z_r)
        a_r, a_i = mag * jnp.cos(z_i), mag * jnp.sin(z_i)
        return a_r, a_i, lambda p: (p[0] * a_r - p[1] * a_i, p[0] * a_i + p[1] * a_r)

    lr, li = lr_ref[0], li_ref[0]
    dt = jnp.exp(ld_ref[0])
    ar, ai, mul_a = step(lr * dt, li * dt)
    nrm = lr * lr + li * li
    cbr = ((ar - 1.0) * lr + ai * li) / nrm
    cbi = (ai * lr - (ar - 1.0) * li) / nrm
    b_re, b_im = bre_ref[0], bim_ref[0]
    bbr = cbr * b_re - cbi * b_im
    bbi = cbr * b_im + cbi * b_re
    p = (jnp.ones_like(ar), jnp.zeros_like(ar))
    for s in reversed(range(S5_T)):
        rows = slice(s * LANES, (s + 1) * LANES)
        wg_ref[0, rows, :ts] = (p[0] * bbr - p[1] * bbi).astype(BF16)
        wg_ref[0, rows, ts:] = (p[0] * bbi + p[1] * bbr).astype(BF16)
        for t in range(s):
            wy_ref[0, rows, t * LANES:(t + 1) * LANES] = jnp.zeros((LANES, LANES), BF16)
        p = mul_a(p)
    dtc = jnp.exp(ldc_ref[0])
    _, _, mul_ac = step(lrc_ref[0] * dtc, lic_ref[0] * dtc)
    c_re, c_im = cre_ref[0], cim_ref[0]
    bb = jnp.concatenate([bbr, bbi], axis=1).astype(BF16)
    pc = (jnp.ones_like(dtc), jnp.zeros_like(dtc))
    for k in range(S5_T + 1):
        blk = jnp.concatenate([pc[0] * c_re - pc[1] * c_im, -(pc[0] * c_im + pc[1] * c_re)], axis=0).astype(BF16)
        if k >= 1:
            wy_ref[0, tw:, (k - 1) * LANES:k * LANES] = blk
        if k < S5_T:
            lag = _dot(bb, blk).astype(BF16)
            for s in range(S5_T - k):
                t = s + k
                wy_ref[0, s * LANES:(s + 1) * LANES, t * LANES:(t + 1) * LANES] = lag
        pc = mul_ac(pc)


def _s5_prep(a_re, a_im, log_dt, b_re, b_im, c_re, c_im):
    gpt = LANES // SSM_GROUP
    eye = jnp.eye(gpt, dtype=bool)

    def rows(p):
        return p.reshape(S5_TILES, 1, S5_TILE_STATES)

    def cols(p):
        return jnp.broadcast_to(p.reshape(S5_TILES, S5_TILE_STATES, 1), (S5_TILES, S5_TILE_STATES, LANES))

    def bmat(b):
        bt = b.reshape(S5_TILES, gpt, SSM_STATE, SSM_GROUP).transpose(0, 1, 3, 2)
        full = jnp.where(eye[None, :, None, :, None], bt[:, :, :, None, :], 0.0)
        return full.reshape(S5_TILES, LANES, S5_TILE_STATES)

    def cmat(c):
        ct = c.reshape(S5_TILES, gpt, SSM_GROUP, SSM_STATE).transpose(0, 1, 3, 2)
        full = jnp.where(eye[None, :, None, :, None], ct[:, :, :, None, :], 0.0)
        return full.reshape(S5_TILES, S5_TILE_STATES, LANES)

    row_spec = pl.BlockSpec((1, 1, S5_TILE_STATES), lambda j: (j, 0, 0))
    b_spec = pl.BlockSpec((1, LANES, S5_TILE_STATES), lambda j: (j, 0, 0))
    c_spec = pl.BlockSpec((1, S5_TILE_STATES, LANES), lambda j: (j, 0, 0))
    tw = S5_T * LANES
    return pl.pallas_call(
        _s5_prep_body,
        grid=(S5_TILES,),
        in_specs=[row_spec] * 3 + [b_spec] * 2 + [c_spec] * 5,
        out_specs=[pl.BlockSpec((1, tw, 2 * S5_TILE_STATES), lambda j: (j, 0, 0)),
                   pl.BlockSpec((1, tw + 2 * S5_TILE_STATES, tw), lambda j: (j, 0, 0))],
        out_shape=[jax.ShapeDtypeStruct((S5_TILES, tw, 2 * S5_TILE_STATES), BF16),
                   jax.ShapeDtypeStruct((S5_TILES, tw + 2 * S5_TILE_STATES, tw), BF16)],
        compiler_params=_params(1),
        name="s5_prep",
    )(rows(a_re), rows(a_im), rows(log_dt), bmat(b_re), bmat(b_im),
      cols(a_re), cols(a_im), cols(log_dt), cmat(c_re), cmat(c_im))


def _s5_pow_body(ex_ref, lr_ref, li_ref, ld_ref, o_ref):
    dt = jnp.exp(ld_ref[...])
    zr, zi = lr_ref[...] * dt, li_ref[...] * dt
    ex = ex_ref[...]
    mag = jnp.exp(ex * zr)
    o_ref[:, :S5_STATES] = mag * jnp.cos(ex * zi)
    o_ref[:, S5_STATES:] = mag * jnp.sin(ex * zi)


def _s5_pow_table(exponents, a_re, a_im, log_dt):
    n = len(exponents)
    ex = jnp.asarray(np.asarray(exponents, np.float32)[:, None])
    flat = lambda p: p.reshape(1, S5_STATES)
    return pl.pallas_call(
        _s5_pow_body,
        grid=(1,),
        in_specs=[pl.BlockSpec((n, 1), lambda i: (0, 0))] + [pl.BlockSpec((1, S5_STATES), lambda i: (0, 0))] * 3,
        out_specs=pl.BlockSpec((n, 2 * S5_STATES), lambda i: (0, 0)),
        out_shape=jax.ShapeDtypeStruct((n, 2 * S5_STATES), F32),
        compiler_params=_params(1),
        name="s5_pow_table",
    )(ex, flat(a_re), flat(a_im), flat(log_dt))


def _s5_body(u_ref, h0_ref, wg_ref, wy_ref, pow2_ref, powr_ref, d_ref, wglu_ref,
             y_ref, hout_ref, carry_ref, us_ref, ys_ref, *, ni, chained):
    ns = S5_STATES
    ts = S5_TILE_STATES
    segtok = ni * S5_T
    pitch = _s5_pitch(ni)

    @pl.when(pl.program_id(1) == 0)
    def _():
        carry_ref[...] = h0_ref[0]

    for j in range(S5_TILES):
        for s in range(SUBLANES):
            us_ref[j, s * pitch:s * pitch + segtok] = u_ref[0, s * segtok:(s + 1) * segtok, j * LANES:(j + 1) * LANES]

    def gather(j, t):
        return jnp.concatenate([us_ref[j, pl.ds(S5_T * i + t, SUBLANES, stride=pitch), :]
                                for i in range(ni)], axis=0)

    def fma(p_r, p_i, x_r, x_i, y_r, y_i):
        return y_r + p_r * x_r - p_i * x_i, y_i + p_r * x_i + p_i * x_r

    xs, g_re, g_im = [], [], []
    for j in range(S5_TILES):
        xj = jnp.concatenate([gather(j, s) for s in range(S5_T)], axis=1).astype(BF16)
        g = _dot(xj, wg_ref[j])
        xs.append(xj)
        g_re.append(g[:, :ts])
        g_im.append(g[:, ts:])
    g_re = jnp.concatenate(g_re, axis=1)
    g_im = jnp.concatenate(g_im, axis=1)
    slab = lambda a, i: a[i * SUBLANES:(i + 1) * SUBLANES]
    power = lambda ref, r: (ref[r:r + 1, :ns], ref[r:r + 1, ns:])
    a_r, a_i = power(powr_ref, 0)
    cin = carry_ref[...]
    h_r, h_i = cin[:, :ns], cin[:, ns:]
    prev = []
    if chained:
        local = [(slab(g_re, 0), slab(g_im, 0))]
        for i in range(1, ni):
            local.append(fma(a_r, a_i, *local[-1], slab(g_re, i), slab(g_im, i)))
        sub = lax.broadcasted_iota(jnp.int32, (SUBLANES, 1), 0)
        c_r = jnp.where(sub == 0, h_r, pltpu.roll(local[-1][0], 1, 0))
        c_i = jnp.where(sub == 0, h_i, pltpu.roll(local[-1][1], 1, 0))
        for m in range(3):
            k = 1 << m
            s_r = jnp.where(sub >= k, pltpu.roll(c_r, k, 0), 0.0)
            s_i = jnp.where(sub >= k, pltpu.roll(c_i, k, 0), 0.0)
            c_r, c_i = fma(*power(pow2_ref, m), s_r, s_i, c_r, c_i)
        h_r, h_i = c_r, c_i
        for i in range(ni):
            prev.append((h_r, h_i))
            h_r, h_i = fma(*power(powr_ref, i), c_r, c_i, *local[i])
        last = jnp.concatenate([h_r[SUBLANES - 1:], h_i[SUBLANES - 1:]], axis=1)
    else:
        for i in range(ni):
            prev.append((h_r, h_i))
            h_r, h_i = fma(a_r, a_i, h_r, h_i, slab(g_re, i), slab(g_im, i))
        last = jnp.concatenate([h_r, h_i], axis=1)
    carry_ref[...] = last
    hout_ref[0] = last
    prev_re = jnp.concatenate([p[0] for p in prev], axis=0)
    prev_im = jnp.concatenate([p[1] for p in prev], axis=0)
    y_tiles = []
    for j in range(S5_TILES):
        lhs = jnp.concatenate([xs[j], prev_re[:, j * ts:(j + 1) * ts].astype(BF16),
                               prev_im[:, j * ts:(j + 1) * ts].astype(BF16)], axis=1)
        y_tiles.append(_dot(lhs, wy_ref[j]))
    d = d_ref[...]
    wglu = wglu_ref[0]
    for t in range(S5_T):
        ut = jnp.concatenate([gather(j, t) for j in range(S5_TILES)], axis=1)
        yt = jnp.concatenate([y_tiles[j][:, t * LANES:(t + 1) * LANES] for j in range(S5_TILES)], axis=1)
        g = jax.nn.gelu(yt + d * ut)
        o = g * jax.nn.sigmoid(_dot(g.astype(BF16), wglu))
        for j in range(S5_TILES):
            for i in range(ni):
                ys_ref[j, pl.ds(S5_T * i + t, SUBLANES, stride=pitch), :] = slab(o, i)[:, j * LANES:(j + 1) * LANES]
    for j in range(S5_TILES):
        for s in range(SUBLANES):
            y_ref[0, s * segtok:(s + 1) * segtok, j * LANES:(j + 1) * LANES] = (
                ys_ref[j, s * pitch:s * pitch + segtok].astype(BF16))


def _s5_pitch(ni):
    segtok = ni * S5_T
    return segtok + SUBLANES if (segtok // SUBLANES) % 2 == 0 else segtok


def _s5_mixer(u, h0, wg, wy, pow2, powr, d, w_glu, layer_e, ni, chained):
    nb, ntok, width = u.shape
    tb = SUBLANES * ni * S5_T
    hrows = h0.shape[1]
    blk = pl.BlockSpec((1, tb, width), lambda b, i: (b, i, 0))
    state = pl.BlockSpec((1, hrows, 2 * S5_STATES), lambda b, i: (b, 0, 0))
    return pl.pallas_call(
        functools.partial(_s5_body, ni=ni, chained=chained),
        grid=(nb, ntok // tb),
        in_specs=[blk, state, _const_spec(wg.shape), _const_spec(wy.shape),
                  _const_spec(pow2.shape), _const_spec(powr.shape), _const_spec((1, SSM_WIDTH)),
                  _const_spec((1, SSM_WIDTH, SSM_WIDTH), (layer_e, 0, 0))],
        out_specs=[blk, state],
        out_shape=[jax.ShapeDtypeStruct((nb, ntok, width), BF16),
                   jax.ShapeDtypeStruct((nb, hrows, 2 * S5_STATES), F32)],
        scratch_shapes=[pltpu.VMEM((hrows, 2 * S5_STATES), F32)]
        + [pltpu.VMEM((S5_TILES, SUBLANES * _s5_pitch(ni), LANES), F32)] * 2,
        compiler_params=_params(2),
        name="s5_mixer",
    )(u, h0, wg, wy, pow2, powr, d, w_glu)


def _swa_body(sink_ref, q_ref, kc_ref, vc_ref, kp_ref, vp_ref, pk_ref, pv_ref, o_ref, *, tq, qg, start):
    first = pl.program_id(1) == 0
    k_prev = jnp.where(first, pk_ref[0], kp_ref[0])
    v_prev = jnp.where(first, pv_ref[0], vp_ref[0])
    k_all = jnp.concatenate([k_prev, kc_ref[0]], axis=0)
    v_all = jnp.concatenate([v_prev, vc_ref[0]], axis=0)
    nk = WINDOW + qg
    lane = lax.broadcasted_iota(jnp.int32, (1, KV_WIDTH), 1)
    slot = lax.broadcasted_iota(jnp.int32, (1, nk), 1)
    qrow = lax.broadcasted_iota(jnp.int32, (2 * qg, 1), 0)
    shift = CHUNK.bit_length() - 1
    q_chunk = (qrow & (qg - 1)) >> shift
    k_chunk = (slot >> shift) - WINDOW // CHUNK
    visible = (k_chunk <= q_chunk) & (k_chunk >= q_chunk - WINDOW // CHUNK)
    cached_ok = slot >= jnp.where(first, WINDOW - start, 0)
    scale = HEAD_DIM ** -0.5
    placed = []
    for kh in range(N_KV_HEADS):
        own = (lane >> (HEAD_DIM.bit_length() - 1)) == kh
        kz, vz = jnp.where(own, k_all, 0.0), jnp.where(own, v_all, 0.0)
        kr, vr = pltpu.roll(kz, HEAD_DIM, 1), pltpu.roll(vz, HEAD_DIM, 1)
        k_lo, k_hi = (kz, kr) if kh == 0 else (kr, kz)
        v_lo, v_hi = (vz, vr) if kh == 0 else (vr, vz)
        placed.append(((k_lo.astype(BF16), v_lo.astype(BF16)), (k_hi.astype(BF16), v_hi.astype(BF16))))
    for gi in range(tq // qg):
        mask = visible & cached_ok if gi == 0 else visible
        key_rows = slice(gi * qg, gi * qg + nk)
        qgrp = q_ref[0, gi * qg:(gi + 1) * qg, :]
        outs = []
        for kh in range(N_KV_HEADS):
            base = kh * GQ * HEAD_DIM
            qs = jnp.concatenate([qgrp[:, base:base + LANES], qgrp[:, base + LANES:base + 2 * LANES]], axis=0)
            acc = None
            for par, (k_pl, v_pl) in enumerate(placed[kh]):
                kk, vv = k_pl[key_rows], v_pl[key_rows]
                s = jnp.where(mask, _dot_t(qs, kk) * scale, NEG)
                sink = jnp.where(qrow < qg, sink_ref[kh * GQ + par], sink_ref[kh * GQ + 2 + par])
                m = jnp.maximum(jnp.max(s, axis=1, keepdims=True), sink)
                p = jnp.exp(s - m)
                den = jnp.sum(p, axis=1, keepdims=True) + jnp.exp(sink - m)
                o = _dot(p.astype(BF16), vv) / den
                acc = o if acc is None else acc + o
            outs += [acc[:qg], acc[qg:]]
        o_ref[0, gi * qg:(gi + 1) * qg, :] = jnp.concatenate(outs, axis=1).astype(BF16)


def _swa(q, k, v, past_k, past_v, sinks, start, tq, qg):
    bsz, seq, _ = q.shape
    ntiles = seq // tq
    tok = lambda w: pl.BlockSpec((1, tq, w), lambda b, i: (b, i, 0))
    past = pl.BlockSpec((1, WINDOW, KV_WIDTH), lambda b, i: (b, 0, 0))
    if ntiles > 1:
        per = tq // WINDOW
        prev = pl.BlockSpec((1, WINDOW, KV_WIDTH), lambda b, i: (b, jnp.maximum(i * per - 1, 0), 0))
        k_prev, v_prev = k, v
    else:
        prev, k_prev, v_prev = past, past_k, past_v
    return pl.pallas_call(
        functools.partial(_swa_body, tq=tq, qg=qg, start=start),
        grid=(bsz, ntiles),
        in_specs=[pl.BlockSpec(memory_space=pltpu.SMEM), tok(N_HEADS * HEAD_DIM), tok(KV_WIDTH), tok(KV_WIDTH),
                  prev, prev, past, past],
        out_specs=tok(N_HEADS * HEAD_DIM),
        out_shape=jax.ShapeDtypeStruct((bsz, seq, N_HEADS * HEAD_DIM), BF16),
        compiler_params=_params(2),
        name="swa",
    )(sinks, q, k, v, k_prev, v_prev, past_k, past_v)


def _xattn(x1, g, wq, mk_ref, mv_ref, wo):
    h = _rms(x1, g).astype(BF16)
    q = _dot(h, wq).astype(BF16)
    scale = X_HEAD_DIM ** -0.5
    outs = []
    for hd in range(X_HEADS):
        cols = slice(hd * X_HEAD_DIM, (hd + 1) * X_HEAD_DIM)
        s = _dot_t(q[:, cols], mk_ref[0, 0, :, cols]) * scale
        p = jnp.exp(s - jnp.max(s, axis=1, keepdims=True))
        den = jnp.sum(p, axis=1, keepdims=True)
        outs.append(_dot(p.astype(BF16), mv_ref[0, 0, :, cols]) / den)
    o = jnp.concatenate(outs, axis=1).astype(BF16)
    return x1 + _dot(o, wo)


def _conv3(cur, carry, w):
    n = cur.shape[0]
    row = lax.broadcasted_iota(jnp.int32, (n, 1), 0)
    c1, c2 = carry[SUBLANES - 1:SUBLANES], carry[SUBLANES - 2:SUBLANES - 1]
    m1 = jnp.where(row == 0, c1, pltpu.roll(cur, 1, 0))
    m2 = jnp.where(row == 0, c2, jnp.where(row == 1, c1, pltpu.roll(cur, 2, 0)))
    return w[0:1] * m2 + w[1:2] * m1 + w[2:3] * cur


def _mix_even_xattn_body(x_ref, ya_ref, yb_ref, woa_ref, wob_ref, g_ref, wq_ref, mk_ref, mv_ref, wo_ref, o_ref):
    x1 = x_ref[0] + _dot(ya_ref[0], woa_ref[0]) + _dot(yb_ref[0], wob_ref[0])
    o_ref[0] = _xattn(x1, g_ref[...], wq_ref[0], mk_ref, mv_ref, wo_ref[0])


def _mix_even_xattn(x, ya, yb, w_out, g, wq, mem_k, mem_v, wo, layer, layer_e, tm):
    bsz, seq, _ = x.shape
    tok = lambda w: pl.BlockSpec((1, tm, w), lambda b, i: (b, i, 0))
    mem = pl.BlockSpec((1, 1, N_MEM, D_MODEL), lambda b, i: (layer, b, 0, 0))
    half = D_MODEL // 2
    sq = lambda: _const_spec((1, D_MODEL, D_MODEL), (layer, 0, 0))
    return pl.pallas_call(
        _mix_even_xattn_body,
        grid=(bsz, seq // tm),
        in_specs=[tok(D_MODEL), tok(half), tok(half),
                  _const_spec((1, half, D_MODEL), (layer_e, 0, 0)), _const_spec((1, half, D_MODEL), (layer_e, 1, 0)),
                  _const_spec((1, D_MODEL)), sq(), mem, mem, sq()],
        out_specs=tok(D_MODEL),
        out_shape=jax.ShapeDtypeStruct(x.shape, F32),
        compiler_params=_params(2),
        name="mix_even_xattn",
    )(x, ya, yb, w_out, w_out, g, wq, mem_k, mem_v, wo)


def _mix_odd_xattn_body(x_ref, gm_ref, win_ref, cw_ref, wout_ref, prev_ref, g_ref, wq_ref, mk_ref, mv_ref, wo_ref,
                        o_ref, st_ref, carry_ref, *, tm):
    @pl.when(pl.program_id(1) == 0)
    def _():
        carry_ref[SUBLANES - 2:SUBLANES, :] = prev_ref[0]

    x = x_ref[0]
    proj = _dot(_rms(x, gm_ref[...]).astype(BF16), win_ref[0])
    gate_b, gate_c, z = proj[:, :D_MODEL], proj[:, D_MODEL:2 * D_MODEL], proj[:, 2 * D_MODEL:]
    cz = gate_c * z
    z_conv = _conv3(cz, carry_ref[...], cw_ref[0])
    carry_ref[...] = cz[tm - SUBLANES:]
    st_ref[0] = cz[tm - (CONV_WIDTH - 1):]
    x1 = x + _dot((gate_b * z_conv).astype(BF16), wout_ref[0])
    o_ref[0] = _xattn(x1, g_ref[...], wq_ref[0], mk_ref, mv_ref, wo_ref[0])


def _mix_odd_xattn(x, gm, w_in, conv_w, w_out, prev, g, wq, mem_k, mem_v, wo, layer, layer_o, tm):
    bsz, seq, _ = x.shape
    tok = pl.BlockSpec((1, tm, D_MODEL), lambda b, i: (b, i, 0))
    mem = pl.BlockSpec((1, 1, N_MEM, D_MODEL), lambda b, i: (layer, b, 0, 0))
    st = pl.BlockSpec((1, CONV_WIDTH - 1, D_MODEL), lambda b, i: (b, 0, 0))
    sq = lambda l: _const_spec((1, D_MODEL, D_MODEL), (l, 0, 0))
    return pl.pallas_call(
        functools.partial(_mix_odd_xattn_body, tm=tm),
        grid=(bsz, seq // tm),
        in_specs=[tok, _const_spec((1, D_MODEL)), _const_spec((1, D_MODEL, 3 * D_MODEL), (layer_o, 0, 0)),
                  _const_spec((1, CONV_WIDTH, D_MODEL), (layer_o, 0, 0)), sq(layer_o), st,
                  _const_spec((1, D_MODEL)), sq(layer), mem, mem, sq(layer)],
        out_specs=[tok, st],
        out_shape=[jax.ShapeDtypeStruct(x.shape, F32),
                   jax.ShapeDtypeStruct((bsz, CONV_WIDTH - 1, D_MODEL), F32)],
        scratch_shapes=[pltpu.VMEM((SUBLANES, D_MODEL), F32)],
        compiler_params=_params(2),
        name="mix_odd_xattn",
    )(x, gm, w_in, conv_w, w_out, prev, g, wq, mem_k, mem_v, wo)


def _ffn_body(x_ref, g_ref, wup_ref, cw_ref, wdn_ref, prev_ref, gfin_ref, o_ref, st_ref, carry_ref, *, tm, final):
    @pl.when(pl.program_id(1) == 0)
    def _():
        carry_ref[SUBLANES - 2:SUBLANES, :] = prev_ref[0]

    x = x_ref[0]
    h = _rms(x, g_ref[...]).astype(BF16)
    acc = x
    for lo in range(0, D_FF, FF_CHUNK):
        cols = slice(lo, min(lo + FF_CHUNK, D_FF))
        gate = _dot(h, wup_ref[0, :, cols])
        val = _dot(h, wup_ref[0, :, D_FF + cols.start:D_FF + cols.stop])
        gate_c = _conv3(gate, carry_ref[:, cols], cw_ref[0, :, cols])
        carry_ref[:, cols] = gate[tm - SUBLANES:]
        st_ref[0, :, cols] = gate[tm - (CONV_WIDTH - 1):]
        act = gate_c * jax.nn.sigmoid(gate_c) * val
        acc = acc + _dot(act.astype(BF16), wdn_ref[0, cols, :])
    o_ref[0] = _rms(acc, gfin_ref[...]) if final else acc


def _ffn(x, g, w_up, conv_w, w_down, prev, g_final, layer, tm, final):
    bsz, seq, _ = x.shape
    tok = pl.BlockSpec((1, tm, D_MODEL), lambda b, i: (b, i, 0))
    st = pl.BlockSpec((1, CONV_WIDTH - 1, D_FF), lambda b, i: (b, 0, 0))
    return pl.pallas_call(
        functools.partial(_ffn_body, tm=tm, final=final),
        grid=(bsz, seq // tm),
        in_specs=[tok, _const_spec((1, D_MODEL)), _const_spec((1, D_MODEL, 2 * D_FF), (layer, 0, 0)),
                  _const_spec((1, CONV_WIDTH, D_FF), (layer, 0, 0)), _const_spec((1, D_FF, D_MODEL), (layer, 0, 0)),
                  st, _const_spec((1, D_MODEL))],
        out_specs=[tok, st],
        out_shape=[jax.ShapeDtypeStruct(x.shape, F32),
                   jax.ShapeDtypeStruct((bsz, CONV_WIDTH - 1, D_FF), F32)],
        scratch_shapes=[pltpu.VMEM((SUBLANES, D_FF), F32)],
        compiler_params=_params(2),
        name="conv_ffn",
    )(x, g, w_up, conv_w, w_down, prev, g_final)


def _mem_kv_body(m_ref, w_ref, k_ref, v_ref, kf_ref, vf_ref):
    kv = _dot(m_ref[0].astype(BF16), w_ref[0])
    kf_ref[0, 0] = kv[:, :D_MODEL].astype(BF16)
    vf_ref[0, 0] = kv[:, D_MODEL:].astype(BF16)
    for hd in range(X_HEADS):
        k_ref[0, 0, :, hd, :] = kv[:, hd * X_HEAD_DIM:(hd + 1) * X_HEAD_DIM]
        v_ref[0, 0, :, hd, :] = kv[:, D_MODEL + hd * X_HEAD_DIM:D_MODEL + (hd + 1) * X_HEAD_DIM]


def _mem_kv(mem, w_kv):
    bsz = mem.shape[0]
    heads = pl.BlockSpec((1, 1, N_MEM, X_HEADS, X_HEAD_DIM), lambda l, b: (l, b, 0, 0, 0))
    flat = pl.BlockSpec((1, 1, N_MEM, D_MODEL), lambda l, b: (l, b, 0, 0))
    return pl.pallas_call(
        _mem_kv_body,
        grid=(DEPTH, bsz),
        in_specs=[pl.BlockSpec((1, N_MEM, D_MODEL), lambda l, b: (b, 0, 0)),
                  pl.BlockSpec((1, D_MODEL, 2 * D_MODEL), lambda l, b: (l, 0, 0))],
        out_specs=[heads, heads, flat, flat],
        out_shape=[jax.ShapeDtypeStruct((DEPTH, bsz, N_MEM, X_HEADS, X_HEAD_DIM), F32)] * 2
        + [jax.ShapeDtypeStruct((DEPTH, bsz, N_MEM, D_MODEL), BF16)] * 2,
        compiler_params=_params(2),
        name="mem_kv",
    )(mem, w_kv)


def _trunk(x, start, past_k, past_v, ssm_re, ssm_im, conv_mix_prev, conv_ffn_prev, mem_k, mem_v, w, s5, cfg):
    bsz, seq, _ = x.shape
    tm, tq, qg = cfg["tm"], cfg["tq"], cfg["qg"]
    ni, chained = cfg["ni"], cfg["chained"]
    tables = _rope_tables(seq, start, tm)
    u, q, k, v = _inproj_even(x, w["norm_mix"][0:1], w["w_in_even"], tables, 0, tm)
    h0 = jnp.concatenate([ssm_re[0].reshape(bsz, S5_STATES), ssm_im[0].reshape(bsz, S5_STATES)], axis=1)
    if chained:
        ya, h_all = _s5_mixer(u, h0[:, None, :], s5["wg"], s5["wy"], cfg["pow2"], cfg["powr"], w["ssm_d"],
                              w["w_glu"], 0, ni, True)
        h_last = h_all[:, 0]
    else:
        assert bsz == SUBLANES and seq == ni * S5_T
        ya, h_all = _s5_mixer(u.reshape(1, bsz * seq, SSM_WIDTH), h0[None], s5["wg"], s5["wy"], cfg["pow2"],
                              cfg["powr"], w["ssm_d"], w["w_glu"], 0, ni, False)
        ya = ya.reshape(bsz, seq, SSM_WIDTH)
        h_last = h_all[0]
    new_re = h_last[:, :S5_STATES].reshape(1, bsz, SSM_GROUPS, SSM_STATE)
    new_im = h_last[:, S5_STATES:].reshape(1, bsz, SSM_GROUPS, SSM_STATE)
    yb = _swa(q, k, v, past_k[0].reshape(bsz, WINDOW, KV_WIDTH), past_v[0].reshape(bsz, WINDOW, KV_WIDTH),
              w["attn_sinks"][0], start, tq, qg)
    x = _mix_even_xattn(x, ya, yb, w["w_out_even"], w["norm_xattn"][0:1], w["xattn_wq"], mem_k, mem_v,
                        w["xattn_wo"], 0, 0, tm)
    x, cf0 = _ffn(x, w["norm_ffn"][0:1], w["ffn_w_up"], w["ffn_conv_w"], w["ffn_w_down"], conv_ffn_prev[0],
                  w["norm_final"], 0, tm, False)
    x, cm = _mix_odd_xattn(x, w["norm_mix"][1:2], w["w_in_odd"], w["conv_mix_w"], w["w_out_odd"], conv_mix_prev[0],
                           w["norm_xattn"][1:2], w["xattn_wq"], mem_k, mem_v, w["xattn_wo"], 1, 0, tm)
    y, cf1 = _ffn(x, w["norm_ffn"][1:2], w["ffn_w_up"], w["ffn_conv_w"], w["ffn_w_down"], conv_ffn_prev[1],
                  w["norm_final"], 1, tm, True)
    keep = min(seq, WINDOW)
    new_k = k[:, seq - keep:].reshape(1, bsz, keep, N_KV_HEADS, HEAD_DIM)
    new_v = v[:, seq - keep:].reshape(1, bsz, keep, N_KV_HEADS, HEAD_DIM)
    return y, new_k, new_v, new_re, new_im, cm[None], jnp.stack([cf0, cf1])


def kernel(x_prompt, x_sample, mem_prompt, cache_win_k, cache_win_v, state_ssm_re, state_ssm_im, state_conv_mix, state_conv_ffn, cache_mem_k, cache_mem_v, norm_mix, norm_xattn, norm_ffn, norm_final, w_in_even, w_out_even, ssm_a_re, ssm_a_im, ssm_log_dt, ssm_b_re, ssm_b_im, ssm_c_re, ssm_c_im, ssm_d, w_glu, attn_sinks, w_in_odd, conv_mix_w, w_out_odd, xattn_wq, xattn_wkv, xattn_wo, ffn_w_up, ffn_conv_w, ffn_w_down):
    bp, seq_p, _ = x_prompt.shape
    bs, seq_s, _ = x_sample.shape
    w = dict(norm_mix=norm_mix, norm_xattn=norm_xattn, norm_ffn=norm_ffn, norm_final=norm_final.reshape(1, D_MODEL),
             w_in_even=w_in_even.astype(BF16), w_out_even=w_out_even.astype(BF16),
             ssm_d=ssm_d[0].reshape(1, SSM_WIDTH), w_glu=w_glu.astype(BF16), attn_sinks=attn_sinks,
             w_in_odd=w_in_odd.astype(BF16), conv_mix_w=conv_mix_w, w_out_odd=w_out_odd.astype(BF16),
             xattn_wq=xattn_wq.astype(BF16), xattn_wo=xattn_wo.astype(BF16),
             ffn_w_up=ffn_w_up.astype(BF16), ffn_conv_w=ffn_conv_w, ffn_w_down=ffn_w_down.astype(BF16))

    wg, wy = _s5_prep(ssm_a_re[0], ssm_a_im[0], ssm_log_dt[0], ssm_b_re[0], ssm_b_im[0], ssm_c_re[0], ssm_c_im[0])
    s5 = dict(wg=wg, wy=wy)
    ni_p, ni_s = S5_SEG_CHUNKS, seq_s // S5_T
    padded = lambda e: e + [1.0] * (-len(e) % SUBLANES)
    exps = (padded([float(S5_T * ni_p << m) for m in range(3)])
            + padded([float(S5_T * (i + 1)) for i in range(ni_p)])
            + padded([float(S5_T * (i + 1)) for i in range(ni_s)]))
    pows = _s5_pow_table(exps, ssm_a_re[0], ssm_a_im[0], ssm_log_dt[0])
    o1 = SUBLANES
    o2 = o1 + len(padded([0.0] * ni_p))
    cfg_p = dict(tm=512, tq=256, qg=128, ni=ni_p, chained=True, pow2=pows[:o1], powr=pows[o1:o2])
    cfg_s = dict(tm=seq_s, tq=seq_s, qg=seq_s, ni=ni_s, chained=False, pow2=pows[:o1], powr=pows[o2:])

    mem_k_p, mem_v_p, mem_kf_p, mem_vf_p = _mem_kv(mem_prompt, xattn_wkv.astype(BF16))
    cw = CONV_WIDTH - 1
    zk = jnp.zeros((1, bp, WINDOW, N_KV_HEADS, HEAD_DIM), F32)
    zs = jnp.zeros((1, bp, SSM_GROUPS, SSM_STATE), F32)
    zcm = jnp.zeros((1, bp, cw, D_MODEL), F32)
    zcf = jnp.zeros((DEPTH, bp, cw, D_FF), F32)
    y_p, k_p, v_p, re_p, im_p, cm_p, cf_p = _trunk(x_prompt, 0, zk, zk, zs, zs, zcm, zcf, mem_kf_p, mem_vf_p, w, s5, cfg_p)
    mem_kf_s = cache_mem_k.reshape(DEPTH, bs, N_MEM, D_MODEL).astype(BF16)
    mem_vf_s = cache_mem_v.reshape(DEPTH, bs, N_MEM, D_MODEL).astype(BF16)
    y_s, k_s, v_s, re_s, im_s, cm_s, cf_s = _trunk(x_sample, PAST_LEN, cache_win_k, cache_win_v, state_ssm_re,
                                                   state_ssm_im, state_conv_mix, state_conv_ffn, mem_kf_s,
                                                   mem_vf_s, w, s5, cfg_s)
    return (y_p, y_s, k_p, v_p, re_p, im_p, cm_p, cf_p, mem_k_p, mem_v_p, k_s, v_s, re_s, im_s, cm_s, cf_s)
```

```python
import functools
import math

import numpy as np
import jax
import jax.numpy as jnp
from jax import lax
from jax.experimental import pallas as pl
from jax.experimental.pallas import tpu as pltpu

F32 = jnp.float32
BF16 = jnp.bfloat16

D_MODEL = 1024
DEPTH = 2
PAST_LEN = 2048
CHUNK = 64
SSM_WIDTH = 512
SSM_GROUP = 16
SSM_GROUPS = 32
SSM_STATE = 64
HEAD_DIM = 64
N_HEADS = 8
N_KV_HEADS = 2
GQ = N_HEADS // N_KV_HEADS
WINDOW = 128
ROPE_THETA = 10000.0
KV_WIDTH = N_KV_HEADS * HEAD_DIM
EVEN_IN = SSM_WIDTH + N_HEADS * HEAD_DIM + 2 * KV_WIDTH
CONV_WIDTH = 3
N_MEM = 256
X_HEADS = 4
X_HEAD_DIM = 256
D_FF = 2816
EPS = 1e-6
NEG = -1e30

LANES = 128
SUBLANES = 8
S5_T = 8
S5_TILES = SSM_WIDTH // LANES
S5_TILE_STATES = (LANES // SSM_GROUP) * SSM_STATE
S5_STATES = SSM_GROUPS * SSM_STATE
S5_SEG_CHUNKS = 16
MXU_DIM = 256
FF_CHUNK = 4 * MXU_DIM
VMEM_LIMIT = 56 * 1024 * 1024


def _const_spec(shape, index=None):
    idx = tuple(index) if index is not None else (0,) * len(shape)
    return pl.BlockSpec(shape, lambda *_: idx, pipeline_mode=pl.Buffered(1))


def _params(n_axes):
    return pltpu.CompilerParams(dimension_semantics=("arbitrary",) * n_axes,
                                vmem_limit_bytes=VMEM_LIMIT)


def _rms(x, g):
    return x * lax.rsqrt(jnp.mean(x * x, axis=-1, keepdims=True) + EPS) * g


def _dot(a, b):
    return jnp.dot(a, b, preferred_element_type=F32)


def _dot_t(a, b):
    return lax.dot_general(a, b, (((1,), (1,)), ((), ())), preferred_element_type=F32)


def _rope_table_body(cos_ref, sina_ref, sinb_ref, *, start, tl):
    i = pl.program_id(0)
    row = lax.broadcasted_iota(jnp.int32, (tl, LANES), 0)
    lane = lax.broadcasted_iota(jnp.int32, (tl, LANES), 1)
    pos = (start + i * tl + row).astype(F32)
    half = HEAD_DIM // 2
    freq = (lane[0:1] & (half - 1)).astype(F32)
    inv = jnp.exp(freq * (-2.0 / HEAD_DIM * math.log(ROPE_THETA)))
    ang = pos * inv
    c = jnp.cos(ang)
    s = jnp.sin(ang)
    first = (lane & (HEAD_DIM - 1)) < HEAD_DIM // 2
    cos_ref[...] = c
    sina_ref[...] = jnp.where(first, -s, 0.0)
    sinb_ref[...] = jnp.where(first, 0.0, s)


def _rope_tables(seq, start, tl):
    spec = pl.BlockSpec((tl, LANES), lambda i: (i, 0))
    return pl.pallas_call(
        functools.partial(_rope_table_body, start=start, tl=tl),
        grid=(seq // tl,),
        in_specs=[],
        out_specs=[spec, spec, spec],
        out_shape=[jax.ShapeDtypeStruct((seq, LANES), F32)] * 3,
        compiler_params=_params(1),
        name="rope_tables",
    )()


def _rope(t, cos, sina, sinb):
    n = t.shape[1]
    half = HEAD_DIM // 2
    return t * cos + pltpu.roll(t, n - half, 1) * sina + pltpu.roll(t, half, 1) * sinb


def _inproj_even_body(x_ref, g_ref, w_ref, cos_ref, sa_ref, sb_ref, u_ref, q_ref, k_ref, v_ref):
    h = _rms(x_ref[0], g_ref[...]).astype(BF16)
    proj = _dot(h, w_ref[0])
    o0 = SSM_WIDTH
    o1 = o0 + N_HEADS * HEAD_DIM
    o2 = o1 + KV_WIDTH
    cos, sa, sb = cos_ref[...], sa_ref[...], sb_ref[...]
    rep = (o1 - o0) // LANES
    cos_q = jnp.concatenate([cos] * rep, axis=1)
    sa_q = jnp.concatenate([sa] * rep, axis=1)
    sb_q = jnp.concatenate([sb] * rep, axis=1)
    u_ref[0] = proj[:, :o0]
    q_ref[0] = (_rope(proj[:, o0:o1], cos_q, sa_q, sb_q) * HEAD_DIM ** -0.5).astype(BF16)
    k_ref[0] = _rope(proj[:, o1:o2], cos, sa, sb)
    v_ref[0] = proj[:, o2:]


def _inproj_even(x, g, w_in, tables, layer_e, tm):
    bsz, seq, _ = x.shape
    cos, sa, sb = tables
    tok = lambda w: pl.BlockSpec((1, tm, w), lambda b, i: (b, i, 0))
    tab = pl.BlockSpec((tm, LANES), lambda b, i: (i, 0))
    return pl.pallas_call(
        _inproj_even_body,
        grid=(bsz, seq // tm),
        in_specs=[tok(D_MODEL), _const_spec((1, D_MODEL)),
                  _const_spec((1, D_MODEL, EVEN_IN), (layer_e, 0, 0)), tab, tab, tab],
        out_specs=[tok(SSM_WIDTH), tok(N_HEADS * HEAD_DIM), tok(KV_WIDTH), tok(KV_WIDTH)],
        out_shape=[jax.ShapeDtypeStruct((bsz, seq, SSM_WIDTH), F32),
                   jax.ShapeDtypeStruct((bsz, seq, N_HEADS * HEAD_DIM), BF16),
                   jax.ShapeDtypeStruct((bsz, seq, KV_WIDTH), F32),
                   jax.ShapeDtypeStruct((bsz, seq, KV_WIDTH), F32)],
        compiler_params=_params(2),
        name="inproj_even",
    )(x, g, w_in, cos, sa, sb)


def _s5_prep_body(lr_ref, li_ref, ld_ref, bre_ref, bim_ref, lrc_ref, lic_ref, ldc_ref,
                  cre_ref, cim_ref, wg_ref, wy_ref):
    ts = S5_TILE_STATES
    tw = S5_T * LANES

    def step(z_r, z_i):
        mag = jnp.exp(z_r)
        a_r, a_i = mag * jnp.cos(z_i), mag * jnp.sin(z_i)
        return a_r, a_i, lambda p: (p[0] * a_r - p[1] * a_i, p[0] * a_i + p[1] * a_r)

    lr, li = lr_ref[0], li_ref[0]
    dt = jnp.exp(ld_ref[0])
    ar, ai, mul_a = step(lr * dt, li * dt)
    nrm = lr * lr + li * li
    cbr = ((ar - 1.0) * lr + ai * li) / nrm
    cbi = (ai * lr - (ar - 1.0) * li) / nrm
    b_re, b_im = bre_ref[0], bim_ref[0]
    bbr = cbr * b_re - cbi * b_im
    bbi = cbr * b_im + cbi * b_re
    p = (jnp.ones_like(ar), jnp.zeros_like(ar))
    for s in reversed(range(S5_T)):
        rows = slice(s * LANES, (s + 1) * LANES)
        wg_ref[0, rows, :ts] = (p[0] * bbr - p[1] * bbi).astype(BF16)
        wg_ref[0, rows, ts:] = (p[0] * bbi + p[1] * bbr).astype(BF16)
        for t in range(s):
            wy_ref[0, rows, t * LANES:(t + 1) * LANES] = jnp.zeros((LANES, LANES), BF16)
        p = mul_a(p)
    dtc = jnp.exp(ldc_ref[0])
    _, _, mul_ac = step(lrc_ref[0] * dtc, lic_ref[0] * dtc)
    c_re, c_im = cre_ref[0], cim_ref[0]
    bb = jnp.concatenate([bbr, bbi], axis=1).astype(BF16)
    pc = (jnp.ones_like(dtc), jnp.zeros_like(dtc))
    for k in range(S5_T + 1):
        blk = jnp.concatenate([pc[0] * c_re - pc[1] * c_im, -(pc[0] * c_im + pc[1] * c_re)], axis=0).astype(BF16)
        if k >= 1:
            wy_ref[0, tw:, (k - 1) * LANES:k * LANES] = blk
        if k < S5_T:
            lag = _dot(bb, blk).astype(BF16)
            for s in range(S5_T - k):
                t = s + k
                wy_ref[0, s * LANES:(s + 1) * LANES, t * LANES:(t + 1) * LANES] = lag
        pc = mul_ac(pc)


def _s5_prep(a_re, a_im, log_dt, b_re, b_im, c_re, c_im):
    gpt = LANES // SSM_GROUP
    eye = jnp.eye(gpt, dtype=bool)

    def rows(p):
        return p.reshape(S5_TILES, 1, S5_TILE_STATES)

    def cols(p):
        return jnp.broadcast_to(p.reshape(S5_TILES, S5_TILE_STATES, 1), (S5_TILES, S5_TILE_STATES, LANES))

    def bmat(b):
        bt = b.reshape(S5_TILES, gpt, SSM_STATE, SSM_GROUP).transpose(0, 1, 3, 2)
        full = jnp.where(eye[None, :, None, :, None], bt[:, :, :, None, :], 0.0)
        return full.reshape(S5_TILES, LANES, S5_TILE_STATES)

    def cmat(c):
        ct = c.reshape(S5_TILES, gpt, SSM_GROUP, SSM_STATE).transpose(0, 1, 3, 2)
        full = jnp.where(eye[None, :, None, :, None], ct[:, :, :, None, :], 0.0)
        return full.reshape(S5_TILES, S5_TILE_STATES, LANES)

    row_spec = pl.BlockSpec((1, 1, S5_TILE_STATES), lambda j: (j, 0, 0))
    b_spec = pl.BlockSpec((1, LANES, S5_TILE_STATES), lambda j: (j, 0, 0))
    c_spec = pl.BlockSpec((1, S5_TILE_STATES, LANES), lambda j: (j, 0, 0))
    tw = S5_T * LANES
    return pl.pallas_call(
        _s5_prep_body,
        grid=(S5_TILES,),
        in_specs=[row_spec] * 3 + [b_spec] * 2 + [c_spec] * 5,
        out_specs=[pl.BlockSpec((1, tw, 2 * S5_TILE_STATES), lambda j: (j, 0, 0)),
                   pl.BlockSpec((1, tw + 2 * S5_TILE_STATES, tw), lambda j: (j, 0, 0))],
        out_shape=[jax.ShapeDtypeStruct((S5_TILES, tw, 2 * S5_TILE_STATES), BF16),
                   jax.ShapeDtypeStruct((S5_TILES, tw + 2 * S5_TILE_STATES, tw), BF16)],
        compiler_params=_params(1),
        name="s5_prep",
    )(rows(a_re), rows(a_im), rows(log_dt), bmat(b_re), bmat(b_im),
      cols(a_re), cols(a_im), cols(log_dt), cmat(c_re), cmat(c_im))


def _s5_pow_body(ex_ref, lr_ref, li_ref, ld_ref, o_ref):
    dt = jnp.exp(ld_ref[...])
    zr, zi = lr_ref[...] * dt, li_ref[...] * dt
    ex = ex_ref[...]
    mag = jnp.exp(ex * zr)
    o_ref[:, :S5_STATES] = mag * jnp.cos(ex * zi)
    o_ref[:, S5_STATES:] = mag * jnp.sin(ex * zi)


def _s5_pow_table(exponents, a_re, a_im, log_dt):
    n = len(exponents)
    ex = jnp.asarray(np.asarray(exponents, np.float32)[:, None])
    flat = lambda p: p.reshape(1, S5_STATES)
    return pl.pallas_call(
        _s5_pow_body,
        grid=(1,),
        in_specs=[pl.BlockSpec((n, 1), lambda i: (0, 0))] + [pl.BlockSpec((1, S5_STATES), lambda i: (0, 0))] * 3,
        out_specs=pl.BlockSpec((n, 2 * S5_STATES), lambda i: (0, 0)),
        out_shape=jax.ShapeDtypeStruct((n, 2 * S5_STATES), F32),
        compiler_params=_params(1),
        name="s5_pow_table",
    )(ex, flat(a_re), flat(a_im), flat(log_dt))


def _s5_body(u_ref, h0_ref, wg_ref, wy_ref, pow2_ref, powr_ref, d_ref, wglu_ref,
             y_ref, hout_ref, carry_ref, us_ref, ys_ref, *, ni, chained):
    ns = S5_STATES
    ts = S5_TILE_STATES
    segtok = ni * S5_T
    pitch = _s5_pitch(ni)

    @pl.when(pl.program_id(1) == 0)
    def _():
        carry_ref[...] = h0_ref[0]

    for j in range(S5_TILES):
        for s in range(SUBLANES):
            us_ref[j, s * pitch:s * pitch + segtok] = u_ref[0, s * segtok:(s + 1) * segtok, j * LANES:(j + 1) * LANES]

    def gather(j, t):
        return jnp.concatenate([us_ref[j, pl.ds(S5_T * i + t, SUBLANES, stride=pitch), :]
                                for i in range(ni)], axis=0)

    def fma(p_r, p_i, x_r, x_i, y_r, y_i):
        return y_r + p_r * x_r - p_i * x_i, y_i + p_r * x_i + p_i * x_r

    xs, g_re, g_im = [], [], []
    for j in range(S5_TILES):
        xj = jnp.concatenate([gather(j, s) for s in range(S5_T)], axis=1).astype(BF16)
        g = _dot(xj, wg_ref[j])
        xs.append(xj)
        g_re.append(g[:, :ts])
        g_im.append(g[:, ts:])
    g_re = jnp.concatenate(g_re, axis=1)
    g_im = jnp.concatenate(g_im, axis=1)
    slab = lambda a, i: a[i * SUBLANES:(i + 1) * SUBLANES]
    power = lambda ref, r: (ref[r:r + 1, :ns], ref[r:r + 1, ns:])
    a_r, a_i = power(powr_ref, 0)
    cin = carry_ref[...]
    h_r, h_i = cin[:, :ns], cin[:, ns:]
    prev = []
    if chained:
        local = [(slab(g_re, 0), slab(g_im, 0))]
        for i in range(1, ni):
            local.append(fma(a_r, a_i, *local[-1], slab(g_re, i), slab(g_im, i)))
        sub = lax.broadcasted_iota(jnp.int32, (SUBLANES, 1), 0)
        c_r = jnp.where(sub == 0, h_r, pltpu.roll(local[-1][0], 1, 0))
        c_i = jnp.where(sub == 0, h_i, pltpu.roll(local[-1][1], 1, 0))
        for m in range(3):
            k = 1 << m
            s_r = jnp.where(sub >= k, pltpu.roll(c_r, k, 0), 0.0)
            s_i = jnp.where(sub >= k, pltpu.roll(c_i, k, 0), 0.0)
            c_r, c_i = fma(*power(pow2_ref, m), s_r, s_i, c_r, c_i)
        h_r, h_i = c_r, c_i
        for i in range(ni):
            prev.append((h_r, h_i))
            h_r, h_i = fma(*power(powr_ref, i), c_r, c_i, *local[i])
        last = jnp.concatenate([h_r[SUBLANES - 1:], h_i[SUBLANES - 1:]], axis=1)
    else:
        for i in range(ni):
            prev.append((h_r, h_i))
            h_r, h_i = fma(a_r, a_i, h_r, h_i, slab(g_re, i), slab(g_im, i))
        last = jnp.concatenate([h_r, h_i], axis=1)
    carry_ref[...] = last
    hout_ref[0] = last
    prev_re = jnp.concatenate([p[0] for p in prev], axis=0)
    prev_im = jnp.concatenate([p[1] for p in prev], axis=0)
    y_tiles = []
    for j in range(S5_TILES):
        lhs = jnp.concatenate([xs[j], prev_re[:, j * ts:(j + 1) * ts].astype(BF16),
                               prev_im[:, j * ts:(j + 1) * ts].astype(BF16)], axis=1)
        y_tiles.append(_dot(lhs, wy_ref[j]))
    d = d_ref[...]
    wglu = wglu_ref[0]
    for t in range(S5_T):
        ut = jnp.concatenate([gather(j, t) for j in range(S5_TILES)], axis=1)
        yt = jnp.concatenate([y_tiles[j][:, t * LANES:(t + 1) * LANES] for j in range(S5_TILES)], axis=1)
        g = jax.nn.gelu(yt + d * ut)
        o = g * jax.nn.sigmoid(_dot(g.astype(BF16), wglu))
        for j in range(S5_TILES):
            for i in range(ni):
                ys_ref[j, pl.ds(S5_T * i + t, SUBLANES, stride=pitch), :] = slab(o, i)[:, j * LANES:(j + 1) * LANES]
    for j in range(S5_TILES):
        for s in range(SUBLANES):
            y_ref[0, s * segtok:(s + 1) * segtok, j * LANES:(j + 1) * LANES] = (
                ys_ref[j, s * pitch:s * pitch + segtok].astype(BF16))


def _s5_pitch(ni):
    segtok = ni * S5_T
    return segtok + SUBLANES if (segtok // SUBLANES) % 2 == 0 else segtok


def _s5_mixer(u, h0, wg, wy, pow2, powr, d, w_glu, layer_e, ni, chained):
    nb, ntok, width = u.shape
    tb = SUBLANES * ni * S5_T
    hrows = h0.shape[1]
    blk = pl.BlockSpec((1, tb, width), lambda b, i: (b, i, 0))
    state = pl.BlockSpec((1, hrows, 2 * S5_STATES), lambda b, i: (b, 0, 0))
    return pl.pallas_call(
        functools.partial(_s5_body, ni=ni, chained=chained),
        grid=(nb, ntok // tb),
        in_specs=[blk, state, _const_spec(wg.shape), _const_spec(wy.shape),
                  _const_spec(pow2.shape), _const_spec(powr.shape), _const_spec((1, SSM_WIDTH)),
                  _const_spec((1, SSM_WIDTH, SSM_WIDTH), (layer_e, 0, 0))],
        out_specs=[blk, state],
        out_shape=[jax.ShapeDtypeStruct((nb, ntok, width), BF16),
                   jax.ShapeDtypeStruct((nb, hrows, 2 * S5_STATES), F32)],
        scratch_shapes=[pltpu.VMEM((hrows, 2 * S5_STATES), F32)]
        + [pltpu.VMEM((S5_TILES, SUBLANES * _s5_pitch(ni), LANES), F32)] * 2,
        compiler_params=_params(2),
        name="s5_mixer",
    )(u, h0, wg, wy, pow2, powr, d, w_glu)


def _swa_body(sink_ref, q_ref, kc_ref, vc_ref, kp_ref, vp_ref, pk_ref, pv_ref, o_ref, *, tq, qg, start):
    first = pl.program_id(1) == 0
    k_prev = jnp.where(first, pk_ref[0], kp_ref[0])
    v_prev = jnp.where(first, pv_ref[0], vp_ref[0])
    k_all = jnp.concatenate([k_prev, kc_ref[0]], axis=0)
    v_all = jnp.concatenate([v_prev, vc_ref[0]], axis=0)
    nk = WINDOW + qg
    lane = lax.broadcasted_iota(jnp.int32, (1, KV_WIDTH), 1)
    slot = lax.broadcasted_iota(jnp.int32, (1, nk), 1)
    qrow = lax.broadcasted_iota(jnp.int32, (2 * qg, 1), 0)
    lower = lane < HEAD_DIM
    placed = []
    for kh in range(N_KV_HEADS):
        own = lower if kh == 0 else jnp.logical_not(lower)
        kz, vz = jnp.where(own, k_all, 0.0), jnp.where(own, v_all, 1.0)
        kr, vr = pltpu.roll(kz, HEAD_DIM, 1), pltpu.roll(vz, HEAD_DIM, 1)
        k_lo, k_hi = (kz, kr) if kh == 0 else (kr, kz)
        v_lo, v_hi = (vz, vr) if kh == 0 else (vr, vz)
        placed.append(((k_lo.astype(BF16), v_lo.astype(BF16)), (k_hi.astype(BF16), v_hi.astype(BF16))))
    combos = [(kh, par) for kh in range(N_KV_HEADS) for par in range(2)]

    def scores(c):
        qc = q_ref[0, c * qg:(c + 1) * qg, :]
        out = []
        for kh, par in combos:
            base = kh * GQ * HEAD_DIM
            qs = jnp.concatenate([qc[:, base:base + LANES], qc[:, base + LANES:base + 2 * LANES]], axis=0)
            out.append(_dot_t(qs, placed[kh][par][0][c * qg:c * qg + nk]))
        return out

    def softmax(c, ss):
        n_bad = jnp.where(first, WINDOW - start - c * qg, 0) if c * qg < WINDOW - start else None
        probs, sinks = [], []
        for (kh, par), s in zip(combos, ss):
            if n_bad is not None:
                s = jnp.where(slot >= n_bad, s, NEG)
            sink = jnp.where(qrow < qg, sink_ref[kh * GQ + par], sink_ref[kh * GQ + 2 + par])
            m = jnp.maximum(jnp.max(s, axis=1, keepdims=True), sink)
            probs.append(jnp.exp(s - m).astype(BF16))
            sinks.append(jnp.exp(sink - m))
        return probs, sinks

    def finish(c, probs, sinks):
        pvs = [_dot(p, placed[kh][par][1][c * qg:c * qg + nk]) for (kh, par), p in zip(combos, probs)]
        halves = [pv / (pltpu.roll(pv, HEAD_DIM, 1) + sk) for pv, sk in zip(pvs, sinks)]
        outs = []
        for kh in range(N_KV_HEADS):
            acc = jnp.where(lower, halves[2 * kh], halves[2 * kh + 1])
            outs += [acc[:qg], acc[qg:]]
        o_ref[0, c * qg:(c + 1) * qg, :] = jnp.concatenate(outs, axis=1).astype(BF16)

    nchunk = tq // qg
    s_q, p_q = {}, {}
    for step in range(nchunk + 2):
        if step < nchunk:
            s_q[step] = scores(step)
        if 0 <= step - 1 < nchunk:
            p_q[step - 1] = softmax(step - 1, s_q.pop(step - 1))
        if 0 <= step - 2 < nchunk:
            finish(step - 2, *p_q.pop(step - 2))


def _swa(q, k, v, past_k, past_v, sinks, start, tq, qg):
    bsz, seq, _ = q.shape
    ntiles = seq // tq
    tok = lambda w: pl.BlockSpec((1, tq, w), lambda b, i: (b, i, 0))
    past = pl.BlockSpec((1, WINDOW, KV_WIDTH), lambda b, i: (b, 0, 0))
    if ntiles > 1:
        per = tq // WINDOW
        prev = pl.BlockSpec((1, WINDOW, KV_WIDTH), lambda b, i: (b, jnp.maximum(i * per - 1, 0), 0))
        k_prev, v_prev = k, v
    else:
        prev, k_prev, v_prev = past, past_k, past_v
    return pl.pallas_call(
        functools.partial(_swa_body, tq=tq, qg=qg, start=start),
        grid=(bsz, ntiles),
        in_specs=[pl.BlockSpec(memory_space=pltpu.SMEM), tok(N_HEADS * HEAD_DIM), tok(KV_WIDTH), tok(KV_WIDTH),
                  prev, prev, past, past],
        out_specs=tok(N_HEADS * HEAD_DIM),
        out_shape=jax.ShapeDtypeStruct((bsz, seq, N_HEADS * HEAD_DIM), BF16),
        compiler_params=_params(2),
        name="swa",
    )(sinks, q, k, v, k_prev, v_prev, past_k, past_v)


def _xattn(x1, g, wq, mk_ref, mv_ref, wo):
    h = _rms(x1, g).astype(BF16)
    q = _dot(h, wq).astype(BF16)
    scale = X_HEAD_DIM ** -0.5
    outs = []
    for hd in range(X_HEADS):
        cols = slice(hd * X_HEAD_DIM, (hd + 1) * X_HEAD_DIM)
        s = _dot_t(q[:, cols], mk_ref[0, 0, :, cols]) * scale
        p = jnp.exp(s - jnp.max(s, axis=1, keepdims=True))
        den = jnp.sum(p, axis=1, keepdims=True)
        outs.append(_dot(p.astype(BF16), mv_ref[0, 0, :, cols]) / den)
    o = jnp.concatenate(outs, axis=1).astype(BF16)
    return x1 + _dot(o, wo)


def _conv3(cur, carry, w):
    n = cur.shape[0]
    row = lax.broadcasted_iota(jnp.int32, (n, 1), 0)
    c1, c2 = carry[SUBLANES - 1:SUBLANES], carry[SUBLANES - 2:SUBLANES - 1]
    m1 = jnp.where(row == 0, c1, pltpu.roll(cur, 1, 0))
    m2 = jnp.where(row == 0, c2, jnp.where(row == 1, c1, pltpu.roll(cur, 2, 0)))
    return w[0:1] * m2 + w[1:2] * m1 + w[2:3] * cur


def _mix_even_xattn_body(x_ref, ya_ref, yb_ref, woa_ref, wob_ref, g_ref, wq_ref, mk_ref, mv_ref, wo_ref, o_ref):
    x1 = x_ref[0] + _dot(ya_ref[0], woa_ref[0]) + _dot(yb_ref[0], wob_ref[0])
    o_ref[0] = _xattn(x1, g_ref[...], wq_ref[0], mk_ref, mv_ref, wo_ref[0])


def _mix_even_xattn(x, ya, yb, w_out, g, wq, mem_k, mem_v, wo, layer, layer_e, tm):
    bsz, seq, _ = x.shape
    tok = lambda w: pl.BlockSpec((1, tm, w), lambda b, i: (b, i, 0))
    mem = pl.BlockSpec((1, 1, N_MEM, D_MODEL), lambda b, i: (layer, b, 0, 0))
    half = D_MODEL // 2
    sq = lambda: _const_spec((1, D_MODEL, D_MODEL), (layer, 0, 0))
    return pl.pallas_call(
        _mix_even_xattn_body,
        grid=(bsz, seq // tm),
        in_specs=[tok(D_MODEL), tok(half), tok(half),
                  _const_spec((1, half, D_MODEL), (layer_e, 0, 0)), _const_spec((1, half, D_MODEL), (layer_e, 1, 0)),
                  _const_spec((1, D_MODEL)), sq(), mem, mem, sq()],
        out_specs=tok(D_MODEL),
        out_shape=jax.ShapeDtypeStruct(x.shape, F32),
        compiler_params=_params(2),
        name="mix_even_xattn",
    )(x, ya, yb, w_out, w_out, g, wq, mem_k, mem_v, wo)


def _mix_odd_xattn_body(x_ref, gm_ref, win_ref, cw_ref, wout_ref, prev_ref, g_ref, wq_ref, mk_ref, mv_ref, wo_ref,
                        o_ref, st_ref, carry_ref, *, tm):
    @pl.when(pl.program_id(1) == 0)
    def _():
        carry_ref[SUBLANES - 2:SUBLANES, :] = prev_ref[0]

    x = x_ref[0]
    proj = _dot(_rms(x, gm_ref[...]).astype(BF16), win_ref[0])
    gate_b, gate_c, z = proj[:, :D_MODEL], proj[:, D_MODEL:2 * D_MODEL], proj[:, 2 * D_MODEL:]
    cz = gate_c * z
    z_conv = _conv3(cz, carry_ref[...], cw_ref[0])
    carry_ref[...] = cz[tm - SUBLANES:]
    st_ref[0] = cz[tm - (CONV_WIDTH - 1):]
    x1 = x + _dot((gate_b * z_conv).astype(BF16), wout_ref[0])
    o_ref[0] = _xattn(x1, g_ref[...], wq_ref[0], mk_ref, mv_ref, wo_ref[0])


def _mix_odd_xattn(x, gm, w_in, conv_w, w_out, prev, g, wq, mem_k, mem_v, wo, layer, layer_o, tm):
    bsz, seq, _ = x.shape
    tok = pl.BlockSpec((1, tm, D_MODEL), lambda b, i: (b, i, 0))
    mem = pl.BlockSpec((1, 1, N_MEM, D_MODEL), lambda b, i: (layer, b, 0, 0))
    st = pl.BlockSpec((1, CONV_WIDTH - 1, D_MODEL), lambda b, i: (b, 0, 0))
    sq = lambda l: _const_spec((1, D_MODEL, D_MODEL), (l, 0, 0))
    return pl.pallas_call(
        functools.partial(_mix_odd_xattn_body, tm=tm),
        grid=(bsz, seq // tm),
        in_specs=[tok, _const_spec((1, D_MODEL)), _const_spec((1, D_MODEL, 3 * D_MODEL), (layer_o, 0, 0)),
                  _const_spec((1, CONV_WIDTH, D_MODEL), (layer_o, 0, 0)), sq(layer_o), st,
                  _const_spec((1, D_MODEL)), sq(layer), mem, mem, sq(layer)],
        out_specs=[tok, st],
        out_shape=[jax.ShapeDtypeStruct(x.shape, F32),
                   jax.ShapeDtypeStruct((bsz, CONV_WIDTH - 1, D_MODEL), F32)],
        scratch_shapes=[pltpu.VMEM((SUBLANES, D_MODEL), F32)],
        compiler_params=_params(2),
        name="mix_odd_xattn",
    )(x, gm, w_in, conv_w, w_out, prev, g, wq, mem_k, mem_v, wo)


def _ffn_body(x_ref, g_ref, wup_ref, cw_ref, wdn_ref, prev_ref, gfin_ref, o_ref, st_ref, carry_ref, *, tm, final):
    @pl.when(pl.program_id(1) == 0)
    def _():
        carry_ref[SUBLANES - 2:SUBLANES, :] = prev_ref[0]

    x = x_ref[0]
    h = _rms(x, g_ref[...]).astype(BF16)
    acc = x
    for lo in range(0, D_FF, FF_CHUNK):
        cols = slice(lo, min(lo + FF_CHUNK, D_FF))
        gate = _dot(h, wup_ref[0, :, cols])
        val = _dot(h, wup_ref[0, :, D_FF + cols.start:D_FF + cols.stop])
        gate_c = _conv3(gate, carry_ref[:, cols], cw_ref[0, :, cols])
        carry_ref[:, cols] = gate[tm - SUBLANES:]
        st_ref[0, :, cols] = gate[tm - (CONV_WIDTH - 1):]
        act = gate_c * jax.nn.sigmoid(gate_c) * val
        acc = acc + _dot(act.astype(BF16), wdn_ref[0, cols, :])
    o_ref[0] = _rms(acc, gfin_ref[...]) if final else acc


def _ffn(x, g, w_up, conv_w, w_down, prev, g_final, layer, tm, final):
    bsz, seq, _ = x.shape
    tok = pl.BlockSpec((1, tm, D_MODEL), lambda b, i: (b, i, 0))
    st = pl.BlockSpec((1, CONV_WIDTH - 1, D_FF), lambda b, i: (b, 0, 0))
    return pl.pallas_call(
        functools.partial(_ffn_body, tm=tm, final=final),
        grid=(bsz, seq // tm),
        in_specs=[tok, _const_spec((1, D_MODEL)), _const_spec((1, D_MODEL, 2 * D_FF), (layer, 0, 0)),
                  _const_spec((1, CONV_WIDTH, D_FF), (layer, 0, 0)), _const_spec((1, D_FF, D_MODEL), (layer, 0, 0)),
                  st, _const_spec((1, D_MODEL))],
        out_specs=[tok, st],
        out_shape=[jax.ShapeDtypeStruct(x.shape, F32),
                   jax.ShapeDtypeStruct((bsz, CONV_WIDTH - 1, D_FF), F32)],
        scratch_shapes=[pltpu.VMEM((SUBLANES, D_FF), F32)],
        compiler_params=_params(2),
        name="conv_ffn",
    )(x, g, w_up, conv_w, w_down, prev, g_final)


def _mem_kv_body(m_ref, w_ref, k_ref, v_ref, kf_ref, vf_ref):
    kv = _dot(m_ref[0].astype(BF16), w_ref[0])
    kf_ref[0, 0] = kv[:, :D_MODEL].astype(BF16)
    vf_ref[0, 0] = kv[:, D_MODEL:].astype(BF16)
    for hd in range(X_HEADS):
        k_ref[0, 0, :, hd, :] = kv[:, hd * X_HEAD_DIM:(hd + 1) * X_HEAD_DIM]
        v_ref[0, 0, :, hd, :] = kv[:, D_MODEL + hd * X_HEAD_DIM:D_MODEL + (hd + 1) * X_HEAD_DIM]


def _mem_kv(mem, w_kv):
    bsz = mem.shape[0]
    heads = pl.BlockSpec((1, 1, N_MEM, X_HEADS, X_HEAD_DIM), lambda l, b: (l, b, 0, 0, 0))
    flat = pl.BlockSpec((1, 1, N_MEM, D_MODEL), lambda l, b: (l, b, 0, 0))
    return pl.pallas_call(
        _mem_kv_body,
        grid=(DEPTH, bsz),
        in_specs=[pl.BlockSpec((1, N_MEM, D_MODEL), lambda l, b: (b, 0, 0)),
                  pl.BlockSpec((1, D_MODEL, 2 * D_MODEL), lambda l, b: (l, 0, 0))],
        out_specs=[heads, heads, flat, flat],
        out_shape=[jax.ShapeDtypeStruct((DEPTH, bsz, N_MEM, X_HEADS, X_HEAD_DIM), F32)] * 2
        + [jax.ShapeDtypeStruct((DEPTH, bsz, N_MEM, D_MODEL), BF16)] * 2,
        compiler_params=_params(2),
        name="mem_kv",
    )(mem, w_kv)


def _trunk(x, start, past_k, past_v, ssm_re, ssm_im, conv_mix_prev, conv_ffn_prev, mem_k, mem_v, w, s5, cfg):
    bsz, seq, _ = x.shape
    tm, tq, qg = cfg["tm"], cfg["tq"], cfg["qg"]
    ni, chained = cfg["ni"], cfg["chained"]
    tables = _rope_tables(seq, start, tm)
    u, q, k, v = _inproj_even(x, w["norm_mix"][0:1], w["w_in_even"], tables, 0, tm)
    h0 = jnp.concatenate([ssm_re[0].reshape(bsz, S5_STATES), ssm_im[0].reshape(bsz, S5_STATES)], axis=1)
    if chained:
        ya, h_all = _s5_mixer(u, h0[:, None, :], s5["wg"], s5["wy"], cfg["pow2"], cfg["powr"], w["ssm_d"],
                              w["w_glu"], 0, ni, True)
        h_last = h_all[:, 0]
    else:
        assert bsz == SUBLANES and seq == ni * S5_T
        ya, h_all = _s5_mixer(u.reshape(1, bsz * seq, SSM_WIDTH), h0[None], s5["wg"], s5["wy"], cfg["pow2"],
                              cfg["powr"], w["ssm_d"], w["w_glu"], 0, ni, False)
        ya = ya.reshape(bsz, seq, SSM_WIDTH)
        h_last = h_all[0]
    new_re = h_last[:, :S5_STATES].reshape(1, bsz, SSM_GROUPS, SSM_STATE)
    new_im = h_last[:, S5_STATES:].reshape(1, bsz, SSM_GROUPS, SSM_STATE)
    yb = _swa(q, k, v, past_k[0].reshape(bsz, WINDOW, KV_WIDTH), past_v[0].reshape(bsz, WINDOW, KV_WIDTH),
              w["attn_sinks"][0], start, tq, qg)
    x = _mix_even_xattn(x, ya, yb, w["w_out_even"], w["norm_xattn"][0:1], w["xattn_wq"], mem_k, mem_v,
                        w["xattn_wo"], 0, 0, tm)
    x, cf0 = _ffn(x, w["norm_ffn"][0:1], w["ffn_w_up"], w["ffn_conv_w"], w["ffn_w_down"], conv_ffn_prev[0],
                  w["norm_final"], 0, tm, False)
    x, cm = _mix_odd_xattn(x, w["norm_mix"][1:2], w["w_in_odd"], w["conv_mix_w"], w["w_out_odd"], conv_mix_prev[0],
                           w["norm_xattn"][1:2], w["xattn_wq"], mem_k, mem_v, w["xattn_wo"], 1, 0, tm)
    y, cf1 = _ffn(x, w["norm_ffn"][1:2], w["ffn_w_up"], w["ffn_conv_w"], w["ffn_w_down"], conv_ffn_prev[1],
                  w["norm_final"], 1, tm, True)
    keep = min(seq, WINDOW)
    new_k = k[:, seq - keep:].reshape(1, bsz, keep, N_KV_HEADS, HEAD_DIM)
    new_v = v[:, seq - keep:].reshape(1, bsz, keep, N_KV_HEADS, HEAD_DIM)
    return y, new_k, new_v, new_re, new_im, cm[None], jnp.stack([cf0, cf1])


def kernel(x_prompt, x_sample, mem_prompt, cache_win_k, cache_win_v, state_ssm_re, state_ssm_im, state_conv_mix, state_conv_ffn, cache_mem_k, cache_mem_v, norm_mix, norm_xattn, norm_ffn, norm_final, w_in_even, w_out_even, ssm_a_re, ssm_a_im, ssm_log_dt, ssm_b_re, ssm_b_im, ssm_c_re, ssm_c_im, ssm_d, w_glu, attn_sinks, w_in_odd, conv_mix_w, w_out_odd, xattn_wq, xattn_wkv, xattn_wo, ffn_w_up, ffn_conv_w, ffn_w_down):
    bp, seq_p, _ = x_prompt.shape
    bs, seq_s, _ = x_sample.shape
    w = dict(norm_mix=norm_mix, norm_xattn=norm_xattn, norm_ffn=norm_ffn, norm_final=norm_final.reshape(1, D_MODEL),
             w_in_even=w_in_even.astype(BF16), w_out_even=w_out_even.astype(BF16),
             ssm_d=ssm_d[0].reshape(1, SSM_WIDTH), w_glu=w_glu.astype(BF16), attn_sinks=attn_sinks,
             w_in_odd=w_in_odd.astype(BF16), conv_mix_w=conv_mix_w, w_out_odd=w_out_odd.astype(BF16),
             xattn_wq=xattn_wq.astype(BF16), xattn_wo=xattn_wo.astype(BF16),
             ffn_w_up=ffn_w_up.astype(BF16), ffn_conv_w=ffn_conv_w, ffn_w_down=ffn_w_down.astype(BF16))

    wg, wy = _s5_prep(ssm_a_re[0], ssm_a_im[0], ssm_log_dt[0], ssm_b_re[0], ssm_b_im[0], ssm_c_re[0], ssm_c_im[0])
    s5 = dict(wg=wg, wy=wy)
    ni_p, ni_s = S5_SEG_CHUNKS, seq_s // S5_T
    padded = lambda e: e + [1.0] * (-len(e) % SUBLANES)
    exps = (padded([float(S5_T * ni_p << m) for m in range(3)])
            + padded([float(S5_T * (i + 1)) for i in range(ni_p)])
            + padded([float(S5_T * (i + 1)) for i in range(ni_s)]))
    pows = _s5_pow_table(exps, ssm_a_re[0], ssm_a_im[0], ssm_log_dt[0])
    o1 = SUBLANES
    o2 = o1 + len(padded([0.0] * ni_p))
    cfg_p = dict(tm=512, tq=512, qg=CHUNK, ni=ni_p, chained=True, pow2=pows[:o1], powr=pows[o1:o2])
    cfg_s = dict(tm=seq_s, tq=seq_s, qg=seq_s, ni=ni_s, chained=False, pow2=pows[:o1], powr=pows[o2:])

    mem_k_p, mem_v_p, mem_kf_p, mem_vf_p = _mem_kv(mem_prompt, xattn_wkv.astype(BF16))
    cw = CONV_WIDTH - 1
    zk = jnp.zeros((1, bp, WINDOW, N_KV_HEADS, HEAD_DIM), F32)
    zs = jnp.zeros((1, bp, SSM_GROUPS, SSM_STATE), F32)
    zcm = jnp.zeros((1, bp, cw, D_MODEL), F32)
    zcf = jnp.zeros((DEPTH, bp, cw, D_FF), F32)
    y_p, k_p, v_p, re_p, im_p, cm_p, cf_p = _trunk(x_prompt, 0, zk, zk, zs, zs, zcm, zcf, mem_kf_p, mem_vf_p, w, s5, cfg_p)
    mem_kf_s = cache_mem_k.reshape(DEPTH, bs, N_MEM, D_MODEL).astype(BF16)
    mem_vf_s = cache_mem_v.reshape(DEPTH, bs, N_MEM, D_MODEL).astype(BF16)
    y_s, k_s, v_s, re_s, im_s, cm_s, cf_s = _trunk(x_sample, PAST_LEN, cache_win_k, cache_win_v, state_ssm_re,
                                                   state_ssm_im, state_conv_mix, state_conv_ffn, mem_kf_s,
                                                   mem_vf_s, w, s5, cfg_s)
    return (y_p, y_s, k_p, v_p, re_p, im_p, cm_p, cf_p, mem_k_p, mem_v_p, k_s, v_s, re_s, im_s, cm_s, cf_s)
```

```python
import functools
import math

import numpy as np
import jax
import jax.numpy as jnp
from jax import lax
from jax.experimental import pallas as pl
from jax.experimental.pallas import tpu as pltpu

F32 = jnp.float32
BF16 = jnp.bfloat16

D_MODEL = 1024
DEPTH = 2
PAST_LEN = 2048
CHUNK = 64
SSM_WIDTH = 512
SSM_GROUP = 16
SSM_GROUPS = 32
SSM_STATE = 64
HEAD_DIM = 64
N_HEADS = 8
N_KV_HEADS = 2
GQ = N_HEADS // N_KV_HEADS
WINDOW = 128
ROPE_THETA = 10000.0
KV_WIDTH = N_KV_HEADS * HEAD_DIM
EVEN_IN = SSM_WIDTH + N_HEADS * HEAD_DIM + 2 * KV_WIDTH
CONV_WIDTH = 3
N_MEM = 256
X_HEADS = 4
X_HEAD_DIM = 256
D_FF = 2816
EPS = 1e-6
NEG = -1e30

LANES = 128
SUBLANES = 8
S5_T = 8
S5_TILES = SSM_WIDTH // LANES
S5_TILE_STATES = (LANES // SSM_GROUP) * SSM_STATE
S5_STATES = SSM_GROUPS * SSM_STATE
S5_SEG_CHUNKS = 16
MXU_DIM = 256
FF_CHUNK = 4 * MXU_DIM
VMEM_LIMIT = 56 * 1024 * 1024


def _const_spec(shape, index=None):
    idx = tuple(index) if index is not None else (0,) * len(shape)
    return pl.BlockSpec(shape, lambda *_: idx, pipeline_mode=pl.Buffered(1))


def _params(n_axes):
    return pltpu.CompilerParams(dimension_semantics=("arbitrary",) * n_axes,
                                vmem_limit_bytes=VMEM_LIMIT)


def _rms(x, g):
    return x * lax.rsqrt(jnp.mean(x * x, axis=-1, keepdims=True) + EPS) * g


def _dot(a, b):
    return jnp.dot(a, b, preferred_element_type=F32)


def _dot_t(a, b):
    return lax.dot_general(a, b, (((1,), (1,)), ((), ())), preferred_element_type=F32)


def _rope_table_body(cos_ref, sina_ref, sinb_ref, *, start, tl, period):
    i = pl.program_id(0)
    row = lax.broadcasted_iota(jnp.int32, (tl, LANES), 0)
    lane = lax.broadcasted_iota(jnp.int32, (tl, LANES), 1)
    pos = (start + ((i * tl + row) & (period - 1))).astype(F32)
    half = HEAD_DIM // 2
    freq = (lane[0:1] & (half - 1)).astype(F32)
    inv = jnp.exp(freq * (-2.0 / HEAD_DIM * math.log(ROPE_THETA)))
    ang = pos * inv
    c = jnp.cos(ang)
    s = jnp.sin(ang)
    first = (lane & (HEAD_DIM - 1)) < HEAD_DIM // 2
    cos_ref[...] = c
    sina_ref[...] = jnp.where(first, -s, 0.0)
    sinb_ref[...] = jnp.where(first, 0.0, s)


def _rope_tables(rows, period, start, tl):
    spec = pl.BlockSpec((tl, LANES), lambda i: (i, 0))
    return pl.pallas_call(
        functools.partial(_rope_table_body, start=start, tl=tl, period=period),
        grid=(rows // tl,),
        in_specs=[],
        out_specs=[spec, spec, spec],
        out_shape=[jax.ShapeDtypeStruct((rows, LANES), F32)] * 3,
        compiler_params=_params(1),
        name="rope_tables",
    )()


def _rope(t, cos, sina, sinb):
    n = t.shape[1]
    half = HEAD_DIM // 2
    return t * cos + pltpu.roll(t, n - half, 1) * sina + pltpu.roll(t, half, 1) * sinb


def _inproj_even_body(x_ref, g_ref, w_ref, cos_ref, sa_ref, sb_ref, u_ref, q_ref, k_ref, v_ref):
    h = _rms(x_ref[0], g_ref[...]).astype(BF16)
    proj = _dot(h, w_ref[0])
    o0 = SSM_WIDTH
    o1 = o0 + N_HEADS * HEAD_DIM
    o2 = o1 + KV_WIDTH
    cos, sa, sb = cos_ref[...], sa_ref[...], sb_ref[...]
    rep = (o1 - o0) // LANES
    cos_q = jnp.concatenate([cos] * rep, axis=1)
    sa_q = jnp.concatenate([sa] * rep, axis=1)
    sb_q = jnp.concatenate([sb] * rep, axis=1)
    u_ref[0] = proj[:, :o0]
    q_ref[0] = (_rope(proj[:, o0:o1], cos_q, sa_q, sb_q) * HEAD_DIM ** -0.5).astype(BF16)
    k_ref[0] = _rope(proj[:, o1:o2], cos, sa, sb)
    v_ref[0] = proj[:, o2:]


def _inproj_even(x, g, w_in, tables, layer_e, tm):
    bsz, seq, _ = x.shape
    cos, sa, sb = tables
    tok = lambda w: pl.BlockSpec((1, tm, w), lambda b, i: (b, i, 0))
    tab = pl.BlockSpec((tm, LANES), lambda b, i: (i, 0))
    return pl.pallas_call(
        _inproj_even_body,
        grid=(bsz, seq // tm),
        in_specs=[tok(D_MODEL), _const_spec((1, D_MODEL)),
                  _const_spec((1, D_MODEL, EVEN_IN), (layer_e, 0, 0)), tab, tab, tab],
        out_specs=[tok(SSM_WIDTH), tok(N_HEADS * HEAD_DIM), tok(KV_WIDTH), tok(KV_WIDTH)],
        out_shape=[jax.ShapeDtypeStruct((bsz, seq, SSM_WIDTH), F32),
                   jax.ShapeDtypeStruct((bsz, seq, N_HEADS * HEAD_DIM), BF16),
                   jax.ShapeDtypeStruct((bsz, seq, KV_WIDTH), F32),
                   jax.ShapeDtypeStruct((bsz, seq, KV_WIDTH), F32)],
        compiler_params=_params(2),
        name="inproj_even",
    )(x, g, w_in, cos, sa, sb)


def _s5_prep_body(lr_ref, li_ref, ld_ref, bre_ref, bim_ref, lrc_ref, lic_ref, ldc_ref,
                  cre_ref, cim_ref, wg_ref, wy_ref):
    ts = S5_TILE_STATES
    tw = S5_T * LANES

    def step(z_r, z_i):
        mag = jnp.exp(z_r)
        a_r, a_i = mag * jnp.cos(z_i), mag * jnp.sin(z_i)
        return a_r, a_i, lambda p: (p[0] * a_r - p[1] * a_i, p[0] * a_i + p[1] * a_r)

    lr, li = lr_ref[0], li_ref[0]
    dt = jnp.exp(ld_ref[0])
    ar, ai, mul_a = step(lr * dt, li * dt)
    nrm = lr * lr + li * li
    cbr = ((ar - 1.0) * lr + ai * li) / nrm
    cbi = (ai * lr - (ar - 1.0) * li) / nrm
    b_re, b_im = bre_ref[0], bim_ref[0]
    bbr = cbr * b_re - cbi * b_im
    bbi = cbr * b_im + cbi * b_re
    p = (jnp.ones_like(ar), jnp.zeros_like(ar))
    for s in reversed(range(S5_T)):
        rows = slice(s * LANES, (s + 1) * LANES)
        wg_ref[0, rows, :ts] = (p[0] * bbr - p[1] * bbi).astype(BF16)
        wg_ref[0, rows, ts:] = (p[0] * bbi + p[1] * bbr).astype(BF16)
        for t in range(s):
            wy_ref[0, rows, t * LANES:(t + 1) * LANES] = jnp.zeros((LANES, LANES), BF16)
        p = mul_a(p)
    dtc = jnp.exp(ldc_ref[0])
    _, _, mul_ac = step(lrc_ref[0] * dtc, lic_ref[0] * dtc)
    c_re, c_im = cre_ref[0], cim_ref[0]
    bb = jnp.concatenate([bbr, bbi], axis=1).astype(BF16)
    pc = (jnp.ones_like(dtc), jnp.zeros_like(dtc))
    for k in range(S5_T + 1):
        blk = jnp.concatenate([pc[0] * c_re - pc[1] * c_im, -(pc[0] * c_im + pc[1] * c_re)], axis=0).astype(BF16)
        if k >= 1:
            wy_ref[0, tw:, (k - 1) * LANES:k * LANES] = blk
        if k < S5_T:
            lag = _dot(bb, blk).astype(BF16)
            for s in range(S5_T - k):
                t = s + k
                wy_ref[0, s * LANES:(s + 1) * LANES, t * LANES:(t + 1) * LANES] = lag
        pc = mul_ac(pc)


def _s5_prep(a_re, a_im, log_dt, b_re, b_im, c_re, c_im):
    gpt = LANES // SSM_GROUP
    eye = jnp.eye(gpt, dtype=bool)

    def rows(p):
        return p.reshape(S5_TILES, 1, S5_TILE_STATES)

    def cols(p):
        return jnp.broadcast_to(p.reshape(S5_TILES, S5_TILE_STATES, 1), (S5_TILES, S5_TILE_STATES, LANES))

    def bmat(b):
        bt = b.reshape(S5_TILES, gpt, SSM_STATE, SSM_GROUP).transpose(0, 1, 3, 2)
        full = jnp.where(eye[None, :, None, :, None], bt[:, :, :, None, :], 0.0)
        return full.reshape(S5_TILES, LANES, S5_TILE_STATES)

    def cmat(c):
        ct = c.reshape(S5_TILES, gpt, SSM_GROUP, SSM_STATE).transpose(0, 1, 3, 2)
        full = jnp.where(eye[None, :, None, :, None], ct[:, :, :, None, :], 0.0)
        return full.reshape(S5_TILES, S5_TILE_STATES, LANES)

    row_spec = pl.BlockSpec((1, 1, S5_TILE_STATES), lambda j: (j, 0, 0))
    b_spec = pl.BlockSpec((1, LANES, S5_TILE_STATES), lambda j: (j, 0, 0))
    c_spec = pl.BlockSpec((1, S5_TILE_STATES, LANES), lambda j: (j, 0, 0))
    tw = S5_T * LANES
    return pl.pallas_call(
        _s5_prep_body,
        grid=(S5_TILES,),
        in_specs=[row_spec] * 3 + [b_spec] * 2 + [c_spec] * 5,
        out_specs=[pl.BlockSpec((1, tw, 2 * S5_TILE_STATES), lambda j: (j, 0, 0)),
                   pl.BlockSpec((1, tw + 2 * S5_TILE_STATES, tw), lambda j: (j, 0, 0))],
        out_shape=[jax.ShapeDtypeStruct((S5_TILES, tw, 2 * S5_TILE_STATES), BF16),
                   jax.ShapeDtypeStruct((S5_TILES, tw + 2 * S5_TILE_STATES, tw), BF16)],
        compiler_params=_params(1),
        name="s5_prep",
    )(rows(a_re), rows(a_im), rows(log_dt), bmat(b_re), bmat(b_im),
      cols(a_re), cols(a_im), cols(log_dt), cmat(c_re), cmat(c_im))


def _s5_pow_body(ex_ref, lr_ref, li_ref, ld_ref, o_ref):
    dt = jnp.exp(ld_ref[...])
    zr, zi = lr_ref[...] * dt, li_ref[...] * dt
    ex = ex_ref[...]
    mag = jnp.exp(ex * zr)
    o_ref[:, :S5_STATES] = mag * jnp.cos(ex * zi)
    o_ref[:, S5_STATES:] = mag * jnp.sin(ex * zi)


def _s5_pow_table(exponents, a_re, a_im, log_dt):
    n = len(exponents)
    ex = jnp.asarray(np.asarray(exponents, np.float32)[:, None])
    flat = lambda p: p.reshape(1, S5_STATES)
    return pl.pallas_call(
        _s5_pow_body,
        grid=(1,),
        in_specs=[pl.BlockSpec((n, 1), lambda i: (0, 0))] + [pl.BlockSpec((1, S5_STATES), lambda i: (0, 0))] * 3,
        out_specs=pl.BlockSpec((n, 2 * S5_STATES), lambda i: (0, 0)),
        out_shape=jax.ShapeDtypeStruct((n, 2 * S5_STATES), F32),
        compiler_params=_params(1),
        name="s5_pow_table",
    )(ex, flat(a_re), flat(a_im), flat(log_dt))


def _s5_body(u_ref, h0_ref, wg_ref, wy_ref, pow2_ref, powr_ref, d_ref, wglu_ref,
             y_ref, hout_ref, carry_ref, us_ref, ys_ref, *, ni, chained):
    ns = S5_STATES
    ts = S5_TILE_STATES
    segtok = ni * S5_T
    pitch = _s5_pitch(ni)

    @pl.when(pl.program_id(1) == 0)
    def _():
        carry_ref[...] = h0_ref[0]

    for j in range(S5_TILES):
        for s in range(SUBLANES):
            us_ref[j, s * pitch:s * pitch + segtok] = u_ref[0, s * segtok:(s + 1) * segtok, j * LANES:(j + 1) * LANES]

    def gather(j, t):
        return jnp.concatenate([us_ref[j, pl.ds(S5_T * i + t, SUBLANES, stride=pitch), :]
                                for i in range(ni)], axis=0)

    def fma(p_r, p_i, x_r, x_i, y_r, y_i):
        return y_r + p_r * x_r - p_i * x_i, y_i + p_r * x_i + p_i * x_r

    xs, g_re, g_im = [], [], []
    for j in range(S5_TILES):
        xj = jnp.concatenate([gather(j, s) for s in range(S5_T)], axis=1).astype(BF16)
        g = _dot(xj, wg_ref[j])
        xs.append(xj)
        g_re.append(g[:, :ts])
        g_im.append(g[:, ts:])
    g_re = jnp.concatenate(g_re, axis=1)
    g_im = jnp.concatenate(g_im, axis=1)
    slab = lambda a, i: a[i * SUBLANES:(i + 1) * SUBLANES]
    power = lambda ref, r: (ref[r:r + 1, :ns], ref[r:r + 1, ns:])
    a_r, a_i = power(powr_ref, 0)
    cin = carry_ref[...]
    h_r, h_i = cin[:, :ns], cin[:, ns:]
    prev = []
    if chained:
        local = [(slab(g_re, 0), slab(g_im, 0))]
        for i in range(1, ni):
            local.append(fma(a_r, a_i, *local[-1], slab(g_re, i), slab(g_im, i)))
        sub = lax.broadcasted_iota(jnp.int32, (SUBLANES, 1), 0)
        c_r = jnp.where(sub == 0, h_r, pltpu.roll(local[-1][0], 1, 0))
        c_i = jnp.where(sub == 0, h_i, pltpu.roll(local[-1][1], 1, 0))
        for m in range(3):
            k = 1 << m
            s_r = jnp.where(sub >= k, pltpu.roll(c_r, k, 0), 0.0)
            s_i = jnp.where(sub >= k, pltpu.roll(c_i, k, 0), 0.0)
            c_r, c_i = fma(*power(pow2_ref, m), s_r, s_i, c_r, c_i)
        h_r, h_i = c_r, c_i
        for i in range(ni):
            prev.append((h_r, h_i))
            h_r, h_i = fma(*power(powr_ref, i), c_r, c_i, *local[i])
        last = jnp.concatenate([h_r[SUBLANES - 1:], h_i[SUBLANES - 1:]], axis=1)
    else:
        for i in range(ni):
            prev.append((h_r, h_i))
            h_r, h_i = fma(a_r, a_i, h_r, h_i, slab(g_re, i), slab(g_im, i))
        last = jnp.concatenate([h_r, h_i], axis=1)
    carry_ref[...] = last
    hout_ref[0] = last
    prev_re = jnp.concatenate([p[0] for p in prev], axis=0)
    prev_im = jnp.concatenate([p[1] for p in prev], axis=0)
    y_tiles = []
    for j in range(S5_TILES):
        lhs = jnp.concatenate([xs[j], prev_re[:, j * ts:(j + 1) * ts].astype(BF16),
                               prev_im[:, j * ts:(j + 1) * ts].astype(BF16)], axis=1)
        y_tiles.append(_dot(lhs, wy_ref[j]))
    d = d_ref[...]
    wglu = wglu_ref[0]
    for t in range(S5_T):
        ut = jnp.concatenate([gather(j, t) for j in range(S5_TILES)], axis=1)
        yt = jnp.concatenate([y_tiles[j][:, t * LANES:(t + 1) * LANES] for j in range(S5_TILES)], axis=1)
        g = jax.nn.gelu(yt + d * ut)
        o = g * jax.nn.sigmoid(_dot(g.astype(BF16), wglu))
        for j in range(S5_TILES):
            for i in range(ni):
                ys_ref[j, pl.ds(S5_T * i + t, SUBLANES, stride=pitch), :] = slab(o, i)[:, j * LANES:(j + 1) * LANES]
    for j in range(S5_TILES):
        for s in range(SUBLANES):
            y_ref[0, s * segtok:(s + 1) * segtok, j * LANES:(j + 1) * LANES] = (
                ys_ref[j, s * pitch:s * pitch + segtok].astype(BF16))


def _s5_pitch(ni):
    segtok = ni * S5_T
    return segtok + SUBLANES if (segtok // SUBLANES) % 2 == 0 else segtok


def _s5_mixer(u, h0, wg, wy, pow2, powr, d, w_glu, layer_e, ni, chained):
    nb, ntok, width = u.shape
    tb = SUBLANES * ni * S5_T
    hrows = h0.shape[1]
    blk = pl.BlockSpec((1, tb, width), lambda b, i: (b, i, 0))
    state = pl.BlockSpec((1, hrows, 2 * S5_STATES), lambda b, i: (b, 0, 0))
    return pl.pallas_call(
        functools.partial(_s5_body, ni=ni, chained=chained),
        grid=(nb, ntok // tb),
        in_specs=[blk, state, _const_spec(wg.shape), _const_spec(wy.shape),
                  _const_spec(pow2.shape), _const_spec(powr.shape), _const_spec((1, SSM_WIDTH)),
                  _const_spec((1, SSM_WIDTH, SSM_WIDTH), (layer_e, 0, 0))],
        out_specs=[blk, state],
        out_shape=[jax.ShapeDtypeStruct((nb, ntok, width), BF16),
                   jax.ShapeDtypeStruct((nb, hrows, 2 * S5_STATES), F32)],
        scratch_shapes=[pltpu.VMEM((hrows, 2 * S5_STATES), F32)]
        + [pltpu.VMEM((S5_TILES, SUBLANES * _s5_pitch(ni), LANES), F32)] * 2,
        compiler_params=_params(2),
        name="s5_mixer",
    )(u, h0, wg, wy, pow2, powr, d, w_glu)


def _swa_body(sink_ref, q_ref, kc_ref, vc_ref, kp_ref, vp_ref, pk_ref, pv_ref, o_ref, *, tq, qg, start):
    first = pl.program_id(1) == 0
    k_prev = jnp.where(first, pk_ref[0], kp_ref[0])
    v_prev = jnp.where(first, pv_ref[0], vp_ref[0])
    k_all = jnp.concatenate([k_prev, kc_ref[0]], axis=0)
    v_all = jnp.concatenate([v_prev, vc_ref[0]], axis=0)
    nk = WINDOW + qg
    lane = lax.broadcasted_iota(jnp.int32, (1, KV_WIDTH), 1)
    slot = lax.broadcasted_iota(jnp.int32, (1, nk), 1)
    qrow = lax.broadcasted_iota(jnp.int32, (2 * qg, 1), 0)
    lower = lane < HEAD_DIM
    placed = []
    for kh in range(N_KV_HEADS):
        own = lower if kh == 0 else jnp.logical_not(lower)
        kz, vz = jnp.where(own, k_all, 0.0), jnp.where(own, v_all, 1.0)
        kr, vr = pltpu.roll(kz, HEAD_DIM, 1), pltpu.roll(vz, HEAD_DIM, 1)
        k_lo, k_hi = (kz, kr) if kh == 0 else (kr, kz)
        v_lo, v_hi = (vz, vr) if kh == 0 else (vr, vz)
        placed.append(((k_lo.astype(BF16), v_lo.astype(BF16)), (k_hi.astype(BF16), v_hi.astype(BF16))))
    combos = [(kh, par) for kh in range(N_KV_HEADS) for par in range(2)]

    def scores(c):
        qc = q_ref[0, c * qg:(c + 1) * qg, :]
        out = []
        for kh, par in combos:
            base = kh * GQ * HEAD_DIM
            qs = jnp.concatenate([qc[:, base:base + LANES], qc[:, base + LANES:base + 2 * LANES]], axis=0)
            out.append(_dot_t(qs, placed[kh][par][0][c * qg:c * qg + nk]))
        return out

    def softmax(c, ss):
        n_bad = jnp.where(first, WINDOW - start - c * qg, 0) if c * qg < WINDOW - start else None
        probs, sinks = [], []
        for (kh, par), s in zip(combos, ss):
            if n_bad is not None:
                s = jnp.where(slot >= n_bad, s, NEG)
            sink = jnp.where(qrow < qg, sink_ref[kh * GQ + par], sink_ref[kh * GQ + 2 + par])
            m = jnp.maximum(jnp.max(s, axis=1, keepdims=True), sink)
            probs.append(jnp.exp(s - m).astype(BF16))
            sinks.append(jnp.exp(sink - m))
        return probs, sinks

    def finish(c, probs, sinks):
        pvs = [_dot(p, placed[kh][par][1][c * qg:c * qg + nk]) for (kh, par), p in zip(combos, probs)]
        halves = [pv / (pltpu.roll(pv, HEAD_DIM, 1) + sk) for pv, sk in zip(pvs, sinks)]
        outs = []
        for kh in range(N_KV_HEADS):
            acc = jnp.where(lower, halves[2 * kh], halves[2 * kh + 1])
            outs += [acc[:qg], acc[qg:]]
        o_ref[0, c * qg:(c + 1) * qg, :] = jnp.concatenate(outs, axis=1).astype(BF16)

    nchunk = tq // qg
    s_q, p_q = {}, {}
    for step in range(nchunk + 2):
        if step < nchunk:
            s_q[step] = scores(step)
        if 0 <= step - 1 < nchunk:
            p_q[step - 1] = softmax(step - 1, s_q.pop(step - 1))
        if 0 <= step - 2 < nchunk:
            finish(step - 2, *p_q.pop(step - 2))


def _swa(q, k, v, past_k, past_v, sinks, start, tq, qg):
    bsz, seq, _ = q.shape
    ntiles = seq // tq
    tok = lambda w: pl.BlockSpec((1, tq, w), lambda b, i: (b, i, 0))
    past = pl.BlockSpec((1, WINDOW, KV_WIDTH), lambda b, i: (b, 0, 0))
    if ntiles > 1:
        per = tq // WINDOW
        prev = pl.BlockSpec((1, WINDOW, KV_WIDTH), lambda b, i: (b, jnp.maximum(i * per - 1, 0), 0))
        k_prev, v_prev = k, v
    else:
        prev, k_prev, v_prev = past, past_k, past_v
    return pl.pallas_call(
        functools.partial(_swa_body, tq=tq, qg=qg, start=start),
        grid=(bsz, ntiles),
        in_specs=[pl.BlockSpec(memory_space=pltpu.SMEM), tok(N_HEADS * HEAD_DIM), tok(KV_WIDTH), tok(KV_WIDTH),
                  prev, prev, past, past],
        out_specs=tok(N_HEADS * HEAD_DIM),
        out_shape=jax.ShapeDtypeStruct((bsz, seq, N_HEADS * HEAD_DIM), BF16),
        compiler_params=_params(2),
        name="swa",
    )(sinks, q, k, v, k_prev, v_prev, past_k, past_v)


def _xattn(x1, g, wq, mk_ref, mv_ref, wo, nseq):
    h = _rms(x1, g).astype(BF16)
    q = _dot(h, wq).astype(BF16)
    scale = X_HEAD_DIM ** -0.5
    rows = x1.shape[0] // nseq
    parts = [(b, slice(b * rows, (b + 1) * rows), slice(hd * X_HEAD_DIM, (hd + 1) * X_HEAD_DIM))
             for b in range(nseq) for hd in range(X_HEADS)]
    scores = [_dot_t(q[r, c], mk_ref[0, b, :, c]) * scale for b, r, c in parts]
    probs, dens = [], []
    for s in scores:
        p = jnp.exp(s - jnp.max(s, axis=1, keepdims=True))
        probs.append(p.astype(BF16))
        dens.append(jnp.sum(p, axis=1, keepdims=True))
    outs = [_dot(p, mv_ref[0, b, :, c]) / den for (b, r, c), p, den in zip(parts, probs, dens)]
    o = jnp.concatenate([jnp.concatenate(outs[b * X_HEADS:(b + 1) * X_HEADS], axis=1) for b in range(nseq)], axis=0)
    return x1 + _dot(o.astype(BF16), wo)


def _conv3(cur, carry, w):
    n = cur.shape[0]
    row = lax.broadcasted_iota(jnp.int32, (n, 1), 0)
    c1, c2 = carry[SUBLANES - 1:SUBLANES], carry[SUBLANES - 2:SUBLANES - 1]
    m1 = jnp.where(row == 0, c1, pltpu.roll(cur, 1, 0))
    m2 = jnp.where(row == 0, c2, jnp.where(row == 1, c1, pltpu.roll(cur, 2, 0)))
    return w[0:1] * m2 + w[1:2] * m1 + w[2:3] * cur


def _conv3_seqs(cur, carries, w):
    nseq = len(carries)
    if nseq == 1:
        return _conv3(cur, carries[0], w)
    rows = cur.shape[0] // nseq
    body = _conv3(cur, carries[0], w)
    pieces = []
    for b in range(nseq):
        pieces.append(_conv3(cur[b * rows:b * rows + SUBLANES], carries[b], w))
        pieces.append(body[b * rows + SUBLANES:(b + 1) * rows])
    return jnp.concatenate(pieces, axis=0)


def _store_last_rows(st_ref, cols, cur, nseq):
    rows = cur.shape[0] // nseq
    for b in range(nseq):
        st_ref[b, :, cols] = cur[(b + 1) * rows - (CONV_WIDTH - 1):(b + 1) * rows]


def _mix_even_xattn_body(x_ref, ya_ref, yb_ref, woa_ref, wob_ref, g_ref, wq_ref, mk_ref, mv_ref, wo_ref, o_ref,
                         *, nseq):
    x1 = x_ref[0] + _dot(ya_ref[0], woa_ref[0]) + _dot(yb_ref[0], wob_ref[0])
    o_ref[0] = _xattn(x1, g_ref[...], wq_ref[0], mk_ref, mv_ref, wo_ref[0], nseq)


def _mix_even_xattn(x, ya, yb, w_out, g, wq, mem_k, mem_v, wo, layer, layer_e, tm, nseq):
    bsz, seq, _ = x.shape
    tok = lambda w: pl.BlockSpec((1, tm, w), lambda b, i: (b, i, 0))
    mem = pl.BlockSpec((1, nseq, N_MEM, D_MODEL), lambda b, i: (layer, b, 0, 0))
    half = D_MODEL // 2
    sq = lambda: _const_spec((1, D_MODEL, D_MODEL), (layer, 0, 0))
    return pl.pallas_call(
        functools.partial(_mix_even_xattn_body, nseq=nseq),
        grid=(bsz, seq // tm),
        in_specs=[tok(D_MODEL), tok(half), tok(half),
                  _const_spec((1, half, D_MODEL), (layer_e, 0, 0)), _const_spec((1, half, D_MODEL), (layer_e, 1, 0)),
                  _const_spec((1, D_MODEL)), sq(), mem, mem, sq()],
        out_specs=tok(D_MODEL),
        out_shape=jax.ShapeDtypeStruct(x.shape, F32),
        compiler_params=_params(2),
        name="mix_even_xattn",
    )(x, ya, yb, w_out, w_out, g, wq, mem_k, mem_v, wo)


def _mix_odd_xattn_body(x_ref, gm_ref, win_ref, cw_ref, wout_ref, prev_ref, g_ref, wq_ref, mk_ref, mv_ref, wo_ref,
                        o_ref, st_ref, carry_ref, *, tm, nseq):
    @pl.when(pl.program_id(1) == 0)
    def _():
        carry_ref[...] = prev_ref[0]

    x = x_ref[0]
    proj = _dot(_rms(x, gm_ref[...]).astype(BF16), win_ref[0])
    gate_b, gate_c, z = proj[:, :D_MODEL], proj[:, D_MODEL:2 * D_MODEL], proj[:, 2 * D_MODEL:]
    cz = gate_c * z
    carries = [carry_ref[...]] if nseq == 1 else [prev_ref[b] for b in range(nseq)]
    z_conv = _conv3_seqs(cz, carries, cw_ref[0])
    carry_ref[...] = cz[tm - SUBLANES:]
    _store_last_rows(st_ref, slice(None), cz, nseq)
    x1 = x + _dot((gate_b * z_conv).astype(BF16), wout_ref[0])
    o_ref[0] = _xattn(x1, g_ref[...], wq_ref[0], mk_ref, mv_ref, wo_ref[0], nseq)


def _mix_odd_xattn(x, gm, w_in, conv_w, w_out, prev, g, wq, mem_k, mem_v, wo, layer, layer_o, tm, nseq):
    bsz, seq, _ = x.shape
    tok = pl.BlockSpec((1, tm, D_MODEL), lambda b, i: (b, i, 0))
    mem = pl.BlockSpec((1, nseq, N_MEM, D_MODEL), lambda b, i: (layer, b, 0, 0))
    hist = pl.BlockSpec((nseq, SUBLANES, D_MODEL), lambda b, i: (b, 0, 0))
    st = pl.BlockSpec((nseq, CONV_WIDTH - 1, D_MODEL), lambda b, i: (b, 0, 0))
    sq = lambda l: _const_spec((1, D_MODEL, D_MODEL), (l, 0, 0))
    return pl.pallas_call(
        functools.partial(_mix_odd_xattn_body, tm=tm, nseq=nseq),
        grid=(bsz, seq // tm),
        in_specs=[tok, _const_spec((1, D_MODEL)), _const_spec((1, D_MODEL, 3 * D_MODEL), (layer_o, 0, 0)),
                  _const_spec((1, CONV_WIDTH, D_MODEL), (layer_o, 0, 0)), sq(layer_o), hist,
                  _const_spec((1, D_MODEL)), sq(layer), mem, mem, sq(layer)],
        out_specs=[tok, st],
        out_shape=[jax.ShapeDtypeStruct(x.shape, F32),
                   jax.ShapeDtypeStruct((bsz * nseq, CONV_WIDTH - 1, D_MODEL), F32)],
        scratch_shapes=[pltpu.VMEM((SUBLANES, D_MODEL), F32)],
        compiler_params=_params(2),
        name="mix_odd_xattn",
    )(x, gm, w_in, conv_w, w_out, prev, g, wq, mem_k, mem_v, wo)


def _ffn_body(x_ref, g_ref, wup_ref, cw_ref, wdn_ref, prev_ref, gfin_ref, o_ref, st_ref, carry_ref,
              *, tm, final, nseq):
    @pl.when(pl.program_id(1) == 0)
    def _():
        carry_ref[...] = prev_ref[0]

    x = x_ref[0]
    h = _rms(x, g_ref[...]).astype(BF16)
    acc = x
    for lo in range(0, D_FF, FF_CHUNK):
        cols = slice(lo, min(lo + FF_CHUNK, D_FF))
        gate = _dot(h, wup_ref[0, :, cols])
        val = _dot(h, wup_ref[0, :, D_FF + cols.start:D_FF + cols.stop])
        carries = [carry_ref[:, cols]] if nseq == 1 else [prev_ref[b, :, cols] for b in range(nseq)]
        gate_c = _conv3_seqs(gate, carries, cw_ref[0, :, cols])
        carry_ref[:, cols] = gate[tm - SUBLANES:]
        _store_last_rows(st_ref, cols, gate, nseq)
        act = gate_c * jax.nn.sigmoid(gate_c) * val
        acc = acc + _dot(act.astype(BF16), wdn_ref[0, cols, :])
    o_ref[0] = _rms(acc, gfin_ref[...]) if final else acc


def _ffn(x, g, w_up, conv_w, w_down, prev, g_final, layer, tm, final, nseq):
    bsz, seq, _ = x.shape
    tok = pl.BlockSpec((1, tm, D_MODEL), lambda b, i: (b, i, 0))
    hist = pl.BlockSpec((nseq, SUBLANES, D_FF), lambda b, i: (b, 0, 0))
    st = pl.BlockSpec((nseq, CONV_WIDTH - 1, D_FF), lambda b, i: (b, 0, 0))
    return pl.pallas_call(
        functools.partial(_ffn_body, tm=tm, final=final, nseq=nseq),
        grid=(bsz, seq // tm),
        in_specs=[tok, _const_spec((1, D_MODEL)), _const_spec((1, D_MODEL, 2 * D_FF), (layer, 0, 0)),
                  _const_spec((1, CONV_WIDTH, D_FF), (layer, 0, 0)), _const_spec((1, D_FF, D_MODEL), (layer, 0, 0)),
                  hist, _const_spec((1, D_MODEL))],
        out_specs=[tok, st],
        out_shape=[jax.ShapeDtypeStruct(x.shape, F32),
                   jax.ShapeDtypeStruct((bsz * nseq, CONV_WIDTH - 1, D_FF), F32)],
        scratch_shapes=[pltpu.VMEM((SUBLANES, D_FF), F32)],
        compiler_params=_params(2),
        name="conv_ffn",
    )(x, g, w_up, conv_w, w_down, prev, g_final)


def _mem_kv_body(m_ref, w_ref, k_ref, v_ref, kf_ref, vf_ref):
    kv = _dot(m_ref[0].astype(BF16), w_ref[0])
    kf_ref[0, 0] = kv[:, :D_MODEL].astype(BF16)
    vf_ref[0, 0] = kv[:, D_MODEL:].astype(BF16)
    for hd in range(X_HEADS):
        k_ref[0, 0, :, hd, :] = kv[:, hd * X_HEAD_DIM:(hd + 1) * X_HEAD_DIM]
        v_ref[0, 0, :, hd, :] = kv[:, D_MODEL + hd * X_HEAD_DIM:D_MODEL + (hd + 1) * X_HEAD_DIM]


def _mem_kv(mem, w_kv):
    bsz = mem.shape[0]
    heads = pl.BlockSpec((1, 1, N_MEM, X_HEADS, X_HEAD_DIM), lambda l, b: (l, b, 0, 0, 0))
    flat = pl.BlockSpec((1, 1, N_MEM, D_MODEL), lambda l, b: (l, b, 0, 0))
    return pl.pallas_call(
        _mem_kv_body,
        grid=(DEPTH, bsz),
        in_specs=[pl.BlockSpec((1, N_MEM, D_MODEL), lambda l, b: (b, 0, 0)),
                  pl.BlockSpec((1, D_MODEL, 2 * D_MODEL), lambda l, b: (l, 0, 0))],
        out_specs=[heads, heads, flat, flat],
        out_shape=[jax.ShapeDtypeStruct((DEPTH, bsz, N_MEM, X_HEADS, X_HEAD_DIM), F32)] * 2
        + [jax.ShapeDtypeStruct((DEPTH, bsz, N_MEM, D_MODEL), BF16)] * 2,
        compiler_params=_params(2),
        name="mem_kv",
    )(mem, w_kv)


def _trunk(x, start, past_k, past_v, ssm_re, ssm_im, conv_mix_prev, conv_ffn_prev, mem_k, mem_v, w, s5, cfg):
    bsz, seq, _ = x.shape
    tm, tq, qg, nseq = cfg["tm"], cfg["tq"], cfg["qg"], cfg["nseq"]
    ni, chained = cfg["ni"], cfg["chained"]
    ngrp, grows = bsz // nseq, nseq * seq
    grouped = lambda t: t.reshape(ngrp, grows, t.shape[-1])
    per_seq = lambda t: t.reshape(bsz, seq, t.shape[-1])
    hist = lambda st: jnp.pad(st, ((0, 0), (SUBLANES - (CONV_WIDTH - 1), 0), (0, 0)))
    tables = _rope_tables(grows, seq, start, tm)
    u, q, k, v = _inproj_even(grouped(x), w["norm_mix"][0:1], w["w_in_even"], tables, 0, tm)
    h0 = jnp.concatenate([ssm_re[0].reshape(bsz, S5_STATES), ssm_im[0].reshape(bsz, S5_STATES)], axis=1)
    if chained:
        ya, h_all = _s5_mixer(u, h0[:, None, :], s5["wg"], s5["wy"], cfg["pow2"], cfg["powr"], w["ssm_d"],
                              w["w_glu"], 0, ni, True)
        h_last = h_all[:, 0]
    else:
        assert ngrp == 1 and bsz == SUBLANES and seq == ni * S5_T
        ya, h_all = _s5_mixer(u, h0[None], s5["wg"], s5["wy"], cfg["pow2"], cfg["powr"], w["ssm_d"],
                              w["w_glu"], 0, ni, False)
        h_last = h_all[0]
    new_re = h_last[:, :S5_STATES].reshape(1, bsz, SSM_GROUPS, SSM_STATE)
    new_im = h_last[:, S5_STATES:].reshape(1, bsz, SSM_GROUPS, SSM_STATE)
    k, v = per_seq(k), per_seq(v)
    yb = _swa(per_seq(q), k, v, past_k[0].reshape(bsz, WINDOW, KV_WIDTH), past_v[0].reshape(bsz, WINDOW, KV_WIDTH),
              w["attn_sinks"][0], start, tq, qg)
    xg = _mix_even_xattn(grouped(x), ya, grouped(yb), w["w_out_even"], w["norm_xattn"][0:1], w["xattn_wq"],
                         mem_k, mem_v, w["xattn_wo"], 0, 0, tm, nseq)
    xg, cf0 = _ffn(xg, w["norm_ffn"][0:1], w["ffn_w_up"], w["ffn_conv_w"], w["ffn_w_down"], hist(conv_ffn_prev[0]),
                   w["norm_final"], 0, tm, False, nseq)
    xg, cm = _mix_odd_xattn(xg, w["norm_mix"][1:2], w["w_in_odd"], w["conv_mix_w"], w["w_out_odd"],
                            hist(conv_mix_prev[0]), w["norm_xattn"][1:2], w["xattn_wq"], mem_k, mem_v,
                            w["xattn_wo"], 1, 0, tm, nseq)
    y, cf1 = _ffn(xg, w["norm_ffn"][1:2], w["ffn_w_up"], w["ffn_conv_w"], w["ffn_w_down"], hist(conv_ffn_prev[1]),
                  w["norm_final"], 1, tm, True, nseq)
    keep = min(seq, WINDOW)
    new_k = k[:, seq - keep:].reshape(1, bsz, keep, N_KV_HEADS, HEAD_DIM)
    new_v = v[:, seq - keep:].reshape(1, bsz, keep, N_KV_HEADS, HEAD_DIM)
    return per_seq(y), new_k, new_v, new_re, new_im, cm[None], jnp.stack([cf0, cf1])


def kernel(x_prompt, x_sample, mem_prompt, cache_win_k, cache_win_v, state_ssm_re, state_ssm_im, state_conv_mix, state_conv_ffn, cache_mem_k, cache_mem_v, norm_mix, norm_xattn, norm_ffn, norm_final, w_in_even, w_out_even, ssm_a_re, ssm_a_im, ssm_log_dt, ssm_b_re, ssm_b_im, ssm_c_re, ssm_c_im, ssm_d, w_glu, attn_sinks, w_in_odd, conv_mix_w, w_out_odd, xattn_wq, xattn_wkv, xattn_wo, ffn_w_up, ffn_conv_w, ffn_w_down):
    bp, seq_p, _ = x_prompt.shape
    bs, seq_s, _ = x_sample.shape
    w = dict(norm_mix=norm_mix, norm_xattn=norm_xattn, norm_ffn=norm_ffn, norm_final=norm_final.reshape(1, D_MODEL),
             w_in_even=w_in_even.astype(BF16), w_out_even=w_out_even.astype(BF16),
             ssm_d=ssm_d[0].reshape(1, SSM_WIDTH), w_glu=w_glu.astype(BF16), attn_sinks=attn_sinks,
             w_in_odd=w_in_odd.astype(BF16), conv_mix_w=conv_mix_w, w_out_odd=w_out_odd.astype(BF16),
             xattn_wq=xattn_wq.astype(BF16), xattn_wo=xattn_wo.astype(BF16),
             ffn_w_up=ffn_w_up.astype(BF16), ffn_conv_w=ffn_conv_w, ffn_w_down=ffn_w_down.astype(BF16))

    wg, wy = _s5_prep(ssm_a_re[0], ssm_a_im[0], ssm_log_dt[0], ssm_b_re[0], ssm_b_im[0], ssm_c_re[0], ssm_c_im[0])
    s5 = dict(wg=wg, wy=wy)
    ni_p, ni_s = S5_SEG_CHUNKS, seq_s // S5_T
    padded = lambda e: e + [1.0] * (-len(e) % SUBLANES)
    exps = (padded([float(S5_T * ni_p << m) for m in range(3)])
            + padded([float(S5_T * (i + 1)) for i in range(ni_p)])
            + padded([float(S5_T * (i + 1)) for i in range(ni_s)]))
    pows = _s5_pow_table(exps, ssm_a_re[0], ssm_a_im[0], ssm_log_dt[0])
    o1 = SUBLANES
    o2 = o1 + len(padded([0.0] * ni_p))
    cfg_p = dict(tm=512, tq=512, qg=CHUNK, nseq=1, ni=ni_p, chained=True, pow2=pows[:o1], powr=pows[o1:o2])
    cfg_s = dict(tm=bs * seq_s, tq=seq_s, qg=seq_s, nseq=bs, ni=ni_s, chained=False, pow2=pows[:o1], powr=pows[o2:])

    mem_k_p, mem_v_p, mem_kf_p, mem_vf_p = _mem_kv(mem_prompt, xattn_wkv.astype(BF16))
    cw = CONV_WIDTH - 1
    zk = jnp.zeros((1, bp, WINDOW, N_KV_HEADS, HEAD_DIM), F32)
    zs = jnp.zeros((1, bp, SSM_GROUPS, SSM_STATE), F32)
    zcm = jnp.zeros((1, bp, cw, D_MODEL), F32)
    zcf = jnp.zeros((DEPTH, bp, cw, D_FF), F32)
    y_p, k_p, v_p, re_p, im_p, cm_p, cf_p = _trunk(x_prompt, 0, zk, zk, zs, zs, zcm, zcf, mem_kf_p, mem_vf_p, w, s5, cfg_p)
    mem_kf_s = cache_mem_k.reshape(DEPTH, bs, N_MEM, D_MODEL).astype(BF16)
    mem_vf_s = cache_mem_v.reshape(DEPTH, bs, N_MEM, D_MODEL).astype(BF16)
    y_s, k_s, v_s, re_s, im_s, cm_s, cf_s = _trunk(x_sample, PAST_LEN, cache_win_k, cache_win_v, state_ssm_re,
                                                   state_ssm_im, state_conv_mix, state_conv_ffn, mem_kf_s,
                                                   mem_vf_s, w, s5, cfg_s)
    return (y_p, y_s, k_p, v_p, re_p, im_p, cm_p, cf_p, mem_k_p, mem_v_p, k_s, v_s, re_s, im_s, cm_s, cf_s)
```

```python
import functools
import math

import numpy as np
import jax
import jax.numpy as jnp
from jax import lax
from jax.experimental import pallas as pl
from jax.experimental.pallas import tpu as pltpu

F32 = jnp.float32
BF16 = jnp.bfloat16

D_MODEL = 1024
DEPTH = 2
PAST_LEN = 2048
CHUNK = 64
SSM_WIDTH = 512
SSM_GROUP = 16
SSM_GROUPS = 32
SSM_STATE = 64
HEAD_DIM = 64
N_HEADS = 8
N_KV_HEADS = 2
GQ = N_HEADS // N_KV_HEADS
WINDOW = 128
ROPE_THETA = 10000.0
KV_WIDTH = N_KV_HEADS * HEAD_DIM
EVEN_IN = SSM_WIDTH + N_HEADS * HEAD_DIM + 2 * KV_WIDTH
CONV_WIDTH = 3
N_MEM = 256
X_HEADS = 4
X_HEAD_DIM = 256
D_FF = 2816
EPS = 1e-6
NEG = -1e30

LANES = 128
SUBLANES = 8
S5_T = 8
S5_TILES = SSM_WIDTH // LANES
S5_TILE_STATES = (LANES // SSM_GROUP) * SSM_STATE
S5_STATES = SSM_GROUPS * SSM_STATE
S5_SEG_CHUNKS = 16
MXU_DIM = 256
FF_CHUNK = 4 * MXU_DIM
VMEM_LIMIT = 56 * 1024 * 1024


def _const_spec(shape, index=None):
    idx = tuple(index) if index is not None else (0,) * len(shape)
    return pl.BlockSpec(shape, lambda *_: idx, pipeline_mode=pl.Buffered(1))


def _params(n_axes):
    return pltpu.CompilerParams(dimension_semantics=("arbitrary",) * n_axes,
                                vmem_limit_bytes=VMEM_LIMIT)


def _rms(x, g):
    return x * lax.rsqrt(jnp.mean(x * x, axis=-1, keepdims=True) + EPS) * g


def _dot(a, b):
    return jnp.dot(a, b, preferred_element_type=F32)


def _dot_t(a, b):
    return lax.dot_general(a, b, (((1,), (1,)), ((), ())), preferred_element_type=F32)


def _rope_table_body(cos_ref, sina_ref, sinb_ref, *, start, tl, period):
    i = pl.program_id(0)
    row = lax.broadcasted_iota(jnp.int32, (tl, LANES), 0)
    lane = lax.broadcasted_iota(jnp.int32, (tl, LANES), 1)
    pos = (start + ((i * tl + row) & (period - 1))).astype(F32)
    half = HEAD_DIM // 2
    freq = (lane[0:1] & (half - 1)).astype(F32)
    inv = jnp.exp(freq * (-2.0 / HEAD_DIM * math.log(ROPE_THETA)))
    ang = pos * inv
    c = jnp.cos(ang)
    s = jnp.sin(ang)
    first = (lane & (HEAD_DIM - 1)) < HEAD_DIM // 2
    cos_ref[...] = c
    sina_ref[...] = jnp.where(first, -s, 0.0)
    sinb_ref[...] = jnp.where(first, 0.0, s)


def _rope_tables(rows, period, start, tl):
    spec = pl.BlockSpec((tl, LANES), lambda i: (i, 0))
    return pl.pallas_call(
        functools.partial(_rope_table_body, start=start, tl=tl, period=period),
        grid=(rows // tl,),
        in_specs=[],
        out_specs=[spec, spec, spec],
        out_shape=[jax.ShapeDtypeStruct((rows, LANES), F32)] * 3,
        compiler_params=_params(1),
        name="rope_tables",
    )()


def _rope(t, cos, sina, sinb):
    n = t.shape[1]
    half = HEAD_DIM // 2
    return t * cos + pltpu.roll(t, n - half, 1) * sina + pltpu.roll(t, half, 1) * sinb


def _inproj_even_body(x_ref, g_ref, w_ref, cos_ref, sa_ref, sb_ref, u_ref, q_ref, k_ref, v_ref):
    h = _rms(x_ref[0], g_ref[...]).astype(BF16)
    proj = _dot(h, w_ref[0])
    o0 = SSM_WIDTH
    o1 = o0 + N_HEADS * HEAD_DIM
    o2 = o1 + KV_WIDTH
    cos, sa, sb = cos_ref[...], sa_ref[...], sb_ref[...]
    rep = (o1 - o0) // LANES
    cos_q = jnp.concatenate([cos] * rep, axis=1)
    sa_q = jnp.concatenate([sa] * rep, axis=1)
    sb_q = jnp.concatenate([sb] * rep, axis=1)
    u_ref[0] = proj[:, :o0]
    q_ref[0] = (_rope(proj[:, o0:o1], cos_q, sa_q, sb_q) * HEAD_DIM ** -0.5).astype(BF16)
    k_ref[0] = _rope(proj[:, o1:o2], cos, sa, sb)
    v_ref[0] = proj[:, o2:]


def _inproj_even(x, g, w_in, tables, layer_e, tm):
    bsz, seq, _ = x.shape
    cos, sa, sb = tables
    tok = lambda w: pl.BlockSpec((1, tm, w), lambda b, i: (b, i, 0))
    tab = pl.BlockSpec((tm, LANES), lambda b, i: (i, 0))
    return pl.pallas_call(
        _inproj_even_body,
        grid=(bsz, seq // tm),
        in_specs=[tok(D_MODEL), _const_spec((1, D_MODEL)),
                  _const_spec((1, D_MODEL, EVEN_IN), (layer_e, 0, 0)), tab, tab, tab],
        out_specs=[tok(SSM_WIDTH), tok(N_HEADS * HEAD_DIM), tok(KV_WIDTH), tok(KV_WIDTH)],
        out_shape=[jax.ShapeDtypeStruct((bsz, seq, SSM_WIDTH), F32),
                   jax.ShapeDtypeStruct((bsz, seq, N_HEADS * HEAD_DIM), BF16),
                   jax.ShapeDtypeStruct((bsz, seq, KV_WIDTH), F32),
                   jax.ShapeDtypeStruct((bsz, seq, KV_WIDTH), F32)],
        compiler_params=_params(2),
        name="inproj_even",
    )(x, g, w_in, cos, sa, sb)


def _s5_prep_body(lr_ref, li_ref, ld_ref, bre_ref, bim_ref, lrc_ref, lic_ref, ldc_ref,
                  cre_ref, cim_ref, wg_ref, wy_ref):
    ts = S5_TILE_STATES
    tw = S5_T * LANES

    def step(z_r, z_i):
        mag = jnp.exp(z_r)
        a_r, a_i = mag * jnp.cos(z_i), mag * jnp.sin(z_i)
        return a_r, a_i, lambda p: (p[0] * a_r - p[1] * a_i, p[0] * a_i + p[1] * a_r)

    lr, li = lr_ref[0], li_ref[0]
    dt = jnp.exp(ld_ref[0])
    ar, ai, mul_a = step(lr * dt, li * dt)
    nrm = lr * lr + li * li
    cbr = ((ar - 1.0) * lr + ai * li) / nrm
    cbi = (ai * lr - (ar - 1.0) * li) / nrm
    b_re, b_im = bre_ref[0], bim_ref[0]
    bbr = cbr * b_re - cbi * b_im
    bbi = cbr * b_im + cbi * b_re
    p = (jnp.ones_like(ar), jnp.zeros_like(ar))
    for s in reversed(range(S5_T)):
        rows = slice(s * LANES, (s + 1) * LANES)
        wg_ref[0, rows, :ts] = (p[0] * bbr - p[1] * bbi).astype(BF16)
        wg_ref[0, rows, ts:] = (p[0] * bbi + p[1] * bbr).astype(BF16)
        for t in range(s):
            wy_ref[0, rows, t * LANES:(t + 1) * LANES] = jnp.zeros((LANES, LANES), BF16)
        p = mul_a(p)
    dtc = jnp.exp(ldc_ref[0])
    _, _, mul_ac = step(lrc_ref[0] * dtc, lic_ref[0] * dtc)
    c_re, c_im = cre_ref[0], cim_ref[0]
    bb = jnp.concatenate([bbr, bbi], axis=1).astype(BF16)
    pc = (jnp.ones_like(dtc), jnp.zeros_like(dtc))
    for k in range(S5_T + 1):
        blk = jnp.concatenate([pc[0] * c_re - pc[1] * c_im, -(pc[0] * c_im + pc[1] * c_re)], axis=0).astype(BF16)
        if k >= 1:
            wy_ref[0, tw:, (k - 1) * LANES:k * LANES] = blk
        if k < S5_T:
            lag = _dot(bb, blk).astype(BF16)
            for s in range(S5_T - k):
                t = s + k
                wy_ref[0, s * LANES:(s + 1) * LANES, t * LANES:(t + 1) * LANES] = lag
        pc = mul_ac(pc)


def _s5_prep(a_re, a_im, log_dt, b_re, b_im, c_re, c_im):
    gpt = LANES // SSM_GROUP
    eye = jnp.eye(gpt, dtype=bool)

    def rows(p):
        return p.reshape(S5_TILES, 1, S5_TILE_STATES)

    def cols(p):
        return jnp.broadcast_to(p.reshape(S5_TILES, S5_TILE_STATES, 1), (S5_TILES, S5_TILE_STATES, LANES))

    def bmat(b):
        bt = b.reshape(S5_TILES, gpt, SSM_STATE, SSM_GROUP).transpose(0, 1, 3, 2)
        full = jnp.where(eye[None, :, None, :, None], bt[:, :, :, None, :], 0.0)
        return full.reshape(S5_TILES, LANES, S5_TILE_STATES)

    def cmat(c):
        ct = c.reshape(S5_TILES, gpt, SSM_GROUP, SSM_STATE).transpose(0, 1, 3, 2)
        full = jnp.where(eye[None, :, None, :, None], ct[:, :, :, None, :], 0.0)
        return full.reshape(S5_TILES, S5_TILE_STATES, LANES)

    row_spec = pl.BlockSpec((1, 1, S5_TILE_STATES), lambda j: (j, 0, 0))
    b_spec = pl.BlockSpec((1, LANES, S5_TILE_STATES), lambda j: (j, 0, 0))
    c_spec = pl.BlockSpec((1, S5_TILE_STATES, LANES), lambda j: (j, 0, 0))
    tw = S5_T * LANES
    return pl.pallas_call(
        _s5_prep_body,
        grid=(S5_TILES,),
        in_specs=[row_spec] * 3 + [b_spec] * 2 + [c_spec] * 5,
        out_specs=[pl.BlockSpec((1, tw, 2 * S5_TILE_STATES), lambda j: (j, 0, 0)),
                   pl.BlockSpec((1, tw + 2 * S5_TILE_STATES, tw), lambda j: (j, 0, 0))],
        out_shape=[jax.ShapeDtypeStruct((S5_TILES, tw, 2 * S5_TILE_STATES), BF16),
                   jax.ShapeDtypeStruct((S5_TILES, tw + 2 * S5_TILE_STATES, tw), BF16)],
        compiler_params=_params(1),
        name="s5_prep",
    )(rows(a_re), rows(a_im), rows(log_dt), bmat(b_re), bmat(b_im),
      cols(a_re), cols(a_im), cols(log_dt), cmat(c_re), cmat(c_im))


def _s5_pow_body(ex_ref, lr_ref, li_ref, ld_ref, o_ref):
    dt = jnp.exp(ld_ref[...])
    zr, zi = lr_ref[...] * dt, li_ref[...] * dt
    ex = ex_ref[...]
    mag = jnp.exp(ex * zr)
    o_ref[:, :S5_STATES] = mag * jnp.cos(ex * zi)
    o_ref[:, S5_STATES:] = mag * jnp.sin(ex * zi)


def _s5_pow_table(exponents, a_re, a_im, log_dt):
    n = len(exponents)
    ex = jnp.asarray(np.asarray(exponents, np.float32)[:, None])
    flat = lambda p: p.reshape(1, S5_STATES)
    return pl.pallas_call(
        _s5_pow_body,
        grid=(1,),
        in_specs=[pl.BlockSpec((n, 1), lambda i: (0, 0))] + [pl.BlockSpec((1, S5_STATES), lambda i: (0, 0))] * 3,
        out_specs=pl.BlockSpec((n, 2 * S5_STATES), lambda i: (0, 0)),
        out_shape=jax.ShapeDtypeStruct((n, 2 * S5_STATES), F32),
        compiler_params=_params(1),
        name="s5_pow_table",
    )(ex, flat(a_re), flat(a_im), flat(log_dt))


def _s5_body(u_ref, h0_ref, wg_ref, wy_ref, pow2_ref, powr_ref, d_ref, wglu_ref,
             y_ref, hout_ref, carry_ref, us_ref, ys_ref, *, ni, chained):
    ns = S5_STATES
    ts = S5_TILE_STATES
    segtok = ni * S5_T
    pitch = _s5_pitch(ni)

    @pl.when(pl.program_id(1) == 0)
    def _():
        carry_ref[...] = h0_ref[0]

    for j in range(S5_TILES):
        for s in range(SUBLANES):
            us_ref[j, s * pitch:s * pitch + segtok] = u_ref[0, s * segtok:(s + 1) * segtok, j * LANES:(j + 1) * LANES]

    def gather(j, t):
        return jnp.concatenate([us_ref[j, pl.ds(S5_T * i + t, SUBLANES, stride=pitch), :]
                                for i in range(ni)], axis=0)

    def fma(p_r, p_i, x_r, x_i, y_r, y_i):
        return y_r + p_r * x_r - p_i * x_i, y_i + p_r * x_i + p_i * x_r

    xs, g_re, g_im = [], [], []
    for j in range(S5_TILES):
        xj = jnp.concatenate([gather(j, s) for s in range(S5_T)], axis=1).astype(BF16)
        g = _dot(xj, wg_ref[j])
        xs.append(xj)
        g_re.append(g[:, :ts])
        g_im.append(g[:, ts:])
    g_re = jnp.concatenate(g_re, axis=1)
    g_im = jnp.concatenate(g_im, axis=1)
    slab = lambda a, i: a[i * SUBLANES:(i + 1) * SUBLANES]
    power = lambda ref, r: (ref[r:r + 1, :ns], ref[r:r + 1, ns:])
    a_r, a_i = power(powr_ref, 0)
    cin = carry_ref[...]
    h_r, h_i = cin[:, :ns], cin[:, ns:]
    prev = []
    if chained:
        local = [(slab(g_re, 0), slab(g_im, 0))]
        for i in range(1, ni):
            local.append(fma(a_r, a_i, *local[-1], slab(g_re, i), slab(g_im, i)))
        sub = lax.broadcasted_iota(jnp.int32, (SUBLANES, 1), 0)
        c_r = jnp.where(sub == 0, h_r, pltpu.roll(local[-1][0], 1, 0))
        c_i = jnp.where(sub == 0, h_i, pltpu.roll(local[-1][1], 1, 0))
        for m in range(3):
            k = 1 << m
            s_r = jnp.where(sub >= k, pltpu.roll(c_r, k, 0), 0.0)
            s_i = jnp.where(sub >= k, pltpu.roll(c_i, k, 0), 0.0)
            c_r, c_i = fma(*power(pow2_ref, m), s_r, s_i, c_r, c_i)
        h_r, h_i = c_r, c_i
        for i in range(ni):
            prev.append((h_r, h_i))
            h_r, h_i = fma(*power(powr_ref, i), c_r, c_i, *local[i])
        last = jnp.concatenate([h_r[SUBLANES - 1:], h_i[SUBLANES - 1:]], axis=1)
    else:
        for i in range(ni):
            prev.append((h_r, h_i))
            h_r, h_i = fma(a_r, a_i, h_r, h_i, slab(g_re, i), slab(g_im, i))
        last = jnp.concatenate([h_r, h_i], axis=1)
    carry_ref[...] = last
    hout_ref[0] = last
    prev_re = jnp.concatenate([p[0] for p in prev], axis=0)
    prev_im = jnp.concatenate([p[1] for p in prev], axis=0)
    y_tiles = []
    for j in range(S5_TILES):
        lhs = jnp.concatenate([xs[j], prev_re[:, j * ts:(j + 1) * ts].astype(BF16),
                               prev_im[:, j * ts:(j + 1) * ts].astype(BF16)], axis=1)
        y_tiles.append(_dot(lhs, wy_ref[j]))
    d = d_ref[...]
    wglu = wglu_ref[0]
    for t in range(S5_T):
        ut = jnp.concatenate([gather(j, t) for j in range(S5_TILES)], axis=1)
        yt = jnp.concatenate([y_tiles[j][:, t * LANES:(t + 1) * LANES] for j in range(S5_TILES)], axis=1)
        g = jax.nn.gelu(yt + d * ut)
        o = g * jax.nn.sigmoid(_dot(g.astype(BF16), wglu))
        for j in range(S5_TILES):
            for i in range(ni):
                ys_ref[j, pl.ds(S5_T * i + t, SUBLANES, stride=pitch), :] = slab(o, i)[:, j * LANES:(j + 1) * LANES]
    for j in range(S5_TILES):
        for s in range(SUBLANES):
            y_ref[0, s * segtok:(s + 1) * segtok, j * LANES:(j + 1) * LANES] = (
                ys_ref[j, s * pitch:s * pitch + segtok].astype(BF16))


def _s5_pitch(ni):
    segtok = ni * S5_T
    return segtok + SUBLANES if (segtok // SUBLANES) % 2 == 0 else segtok


def _s5_mixer(u, h0, wg, wy, pow2, powr, d, w_glu, layer_e, ni, chained):
    nb, ntok, width = u.shape
    tb = SUBLANES * ni * S5_T
    hrows = h0.shape[1]
    blk = pl.BlockSpec((1, tb, width), lambda b, i: (b, i, 0))
    state = pl.BlockSpec((1, hrows, 2 * S5_STATES), lambda b, i: (b, 0, 0))
    return pl.pallas_call(
        functools.partial(_s5_body, ni=ni, chained=chained),
        grid=(nb, ntok // tb),
        in_specs=[blk, state, _const_spec(wg.shape), _const_spec(wy.shape),
                  _const_spec(pow2.shape), _const_spec(powr.shape), _const_spec((1, SSM_WIDTH)),
                  _const_spec((1, SSM_WIDTH, SSM_WIDTH), (layer_e, 0, 0))],
        out_specs=[blk, state],
        out_shape=[jax.ShapeDtypeStruct((nb, ntok, width), BF16),
                   jax.ShapeDtypeStruct((nb, hrows, 2 * S5_STATES), F32)],
        scratch_shapes=[pltpu.VMEM((hrows, 2 * S5_STATES), F32)]
        + [pltpu.VMEM((S5_TILES, SUBLANES * _s5_pitch(ni), LANES), F32)] * 2,
        compiler_params=_params(2),
        name="s5_mixer",
    )(u, h0, wg, wy, pow2, powr, d, w_glu)


def _swa_body(sink_ref, q_ref, kc_ref, vc_ref, kp_ref, vp_ref, pk_ref, pv_ref, o_ref, *, tq, qg, start):
    first = pl.program_id(1) == 0
    k_prev = jnp.where(first, pk_ref[0], kp_ref[0])
    v_prev = jnp.where(first, pv_ref[0], vp_ref[0])
    k_all = jnp.concatenate([k_prev, kc_ref[0]], axis=0)
    v_all = jnp.concatenate([v_prev, vc_ref[0]], axis=0)
    nk = WINDOW + qg
    lane = lax.broadcasted_iota(jnp.int32, (1, KV_WIDTH), 1)
    slot = lax.broadcasted_iota(jnp.int32, (1, nk), 1)
    qrow = lax.broadcasted_iota(jnp.int32, (2 * qg, 1), 0)
    lower = lane < HEAD_DIM
    placed = []
    for kh in range(N_KV_HEADS):
        own = lower if kh == 0 else jnp.logical_not(lower)
        kz, vz = jnp.where(own, k_all, 0.0), jnp.where(own, v_all, 1.0)
        kr, vr = pltpu.roll(kz, HEAD_DIM, 1), pltpu.roll(vz, HEAD_DIM, 1)
        k_lo, k_hi = (kz, kr) if kh == 0 else (kr, kz)
        v_lo, v_hi = (vz, vr) if kh == 0 else (vr, vz)
        placed.append(((k_lo.astype(BF16), v_lo.astype(BF16)), (k_hi.astype(BF16), v_hi.astype(BF16))))
    combos = [(kh, par) for kh in range(N_KV_HEADS) for par in range(2)]

    def scores(c):
        qc = q_ref[0, c * qg:(c + 1) * qg, :]
        out = []
        for kh, par in combos:
            base = kh * GQ * HEAD_DIM
            qs = jnp.concatenate([qc[:, base:base + LANES], qc[:, base + LANES:base + 2 * LANES]], axis=0)
            out.append(_dot_t(qs, placed[kh][par][0][c * qg:c * qg + nk]))
        return out

    def softmax(c, ss):
        n_bad = jnp.where(first, WINDOW - start - c * qg, 0) if c * qg < WINDOW - start else None
        probs, sinks = [], []
        for (kh, par), s in zip(combos, ss):
            if n_bad is not None:
                s = jnp.where(slot >= n_bad, s, NEG)
            sink = jnp.where(qrow < qg, sink_ref[kh * GQ + par], sink_ref[kh * GQ + 2 + par])
            m = jnp.maximum(jnp.max(s, axis=1, keepdims=True), sink)
            probs.append(jnp.exp(s - m).astype(BF16))
            sinks.append(jnp.exp(sink - m))
        return probs, sinks

    def finish(c, probs, sinks):
        pvs = [_dot(p, placed[kh][par][1][c * qg:c * qg + nk]) for (kh, par), p in zip(combos, probs)]
        halves = [pv / (pltpu.roll(pv, HEAD_DIM, 1) + sk) for pv, sk in zip(pvs, sinks)]
        outs = []
        for kh in range(N_KV_HEADS):
            acc = jnp.where(lower, halves[2 * kh], halves[2 * kh + 1])
            outs += [acc[:qg], acc[qg:]]
        o_ref[0, c * qg:(c + 1) * qg, :] = jnp.concatenate(outs, axis=1).astype(BF16)

    nchunk = tq // qg
    s_q, p_q = {}, {}
    for step in range(nchunk + 2):
        if step < nchunk:
            s_q[step] = scores(step)
        if 0 <= step - 1 < nchunk:
            p_q[step - 1] = softmax(step - 1, s_q.pop(step - 1))
        if 0 <= step - 2 < nchunk:
            finish(step - 2, *p_q.pop(step - 2))


def _swa(q, k, v, past_k, past_v, sinks, start, tq, qg):
    bsz, seq, _ = q.shape
    ntiles = seq // tq
    tok = lambda w: pl.BlockSpec((1, tq, w), lambda b, i: (b, i, 0))
    past = pl.BlockSpec((1, WINDOW, KV_WIDTH), lambda b, i: (b, 0, 0))
    if ntiles > 1:
        per = tq // WINDOW
        prev = pl.BlockSpec((1, WINDOW, KV_WIDTH), lambda b, i: (b, jnp.maximum(i * per - 1, 0), 0))
        k_prev, v_prev = k, v
    else:
        prev, k_prev, v_prev = past, past_k, past_v
    return pl.pallas_call(
        functools.partial(_swa_body, tq=tq, qg=qg, start=start),
        grid=(bsz, ntiles),
        in_specs=[pl.BlockSpec(memory_space=pltpu.SMEM), tok(N_HEADS * HEAD_DIM), tok(KV_WIDTH), tok(KV_WIDTH),
                  prev, prev, past, past],
        out_specs=tok(N_HEADS * HEAD_DIM),
        out_shape=jax.ShapeDtypeStruct((bsz, seq, N_HEADS * HEAD_DIM), BF16),
        compiler_params=_params(2),
        name="swa",
    )(sinks, q, k, v, k_prev, v_prev, past_k, past_v)


def _xattn(x1, g, wq, mk_ref, mv_ref, wo, nseq):
    h = _rms(x1, g).astype(BF16)
    q = _dot(h, wq).astype(BF16)
    scale = X_HEAD_DIM ** -0.5
    rows = x1.shape[0] // nseq
    parts = [(b, slice(b * rows, (b + 1) * rows), slice(hd * X_HEAD_DIM, (hd + 1) * X_HEAD_DIM))
             for b in range(nseq) for hd in range(X_HEADS)]
    scores = [_dot_t(q[r, c], mk_ref[0, b, :, c]) * scale for b, r, c in parts]
    probs, dens = [], []
    for s in scores:
        p = jnp.exp(s - jnp.max(s, axis=1, keepdims=True))
        probs.append(p.astype(BF16))
        dens.append(jnp.sum(p, axis=1, keepdims=True))
    outs = [_dot(p, mv_ref[0, b, :, c]) / den for (b, r, c), p, den in zip(parts, probs, dens)]
    o = jnp.concatenate([jnp.concatenate(outs[b * X_HEADS:(b + 1) * X_HEADS], axis=1) for b in range(nseq)], axis=0)
    return x1 + _dot(o.astype(BF16), wo)


def _conv3(cur, carry, w):
    n = cur.shape[0]
    row = lax.broadcasted_iota(jnp.int32, (n, 1), 0)
    c1, c2 = carry[SUBLANES - 1:SUBLANES], carry[SUBLANES - 2:SUBLANES - 1]
    m1 = jnp.where(row == 0, c1, pltpu.roll(cur, 1, 0))
    m2 = jnp.where(row == 0, c2, jnp.where(row == 1, c1, pltpu.roll(cur, 2, 0)))
    return w[0:1] * m2 + w[1:2] * m1 + w[2:3] * cur


def _conv3_seqs(cur, carries, w):
    nseq = len(carries)
    if nseq == 1:
        return _conv3(cur, carries[0], w)
    rows = cur.shape[0] // nseq
    body = _conv3(cur, carries[0], w)
    pieces = []
    for b in range(nseq):
        pieces.append(_conv3(cur[b * rows:b * rows + SUBLANES], carries[b], w))
        pieces.append(body[b * rows + SUBLANES:(b + 1) * rows])
    return jnp.concatenate(pieces, axis=0)


def _store_last_rows(st_ref, cols, cur, nseq):
    rows = cur.shape[0] // nseq
    for b in range(nseq):
        st_ref[b, :, cols] = cur[(b + 1) * rows - (CONV_WIDTH - 1):(b + 1) * rows]


def _mix_even_xattn_body(x_ref, ya_ref, yb_ref, woa_ref, wob_ref, g_ref, wq_ref, mk_ref, mv_ref, wo_ref, o_ref,
                         *, nseq):
    x1 = x_ref[0] + _dot(ya_ref[0], woa_ref[0]) + _dot(yb_ref[0], wob_ref[0])
    o_ref[0] = _xattn(x1, g_ref[...], wq_ref[0], mk_ref, mv_ref, wo_ref[0], nseq)


def _mix_even_xattn(x, ya, yb, w_out, g, wq, mem_k, mem_v, wo, layer, layer_e, tm, nseq):
    bsz, seq, _ = x.shape
    tok = lambda w: pl.BlockSpec((1, tm, w), lambda b, i: (b, i, 0))
    mem = pl.BlockSpec((1, nseq, N_MEM, D_MODEL), lambda b, i: (layer, b, 0, 0))
    half = D_MODEL // 2
    sq = lambda: _const_spec((1, D_MODEL, D_MODEL), (layer, 0, 0))
    return pl.pallas_call(
        functools.partial(_mix_even_xattn_body, nseq=nseq),
        grid=(bsz, seq // tm),
        in_specs=[tok(D_MODEL), tok(half), tok(half),
                  _const_spec((1, half, D_MODEL), (layer_e, 0, 0)), _const_spec((1, half, D_MODEL), (layer_e, 1, 0)),
                  _const_spec((1, D_MODEL)), sq(), mem, mem, sq()],
        out_specs=tok(D_MODEL),
        out_shape=jax.ShapeDtypeStruct(x.shape, F32),
        compiler_params=_params(2),
        name="mix_even_xattn",
    )(x, ya, yb, w_out, w_out, g, wq, mem_k, mem_v, wo)


def _mix_odd_xattn_body(x_ref, gm_ref, win_ref, cw_ref, wout_ref, prev_ref, g_ref, wq_ref, mk_ref, mv_ref, wo_ref,
                        o_ref, st_ref, carry_ref, *, tm, nseq):
    @pl.when(pl.program_id(1) == 0)
    def _():
        carry_ref[...] = prev_ref[0]

    x = x_ref[0]
    proj = _dot(_rms(x, gm_ref[...]).astype(BF16), win_ref[0])
    gate_b, gate_c, z = proj[:, :D_MODEL], proj[:, D_MODEL:2 * D_MODEL], proj[:, 2 * D_MODEL:]
    cz = gate_c * z
    carries = [carry_ref[...]] if nseq == 1 else [prev_ref[b] for b in range(nseq)]
    z_conv = _conv3_seqs(cz, carries, cw_ref[0])
    carry_ref[...] = cz[tm - SUBLANES:]
    _store_last_rows(st_ref, slice(None), cz, nseq)
    x1 = x + _dot((gate_b * z_conv).astype(BF16), wout_ref[0])
    o_ref[0] = _xattn(x1, g_ref[...], wq_ref[0], mk_ref, mv_ref, wo_ref[0], nseq)


def _mix_odd_xattn(x, gm, w_in, conv_w, w_out, prev, g, wq, mem_k, mem_v, wo, layer, layer_o, tm, nseq):
    bsz, seq, _ = x.shape
    tok = pl.BlockSpec((1, tm, D_MODEL), lambda b, i: (b, i, 0))
    mem = pl.BlockSpec((1, nseq, N_MEM, D_MODEL), lambda b, i: (layer, b, 0, 0))
    hist = pl.BlockSpec((nseq, SUBLANES, D_MODEL), lambda b, i: (b, 0, 0))
    st = pl.BlockSpec((nseq, CONV_WIDTH - 1, D_MODEL), lambda b, i: (b, 0, 0))
    sq = lambda l: _const_spec((1, D_MODEL, D_MODEL), (l, 0, 0))
    return pl.pallas_call(
        functools.partial(_mix_odd_xattn_body, tm=tm, nseq=nseq),
        grid=(bsz, seq // tm),
        in_specs=[tok, _const_spec((1, D_MODEL)), _const_spec((1, D_MODEL, 3 * D_MODEL), (layer_o, 0, 0)),
                  _const_spec((1, CONV_WIDTH, D_MODEL), (layer_o, 0, 0)), sq(layer_o), hist,
                  _const_spec((1, D_MODEL)), sq(layer), mem, mem, sq(layer)],
        out_specs=[tok, st],
        out_shape=[jax.ShapeDtypeStruct(x.shape, F32),
                   jax.ShapeDtypeStruct((bsz * nseq, CONV_WIDTH - 1, D_MODEL), F32)],
        scratch_shapes=[pltpu.VMEM((SUBLANES, D_MODEL), F32)],
        compiler_params=_params(2),
        name="mix_odd_xattn",
    )(x, gm, w_in, conv_w, w_out, prev, g, wq, mem_k, mem_v, wo)


def _ffn_body(x_ref, g_ref, wup_ref, cw_ref, wdn_ref, prev_ref, gfin_ref, o_ref, st_ref, carry_ref,
              *, tm, final, nseq):
    @pl.when(pl.program_id(1) == 0)
    def _():
        carry_ref[...] = prev_ref[0]

    x = x_ref[0]
    h = _rms(x, g_ref[...]).astype(BF16)
    acc = x
    for lo in range(0, D_FF, FF_CHUNK):
        cols = slice(lo, min(lo + FF_CHUNK, D_FF))
        gate = _dot(h, wup_ref[0, :, cols])
        val = _dot(h, wup_ref[0, :, D_FF + cols.start:D_FF + cols.stop])
        carries = [carry_ref[:, cols]] if nseq == 1 else [prev_ref[b, :, cols] for b in range(nseq)]
        gate_c = _conv3_seqs(gate, carries, cw_ref[0, :, cols])
        carry_ref[:, cols] = gate[tm - SUBLANES:]
        _store_last_rows(st_ref, cols, gate, nseq)
        act = gate_c * jax.nn.sigmoid(gate_c) * val
        acc = acc + _dot(act.astype(BF16), wdn_ref[0, cols, :])
    o_ref[0] = _rms(acc, gfin_ref[...]) if final else acc


def _ffn(x, g, w_up, conv_w, w_down, prev, g_final, layer, tm, final, nseq):
    bsz, seq, _ = x.shape
    tok = pl.BlockSpec((1, tm, D_MODEL), lambda b, i: (b, i, 0))
    hist = pl.BlockSpec((nseq, SUBLANES, D_FF), lambda b, i: (b, 0, 0))
    st = pl.BlockSpec((nseq, CONV_WIDTH - 1, D_FF), lambda b, i: (b, 0, 0))
    return pl.pallas_call(
        functools.partial(_ffn_body, tm=tm, final=final, nseq=nseq),
        grid=(bsz, seq // tm),
        in_specs=[tok, _const_spec((1, D_MODEL)), _const_spec((1, D_MODEL, 2 * D_FF), (layer, 0, 0)),
                  _const_spec((1, CONV_WIDTH, D_FF), (layer, 0, 0)), _const_spec((1, D_FF, D_MODEL), (layer, 0, 0)),
                  hist, _const_spec((1, D_MODEL))],
        out_specs=[tok, st],
        out_shape=[jax.ShapeDtypeStruct(x.shape, F32),
                   jax.ShapeDtypeStruct((bsz * nseq, CONV_WIDTH - 1, D_FF), F32)],
        scratch_shapes=[pltpu.VMEM((SUBLANES, D_FF), F32)],
        compiler_params=_params(2),
        name="conv_ffn",
    )(x, g, w_up, conv_w, w_down, prev, g_final)


def _mem_kv_body(m_ref, w_ref, k_ref, v_ref, kf_ref, vf_ref):
    kv = _dot(m_ref[0].astype(BF16), w_ref[0])
    kf_ref[0, 0] = kv[:, :D_MODEL].astype(BF16)
    vf_ref[0, 0] = kv[:, D_MODEL:].astype(BF16)
    for hd in range(X_HEADS):
        k_ref[0, 0, :, hd, :] = kv[:, hd * X_HEAD_DIM:(hd + 1) * X_HEAD_DIM]
        v_ref[0, 0, :, hd, :] = kv[:, D_MODEL + hd * X_HEAD_DIM:D_MODEL + (hd + 1) * X_HEAD_DIM]


def _mem_kv(mem, w_kv):
    bsz = mem.shape[0]
    heads = pl.BlockSpec((1, 1, N_MEM, X_HEADS, X_HEAD_DIM), lambda l, b: (l, b, 0, 0, 0))
    flat = pl.BlockSpec((1, 1, N_MEM, D_MODEL), lambda l, b: (l, b, 0, 0))
    return pl.pallas_call(
        _mem_kv_body,
        grid=(DEPTH, bsz),
        in_specs=[pl.BlockSpec((1, N_MEM, D_MODEL), lambda l, b: (b, 0, 0)),
                  pl.BlockSpec((1, D_MODEL, 2 * D_MODEL), lambda l, b: (l, 0, 0))],
        out_specs=[heads, heads, flat, flat],
        out_shape=[jax.ShapeDtypeStruct((DEPTH, bsz, N_MEM, X_HEADS, X_HEAD_DIM), F32)] * 2
        + [jax.ShapeDtypeStruct((DEPTH, bsz, N_MEM, D_MODEL), BF16)] * 2,
        compiler_params=_params(2),
        name="mem_kv",
    )(mem, w_kv)


def _trunk(x, start, past_k, past_v, ssm_re, ssm_im, conv_mix_prev, conv_ffn_prev, mem_k, mem_v, w, s5, cfg):
    bsz, seq, _ = x.shape
    tq, qg, nseq = cfg["tq"], cfg["qg"], cfg["nseq"]
    tm_in, tm_even, tm_odd, tm_ffn = cfg["tm_in"], cfg["tm_even"], cfg["tm_odd"], cfg["tm_ffn"]
    ni, chained = cfg["ni"], cfg["chained"]
    ngrp, grows = bsz // nseq, nseq * seq
    grouped = lambda t: t.reshape(ngrp, grows, t.shape[-1])
    per_seq = lambda t: t.reshape(bsz, seq, t.shape[-1])
    hist = lambda st: jnp.pad(st, ((0, 0), (SUBLANES - (CONV_WIDTH - 1), 0), (0, 0)))
    tables = _rope_tables(grows, seq, start, tm_in)
    u, q, k, v = _inproj_even(grouped(x), w["norm_mix"][0:1], w["w_in_even"], tables, 0, tm_in)
    h0 = jnp.concatenate([ssm_re[0].reshape(bsz, S5_STATES), ssm_im[0].reshape(bsz, S5_STATES)], axis=1)
    if chained:
        ya, h_all = _s5_mixer(u, h0[:, None, :], s5["wg"], s5["wy"], cfg["pow2"], cfg["powr"], w["ssm_d"],
                              w["w_glu"], 0, ni, True)
        h_last = h_all[:, 0]
    else:
        assert ngrp == 1 and bsz == SUBLANES and seq == ni * S5_T
        ya, h_all = _s5_mixer(u, h0[None], s5["wg"], s5["wy"], cfg["pow2"], cfg["powr"], w["ssm_d"],
                              w["w_glu"], 0, ni, False)
        h_last = h_all[0]
    new_re = h_last[:, :S5_STATES].reshape(1, bsz, SSM_GROUPS, SSM_STATE)
    new_im = h_last[:, S5_STATES:].reshape(1, bsz, SSM_GROUPS, SSM_STATE)
    k, v = per_seq(k), per_seq(v)
    yb = _swa(per_seq(q), k, v, past_k[0].reshape(bsz, WINDOW, KV_WIDTH), past_v[0].reshape(bsz, WINDOW, KV_WIDTH),
              w["attn_sinks"][0], start, tq, qg)
    xg = _mix_even_xattn(grouped(x), ya, grouped(yb), w["w_out_even"], w["norm_xattn"][0:1], w["xattn_wq"],
                         mem_k, mem_v, w["xattn_wo"], 0, 0, tm_even, nseq)
    xg, cf0 = _ffn(xg, w["norm_ffn"][0:1], w["ffn_w_up"], w["ffn_conv_w"], w["ffn_w_down"], hist(conv_ffn_prev[0]),
                   w["norm_final"], 0, tm_ffn, False, nseq)
    xg, cm = _mix_odd_xattn(xg, w["norm_mix"][1:2], w["w_in_odd"], w["conv_mix_w"], w["w_out_odd"],
                            hist(conv_mix_prev[0]), w["norm_xattn"][1:2], w["xattn_wq"], mem_k, mem_v,
                            w["xattn_wo"], 1, 0, tm_odd, nseq)
    y, cf1 = _ffn(xg, w["norm_ffn"][1:2], w["ffn_w_up"], w["ffn_conv_w"], w["ffn_w_down"], hist(conv_ffn_prev[1]),
                  w["norm_final"], 1, tm_ffn, True, nseq)
    keep = min(seq, WINDOW)
    new_k = k[:, seq - keep:].reshape(1, bsz, keep, N_KV_HEADS, HEAD_DIM)
    new_v = v[:, seq - keep:].reshape(1, bsz, keep, N_KV_HEADS, HEAD_DIM)
    return per_seq(y), new_k, new_v, new_re, new_im, cm[None], jnp.stack([cf0, cf1])


def kernel(x_prompt, x_sample, mem_prompt, cache_win_k, cache_win_v, state_ssm_re, state_ssm_im, state_conv_mix, state_conv_ffn, cache_mem_k, cache_mem_v, norm_mix, norm_xattn, norm_ffn, norm_final, w_in_even, w_out_even, ssm_a_re, ssm_a_im, ssm_log_dt, ssm_b_re, ssm_b_im, ssm_c_re, ssm_c_im, ssm_d, w_glu, attn_sinks, w_in_odd, conv_mix_w, w_out_odd, xattn_wq, xattn_wkv, xattn_wo, ffn_w_up, ffn_conv_w, ffn_w_down):
    bp, seq_p, _ = x_prompt.shape
    bs, seq_s, _ = x_sample.shape
    w = dict(norm_mix=norm_mix, norm_xattn=norm_xattn, norm_ffn=norm_ffn, norm_final=norm_final.reshape(1, D_MODEL),
             w_in_even=w_in_even.astype(BF16), w_out_even=w_out_even.astype(BF16),
             ssm_d=ssm_d[0].reshape(1, SSM_WIDTH), w_glu=w_glu.astype(BF16), attn_sinks=attn_sinks,
             w_in_odd=w_in_odd.astype(BF16), conv_mix_w=conv_mix_w, w_out_odd=w_out_odd.astype(BF16),
             xattn_wq=xattn_wq.astype(BF16), xattn_wo=xattn_wo.astype(BF16),
             ffn_w_up=ffn_w_up.astype(BF16), ffn_conv_w=ffn_conv_w, ffn_w_down=ffn_w_down.astype(BF16))

    wg, wy = _s5_prep(ssm_a_re[0], ssm_a_im[0], ssm_log_dt[0], ssm_b_re[0], ssm_b_im[0], ssm_c_re[0], ssm_c_im[0])
    s5 = dict(wg=wg, wy=wy)
    ni_p, ni_s = S5_SEG_CHUNKS, seq_s // S5_T
    padded = lambda e: e + [1.0] * (-len(e) % SUBLANES)
    exps = (padded([float(S5_T * ni_p << m) for m in range(3)])
            + padded([float(S5_T * (i + 1)) for i in range(ni_p)])
            + padded([float(S5_T * (i + 1)) for i in range(ni_s)]))
    pows = _s5_pow_table(exps, ssm_a_re[0], ssm_a_im[0], ssm_log_dt[0])
    o1 = SUBLANES
    o2 = o1 + len(padded([0.0] * ni_p))
    cfg_p = dict(tm_in=1024, tm_even=1024, tm_odd=1024, tm_ffn=1024, tq=512, qg=CHUNK, nseq=1, ni=ni_p, chained=True,
                 pow2=pows[:o1], powr=pows[o1:o2])
    rows_s = bs * seq_s
    cfg_s = dict(tm_in=rows_s, tm_even=rows_s, tm_odd=rows_s, tm_ffn=rows_s, tq=seq_s, qg=seq_s, nseq=bs, ni=ni_s,
                 chained=False, pow2=pows[:o1], powr=pows[o2:])

    mem_k_p, mem_v_p, mem_kf_p, mem_vf_p = _mem_kv(mem_prompt, xattn_wkv.astype(BF16))
    cw = CONV_WIDTH - 1
    zk = jnp.zeros((1, bp, WINDOW, N_KV_HEADS, HEAD_DIM), F32)
    zs = jnp.zeros((1, bp, SSM_GROUPS, SSM_STATE), F32)
    zcm = jnp.zeros((1, bp, cw, D_MODEL), F32)
    zcf = jnp.zeros((DEPTH, bp, cw, D_FF), F32)
    y_p, k_p, v_p, re_p, im_p, cm_p, cf_p = _trunk(x_prompt, 0, zk, zk, zs, zs, zcm, zcf, mem_kf_p, mem_vf_p, w, s5, cfg_p)
    mem_kf_s = cache_mem_k.reshape(DEPTH, bs, N_MEM, D_MODEL).astype(BF16)
    mem_vf_s = cache_mem_v.reshape(DEPTH, bs, N_MEM, D_MODEL).astype(BF16)
    y_s, k_s, v_s, re_s, im_s, cm_s, cf_s = _trunk(x_sample, PAST_LEN, cache_win_k, cache_win_v, state_ssm_re,
                                                   state_ssm_im, state_conv_mix, state_conv_ffn, mem_kf_s,
                                                   mem_vf_s, w, s5, cfg_s)
    return (y_p, y_s, k_p, v_p, re_p, im_p, cm_p, cf_p, mem_k_p, mem_v_p, k_s, v_s, re_s, im_s, cm_s, cf_s)
```

```python
import functools
import math

import numpy as np
import jax
import jax.numpy as jnp
from jax import lax
from jax.experimental import pallas as pl
from jax.experimental.pallas import tpu as pltpu

F32 = jnp.float32
BF16 = jnp.bfloat16

D_MODEL = 1024
DEPTH = 2
PAST_LEN = 2048
CHUNK = 64
SSM_WIDTH = 512
SSM_GROUP = 16
SSM_GROUPS = 32
SSM_STATE = 64
HEAD_DIM = 64
N_HEADS = 8
N_KV_HEADS = 2
GQ = N_HEADS // N_KV_HEADS
WINDOW = 128
ROPE_THETA = 10000.0
KV_WIDTH = N_KV_HEADS * HEAD_DIM
EVEN_IN = SSM_WIDTH + N_HEADS * HEAD_DIM + 2 * KV_WIDTH
CONV_WIDTH = 3
N_MEM = 256
X_HEADS = 4
X_HEAD_DIM = 256
D_FF = 2816
EPS = 1e-6
NEG = -1e30

LANES = 128
SUBLANES = 8
S5_T = 8
S5_TILES = SSM_WIDTH // LANES
S5_TILE_STATES = (LANES // SSM_GROUP) * SSM_STATE
S5_STATES = SSM_GROUPS * SSM_STATE
SWA_STAGE_CHUNKS = 1
S5_SEG_CHUNKS = 16
MXU_DIM = 256
FF_CHUNK = 4 * MXU_DIM
VMEM_LIMIT = 56 * 1024 * 1024


def _const_spec(shape, index=None):
    idx = tuple(index) if index is not None else (0,) * len(shape)
    return pl.BlockSpec(shape, lambda *_: idx, pipeline_mode=pl.Buffered(1))


def _params(n_axes):
    return pltpu.CompilerParams(dimension_semantics=("arbitrary",) * n_axes,
                                vmem_limit_bytes=VMEM_LIMIT)


def _rms(x, g):
    return x * lax.rsqrt(jnp.mean(x * x, axis=-1, keepdims=True) + EPS) * g


def _dot(a, b):
    return jnp.dot(a, b, preferred_element_type=F32)


def _dot_t(a, b):
    return lax.dot_general(a, b, (((1,), (1,)), ((), ())), preferred_element_type=F32)


def _rope_table_body(cos_ref, sina_ref, sinb_ref, *, start, tl, period):
    i = pl.program_id(0)
    row = lax.broadcasted_iota(jnp.int32, (tl, LANES), 0)
    lane = lax.broadcasted_iota(jnp.int32, (tl, LANES), 1)
    pos = (start + ((i * tl + row) & (period - 1))).astype(F32)
    half = HEAD_DIM // 2
    freq = (lane[0:1] & (half - 1)).astype(F32)
    inv = jnp.exp(freq * (-2.0 / HEAD_DIM * math.log(ROPE_THETA)))
    ang = pos * inv
    c = jnp.cos(ang)
    s = jnp.sin(ang)
    first = (lane & (HEAD_DIM - 1)) < HEAD_DIM // 2
    cos_ref[...] = c
    sina_ref[...] = jnp.where(first, -s, 0.0)
    sinb_ref[...] = jnp.where(first, 0.0, s)


def _rope_tables(rows, period, start, tl):
    spec = pl.BlockSpec((tl, LANES), lambda i: (i, 0))
    return pl.pallas_call(
        functools.partial(_rope_table_body, start=start, tl=tl, period=period),
        grid=(rows // tl,),
        in_specs=[],
        out_specs=[spec, spec, spec],
        out_shape=[jax.ShapeDtypeStruct((rows, LANES), F32)] * 3,
        compiler_params=_params(1),
        name="rope_tables",
    )()


def _rope(t, cos, sina, sinb):
    n = t.shape[1]
    half = HEAD_DIM // 2
    return t * cos + pltpu.roll(t, n - half, 1) * sina + pltpu.roll(t, half, 1) * sinb


def _inproj_even_body(x_ref, g_ref, w_ref, cos_ref, sa_ref, sb_ref, u_ref, q_ref, k_ref, v_ref):
    h = _rms(x_ref[0], g_ref[...]).astype(BF16)
    proj = _dot(h, w_ref[0])
    o0 = SSM_WIDTH
    o1 = o0 + N_HEADS * HEAD_DIM
    o2 = o1 + KV_WIDTH
    cos, sa, sb = cos_ref[...], sa_ref[...], sb_ref[...]
    rep = (o1 - o0) // LANES
    cos_q = jnp.concatenate([cos] * rep, axis=1)
    sa_q = jnp.concatenate([sa] * rep, axis=1)
    sb_q = jnp.concatenate([sb] * rep, axis=1)
    u_ref[0] = proj[:, :o0]
    q_ref[0] = (_rope(proj[:, o0:o1], cos_q, sa_q, sb_q) * HEAD_DIM ** -0.5).astype(BF16)
    k_ref[0] = _rope(proj[:, o1:o2], cos, sa, sb)
    v_ref[0] = proj[:, o2:]


def _inproj_even(x, g, w_in, tables, layer_e, tm):
    bsz, seq, _ = x.shape
    cos, sa, sb = tables
    tok = lambda w: pl.BlockSpec((1, tm, w), lambda b, i: (b, i, 0))
    tab = pl.BlockSpec((tm, LANES), lambda b, i: (i, 0))
    return pl.pallas_call(
        _inproj_even_body,
        grid=(bsz, seq // tm),
        in_specs=[tok(D_MODEL), _const_spec((1, D_MODEL)),
                  _const_spec((1, D_MODEL, EVEN_IN), (layer_e, 0, 0)), tab, tab, tab],
        out_specs=[tok(SSM_WIDTH), tok(N_HEADS * HEAD_DIM), tok(KV_WIDTH), tok(KV_WIDTH)],
        out_shape=[jax.ShapeDtypeStruct((bsz, seq, SSM_WIDTH), F32),
                   jax.ShapeDtypeStruct((bsz, seq, N_HEADS * HEAD_DIM), BF16),
                   jax.ShapeDtypeStruct((bsz, seq, KV_WIDTH), F32),
                   jax.ShapeDtypeStruct((bsz, seq, KV_WIDTH), F32)],
        compiler_params=_params(2),
        name="inproj_even",
    )(x, g, w_in, cos, sa, sb)


def _s5_prep_body(lr_ref, li_ref, ld_ref, bre_ref, bim_ref, lrc_ref, lic_ref, ldc_ref,
                  cre_ref, cim_ref, wg_ref, wy_ref):
    ts = S5_TILE_STATES
    tw = S5_T * LANES

    def step(z_r, z_i):
        mag = jnp.exp(z_r)
        a_r, a_i = mag * jnp.cos(z_i), mag * jnp.sin(z_i)
        return a_r, a_i, lambda p: (p[0] * a_r - p[1] * a_i, p[0] * a_i + p[1] * a_r)

    lr, li = lr_ref[0], li_ref[0]
    dt = jnp.exp(ld_ref[0])
    ar, ai, mul_a = step(lr * dt, li * dt)
    nrm = lr * lr + li * li
    cbr = ((ar - 1.0) * lr + ai * li) / nrm
    cbi = (ai * lr - (ar - 1.0) * li) / nrm
    b_re, b_im = bre_ref[0], bim_ref[0]
    bbr = cbr * b_re - cbi * b_im
    bbi = cbr * b_im + cbi * b_re
    p = (jnp.ones_like(ar), jnp.zeros_like(ar))
    for s in reversed(range(S5_T)):
        rows = slice(s * LANES, (s + 1) * LANES)
        wg_ref[0, rows, :ts] = (p[0] * bbr - p[1] * bbi).astype(BF16)
        wg_ref[0, rows, ts:] = (p[0] * bbi + p[1] * bbr).astype(BF16)
        for t in range(s):
            wy_ref[0, rows, t * LANES:(t + 1) * LANES] = jnp.zeros((LANES, LANES), BF16)
        p = mul_a(p)
    dtc = jnp.exp(ldc_ref[0])
    _, _, mul_ac = step(lrc_ref[0] * dtc, lic_ref[0] * dtc)
    c_re, c_im = cre_ref[0], cim_ref[0]
    bb = jnp.concatenate([bbr, bbi], axis=1).astype(BF16)
    pc = (jnp.ones_like(dtc), jnp.zeros_like(dtc))
    for k in range(S5_T + 1):
        blk = jnp.concatenate([pc[0] * c_re - pc[1] * c_im, -(pc[0] * c_im + pc[1] * c_re)], axis=0).astype(BF16)
        if k >= 1:
            wy_ref[0, tw:, (k - 1) * LANES:k * LANES] = blk
        if k < S5_T:
            lag = _dot(bb, blk).astype(BF16)
            for s in range(S5_T - k):
                t = s + k
                wy_ref[0, s * LANES:(s + 1) * LANES, t * LANES:(t + 1) * LANES] = lag
        pc = mul_ac(pc)


def _s5_prep(a_re, a_im, log_dt, b_re, b_im, c_re, c_im):
    gpt = LANES // SSM_GROUP
    eye = jnp.eye(gpt, dtype=bool)

    def rows(p):
        return p.reshape(S5_TILES, 1, S5_TILE_STATES)

    def cols(p):
        return jnp.broadcast_to(p.reshape(S5_TILES, S5_TILE_STATES, 1), (S5_TILES, S5_TILE_STATES, LANES))

    def bmat(b):
        bt = b.reshape(S5_TILES, gpt, SSM_STATE, SSM_GROUP).transpose(0, 1, 3, 2)
        full = jnp.where(eye[None, :, None, :, None], bt[:, :, :, None, :], 0.0)
        return full.reshape(S5_TILES, LANES, S5_TILE_STATES)

    def cmat(c):
        ct = c.reshape(S5_TILES, gpt, SSM_GROUP, SSM_STATE).transpose(0, 1, 3, 2)
        full = jnp.where(eye[None, :, None, :, None], ct[:, :, :, None, :], 0.0)
        return full.reshape(S5_TILES, S5_TILE_STATES, LANES)

    row_spec = pl.BlockSpec((1, 1, S5_TILE_STATES), lambda j: (j, 0, 0))
    b_spec = pl.BlockSpec((1, LANES, S5_TILE_STATES), lambda j: (j, 0, 0))
    c_spec = pl.BlockSpec((1, S5_TILE_STATES, LANES), lambda j: (j, 0, 0))
    tw = S5_T * LANES
    return pl.pallas_call(
        _s5_prep_body,
        grid=(S5_TILES,),
        in_specs=[row_spec] * 3 + [b_spec] * 2 + [c_spec] * 5,
        out_specs=[pl.BlockSpec((1, tw, 2 * S5_TILE_STATES), lambda j: (j, 0, 0)),
                   pl.BlockSpec((1, tw + 2 * S5_TILE_STATES, tw), lambda j: (j, 0, 0))],
        out_shape=[jax.ShapeDtypeStruct((S5_TILES, tw, 2 * S5_TILE_STATES), BF16),
                   jax.ShapeDtypeStruct((S5_TILES, tw + 2 * S5_TILE_STATES, tw), BF16)],
        compiler_params=_params(1),
        name="s5_prep",
    )(rows(a_re), rows(a_im), rows(log_dt), bmat(b_re), bmat(b_im),
      cols(a_re), cols(a_im), cols(log_dt), cmat(c_re), cmat(c_im))


def _s5_pow_body(ex_ref, lr_ref, li_ref, ld_ref, o_ref):
    dt = jnp.exp(ld_ref[...])
    zr, zi = lr_ref[...] * dt, li_ref[...] * dt
    ex = ex_ref[...]
    mag = jnp.exp(ex * zr)
    o_ref[:, :S5_STATES] = mag * jnp.cos(ex * zi)
    o_ref[:, S5_STATES:] = mag * jnp.sin(ex * zi)


def _s5_pow_table(exponents, a_re, a_im, log_dt):
    n = len(exponents)
    ex = jnp.asarray(np.asarray(exponents, np.float32)[:, None])
    flat = lambda p: p.reshape(1, S5_STATES)
    return pl.pallas_call(
        _s5_pow_body,
        grid=(1,),
        in_specs=[pl.BlockSpec((n, 1), lambda i: (0, 0))] + [pl.BlockSpec((1, S5_STATES), lambda i: (0, 0))] * 3,
        out_specs=pl.BlockSpec((n, 2 * S5_STATES), lambda i: (0, 0)),
        out_shape=jax.ShapeDtypeStruct((n, 2 * S5_STATES), F32),
        compiler_params=_params(1),
        name="s5_pow_table",
    )(ex, flat(a_re), flat(a_im), flat(log_dt))


def _s5_body(u_ref, h0_ref, wg_ref, wy_ref, pow2_ref, powr_ref, d_ref, wglu_ref,
             y_ref, hout_ref, carry_ref, us_ref, ys_ref, *, ni, chained):
    ns = S5_STATES
    ts = S5_TILE_STATES
    segtok = ni * S5_T
    pitch = _s5_pitch(ni)

    @pl.when(pl.program_id(1) == 0)
    def _():
        carry_ref[...] = h0_ref[0]

    for j in range(S5_TILES):
        for s in range(SUBLANES):
            us_ref[j, s * pitch:s * pitch + segtok] = u_ref[0, s * segtok:(s + 1) * segtok, j * LANES:(j + 1) * LANES]

    def gather(j, t):
        return jnp.concatenate([us_ref[j, pl.ds(S5_T * i + t, SUBLANES, stride=pitch), :]
                                for i in range(ni)], axis=0)

    def fma(p_r, p_i, x_r, x_i, y_r, y_i):
        return y_r + p_r * x_r - p_i * x_i, y_i + p_r * x_i + p_i * x_r

    xs, g_re, g_im = [], [], []
    for j in range(S5_TILES):
        xj = jnp.concatenate([gather(j, s) for s in range(S5_T)], axis=1).astype(BF16)
        g = _dot(xj, wg_ref[j])
        xs.append(xj)
        g_re.append(g[:, :ts])
        g_im.append(g[:, ts:])
    g_re = jnp.concatenate(g_re, axis=1)
    g_im = jnp.concatenate(g_im, axis=1)
    slab = lambda a, i: a[i * SUBLANES:(i + 1) * SUBLANES]
    power = lambda ref, r: (ref[r:r + 1, :ns], ref[r:r + 1, ns:])
    a_r, a_i = power(powr_ref, 0)
    cin = carry_ref[...]
    h_r, h_i = cin[:, :ns], cin[:, ns:]
    prev = []
    if chained:
        local = [(slab(g_re, 0), slab(g_im, 0))]
        for i in range(1, ni):
            local.append(fma(a_r, a_i, *local[-1], slab(g_re, i), slab(g_im, i)))
        sub = lax.broadcasted_iota(jnp.int32, (SUBLANES, 1), 0)
        c_r = jnp.where(sub == 0, h_r, pltpu.roll(local[-1][0], 1, 0))
        c_i = jnp.where(sub == 0, h_i, pltpu.roll(local[-1][1], 1, 0))
        for m in range(3):
            k = 1 << m
            s_r = jnp.where(sub >= k, pltpu.roll(c_r, k, 0), 0.0)
            s_i = jnp.where(sub >= k, pltpu.roll(c_i, k, 0), 0.0)
            c_r, c_i = fma(*power(pow2_ref, m), s_r, s_i, c_r, c_i)
        h_r, h_i = c_r, c_i
        for i in range(ni):
            prev.append((h_r, h_i))
            h_r, h_i = fma(*power(powr_ref, i), c_r, c_i, *local[i])
        last = jnp.concatenate([h_r[SUBLANES - 1:], h_i[SUBLANES - 1:]], axis=1)
    else:
        for i in range(ni):
            prev.append((h_r, h_i))
            h_r, h_i = fma(a_r, a_i, h_r, h_i, slab(g_re, i), slab(g_im, i))
        last = jnp.concatenate([h_r, h_i], axis=1)
    carry_ref[...] = last
    hout_ref[0] = last
    prev_re = jnp.concatenate([p[0] for p in prev], axis=0)
    prev_im = jnp.concatenate([p[1] for p in prev], axis=0)
    y_tiles = []
    for j in range(S5_TILES):
        lhs = jnp.concatenate([xs[j], prev_re[:, j * ts:(j + 1) * ts].astype(BF16),
                               prev_im[:, j * ts:(j + 1) * ts].astype(BF16)], axis=1)
        y_tiles.append(_dot(lhs, wy_ref[j]))
    d = d_ref[...]
    wglu = wglu_ref[0]
    def gelu_in(t):
        ut = jnp.concatenate([gather(j, t) for j in range(S5_TILES)], axis=1)
        yt = jnp.concatenate([y_tiles[j][:, t * LANES:(t + 1) * LANES] for j in range(S5_TILES)], axis=1)
        return jax.nn.gelu(yt + d * ut)

    def scatter(t, o):
        for j in range(S5_TILES):
            for i in range(ni):
                ys_ref[j, pl.ds(S5_T * i + t, SUBLANES, stride=pitch), :] = slab(o, i)[:, j * LANES:(j + 1) * LANES]

    g_q, z_q = {}, {}
    for step in range(S5_T + 2):
        if step < S5_T:
            g_q[step] = gelu_in(step)
        if 0 <= step - 1 < S5_T:
            z_q[step - 1] = _dot(g_q[step - 1].astype(BF16), wglu)
        if 0 <= step - 2 < S5_T:
            scatter(step - 2, g_q.pop(step - 2) * jax.nn.sigmoid(z_q.pop(step - 2)))
    for j in range(S5_TILES):
        for s in range(SUBLANES):
            y_ref[0, s * segtok:(s + 1) * segtok, j * LANES:(j + 1) * LANES] = (
                ys_ref[j, s * pitch:s * pitch + segtok].astype(BF16))


def _s5_pitch(ni):
    segtok = ni * S5_T
    return segtok + SUBLANES if (segtok // SUBLANES) % 2 == 0 else segtok


def _s5_mixer(u, h0, wg, wy, pow2, powr, d, w_glu, layer_e, ni, chained):
    nb, ntok, width = u.shape
    tb = SUBLANES * ni * S5_T
    hrows = h0.shape[1]
    blk = pl.BlockSpec((1, tb, width), lambda b, i: (b, i, 0))
    state = pl.BlockSpec((1, hrows, 2 * S5_STATES), lambda b, i: (b, 0, 0))
    return pl.pallas_call(
        functools.partial(_s5_body, ni=ni, chained=chained),
        grid=(nb, ntok // tb),
        in_specs=[blk, state, _const_spec(wg.shape), _const_spec(wy.shape),
                  _const_spec(pow2.shape), _const_spec(powr.shape), _const_spec((1, SSM_WIDTH)),
                  _const_spec((1, SSM_WIDTH, SSM_WIDTH), (layer_e, 0, 0))],
        out_specs=[blk, state],
        out_shape=[jax.ShapeDtypeStruct((nb, ntok, width), BF16),
                   jax.ShapeDtypeStruct((nb, hrows, 2 * S5_STATES), F32)],
        scratch_shapes=[pltpu.VMEM((hrows, 2 * S5_STATES), F32)]
        + [pltpu.VMEM((S5_TILES, SUBLANES * _s5_pitch(ni), LANES), F32)] * 2,
        compiler_params=_params(2),
        name="s5_mixer",
    )(u, h0, wg, wy, pow2, powr, d, w_glu)


def _swa_body(sink_ref, q_ref, kc_ref, vc_ref, kp_ref, vp_ref, pk_ref, pv_ref, o_ref, *, tq, qg, start):
    first = pl.program_id(1) == 0
    k_prev = jnp.where(first, pk_ref[0], kp_ref[0])
    v_prev = jnp.where(first, pv_ref[0], vp_ref[0])
    k_all = jnp.concatenate([k_prev, kc_ref[0]], axis=0)
    v_all = jnp.concatenate([v_prev, vc_ref[0]], axis=0)
    nk = WINDOW + qg
    lane = lax.broadcasted_iota(jnp.int32, (1, KV_WIDTH), 1)
    slot = lax.broadcasted_iota(jnp.int32, (1, nk), 1)
    qrow = lax.broadcasted_iota(jnp.int32, (2 * qg, 1), 0)
    lower = lane < HEAD_DIM
    placed = []
    for kh in range(N_KV_HEADS):
        own = lower if kh == 0 else jnp.logical_not(lower)
        kz, vz = jnp.where(own, k_all, 0.0), jnp.where(own, v_all, 1.0)
        kr, vr = pltpu.roll(kz, HEAD_DIM, 1), pltpu.roll(vz, HEAD_DIM, 1)
        k_lo, k_hi = (kz, kr) if kh == 0 else (kr, kz)
        v_lo, v_hi = (vz, vr) if kh == 0 else (vr, vz)
        placed.append(((k_lo.astype(BF16), v_lo.astype(BF16)), (k_hi.astype(BF16), v_hi.astype(BF16))))
    combos = [(kh, par) for kh in range(N_KV_HEADS) for par in range(2)]

    def scores(c):
        qc = q_ref[0, c * qg:(c + 1) * qg, :]
        out = []
        for kh, par in combos:
            base = kh * GQ * HEAD_DIM
            qs = jnp.concatenate([qc[:, base:base + LANES], qc[:, base + LANES:base + 2 * LANES]], axis=0)
            out.append(_dot_t(qs, placed[kh][par][0][c * qg:c * qg + nk]))
        return out

    def softmax(c, ss):
        n_bad = jnp.where(first, WINDOW - start - c * qg, 0) if c * qg < WINDOW - start else None
        probs, sinks = [], []
        for (kh, par), s in zip(combos, ss):
            if n_bad is not None:
                s = jnp.where(slot >= n_bad, s, NEG)
            sink = jnp.where(qrow < qg, sink_ref[kh * GQ + par], sink_ref[kh * GQ + 2 + par])
            m = jnp.maximum(jnp.max(s, axis=1, keepdims=True), sink)
            probs.append(jnp.exp(s - m).astype(BF16))
            sinks.append(jnp.exp(sink - m))
        return probs, sinks

    def finish(c, probs, sinks):
        pvs = [_dot(p, placed[kh][par][1][c * qg:c * qg + nk]) for (kh, par), p in zip(combos, probs)]
        outs = []
        for kh in range(N_KV_HEADS):
            pv_e, pv_o = pvs[2 * kh], pvs[2 * kh + 1]
            num = jnp.where(lower, pv_e, pv_o)
            den = pltpu.roll(jnp.where(lower, pv_o, pv_e), HEAD_DIM, 1) + jnp.where(lower, sinks[2 * kh], sinks[2 * kh + 1])
            acc = num / den
            outs += [acc[:qg], acc[qg:]]
        o_ref[0, c * qg:(c + 1) * qg, :] = jnp.concatenate(outs, axis=1).astype(BF16)

    nchunk = tq // qg
    per = min(SWA_STAGE_CHUNKS, nchunk)
    groups = [range(g, min(g + per, nchunk)) for g in range(0, nchunk, per)]
    s_q, p_q = {}, {}
    for step in range(len(groups) + 2):
        if step < len(groups):
            s_q[step] = [scores(c) for c in groups[step]]
        if 0 <= step - 1 < len(groups):
            p_q[step - 1] = [softmax(c, ss) for c, ss in zip(groups[step - 1], s_q.pop(step - 1))]
        if 0 <= step - 2 < len(groups):
            for c, (probs, sinks) in zip(groups[step - 2], p_q.pop(step - 2)):
                finish(c, probs, sinks)


def _swa(q, k, v, past_k, past_v, sinks, start, tq, qg):
    bsz, seq, _ = q.shape
    ntiles = seq // tq
    tok = lambda w: pl.BlockSpec((1, tq, w), lambda b, i: (b, i, 0))
    past = pl.BlockSpec((1, WINDOW, KV_WIDTH), lambda b, i: (b, 0, 0))
    if ntiles > 1:
        per = tq // WINDOW
        prev = pl.BlockSpec((1, WINDOW, KV_WIDTH), lambda b, i: (b, jnp.maximum(i * per - 1, 0), 0))
        k_prev, v_prev = k, v
    else:
        prev, k_prev, v_prev = past, past_k, past_v
    return pl.pallas_call(
        functools.partial(_swa_body, tq=tq, qg=qg, start=start),
        grid=(bsz, ntiles),
        in_specs=[pl.BlockSpec(memory_space=pltpu.SMEM), tok(N_HEADS * HEAD_DIM), tok(KV_WIDTH), tok(KV_WIDTH),
                  prev, prev, past, past],
        out_specs=tok(N_HEADS * HEAD_DIM),
        out_shape=jax.ShapeDtypeStruct((bsz, seq, N_HEADS * HEAD_DIM), BF16),
        compiler_params=_params(2),
        name="swa",
    )(sinks, q, k, v, k_prev, v_prev, past_k, past_v)


def _xattn(x1, g, wq, mk_ref, mv_ref, wo, nseq):
    h = _rms(x1, g).astype(BF16)
    q = _dot(h, wq).astype(BF16)
    scale = X_HEAD_DIM ** -0.5
    rows = x1.shape[0] // nseq
    parts = [(b, slice(b * rows, (b + 1) * rows), slice(hd * X_HEAD_DIM, (hd + 1) * X_HEAD_DIM))
             for b in range(nseq) for hd in range(X_HEADS)]
    scores = [_dot_t(q[r, c], mk_ref[0, b, :, c]) * scale for b, r, c in parts]
    probs, dens = [], []
    for s in scores:
        p = jnp.exp(s - jnp.max(s, axis=1, keepdims=True))
        probs.append(p.astype(BF16))
        dens.append(jnp.sum(p, axis=1, keepdims=True))
    outs = [_dot(p, mv_ref[0, b, :, c]) / den for (b, r, c), p, den in zip(parts, probs, dens)]
    o = jnp.concatenate([jnp.concatenate(outs[b * X_HEADS:(b + 1) * X_HEADS], axis=1) for b in range(nseq)], axis=0)
    return x1 + _dot(o.astype(BF16), wo)


def _conv3(cur, carry, w):
    n = cur.shape[0]
    row = lax.broadcasted_iota(jnp.int32, (n, 1), 0)
    c1, c2 = carry[SUBLANES - 1:SUBLANES], carry[SUBLANES - 2:SUBLANES - 1]
    m1 = jnp.where(row == 0, c1, pltpu.roll(cur, 1, 0))
    m2 = jnp.where(row == 0, c2, jnp.where(row == 1, c1, pltpu.roll(cur, 2, 0)))
    return w[0:1] * m2 + w[1:2] * m1 + w[2:3] * cur


def _conv3_seqs(cur, carries, w):
    nseq = len(carries)
    if nseq == 1:
        return _conv3(cur, carries[0], w)
    rows = cur.shape[0] // nseq
    body = _conv3(cur, carries[0], w)
    pieces = []
    for b in range(nseq):
        pieces.append(_conv3(cur[b * rows:b * rows + SUBLANES], carries[b], w))
        pieces.append(body[b * rows + SUBLANES:(b + 1) * rows])
    return jnp.concatenate(pieces, axis=0)


def _store_last_rows(st_ref, cols, cur, nseq):
    rows = cur.shape[0] // nseq
    for b in range(nseq):
        st_ref[b, :, cols] = cur[(b + 1) * rows - (CONV_WIDTH - 1):(b + 1) * rows]


def _mix_even_xattn_body(x_ref, ya_ref, yb_ref, woa_ref, wob_ref, g_ref, wq_ref, mk_ref, mv_ref, wo_ref, o_ref,
                         *, nseq):
    x1 = x_ref[0] + _dot(ya_ref[0], woa_ref[0]) + _dot(yb_ref[0], wob_ref[0])
    o_ref[0] = _xattn(x1, g_ref[...], wq_ref[0], mk_ref, mv_ref, wo_ref[0], nseq)


def _mix_even_xattn(x, ya, yb, w_out, g, wq, mem_k, mem_v, wo, layer, layer_e, tm, nseq):
    bsz, seq, _ = x.shape
    tok = lambda w: pl.BlockSpec((1, tm, w), lambda b, i: (b, i, 0))
    mem = pl.BlockSpec((1, nseq, N_MEM, D_MODEL), lambda b, i: (layer, b, 0, 0))
    half = D_MODEL // 2
    sq = lambda: _const_spec((1, D_MODEL, D_MODEL), (layer, 0, 0))
    return pl.pallas_call(
        functools.partial(_mix_even_xattn_body, nseq=nseq),
        grid=(bsz, seq // tm),
        in_specs=[tok(D_MODEL), tok(half), tok(half),
                  _const_spec((1, half, D_MODEL), (layer_e, 0, 0)), _const_spec((1, half, D_MODEL), (layer_e, 1, 0)),
                  _const_spec((1, D_MODEL)), sq(), mem, mem, sq()],
        out_specs=tok(D_MODEL),
        out_shape=jax.ShapeDtypeStruct(x.shape, F32),
        compiler_params=_params(2),
        name="mix_even_xattn",
    )(x, ya, yb, w_out, w_out, g, wq, mem_k, mem_v, wo)


def _mix_odd_xattn_body(x_ref, gm_ref, win_ref, cw_ref, wout_ref, prev_ref, g_ref, wq_ref, mk_ref, mv_ref, wo_ref,
                        o_ref, st_ref, carry_ref, *, tm, nseq):
    @pl.when(pl.program_id(1) == 0)
    def _():
        carry_ref[...] = prev_ref[0]

    x = x_ref[0]
    proj = _dot(_rms(x, gm_ref[...]).astype(BF16), win_ref[0])
    gate_b, gate_c, z = proj[:, :D_MODEL], proj[:, D_MODEL:2 * D_MODEL], proj[:, 2 * D_MODEL:]
    cz = gate_c * z
    carries = [carry_ref[...]] if nseq == 1 else [prev_ref[b] for b in range(nseq)]
    z_conv = _conv3_seqs(cz, carries, cw_ref[0])
    carry_ref[...] = cz[tm - SUBLANES:]
    _store_last_rows(st_ref, slice(None), cz, nseq)
    x1 = x + _dot((gate_b * z_conv).astype(BF16), wout_ref[0])
    o_ref[0] = _xattn(x1, g_ref[...], wq_ref[0], mk_ref, mv_ref, wo_ref[0], nseq)


def _mix_odd_xattn(x, gm, w_in, conv_w, w_out, prev, g, wq, mem_k, mem_v, wo, layer, layer_o, tm, nseq):
    bsz, seq, _ = x.shape
    tok = pl.BlockSpec((1, tm, D_MODEL), lambda b, i: (b, i, 0))
    mem = pl.BlockSpec((1, nseq, N_MEM, D_MODEL), lambda b, i: (layer, b, 0, 0))
    hist = pl.BlockSpec((nseq, SUBLANES, D_MODEL), lambda b, i: (b, 0, 0))
    st = pl.BlockSpec((nseq, CONV_WIDTH - 1, D_MODEL), lambda b, i: (b, 0, 0))
    sq = lambda l: _const_spec((1, D_MODEL, D_MODEL), (l, 0, 0))
    return pl.pallas_call(
        functools.partial(_mix_odd_xattn_body, tm=tm, nseq=nseq),
        grid=(bsz, seq // tm),
        in_specs=[tok, _const_spec((1, D_MODEL)), _const_spec((1, D_MODEL, 3 * D_MODEL), (layer_o, 0, 0)),
                  _const_spec((1, CONV_WIDTH, D_MODEL), (layer_o, 0, 0)), sq(layer_o), hist,
                  _const_spec((1, D_MODEL)), sq(layer), mem, mem, sq(layer)],
        out_specs=[tok, st],
        out_shape=[jax.ShapeDtypeStruct(x.shape, F32),
                   jax.ShapeDtypeStruct((bsz * nseq, CONV_WIDTH - 1, D_MODEL), F32)],
        scratch_shapes=[pltpu.VMEM((SUBLANES, D_MODEL), F32)],
        compiler_params=_params(2),
        name="mix_odd_xattn",
    )(x, gm, w_in, conv_w, w_out, prev, g, wq, mem_k, mem_v, wo)


def _ffn_body(x_ref, g_ref, wup_ref, cw_ref, wdn_ref, prev_ref, gfin_ref, o_ref, st_ref, carry_ref,
              *, tm, final, nseq):
    @pl.when(pl.program_id(1) == 0)
    def _():
        carry_ref[...] = prev_ref[0]

    x = x_ref[0]
    h = _rms(x, g_ref[...]).astype(BF16)
    chunks = [slice(lo, min(lo + FF_CHUNK, D_FF)) for lo in range(0, D_FF, FF_CHUNK)]

    def up(cols):
        return _dot(h, wup_ref[0, :, cols]), _dot(h, wup_ref[0, :, D_FF + cols.start:D_FF + cols.stop])

    acc = x
    nxt = up(chunks[0])
    for n, cols in enumerate(chunks):
        (gate, val), nxt = nxt, (up(chunks[n + 1]) if n + 1 < len(chunks) else None)
        carries = [carry_ref[:, cols]] if nseq == 1 else [prev_ref[b, :, cols] for b in range(nseq)]
        gate_c = _conv3_seqs(gate, carries, cw_ref[0, :, cols])
        carry_ref[:, cols] = gate[tm - SUBLANES:]
        _store_last_rows(st_ref, cols, gate, nseq)
        act = gate_c * jax.nn.sigmoid(gate_c) * val
        acc = acc + _dot(act.astype(BF16), wdn_ref[0, cols, :])
    o_ref[0] = _rms(acc, gfin_ref[...]) if final else acc


def _ffn(x, g, w_up, conv_w, w_down, prev, g_final, layer, tm, final, nseq):
    bsz, seq, _ = x.shape
    tok = pl.BlockSpec((1, tm, D_MODEL), lambda b, i: (b, i, 0))
    hist = pl.BlockSpec((nseq, SUBLANES, D_FF), lambda b, i: (b, 0, 0))
    st = pl.BlockSpec((nseq, CONV_WIDTH - 1, D_FF), lambda b, i: (b, 0, 0))
    return pl.pallas_call(
        functools.partial(_ffn_body, tm=tm, final=final, nseq=nseq),
        grid=(bsz, seq // tm),
        in_specs=[tok, _const_spec((1, D_MODEL)), _const_spec((1, D_MODEL, 2 * D_FF), (layer, 0, 0)),
                  _const_spec((1, CONV_WIDTH, D_FF), (layer, 0, 0)), _const_spec((1, D_FF, D_MODEL), (layer, 0, 0)),
                  hist, _const_spec((1, D_MODEL))],
        out_specs=[tok, st],
        out_shape=[jax.ShapeDtypeStruct(x.shape, F32),
                   jax.ShapeDtypeStruct((bsz * nseq, CONV_WIDTH - 1, D_FF), F32)],
        scratch_shapes=[pltpu.VMEM((SUBLANES, D_FF), F32)],
        compiler_params=_params(2),
        name="conv_ffn",
    )(x, g, w_up, conv_w, w_down, prev, g_final)


def _mem_kv_body(m_ref, w_ref, k_ref, v_ref, kf_ref, vf_ref):
    kv = _dot(m_ref[0].astype(BF16), w_ref[0])
    kf_ref[0, 0] = kv[:, :D_MODEL].astype(BF16)
    vf_ref[0, 0] = kv[:, D_MODEL:].astype(BF16)
    for hd in range(X_HEADS):
        k_ref[0, 0, :, hd, :] = kv[:, hd * X_HEAD_DIM:(hd + 1) * X_HEAD_DIM]
        v_ref[0, 0, :, hd, :] = kv[:, D_MODEL + hd * X_HEAD_DIM:D_MODEL + (hd + 1) * X_HEAD_DIM]


def _mem_kv(mem, w_kv):
    bsz = mem.shape[0]
    heads = pl.BlockSpec((1, 1, N_MEM, X_HEADS, X_HEAD_DIM), lambda l, b: (l, b, 0, 0, 0))
    flat = pl.BlockSpec((1, 1, N_MEM, D_MODEL), lambda l, b: (l, b, 0, 0))
    return pl.pallas_call(
        _mem_kv_body,
        grid=(DEPTH, bsz),
        in_specs=[pl.BlockSpec((1, N_MEM, D_MODEL), lambda l, b: (b, 0, 0)),
                  pl.BlockSpec((1, D_MODEL, 2 * D_MODEL), lambda l, b: (l, 0, 0))],
        out_specs=[heads, heads, flat, flat],
        out_shape=[jax.ShapeDtypeStruct((DEPTH, bsz, N_MEM, X_HEADS, X_HEAD_DIM), F32)] * 2
        + [jax.ShapeDtypeStruct((DEPTH, bsz, N_MEM, D_MODEL), BF16)] * 2,
        compiler_params=_params(2),
        name="mem_kv",
    )(mem, w_kv)


def _trunk(x, start, past_k, past_v, ssm_re, ssm_im, conv_mix_prev, conv_ffn_prev, mem_k, mem_v, w, s5, cfg):
    bsz, seq, _ = x.shape
    tq, qg, nseq = cfg["tq"], cfg["qg"], cfg["nseq"]
    tm_in, tm_even, tm_odd, tm_ffn = cfg["tm_in"], cfg["tm_even"], cfg["tm_odd"], cfg["tm_ffn"]
    ni, chained = cfg["ni"], cfg["chained"]
    ngrp, grows = bsz // nseq, nseq * seq
    grouped = lambda t: t.reshape(ngrp, grows, t.shape[-1])
    per_seq = lambda t: t.reshape(bsz, seq, t.shape[-1])
    hist = lambda st: jnp.pad(st, ((0, 0), (SUBLANES - (CONV_WIDTH - 1), 0), (0, 0)))
    tables = _rope_tables(grows, seq, start, tm_in)
    u, q, k, v = _inproj_even(grouped(x), w["norm_mix"][0:1], w["w_in_even"], tables, 0, tm_in)
    h0 = jnp.concatenate([ssm_re[0].reshape(bsz, S5_STATES), ssm_im[0].reshape(bsz, S5_STATES)], axis=1)
    if chained:
        ya, h_all = _s5_mixer(u, h0[:, None, :], s5["wg"], s5["wy"], cfg["pow2"], cfg["powr"], w["ssm_d"],
                              w["w_glu"], 0, ni, True)
        h_last = h_all[:, 0]
    else:
        assert ngrp == 1 and bsz == SUBLANES and seq == ni * S5_T
        ya, h_all = _s5_mixer(u, h0[None], s5["wg"], s5["wy"], cfg["pow2"], cfg["powr"], w["ssm_d"],
                              w["w_glu"], 0, ni, False)
        h_last = h_all[0]
    new_re = h_last[:, :S5_STATES].reshape(1, bsz, SSM_GROUPS, SSM_STATE)
    new_im = h_last[:, S5_STATES:].reshape(1, bsz, SSM_GROUPS, SSM_STATE)
    k, v = per_seq(k), per_seq(v)
    yb = _swa(per_seq(q), k, v, past_k[0].reshape(bsz, WINDOW, KV_WIDTH), past_v[0].reshape(bsz, WINDOW, KV_WIDTH),
              w["attn_sinks"][0], start, tq, qg)
    xg = _mix_even_xattn(grouped(x), ya, grouped(yb), w["w_out_even"], w["norm_xattn"][0:1], w["xattn_wq"],
                         mem_k, mem_v, w["xattn_wo"], 0, 0, tm_even, nseq)
    xg, cf0 = _ffn(xg, w["norm_ffn"][0:1], w["ffn_w_up"], w["ffn_conv_w"], w["ffn_w_down"], hist(conv_ffn_prev[0]),
                   w["norm_final"], 0, tm_ffn, False, nseq)
    xg, cm = _mix_odd_xattn(xg, w["norm_mix"][1:2], w["w_in_odd"], w["conv_mix_w"], w["w_out_odd"],
                            hist(conv_mix_prev[0]), w["norm_xattn"][1:2], w["xattn_wq"], mem_k, mem_v,
                            w["xattn_wo"], 1, 0, tm_odd, nseq)
    y, cf1 = _ffn(xg, w["norm_ffn"][1:2], w["ffn_w_up"], w["ffn_conv_w"], w["ffn_w_down"], hist(conv_ffn_prev[1]),
                  w["norm_final"], 1, tm_ffn, True, nseq)
    keep = min(seq, WINDOW)
    new_k = k[:, seq - keep:].reshape(1, bsz, keep, N_KV_HEADS, HEAD_DIM)
    new_v = v[:, seq - keep:].reshape(1, bsz, keep, N_KV_HEADS, HEAD_DIM)
    return per_seq(y), new_k, new_v, new_re, new_im, cm[None], jnp.stack([cf0, cf1])


def kernel(x_prompt, x_sample, mem_prompt, cache_win_k, cache_win_v, state_ssm_re, state_ssm_im, state_conv_mix, state_conv_ffn, cache_mem_k, cache_mem_v, norm_mix, norm_xattn, norm_ffn, norm_final, w_in_even, w_out_even, ssm_a_re, ssm_a_im, ssm_log_dt, ssm_b_re, ssm_b_im, ssm_c_re, ssm_c_im, ssm_d, w_glu, attn_sinks, w_in_odd, conv_mix_w, w_out_odd, xattn_wq, xattn_wkv, xattn_wo, ffn_w_up, ffn_conv_w, ffn_w_down):
    bp, seq_p, _ = x_prompt.shape
    bs, seq_s, _ = x_sample.shape
    w = dict(norm_mix=norm_mix, norm_xattn=norm_xattn, norm_ffn=norm_ffn, norm_final=norm_final.reshape(1, D_MODEL),
             w_in_even=w_in_even.astype(BF16), w_out_even=w_out_even.astype(BF16),
             ssm_d=ssm_d[0].reshape(1, SSM_WIDTH), w_glu=w_glu.astype(BF16), attn_sinks=attn_sinks,
             w_in_odd=w_in_odd.astype(BF16), conv_mix_w=conv_mix_w, w_out_odd=w_out_odd.astype(BF16),
             xattn_wq=xattn_wq.astype(BF16), xattn_wo=xattn_wo.astype(BF16),
             ffn_w_up=ffn_w_up.astype(BF16), ffn_conv_w=ffn_conv_w, ffn_w_down=ffn_w_down.astype(BF16))

    wg, wy = _s5_prep(ssm_a_re[0], ssm_a_im[0], ssm_log_dt[0], ssm_b_re[0], ssm_b_im[0], ssm_c_re[0], ssm_c_im[0])
    s5 = dict(wg=wg, wy=wy)
    ni_p, ni_s = S5_SEG_CHUNKS, seq_s // S5_T
    padded = lambda e: e + [1.0] * (-len(e) % SUBLANES)
    exps = (padded([float(S5_T * ni_p << m) for m in range(3)])
            + padded([float(S5_T * (i + 1)) for i in range(ni_p)])
            + padded([float(S5_T * (i + 1)) for i in range(ni_s)]))
    pows = _s5_pow_table(exps, ssm_a_re[0], ssm_a_im[0], ssm_log_dt[0])
    o1 = SUBLANES
    o2 = o1 + len(padded([0.0] * ni_p))
    cfg_p = dict(tm_in=1024, tm_even=1024, tm_odd=1024, tm_ffn=1024, tq=512, qg=CHUNK, nseq=1, ni=ni_p, chained=True,
                 pow2=pows[:o1], powr=pows[o1:o2])
    rows_s = bs * seq_s
    cfg_s = dict(tm_in=rows_s, tm_even=rows_s, tm_odd=rows_s, tm_ffn=rows_s, tq=seq_s, qg=seq_s, nseq=bs, ni=ni_s,
                 chained=False, pow2=pows[:o1], powr=pows[o2:])

    mem_k_p, mem_v_p, mem_kf_p, mem_vf_p = _mem_kv(mem_prompt, xattn_wkv.astype(BF16))
    cw = CONV_WIDTH - 1
    zk = jnp.zeros((1, bp, WINDOW, N_KV_HEADS, HEAD_DIM), F32)
    zs = jnp.zeros((1, bp, SSM_GROUPS, SSM_STATE), F32)
    zcm = jnp.zeros((1, bp, cw, D_MODEL), F32)
    zcf = jnp.zeros((DEPTH, bp, cw, D_FF), F32)
    y_p, k_p, v_p, re_p, im_p, cm_p, cf_p = _trunk(x_prompt, 0, zk, zk, zs, zs, zcm, zcf, mem_kf_p, mem_vf_p, w, s5, cfg_p)
    mem_kf_s = cache_mem_k.reshape(DEPTH, bs, N_MEM, D_MODEL).astype(BF16)
    mem_vf_s = cache_mem_v.reshape(DEPTH, bs, N_MEM, D_MODEL).astype(BF16)
    y_s, k_s, v_s, re_s, im_s, cm_s, cf_s = _trunk(x_sample, PAST_LEN, cache_win_k, cache_win_v, state_ssm_re,
                                                   state_ssm_im, state_conv_mix, state_conv_ffn, mem_kf_s,
                                                   mem_vf_s, w, s5, cfg_s)
    return (y_p, y_s, k_p, v_p, re_p, im_p, cm_p, cf_p, mem_k_p, mem_v_p, k_s, v_s, re_s, im_s, cm_s, cf_s)
```

```python
import functools
import math

import numpy as np
import jax
import jax.numpy as jnp
from jax import lax
from jax.experimental import pallas as pl
from jax.experimental.pallas import tpu as pltpu

F32 = jnp.float32
BF16 = jnp.bfloat16

D_MODEL = 1024
DEPTH = 2
PAST_LEN = 2048
CHUNK = 64
SSM_WIDTH = 512
SSM_GROUP = 16
SSM_GROUPS = 32
SSM_STATE = 64
HEAD_DIM = 64
N_HEADS = 8
N_KV_HEADS = 2
GQ = N_HEADS // N_KV_HEADS
WINDOW = 128
ROPE_THETA = 10000.0
KV_WIDTH = N_KV_HEADS * HEAD_DIM
EVEN_IN = SSM_WIDTH + N_HEADS * HEAD_DIM + 2 * KV_WIDTH
CONV_WIDTH = 3
N_MEM = 256
X_HEADS = 4
X_HEAD_DIM = 256
D_FF = 2816
EPS = 1e-6
NEG = -1e30

LANES = 128
SUBLANES = 8
S5_T = 8
S5_TILES = SSM_WIDTH // LANES
S5_TILE_STATES = (LANES // SSM_GROUP) * SSM_STATE
S5_STATES = SSM_GROUPS * SSM_STATE
SWA_STAGE_CHUNKS = 1
XATTN_PARTS = 2
S5_SEG_CHUNKS = 16
MXU_DIM = 256
FF_CHUNK = 6 * MXU_DIM
VMEM_LIMIT = 56 * 1024 * 1024


def _const_spec(shape, index=None):
    idx = tuple(index) if index is not None else (0,) * len(shape)
    return pl.BlockSpec(shape, lambda *_: idx, pipeline_mode=pl.Buffered(1))


def _params(n_axes):
    return pltpu.CompilerParams(dimension_semantics=("arbitrary",) * n_axes,
                                vmem_limit_bytes=VMEM_LIMIT)


def _rms(x, g):
    return x * lax.rsqrt(jnp.mean(x * x, axis=-1, keepdims=True) + EPS) * g


def _dot(a, b):
    return jnp.dot(a, b, preferred_element_type=F32)


def _dot_t(a, b):
    return lax.dot_general(a, b, (((1,), (1,)), ((), ())), preferred_element_type=F32)


def _rope_table_body(cos_ref, sina_ref, sinb_ref, *, start, tl, period):
    i = pl.program_id(0)
    row = lax.broadcasted_iota(jnp.int32, (tl, LANES), 0)
    lane = lax.broadcasted_iota(jnp.int32, (tl, LANES), 1)
    pos = (start + ((i * tl + row) & (period - 1))).astype(F32)
    half = HEAD_DIM // 2
    freq = (lane[0:1] & (half - 1)).astype(F32)
    inv = jnp.exp(freq * (-2.0 / HEAD_DIM * math.log(ROPE_THETA)))
    ang = pos * inv
    c = jnp.cos(ang)
    s = jnp.sin(ang)
    first = (lane & (HEAD_DIM - 1)) < HEAD_DIM // 2
    cos_ref[...] = c
    sina_ref[...] = jnp.where(first, -s, 0.0)
    sinb_ref[...] = jnp.where(first, 0.0, s)


def _rope_tables(rows, period, start, tl):
    spec = pl.BlockSpec((tl, LANES), lambda i: (i, 0))
    return pl.pallas_call(
        functools.partial(_rope_table_body, start=start, tl=tl, period=period),
        grid=(rows // tl,),
        in_specs=[],
        out_specs=[spec, spec, spec],
        out_shape=[jax.ShapeDtypeStruct((rows, LANES), F32)] * 3,
        compiler_params=_params(1),
        name="rope_tables",
    )()


def _rope(t, cos, sina, sinb):
    n = t.shape[1]
    half = HEAD_DIM // 2
    return t * cos + pltpu.roll(t, n - half, 1) * sina + pltpu.roll(t, half, 1) * sinb


def _inproj_even_body(x_ref, g_ref, w_ref, cos_ref, sa_ref, sb_ref, u_ref, q_ref, k_ref, v_ref):
    h = _rms(x_ref[0], g_ref[...]).astype(BF16)
    proj = _dot(h, w_ref[0])
    o0 = SSM_WIDTH
    o1 = o0 + N_HEADS * HEAD_DIM
    o2 = o1 + KV_WIDTH
    cos, sa, sb = cos_ref[...], sa_ref[...], sb_ref[...]
    rep = (o1 - o0) // LANES
    cos_q = jnp.concatenate([cos] * rep, axis=1)
    sa_q = jnp.concatenate([sa] * rep, axis=1)
    sb_q = jnp.concatenate([sb] * rep, axis=1)
    u_ref[0] = proj[:, :o0]
    q_ref[0] = (_rope(proj[:, o0:o1], cos_q, sa_q, sb_q) * HEAD_DIM ** -0.5).astype(BF16)
    k_ref[0] = _rope(proj[:, o1:o2], cos, sa, sb)
    v_ref[0] = proj[:, o2:]


def _inproj_even(x, g, w_in, tables, layer_e, tm):
    bsz, seq, _ = x.shape
    cos, sa, sb = tables
    tok = lambda w: pl.BlockSpec((1, tm, w), lambda b, i: (b, i, 0))
    tab = pl.BlockSpec((tm, LANES), lambda b, i: (i, 0))
    return pl.pallas_call(
        _inproj_even_body,
        grid=(bsz, seq // tm),
        in_specs=[tok(D_MODEL), _const_spec((1, D_MODEL)),
                  _const_spec((1, D_MODEL, EVEN_IN), (layer_e, 0, 0)), tab, tab, tab],
        out_specs=[tok(SSM_WIDTH), tok(N_HEADS * HEAD_DIM), tok(KV_WIDTH), tok(KV_WIDTH)],
        out_shape=[jax.ShapeDtypeStruct((bsz, seq, SSM_WIDTH), F32),
                   jax.ShapeDtypeStruct((bsz, seq, N_HEADS * HEAD_DIM), BF16),
                   jax.ShapeDtypeStruct((bsz, seq, KV_WIDTH), F32),
                   jax.ShapeDtypeStruct((bsz, seq, KV_WIDTH), F32)],
        compiler_params=_params(2),
        name="inproj_even",
    )(x, g, w_in, cos, sa, sb)


def _s5_prep_body(lr_ref, li_ref, ld_ref, bre_ref, bim_ref, lrc_ref, lic_ref, ldc_ref,
                  cre_ref, cim_ref, wg_ref, wy_ref):
    ts = S5_TILE_STATES
    tw = S5_T * LANES

    def step(z_r, z_i):
        mag = jnp.exp(z_r)
        a_r, a_i = mag * jnp.cos(z_i), mag * jnp.sin(z_i)
        return a_r, a_i, lambda p: (p[0] * a_r - p[1] * a_i, p[0] * a_i + p[1] * a_r)

    lr, li = lr_ref[0], li_ref[0]
    dt = jnp.exp(ld_ref[0])
    ar, ai, mul_a = step(lr * dt, li * dt)
    nrm = lr * lr + li * li
    cbr = ((ar - 1.0) * lr + ai * li) / nrm
    cbi = (ai * lr - (ar - 1.0) * li) / nrm
    b_re, b_im = bre_ref[0], bim_ref[0]
    bbr = cbr * b_re - cbi * b_im
    bbi = cbr * b_im + cbi * b_re
    p = (jnp.ones_like(ar), jnp.zeros_like(ar))
    for s in reversed(range(S5_T)):
        rows = slice(s * LANES, (s + 1) * LANES)
        wg_ref[0, rows, :ts] = (p[0] * bbr - p[1] * bbi).astype(BF16)
        wg_ref[0, rows, ts:] = (p[0] * bbi + p[1] * bbr).astype(BF16)
        for t in range(s):
            wy_ref[0, rows, t * LANES:(t + 1) * LANES] = jnp.zeros((LANES, LANES), BF16)
        p = mul_a(p)
    dtc = jnp.exp(ldc_ref[0])
    _, _, mul_ac = step(lrc_ref[0] * dtc, lic_ref[0] * dtc)
    c_re, c_im = cre_ref[0], cim_ref[0]
    bb = jnp.concatenate([bbr, bbi], axis=1).astype(BF16)
    pc = (jnp.ones_like(dtc), jnp.zeros_like(dtc))
    for k in range(S5_T + 1):
        blk = jnp.concatenate([pc[0] * c_re - pc[1] * c_im, -(pc[0] * c_im + pc[1] * c_re)], axis=0).astype(BF16)
        if k >= 1:
            wy_ref[0, tw:, (k - 1) * LANES:k * LANES] = blk
        if k < S5_T:
            lag = _dot(bb, blk).astype(BF16)
            for s in range(S5_T - k):
                t = s + k
                wy_ref[0, s * LANES:(s + 1) * LANES, t * LANES:(t + 1) * LANES] = lag
        pc = mul_ac(pc)


def _s5_prep(a_re, a_im, log_dt, b_re, b_im, c_re, c_im):
    gpt = LANES // SSM_GROUP
    eye = jnp.eye(gpt, dtype=bool)

    def rows(p):
        return p.reshape(S5_TILES, 1, S5_TILE_STATES)

    def cols(p):
        return jnp.broadcast_to(p.reshape(S5_TILES, S5_TILE_STATES, 1), (S5_TILES, S5_TILE_STATES, LANES))

    def bmat(b):
        bt = b.reshape(S5_TILES, gpt, SSM_STATE, SSM_GROUP).transpose(0, 1, 3, 2)
        full = jnp.where(eye[None, :, None, :, None], bt[:, :, :, None, :], 0.0)
        return full.reshape(S5_TILES, LANES, S5_TILE_STATES)

    def cmat(c):
        ct = c.reshape(S5_TILES, gpt, SSM_GROUP, SSM_STATE).transpose(0, 1, 3, 2)
        full = jnp.where(eye[None, :, None, :, None], ct[:, :, :, None, :], 0.0)
        return full.reshape(S5_TILES, S5_TILE_STATES, LANES)

    row_spec = pl.BlockSpec((1, 1, S5_TILE_STATES), lambda j: (j, 0, 0))
    b_spec = pl.BlockSpec((1, LANES, S5_TILE_STATES), lambda j: (j, 0, 0))
    c_spec = pl.BlockSpec((1, S5_TILE_STATES, LANES), lambda j: (j, 0, 0))
    tw = S5_T * LANES
    return pl.pallas_call(
        _s5_prep_body,
        grid=(S5_TILES,),
        in_specs=[row_spec] * 3 + [b_spec] * 2 + [c_spec] * 5,
        out_specs=[pl.BlockSpec((1, tw, 2 * S5_TILE_STATES), lambda j: (j, 0, 0)),
                   pl.BlockSpec((1, tw + 2 * S5_TILE_STATES, tw), lambda j: (j, 0, 0))],
        out_shape=[jax.ShapeDtypeStruct((S5_TILES, tw, 2 * S5_TILE_STATES), BF16),
                   jax.ShapeDtypeStruct((S5_TILES, tw + 2 * S5_TILE_STATES, tw), BF16)],
        compiler_params=_params(1),
        name="s5_prep",
    )(rows(a_re), rows(a_im), rows(log_dt), bmat(b_re), bmat(b_im),
      cols(a_re), cols(a_im), cols(log_dt), cmat(c_re), cmat(c_im))


def _s5_pow_body(ex_ref, lr_ref, li_ref, ld_ref, o_ref):
    dt = jnp.exp(ld_ref[...])
    zr, zi = lr_ref[...] * dt, li_ref[...] * dt
    ex = ex_ref[...]
    mag = jnp.exp(ex * zr)
    o_ref[:, :S5_STATES] = mag * jnp.cos(ex * zi)
    o_ref[:, S5_STATES:] = mag * jnp.sin(ex * zi)


def _s5_pow_table(exponents, a_re, a_im, log_dt):
    n = len(exponents)
    ex = jnp.asarray(np.asarray(exponents, np.float32)[:, None])
    flat = lambda p: p.reshape(1, S5_STATES)
    return pl.pallas_call(
        _s5_pow_body,
        grid=(1,),
        in_specs=[pl.BlockSpec((n, 1), lambda i: (0, 0))] + [pl.BlockSpec((1, S5_STATES), lambda i: (0, 0))] * 3,
        out_specs=pl.BlockSpec((n, 2 * S5_STATES), lambda i: (0, 0)),
        out_shape=jax.ShapeDtypeStruct((n, 2 * S5_STATES), F32),
        compiler_params=_params(1),
        name="s5_pow_table",
    )(ex, flat(a_re), flat(a_im), flat(log_dt))


def _s5_body(u_ref, h0_ref, wg_ref, wy_ref, pow2_ref, powr_ref, d_ref, wglu_ref,
             y_ref, hout_ref, carry_ref, us_ref, ys_ref, *, ni, chained):
    ns = S5_STATES
    ts = S5_TILE_STATES
    segtok = ni * S5_T
    pitch = _s5_pitch(ni)

    @pl.when(pl.program_id(1) == 0)
    def _():
        carry_ref[...] = h0_ref[0]

    for j in range(S5_TILES):
        for s in range(SUBLANES):
            us_ref[j, s * pitch:s * pitch + segtok] = u_ref[0, s * segtok:(s + 1) * segtok, j * LANES:(j + 1) * LANES]

    def gather(j, t):
        return jnp.concatenate([us_ref[j, pl.ds(S5_T * i + t, SUBLANES, stride=pitch), :]
                                for i in range(ni)], axis=0)

    def fma(p_r, p_i, x_r, x_i, y_r, y_i):
        return y_r + p_r * x_r - p_i * x_i, y_i + p_r * x_i + p_i * x_r

    xs, g_re, g_im = [], [], []
    for j in range(S5_TILES):
        xj = jnp.concatenate([gather(j, s) for s in range(S5_T)], axis=1).astype(BF16)
        g = _dot(xj, wg_ref[j])
        xs.append(xj)
        g_re.append(g[:, :ts])
        g_im.append(g[:, ts:])
    g_re = jnp.concatenate(g_re, axis=1)
    g_im = jnp.concatenate(g_im, axis=1)
    slab = lambda a, i: a[i * SUBLANES:(i + 1) * SUBLANES]
    power = lambda ref, r: (ref[r:r + 1, :ns], ref[r:r + 1, ns:])
    a_r, a_i = power(powr_ref, 0)
    cin = carry_ref[...]
    h_r, h_i = cin[:, :ns], cin[:, ns:]
    prev = []
    if chained:
        local = [(slab(g_re, 0), slab(g_im, 0))]
        for i in range(1, ni):
            local.append(fma(a_r, a_i, *local[-1], slab(g_re, i), slab(g_im, i)))
        sub = lax.broadcasted_iota(jnp.int32, (SUBLANES, 1), 0)
        c_r = jnp.where(sub == 0, h_r, pltpu.roll(local[-1][0], 1, 0))
        c_i = jnp.where(sub == 0, h_i, pltpu.roll(local[-1][1], 1, 0))
        for m in range(3):
            k = 1 << m
            s_r = jnp.where(sub >= k, pltpu.roll(c_r, k, 0), 0.0)
            s_i = jnp.where(sub >= k, pltpu.roll(c_i, k, 0), 0.0)
            c_r, c_i = fma(*power(pow2_ref, m), s_r, s_i, c_r, c_i)
        h_r, h_i = c_r, c_i
        for i in range(ni):
            prev.append((h_r, h_i))
            h_r, h_i = fma(*power(powr_ref, i), c_r, c_i, *local[i])
        last = jnp.concatenate([h_r[SUBLANES - 1:], h_i[SUBLANES - 1:]], axis=1)
    else:
        for i in range(ni):
            prev.append((h_r, h_i))
            h_r, h_i = fma(a_r, a_i, h_r, h_i, slab(g_re, i), slab(g_im, i))
        last = jnp.concatenate([h_r, h_i], axis=1)
    carry_ref[...] = last
    hout_ref[0] = last
    prev_re = jnp.concatenate([p[0] for p in prev], axis=0)
    prev_im = jnp.concatenate([p[1] for p in prev], axis=0)
    y_tiles = []
    for j in range(S5_TILES):
        lhs = jnp.concatenate([xs[j], prev_re[:, j * ts:(j + 1) * ts].astype(BF16),
                               prev_im[:, j * ts:(j + 1) * ts].astype(BF16)], axis=1)
        y_tiles.append(_dot(lhs, wy_ref[j]))
    d = d_ref[...]
    wglu = wglu_ref[0]
    def gelu_in(t):
        ut = jnp.concatenate([gather(j, t) for j in range(S5_TILES)], axis=1)
        yt = jnp.concatenate([y_tiles[j][:, t * LANES:(t + 1) * LANES] for j in range(S5_TILES)], axis=1)
        return jax.nn.gelu(yt + d * ut)

    def scatter(t, o):
        for j in range(S5_TILES):
            for i in range(ni):
                ys_ref[j, pl.ds(S5_T * i + t, SUBLANES, stride=pitch), :] = slab(o, i)[:, j * LANES:(j + 1) * LANES]

    g_q, z_q = {}, {}
    for step in range(S5_T + 2):
        if step < S5_T:
            g_q[step] = gelu_in(step)
        if 0 <= step - 1 < S5_T:
            z_q[step - 1] = _dot(g_q[step - 1].astype(BF16), wglu)
        if 0 <= step - 2 < S5_T:
            scatter(step - 2, g_q.pop(step - 2) * jax.nn.sigmoid(z_q.pop(step - 2)))
    for j in range(S5_TILES):
        for s in range(SUBLANES):
            y_ref[0, s * segtok:(s + 1) * segtok, j * LANES:(j + 1) * LANES] = (
                ys_ref[j, s * pitch:s * pitch + segtok].astype(BF16))


def _s5_pitch(ni):
    segtok = ni * S5_T
    return segtok + SUBLANES if (segtok // SUBLANES) % 2 == 0 else segtok


def _s5_mixer(u, h0, wg, wy, pow2, powr, d, w_glu, layer_e, ni, chained):
    nb, ntok, width = u.shape
    tb = SUBLANES * ni * S5_T
    hrows = h0.shape[1]
    blk = pl.BlockSpec((1, tb, width), lambda b, i: (b, i, 0))
    state = pl.BlockSpec((1, hrows, 2 * S5_STATES), lambda b, i: (b, 0, 0))
    return pl.pallas_call(
        functools.partial(_s5_body, ni=ni, chained=chained),
        grid=(nb, ntok // tb),
        in_specs=[blk, state, _const_spec(wg.shape), _const_spec(wy.shape),
                  _const_spec(pow2.shape), _const_spec(powr.shape), _const_spec((1, SSM_WIDTH)),
                  _const_spec((1, SSM_WIDTH, SSM_WIDTH), (layer_e, 0, 0))],
        out_specs=[blk, state],
        out_shape=[jax.ShapeDtypeStruct((nb, ntok, width), BF16),
                   jax.ShapeDtypeStruct((nb, hrows, 2 * S5_STATES), F32)],
        scratch_shapes=[pltpu.VMEM((hrows, 2 * S5_STATES), F32)]
        + [pltpu.VMEM((S5_TILES, SUBLANES * _s5_pitch(ni), LANES), F32)] * 2,
        compiler_params=_params(2),
        name="s5_mixer",
    )(u, h0, wg, wy, pow2, powr, d, w_glu)


def _swa_body(sink_ref, q_ref, kc_ref, vc_ref, kp_ref, vp_ref, pk_ref, pv_ref, o_ref, *, tq, qg, start):
    first = pl.program_id(1) == 0
    k_prev = jnp.where(first, pk_ref[0], kp_ref[0])
    v_prev = jnp.where(first, pv_ref[0], vp_ref[0])
    k_all = jnp.concatenate([k_prev, kc_ref[0]], axis=0)
    v_all = jnp.concatenate([v_prev, vc_ref[0]], axis=0)
    nk = WINDOW + qg
    lane = lax.broadcasted_iota(jnp.int32, (1, KV_WIDTH), 1)
    slot = lax.broadcasted_iota(jnp.int32, (1, nk), 1)
    qrow = lax.broadcasted_iota(jnp.int32, (2 * qg, 1), 0)
    lower = lane < HEAD_DIM
    placed = []
    for kh in range(N_KV_HEADS):
        own = lower if kh == 0 else jnp.logical_not(lower)
        kz, vz = jnp.where(own, k_all, 0.0), jnp.where(own, v_all, 1.0)
        kr, vr = pltpu.roll(kz, HEAD_DIM, 1), pltpu.roll(vz, HEAD_DIM, 1)
        k_lo, k_hi = (kz, kr) if kh == 0 else (kr, kz)
        v_lo, v_hi = (vz, vr) if kh == 0 else (vr, vz)
        placed.append(((k_lo.astype(BF16), v_lo.astype(BF16)), (k_hi.astype(BF16), v_hi.astype(BF16))))
    combos = [(kh, par) for kh in range(N_KV_HEADS) for par in range(2)]

    def scores(c):
        qc = q_ref[0, c * qg:(c + 1) * qg, :]
        out = []
        for kh, par in combos:
            base = kh * GQ * HEAD_DIM
            qs = jnp.concatenate([qc[:, base:base + LANES], qc[:, base + LANES:base + 2 * LANES]], axis=0)
            out.append(_dot_t(qs, placed[kh][par][0][c * qg:c * qg + nk]))
        return out

    def softmax(c, ss):
        n_bad = jnp.where(first, WINDOW - start - c * qg, 0) if c * qg < WINDOW - start else None
        probs, sinks = [], []
        for (kh, par), s in zip(combos, ss):
            if n_bad is not None:
                s = jnp.where(slot >= n_bad, s, NEG)
            sink = jnp.where(qrow < qg, sink_ref[kh * GQ + par], sink_ref[kh * GQ + 2 + par])
            m = jnp.maximum(jnp.max(s, axis=1, keepdims=True), sink)
            probs.append(jnp.exp(s - m).astype(BF16))
            sinks.append(jnp.exp(sink - m))
        return probs, sinks

    def finish(c, probs, sinks):
        pvs = [_dot(p, placed[kh][par][1][c * qg:c * qg + nk]) for (kh, par), p in zip(combos, probs)]
        outs = []
        for kh in range(N_KV_HEADS):
            pv_e, pv_o = pvs[2 * kh], pvs[2 * kh + 1]
            num = jnp.where(lower, pv_e, pv_o)
            den = pltpu.roll(jnp.where(lower, pv_o, pv_e), HEAD_DIM, 1) + jnp.where(lower, sinks[2 * kh], sinks[2 * kh + 1])
            acc = num / den
            outs += [acc[:qg], acc[qg:]]
        o_ref[0, c * qg:(c + 1) * qg, :] = jnp.concatenate(outs, axis=1).astype(BF16)

    nchunk = tq // qg
    per = min(SWA_STAGE_CHUNKS, nchunk)
    groups = [range(g, min(g + per, nchunk)) for g in range(0, nchunk, per)]
    s_q, p_q = {}, {}
    for step in range(len(groups) + 2):
        if step < len(groups):
            s_q[step] = [scores(c) for c in groups[step]]
        if 0 <= step - 1 < len(groups):
            p_q[step - 1] = [softmax(c, ss) for c, ss in zip(groups[step - 1], s_q.pop(step - 1))]
        if 0 <= step - 2 < len(groups):
            for c, (probs, sinks) in zip(groups[step - 2], p_q.pop(step - 2)):
                finish(c, probs, sinks)


def _swa(q, k, v, past_k, past_v, sinks, start, tq, qg):
    bsz, seq, _ = q.shape
    ntiles = seq // tq
    tok = lambda w: pl.BlockSpec((1, tq, w), lambda b, i: (b, i, 0))
    past = pl.BlockSpec((1, WINDOW, KV_WIDTH), lambda b, i: (b, 0, 0))
    if ntiles > 1:
        per = tq // WINDOW
        prev = pl.BlockSpec((1, WINDOW, KV_WIDTH), lambda b, i: (b, jnp.maximum(i * per - 1, 0), 0))
        k_prev, v_prev = k, v
    else:
        prev, k_prev, v_prev = past, past_k, past_v
    return pl.pallas_call(
        functools.partial(_swa_body, tq=tq, qg=qg, start=start),
        grid=(bsz, ntiles),
        in_specs=[pl.BlockSpec(memory_space=pltpu.SMEM), tok(N_HEADS * HEAD_DIM), tok(KV_WIDTH), tok(KV_WIDTH),
                  prev, prev, past, past],
        out_specs=tok(N_HEADS * HEAD_DIM),
        out_shape=jax.ShapeDtypeStruct((bsz, seq, N_HEADS * HEAD_DIM), BF16),
        compiler_params=_params(2),
        name="swa",
    )(sinks, q, k, v, k_prev, v_prev, past_k, past_v)


def _xattn(x1, g, wq, mk_ref, mv_ref, wo, nseq, o_ref):
    h = _rms(x1, g).astype(BF16)
    q = (_dot(h, wq) * X_HEAD_DIM ** -0.5).astype(BF16)
    nparts = nseq if nseq > 1 else XATTN_PARTS
    prows = x1.shape[0] // nparts
    heads = [slice(hd * X_HEAD_DIM, (hd + 1) * X_HEAD_DIM) for hd in range(X_HEADS)]

    def scores(p):
        r, b = slice(p * prows, (p + 1) * prows), (p if nseq > 1 else 0)
        return [_dot_t(q[r, c], mk_ref[0, b, :, c]) for c in heads]

    def softmax(ss):
        out = []
        for s in ss:
            e = jnp.exp(s - jnp.max(s, axis=1, keepdims=True))
            out.append((e.astype(BF16), jnp.sum(e, axis=1, keepdims=True)))
        return out

    def values(p, pd):
        b = p if nseq > 1 else 0
        return jnp.concatenate([_dot(e, mv_ref[0, b, :, c]) / den for c, (e, den) in zip(heads, pd)],
                               axis=1).astype(BF16)

    def finish(p, pd):
        r = slice(p * prows, (p + 1) * prows)
        o_ref[0, r, :] = x1[r] + _dot(values(p, pd), wo)

    if nseq > 1:
        pds = [softmax(ss) for ss in [scores(p) for p in range(nparts)]]
        o = jnp.concatenate([values(p, pd) for p, pd in enumerate(pds)], axis=0)
        o_ref[0] = x1 + _dot(o, wo)
        return
    s_q, p_q = {}, {}
    for step in range(nparts + 2):
        if step < nparts:
            s_q[step] = scores(step)
        if 0 <= step - 1 < nparts:
            p_q[step - 1] = softmax(s_q.pop(step - 1))
        if 0 <= step - 2 < nparts:
            finish(step - 2, p_q.pop(step - 2))


def _conv3(cur, carry, w):
    n = cur.shape[0]
    row = lax.broadcasted_iota(jnp.int32, (n, 1), 0)
    c1, c2 = carry[SUBLANES - 1:SUBLANES], carry[SUBLANES - 2:SUBLANES - 1]
    m1 = jnp.where(row == 0, c1, pltpu.roll(cur, 1, 0))
    m2 = jnp.where(row == 0, c2, jnp.where(row == 1, c1, pltpu.roll(cur, 2, 0)))
    return w[0:1] * m2 + w[1:2] * m1 + w[2:3] * cur


def _conv3_seqs(cur, carries, w):
    nseq = len(carries)
    if nseq == 1:
        return _conv3(cur, carries[0], w)
    rows = cur.shape[0] // nseq
    body = _conv3(cur, carries[0], w)
    pieces = []
    for b in range(nseq):
        pieces.append(_conv3(cur[b * rows:b * rows + SUBLANES], carries[b], w))
        pieces.append(body[b * rows + SUBLANES:(b + 1) * rows])
    return jnp.concatenate(pieces, axis=0)


def _store_last_rows(st_ref, cols, cur, nseq):
    rows = cur.shape[0] // nseq
    for b in range(nseq):
        st_ref[b, :, cols] = cur[(b + 1) * rows - (CONV_WIDTH - 1):(b + 1) * rows]


def _mix_even_xattn_body(x_ref, ya_ref, yb_ref, woa_ref, wob_ref, g_ref, wq_ref, mk_ref, mv_ref, wo_ref, o_ref,
                         *, nseq):
    x1 = x_ref[0] + _dot(ya_ref[0], woa_ref[0]) + _dot(yb_ref[0], wob_ref[0])
    _xattn(x1, g_ref[...], wq_ref[0], mk_ref, mv_ref, wo_ref[0], nseq, o_ref)


def _mix_even_xattn(x, ya, yb, w_out, g, wq, mem_k, mem_v, wo, layer, layer_e, tm, nseq):
    bsz, seq, _ = x.shape
    tok = lambda w: pl.BlockSpec((1, tm, w), lambda b, i: (b, i, 0))
    mem = pl.BlockSpec((1, nseq, N_MEM, D_MODEL), lambda b, i: (layer, b, 0, 0))
    half = D_MODEL // 2
    sq = lambda: _const_spec((1, D_MODEL, D_MODEL), (layer, 0, 0))
    return pl.pallas_call(
        functools.partial(_mix_even_xattn_body, nseq=nseq),
        grid=(bsz, seq // tm),
        in_specs=[tok(D_MODEL), tok(half), tok(half),
                  _const_spec((1, half, D_MODEL), (layer_e, 0, 0)), _const_spec((1, half, D_MODEL), (layer_e, 1, 0)),
                  _const_spec((1, D_MODEL)), sq(), mem, mem, sq()],
        out_specs=tok(D_MODEL),
        out_shape=jax.ShapeDtypeStruct(x.shape, F32),
        compiler_params=_params(2),
        name="mix_even_xattn",
    )(x, ya, yb, w_out, w_out, g, wq, mem_k, mem_v, wo)


def _mix_odd_xattn_body(x_ref, gm_ref, win_ref, cw_ref, wout_ref, prev_ref, g_ref, wq_ref, mk_ref, mv_ref, wo_ref,
                        o_ref, st_ref, carry_ref, *, tm, nseq):
    @pl.when(pl.program_id(1) == 0)
    def _():
        carry_ref[...] = prev_ref[0]

    x = x_ref[0]
    proj = _dot(_rms(x, gm_ref[...]).astype(BF16), win_ref[0])
    gate_b, gate_c, z = proj[:, :D_MODEL], proj[:, D_MODEL:2 * D_MODEL], proj[:, 2 * D_MODEL:]
    cz = gate_c * z
    carries = [carry_ref[...]] if nseq == 1 else [prev_ref[b] for b in range(nseq)]
    z_conv = _conv3_seqs(cz, carries, cw_ref[0])
    carry_ref[...] = cz[tm - SUBLANES:]
    _store_last_rows(st_ref, slice(None), cz, nseq)
    x1 = x + _dot((gate_b * z_conv).astype(BF16), wout_ref[0])
    _xattn(x1, g_ref[...], wq_ref[0], mk_ref, mv_ref, wo_ref[0], nseq, o_ref)


def _mix_odd_xattn(x, gm, w_in, conv_w, w_out, prev, g, wq, mem_k, mem_v, wo, layer, layer_o, tm, nseq):
    bsz, seq, _ = x.shape
    tok = pl.BlockSpec((1, tm, D_MODEL), lambda b, i: (b, i, 0))
    mem = pl.BlockSpec((1, nseq, N_MEM, D_MODEL), lambda b, i: (layer, b, 0, 0))
    hist = pl.BlockSpec((nseq, SUBLANES, D_MODEL), lambda b, i: (b, 0, 0))
    st = pl.BlockSpec((nseq, CONV_WIDTH - 1, D_MODEL), lambda b, i: (b, 0, 0))
    sq = lambda l: _const_spec((1, D_MODEL, D_MODEL), (l, 0, 0))
    return pl.pallas_call(
        functools.partial(_mix_odd_xattn_body, tm=tm, nseq=nseq),
        grid=(bsz, seq // tm),
        in_specs=[tok, _const_spec((1, D_MODEL)), _const_spec((1, D_MODEL, 3 * D_MODEL), (layer_o, 0, 0)),
                  _const_spec((1, CONV_WIDTH, D_MODEL), (layer_o, 0, 0)), sq(layer_o), hist,
                  _const_spec((1, D_MODEL)), sq(layer), mem, mem, sq(layer)],
        out_specs=[tok, st],
        out_shape=[jax.ShapeDtypeStruct(x.shape, F32),
                   jax.ShapeDtypeStruct((bsz * nseq, CONV_WIDTH - 1, D_MODEL), F32)],
        scratch_shapes=[pltpu.VMEM((SUBLANES, D_MODEL), F32)],
        compiler_params=_params(2),
        name="mix_odd_xattn",
    )(x, gm, w_in, conv_w, w_out, prev, g, wq, mem_k, mem_v, wo)


def _ffn_body(x_ref, g_ref, wup_ref, cw_ref, wdn_ref, prev_ref, gfin_ref, o_ref, st_ref, carry_ref,
              *, tm, final, nseq):
    @pl.when(pl.program_id(1) == 0)
    def _():
        carry_ref[...] = prev_ref[0]

    x = x_ref[0]
    h = _rms(x, g_ref[...]).astype(BF16)
    chunks = [slice(lo, min(lo + FF_CHUNK, D_FF)) for lo in range(0, D_FF, FF_CHUNK)]

    def up(cols):
        return _dot(h, wup_ref[0, :, cols]), _dot(h, wup_ref[0, :, D_FF + cols.start:D_FF + cols.stop])

    acc = x
    nxt = up(chunks[0])
    for n, cols in enumerate(chunks):
        (gate, val), nxt = nxt, (up(chunks[n + 1]) if n + 1 < len(chunks) else None)
        carries = [carry_ref[:, cols]] if nseq == 1 else [prev_ref[b, :, cols] for b in range(nseq)]
        gate_c = _conv3_seqs(gate, carries, cw_ref[0, :, cols])
        carry_ref[:, cols] = gate[tm - SUBLANES:]
        _store_last_rows(st_ref, cols, gate, nseq)
        act = (gate_c * jax.nn.sigmoid(gate_c) * val).astype(BF16)
        if n + 1 < len(chunks):
            acc = acc + _dot(act, wdn_ref[0, cols, :])
        else:
            half = tm // 2
            for rows in (slice(0, half), slice(half, tm)):
                out = acc[rows] + _dot(act[rows], wdn_ref[0, cols, :])
                o_ref[0, rows, :] = _rms(out, gfin_ref[...]) if final else out


def _ffn(x, g, w_up, conv_w, w_down, prev, g_final, layer, tm, final, nseq):
    bsz, seq, _ = x.shape
    tok = pl.BlockSpec((1, tm, D_MODEL), lambda b, i: (b, i, 0))
    hist = pl.BlockSpec((nseq, SUBLANES, D_FF), lambda b, i: (b, 0, 0))
    st = pl.BlockSpec((nseq, CONV_WIDTH - 1, D_FF), lambda b, i: (b, 0, 0))
    return pl.pallas_call(
        functools.partial(_ffn_body, tm=tm, final=final, nseq=nseq),
        grid=(bsz, seq // tm),
        in_specs=[tok, _const_spec((1, D_MODEL)), _const_spec((1, D_MODEL, 2 * D_FF), (layer, 0, 0)),
                  _const_spec((1, CONV_WIDTH, D_FF), (layer, 0, 0)), _const_spec((1, D_FF, D_MODEL), (layer, 0, 0)),
                  hist, _const_spec((1, D_MODEL))],
        out_specs=[tok, st],
        out_shape=[jax.ShapeDtypeStruct(x.shape, F32),
                   jax.ShapeDtypeStruct((bsz * nseq, CONV_WIDTH - 1, D_FF), F32)],
        scratch_shapes=[pltpu.VMEM((SUBLANES, D_FF), F32)],
        compiler_params=_params(2),
        name="conv_ffn",
    )(x, g, w_up, conv_w, w_down, prev, g_final)


def _mem_kv_body(m_ref, w_ref, k_ref, v_ref, kf_ref, vf_ref):
    kv = _dot(m_ref[0].astype(BF16), w_ref[0])
    kf_ref[0, 0] = kv[:, :D_MODEL].astype(BF16)
    vf_ref[0, 0] = kv[:, D_MODEL:].astype(BF16)
    for hd in range(X_HEADS):
        k_ref[0, 0, :, hd, :] = kv[:, hd * X_HEAD_DIM:(hd + 1) * X_HEAD_DIM]
        v_ref[0, 0, :, hd, :] = kv[:, D_MODEL + hd * X_HEAD_DIM:D_MODEL + (hd + 1) * X_HEAD_DIM]


def _mem_kv(mem, w_kv):
    bsz = mem.shape[0]
    heads = pl.BlockSpec((1, 1, N_MEM, X_HEADS, X_HEAD_DIM), lambda l, b: (l, b, 0, 0, 0))
    flat = pl.BlockSpec((1, 1, N_MEM, D_MODEL), lambda l, b: (l, b, 0, 0))
    return pl.pallas_call(
        _mem_kv_body,
        grid=(DEPTH, bsz),
        in_specs=[pl.BlockSpec((1, N_MEM, D_MODEL), lambda l, b: (b, 0, 0)),
                  pl.BlockSpec((1, D_MODEL, 2 * D_MODEL), lambda l, b: (l, 0, 0))],
        out_specs=[heads, heads, flat, flat],
        out_shape=[jax.ShapeDtypeStruct((DEPTH, bsz, N_MEM, X_HEADS, X_HEAD_DIM), F32)] * 2
        + [jax.ShapeDtypeStruct((DEPTH, bsz, N_MEM, D_MODEL), BF16)] * 2,
        compiler_params=_params(2),
        name="mem_kv",
    )(mem, w_kv)


def _trunk(x, start, past_k, past_v, ssm_re, ssm_im, conv_mix_prev, conv_ffn_prev, mem_k, mem_v, w, s5, cfg):
    bsz, seq, _ = x.shape
    tq, qg, nseq = cfg["tq"], cfg["qg"], cfg["nseq"]
    tm_in, tm_even, tm_odd, tm_ffn = cfg["tm_in"], cfg["tm_even"], cfg["tm_odd"], cfg["tm_ffn"]
    ni, chained = cfg["ni"], cfg["chained"]
    ngrp, grows = bsz // nseq, nseq * seq
    grouped = lambda t: t.reshape(ngrp, grows, t.shape[-1])
    per_seq = lambda t: t.reshape(bsz, seq, t.shape[-1])
    hist = lambda st: jnp.pad(st, ((0, 0), (SUBLANES - (CONV_WIDTH - 1), 0), (0, 0)))
    tables = _rope_tables(grows, seq, start, tm_in)
    u, q, k, v = _inproj_even(grouped(x), w["norm_mix"][0:1], w["w_in_even"], tables, 0, tm_in)
    h0 = jnp.concatenate([ssm_re[0].reshape(bsz, S5_STATES), ssm_im[0].reshape(bsz, S5_STATES)], axis=1)
    if chained:
        ya, h_all = _s5_mixer(u, h0[:, None, :], s5["wg"], s5["wy"], cfg["pow2"], cfg["powr"], w["ssm_d"],
                              w["w_glu"], 0, ni, True)
        h_last = h_all[:, 0]
    else:
        assert ngrp == 1 and bsz == SUBLANES and seq == ni * S5_T
        ya, h_all = _s5_mixer(u, h0[None], s5["wg"], s5["wy"], cfg["pow2"], cfg["powr"], w["ssm_d"],
                              w["w_glu"], 0, ni, False)
        h_last = h_all[0]
    new_re = h_last[:, :S5_STATES].reshape(1, bsz, SSM_GROUPS, SSM_STATE)
    new_im = h_last[:, S5_STATES:].reshape(1, bsz, SSM_GROUPS, SSM_STATE)
    k, v = per_seq(k), per_seq(v)
    yb = _swa(per_seq(q), k, v, past_k[0].reshape(bsz, WINDOW, KV_WIDTH), past_v[0].reshape(bsz, WINDOW, KV_WIDTH),
              w["attn_sinks"][0], start, tq, qg)
    xg = _mix_even_xattn(grouped(x), ya, grouped(yb), w["w_out_even"], w["norm_xattn"][0:1], w["xattn_wq"],
                         mem_k, mem_v, w["xattn_wo"], 0, 0, tm_even, nseq)
    xg, cf0 = _ffn(xg, w["norm_ffn"][0:1], w["ffn_w_up"], w["ffn_conv_w"], w["ffn_w_down"], hist(conv_ffn_prev[0]),
                   w["norm_final"], 0, tm_ffn, False, nseq)
    xg, cm = _mix_odd_xattn(xg, w["norm_mix"][1:2], w["w_in_odd"], w["conv_mix_w"], w["w_out_odd"],
                            hist(conv_mix_prev[0]), w["norm_xattn"][1:2], w["xattn_wq"], mem_k, mem_v,
                            w["xattn_wo"], 1, 0, tm_odd, nseq)
    y, cf1 = _ffn(xg, w["norm_ffn"][1:2], w["ffn_w_up"], w["ffn_conv_w"], w["ffn_w_down"], hist(conv_ffn_prev[1]),
                  w["norm_final"], 1, tm_ffn, True, nseq)
    keep = min(seq, WINDOW)
    new_k = k[:, seq - keep:].reshape(1, bsz, keep, N_KV_HEADS, HEAD_DIM)
    new_v = v[:, seq - keep:].reshape(1, bsz, keep, N_KV_HEADS, HEAD_DIM)
    return per_seq(y), new_k, new_v, new_re, new_im, cm[None], jnp.stack([cf0, cf1])


def kernel(x_prompt, x_sample, mem_prompt, cache_win_k, cache_win_v, state_ssm_re, state_ssm_im, state_conv_mix, state_conv_ffn, cache_mem_k, cache_mem_v, norm_mix, norm_xattn, norm_ffn, norm_final, w_in_even, w_out_even, ssm_a_re, ssm_a_im, ssm_log_dt, ssm_b_re, ssm_b_im, ssm_c_re, ssm_c_im, ssm_d, w_glu, attn_sinks, w_in_odd, conv_mix_w, w_out_odd, xattn_wq, xattn_wkv, xattn_wo, ffn_w_up, ffn_conv_w, ffn_w_down):
    bp, seq_p, _ = x_prompt.shape
    bs, seq_s, _ = x_sample.shape
    w = dict(norm_mix=norm_mix, norm_xattn=norm_xattn, norm_ffn=norm_ffn, norm_final=norm_final.reshape(1, D_MODEL),
             w_in_even=w_in_even.astype(BF16), w_out_even=w_out_even.astype(BF16),
             ssm_d=ssm_d[0].reshape(1, SSM_WIDTH), w_glu=w_glu.astype(BF16), attn_sinks=attn_sinks,
             w_in_odd=w_in_odd.astype(BF16), conv_mix_w=conv_mix_w, w_out_odd=w_out_odd.astype(BF16),
             xattn_wq=xattn_wq.astype(BF16), xattn_wo=xattn_wo.astype(BF16),
             ffn_w_up=ffn_w_up.astype(BF16), ffn_conv_w=ffn_conv_w, ffn_w_down=ffn_w_down.astype(BF16))

    wg, wy = _s5_prep(ssm_a_re[0], ssm_a_im[0], ssm_log_dt[0], ssm_b_re[0], ssm_b_im[0], ssm_c_re[0], ssm_c_im[0])
    s5 = dict(wg=wg, wy=wy)
    ni_p, ni_s = S5_SEG_CHUNKS, seq_s // S5_T
    padded = lambda e: e + [1.0] * (-len(e) % SUBLANES)
    exps = (padded([float(S5_T * ni_p << m) for m in range(3)])
            + padded([float(S5_T * (i + 1)) for i in range(ni_p)])
            + padded([float(S5_T * (i + 1)) for i in range(ni_s)]))
    pows = _s5_pow_table(exps, ssm_a_re[0], ssm_a_im[0], ssm_log_dt[0])
    o1 = SUBLANES
    o2 = o1 + len(padded([0.0] * ni_p))
    cfg_p = dict(tm_in=1024, tm_even=1024, tm_odd=1024, tm_ffn=1024, tq=512, qg=CHUNK, nseq=1, ni=ni_p, chained=True,
                 pow2=pows[:o1], powr=pows[o1:o2])
    rows_s = bs * seq_s
    cfg_s = dict(tm_in=rows_s, tm_even=rows_s, tm_odd=rows_s, tm_ffn=rows_s, tq=seq_s, qg=seq_s, nseq=bs, ni=ni_s,
                 chained=False, pow2=pows[:o1], powr=pows[o2:])

    mem_k_p, mem_v_p, mem_kf_p, mem_vf_p = _mem_kv(mem_prompt, xattn_wkv.astype(BF16))
    cw = CONV_WIDTH - 1
    zk = jnp.zeros((1, bp, WINDOW, N_KV_HEADS, HEAD_DIM), F32)
    zs = jnp.zeros((1, bp, SSM_GROUPS, SSM_STATE), F32)
    zcm = jnp.zeros((1, bp, cw, D_MODEL), F32)
    zcf = jnp.zeros((DEPTH, bp, cw, D_FF), F32)
    y_p, k_p, v_p, re_p, im_p, cm_p, cf_p = _trunk(x_prompt, 0, zk, zk, zs, zs, zcm, zcf, mem_kf_p, mem_vf_p, w, s5, cfg_p)
    mem_kf_s = cache_mem_k.reshape(DEPTH, bs, N_MEM, D_MODEL).astype(BF16)
    mem_vf_s = cache_mem_v.reshape(DEPTH, bs, N_MEM, D_MODEL).astype(BF16)
    y_s, k_s, v_s, re_s, im_s, cm_s, cf_s = _trunk(x_sample, PAST_LEN, cache_win_k, cache_win_v, state_ssm_re,
                                                   state_ssm_im, state_conv_mix, state_conv_ffn, mem_kf_s,
                                                   mem_vf_s, w, s5, cfg_s)
    return (y_p, y_s, k_p, v_p, re_p, im_p, cm_p, cf_p, mem_k_p, mem_v_p, k_s, v_s, re_s, im_s, cm_s, cf_s)
```

```python
import functools
import math

import numpy as np
import jax
import jax.numpy as jnp
from jax import lax
from jax.experimental import pallas as pl
from jax.experimental.pallas import tpu as pltpu

F32 = jnp.float32
BF16 = jnp.bfloat16

D_MODEL = 1024
DEPTH = 2
PAST_LEN = 2048
CHUNK = 64
SSM_WIDTH = 512
SSM_GROUP = 16
SSM_GROUPS = 32
SSM_STATE = 64
HEAD_DIM = 64
N_HEADS = 8
N_KV_HEADS = 2
GQ = N_HEADS // N_KV_HEADS
WINDOW = 128
ROPE_THETA = 10000.0
KV_WIDTH = N_KV_HEADS * HEAD_DIM
EVEN_IN = SSM_WIDTH + N_HEADS * HEAD_DIM + 2 * KV_WIDTH
CONV_WIDTH = 3
N_MEM = 256
X_HEADS = 4
X_HEAD_DIM = 256
D_FF = 2816
EPS = 1e-6
NEG = -1e30

LANES = 128
SUBLANES = 8
S5_T = 8
S5_TILES = SSM_WIDTH // LANES
S5_TILE_STATES = (LANES // SSM_GROUP) * SSM_STATE
S5_STATES = SSM_GROUPS * SSM_STATE
SWA_STAGE_CHUNKS = 1
XATTN_PARTS = 2
S5_SEG_CHUNKS = 16
MXU_DIM = 256
FF_CHUNK = 6 * MXU_DIM
VMEM_LIMIT = 56 * 1024 * 1024


def _const_spec(shape, index=None):
    idx = tuple(index) if index is not None else (0,) * len(shape)
    return pl.BlockSpec(shape, lambda *_: idx, pipeline_mode=pl.Buffered(1))


def _params(n_axes):
    return pltpu.CompilerParams(dimension_semantics=("arbitrary",) * n_axes,
                                vmem_limit_bytes=VMEM_LIMIT)


def _rms(x, g):
    return x * lax.rsqrt(jnp.mean(x * x, axis=-1, keepdims=True) + EPS) * g


def _dot(a, b):
    return jnp.dot(a, b, preferred_element_type=F32)


def _dot_t(a, b):
    return lax.dot_general(a, b, (((1,), (1,)), ((), ())), preferred_element_type=F32)


def _rope_table_body(cos_ref, sina_ref, sinb_ref, *, start, tl, period):
    i = pl.program_id(0)
    row = lax.broadcasted_iota(jnp.int32, (tl, LANES), 0)
    lane = lax.broadcasted_iota(jnp.int32, (tl, LANES), 1)
    pos = (start + ((i * tl + row) & (period - 1))).astype(F32)
    half = HEAD_DIM // 2
    freq = (lane[0:1] & (half - 1)).astype(F32)
    inv = jnp.exp(freq * (-2.0 / HEAD_DIM * math.log(ROPE_THETA)))
    ang = pos * inv
    c = jnp.cos(ang)
    s = jnp.sin(ang)
    first = (lane & (HEAD_DIM - 1)) < HEAD_DIM // 2
    cos_ref[...] = c
    sina_ref[...] = jnp.where(first, -s, 0.0)
    sinb_ref[...] = jnp.where(first, 0.0, s)


def _rope_tables(rows, period, start, tl):
    spec = pl.BlockSpec((tl, LANES), lambda i: (i, 0))
    return pl.pallas_call(
        functools.partial(_rope_table_body, start=start, tl=tl, period=period),
        grid=(rows // tl,),
        in_specs=[],
        out_specs=[spec, spec, spec],
        out_shape=[jax.ShapeDtypeStruct((rows, LANES), F32)] * 3,
        compiler_params=_params(1),
        name="rope_tables",
    )()


def _rope(t, cos, sina, sinb):
    n = t.shape[1]
    half = HEAD_DIM // 2
    return t * cos + pltpu.roll(t, n - half, 1) * sina + pltpu.roll(t, half, 1) * sinb


def _inproj_even_body(x_ref, g_ref, w_ref, cos_ref, sa_ref, sb_ref, u_ref, q_ref, k_ref, v_ref):
    o0 = SSM_WIDTH
    o1 = o0 + N_HEADS * HEAD_DIM
    o2 = o1 + KV_WIDTH
    tm = x_ref.shape[1]
    nparts = 2 if tm % (2 * LANES) == 0 else 1
    parts = [slice(p * tm // nparts, (p + 1) * tm // nparts) for p in range(nparts)]
    projs = [_dot(_rms(x_ref[0, r, :], g_ref[...]).astype(BF16), w_ref[0]) for r in parts]
    for r, proj in zip(parts, projs):
        cos, sa, sb = cos_ref[r, :], sa_ref[r, :], sb_ref[r, :]
        u_ref[0, r, :] = proj[:, :o0]
        q = [_rope(proj[:, c:c + LANES], cos, sa, sb) for c in range(o0, o1, LANES)]
        q_ref[0, r, :] = (jnp.concatenate(q, axis=1) * HEAD_DIM ** -0.5).astype(BF16)
        k_ref[0, r, :] = _rope(proj[:, o1:o2], cos, sa, sb)
        v_ref[0, r, :] = proj[:, o2:]


def _inproj_even(x, g, w_in, tables, layer_e, tm):
    bsz, seq, _ = x.shape
    cos, sa, sb = tables
    tok = lambda w: pl.BlockSpec((1, tm, w), lambda b, i: (b, i, 0))
    tab = pl.BlockSpec((tm, LANES), lambda b, i: (i, 0))
    return pl.pallas_call(
        _inproj_even_body,
        grid=(bsz, seq // tm),
        in_specs=[tok(D_MODEL), _const_spec((1, D_MODEL)),
                  _const_spec((1, D_MODEL, EVEN_IN), (layer_e, 0, 0)), tab, tab, tab],
        out_specs=[tok(SSM_WIDTH), tok(N_HEADS * HEAD_DIM), tok(KV_WIDTH), tok(KV_WIDTH)],
        out_shape=[jax.ShapeDtypeStruct((bsz, seq, SSM_WIDTH), F32),
                   jax.ShapeDtypeStruct((bsz, seq, N_HEADS * HEAD_DIM), BF16),
                   jax.ShapeDtypeStruct((bsz, seq, KV_WIDTH), F32),
                   jax.ShapeDtypeStruct((bsz, seq, KV_WIDTH), F32)],
        compiler_params=_params(2),
        name="inproj_even",
    )(x, g, w_in, cos, sa, sb)


def _s5_prep_body(lr_ref, li_ref, ld_ref, bre_ref, bim_ref, lrc_ref, lic_ref, ldc_ref,
                  cre_ref, cim_ref, wg_ref, wy_ref):
    ts = S5_TILE_STATES
    tw = S5_T * LANES

    def step(z_r, z_i):
        mag = jnp.exp(z_r)
        a_r, a_i = mag * jnp.cos(z_i), mag * jnp.sin(z_i)
        return a_r, a_i, lambda p: (p[0] * a_r - p[1] * a_i, p[0] * a_i + p[1] * a_r)

    lr, li = lr_ref[0], li_ref[0]
    dt = jnp.exp(ld_ref[0])
    ar, ai, mul_a = step(lr * dt, li * dt)
    nrm = lr * lr + li * li
    cbr = ((ar - 1.0) * lr + ai * li) / nrm
    cbi = (ai * lr - (ar - 1.0) * li) / nrm
    b_re, b_im = bre_ref[0], bim_ref[0]
    bbr = cbr * b_re - cbi * b_im
    bbi = cbr * b_im + cbi * b_re
    p = (jnp.ones_like(ar), jnp.zeros_like(ar))
    for s in reversed(range(S5_T)):
        rows = slice(s * LANES, (s + 1) * LANES)
        wg_ref[0, rows, :ts] = (p[0] * bbr - p[1] * bbi).astype(BF16)
        wg_ref[0, rows, ts:] = (p[0] * bbi + p[1] * bbr).astype(BF16)
        for t in range(s):
            wy_ref[0, rows, t * LANES:(t + 1) * LANES] = jnp.zeros((LANES, LANES), BF16)
        p = mul_a(p)
    dtc = jnp.exp(ldc_ref[0])
    _, _, mul_ac = step(lrc_ref[0] * dtc, lic_ref[0] * dtc)
    c_re, c_im = cre_ref[0], cim_ref[0]
    bb = jnp.concatenate([bbr, bbi], axis=1).astype(BF16)
    pc = (jnp.ones_like(dtc), jnp.zeros_like(dtc))
    for k in range(S5_T + 1):
        blk = jnp.concatenate([pc[0] * c_re - pc[1] * c_im, -(pc[0] * c_im + pc[1] * c_re)], axis=0).astype(BF16)
        if k >= 1:
            wy_ref[0, tw:, (k - 1) * LANES:k * LANES] = blk
        if k < S5_T:
            lag = _dot(bb, blk).astype(BF16)
            for s in range(S5_T - k):
                t = s + k
                wy_ref[0, s * LANES:(s + 1) * LANES, t * LANES:(t + 1) * LANES] = lag
        pc = mul_ac(pc)


def _s5_prep(a_re, a_im, log_dt, b_re, b_im, c_re, c_im):
    gpt = LANES // SSM_GROUP
    eye = jnp.eye(gpt, dtype=bool)

    def rows(p):
        return p.reshape(S5_TILES, 1, S5_TILE_STATES)

    def cols(p):
        return jnp.broadcast_to(p.reshape(S5_TILES, S5_TILE_STATES, 1), (S5_TILES, S5_TILE_STATES, LANES))

    def bmat(b):
        bt = b.reshape(S5_TILES, gpt, SSM_STATE, SSM_GROUP).transpose(0, 1, 3, 2)
        full = jnp.where(eye[None, :, None, :, None], bt[:, :, :, None, :], 0.0)
        return full.reshape(S5_TILES, LANES, S5_TILE_STATES)

    def cmat(c):
        ct = c.reshape(S5_TILES, gpt, SSM_GROUP, SSM_STATE).transpose(0, 1, 3, 2)
        full = jnp.where(eye[None, :, None, :, None], ct[:, :, :, None, :], 0.0)
        return full.reshape(S5_TILES, S5_TILE_STATES, LANES)

    row_spec = pl.BlockSpec((1, 1, S5_TILE_STATES), lambda j: (j, 0, 0))
    b_spec = pl.BlockSpec((1, LANES, S5_TILE_STATES), lambda j: (j, 0, 0))
    c_spec = pl.BlockSpec((1, S5_TILE_STATES, LANES), lambda j: (j, 0, 0))
    tw = S5_T * LANES
    return pl.pallas_call(
        _s5_prep_body,
        grid=(S5_TILES,),
        in_specs=[row_spec] * 3 + [b_spec] * 2 + [c_spec] * 5,
        out_specs=[pl.BlockSpec((1, tw, 2 * S5_TILE_STATES), lambda j: (j, 0, 0)),
                   pl.BlockSpec((1, tw + 2 * S5_TILE_STATES, tw), lambda j: (j, 0, 0))],
        out_shape=[jax.ShapeDtypeStruct((S5_TILES, tw, 2 * S5_TILE_STATES), BF16),
                   jax.ShapeDtypeStruct((S5_TILES, tw + 2 * S5_TILE_STATES, tw), BF16)],
        compiler_params=_params(1),
        name="s5_prep",
    )(rows(a_re), rows(a_im), rows(log_dt), bmat(b_re), bmat(b_im),
      cols(a_re), cols(a_im), cols(log_dt), cmat(c_re), cmat(c_im))


def _s5_pow_body(ex_ref, lr_ref, li_ref, ld_ref, o_ref):
    dt = jnp.exp(ld_ref[...])
    zr, zi = lr_ref[...] * dt, li_ref[...] * dt
    ex = ex_ref[...]
    mag = jnp.exp(ex * zr)
    o_ref[:, :S5_STATES] = mag * jnp.cos(ex * zi)
    o_ref[:, S5_STATES:] = mag * jnp.sin(ex * zi)


def _s5_pow_table(exponents, a_re, a_im, log_dt):
    n = len(exponents)
    ex = jnp.asarray(np.asarray(exponents, np.float32)[:, None])
    flat = lambda p: p.reshape(1, S5_STATES)
    return pl.pallas_call(
        _s5_pow_body,
        grid=(1,),
        in_specs=[pl.BlockSpec((n, 1), lambda i: (0, 0))] + [pl.BlockSpec((1, S5_STATES), lambda i: (0, 0))] * 3,
        out_specs=pl.BlockSpec((n, 2 * S5_STATES), lambda i: (0, 0)),
        out_shape=jax.ShapeDtypeStruct((n, 2 * S5_STATES), F32),
        compiler_params=_params(1),
        name="s5_pow_table",
    )(ex, flat(a_re), flat(a_im), flat(log_dt))


def _s5_body(u_ref, h0_ref, wg_ref, wy_ref, pow2_ref, powr_ref, d_ref, wglu_ref,
             y_ref, hout_ref, carry_ref, us_ref, ys_ref, *, ni, chained):
    ns = S5_STATES
    ts = S5_TILE_STATES
    segtok = ni * S5_T
    pitch = _s5_pitch(ni)

    @pl.when(pl.program_id(1) == 0)
    def _():
        carry_ref[...] = h0_ref[0]

    for j in range(S5_TILES):
        for s in range(SUBLANES):
            us_ref[j, s * pitch:s * pitch + segtok] = u_ref[0, s * segtok:(s + 1) * segtok, j * LANES:(j + 1) * LANES]

    def gather(j, t):
        return jnp.concatenate([us_ref[j, pl.ds(S5_T * i + t, SUBLANES, stride=pitch), :]
                                for i in range(ni)], axis=0)

    def fma(p_r, p_i, x_r, x_i, y_r, y_i):
        return y_r + p_r * x_r - p_i * x_i, y_i + p_r * x_i + p_i * x_r

    xs, g_re, g_im = [], [], []
    for j in range(S5_TILES):
        xj = jnp.concatenate([gather(j, s) for s in range(S5_T)], axis=1).astype(BF16)
        g = _dot(xj, wg_ref[j])
        xs.append(xj)
        g_re.append(g[:, :ts])
        g_im.append(g[:, ts:])
    g_re = jnp.concatenate(g_re, axis=1)
    g_im = jnp.concatenate(g_im, axis=1)
    slab = lambda a, i: a[i * SUBLANES:(i + 1) * SUBLANES]
    power = lambda ref, r: (ref[r:r + 1, :ns], ref[r:r + 1, ns:])
    a_r, a_i = power(powr_ref, 0)
    cin = carry_ref[...]
    h_r, h_i = cin[:, :ns], cin[:, ns:]
    prev = []
    if chained:
        local = [(slab(g_re, 0), slab(g_im, 0))]
        for i in range(1, ni):
            local.append(fma(a_r, a_i, *local[-1], slab(g_re, i), slab(g_im, i)))
        sub = lax.broadcasted_iota(jnp.int32, (SUBLANES, 1), 0)
        c_r = jnp.where(sub == 0, h_r, pltpu.roll(local[-1][0], 1, 0))
        c_i = jnp.where(sub == 0, h_i, pltpu.roll(local[-1][1], 1, 0))
        for m in range(3):
            k = 1 << m
            s_r = jnp.where(sub >= k, pltpu.roll(c_r, k, 0), 0.0)
            s_i = jnp.where(sub >= k, pltpu.roll(c_i, k, 0), 0.0)
            c_r, c_i = fma(*power(pow2_ref, m), s_r, s_i, c_r, c_i)
        h_r, h_i = c_r, c_i
        for i in range(ni):
            prev.append((h_r, h_i))
            h_r, h_i = fma(*power(powr_ref, i), c_r, c_i, *local[i])
        last = jnp.concatenate([h_r[SUBLANES - 1:], h_i[SUBLANES - 1:]], axis=1)
    else:
        for i in range(ni):
            prev.append((h_r, h_i))
            h_r, h_i = fma(a_r, a_i, h_r, h_i, slab(g_re, i), slab(g_im, i))
        last = jnp.concatenate([h_r, h_i], axis=1)
    carry_ref[...] = last
    hout_ref[0] = last
    prev_re = jnp.concatenate([p[0] for p in prev], axis=0)
    prev_im = jnp.concatenate([p[1] for p in prev], axis=0)
    y_tiles = []
    for j in range(S5_TILES):
        lhs = jnp.concatenate([xs[j], prev_re[:, j * ts:(j + 1) * ts].astype(BF16),
                               prev_im[:, j * ts:(j + 1) * ts].astype(BF16)], axis=1)
        y_tiles.append(_dot(lhs, wy_ref[j]))
    d = d_ref[...]
    wglu = wglu_ref[0]
    def gelu_in(t):
        ut = jnp.concatenate([gather(j, t) for j in range(S5_TILES)], axis=1)
        yt = jnp.concatenate([y_tiles[j][:, t * LANES:(t + 1) * LANES] for j in range(S5_TILES)], axis=1)
        return jax.nn.gelu(yt + d * ut)

    def scatter(t, o):
        for j in range(S5_TILES):
            for i in range(ni):
                ys_ref[j, pl.ds(S5_T * i + t, SUBLANES, stride=pitch), :] = slab(o, i)[:, j * LANES:(j + 1) * LANES]

    g_q, z_q = {}, {}
    for step in range(S5_T + 2):
        if step < S5_T:
            g_q[step] = gelu_in(step)
        if 0 <= step - 1 < S5_T:
            z_q[step - 1] = _dot(g_q[step - 1].astype(BF16), wglu)
        if 0 <= step - 2 < S5_T:
            scatter(step - 2, g_q.pop(step - 2) * jax.nn.sigmoid(z_q.pop(step - 2)))
    for j in range(S5_TILES):
        for s in range(SUBLANES):
            y_ref[0, s * segtok:(s + 1) * segtok, j * LANES:(j + 1) * LANES] = (
                ys_ref[j, s * pitch:s * pitch + segtok].astype(BF16))


def _s5_pitch(ni):
    segtok = ni * S5_T
    return segtok + SUBLANES if (segtok // SUBLANES) % 2 == 0 else segtok


def _s5_mixer(u, h0, wg, wy, pow2, powr, d, w_glu, layer_e, ni, chained):
    nb, ntok, width = u.shape
    tb = SUBLANES * ni * S5_T
    hrows = h0.shape[1]
    blk = pl.BlockSpec((1, tb, width), lambda b, i: (b, i, 0))
    state = pl.BlockSpec((1, hrows, 2 * S5_STATES), lambda b, i: (b, 0, 0))
    return pl.pallas_call(
        functools.partial(_s5_body, ni=ni, chained=chained),
        grid=(nb, ntok // tb),
        in_specs=[blk, state, _const_spec(wg.shape), _const_spec(wy.shape),
                  _const_spec(pow2.shape), _const_spec(powr.shape), _const_spec((1, SSM_WIDTH)),
                  _const_spec((1, SSM_WIDTH, SSM_WIDTH), (layer_e, 0, 0))],
        out_specs=[blk, state],
        out_shape=[jax.ShapeDtypeStruct((nb, ntok, width), BF16),
                   jax.ShapeDtypeStruct((nb, hrows, 2 * S5_STATES), F32)],
        scratch_shapes=[pltpu.VMEM((hrows, 2 * S5_STATES), F32)]
        + [pltpu.VMEM((S5_TILES, SUBLANES * _s5_pitch(ni), LANES), F32)] * 2,
        compiler_params=_params(2),
        name="s5_mixer",
    )(u, h0, wg, wy, pow2, powr, d, w_glu)


def _swa_body(sink_ref, q_ref, kc_ref, vc_ref, kp_ref, vp_ref, pk_ref, pv_ref, o_ref, *, tq, qg, start):
    first = pl.program_id(1) == 0
    k_prev = jnp.where(first, pk_ref[0], kp_ref[0])
    v_prev = jnp.where(first, pv_ref[0], vp_ref[0])
    k_all = jnp.concatenate([k_prev, kc_ref[0]], axis=0)
    v_all = jnp.concatenate([v_prev, vc_ref[0]], axis=0)
    nk = WINDOW + qg
    lane = lax.broadcasted_iota(jnp.int32, (1, KV_WIDTH), 1)
    slot = lax.broadcasted_iota(jnp.int32, (1, nk), 1)
    qrow = lax.broadcasted_iota(jnp.int32, (2 * qg, 1), 0)
    lower = lane < HEAD_DIM
    placed = []
    for kh in range(N_KV_HEADS):
        own = lower if kh == 0 else jnp.logical_not(lower)
        kz, vz = jnp.where(own, k_all, 0.0), jnp.where(own, v_all, 1.0)
        kr, vr = pltpu.roll(kz, HEAD_DIM, 1), pltpu.roll(vz, HEAD_DIM, 1)
        k_lo, k_hi = (kz, kr) if kh == 0 else (kr, kz)
        v_lo, v_hi = (vz, vr) if kh == 0 else (vr, vz)
        placed.append(((k_lo.astype(BF16), v_lo.astype(BF16)), (k_hi.astype(BF16), v_hi.astype(BF16))))
    combos = [(kh, par) for kh in range(N_KV_HEADS) for par in range(2)]

    def scores(c):
        qc = q_ref[0, c * qg:(c + 1) * qg, :]
        out = []
        for kh, par in combos:
            base = kh * GQ * HEAD_DIM
            qs = jnp.concatenate([qc[:, base:base + LANES], qc[:, base + LANES:base + 2 * LANES]], axis=0)
            out.append(_dot_t(qs, placed[kh][par][0][c * qg:c * qg + nk]))
        return out

    def softmax(c, ss):
        n_bad = jnp.where(first, WINDOW - start - c * qg, 0) if c * qg < WINDOW - start else None
        probs, sinks = [], []
        for (kh, par), s in zip(combos, ss):
            if n_bad is not None:
                s = jnp.where(slot >= n_bad, s, NEG)
            sink = jnp.where(qrow < qg, sink_ref[kh * GQ + par], sink_ref[kh * GQ + 2 + par])
            m = jnp.maximum(jnp.max(s, axis=1, keepdims=True), sink)
            probs.append(jnp.exp(s - m).astype(BF16))
            sinks.append(jnp.exp(sink - m))
        return probs, sinks

    def finish(c, probs, sinks):
        pvs = [_dot(p, placed[kh][par][1][c * qg:c * qg + nk]) for (kh, par), p in zip(combos, probs)]
        outs = []
        for kh in range(N_KV_HEADS):
            pv_e, pv_o = pvs[2 * kh], pvs[2 * kh + 1]
            num = jnp.where(lower, pv_e, pv_o)
            den = pltpu.roll(jnp.where(lower, pv_o, pv_e), HEAD_DIM, 1) + jnp.where(lower, sinks[2 * kh], sinks[2 * kh + 1])
            acc = num / den
            outs += [acc[:qg], acc[qg:]]
        o_ref[0, c * qg:(c + 1) * qg, :] = jnp.concatenate(outs, axis=1).astype(BF16)

    nchunk = tq // qg
    per = min(SWA_STAGE_CHUNKS, nchunk)
    groups = [range(g, min(g + per, nchunk)) for g in range(0, nchunk, per)]
    s_q, p_q = {}, {}
    for step in range(len(groups) + 2):
        if step < len(groups):
            s_q[step] = [scores(c) for c in groups[step]]
        if 0 <= step - 1 < len(groups):
            p_q[step - 1] = [softmax(c, ss) for c, ss in zip(groups[step - 1], s_q.pop(step - 1))]
        if 0 <= step - 2 < len(groups):
            for c, (probs, sinks) in zip(groups[step - 2], p_q.pop(step - 2)):
                finish(c, probs, sinks)


def _swa(q, k, v, past_k, past_v, sinks, start, tq, qg):
    bsz, seq, _ = q.shape
    ntiles = seq // tq
    tok = lambda w: pl.BlockSpec((1, tq, w), lambda b, i: (b, i, 0))
    past = pl.BlockSpec((1, WINDOW, KV_WIDTH), lambda b, i: (b, 0, 0))
    if ntiles > 1:
        per = tq // WINDOW
        prev = pl.BlockSpec((1, WINDOW, KV_WIDTH), lambda b, i: (b, jnp.maximum(i * per - 1, 0), 0))
        k_prev, v_prev = k, v
    else:
        prev, k_prev, v_prev = past, past_k, past_v
    return pl.pallas_call(
        functools.partial(_swa_body, tq=tq, qg=qg, start=start),
        grid=(bsz, ntiles),
        in_specs=[pl.BlockSpec(memory_space=pltpu.SMEM), tok(N_HEADS * HEAD_DIM), tok(KV_WIDTH), tok(KV_WIDTH),
                  prev, prev, past, past],
        out_specs=tok(N_HEADS * HEAD_DIM),
        out_shape=jax.ShapeDtypeStruct((bsz, seq, N_HEADS * HEAD_DIM), BF16),
        compiler_params=_params(2),
        name="swa",
    )(sinks, q, k, v, k_prev, v_prev, past_k, past_v)


def _xattn(x1, g, wq, mk_ref, mv_ref, wo, nseq, o_ref):
    h = _rms(x1, g).astype(BF16)
    q = (_dot(h, wq) * X_HEAD_DIM ** -0.5).astype(BF16)
    nparts = nseq if nseq > 1 else XATTN_PARTS
    prows = x1.shape[0] // nparts
    heads = [slice(hd * X_HEAD_DIM, (hd + 1) * X_HEAD_DIM) for hd in range(X_HEADS)]

    def scores(p):
        r, b = slice(p * prows, (p + 1) * prows), (p if nseq > 1 else 0)
        return [_dot_t(q[r, c], mk_ref[0, b, :, c]) for c in heads]

    def softmax(ss):
        out = []
        for s in ss:
            e = jnp.exp(s - jnp.max(s, axis=1, keepdims=True))
            out.append((e.astype(BF16), jnp.sum(e, axis=1, keepdims=True)))
        return out

    def values(p, pd):
        b = p if nseq > 1 else 0
        return jnp.concatenate([_dot(e, mv_ref[0, b, :, c]) / den for c, (e, den) in zip(heads, pd)],
                               axis=1).astype(BF16)

    def finish(p, pd):
        r = slice(p * prows, (p + 1) * prows)
        o_ref[0, r, :] = x1[r] + _dot(values(p, pd), wo)

    if nseq > 1:
        pds = [softmax(ss) for ss in [scores(p) for p in range(nparts)]]
        o = jnp.concatenate([values(p, pd) for p, pd in enumerate(pds)], axis=0)
        o_ref[0] = x1 + _dot(o, wo)
        return
    s_q, p_q = {}, {}
    for step in range(nparts + 2):
        if step < nparts:
            s_q[step] = scores(step)
        if 0 <= step - 1 < nparts:
            p_q[step - 1] = softmax(s_q.pop(step - 1))
        if 0 <= step - 2 < nparts:
            finish(step - 2, p_q.pop(step - 2))


def _conv3(cur, carry, w):
    n = cur.shape[0]
    row = lax.broadcasted_iota(jnp.int32, (n, 1), 0)
    c1, c2 = carry[SUBLANES - 1:SUBLANES], carry[SUBLANES - 2:SUBLANES - 1]
    m1 = jnp.where(row == 0, c1, pltpu.roll(cur, 1, 0))
    m2 = jnp.where(row == 0, c2, jnp.where(row == 1, c1, pltpu.roll(cur, 2, 0)))
    return w[0:1] * m2 + w[1:2] * m1 + w[2:3] * cur


def _conv3_seqs(cur, carries, w):
    nseq = len(carries)
    if nseq == 1:
        return _conv3(cur, carries[0], w)
    rows = cur.shape[0] // nseq
    body = _conv3(cur, carries[0], w)
    pieces = []
    for b in range(nseq):
        pieces.append(_conv3(cur[b * rows:b * rows + SUBLANES], carries[b], w))
        pieces.append(body[b * rows + SUBLANES:(b + 1) * rows])
    return jnp.concatenate(pieces, axis=0)


def _store_last_rows(st_ref, cols, cur, nseq):
    rows = cur.shape[0] // nseq
    for b in range(nseq):
        st_ref[b, :, cols] = cur[(b + 1) * rows - (CONV_WIDTH - 1):(b + 1) * rows]


def _mix_even_xattn_body(x_ref, ya_ref, yb_ref, woa_ref, wob_ref, g_ref, wq_ref, mk_ref, mv_ref, wo_ref, o_ref,
                         *, nseq):
    x1 = x_ref[0] + _dot(ya_ref[0], woa_ref[0]) + _dot(yb_ref[0], wob_ref[0])
    _xattn(x1, g_ref[...], wq_ref[0], mk_ref, mv_ref, wo_ref[0], nseq, o_ref)


def _mix_even_xattn(x, ya, yb, w_out, g, wq, mem_k, mem_v, wo, layer, layer_e, tm, nseq):
    bsz, seq, _ = x.shape
    tok = lambda w: pl.BlockSpec((1, tm, w), lambda b, i: (b, i, 0))
    mem = pl.BlockSpec((1, nseq, N_MEM, D_MODEL), lambda b, i: (layer, b, 0, 0))
    half = D_MODEL // 2
    sq = lambda: _const_spec((1, D_MODEL, D_MODEL), (layer, 0, 0))
    return pl.pallas_call(
        functools.partial(_mix_even_xattn_body, nseq=nseq),
        grid=(bsz, seq // tm),
        in_specs=[tok(D_MODEL), tok(half), tok(half),
                  _const_spec((1, half, D_MODEL), (layer_e, 0, 0)), _const_spec((1, half, D_MODEL), (layer_e, 1, 0)),
                  _const_spec((1, D_MODEL)), sq(), mem, mem, sq()],
        out_specs=tok(D_MODEL),
        out_shape=jax.ShapeDtypeStruct(x.shape, F32),
        compiler_params=_params(2),
        name="mix_even_xattn",
    )(x, ya, yb, w_out, w_out, g, wq, mem_k, mem_v, wo)


def _mix_odd_xattn_body(x_ref, gm_ref, win_ref, cw_ref, wout_ref, prev_ref, g_ref, wq_ref, mk_ref, mv_ref, wo_ref,
                        o_ref, st_ref, carry_ref, *, tm, nseq):
    @pl.when(pl.program_id(1) == 0)
    def _():
        carry_ref[...] = prev_ref[0]

    x = x_ref[0]
    proj = _dot(_rms(x, gm_ref[...]).astype(BF16), win_ref[0])
    gate_b, gate_c, z = proj[:, :D_MODEL], proj[:, D_MODEL:2 * D_MODEL], proj[:, 2 * D_MODEL:]
    cz = gate_c * z
    carries = [carry_ref[...]] if nseq == 1 else [prev_ref[b] for b in range(nseq)]
    z_conv = _conv3_seqs(cz, carries, cw_ref[0])
    carry_ref[...] = cz[tm - SUBLANES:]
    _store_last_rows(st_ref, slice(None), cz, nseq)
    x1 = x + _dot((gate_b * z_conv).astype(BF16), wout_ref[0])
    _xattn(x1, g_ref[...], wq_ref[0], mk_ref, mv_ref, wo_ref[0], nseq, o_ref)


def _mix_odd_xattn(x, gm, w_in, conv_w, w_out, prev, g, wq, mem_k, mem_v, wo, layer, layer_o, tm, nseq):
    bsz, seq, _ = x.shape
    tok = pl.BlockSpec((1, tm, D_MODEL), lambda b, i: (b, i, 0))
    mem = pl.BlockSpec((1, nseq, N_MEM, D_MODEL), lambda b, i: (layer, b, 0, 0))
    hist = pl.BlockSpec((nseq, SUBLANES, D_MODEL), lambda b, i: (b, 0, 0))
    st = pl.BlockSpec((nseq, CONV_WIDTH - 1, D_MODEL), lambda b, i: (b, 0, 0))
    sq = lambda l: _const_spec((1, D_MODEL, D_MODEL), (l, 0, 0))
    return pl.pallas_call(
        functools.partial(_mix_odd_xattn_body, tm=tm, nseq=nseq),
        grid=(bsz, seq // tm),
        in_specs=[tok, _const_spec((1, D_MODEL)), _const_spec((1, D_MODEL, 3 * D_MODEL), (layer_o, 0, 0)),
                  _const_spec((1, CONV_WIDTH, D_MODEL), (layer_o, 0, 0)), sq(layer_o), hist,
                  _const_spec((1, D_MODEL)), sq(layer), mem, mem, sq(layer)],
        out_specs=[tok, st],
        out_shape=[jax.ShapeDtypeStruct(x.shape, F32),
                   jax.ShapeDtypeStruct((bsz * nseq, CONV_WIDTH - 1, D_MODEL), F32)],
        scratch_shapes=[pltpu.VMEM((SUBLANES, D_MODEL), F32)],
        compiler_params=_params(2),
        name="mix_odd_xattn",
    )(x, gm, w_in, conv_w, w_out, prev, g, wq, mem_k, mem_v, wo)


def _ffn_body(x_ref, g_ref, wup_ref, cw_ref, wdn_ref, prev_ref, gfin_ref, o_ref, st_ref, carry_ref,
              *, tm, final, nseq):
    @pl.when(pl.program_id(1) == 0)
    def _():
        carry_ref[...] = prev_ref[0]

    x = x_ref[0]
    h = _rms(x, g_ref[...]).astype(BF16)
    chunks = [slice(lo, min(lo + FF_CHUNK, D_FF)) for lo in range(0, D_FF, FF_CHUNK)]

    def up(cols):
        return _dot(h, wup_ref[0, :, cols]), _dot(h, wup_ref[0, :, D_FF + cols.start:D_FF + cols.stop])

    acc = x
    nxt = up(chunks[0])
    for n, cols in enumerate(chunks):
        (gate, val), nxt = nxt, (up(chunks[n + 1]) if n + 1 < len(chunks) else None)
        carries = [carry_ref[:, cols]] if nseq == 1 else [prev_ref[b, :, cols] for b in range(nseq)]
        gate_c = _conv3_seqs(gate, carries, cw_ref[0, :, cols])
        carry_ref[:, cols] = gate[tm - SUBLANES:]
        _store_last_rows(st_ref, cols, gate, nseq)
        act = (gate_c * jax.nn.sigmoid(gate_c) * val).astype(BF16)
        if n + 1 < len(chunks):
            acc = acc + _dot(act, wdn_ref[0, cols, :])
        else:
            half = tm // 2
            for rows in (slice(0, half), slice(half, tm)):
                out = acc[rows] + _dot(act[rows], wdn_ref[0, cols, :])
                o_ref[0, rows, :] = _rms(out, gfin_ref[...]) if final else out


def _ffn(x, g, w_up, conv_w, w_down, prev, g_final, layer, tm, final, nseq):
    bsz, seq, _ = x.shape
    tok = pl.BlockSpec((1, tm, D_MODEL), lambda b, i: (b, i, 0))
    hist = pl.BlockSpec((nseq, SUBLANES, D_FF), lambda b, i: (b, 0, 0))
    st = pl.BlockSpec((nseq, CONV_WIDTH - 1, D_FF), lambda b, i: (b, 0, 0))
    return pl.pallas_call(
        functools.partial(_ffn_body, tm=tm, final=final, nseq=nseq),
        grid=(bsz, seq // tm),
        in_specs=[tok, _const_spec((1, D_MODEL)), _const_spec((1, D_MODEL, 2 * D_FF), (layer, 0, 0)),
                  _const_spec((1, CONV_WIDTH, D_FF), (layer, 0, 0)), _const_spec((1, D_FF, D_MODEL), (layer, 0, 0)),
                  hist, _const_spec((1, D_MODEL))],
        out_specs=[tok, st],
        out_shape=[jax.ShapeDtypeStruct(x.shape, F32),
                   jax.ShapeDtypeStruct((bsz * nseq, CONV_WIDTH - 1, D_FF), F32)],
        scratch_shapes=[pltpu.VMEM((SUBLANES, D_FF), F32)],
        compiler_params=_params(2),
        name="conv_ffn",
    )(x, g, w_up, conv_w, w_down, prev, g_final)


def _mem_flat_copies(k_hbm, v_hbm, kbuf, vbuf, sem, step, slot, nbatch):
    lyr, b = step // nbatch, step % nbatch
    return [pltpu.make_async_copy(src.at[lyr, b, :, hd, :], buf.at[slot, hd], sem.at[slot, i, hd])
            for i, (src, buf) in enumerate(((k_hbm, kbuf), (v_hbm, vbuf))) for hd in range(X_HEADS)]


def _mem_flat_body(k_hbm, v_hbm, ko_ref, vo_ref, kbuf, vbuf, sem, *, nbatch):
    s, n = pl.program_id(0), pl.num_programs(0)
    copies = functools.partial(_mem_flat_copies, k_hbm, v_hbm, kbuf, vbuf, sem, nbatch=nbatch)

    @pl.when(s == 0)
    def _():
        for c in copies(step=s, slot=0):
            c.start()

    @pl.when(s + 1 < n)
    def _():
        for c in copies(step=s + 1, slot=(s + 1) % 2):
            c.start()

    slot = s % 2
    for c in copies(step=s, slot=slot):
        c.wait()
    for hd in range(X_HEADS):
        cols = slice(hd * X_HEAD_DIM, (hd + 1) * X_HEAD_DIM)
        ko_ref[0, 0, :, cols] = kbuf[slot, hd].astype(BF16)
        vo_ref[0, 0, :, cols] = vbuf[slot, hd].astype(BF16)


def _mem_flat(mem_k, mem_v):
    depth, nbatch = mem_k.shape[:2]
    flat = pl.BlockSpec((1, 1, N_MEM, D_MODEL), lambda s: (s // nbatch, s % nbatch, 0, 0))
    return pl.pallas_call(
        functools.partial(_mem_flat_body, nbatch=nbatch),
        grid=(depth * nbatch,),
        in_specs=[pl.BlockSpec(memory_space=pl.ANY)] * 2,
        out_specs=[flat, flat],
        out_shape=[jax.ShapeDtypeStruct((depth, nbatch, N_MEM, D_MODEL), BF16)] * 2,
        scratch_shapes=[pltpu.VMEM((2, X_HEADS, N_MEM, X_HEAD_DIM), F32)] * 2
        + [pltpu.SemaphoreType.DMA((2, 2, X_HEADS))],
        compiler_params=_params(1),
        name="mem_flat",
    )(mem_k, mem_v)


def _mem_kv_body(m_ref, w_ref, k_ref, v_ref, kf_ref, vf_ref):
    kv = _dot(m_ref[0].astype(BF16), w_ref[0])
    kf_ref[0, 0] = kv[:, :D_MODEL].astype(BF16)
    vf_ref[0, 0] = kv[:, D_MODEL:].astype(BF16)
    for hd in range(X_HEADS):
        k_ref[0, 0, :, hd, :] = kv[:, hd * X_HEAD_DIM:(hd + 1) * X_HEAD_DIM]
        v_ref[0, 0, :, hd, :] = kv[:, D_MODEL + hd * X_HEAD_DIM:D_MODEL + (hd + 1) * X_HEAD_DIM]


def _mem_kv(mem, w_kv):
    bsz = mem.shape[0]
    heads = pl.BlockSpec((1, 1, N_MEM, X_HEADS, X_HEAD_DIM), lambda l, b: (l, b, 0, 0, 0))
    flat = pl.BlockSpec((1, 1, N_MEM, D_MODEL), lambda l, b: (l, b, 0, 0))
    return pl.pallas_call(
        _mem_kv_body,
        grid=(DEPTH, bsz),
        in_specs=[pl.BlockSpec((1, N_MEM, D_MODEL), lambda l, b: (b, 0, 0)),
                  pl.BlockSpec((1, D_MODEL, 2 * D_MODEL), lambda l, b: (l, 0, 0))],
        out_specs=[heads, heads, flat, flat],
        out_shape=[jax.ShapeDtypeStruct((DEPTH, bsz, N_MEM, X_HEADS, X_HEAD_DIM), F32)] * 2
        + [jax.ShapeDtypeStruct((DEPTH, bsz, N_MEM, D_MODEL), BF16)] * 2,
        compiler_params=_params(2),
        name="mem_kv",
    )(mem, w_kv)


def _trunk(x, start, past_k, past_v, ssm_re, ssm_im, conv_mix_prev, conv_ffn_prev, mem_k, mem_v, w, s5, cfg):
    bsz, seq, _ = x.shape
    tq, qg, nseq = cfg["tq"], cfg["qg"], cfg["nseq"]
    tm_in, tm_even, tm_odd, tm_ffn = cfg["tm_in"], cfg["tm_even"], cfg["tm_odd"], cfg["tm_ffn"]
    ni, chained = cfg["ni"], cfg["chained"]
    ngrp, grows = bsz // nseq, nseq * seq
    grouped = lambda t: t.reshape(ngrp, grows, t.shape[-1])
    per_seq = lambda t: t.reshape(bsz, seq, t.shape[-1])
    hist = lambda st: jnp.pad(st, ((0, 0), (SUBLANES - (CONV_WIDTH - 1), 0), (0, 0)))
    tables = _rope_tables(grows, seq, start, tm_in)
    u, q, k, v = _inproj_even(grouped(x), w["norm_mix"][0:1], w["w_in_even"], tables, 0, tm_in)
    h0 = jnp.concatenate([ssm_re[0].reshape(bsz, S5_STATES), ssm_im[0].reshape(bsz, S5_STATES)], axis=1)
    if chained:
        ya, h_all = _s5_mixer(u, h0[:, None, :], s5["wg"], s5["wy"], cfg["pow2"], cfg["powr"], w["ssm_d"],
                              w["w_glu"], 0, ni, True)
        h_last = h_all[:, 0]
    else:
        assert ngrp == 1 and bsz == SUBLANES and seq == ni * S5_T
        ya, h_all = _s5_mixer(u, h0[None], s5["wg"], s5["wy"], cfg["pow2"], cfg["powr"], w["ssm_d"],
                              w["w_glu"], 0, ni, False)
        h_last = h_all[0]
    new_re = h_last[:, :S5_STATES].reshape(1, bsz, SSM_GROUPS, SSM_STATE)
    new_im = h_last[:, S5_STATES:].reshape(1, bsz, SSM_GROUPS, SSM_STATE)
    k, v = per_seq(k), per_seq(v)
    yb = _swa(per_seq(q), k, v, past_k[0].reshape(bsz, WINDOW, KV_WIDTH), past_v[0].reshape(bsz, WINDOW, KV_WIDTH),
              w["attn_sinks"][0], start, tq, qg)
    xg = _mix_even_xattn(grouped(x), ya, grouped(yb), w["w_out_even"], w["norm_xattn"][0:1], w["xattn_wq"],
                         mem_k, mem_v, w["xattn_wo"], 0, 0, tm_even, nseq)
    xg, cf0 = _ffn(xg, w["norm_ffn"][0:1], w["ffn_w_up"], w["ffn_conv_w"], w["ffn_w_down"], hist(conv_ffn_prev[0]),
                   w["norm_final"], 0, tm_ffn, False, nseq)
    xg, cm = _mix_odd_xattn(xg, w["norm_mix"][1:2], w["w_in_odd"], w["conv_mix_w"], w["w_out_odd"],
                            hist(conv_mix_prev[0]), w["norm_xattn"][1:2], w["xattn_wq"], mem_k, mem_v,
                            w["xattn_wo"], 1, 0, tm_odd, nseq)
    y, cf1 = _ffn(xg, w["norm_ffn"][1:2], w["ffn_w_up"], w["ffn_conv_w"], w["ffn_w_down"], hist(conv_ffn_prev[1]),
                  w["norm_final"], 1, tm_ffn, True, nseq)
    keep = min(seq, WINDOW)
    new_k = k[:, seq - keep:].reshape(1, bsz, keep, N_KV_HEADS, HEAD_DIM)
    new_v = v[:, seq - keep:].reshape(1, bsz, keep, N_KV_HEADS, HEAD_DIM)
    return per_seq(y), new_k, new_v, new_re, new_im, cm[None], jnp.stack([cf0, cf1])


def kernel(x_prompt, x_sample, mem_prompt, cache_win_k, cache_win_v, state_ssm_re, state_ssm_im, state_conv_mix, state_conv_ffn, cache_mem_k, cache_mem_v, norm_mix, norm_xattn, norm_ffn, norm_final, w_in_even, w_out_even, ssm_a_re, ssm_a_im, ssm_log_dt, ssm_b_re, ssm_b_im, ssm_c_re, ssm_c_im, ssm_d, w_glu, attn_sinks, w_in_odd, conv_mix_w, w_out_odd, xattn_wq, xattn_wkv, xattn_wo, ffn_w_up, ffn_conv_w, ffn_w_down):
    bp, seq_p, _ = x_prompt.shape
    bs, seq_s, _ = x_sample.shape
    w = dict(norm_mix=norm_mix, norm_xattn=norm_xattn, norm_ffn=norm_ffn, norm_final=norm_final.reshape(1, D_MODEL),
             w_in_even=w_in_even.astype(BF16), w_out_even=w_out_even.astype(BF16),
             ssm_d=ssm_d[0].reshape(1, SSM_WIDTH), w_glu=w_glu.astype(BF16), attn_sinks=attn_sinks,
             w_in_odd=w_in_odd.astype(BF16), conv_mix_w=conv_mix_w, w_out_odd=w_out_odd.astype(BF16),
             xattn_wq=xattn_wq.astype(BF16), xattn_wo=xattn_wo.astype(BF16),
             ffn_w_up=ffn_w_up.astype(BF16), ffn_conv_w=ffn_conv_w, ffn_w_down=ffn_w_down.astype(BF16))

    wg, wy = _s5_prep(ssm_a_re[0], ssm_a_im[0], ssm_log_dt[0], ssm_b_re[0], ssm_b_im[0], ssm_c_re[0], ssm_c_im[0])
    s5 = dict(wg=wg, wy=wy)
    ni_p, ni_s = S5_SEG_CHUNKS, seq_s // S5_T
    padded = lambda e: e + [1.0] * (-len(e) % SUBLANES)
    exps = (padded([float(S5_T * ni_p << m) for m in range(3)])
            + padded([float(S5_T * (i + 1)) for i in range(ni_p)])
            + padded([float(S5_T * (i + 1)) for i in range(ni_s)]))
    pows = _s5_pow_table(exps, ssm_a_re[0], ssm_a_im[0], ssm_log_dt[0])
    o1 = SUBLANES
    o2 = o1 + len(padded([0.0] * ni_p))
    cfg_p = dict(tm_in=1024, tm_even=1024, tm_odd=1024, tm_ffn=1024, tq=512, qg=CHUNK, nseq=1, ni=ni_p, chained=True,
                 pow2=pows[:o1], powr=pows[o1:o2])
    rows_s = bs * seq_s
    cfg_s = dict(tm_in=rows_s, tm_even=rows_s, tm_odd=rows_s, tm_ffn=rows_s, tq=seq_s, qg=seq_s, nseq=bs, ni=ni_s,
                 chained=False, pow2=pows[:o1], powr=pows[o2:])

    mem_k_p, mem_v_p, mem_kf_p, mem_vf_p = _mem_kv(mem_prompt, xattn_wkv.astype(BF16))
    cw = CONV_WIDTH - 1
    zk = jnp.zeros((1, bp, WINDOW, N_KV_HEADS, HEAD_DIM), F32)
    zs = jnp.zeros((1, bp, SSM_GROUPS, SSM_STATE), F32)
    zcm = jnp.zeros((1, bp, cw, D_MODEL), F32)
    zcf = jnp.zeros((DEPTH, bp, cw, D_FF), F32)
    y_p, k_p, v_p, re_p, im_p, cm_p, cf_p = _trunk(x_prompt, 0, zk, zk, zs, zs, zcm, zcf, mem_kf_p, mem_vf_p, w, s5, cfg_p)
    mem_kf_s, mem_vf_s = _mem_flat(cache_mem_k, cache_mem_v)
    y_s, k_s, v_s, re_s, im_s, cm_s, cf_s = _trunk(x_sample, PAST_LEN, cache_win_k, cache_win_v, state_ssm_re,
                                                   state_ssm_im, state_conv_mix, state_conv_ffn, mem_kf_s,
                                                   mem_vf_s, w, s5, cfg_s)
    return (y_p, y_s, k_p, v_p, re_p, im_p, cm_p, cf_p, mem_k_p, mem_v_p, k_s, v_s, re_s, im_s, cm_s, cf_s)
```

```python
import functools
import math

import numpy as np
import jax
import jax.numpy as jnp
from jax import lax
from jax.experimental import pallas as pl
from jax.experimental.pallas import tpu as pltpu

F32 = jnp.float32
BF16 = jnp.bfloat16

D_MODEL = 1024
DEPTH = 2
PAST_LEN = 2048
CHUNK = 64
SSM_WIDTH = 512
SSM_GROUP = 16
SSM_GROUPS = 32
SSM_STATE = 64
HEAD_DIM = 64
N_HEADS = 8
N_KV_HEADS = 2
GQ = N_HEADS // N_KV_HEADS
WINDOW = 128
ROPE_THETA = 10000.0
KV_WIDTH = N_KV_HEADS * HEAD_DIM
EVEN_IN = SSM_WIDTH + N_HEADS * HEAD_DIM + 2 * KV_WIDTH
CONV_WIDTH = 3
N_MEM = 256
X_HEADS = 4
X_HEAD_DIM = 256
D_FF = 2816
EPS = 1e-6
NEG = -1e30

LANES = 128
SUBLANES = 8
S5_T = 8
S5_TILES = SSM_WIDTH // LANES
S5_TILE_STATES = (LANES // SSM_GROUP) * SSM_STATE
S5_STATES = SSM_GROUPS * SSM_STATE
SWA_STAGE_CHUNKS = 1
XATTN_PARTS = 2
S5_SEG_CHUNKS = 16
MXU_DIM = 256
FF_CHUNK = 6 * MXU_DIM
VMEM_LIMIT = 56 * 1024 * 1024


def _const_spec(shape, index=None):
    idx = tuple(index) if index is not None else (0,) * len(shape)
    return pl.BlockSpec(shape, lambda *_: idx, pipeline_mode=pl.Buffered(1))


def _params(n_axes):
    return pltpu.CompilerParams(dimension_semantics=("arbitrary",) * n_axes,
                                vmem_limit_bytes=VMEM_LIMIT)


def _rms(x, g):
    return x * lax.rsqrt(jnp.mean(x * x, axis=-1, keepdims=True) + EPS) * g


def _dot(a, b):
    return jnp.dot(a, b, preferred_element_type=F32)


def _dot_t(a, b):
    return lax.dot_general(a, b, (((1,), (1,)), ((), ())), preferred_element_type=F32)


def _rope(t, cos, sina, sinb):
    n = t.shape[1]
    half = HEAD_DIM // 2
    return t * cos + pltpu.roll(t, n - half, 1) * sina + pltpu.roll(t, half, 1) * sinb


def _inproj_even_body(x_ref, g_ref, w_ref, u_ref, q_ref, k_ref, v_ref, cphi_ref, sphi_ref, *, start, period):
    o0 = SSM_WIDTH
    o1 = o0 + N_HEADS * HEAD_DIM
    o2 = o1 + KV_WIDTH
    tm = x_ref.shape[1]
    half = HEAD_DIM // 2
    lane = lax.broadcasted_iota(jnp.int32, (1, LANES), 1)
    inv = jnp.exp((lane & (half - 1)).astype(F32) * (-2.0 / HEAD_DIM * math.log(ROPE_THETA)))
    first = (lane & (HEAD_DIM - 1)) < half

    @pl.when((pl.program_id(0) == 0) & (pl.program_id(1) == 0))
    def _():
        row = lax.broadcasted_iota(jnp.int32, (tm, LANES), 0)
        phi = (row & (period - 1)).astype(F32) * inv
        cphi_ref[...] = jnp.cos(phi)
        sphi_ref[...] = jnp.sin(phi)

    theta = (start + ((pl.program_id(1) * tm) & (period - 1))).astype(F32) * inv
    c_th, s_th = jnp.cos(theta), jnp.sin(theta)
    nparts = 2 if tm % (2 * LANES) == 0 else 1
    parts = [slice(p * tm // nparts, (p + 1) * tm // nparts) for p in range(nparts)]
    projs = [_dot(_rms(x_ref[0, r, :], g_ref[...]).astype(BF16), w_ref[0]) for r in parts]
    for r, proj in zip(parts, projs):
        c_ph, s_ph = cphi_ref[r, :], sphi_ref[r, :]
        cos = c_th * c_ph - s_th * s_ph
        sin = s_th * c_ph + c_th * s_ph
        sa = jnp.where(first, -sin, 0.0)
        sb = jnp.where(first, 0.0, sin)
        u_ref[0, r, :] = proj[:, :o0]
        q = [_rope(proj[:, c:c + LANES], cos, sa, sb) for c in range(o0, o1, LANES)]
        q_ref[0, r, :] = (jnp.concatenate(q, axis=1) * HEAD_DIM ** -0.5).astype(BF16)
        k_ref[0, r, :] = _rope(proj[:, o1:o2], cos, sa, sb)
        v_ref[0, r, :] = proj[:, o2:]


def _inproj_even(x, g, w_in, start, period, layer_e, tm):
    bsz, seq, _ = x.shape
    tok = lambda w: pl.BlockSpec((1, tm, w), lambda b, i: (b, i, 0))
    return pl.pallas_call(
        functools.partial(_inproj_even_body, start=start, period=period),
        grid=(bsz, seq // tm),
        in_specs=[tok(D_MODEL), _const_spec((1, D_MODEL)),
                  _const_spec((1, D_MODEL, EVEN_IN), (layer_e, 0, 0))],
        out_specs=[tok(SSM_WIDTH), tok(N_HEADS * HEAD_DIM), tok(KV_WIDTH), tok(KV_WIDTH)],
        out_shape=[jax.ShapeDtypeStruct((bsz, seq, SSM_WIDTH), F32),
                   jax.ShapeDtypeStruct((bsz, seq, N_HEADS * HEAD_DIM), BF16),
                   jax.ShapeDtypeStruct((bsz, seq, KV_WIDTH), F32),
                   jax.ShapeDtypeStruct((bsz, seq, KV_WIDTH), F32)],
        scratch_shapes=[pltpu.VMEM((tm, LANES), F32)] * 2,
        compiler_params=_params(2),
        name="inproj_even",
    )(x, g, w_in)


def _s5_prep_body(lr_ref, li_ref, ld_ref, bre_ref, bim_ref, cre_ref, cim_ref, wg_ref, wy_ref):
    ts = S5_TILE_STATES
    tw = S5_T * LANES

    def times(a_r, a_i):
        return lambda p: (p[0] * a_r - p[1] * a_i, p[0] * a_i + p[1] * a_r)

    lr, li = lr_ref[0], li_ref[0]
    dt = jnp.exp(ld_ref[0])
    mag = jnp.exp(lr * dt)
    ar, ai = mag * jnp.cos(li * dt), mag * jnp.sin(li * dt)
    mul_a = times(ar, ai)
    nrm = lr * lr + li * li
    cbr = ((ar - 1.0) * lr + ai * li) / nrm
    cbi = (ai * lr - (ar - 1.0) * li) / nrm
    b_re, b_im = bre_ref[0], bim_ref[0]
    bbr = cbr * b_re - cbi * b_im
    bbi = cbr * b_im + cbi * b_re
    p = (jnp.ones_like(ar), jnp.zeros_like(ar))
    for s in reversed(range(S5_T)):
        rows = slice(s * LANES, (s + 1) * LANES)
        wg_ref[0, rows, :ts] = (p[0] * bbr - p[1] * bbi).astype(BF16)
        wg_ref[0, rows, ts:] = (p[0] * bbi + p[1] * bbr).astype(BF16)
        for t in range(s):
            wy_ref[0, rows, t * LANES:(t + 1) * LANES] = jnp.zeros((LANES, LANES), BF16)
        p = mul_a(p)
    to_col = lambda row: jnp.transpose(jnp.broadcast_to(row, (LANES, ts)))
    mul_ac = times(to_col(ar), to_col(ai))
    c_re, c_im = cre_ref[0], cim_ref[0]
    bb = jnp.concatenate([bbr, bbi], axis=1).astype(BF16)
    pc = (jnp.ones_like(c_re), jnp.zeros_like(c_re))
    for k in range(S5_T + 1):
        blk = jnp.concatenate([pc[0] * c_re - pc[1] * c_im, -(pc[0] * c_im + pc[1] * c_re)], axis=0).astype(BF16)
        if k >= 1:
            wy_ref[0, tw:, (k - 1) * LANES:k * LANES] = blk
        if k < S5_T:
            lag = _dot(bb, blk).astype(BF16)
            for s in range(S5_T - k):
                t = s + k
                wy_ref[0, s * LANES:(s + 1) * LANES, t * LANES:(t + 1) * LANES] = lag
        pc = mul_ac(pc)


def _s5_prep(a_re, a_im, log_dt, b_re, b_im, c_re, c_im):
    gpt = LANES // SSM_GROUP
    eye = jnp.eye(gpt, dtype=bool)

    def rows(p):
        return p.reshape(S5_TILES, 1, S5_TILE_STATES)

    def bmat(b):
        bt = b.reshape(S5_TILES, gpt, SSM_STATE, SSM_GROUP).transpose(0, 1, 3, 2)
        full = jnp.where(eye[None, :, None, :, None], bt[:, :, :, None, :], 0.0)
        return full.reshape(S5_TILES, LANES, S5_TILE_STATES)

    def cmat(c):
        ct = c.reshape(S5_TILES, gpt, SSM_GROUP, SSM_STATE).transpose(0, 1, 3, 2)
        full = jnp.where(eye[None, :, None, :, None], ct[:, :, :, None, :], 0.0)
        return full.reshape(S5_TILES, S5_TILE_STATES, LANES)

    row_spec = pl.BlockSpec((1, 1, S5_TILE_STATES), lambda j: (j, 0, 0))
    b_spec = pl.BlockSpec((1, LANES, S5_TILE_STATES), lambda j: (j, 0, 0))
    c_spec = pl.BlockSpec((1, S5_TILE_STATES, LANES), lambda j: (j, 0, 0))
    tw = S5_T * LANES
    return pl.pallas_call(
        _s5_prep_body,
        grid=(S5_TILES,),
        in_specs=[row_spec] * 3 + [b_spec] * 2 + [c_spec] * 2,
        out_specs=[pl.BlockSpec((1, tw, 2 * S5_TILE_STATES), lambda j: (j, 0, 0)),
                   pl.BlockSpec((1, tw + 2 * S5_TILE_STATES, tw), lambda j: (j, 0, 0))],
        out_shape=[jax.ShapeDtypeStruct((S5_TILES, tw, 2 * S5_TILE_STATES), BF16),
                   jax.ShapeDtypeStruct((S5_TILES, tw + 2 * S5_TILE_STATES, tw), BF16)],
        compiler_params=_params(1),
        name="s5_prep",
    )(rows(a_re), rows(a_im), rows(log_dt), bmat(b_re), bmat(b_im), cmat(c_re), cmat(c_im))


def _s5_pow_body(ex_ref, lr_ref, li_ref, ld_ref, o_ref):
    dt = jnp.exp(ld_ref[...])
    zr, zi = lr_ref[...] * dt, li_ref[...] * dt
    ex = ex_ref[...]
    mag = jnp.exp(ex * zr)
    o_ref[:, :S5_STATES] = mag * jnp.cos(ex * zi)
    o_ref[:, S5_STATES:] = mag * jnp.sin(ex * zi)


def _s5_pow_table(exponents, a_re, a_im, log_dt):
    n = len(exponents)
    ex = jnp.asarray(np.asarray(exponents, np.float32)[:, None])
    flat = lambda p: p.reshape(1, S5_STATES)
    return pl.pallas_call(
        _s5_pow_body,
        grid=(1,),
        in_specs=[pl.BlockSpec((n, 1), lambda i: (0, 0))] + [pl.BlockSpec((1, S5_STATES), lambda i: (0, 0))] * 3,
        out_specs=pl.BlockSpec((n, 2 * S5_STATES), lambda i: (0, 0)),
        out_shape=jax.ShapeDtypeStruct((n, 2 * S5_STATES), F32),
        compiler_params=_params(1),
        name="s5_pow_table",
    )(ex, flat(a_re), flat(a_im), flat(log_dt))


def _s5_body(u_ref, h0_ref, wg_ref, wy_ref, pow2_ref, powr_ref, d_ref, wglu_ref,
             y_ref, hout_ref, carry_ref, us_ref, ys_ref, *, ni, chained):
    ns = S5_STATES
    ts = S5_TILE_STATES
    segtok = ni * S5_T
    pitch = _s5_pitch(ni)

    @pl.when(pl.program_id(1) == 0)
    def _():
        carry_ref[...] = h0_ref[0]

    for j in range(S5_TILES):
        for s in range(SUBLANES):
            us_ref[j, s * pitch:s * pitch + segtok] = u_ref[0, s * segtok:(s + 1) * segtok, j * LANES:(j + 1) * LANES]

    def gather(j, t):
        return jnp.concatenate([us_ref[j, pl.ds(S5_T * i + t, SUBLANES, stride=pitch), :]
                                for i in range(ni)], axis=0)

    def fma(p_r, p_i, x_r, x_i, y_r, y_i):
        return y_r + p_r * x_r - p_i * x_i, y_i + p_r * x_i + p_i * x_r

    xs, g_re, g_im = [], [], []
    for j in range(S5_TILES):
        xj = jnp.concatenate([gather(j, s) for s in range(S5_T)], axis=1).astype(BF16)
        g = _dot(xj, wg_ref[j])
        xs.append(xj)
        g_re.append(g[:, :ts])
        g_im.append(g[:, ts:])
    g_re = jnp.concatenate(g_re, axis=1)
    g_im = jnp.concatenate(g_im, axis=1)
    slab = lambda a, i: a[i * SUBLANES:(i + 1) * SUBLANES]
    power = lambda ref, r: (ref[r:r + 1, :ns], ref[r:r + 1, ns:])
    a_r, a_i = power(powr_ref, 0)
    cin = carry_ref[...]
    h_r, h_i = cin[:, :ns], cin[:, ns:]
    prev = []
    if chained:
        local = [(slab(g_re, 0), slab(g_im, 0))]
        for i in range(1, ni):
            local.append(fma(a_r, a_i, *local[-1], slab(g_re, i), slab(g_im, i)))
        sub = lax.broadcasted_iota(jnp.int32, (SUBLANES, 1), 0)
        c_r = jnp.where(sub == 0, h_r, pltpu.roll(local[-1][0], 1, 0))
        c_i = jnp.where(sub == 0, h_i, pltpu.roll(local[-1][1], 1, 0))
        for m in range(3):
            k = 1 << m
            s_r = jnp.where(sub >= k, pltpu.roll(c_r, k, 0), 0.0)
            s_i = jnp.where(sub >= k, pltpu.roll(c_i, k, 0), 0.0)
            c_r, c_i = fma(*power(pow2_ref, m), s_r, s_i, c_r, c_i)
        h_r, h_i = c_r, c_i
        for i in range(ni):
            prev.append((h_r, h_i))
            h_r, h_i = fma(*power(powr_ref, i), c_r, c_i, *local[i])
        last = jnp.concatenate([h_r[SUBLANES - 1:], h_i[SUBLANES - 1:]], axis=1)
    else:
        for i in range(ni):
            prev.append((h_r, h_i))
            h_r, h_i = fma(a_r, a_i, h_r, h_i, slab(g_re, i), slab(g_im, i))
        last = jnp.concatenate([h_r, h_i], axis=1)
    carry_ref[...] = last
    hout_ref[0] = last
    prev_re = jnp.concatenate([p[0] for p in prev], axis=0)
    prev_im = jnp.concatenate([p[1] for p in prev], axis=0)
    y_tiles = []
    for j in range(S5_TILES):
        lhs = jnp.concatenate([xs[j], prev_re[:, j * ts:(j + 1) * ts].astype(BF16),
                               prev_im[:, j * ts:(j + 1) * ts].astype(BF16)], axis=1)
        y_tiles.append(_dot(lhs, wy_ref[j]))
    d = d_ref[...]
    wglu = wglu_ref[0]
    def gelu_in(t):
        ut = jnp.concatenate([gather(j, t) for j in range(S5_TILES)], axis=1)
        yt = jnp.concatenate([y_tiles[j][:, t * LANES:(t + 1) * LANES] for j in range(S5_TILES)], axis=1)
        return jax.nn.gelu(yt + d * ut)

    def scatter(t, o):
        for j in range(S5_TILES):
            for i in range(ni):
                ys_ref[j, pl.ds(S5_T * i + t, SUBLANES, stride=pitch), :] = slab(o, i)[:, j * LANES:(j + 1) * LANES]

    g_q, z_q = {}, {}
    for step in range(S5_T + 2):
        if step < S5_T:
            g_q[step] = gelu_in(step)
        if 0 <= step - 1 < S5_T:
            z_q[step - 1] = _dot(g_q[step - 1].astype(BF16), wglu)
        if 0 <= step - 2 < S5_T:
            scatter(step - 2, g_q.pop(step - 2) * jax.nn.sigmoid(z_q.pop(step - 2)))
    for j in range(S5_TILES):
        for s in range(SUBLANES):
            y_ref[0, s * segtok:(s + 1) * segtok, j * LANES:(j + 1) * LANES] = (
                ys_ref[j, s * pitch:s * pitch + segtok].astype(BF16))


def _s5_pitch(ni):
    segtok = ni * S5_T
    return segtok + SUBLANES if (segtok // SUBLANES) % 2 == 0 else segtok


def _s5_mixer(u, h0, wg, wy, pow2, powr, d, w_glu, layer_e, ni, chained):
    nb, ntok, width = u.shape
    tb = SUBLANES * ni * S5_T
    hrows = h0.shape[1]
    blk = pl.BlockSpec((1, tb, width), lambda b, i: (b, i, 0))
    state = pl.BlockSpec((1, hrows, 2 * S5_STATES), lambda b, i: (b, 0, 0))
    return pl.pallas_call(
        functools.partial(_s5_body, ni=ni, chained=chained),
        grid=(nb, ntok // tb),
        in_specs=[blk, state, _const_spec(wg.shape), _const_spec(wy.shape),
                  _const_spec(pow2.shape), _const_spec(powr.shape), _const_spec((1, SSM_WIDTH)),
                  _const_spec((1, SSM_WIDTH, SSM_WIDTH), (layer_e, 0, 0))],
        out_specs=[blk, state],
        out_shape=[jax.ShapeDtypeStruct((nb, ntok, width), BF16),
                   jax.ShapeDtypeStruct((nb, hrows, 2 * S5_STATES), F32)],
        scratch_shapes=[pltpu.VMEM((hrows, 2 * S5_STATES), F32)]
        + [pltpu.VMEM((S5_TILES, SUBLANES * _s5_pitch(ni), LANES), F32)] * 2,
        compiler_params=_params(2),
        name="s5_mixer",
    )(u, h0, wg, wy, pow2, powr, d, w_glu)


def _swa_body(sink_ref, q_ref, kc_ref, vc_ref, kp_ref, vp_ref, pk_ref, pv_ref, o_ref, *, tq, qg, start):
    first = pl.program_id(1) == 0
    k_prev = jnp.where(first, pk_ref[0], kp_ref[0])
    v_prev = jnp.where(first, pv_ref[0], vp_ref[0])
    k_all = jnp.concatenate([k_prev, kc_ref[0]], axis=0)
    v_all = jnp.concatenate([v_prev, vc_ref[0]], axis=0)
    nk = WINDOW + qg
    lane = lax.broadcasted_iota(jnp.int32, (1, KV_WIDTH), 1)
    slot = lax.broadcasted_iota(jnp.int32, (1, nk), 1)
    qrow = lax.broadcasted_iota(jnp.int32, (2 * qg, 1), 0)
    lower = lane < HEAD_DIM
    placed = []
    for kh in range(N_KV_HEADS):
        own = lower if kh == 0 else jnp.logical_not(lower)
        kz, vz = jnp.where(own, k_all, 0.0), jnp.where(own, v_all, 1.0)
        kr, vr = pltpu.roll(kz, HEAD_DIM, 1), pltpu.roll(vz, HEAD_DIM, 1)
        k_lo, k_hi = (kz, kr) if kh == 0 else (kr, kz)
        v_lo, v_hi = (vz, vr) if kh == 0 else (vr, vz)
        placed.append(((k_lo.astype(BF16), v_lo.astype(BF16)), (k_hi.astype(BF16), v_hi.astype(BF16))))
    combos = [(kh, par) for kh in range(N_KV_HEADS) for par in range(2)]

    def scores(c):
        qc = q_ref[0, c * qg:(c + 1) * qg, :]
        out = []
        for kh, par in combos:
            base = kh * GQ * HEAD_DIM
            qs = jnp.concatenate([qc[:, base:base + LANES], qc[:, base + LANES:base + 2 * LANES]], axis=0)
            out.append(_dot_t(qs, placed[kh][par][0][c * qg:c * qg + nk]))
        return out

    def softmax(c, ss):
        n_bad = jnp.where(first, WINDOW - start - c * qg, 0) if c * qg < WINDOW - start else None
        probs, sinks = [], []
        for (kh, par), s in zip(combos, ss):
            if n_bad is not None:
                s = jnp.where(slot >= n_bad, s, NEG)
            sink = jnp.where(qrow < qg, sink_ref[kh * GQ + par], sink_ref[kh * GQ + 2 + par])
            m = jnp.maximum(jnp.max(s, axis=1, keepdims=True), sink)
            probs.append(jnp.exp(s - m).astype(BF16))
            sinks.append(jnp.exp(sink - m))
        return probs, sinks

    def finish(c, probs, sinks):
        pvs = [_dot(p, placed[kh][par][1][c * qg:c * qg + nk]) for (kh, par), p in zip(combos, probs)]
        outs = []
        for kh in range(N_KV_HEADS):
            pv_e, pv_o = pvs[2 * kh], pvs[2 * kh + 1]
            num = jnp.where(lower, pv_e, pv_o)
            den = pltpu.roll(jnp.where(lower, pv_o, pv_e), HEAD_DIM, 1) + jnp.where(lower, sinks[2 * kh], sinks[2 * kh + 1])
            acc = num / den
            outs += [acc[:qg], acc[qg:]]
        o_ref[0, c * qg:(c + 1) * qg, :] = jnp.concatenate(outs, axis=1).astype(BF16)

    nchunk = tq // qg
    per = min(SWA_STAGE_CHUNKS, nchunk)
    groups = [range(g, min(g + per, nchunk)) for g in range(0, nchunk, per)]
    s_q, p_q = {}, {}
    for step in range(len(groups) + 2):
        if step < len(groups):
            s_q[step] = [scores(c) for c in groups[step]]
        if 0 <= step - 1 < len(groups):
            p_q[step - 1] = [softmax(c, ss) for c, ss in zip(groups[step - 1], s_q.pop(step - 1))]
        if 0 <= step - 2 < len(groups):
            for c, (probs, sinks) in zip(groups[step - 2], p_q.pop(step - 2)):
                finish(c, probs, sinks)


def _swa(q, k, v, past_k, past_v, sinks, start, tq, qg):
    bsz, seq, _ = q.shape
    ntiles = seq // tq
    tok = lambda w: pl.BlockSpec((1, tq, w), lambda b, i: (b, i, 0))
    past = pl.BlockSpec((1, WINDOW, KV_WIDTH), lambda b, i: (b, 0, 0))
    if ntiles > 1:
        per = tq // WINDOW
        prev = pl.BlockSpec((1, WINDOW, KV_WIDTH), lambda b, i: (b, jnp.maximum(i * per - 1, 0), 0))
        k_prev, v_prev = k, v
    else:
        prev, k_prev, v_prev = past, past_k, past_v
    return pl.pallas_call(
        functools.partial(_swa_body, tq=tq, qg=qg, start=start),
        grid=(bsz, ntiles),
        in_specs=[pl.BlockSpec(memory_space=pltpu.SMEM), tok(N_HEADS * HEAD_DIM), tok(KV_WIDTH), tok(KV_WIDTH),
                  prev, prev, past, past],
        out_specs=tok(N_HEADS * HEAD_DIM),
        out_shape=jax.ShapeDtypeStruct((bsz, seq, N_HEADS * HEAD_DIM), BF16),
        compiler_params=_params(2),
        name="swa",
    )(sinks, q, k, v, k_prev, v_prev, past_k, past_v)


def _xattn(x1, g, wq, mk_ref, mv_ref, wo, nseq, o_ref):
    h = _rms(x1, g).astype(BF16)
    q = (_dot(h, wq) * X_HEAD_DIM ** -0.5).astype(BF16)
    nparts = nseq if nseq > 1 else XATTN_PARTS
    prows = x1.shape[0] // nparts
    heads = [slice(hd * X_HEAD_DIM, (hd + 1) * X_HEAD_DIM) for hd in range(X_HEADS)]

    def scores(p):
        r, b = slice(p * prows, (p + 1) * prows), (p if nseq > 1 else 0)
        return [_dot_t(q[r, c], mk_ref[0, b, :, c]) for c in heads]

    def softmax(ss):
        out = []
        for s in ss:
            e = jnp.exp(s - jnp.max(s, axis=1, keepdims=True))
            out.append((e.astype(BF16), jnp.sum(e, axis=1, keepdims=True)))
        return out

    def values(p, pd):
        b = p if nseq > 1 else 0
        return jnp.concatenate([_dot(e, mv_ref[0, b, :, c]) / den for c, (e, den) in zip(heads, pd)],
                               axis=1).astype(BF16)

    def finish(p, pd):
        r = slice(p * prows, (p + 1) * prows)
        o_ref[0, r, :] = x1[r] + _dot(values(p, pd), wo)

    if nseq > 1:
        pds = [softmax(ss) for ss in [scores(p) for p in range(nparts)]]
        o = jnp.concatenate([values(p, pd) for p, pd in enumerate(pds)], axis=0)
        o_ref[0] = x1 + _dot(o, wo)
        return
    s_q, p_q = {}, {}
    for step in range(nparts + 2):
        if step < nparts:
            s_q[step] = scores(step)
        if 0 <= step - 1 < nparts:
            p_q[step - 1] = softmax(s_q.pop(step - 1))
        if 0 <= step - 2 < nparts:
            finish(step - 2, p_q.pop(step - 2))


def _conv3(cur, carry, w):
    n = cur.shape[0]
    row = lax.broadcasted_iota(jnp.int32, (n, 1), 0)
    c1, c2 = carry[SUBLANES - 1:SUBLANES], carry[SUBLANES - 2:SUBLANES - 1]
    m1 = jnp.where(row == 0, c1, pltpu.roll(cur, 1, 0))
    m2 = jnp.where(row == 0, c2, jnp.where(row == 1, c1, pltpu.roll(cur, 2, 0)))
    return w[0:1] * m2 + w[1:2] * m1 + w[2:3] * cur


def _conv3_seqs(cur, carries, w):
    nseq = len(carries)
    if nseq == 1:
        return _conv3(cur, carries[0], w)
    rows = cur.shape[0] // nseq
    body = _conv3(cur, carries[0], w)
    pieces = []
    for b in range(nseq):
        pieces.append(_conv3(cur[b * rows:b * rows + SUBLANES], carries[b], w))
        pieces.append(body[b * rows + SUBLANES:(b + 1) * rows])
    return jnp.concatenate(pieces, axis=0)


def _store_last_rows(st_ref, cols, cur, nseq):
    rows = cur.shape[0] // nseq
    for b in range(nseq):
        st_ref[b, :, cols] = cur[(b + 1) * rows - (CONV_WIDTH - 1):(b + 1) * rows]


def _mix_even_xattn_body(x_ref, ya_ref, yb_ref, woa_ref, wob_ref, g_ref, wq_ref, mk_ref, mv_ref, wo_ref, o_ref,
                         *, nseq):
    x1 = x_ref[0] + _dot(ya_ref[0], woa_ref[0]) + _dot(yb_ref[0], wob_ref[0])
    _xattn(x1, g_ref[...], wq_ref[0], mk_ref, mv_ref, wo_ref[0], nseq, o_ref)


def _mix_even_xattn(x, ya, yb, w_out, g, wq, mem_k, mem_v, wo, layer, layer_e, tm, nseq):
    bsz, seq, _ = x.shape
    tok = lambda w: pl.BlockSpec((1, tm, w), lambda b, i: (b, i, 0))
    mem = pl.BlockSpec((1, nseq, N_MEM, D_MODEL), lambda b, i: (layer, b, 0, 0))
    half = D_MODEL // 2
    sq = lambda: _const_spec((1, D_MODEL, D_MODEL), (layer, 0, 0))
    return pl.pallas_call(
        functools.partial(_mix_even_xattn_body, nseq=nseq),
        grid=(bsz, seq // tm),
        in_specs=[tok(D_MODEL), tok(half), tok(half),
                  _const_spec((1, half, D_MODEL), (layer_e, 0, 0)), _const_spec((1, half, D_MODEL), (layer_e, 1, 0)),
                  _const_spec((1, D_MODEL)), sq(), mem, mem, sq()],
        out_specs=tok(D_MODEL),
        out_shape=jax.ShapeDtypeStruct(x.shape, F32),
        compiler_params=_params(2),
        name="mix_even_xattn",
    )(x, ya, yb, w_out, w_out, g, wq, mem_k, mem_v, wo)


def _mix_odd_xattn_body(x_ref, gm_ref, win_ref, cw_ref, wout_ref, prev_ref, g_ref, wq_ref, mk_ref, mv_ref, wo_ref,
                        o_ref, st_ref, carry_ref, *, tm, nseq):
    @pl.when(pl.program_id(1) == 0)
    def _():
        carry_ref[...] = prev_ref[0]

    x = x_ref[0]
    proj = _dot(_rms(x, gm_ref[...]).astype(BF16), win_ref[0])
    gate_b, gate_c, z = proj[:, :D_MODEL], proj[:, D_MODEL:2 * D_MODEL], proj[:, 2 * D_MODEL:]
    cz = gate_c * z
    carries = [carry_ref[...]] if nseq == 1 else [prev_ref[b] for b in range(nseq)]
    z_conv = _conv3_seqs(cz, carries, cw_ref[0])
    carry_ref[...] = cz[tm - SUBLANES:]
    _store_last_rows(st_ref, slice(None), cz, nseq)
    x1 = x + _dot((gate_b * z_conv).astype(BF16), wout_ref[0])
    _xattn(x1, g_ref[...], wq_ref[0], mk_ref, mv_ref, wo_ref[0], nseq, o_ref)


def _mix_odd_xattn(x, gm, w_in, conv_w, w_out, prev, g, wq, mem_k, mem_v, wo, layer, layer_o, tm, nseq):
    bsz, seq, _ = x.shape
    tok = pl.BlockSpec((1, tm, D_MODEL), lambda b, i: (b, i, 0))
    mem = pl.BlockSpec((1, nseq, N_MEM, D_MODEL), lambda b, i: (layer, b, 0, 0))
    hist = pl.BlockSpec((nseq, SUBLANES, D_MODEL), lambda b, i: (b, 0, 0))
    st = pl.BlockSpec((nseq, CONV_WIDTH - 1, D_MODEL), lambda b, i: (b, 0, 0))
    sq = lambda l: _const_spec((1, D_MODEL, D_MODEL), (l, 0, 0))
    return pl.pallas_call(
        functools.partial(_mix_odd_xattn_body, tm=tm, nseq=nseq),
        grid=(bsz, seq // tm),
        in_specs=[tok, _const_spec((1, D_MODEL)), _const_spec((1, D_MODEL, 3 * D_MODEL), (layer_o, 0, 0)),
                  _const_spec((1, CONV_WIDTH, D_MODEL), (layer_o, 0, 0)), sq(layer_o), hist,
                  _const_spec((1, D_MODEL)), sq(layer), mem, mem, sq(layer)],
        out_specs=[tok, st],
        out_shape=[jax.ShapeDtypeStruct(x.shape, F32),
                   jax.ShapeDtypeStruct((bsz * nseq, CONV_WIDTH - 1, D_MODEL), F32)],
        scratch_shapes=[pltpu.VMEM((SUBLANES, D_MODEL), F32)],
        compiler_params=_params(2),
        name="mix_odd_xattn",
    )(x, gm, w_in, conv_w, w_out, prev, g, wq, mem_k, mem_v, wo)


def _ffn_body(x_ref, g_ref, wup_ref, cw_ref, wdn_ref, prev_ref, gfin_ref, o_ref, st_ref, carry_ref,
              *, tm, final, nseq):
    @pl.when(pl.program_id(1) == 0)
    def _():
        carry_ref[...] = prev_ref[0]

    x = x_ref[0]
    h = _rms(x, g_ref[...]).astype(BF16)
    chunks = [slice(lo, min(lo + FF_CHUNK, D_FF)) for lo in range(0, D_FF, FF_CHUNK)]

    def up(cols):
        return _dot(h, wup_ref[0, :, cols]), _dot(h, wup_ref[0, :, D_FF + cols.start:D_FF + cols.stop])

    acc = x
    nxt = up(chunks[0])
    for n, cols in enumerate(chunks):
        (gate, val), nxt = nxt, (up(chunks[n + 1]) if n + 1 < len(chunks) else None)
        carries = [carry_ref[:, cols]] if nseq == 1 else [prev_ref[b, :, cols] for b in range(nseq)]
        gate_c = _conv3_seqs(gate, carries, cw_ref[0, :, cols])
        carry_ref[:, cols] = gate[tm - SUBLANES:]
        _store_last_rows(st_ref, cols, gate, nseq)
        act = (gate_c * jax.nn.sigmoid(gate_c) * val).astype(BF16)
        if n + 1 < len(chunks):
            acc = acc + _dot(act, wdn_ref[0, cols, :])
        else:
            half = tm // 2
            for rows in (slice(0, half), slice(half, tm)):
                out = acc[rows] + _dot(act[rows], wdn_ref[0, cols, :])
                o_ref[0, rows, :] = _rms(out, gfin_ref[...]) if final else out


def _ffn(x, g, w_up, conv_w, w_down, prev, g_final, layer, tm, final, nseq):
    bsz, seq, _ = x.shape
    tok = pl.BlockSpec((1, tm, D_MODEL), lambda b, i: (b, i, 0))
    hist = pl.BlockSpec((nseq, SUBLANES, D_FF), lambda b, i: (b, 0, 0))
    st = pl.BlockSpec((nseq, CONV_WIDTH - 1, D_FF), lambda b, i: (b, 0, 0))
    return pl.pallas_call(
        functools.partial(_ffn_body, tm=tm, final=final, nseq=nseq),
        grid=(bsz, seq // tm),
        in_specs=[tok, _const_spec((1, D_MODEL)), _const_spec((1, D_MODEL, 2 * D_FF), (layer, 0, 0)),
                  _const_spec((1, CONV_WIDTH, D_FF), (layer, 0, 0)), _const_spec((1, D_FF, D_MODEL), (layer, 0, 0)),
                  hist, _const_spec((1, D_MODEL))],
        out_specs=[tok, st],
        out_shape=[jax.ShapeDtypeStruct(x.shape, F32),
                   jax.ShapeDtypeStruct((bsz * nseq, CONV_WIDTH - 1, D_FF), F32)],
        scratch_shapes=[pltpu.VMEM((SUBLANES, D_FF), F32)],
        compiler_params=_params(2),
        name="conv_ffn",
    )(x, g, w_up, conv_w, w_down, prev, g_final)


def _mem_flat_copies(k_hbm, v_hbm, kbuf, vbuf, sem, step, slot, nbatch):
    lyr, b = step // nbatch, step % nbatch
    return [pltpu.make_async_copy(src.at[lyr, b, :, hd, :], buf.at[slot, hd], sem.at[slot, i, hd])
            for i, (src, buf) in enumerate(((k_hbm, kbuf), (v_hbm, vbuf))) for hd in range(X_HEADS)]


def _mem_flat_body(k_hbm, v_hbm, ko_ref, vo_ref, kbuf, vbuf, sem, *, nbatch):
    s, n = pl.program_id(0), pl.num_programs(0)
    copies = functools.partial(_mem_flat_copies, k_hbm, v_hbm, kbuf, vbuf, sem, nbatch=nbatch)

    @pl.when(s == 0)
    def _():
        for c in copies(step=s, slot=0):
            c.start()

    @pl.when(s + 1 < n)
    def _():
        for c in copies(step=s + 1, slot=(s + 1) % 2):
            c.start()

    slot = s % 2
    for c in copies(step=s, slot=slot):
        c.wait()
    for hd in range(X_HEADS):
        cols = slice(hd * X_HEAD_DIM, (hd + 1) * X_HEAD_DIM)
        ko_ref[0, 0, :, cols] = kbuf[slot, hd].astype(BF16)
        vo_ref[0, 0, :, cols] = vbuf[slot, hd].astype(BF16)


def _mem_flat(mem_k, mem_v):
    depth, nbatch = mem_k.shape[:2]
    flat = pl.BlockSpec((1, 1, N_MEM, D_MODEL), lambda s: (s // nbatch, s % nbatch, 0, 0))
    return pl.pallas_call(
        functools.partial(_mem_flat_body, nbatch=nbatch),
        grid=(depth * nbatch,),
        in_specs=[pl.BlockSpec(memory_space=pl.ANY)] * 2,
        out_specs=[flat, flat],
        out_shape=[jax.ShapeDtypeStruct((depth, nbatch, N_MEM, D_MODEL), BF16)] * 2,
        scratch_shapes=[pltpu.VMEM((2, X_HEADS, N_MEM, X_HEAD_DIM), F32)] * 2
        + [pltpu.SemaphoreType.DMA((2, 2, X_HEADS))],
        compiler_params=_params(1),
        name="mem_flat",
    )(mem_k, mem_v)


def _mem_kv_body(m_ref, w_ref, k_ref, v_ref, kf_ref, vf_ref):
    kv = _dot(m_ref[0].astype(BF16), w_ref[0])
    kf_ref[0, 0] = kv[:, :D_MODEL].astype(BF16)
    vf_ref[0, 0] = kv[:, D_MODEL:].astype(BF16)
    for hd in range(X_HEADS):
        k_ref[0, 0, :, hd, :] = kv[:, hd * X_HEAD_DIM:(hd + 1) * X_HEAD_DIM]
        v_ref[0, 0, :, hd, :] = kv[:, D_MODEL + hd * X_HEAD_DIM:D_MODEL + (hd + 1) * X_HEAD_DIM]


def _mem_kv(mem, w_kv):
    bsz = mem.shape[0]
    heads = pl.BlockSpec((1, 1, N_MEM, X_HEADS, X_HEAD_DIM), lambda l, b: (l, b, 0, 0, 0))
    flat = pl.BlockSpec((1, 1, N_MEM, D_MODEL), lambda l, b: (l, b, 0, 0))
    return pl.pallas_call(
        _mem_kv_body,
        grid=(DEPTH, bsz),
        in_specs=[pl.BlockSpec((1, N_MEM, D_MODEL), lambda l, b: (b, 0, 0)),
                  pl.BlockSpec((1, D_MODEL, 2 * D_MODEL), lambda l, b: (l, 0, 0))],
        out_specs=[heads, heads, flat, flat],
        out_shape=[jax.ShapeDtypeStruct((DEPTH, bsz, N_MEM, X_HEADS, X_HEAD_DIM), F32)] * 2
        + [jax.ShapeDtypeStruct((DEPTH, bsz, N_MEM, D_MODEL), BF16)] * 2,
        compiler_params=_params(2),
        name="mem_kv",
    )(mem, w_kv)


def _trunk(x, start, past_k, past_v, ssm_re, ssm_im, conv_mix_prev, conv_ffn_prev, mem_k, mem_v, w, s5, cfg):
    bsz, seq, _ = x.shape
    tq, qg, nseq = cfg["tq"], cfg["qg"], cfg["nseq"]
    tm_in, tm_even, tm_odd, tm_ffn = cfg["tm_in"], cfg["tm_even"], cfg["tm_odd"], cfg["tm_ffn"]
    ni, chained = cfg["ni"], cfg["chained"]
    ngrp, grows = bsz // nseq, nseq * seq
    grouped = lambda t: t.reshape(ngrp, grows, t.shape[-1])
    per_seq = lambda t: t.reshape(bsz, seq, t.shape[-1])
    hist = lambda st: jnp.pad(st, ((0, 0), (SUBLANES - (CONV_WIDTH - 1), 0), (0, 0)))
    u, q, k, v = _inproj_even(grouped(x), w["norm_mix"][0:1], w["w_in_even"], start, seq, 0, tm_in)
    h0 = jnp.concatenate([ssm_re[0].reshape(bsz, S5_STATES), ssm_im[0].reshape(bsz, S5_STATES)], axis=1)
    if chained:
        ya, h_all = _s5_mixer(u, h0[:, None, :], s5["wg"], s5["wy"], cfg["pow2"], cfg["powr"], w["ssm_d"],
                              w["w_glu"], 0, ni, True)
        h_last = h_all[:, 0]
    else:
        assert ngrp == 1 and bsz == SUBLANES and seq == ni * S5_T
        ya, h_all = _s5_mixer(u, h0[None], s5["wg"], s5["wy"], cfg["pow2"], cfg["powr"], w["ssm_d"],
                              w["w_glu"], 0, ni, False)
        h_last = h_all[0]
    new_re = h_last[:, :S5_STATES].reshape(1, bsz, SSM_GROUPS, SSM_STATE)
    new_im = h_last[:, S5_STATES:].reshape(1, bsz, SSM_GROUPS, SSM_STATE)
    k, v = per_seq(k), per_seq(v)
    yb = _swa(per_seq(q), k, v, past_k[0].reshape(bsz, WINDOW, KV_WIDTH), past_v[0].reshape(bsz, WINDOW, KV_WIDTH),
              w["attn_sinks"][0], start, tq, qg)
    xg = _mix_even_xattn(grouped(x), ya, grouped(yb), w["w_out_even"], w["norm_xattn"][0:1], w["xattn_wq"],
                         mem_k, mem_v, w["xattn_wo"], 0, 0, tm_even, nseq)
    xg, cf0 = _ffn(xg, w["norm_ffn"][0:1], w["ffn_w_up"], w["ffn_conv_w"], w["ffn_w_down"], hist(conv_ffn_prev[0]),
                   w["norm_final"], 0, tm_ffn, False, nseq)
    xg, cm = _mix_odd_xattn(xg, w["norm_mix"][1:2], w["w_in_odd"], w["conv_mix_w"], w["w_out_odd"],
                            hist(conv_mix_prev[0]), w["norm_xattn"][1:2], w["xattn_wq"], mem_k, mem_v,
                            w["xattn_wo"], 1, 0, tm_odd, nseq)
    y, cf1 = _ffn(xg, w["norm_ffn"][1:2], w["ffn_w_up"], w["ffn_conv_w"], w["ffn_w_down"], hist(conv_ffn_prev[1]),
                  w["norm_final"], 1, tm_ffn, True, nseq)
    keep = min(seq, WINDOW)
    new_k = k[:, seq - keep:].reshape(1, bsz, keep, N_KV_HEADS, HEAD_DIM)
    new_v = v[:, seq - keep:].reshape(1, bsz, keep, N_KV_HEADS, HEAD_DIM)
    return per_seq(y), new_k, new_v, new_re, new_im, cm[None], jnp.stack([cf0, cf1])


def kernel(x_prompt, x_sample, mem_prompt, cache_win_k, cache_win_v, state_ssm_re, state_ssm_im, state_conv_mix, state_conv_ffn, cache_mem_k, cache_mem_v, norm_mix, norm_xattn, norm_ffn, norm_final, w_in_even, w_out_even, ssm_a_re, ssm_a_im, ssm_log_dt, ssm_b_re, ssm_b_im, ssm_c_re, ssm_c_im, ssm_d, w_glu, attn_sinks, w_in_odd, conv_mix_w, w_out_odd, xattn_wq, xattn_wkv, xattn_wo, ffn_w_up, ffn_conv_w, ffn_w_down):
    bp, seq_p, _ = x_prompt.shape
    bs, seq_s, _ = x_sample.shape
    w = dict(norm_mix=norm_mix, norm_xattn=norm_xattn, norm_ffn=norm_ffn, norm_final=norm_final.reshape(1, D_MODEL),
             w_in_even=w_in_even.astype(BF16), w_out_even=w_out_even.astype(BF16),
             ssm_d=ssm_d[0].reshape(1, SSM_WIDTH), w_glu=w_glu.astype(BF16), attn_sinks=attn_sinks,
             w_in_odd=w_in_odd.astype(BF16), conv_mix_w=conv_mix_w, w_out_odd=w_out_odd.astype(BF16),
             xattn_wq=xattn_wq.astype(BF16), xattn_wo=xattn_wo.astype(BF16),
             ffn_w_up=ffn_w_up.astype(BF16), ffn_conv_w=ffn_conv_w, ffn_w_down=ffn_w_down.astype(BF16))

    wg, wy = _s5_prep(ssm_a_re[0], ssm_a_im[0], ssm_log_dt[0], ssm_b_re[0], ssm_b_im[0], ssm_c_re[0], ssm_c_im[0])
    s5 = dict(wg=wg, wy=wy)
    ni_p, ni_s = S5_SEG_CHUNKS, seq_s // S5_T
    padded = lambda e: e + [1.0] * (-len(e) % SUBLANES)
    exps = (padded([float(S5_T * ni_p << m) for m in range(3)])
            + padded([float(S5_T * (i + 1)) for i in range(ni_p)])
            + padded([float(S5_T * (i + 1)) for i in range(ni_s)]))
    pows = _s5_pow_table(exps, ssm_a_re[0], ssm_a_im[0], ssm_log_dt[0])
    o1 = SUBLANES
    o2 = o1 + len(padded([0.0] * ni_p))
    cfg_p = dict(tm_in=1024, tm_even=1024, tm_odd=1024, tm_ffn=1024, tq=512, qg=CHUNK, nseq=1, ni=ni_p, chained=True,
                 pow2=pows[:o1], powr=pows[o1:o2])
    rows_s = bs * seq_s
    cfg_s = dict(tm_in=rows_s, tm_even=rows_s, tm_odd=rows_s, tm_ffn=rows_s, tq=seq_s, qg=seq_s, nseq=bs, ni=ni_s,
                 chained=False, pow2=pows[:o1], powr=pows[o2:])

    mem_k_p, mem_v_p, mem_kf_p, mem_vf_p = _mem_kv(mem_prompt, xattn_wkv.astype(BF16))
    cw = CONV_WIDTH - 1
    zk = jnp.zeros((1, bp, WINDOW, N_KV_HEADS, HEAD_DIM), F32)
    zs = jnp.zeros((1, bp, SSM_GROUPS, SSM_STATE), F32)
    zcm = jnp.zeros((1, bp, cw, D_MODEL), F32)
    zcf = jnp.zeros((DEPTH, bp, cw, D_FF), F32)
    y_p, k_p, v_p, re_p, im_p, cm_p, cf_p = _trunk(x_prompt, 0, zk, zk, zs, zs, zcm, zcf, mem_kf_p, mem_vf_p, w, s5, cfg_p)
    mem_kf_s, mem_vf_s = _mem_flat(cache_mem_k, cache_mem_v)
    y_s, k_s, v_s, re_s, im_s, cm_s, cf_s = _trunk(x_sample, PAST_LEN, cache_win_k, cache_win_v, state_ssm_re,
                                                   state_ssm_im, state_conv_mix, state_conv_ffn, mem_kf_s,
                                                   mem_vf_s, w, s5, cfg_s)
    return (y_p, y_s, k_p, v_p, re_p, im_p, cm_p, cf_p, mem_k_p, mem_v_p, k_s, v_s, re_s, im_s, cm_s, cf_s)
```

```python
import functools
import math

import numpy as np
import jax
import jax.numpy as jnp
from jax import lax
from jax.experimental import pallas as pl
from jax.experimental.pallas import tpu as pltpu

F32 = jnp.float32
BF16 = jnp.bfloat16

D_MODEL = 1024
DEPTH = 2
PAST_LEN = 2048
CHUNK = 64
SSM_WIDTH = 512
SSM_GROUP = 16
SSM_GROUPS = 32
SSM_STATE = 64
HEAD_DIM = 64
N_HEADS = 8
N_KV_HEADS = 2
GQ = N_HEADS // N_KV_HEADS
WINDOW = 128
ROPE_THETA = 10000.0
KV_WIDTH = N_KV_HEADS * HEAD_DIM
EVEN_IN = SSM_WIDTH + N_HEADS * HEAD_DIM + 2 * KV_WIDTH
CONV_WIDTH = 3
N_MEM = 256
X_HEADS = 4
X_HEAD_DIM = 256
D_FF = 2816
EPS = 1e-6
NEG = -1e30

LANES = 128
SUBLANES = 8
S5_T = 8
S5_TILES = SSM_WIDTH // LANES
S5_TILE_STATES = (LANES // SSM_GROUP) * SSM_STATE
S5_STATES = SSM_GROUPS * SSM_STATE
SWA_STAGE_CHUNKS = 1
XATTN_PARTS = 2
S5_SEG_CHUNKS = 16
MXU_DIM = 256
FF_CHUNK = 6 * MXU_DIM
VMEM_LIMIT = 56 * 1024 * 1024


def _const_spec(shape, index=None):
    idx = tuple(index) if index is not None else (0,) * len(shape)
    return pl.BlockSpec(shape, lambda *_: idx, pipeline_mode=pl.Buffered(1))


def _gain_spec(layer):
    return _const_spec((1, 1, D_MODEL), (layer, 0, 0))


def _params(n_axes):
    return pltpu.CompilerParams(dimension_semantics=("arbitrary",) * n_axes,
                                vmem_limit_bytes=VMEM_LIMIT)


def _rms(x, g):
    return x * lax.rsqrt(jnp.mean(x * x, axis=-1, keepdims=True) + EPS) * g


def _dot(a, b):
    return jnp.dot(a, b, preferred_element_type=F32)


def _dot_t(a, b):
    return lax.dot_general(a, b, (((1,), (1,)), ((), ())), preferred_element_type=F32)


def _rope(t, cos, sina, sinb):
    n = t.shape[1]
    half = HEAD_DIM // 2
    return t * cos + pltpu.roll(t, n - half, 1) * sina + pltpu.roll(t, half, 1) * sinb


def _inproj_even_body(x_ref, g_ref, w_ref, u_ref, q_ref, k_ref, v_ref, cphi_ref, sphi_ref, *, start, period):
    o0 = SSM_WIDTH
    o1 = o0 + N_HEADS * HEAD_DIM
    o2 = o1 + KV_WIDTH
    tm = x_ref.shape[1]
    half = HEAD_DIM // 2
    lane = lax.broadcasted_iota(jnp.int32, (1, LANES), 1)
    inv = jnp.exp((lane & (half - 1)).astype(F32) * (-2.0 / HEAD_DIM * math.log(ROPE_THETA)))
    first = (lane & (HEAD_DIM - 1)) < half

    @pl.when((pl.program_id(0) == 0) & (pl.program_id(1) == 0))
    def _():
        row = lax.broadcasted_iota(jnp.int32, (tm, LANES), 0)
        phi = (row & (period - 1)).astype(F32) * inv
        cphi_ref[...] = jnp.cos(phi)
        sphi_ref[...] = jnp.sin(phi)

    theta = (start + ((pl.program_id(1) * tm) & (period - 1))).astype(F32) * inv
    c_th, s_th = jnp.cos(theta), jnp.sin(theta)
    nparts = 2 if tm % (2 * LANES) == 0 else 1
    parts = [slice(p * tm // nparts, (p + 1) * tm // nparts) for p in range(nparts)]
    projs = [_dot(_rms(x_ref[0, r, :], g_ref[0]).astype(BF16), w_ref[0]) for r in parts]
    for r, proj in zip(parts, projs):
        c_ph, s_ph = cphi_ref[r, :], sphi_ref[r, :]
        cos = c_th * c_ph - s_th * s_ph
        sin = s_th * c_ph + c_th * s_ph
        sa = jnp.where(first, -sin, 0.0)
        sb = jnp.where(first, 0.0, sin)
        u_ref[0, r, :] = proj[:, :o0]
        q = [_rope(proj[:, c:c + LANES], cos, sa, sb) for c in range(o0, o1, LANES)]
        q_ref[0, r, :] = (jnp.concatenate(q, axis=1) * HEAD_DIM ** -0.5).astype(BF16)
        k_ref[0, r, :] = _rope(proj[:, o1:o2], cos, sa, sb)
        v_ref[0, r, :] = proj[:, o2:]


def _inproj_even(x, g, w_in, start, period, layer_e, tm):
    bsz, seq, _ = x.shape
    tok = lambda w: pl.BlockSpec((1, tm, w), lambda b, i: (b, i, 0))
    return pl.pallas_call(
        functools.partial(_inproj_even_body, start=start, period=period),
        grid=(bsz, seq // tm),
        in_specs=[tok(D_MODEL), _gain_spec(2 * layer_e),
                  _const_spec((1, D_MODEL, EVEN_IN), (layer_e, 0, 0))],
        out_specs=[tok(SSM_WIDTH), tok(N_HEADS * HEAD_DIM), tok(KV_WIDTH), tok(KV_WIDTH)],
        out_shape=[jax.ShapeDtypeStruct((bsz, seq, SSM_WIDTH), F32),
                   jax.ShapeDtypeStruct((bsz, seq, N_HEADS * HEAD_DIM), BF16),
                   jax.ShapeDtypeStruct((bsz, seq, KV_WIDTH), F32),
                   jax.ShapeDtypeStruct((bsz, seq, KV_WIDTH), F32)],
        scratch_shapes=[pltpu.VMEM((tm, LANES), F32)] * 2,
        compiler_params=_params(2),
        name="inproj_even",
    )(x, g, w_in)


def _s5_prep_body(lr_ref, li_ref, ld_ref, bre_ref, bim_ref, cre_ref, cim_ref, wg_ref, wy_ref):
    ts = S5_TILE_STATES
    tw = S5_T * LANES

    def times(a_r, a_i):
        return lambda p: (p[0] * a_r - p[1] * a_i, p[0] * a_i + p[1] * a_r)

    lr, li = lr_ref[0], li_ref[0]
    dt = jnp.exp(ld_ref[0])
    mag = jnp.exp(lr * dt)
    ar, ai = mag * jnp.cos(li * dt), mag * jnp.sin(li * dt)
    mul_a = times(ar, ai)
    nrm = lr * lr + li * li
    cbr = ((ar - 1.0) * lr + ai * li) / nrm
    cbi = (ai * lr - (ar - 1.0) * li) / nrm
    b_re, b_im = bre_ref[0], bim_ref[0]
    bbr = cbr * b_re - cbi * b_im
    bbi = cbr * b_im + cbi * b_re
    p = (jnp.ones_like(ar), jnp.zeros_like(ar))
    for s in reversed(range(S5_T)):
        rows = slice(s * LANES, (s + 1) * LANES)
        wg_ref[0, rows, :ts] = (p[0] * bbr - p[1] * bbi).astype(BF16)
        wg_ref[0, rows, ts:] = (p[0] * bbi + p[1] * bbr).astype(BF16)
        for t in range(s):
            wy_ref[0, rows, t * LANES:(t + 1) * LANES] = jnp.zeros((LANES, LANES), BF16)
        p = mul_a(p)
    to_col = lambda row: jnp.transpose(jnp.broadcast_to(row, (LANES, ts)))
    mul_ac = times(to_col(ar), to_col(ai))
    c_re, c_im = cre_ref[0], cim_ref[0]
    bb = jnp.concatenate([bbr, bbi], axis=1).astype(BF16)
    pc = (jnp.ones_like(c_re), jnp.zeros_like(c_re))
    for k in range(S5_T + 1):
        blk = jnp.concatenate([pc[0] * c_re - pc[1] * c_im, -(pc[0] * c_im + pc[1] * c_re)], axis=0).astype(BF16)
        if k >= 1:
            wy_ref[0, tw:, (k - 1) * LANES:k * LANES] = blk
        if k < S5_T:
            lag = _dot(bb, blk).astype(BF16)
            for s in range(S5_T - k):
                t = s + k
                wy_ref[0, s * LANES:(s + 1) * LANES, t * LANES:(t + 1) * LANES] = lag
        pc = mul_ac(pc)


def _s5_prep(a_re, a_im, log_dt, b_re, b_im, c_re, c_im):
    gpt = LANES // SSM_GROUP
    eye = jnp.eye(gpt, dtype=bool)

    def rows(p):
        return p.reshape(S5_TILES, 1, S5_TILE_STATES)

    def bmat(b):
        bt = b.reshape(S5_TILES, gpt, SSM_STATE, SSM_GROUP).transpose(0, 1, 3, 2)
        full = jnp.where(eye[None, :, None, :, None], bt[:, :, :, None, :], 0.0)
        return full.reshape(S5_TILES, LANES, S5_TILE_STATES)

    def cmat(c):
        ct = c.reshape(S5_TILES, gpt, SSM_GROUP, SSM_STATE).transpose(0, 1, 3, 2)
        full = jnp.where(eye[None, :, None, :, None], ct[:, :, :, None, :], 0.0)
        return full.reshape(S5_TILES, S5_TILE_STATES, LANES)

    row_spec = pl.BlockSpec((1, 1, S5_TILE_STATES), lambda j: (j, 0, 0))
    b_spec = pl.BlockSpec((1, LANES, S5_TILE_STATES), lambda j: (j, 0, 0))
    c_spec = pl.BlockSpec((1, S5_TILE_STATES, LANES), lambda j: (j, 0, 0))
    tw = S5_T * LANES
    return pl.pallas_call(
        _s5_prep_body,
        grid=(S5_TILES,),
        in_specs=[row_spec] * 3 + [b_spec] * 2 + [c_spec] * 2,
        out_specs=[pl.BlockSpec((1, tw, 2 * S5_TILE_STATES), lambda j: (j, 0, 0)),
                   pl.BlockSpec((1, tw + 2 * S5_TILE_STATES, tw), lambda j: (j, 0, 0))],
        out_shape=[jax.ShapeDtypeStruct((S5_TILES, tw, 2 * S5_TILE_STATES), BF16),
                   jax.ShapeDtypeStruct((S5_TILES, tw + 2 * S5_TILE_STATES, tw), BF16)],
        compiler_params=_params(1),
        name="s5_prep",
    )(rows(a_re), rows(a_im), rows(log_dt), bmat(b_re), bmat(b_im), cmat(c_re), cmat(c_im))


def _s5_pow_body(ex_ref, lr_ref, li_ref, ld_ref, *o_refs):
    dt = jnp.exp(ld_ref[...])
    zr, zi = lr_ref[...] * dt, li_ref[...] * dt
    lo = 0
    for o_ref in o_refs:
        ex = ex_ref[lo:lo + o_ref.shape[0], :]
        lo += o_ref.shape[0]
        mag = jnp.exp(ex * zr)
        o_ref[:, :S5_STATES] = mag * jnp.cos(ex * zi)
        o_ref[:, S5_STATES:] = mag * jnp.sin(ex * zi)


def _s5_pow_table(groups, a_re, a_im, log_dt):
    sizes = [len(g) for g in groups]
    ex = jnp.asarray(np.asarray(sum(groups, []), np.float32)[:, None])
    flat = lambda p: p.reshape(1, S5_STATES)
    whole = lambda n, w: pl.BlockSpec((n, w), lambda i: (0, 0))
    return pl.pallas_call(
        _s5_pow_body,
        grid=(1,),
        in_specs=[whole(sum(sizes), 1)] + [whole(1, S5_STATES)] * 3,
        out_specs=[whole(n, 2 * S5_STATES) for n in sizes],
        out_shape=[jax.ShapeDtypeStruct((n, 2 * S5_STATES), F32) for n in sizes],
        compiler_params=_params(1),
        name="s5_pow_table",
    )(ex, flat(a_re), flat(a_im), flat(log_dt))


def _s5_body(u_ref, h0_ref, wg_ref, wy_ref, pow2_ref, powr_ref, d_ref, wglu_ref,
             y_ref, hout_ref, carry_ref, us_ref, ys_ref, *, ni, chained):
    ns = S5_STATES
    ts = S5_TILE_STATES
    segtok = ni * S5_T
    pitch = _s5_pitch(ni)

    @pl.when(pl.program_id(1) == 0)
    def _():
        carry_ref[...] = h0_ref[0]

    for j in range(S5_TILES):
        for s in range(SUBLANES):
            us_ref[j, s * pitch:s * pitch + segtok] = u_ref[0, s * segtok:(s + 1) * segtok, j * LANES:(j + 1) * LANES]

    def gather(j, t):
        return jnp.concatenate([us_ref[j, pl.ds(S5_T * i + t, SUBLANES, stride=pitch), :]
                                for i in range(ni)], axis=0)

    def fma(p_r, p_i, x_r, x_i, y_r, y_i):
        return y_r + p_r * x_r - p_i * x_i, y_i + p_r * x_i + p_i * x_r

    xs, g_re, g_im = [], [], []
    for j in range(S5_TILES):
        xj = jnp.concatenate([gather(j, s) for s in range(S5_T)], axis=1).astype(BF16)
        g = _dot(xj, wg_ref[j])
        xs.append(xj)
        g_re.append(g[:, :ts])
        g_im.append(g[:, ts:])
    g_re = jnp.concatenate(g_re, axis=1)
    g_im = jnp.concatenate(g_im, axis=1)
    slab = lambda a, i: a[i * SUBLANES:(i + 1) * SUBLANES]
    power = lambda ref, r: (ref[r:r + 1, :ns], ref[r:r + 1, ns:])
    a_r, a_i = power(powr_ref, 0)
    cin = carry_ref[...]
    h_r, h_i = cin[:, :ns], cin[:, ns:]
    prev = []
    if chained:
        local = [(slab(g_re, 0), slab(g_im, 0))]
        for i in range(1, ni):
            local.append(fma(a_r, a_i, *local[-1], slab(g_re, i), slab(g_im, i)))
        sub = lax.broadcasted_iota(jnp.int32, (SUBLANES, 1), 0)
        c_r = jnp.where(sub == 0, h_r, pltpu.roll(local[-1][0], 1, 0))
        c_i = jnp.where(sub == 0, h_i, pltpu.roll(local[-1][1], 1, 0))
        for m in range(3):
            k = 1 << m
            s_r = jnp.where(sub >= k, pltpu.roll(c_r, k, 0), 0.0)
            s_i = jnp.where(sub >= k, pltpu.roll(c_i, k, 0), 0.0)
            c_r, c_i = fma(*power(pow2_ref, m), s_r, s_i, c_r, c_i)
        h_r, h_i = c_r, c_i
        for i in range(ni):
            prev.append((h_r, h_i))
            h_r, h_i = fma(*power(powr_ref, i), c_r, c_i, *local[i])
        last = jnp.concatenate([h_r[SUBLANES - 1:], h_i[SUBLANES - 1:]], axis=1)
    else:
        for i in range(ni):
            prev.append((h_r, h_i))
            h_r, h_i = fma(a_r, a_i, h_r, h_i, slab(g_re, i), slab(g_im, i))
        last = jnp.concatenate([h_r, h_i], axis=1)
    carry_ref[...] = last
    hout_ref[0] = last
    prev_re = jnp.concatenate([p[0] for p in prev], axis=0)
    prev_im = jnp.concatenate([p[1] for p in prev], axis=0)
    y_tiles = []
    for j in range(S5_TILES):
        lhs = jnp.concatenate([xs[j], prev_re[:, j * ts:(j + 1) * ts].astype(BF16),
                               prev_im[:, j * ts:(j + 1) * ts].astype(BF16)], axis=1)
        y_tiles.append(_dot(lhs, wy_ref[j]))
    d = d_ref[...]
    wglu = wglu_ref[0]
    def gelu_in(t):
        ut = jnp.concatenate([gather(j, t) for j in range(S5_TILES)], axis=1)
        yt = jnp.concatenate([y_tiles[j][:, t * LANES:(t + 1) * LANES] for j in range(S5_TILES)], axis=1)
        return jax.nn.gelu(yt + d * ut)

    def scatter(t, o):
        for j in range(S5_TILES):
            for i in range(ni):
                ys_ref[j, pl.ds(S5_T * i + t, SUBLANES, stride=pitch), :] = slab(o, i)[:, j * LANES:(j + 1) * LANES]

    g_q, z_q = {}, {}
    for step in range(S5_T + 2):
        if step < S5_T:
            g_q[step] = gelu_in(step)
        if 0 <= step - 1 < S5_T:
            z_q[step - 1] = _dot(g_q[step - 1].astype(BF16), wglu)
        if 0 <= step - 2 < S5_T:
            scatter(step - 2, g_q.pop(step - 2) * jax.nn.sigmoid(z_q.pop(step - 2)))
    for j in range(S5_TILES):
        for s in range(SUBLANES):
            y_ref[0, s * segtok:(s + 1) * segtok, j * LANES:(j + 1) * LANES] = (
                ys_ref[j, s * pitch:s * pitch + segtok].astype(BF16))


def _s5_pitch(ni):
    segtok = ni * S5_T
    return segtok + SUBLANES if (segtok // SUBLANES) % 2 == 0 else segtok


def _s5_mixer(u, h0, wg, wy, pow2, powr, d, w_glu, layer_e, ni, chained):
    nb, ntok, width = u.shape
    tb = SUBLANES * ni * S5_T
    hrows = h0.shape[1]
    blk = pl.BlockSpec((1, tb, width), lambda b, i: (b, i, 0))
    state = pl.BlockSpec((1, hrows, 2 * S5_STATES), lambda b, i: (b, 0, 0))
    return pl.pallas_call(
        functools.partial(_s5_body, ni=ni, chained=chained),
        grid=(nb, ntok // tb),
        in_specs=[blk, state, _const_spec(wg.shape), _const_spec(wy.shape),
                  _const_spec(pow2.shape), _const_spec(powr.shape), _const_spec((1, SSM_WIDTH)),
                  _const_spec((1, SSM_WIDTH, SSM_WIDTH), (layer_e, 0, 0))],
        out_specs=[blk, state],
        out_shape=[jax.ShapeDtypeStruct((nb, ntok, width), BF16),
                   jax.ShapeDtypeStruct((nb, hrows, 2 * S5_STATES), F32)],
        scratch_shapes=[pltpu.VMEM((hrows, 2 * S5_STATES), F32)]
        + [pltpu.VMEM((S5_TILES, SUBLANES * _s5_pitch(ni), LANES), F32)] * 2,
        compiler_params=_params(2),
        name="s5_mixer",
    )(u, h0, wg, wy, pow2, powr, d, w_glu)


def _swa_body(sink_ref, q_ref, kc_ref, vc_ref, kp_ref, vp_ref, pk_ref, pv_ref, o_ref, *, tq, qg, start):
    first = pl.program_id(1) == 0
    k_prev = jnp.where(first, pk_ref[0], kp_ref[0])
    v_prev = jnp.where(first, pv_ref[0], vp_ref[0])
    k_all = jnp.concatenate([k_prev, kc_ref[0]], axis=0)
    v_all = jnp.concatenate([v_prev, vc_ref[0]], axis=0)
    nk = WINDOW + qg
    lane = lax.broadcasted_iota(jnp.int32, (1, KV_WIDTH), 1)
    slot = lax.broadcasted_iota(jnp.int32, (1, nk), 1)
    qrow = lax.broadcasted_iota(jnp.int32, (2 * qg, 1), 0)
    lower = lane < HEAD_DIM
    placed = []
    for kh in range(N_KV_HEADS):
        own = lower if kh == 0 else jnp.logical_not(lower)
        kz, vz = jnp.where(own, k_all, 0.0), jnp.where(own, v_all, 1.0)
        kr, vr = pltpu.roll(kz, HEAD_DIM, 1), pltpu.roll(vz, HEAD_DIM, 1)
        k_lo, k_hi = (kz, kr) if kh == 0 else (kr, kz)
        v_lo, v_hi = (vz, vr) if kh == 0 else (vr, vz)
        placed.append(((k_lo.astype(BF16), v_lo.astype(BF16)), (k_hi.astype(BF16), v_hi.astype(BF16))))
    combos = [(kh, par) for kh in range(N_KV_HEADS) for par in range(2)]

    def scores(c):
        qc = q_ref[0, c * qg:(c + 1) * qg, :]
        out = []
        for kh, par in combos:
            base = kh * GQ * HEAD_DIM
            qs = jnp.concatenate([qc[:, base:base + LANES], qc[:, base + LANES:base + 2 * LANES]], axis=0)
            out.append(_dot_t(qs, placed[kh][par][0][c * qg:c * qg + nk]))
        return out

    def softmax(c, ss):
        n_bad = jnp.where(first, WINDOW - start - c * qg, 0) if c * qg < WINDOW - start else None
        probs, sinks = [], []
        for (kh, par), s in zip(combos, ss):
            if n_bad is not None:
                s = jnp.where(slot >= n_bad, s, NEG)
            sink = jnp.where(qrow < qg, sink_ref[kh * GQ + par], sink_ref[kh * GQ + 2 + par])
            m = jnp.maximum(jnp.max(s, axis=1, keepdims=True), sink)
            probs.append(jnp.exp(s - m).astype(BF16))
            sinks.append(jnp.exp(sink - m))
        return probs, sinks

    def finish(c, probs, sinks):
        pvs = [_dot(p, placed[kh][par][1][c * qg:c * qg + nk]) for (kh, par), p in zip(combos, probs)]
        outs = []
        for kh in range(N_KV_HEADS):
            pv_e, pv_o = pvs[2 * kh], pvs[2 * kh + 1]
            num = jnp.where(lower, pv_e, pv_o)
            den = pltpu.roll(jnp.where(lower, pv_o, pv_e), HEAD_DIM, 1) + jnp.where(lower, sinks[2 * kh], sinks[2 * kh + 1])
            acc = num / den
            outs += [acc[:qg], acc[qg:]]
        o_ref[0, c * qg:(c + 1) * qg, :] = jnp.concatenate(outs, axis=1).astype(BF16)

    nchunk = tq // qg
    per = min(SWA_STAGE_CHUNKS, nchunk)
    groups = [range(g, min(g + per, nchunk)) for g in range(0, nchunk, per)]
    s_q, p_q = {}, {}
    for step in range(len(groups) + 2):
        if step < len(groups):
            s_q[step] = [scores(c) for c in groups[step]]
        if 0 <= step - 1 < len(groups):
            p_q[step - 1] = [softmax(c, ss) for c, ss in zip(groups[step - 1], s_q.pop(step - 1))]
        if 0 <= step - 2 < len(groups):
            for c, (probs, sinks) in zip(groups[step - 2], p_q.pop(step - 2)):
                finish(c, probs, sinks)


def _swa(q, k, v, past_k, past_v, sinks, start, tq, qg):
    bsz, seq, _ = q.shape
    ntiles = seq // tq
    tok = lambda w: pl.BlockSpec((1, tq, w), lambda b, i: (b, i, 0))
    past = pl.BlockSpec((1, WINDOW, KV_WIDTH), lambda b, i: (b, 0, 0))
    if ntiles > 1:
        per = tq // WINDOW
        prev = pl.BlockSpec((1, WINDOW, KV_WIDTH), lambda b, i: (b, jnp.maximum(i * per - 1, 0), 0))
        k_prev, v_prev = k, v
    else:
        prev, k_prev, v_prev = past, past_k, past_v
    return pl.pallas_call(
        functools.partial(_swa_body, tq=tq, qg=qg, start=start),
        grid=(bsz, ntiles),
        in_specs=[pl.BlockSpec(memory_space=pltpu.SMEM), tok(N_HEADS * HEAD_DIM), tok(KV_WIDTH), tok(KV_WIDTH),
                  prev, prev, past, past],
        out_specs=tok(N_HEADS * HEAD_DIM),
        out_shape=jax.ShapeDtypeStruct((bsz, seq, N_HEADS * HEAD_DIM), BF16),
        compiler_params=_params(2),
        name="swa",
    )(sinks, q, k, v, k_prev, v_prev, past_k, past_v)


def _xattn(x1, g, wq, mk_ref, mv_ref, wo, nseq, o_ref):
    h = _rms(x1, g).astype(BF16)
    q = (_dot(h, wq) * X_HEAD_DIM ** -0.5).astype(BF16)
    nparts = nseq if nseq > 1 else XATTN_PARTS
    prows = x1.shape[0] // nparts
    heads = [slice(hd * X_HEAD_DIM, (hd + 1) * X_HEAD_DIM) for hd in range(X_HEADS)]

    def scores(p):
        r, b = slice(p * prows, (p + 1) * prows), (p if nseq > 1 else 0)
        return [_dot_t(q[r, c], mk_ref[0, b, :, c]) for c in heads]

    def softmax(ss):
        out = []
        for s in ss:
            e = jnp.exp(s - jnp.max(s, axis=1, keepdims=True))
            out.append((e.astype(BF16), jnp.sum(e, axis=1, keepdims=True)))
        return out

    def values(p, pd):
        b = p if nseq > 1 else 0
        return jnp.concatenate([_dot(e, mv_ref[0, b, :, c]) / den for c, (e, den) in zip(heads, pd)],
                               axis=1).astype(BF16)

    def finish(p, pd):
        r = slice(p * prows, (p + 1) * prows)
        o_ref[0, r, :] = x1[r] + _dot(values(p, pd), wo)

    if nseq > 1:
        pds = [softmax(ss) for ss in [scores(p) for p in range(nparts)]]
        o = jnp.concatenate([values(p, pd) for p, pd in enumerate(pds)], axis=0)
        o_ref[0] = x1 + _dot(o, wo)
        return
    s_q, p_q = {}, {}
    for step in range(nparts + 2):
        if step < nparts:
            s_q[step] = scores(step)
        if 0 <= step - 1 < nparts:
            p_q[step - 1] = softmax(s_q.pop(step - 1))
        if 0 <= step - 2 < nparts:
            finish(step - 2, p_q.pop(step - 2))


def _conv3(cur, carry, w):
    n = cur.shape[0]
    row = lax.broadcasted_iota(jnp.int32, (n, 1), 0)
    nc = carry.shape[0]
    c1, c2 = carry[nc - 1:nc], carry[nc - 2:nc - 1]
    m1 = jnp.where(row == 0, c1, pltpu.roll(cur, 1, 0))
    m2 = jnp.where(row == 0, c2, jnp.where(row == 1, c1, pltpu.roll(cur, 2, 0)))
    return w[0:1] * m2 + w[1:2] * m1 + w[2:3] * cur


def _conv3_seqs(cur, carries, w):
    nseq = len(carries)
    if nseq == 1:
        return _conv3(cur, carries[0], w)
    rows = cur.shape[0] // nseq
    body = _conv3(cur, carries[0], w)
    pieces = []
    for b in range(nseq):
        pieces.append(_conv3(cur[b * rows:b * rows + SUBLANES], carries[b], w))
        pieces.append(body[b * rows + SUBLANES:(b + 1) * rows])
    return jnp.concatenate(pieces, axis=0)


def _store_last_rows(st_ref, cols, cur, nseq):
    rows = cur.shape[0] // nseq
    for b in range(nseq):
        st_ref[b, :, cols] = cur[(b + 1) * rows - (CONV_WIDTH - 1):(b + 1) * rows]


def _mix_even_xattn_body(x_ref, ya_ref, yb_ref, woa_ref, wob_ref, g_ref, wq_ref, mk_ref, mv_ref, wo_ref, o_ref,
                         *, nseq):
    x1 = x_ref[0] + _dot(ya_ref[0], woa_ref[0]) + _dot(yb_ref[0], wob_ref[0])
    _xattn(x1, g_ref[0], wq_ref[0], mk_ref, mv_ref, wo_ref[0], nseq, o_ref)


def _mix_even_xattn(x, ya, yb, w_out, g, wq, mem_k, mem_v, wo, layer, layer_e, tm, nseq):
    bsz, seq, _ = x.shape
    tok = lambda w: pl.BlockSpec((1, tm, w), lambda b, i: (b, i, 0))
    mem = pl.BlockSpec((1, nseq, N_MEM, D_MODEL), lambda b, i: (layer, b, 0, 0))
    half = D_MODEL // 2
    sq = lambda: _const_spec((1, D_MODEL, D_MODEL), (layer, 0, 0))
    return pl.pallas_call(
        functools.partial(_mix_even_xattn_body, nseq=nseq),
        grid=(bsz, seq // tm),
        in_specs=[tok(D_MODEL), tok(half), tok(half),
                  _const_spec((1, half, D_MODEL), (layer_e, 0, 0)), _const_spec((1, half, D_MODEL), (layer_e, 1, 0)),
                  _gain_spec(layer), sq(), mem, mem, sq()],
        out_specs=tok(D_MODEL),
        out_shape=jax.ShapeDtypeStruct(x.shape, F32),
        compiler_params=_params(2),
        name="mix_even_xattn",
    )(x, ya, yb, w_out, w_out, g, wq, mem_k, mem_v, wo)


def _mix_odd_xattn_body(x_ref, gm_ref, win_ref, cw_ref, wout_ref, prev_ref, g_ref, wq_ref, mk_ref, mv_ref, wo_ref,
                        o_ref, st_ref, carry_ref, *, tm, nseq):
    @pl.when(pl.program_id(1) == 0)
    def _():
        carry_ref[SUBLANES - (CONV_WIDTH - 1):, :] = prev_ref[0, 0]

    x = x_ref[0]
    proj = _dot(_rms(x, gm_ref[0]).astype(BF16), win_ref[0])
    gate_b, gate_c, z = proj[:, :D_MODEL], proj[:, D_MODEL:2 * D_MODEL], proj[:, 2 * D_MODEL:]
    cz = gate_c * z
    keep = CONV_WIDTH - 1
    carries = [carry_ref[SUBLANES - keep:, :]] if nseq == 1 else [prev_ref[0, b] for b in range(nseq)]
    z_conv = _conv3_seqs(cz, carries, cw_ref[0])
    carry_ref[...] = cz[tm - SUBLANES:]
    _store_last_rows(st_ref, slice(None), cz, nseq)
    x1 = x + _dot((gate_b * z_conv).astype(BF16), wout_ref[0])
    _xattn(x1, g_ref[0], wq_ref[0], mk_ref, mv_ref, wo_ref[0], nseq, o_ref)


def _mix_odd_xattn(x, gm, w_in, conv_w, w_out, prev, g, wq, mem_k, mem_v, wo, layer, layer_o, tm, nseq):
    bsz, seq, _ = x.shape
    tok = pl.BlockSpec((1, tm, D_MODEL), lambda b, i: (b, i, 0))
    mem = pl.BlockSpec((1, nseq, N_MEM, D_MODEL), lambda b, i: (layer, b, 0, 0))
    hist = pl.BlockSpec((1, nseq, CONV_WIDTH - 1, D_MODEL), lambda b, i: (layer_o, b, 0, 0))
    st = pl.BlockSpec((nseq, CONV_WIDTH - 1, D_MODEL), lambda b, i: (b, 0, 0))
    sq = lambda l: _const_spec((1, D_MODEL, D_MODEL), (l, 0, 0))
    return pl.pallas_call(
        functools.partial(_mix_odd_xattn_body, tm=tm, nseq=nseq),
        grid=(bsz, seq // tm),
        in_specs=[tok, _gain_spec(layer), _const_spec((1, D_MODEL, 3 * D_MODEL), (layer_o, 0, 0)),
                  _const_spec((1, CONV_WIDTH, D_MODEL), (layer_o, 0, 0)), sq(layer_o), hist,
                  _gain_spec(layer), sq(layer), mem, mem, sq(layer)],
        out_specs=[tok, st],
        out_shape=[jax.ShapeDtypeStruct(x.shape, F32),
                   jax.ShapeDtypeStruct((bsz * nseq, CONV_WIDTH - 1, D_MODEL), F32)],
        scratch_shapes=[pltpu.VMEM((SUBLANES, D_MODEL), F32)],
        compiler_params=_params(2),
        name="mix_odd_xattn",
    )(x, gm, w_in, conv_w, w_out, prev, g, wq, mem_k, mem_v, wo)


def _ffn_body(x_ref, g_ref, wup_ref, cw_ref, wdn_ref, prev_ref, gfin_ref, o_ref, st_ref, carry_ref,
              *, tm, final, nseq):
    @pl.when(pl.program_id(1) == 0)
    def _():
        carry_ref[SUBLANES - (CONV_WIDTH - 1):, :] = prev_ref[0, 0]

    x = x_ref[0]
    h = _rms(x, g_ref[0]).astype(BF16)
    chunks = [slice(lo, min(lo + FF_CHUNK, D_FF)) for lo in range(0, D_FF, FF_CHUNK)]

    def up(cols):
        return _dot(h, wup_ref[0, :, cols]), _dot(h, wup_ref[0, :, D_FF + cols.start:D_FF + cols.stop])

    acc = x
    nxt = up(chunks[0])
    for n, cols in enumerate(chunks):
        (gate, val), nxt = nxt, (up(chunks[n + 1]) if n + 1 < len(chunks) else None)
        carries = ([carry_ref[SUBLANES - (CONV_WIDTH - 1):, cols]] if nseq == 1
                   else [prev_ref[0, b, :, cols] for b in range(nseq)])
        gate_c = _conv3_seqs(gate, carries, cw_ref[0, :, cols])
        carry_ref[:, cols] = gate[tm - SUBLANES:]
        _store_last_rows(st_ref, cols, gate, nseq)
        act = (gate_c * jax.nn.sigmoid(gate_c) * val).astype(BF16)
        if n + 1 < len(chunks):
            acc = acc + _dot(act, wdn_ref[0, cols, :])
        else:
            half = tm // 2
            for rows in (slice(0, half), slice(half, tm)):
                out = acc[rows] + _dot(act[rows], wdn_ref[0, cols, :])
                o_ref[0, rows, :] = _rms(out, gfin_ref[0]) if final else out


def _ffn(x, g, w_up, conv_w, w_down, prev, g_final, layer, tm, final, nseq):
    bsz, seq, _ = x.shape
    tok = pl.BlockSpec((1, tm, D_MODEL), lambda b, i: (b, i, 0))
    hist = pl.BlockSpec((1, nseq, CONV_WIDTH - 1, D_FF), lambda b, i: (layer, b, 0, 0))
    st = pl.BlockSpec((nseq, CONV_WIDTH - 1, D_FF), lambda b, i: (b, 0, 0))
    return pl.pallas_call(
        functools.partial(_ffn_body, tm=tm, final=final, nseq=nseq),
        grid=(bsz, seq // tm),
        in_specs=[tok, _gain_spec(layer), _const_spec((1, D_MODEL, 2 * D_FF), (layer, 0, 0)),
                  _const_spec((1, CONV_WIDTH, D_FF), (layer, 0, 0)), _const_spec((1, D_FF, D_MODEL), (layer, 0, 0)),
                  hist, _gain_spec(0)],
        out_specs=[tok, st],
        out_shape=[jax.ShapeDtypeStruct(x.shape, F32),
                   jax.ShapeDtypeStruct((bsz * nseq, CONV_WIDTH - 1, D_FF), F32)],
        scratch_shapes=[pltpu.VMEM((SUBLANES, D_FF), F32)],
        compiler_params=_params(2),
        name="conv_ffn",
    )(x, g, w_up, conv_w, w_down, prev, g_final)


def _mem_flat_copies(k_hbm, v_hbm, kbuf, vbuf, sem, step, slot, nbatch):
    lyr, b = step // nbatch, step % nbatch
    return [pltpu.make_async_copy(src.at[lyr, b, :, hd, :], buf.at[slot, hd], sem.at[slot, i, hd])
            for i, (src, buf) in enumerate(((k_hbm, kbuf), (v_hbm, vbuf))) for hd in range(X_HEADS)]


def _mem_flat_body(k_hbm, v_hbm, ko_ref, vo_ref, kbuf, vbuf, sem, *, nbatch):
    s, n = pl.program_id(0), pl.num_programs(0)
    copies = functools.partial(_mem_flat_copies, k_hbm, v_hbm, kbuf, vbuf, sem, nbatch=nbatch)

    @pl.when(s == 0)
    def _():
        for c in copies(step=s, slot=0):
            c.start()

    @pl.when(s + 1 < n)
    def _():
        for c in copies(step=s + 1, slot=(s + 1) % 2):
            c.start()

    slot = s % 2
    for c in copies(step=s, slot=slot):
        c.wait()
    for hd in range(X_HEADS):
        cols = slice(hd * X_HEAD_DIM, (hd + 1) * X_HEAD_DIM)
        ko_ref[0, 0, :, cols] = kbuf[slot, hd].astype(BF16)
        vo_ref[0, 0, :, cols] = vbuf[slot, hd].astype(BF16)


def _mem_flat(mem_k, mem_v):
    depth, nbatch = mem_k.shape[:2]
    flat = pl.BlockSpec((1, 1, N_MEM, D_MODEL), lambda s: (s // nbatch, s % nbatch, 0, 0))
    return pl.pallas_call(
        functools.partial(_mem_flat_body, nbatch=nbatch),
        grid=(depth * nbatch,),
        in_specs=[pl.BlockSpec(memory_space=pl.ANY)] * 2,
        out_specs=[flat, flat],
        out_shape=[jax.ShapeDtypeStruct((depth, nbatch, N_MEM, D_MODEL), BF16)] * 2,
        scratch_shapes=[pltpu.VMEM((2, X_HEADS, N_MEM, X_HEAD_DIM), F32)] * 2
        + [pltpu.SemaphoreType.DMA((2, 2, X_HEADS))],
        compiler_params=_params(1),
        name="mem_flat",
    )(mem_k, mem_v)


def _mem_kv_body(m_ref, w_ref, k_ref, v_ref, kf_ref, vf_ref):
    kv = _dot(m_ref[0].astype(BF16), w_ref[0])
    kf_ref[0, 0] = kv[:, :D_MODEL].astype(BF16)
    vf_ref[0, 0] = kv[:, D_MODEL:].astype(BF16)
    for hd in range(X_HEADS):
        k_ref[0, 0, :, hd, :] = kv[:, hd * X_HEAD_DIM:(hd + 1) * X_HEAD_DIM]
        v_ref[0, 0, :, hd, :] = kv[:, D_MODEL + hd * X_HEAD_DIM:D_MODEL + (hd + 1) * X_HEAD_DIM]


def _mem_kv(mem, w_kv):
    bsz = mem.shape[0]
    heads = pl.BlockSpec((1, 1, N_MEM, X_HEADS, X_HEAD_DIM), lambda l, b: (l, b, 0, 0, 0))
    flat = pl.BlockSpec((1, 1, N_MEM, D_MODEL), lambda l, b: (l, b, 0, 0))
    return pl.pallas_call(
        _mem_kv_body,
        grid=(DEPTH, bsz),
        in_specs=[pl.BlockSpec((1, N_MEM, D_MODEL), lambda l, b: (b, 0, 0)),
                  pl.BlockSpec((1, D_MODEL, 2 * D_MODEL), lambda l, b: (l, 0, 0))],
        out_specs=[heads, heads, flat, flat],
        out_shape=[jax.ShapeDtypeStruct((DEPTH, bsz, N_MEM, X_HEADS, X_HEAD_DIM), F32)] * 2
        + [jax.ShapeDtypeStruct((DEPTH, bsz, N_MEM, D_MODEL), BF16)] * 2,
        compiler_params=_params(2),
        name="mem_kv",
    )(mem, w_kv)


def _trunk(x, start, past_k, past_v, ssm_re, ssm_im, conv_mix_prev, conv_ffn_prev, mem_k, mem_v, w, s5, cfg):
    bsz, seq, _ = x.shape
    tq, qg, nseq = cfg["tq"], cfg["qg"], cfg["nseq"]
    tm_in, tm_even, tm_odd, tm_ffn = cfg["tm_in"], cfg["tm_even"], cfg["tm_odd"], cfg["tm_ffn"]
    ni, chained = cfg["ni"], cfg["chained"]
    ngrp, grows = bsz // nseq, nseq * seq
    grouped = lambda t: t.reshape(ngrp, grows, t.shape[-1])
    per_seq = lambda t: t.reshape(bsz, seq, t.shape[-1])
    u, q, k, v = _inproj_even(grouped(x), w["norm_mix"], w["w_in_even"], start, seq, 0, tm_in)
    h0 = jnp.concatenate([ssm_re.reshape(bsz, S5_STATES), ssm_im.reshape(bsz, S5_STATES)], axis=1)
    if chained:
        ya, h_all = _s5_mixer(u, h0[:, None, :], s5["wg"], s5["wy"], cfg["pow2"], cfg["powr"], w["ssm_d"],
                              w["w_glu"], 0, ni, True)
        h_last = h_all[:, 0]
    else:
        assert ngrp == 1 and bsz == SUBLANES and seq == ni * S5_T
        ya, h_all = _s5_mixer(u, h0[None], s5["wg"], s5["wy"], cfg["pow2"], cfg["powr"], w["ssm_d"],
                              w["w_glu"], 0, ni, False)
        h_last = h_all[0]
    new_re = h_last[:, :S5_STATES].reshape(1, bsz, SSM_GROUPS, SSM_STATE)
    new_im = h_last[:, S5_STATES:].reshape(1, bsz, SSM_GROUPS, SSM_STATE)
    k, v = per_seq(k), per_seq(v)
    yb = _swa(per_seq(q), k, v, past_k.reshape(bsz, WINDOW, KV_WIDTH), past_v.reshape(bsz, WINDOW, KV_WIDTH),
              w["attn_sinks"], start, tq, qg)
    xg = _mix_even_xattn(grouped(x), ya, grouped(yb), w["w_out_even"], w["norm_xattn"], w["xattn_wq"],
                         mem_k, mem_v, w["xattn_wo"], 0, 0, tm_even, nseq)
    xg, cf0 = _ffn(xg, w["norm_ffn"], w["ffn_w_up"], w["ffn_conv_w"], w["ffn_w_down"], conv_ffn_prev,
                   w["norm_final"], 0, tm_ffn, False, nseq)
    xg, cm = _mix_odd_xattn(xg, w["norm_mix"], w["w_in_odd"], w["conv_mix_w"], w["w_out_odd"],
                            conv_mix_prev, w["norm_xattn"], w["xattn_wq"], mem_k, mem_v,
                            w["xattn_wo"], 1, 0, tm_odd, nseq)
    y, cf1 = _ffn(xg, w["norm_ffn"], w["ffn_w_up"], w["ffn_conv_w"], w["ffn_w_down"], conv_ffn_prev,
                  w["norm_final"], 1, tm_ffn, True, nseq)
    keep = min(seq, WINDOW)
    new_k = k[:, seq - keep:].reshape(1, bsz, keep, N_KV_HEADS, HEAD_DIM)
    new_v = v[:, seq - keep:].reshape(1, bsz, keep, N_KV_HEADS, HEAD_DIM)
    return per_seq(y), new_k, new_v, new_re, new_im, cm[None], jnp.stack([cf0, cf1])


def kernel(x_prompt, x_sample, mem_prompt, cache_win_k, cache_win_v, state_ssm_re, state_ssm_im, state_conv_mix, state_conv_ffn, cache_mem_k, cache_mem_v, norm_mix, norm_xattn, norm_ffn, norm_final, w_in_even, w_out_even, ssm_a_re, ssm_a_im, ssm_log_dt, ssm_b_re, ssm_b_im, ssm_c_re, ssm_c_im, ssm_d, w_glu, attn_sinks, w_in_odd, conv_mix_w, w_out_odd, xattn_wq, xattn_wkv, xattn_wo, ffn_w_up, ffn_conv_w, ffn_w_down):
    bp, seq_p, _ = x_prompt.shape
    bs, seq_s, _ = x_sample.shape
    gains = lambda g: g.reshape(-1, 1, D_MODEL)
    w = dict(norm_mix=gains(norm_mix), norm_xattn=gains(norm_xattn), norm_ffn=gains(norm_ffn),
             norm_final=gains(norm_final),
             w_in_even=w_in_even.astype(BF16), w_out_even=w_out_even.astype(BF16),
             ssm_d=ssm_d.reshape(1, SSM_WIDTH), w_glu=w_glu.astype(BF16), attn_sinks=attn_sinks.reshape(N_HEADS),
             w_in_odd=w_in_odd.astype(BF16), conv_mix_w=conv_mix_w, w_out_odd=w_out_odd.astype(BF16),
             xattn_wq=xattn_wq.astype(BF16), xattn_wo=xattn_wo.astype(BF16),
             ffn_w_up=ffn_w_up.astype(BF16), ffn_conv_w=ffn_conv_w, ffn_w_down=ffn_w_down.astype(BF16))

    gp = (SSM_GROUPS, SSM_STATE)
    a_re, a_im, log_dt = ssm_a_re.reshape(gp), ssm_a_im.reshape(gp), ssm_log_dt.reshape(gp)
    wg, wy = _s5_prep(a_re, a_im, log_dt, ssm_b_re.reshape(gp + (SSM_GROUP,)), ssm_b_im.reshape(gp + (SSM_GROUP,)),
                      ssm_c_re.reshape(SSM_GROUPS, SSM_GROUP, SSM_STATE), ssm_c_im.reshape(SSM_GROUPS, SSM_GROUP, SSM_STATE))
    s5 = dict(wg=wg, wy=wy)
    ni_p, ni_s = S5_SEG_CHUNKS, seq_s // S5_T
    padded = lambda e: e + [1.0] * (-len(e) % SUBLANES)
    pow2, powr_p, powr_s = _s5_pow_table(
        [padded([float(S5_T * ni_p << m) for m in range(3)]),
         padded([float(S5_T * (i + 1)) for i in range(ni_p)]),
         padded([float(S5_T * (i + 1)) for i in range(ni_s)])], a_re, a_im, log_dt)
    cfg_p = dict(tm_in=1024, tm_even=1024, tm_odd=1024, tm_ffn=1024, tq=512, qg=CHUNK, nseq=1, ni=ni_p, chained=True,
                 pow2=pow2, powr=powr_p)
    rows_s = bs * seq_s
    cfg_s = dict(tm_in=rows_s, tm_even=rows_s, tm_odd=rows_s, tm_ffn=rows_s, tq=seq_s, qg=seq_s, nseq=bs, ni=ni_s,
                 chained=False, pow2=pow2, powr=powr_s)

    mem_k_p, mem_v_p, mem_kf_p, mem_vf_p = _mem_kv(mem_prompt, xattn_wkv.astype(BF16))
    cw = CONV_WIDTH - 1
    zk = jnp.zeros((1, bp, WINDOW, N_KV_HEADS, HEAD_DIM), F32)
    zs = jnp.zeros((1, bp, SSM_GROUPS, SSM_STATE), F32)
    zcm = jnp.zeros((1, bp, cw, D_MODEL), F32)
    zcf = jnp.zeros((DEPTH, bp, cw, D_FF), F32)
    y_p, k_p, v_p, re_p, im_p, cm_p, cf_p = _trunk(x_prompt, 0, zk, zk, zs, zs, zcm, zcf, mem_kf_p, mem_vf_p, w, s5, cfg_p)
    mem_kf_s, mem_vf_s = _mem_flat(cache_mem_k, cache_mem_v)
    y_s, k_s, v_s, re_s, im_s, cm_s, cf_s = _trunk(x_sample, PAST_LEN, cache_win_k, cache_win_v, state_ssm_re,
                                                   state_ssm_im, state_conv_mix, state_conv_ffn, mem_kf_s,
                                                   mem_vf_s, w, s5, cfg_s)
    return (y_p, y_s, k_p, v_p, re_p, im_p, cm_p, cf_p, mem_k_p, mem_v_p, k_s, v_s, re_s, im_s, cm_s, cf_s)
```

```python
import functools
import math

import numpy as np
import jax
import jax.numpy as jnp
from jax import lax
from jax.experimental import pallas as pl
from jax.experimental.pallas import tpu as pltpu

F32 = jnp.float32
BF16 = jnp.bfloat16

D_MODEL = 1024
DEPTH = 2
PAST_LEN = 2048
CHUNK = 64
SSM_WIDTH = 512
SSM_GROUP = 16
SSM_GROUPS = 32
SSM_STATE = 64
HEAD_DIM = 64
N_HEADS = 8
N_KV_HEADS = 2
GQ = N_HEADS // N_KV_HEADS
WINDOW = 128
ROPE_THETA = 10000.0
KV_WIDTH = N_KV_HEADS * HEAD_DIM
EVEN_IN = SSM_WIDTH + N_HEADS * HEAD_DIM + 2 * KV_WIDTH
CONV_WIDTH = 3
N_MEM = 256
X_HEADS = 4
X_HEAD_DIM = 256
D_FF = 2816
EPS = 1e-6
NEG = -1e30

LANES = 128
SUBLANES = 8
S5_T = 8
S5_TILES = SSM_WIDTH // LANES
S5_TILE_STATES = (LANES // SSM_GROUP) * SSM_STATE
S5_STATES = SSM_GROUPS * SSM_STATE
SWA_STAGE_CHUNKS = 1
XATTN_PARTS = 4
S5_SEG_CHUNKS = 16
MXU_DIM = 256
FF_CHUNK = 6 * MXU_DIM
VMEM_LIMIT = 56 * 1024 * 1024


def _const_spec(shape, index=None):
    idx = tuple(index) if index is not None else (0,) * len(shape)
    return pl.BlockSpec(shape, lambda *_: idx, pipeline_mode=pl.Buffered(1))


def _gain_spec(layer):
    return _const_spec((1, 1, D_MODEL), (layer, 0, 0))


def _params(n_axes):
    return pltpu.CompilerParams(dimension_semantics=("arbitrary",) * n_axes,
                                vmem_limit_bytes=VMEM_LIMIT)


def _rms(x, g):
    return x * lax.rsqrt(jnp.mean(x * x, axis=-1, keepdims=True) + EPS) * g


def _dot(a, b):
    return jnp.dot(a, b.astype(BF16), preferred_element_type=F32)


def _dot_t(a, b):
    return lax.dot_general(a, b, (((1,), (1,)), ((), ())), preferred_element_type=F32)


def _rope(t, cos, sina, sinb):
    n = t.shape[1]
    half = HEAD_DIM // 2
    return t * cos + pltpu.roll(t, n - half, 1) * sina + pltpu.roll(t, half, 1) * sinb


def _inproj_even_body(x_ref, g_ref, w_ref, u_ref, q_ref, k_ref, v_ref, cphi_ref, sphi_ref, *, start, period):
    o0 = SSM_WIDTH
    o1 = o0 + N_HEADS * HEAD_DIM
    o2 = o1 + KV_WIDTH
    tm = x_ref.shape[1]
    half = HEAD_DIM // 2
    lane = lax.broadcasted_iota(jnp.int32, (1, LANES), 1)
    inv = jnp.exp((lane & (half - 1)).astype(F32) * (-2.0 / HEAD_DIM * math.log(ROPE_THETA)))
    first = (lane & (HEAD_DIM - 1)) < half

    @pl.when((pl.program_id(0) == 0) & (pl.program_id(1) == 0))
    def _():
        row = lax.broadcasted_iota(jnp.int32, (tm, LANES), 0)
        phi = (row & (period - 1)).astype(F32) * inv
        cphi_ref[...] = jnp.cos(phi)
        sphi_ref[...] = jnp.sin(phi)

    theta = (start + ((pl.program_id(1) * tm) & (period - 1))).astype(F32) * inv
    c_th, s_th = jnp.cos(theta), jnp.sin(theta)
    nparts = 2 if tm % (2 * LANES) == 0 else 1
    parts = [slice(p * tm // nparts, (p + 1) * tm // nparts) for p in range(nparts)]
    projs = [_dot(_rms(x_ref[0, r, :], g_ref[0]).astype(BF16), w_ref[0]) for r in parts]
    for r, proj in zip(parts, projs):
        c_ph, s_ph = cphi_ref[r, :], sphi_ref[r, :]
        cos = c_th * c_ph - s_th * s_ph
        sin = s_th * c_ph + c_th * s_ph
        sa = jnp.where(first, -sin, 0.0)
        sb = jnp.where(first, 0.0, sin)
        u_ref[0, r, :] = proj[:, :o0]
        q = [_rope(proj[:, c:c + LANES], cos, sa, sb) for c in range(o0, o1, LANES)]
        q_ref[0, r, :] = (jnp.concatenate(q, axis=1) * HEAD_DIM ** -0.5).astype(BF16)
        k_ref[0, r, :] = _rope(proj[:, o1:o2], cos, sa, sb)
        v_ref[0, r, :] = proj[:, o2:]


def _inproj_even(x, g, w_in, start, period, layer_e, tm):
    bsz, seq, _ = x.shape
    tok = lambda w: pl.BlockSpec((1, tm, w), lambda b, i: (b, i, 0))
    return pl.pallas_call(
        functools.partial(_inproj_even_body, start=start, period=period),
        grid=(bsz, seq // tm),
        in_specs=[tok(D_MODEL), _gain_spec(2 * layer_e),
                  _const_spec((1, D_MODEL, EVEN_IN), (layer_e, 0, 0))],
        out_specs=[tok(SSM_WIDTH), tok(N_HEADS * HEAD_DIM), tok(KV_WIDTH), tok(KV_WIDTH)],
        out_shape=[jax.ShapeDtypeStruct((bsz, seq, SSM_WIDTH), F32),
                   jax.ShapeDtypeStruct((bsz, seq, N_HEADS * HEAD_DIM), BF16),
                   jax.ShapeDtypeStruct((bsz, seq, KV_WIDTH), F32),
                   jax.ShapeDtypeStruct((bsz, seq, KV_WIDTH), F32)],
        scratch_shapes=[pltpu.VMEM((tm, LANES), F32)] * 2,
        compiler_params=_params(2),
        name="inproj_even",
    )(x, g, w_in)


def _s5_prep_body(lr_ref, li_ref, ld_ref, bre_ref, bim_ref, cre_ref, cim_ref, wg_ref, wy_ref):
    ts = S5_TILE_STATES
    tw = S5_T * LANES

    def times(a_r, a_i):
        return lambda p: (p[0] * a_r - p[1] * a_i, p[0] * a_i + p[1] * a_r)

    lr, li = lr_ref[0], li_ref[0]
    dt = jnp.exp(ld_ref[0])
    mag = jnp.exp(lr * dt)
    ar, ai = mag * jnp.cos(li * dt), mag * jnp.sin(li * dt)
    mul_a = times(ar, ai)
    nrm = lr * lr + li * li
    cbr = ((ar - 1.0) * lr + ai * li) / nrm
    cbi = (ai * lr - (ar - 1.0) * li) / nrm
    b_re, b_im = bre_ref[0], bim_ref[0]
    bbr = cbr * b_re - cbi * b_im
    bbi = cbr * b_im + cbi * b_re
    p = (jnp.ones_like(ar), jnp.zeros_like(ar))
    for s in reversed(range(S5_T)):
        rows = slice(s * LANES, (s + 1) * LANES)
        wg_ref[0, rows, :ts] = (p[0] * bbr - p[1] * bbi).astype(BF16)
        wg_ref[0, rows, ts:] = (p[0] * bbi + p[1] * bbr).astype(BF16)
        for t in range(s):
            wy_ref[0, rows, t * LANES:(t + 1) * LANES] = jnp.zeros((LANES, LANES), BF16)
        p = mul_a(p)
    to_col = lambda row: jnp.transpose(jnp.broadcast_to(row, (LANES, ts)))
    mul_ac = times(to_col(ar), to_col(ai))
    c_re, c_im = cre_ref[0], cim_ref[0]
    bb = jnp.concatenate([bbr, bbi], axis=1).astype(BF16)
    pc = (jnp.ones_like(c_re), jnp.zeros_like(c_re))
    for k in range(S5_T + 1):
        blk = jnp.concatenate([pc[0] * c_re - pc[1] * c_im, -(pc[0] * c_im + pc[1] * c_re)], axis=0).astype(BF16)
        if k >= 1:
            wy_ref[0, tw:, (k - 1) * LANES:k * LANES] = blk
        if k < S5_T:
            lag = _dot(bb, blk).astype(BF16)
            for s in range(S5_T - k):
                t = s + k
                wy_ref[0, s * LANES:(s + 1) * LANES, t * LANES:(t + 1) * LANES] = lag
        pc = mul_ac(pc)


def _s5_prep(a_re, a_im, log_dt, b_re, b_im, c_re, c_im):
    gpt = LANES // SSM_GROUP
    eye = jnp.eye(gpt, dtype=bool)

    def rows(p):
        return p.reshape(S5_TILES, 1, S5_TILE_STATES)

    def bmat(b):
        bt = b.reshape(S5_TILES, gpt, SSM_STATE, SSM_GROUP).transpose(0, 1, 3, 2)
        full = jnp.where(eye[None, :, None, :, None], bt[:, :, :, None, :], 0.0)
        return full.reshape(S5_TILES, LANES, S5_TILE_STATES)

    def cmat(c):
        ct = c.reshape(S5_TILES, gpt, SSM_GROUP, SSM_STATE).transpose(0, 1, 3, 2)
        full = jnp.where(eye[None, :, None, :, None], ct[:, :, :, None, :], 0.0)
        return full.reshape(S5_TILES, S5_TILE_STATES, LANES)

    row_spec = pl.BlockSpec((1, 1, S5_TILE_STATES), lambda j: (j, 0, 0))
    b_spec = pl.BlockSpec((1, LANES, S5_TILE_STATES), lambda j: (j, 0, 0))
    c_spec = pl.BlockSpec((1, S5_TILE_STATES, LANES), lambda j: (j, 0, 0))
    tw = S5_T * LANES
    return pl.pallas_call(
        _s5_prep_body,
        grid=(S5_TILES,),
        in_specs=[row_spec] * 3 + [b_spec] * 2 + [c_spec] * 2,
        out_specs=[pl.BlockSpec((1, tw, 2 * S5_TILE_STATES), lambda j: (j, 0, 0)),
                   pl.BlockSpec((1, tw + 2 * S5_TILE_STATES, tw), lambda j: (j, 0, 0))],
        out_shape=[jax.ShapeDtypeStruct((S5_TILES, tw, 2 * S5_TILE_STATES), BF16),
                   jax.ShapeDtypeStruct((S5_TILES, tw + 2 * S5_TILE_STATES, tw), BF16)],
        compiler_params=_params(1),
        name="s5_prep",
    )(rows(a_re), rows(a_im), rows(log_dt), bmat(b_re), bmat(b_im), cmat(c_re), cmat(c_im))


def _s5_pow_body(ex_ref, lr_ref, li_ref, ld_ref, *o_refs):
    dt = jnp.exp(ld_ref[...])
    zr, zi = lr_ref[...] * dt, li_ref[...] * dt
    lo = 0
    for o_ref in o_refs:
        ex = ex_ref[lo:lo + o_ref.shape[0], :]
        lo += o_ref.shape[0]
        mag = jnp.exp(ex * zr)
        o_ref[:, :S5_STATES] = mag * jnp.cos(ex * zi)
        o_ref[:, S5_STATES:] = mag * jnp.sin(ex * zi)


def _s5_pow_table(groups, a_re, a_im, log_dt):
    sizes = [len(g) for g in groups]
    ex = jnp.asarray(np.asarray(sum(groups, []), np.float32)[:, None])
    flat = lambda p: p.reshape(1, S5_STATES)
    whole = lambda n, w: pl.BlockSpec((n, w), lambda i: (0, 0))
    return pl.pallas_call(
        _s5_pow_body,
        grid=(1,),
        in_specs=[whole(sum(sizes), 1)] + [whole(1, S5_STATES)] * 3,
        out_specs=[whole(n, 2 * S5_STATES) for n in sizes],
        out_shape=[jax.ShapeDtypeStruct((n, 2 * S5_STATES), F32) for n in sizes],
        compiler_params=_params(1),
        name="s5_pow_table",
    )(ex, flat(a_re), flat(a_im), flat(log_dt))


def _s5_body(u_ref, h0_ref, wg_ref, wy_ref, pow2_ref, powr_ref, d_ref, wglu_ref,
             y_ref, hout_ref, carry_ref, us_ref, ys_ref, *, ni, chained):
    ns = S5_STATES
    ts = S5_TILE_STATES
    segtok = ni * S5_T
    pitch = _s5_pitch(ni)

    @pl.when(pl.program_id(1) == 0)
    def _():
        carry_ref[...] = h0_ref[0]

    for j in range(S5_TILES):
        for s in range(SUBLANES):
            us_ref[j, s * pitch:s * pitch + segtok] = u_ref[0, s * segtok:(s + 1) * segtok, j * LANES:(j + 1) * LANES]

    def gather(j, t):
        return jnp.concatenate([us_ref[j, pl.ds(S5_T * i + t, SUBLANES, stride=pitch), :]
                                for i in range(ni)], axis=0)

    def fma(p_r, p_i, x_r, x_i, y_r, y_i):
        return y_r + p_r * x_r - p_i * x_i, y_i + p_r * x_i + p_i * x_r

    xs, g_re, g_im = [], [], []
    for j in range(S5_TILES):
        xj = jnp.concatenate([gather(j, s) for s in range(S5_T)], axis=1).astype(BF16)
        g = _dot(xj, wg_ref[j])
        xs.append(xj)
        g_re.append(g[:, :ts])
        g_im.append(g[:, ts:])
    g_re = jnp.concatenate(g_re, axis=1)
    g_im = jnp.concatenate(g_im, axis=1)
    slab = lambda a, i: a[i * SUBLANES:(i + 1) * SUBLANES]
    power = lambda ref, r: (ref[r:r + 1, :ns], ref[r:r + 1, ns:])
    a_r, a_i = power(powr_ref, 0)
    cin = carry_ref[...]
    h_r, h_i = cin[:, :ns], cin[:, ns:]
    prev = []
    if chained:
        local = [(slab(g_re, 0), slab(g_im, 0))]
        for i in range(1, ni):
            local.append(fma(a_r, a_i, *local[-1], slab(g_re, i), slab(g_im, i)))
        sub = lax.broadcasted_iota(jnp.int32, (SUBLANES, 1), 0)
        c_r = jnp.where(sub == 0, h_r, pltpu.roll(local[-1][0], 1, 0))
        c_i = jnp.where(sub == 0, h_i, pltpu.roll(local[-1][1], 1, 0))
        for m in range(3):
            k = 1 << m
            s_r = jnp.where(sub >= k, pltpu.roll(c_r, k, 0), 0.0)
            s_i = jnp.where(sub >= k, pltpu.roll(c_i, k, 0), 0.0)
            c_r, c_i = fma(*power(pow2_ref, m), s_r, s_i, c_r, c_i)
        h_r, h_i = c_r, c_i
        for i in range(ni):
            prev.append((h_r, h_i))
            h_r, h_i = fma(*power(powr_ref, i), c_r, c_i, *local[i])
        last = jnp.concatenate([h_r[SUBLANES - 1:], h_i[SUBLANES - 1:]], axis=1)
    else:
        for i in range(ni):
            prev.append((h_r, h_i))
            h_r, h_i = fma(a_r, a_i, h_r, h_i, slab(g_re, i), slab(g_im, i))
        last = jnp.concatenate([h_r, h_i], axis=1)
    carry_ref[...] = last
    hout_ref[0] = last
    prev_re = jnp.concatenate([p[0] for p in prev], axis=0)
    prev_im = jnp.concatenate([p[1] for p in prev], axis=0)
    y_tiles = []
    for j in range(S5_TILES):
        lhs = jnp.concatenate([xs[j], prev_re[:, j * ts:(j + 1) * ts].astype(BF16),
                               prev_im[:, j * ts:(j + 1) * ts].astype(BF16)], axis=1)
        y_tiles.append(_dot(lhs, wy_ref[j]))
    d = d_ref[...]
    wglu = wglu_ref[0]
    def gelu_in(t):
        ut = jnp.concatenate([gather(j, t) for j in range(S5_TILES)], axis=1)
        yt = jnp.concatenate([y_tiles[j][:, t * LANES:(t + 1) * LANES] for j in range(S5_TILES)], axis=1)
        return jax.nn.gelu(yt + d * ut)

    def scatter(t, o):
        for j in range(S5_TILES):
            for i in range(ni):
                ys_ref[j, pl.ds(S5_T * i + t, SUBLANES, stride=pitch), :] = slab(o, i)[:, j * LANES:(j + 1) * LANES]

    g_q, z_q = {}, {}
    for step in range(S5_T + 2):
        if step < S5_T:
            g_q[step] = gelu_in(step)
        if 0 <= step - 1 < S5_T:
            z_q[step - 1] = _dot(g_q[step - 1].astype(BF16), wglu)
        if 0 <= step - 2 < S5_T:
            scatter(step - 2, g_q.pop(step - 2) * jax.nn.sigmoid(z_q.pop(step - 2)))
    for j in range(S5_TILES):
        for s in range(SUBLANES):
            y_ref[0, s * segtok:(s + 1) * segtok, j * LANES:(j + 1) * LANES] = (
                ys_ref[j, s * pitch:s * pitch + segtok].astype(BF16))


def _s5_pitch(ni):
    segtok = ni * S5_T
    return segtok + SUBLANES if (segtok // SUBLANES) % 2 == 0 else segtok


def _s5_mixer(u, h0, wg, wy, pow2, powr, d, w_glu, layer_e, ni, chained):
    nb, ntok, width = u.shape
    tb = SUBLANES * ni * S5_T
    hrows = h0.shape[1]
    blk = pl.BlockSpec((1, tb, width), lambda b, i: (b, i, 0))
    state = pl.BlockSpec((1, hrows, 2 * S5_STATES), lambda b, i: (b, 0, 0))
    return pl.pallas_call(
        functools.partial(_s5_body, ni=ni, chained=chained),
        grid=(nb, ntok // tb),
        in_specs=[blk, state, _const_spec(wg.shape), _const_spec(wy.shape),
                  _const_spec(pow2.shape), _const_spec(powr.shape), _const_spec((1, SSM_WIDTH)),
                  _const_spec((1, SSM_WIDTH, SSM_WIDTH), (layer_e, 0, 0))],
        out_specs=[blk, state],
        out_shape=[jax.ShapeDtypeStruct((nb, ntok, width), BF16),
                   jax.ShapeDtypeStruct((nb, hrows, 2 * S5_STATES), F32)],
        scratch_shapes=[pltpu.VMEM((hrows, 2 * S5_STATES), F32)]
        + [pltpu.VMEM((S5_TILES, SUBLANES * _s5_pitch(ni), LANES), F32)] * 2,
        compiler_params=_params(2),
        name="s5_mixer",
    )(u, h0, wg, wy, pow2, powr, d, w_glu)


def _swa_body(sink_ref, q_ref, kc_ref, vc_ref, kp_ref, vp_ref, pk_ref, pv_ref, o_ref, *, tq, qg, start):
    first = pl.program_id(1) == 0
    k_prev = jnp.where(first, pk_ref[0], kp_ref[0])
    v_prev = jnp.where(first, pv_ref[0], vp_ref[0])
    k_all = jnp.concatenate([k_prev, kc_ref[0]], axis=0)
    v_all = jnp.concatenate([v_prev, vc_ref[0]], axis=0)
    nk = WINDOW + qg
    lane = lax.broadcasted_iota(jnp.int32, (1, KV_WIDTH), 1)
    slot = lax.broadcasted_iota(jnp.int32, (1, nk), 1)
    qrow = lax.broadcasted_iota(jnp.int32, (2 * qg, 1), 0)
    lower = lane < HEAD_DIM
    placed = []
    for kh in range(N_KV_HEADS):
        own = lower if kh == 0 else jnp.logical_not(lower)
        kz, vz = jnp.where(own, k_all, 0.0), jnp.where(own, v_all, 1.0)
        kr, vr = pltpu.roll(kz, HEAD_DIM, 1), pltpu.roll(vz, HEAD_DIM, 1)
        k_lo, k_hi = (kz, kr) if kh == 0 else (kr, kz)
        v_lo, v_hi = (vz, vr) if kh == 0 else (vr, vz)
        placed.append(((k_lo.astype(BF16), v_lo.astype(BF16)), (k_hi.astype(BF16), v_hi.astype(BF16))))
    combos = [(kh, par) for kh in range(N_KV_HEADS) for par in range(2)]

    def scores(c):
        qc = q_ref[0, c * qg:(c + 1) * qg, :]
        out = []
        for kh, par in combos:
            base = kh * GQ * HEAD_DIM
            qs = jnp.concatenate([qc[:, base:base + LANES], qc[:, base + LANES:base + 2 * LANES]], axis=0)
            out.append(_dot_t(qs, placed[kh][par][0][c * qg:c * qg + nk]))
        return out

    def softmax(c, ss):
        n_bad = jnp.where(first, WINDOW - start - c * qg, 0) if c * qg < WINDOW - start else None
        probs, sinks = [], []
        for (kh, par), s in zip(combos, ss):
            if n_bad is not None:
                s = jnp.where(slot >= n_bad, s, NEG)
            sink = jnp.where(qrow < qg, sink_ref[kh * GQ + par], sink_ref[kh * GQ + 2 + par])
            m = jnp.maximum(jnp.max(s, axis=1, keepdims=True), sink)
            probs.append(jnp.exp(s - m).astype(BF16))
            sinks.append(jnp.exp(sink - m))
        return probs, sinks

    def finish(c, probs, sinks):
        pvs = [_dot(p, placed[kh][par][1][c * qg:c * qg + nk]) for (kh, par), p in zip(combos, probs)]
        outs = []
        for kh in range(N_KV_HEADS):
            pv_e, pv_o = pvs[2 * kh], pvs[2 * kh + 1]
            num = jnp.where(lower, pv_e, pv_o)
            den = pltpu.roll(jnp.where(lower, pv_o, pv_e), HEAD_DIM, 1) + jnp.where(lower, sinks[2 * kh], sinks[2 * kh + 1])
            acc = num / den
            outs += [acc[:qg], acc[qg:]]
        o_ref[0, c * qg:(c + 1) * qg, :] = jnp.concatenate(outs, axis=1).astype(BF16)

    nchunk = tq // qg
    per = min(SWA_STAGE_CHUNKS, nchunk)
    groups = [range(g, min(g + per, nchunk)) for g in range(0, nchunk, per)]
    s_q, p_q = {}, {}
    for step in range(len(groups) + 2):
        if step < len(groups):
            s_q[step] = [scores(c) for c in groups[step]]
        if 0 <= step - 1 < len(groups):
            p_q[step - 1] = [softmax(c, ss) for c, ss in zip(groups[step - 1], s_q.pop(step - 1))]
        if 0 <= step - 2 < len(groups):
            for c, (probs, sinks) in zip(groups[step - 2], p_q.pop(step - 2)):
                finish(c, probs, sinks)


def _swa(q, k, v, past_k, past_v, sinks, start, tq, qg):
    bsz, seq, _ = q.shape
    ntiles = seq // tq
    tok = lambda w: pl.BlockSpec((1, tq, w), lambda b, i: (b, i, 0))
    past = pl.BlockSpec((1, WINDOW, KV_WIDTH), lambda b, i: (b, 0, 0))
    if ntiles > 1:
        per = tq // WINDOW
        prev = pl.BlockSpec((1, WINDOW, KV_WIDTH), lambda b, i: (b, jnp.maximum(i * per - 1, 0), 0))
        k_prev, v_prev = k, v
    else:
        prev, k_prev, v_prev = past, past_k, past_v
    return pl.pallas_call(
        functools.partial(_swa_body, tq=tq, qg=qg, start=start),
        grid=(bsz, ntiles),
        in_specs=[pl.BlockSpec(memory_space=pltpu.SMEM), tok(N_HEADS * HEAD_DIM), tok(KV_WIDTH), tok(KV_WIDTH),
                  prev, prev, past, past],
        out_specs=tok(N_HEADS * HEAD_DIM),
        out_shape=jax.ShapeDtypeStruct((bsz, seq, N_HEADS * HEAD_DIM), BF16),
        compiler_params=_params(2),
        name="swa",
    )(sinks, q, k, v, k_prev, v_prev, past_k, past_v)


def _xattn(x1, g, wq, mk_ref, mv_ref, wo, nseq, o_ref):
    h = _rms(x1, g).astype(BF16)
    q = (_dot(h, wq) * X_HEAD_DIM ** -0.5).astype(BF16)
    nparts = nseq if nseq > 1 else XATTN_PARTS
    prows = x1.shape[0] // nparts
    heads = [slice(hd * X_HEAD_DIM, (hd + 1) * X_HEAD_DIM) for hd in range(X_HEADS)]

    def scores(p):
        r, b = slice(p * prows, (p + 1) * prows), (p if nseq > 1 else 0)
        return [_dot_t(q[r, c], mk_ref[0, b, :, c]) for c in heads]

    def softmax(ss):
        out = []
        for s in ss:
            e = jnp.exp(s - jnp.max(s, axis=1, keepdims=True))
            out.append((e.astype(BF16), jnp.sum(e, axis=1, keepdims=True)))
        return out

    def values(p, pd):
        b = p if nseq > 1 else 0
        return jnp.concatenate([_dot(e, mv_ref[0, b, :, c]) / den for c, (e, den) in zip(heads, pd)],
                               axis=1).astype(BF16)

    def finish(p, pd):
        r = slice(p * prows, (p + 1) * prows)
        o_ref[0, r, :] = x1[r] + _dot(values(p, pd), wo)

    if nseq > 1:
        pds = [softmax(ss) for ss in [scores(p) for p in range(nparts)]]
        o = jnp.concatenate([values(p, pd) for p, pd in enumerate(pds)], axis=0)
        o_ref[0] = x1 + _dot(o, wo)
        return
    s_q, p_q = {}, {}
    for step in range(nparts + 2):
        if step < nparts:
            s_q[step] = scores(step)
        if 0 <= step - 1 < nparts:
            p_q[step - 1] = softmax(s_q.pop(step - 1))
        if 0 <= step - 2 < nparts:
            finish(step - 2, p_q.pop(step - 2))


def _conv3(cur, carry, w):
    n = cur.shape[0]
    row = lax.broadcasted_iota(jnp.int32, (n, 1), 0)
    nc = carry.shape[0]
    c1, c2 = carry[nc - 1:nc], carry[nc - 2:nc - 1]
    m1 = jnp.where(row == 0, c1, pltpu.roll(cur, 1, 0))
    m2 = jnp.where(row == 0, c2, jnp.where(row == 1, c1, pltpu.roll(cur, 2, 0)))
    return w[0:1] * m2 + w[1:2] * m1 + w[2:3] * cur


def _conv3_seqs(cur, carries, w):
    nseq = len(carries)
    if nseq == 1:
        return _conv3(cur, carries[0], w)
    rows = cur.shape[0] // nseq
    body = _conv3(cur, carries[0], w)
    pieces = []
    for b in range(nseq):
        pieces.append(_conv3(cur[b * rows:b * rows + SUBLANES], carries[b], w))
        pieces.append(body[b * rows + SUBLANES:(b + 1) * rows])
    return jnp.concatenate(pieces, axis=0)


def _store_last_rows(st_ref, cols, cur, nseq):
    rows = cur.shape[0] // nseq
    for b in range(nseq):
        st_ref[b, :, cols] = cur[(b + 1) * rows - (CONV_WIDTH - 1):(b + 1) * rows]


def _mix_even_xattn_body(x_ref, ya_ref, yb_ref, wout_ref, g_ref, wq_ref, mk_ref, mv_ref, wo_ref, o_ref, *, nseq):
    x1 = x_ref[0] + _dot(jnp.concatenate([ya_ref[0], yb_ref[0]], axis=1), wout_ref[0])
    _xattn(x1, g_ref[0], wq_ref[0], mk_ref, mv_ref, wo_ref[0], nseq, o_ref)


def _mix_even_xattn(x, ya, yb, w_out, g, wq, mem_k, mem_v, wo, layer, layer_e, tm, nseq):
    bsz, seq, _ = x.shape
    tok = lambda w: pl.BlockSpec((1, tm, w), lambda b, i: (b, i, 0))
    mem = pl.BlockSpec((1, nseq, N_MEM, D_MODEL), lambda b, i: (layer, b, 0, 0))
    half = D_MODEL // 2
    sq = lambda: _const_spec((1, D_MODEL, D_MODEL), (layer, 0, 0))
    return pl.pallas_call(
        functools.partial(_mix_even_xattn_body, nseq=nseq),
        grid=(bsz, seq // tm),
        in_specs=[tok(D_MODEL), tok(half), tok(half),
                  _const_spec((1, D_MODEL, D_MODEL), (layer_e, 0, 0)),
                  _gain_spec(layer), sq(), mem, mem, sq()],
        out_specs=tok(D_MODEL),
        out_shape=jax.ShapeDtypeStruct(x.shape, F32),
        compiler_params=_params(2),
        name="mix_even_xattn",
    )(x, ya, yb, w_out, g, wq, mem_k, mem_v, wo)


def _mix_odd_xattn_body(x_ref, gm_ref, win_ref, cw_ref, wout_ref, prev_ref, g_ref, wq_ref, mk_ref, mv_ref, wo_ref,
                        o_ref, st_ref, carry_ref, *, tm, nseq):
    @pl.when(pl.program_id(1) == 0)
    def _():
        carry_ref[SUBLANES - (CONV_WIDTH - 1):, :] = prev_ref[0, 0]

    x = x_ref[0]
    proj = _dot(_rms(x, gm_ref[0]).astype(BF16), win_ref[0])
    gate_b, gate_c, z = proj[:, :D_MODEL], proj[:, D_MODEL:2 * D_MODEL], proj[:, 2 * D_MODEL:]
    cz = gate_c * z
    keep = CONV_WIDTH - 1
    carries = [carry_ref[SUBLANES - keep:, :]] if nseq == 1 else [prev_ref[0, b] for b in range(nseq)]
    z_conv = _conv3_seqs(cz, carries, cw_ref[0])
    carry_ref[...] = cz[tm - SUBLANES:]
    _store_last_rows(st_ref, slice(None), cz, nseq)
    x1 = x + _dot((gate_b * z_conv).astype(BF16), wout_ref[0])
    _xattn(x1, g_ref[0], wq_ref[0], mk_ref, mv_ref, wo_ref[0], nseq, o_ref)


def _mix_odd_xattn(x, gm, w_in, conv_w, w_out, prev, g, wq, mem_k, mem_v, wo, layer, layer_o, tm, nseq):
    bsz, seq, _ = x.shape
    tok = pl.BlockSpec((1, tm, D_MODEL), lambda b, i: (b, i, 0))
    mem = pl.BlockSpec((1, nseq, N_MEM, D_MODEL), lambda b, i: (layer, b, 0, 0))
    hist = pl.BlockSpec((1, nseq, CONV_WIDTH - 1, D_MODEL), lambda b, i: (layer_o, b, 0, 0))
    st = pl.BlockSpec((nseq, CONV_WIDTH - 1, D_MODEL), lambda b, i: (b, 0, 0))
    sq = lambda l: _const_spec((1, D_MODEL, D_MODEL), (l, 0, 0))
    return pl.pallas_call(
        functools.partial(_mix_odd_xattn_body, tm=tm, nseq=nseq),
        grid=(bsz, seq // tm),
        in_specs=[tok, _gain_spec(layer), _const_spec((1, D_MODEL, 3 * D_MODEL), (layer_o, 0, 0)),
                  _const_spec((1, CONV_WIDTH, D_MODEL), (layer_o, 0, 0)), sq(layer_o), hist,
                  _gain_spec(layer), sq(layer), mem, mem, sq(layer)],
        out_specs=[tok, st],
        out_shape=[jax.ShapeDtypeStruct(x.shape, F32),
                   jax.ShapeDtypeStruct((bsz * nseq, CONV_WIDTH - 1, D_MODEL), F32)],
        scratch_shapes=[pltpu.VMEM((SUBLANES, D_MODEL), F32)],
        compiler_params=_params(2),
        name="mix_odd_xattn",
    )(x, gm, w_in, conv_w, w_out, prev, g, wq, mem_k, mem_v, wo)


def _ffn_body(x_ref, g_ref, wup_ref, cw_ref, wdn_ref, prev_ref, gfin_ref, o_ref, st_ref, carry_ref,
              *, tm, final, nseq):
    @pl.when(pl.program_id(1) == 0)
    def _():
        carry_ref[SUBLANES - (CONV_WIDTH - 1):, :] = prev_ref[0, 0]

    x = x_ref[0]
    h = _rms(x, g_ref[0]).astype(BF16)
    chunks = [slice(lo, min(lo + FF_CHUNK, D_FF)) for lo in range(0, D_FF, FF_CHUNK)]

    def up(cols):
        return _dot(h, wup_ref[0, :, cols]), _dot(h, wup_ref[0, :, D_FF + cols.start:D_FF + cols.stop])

    acc = x
    nxt = up(chunks[0])
    for n, cols in enumerate(chunks):
        (gate, val), nxt = nxt, (up(chunks[n + 1]) if n + 1 < len(chunks) else None)
        carries = ([carry_ref[SUBLANES - (CONV_WIDTH - 1):, cols]] if nseq == 1
                   else [prev_ref[0, b, :, cols] for b in range(nseq)])
        gate_c = _conv3_seqs(gate, carries, cw_ref[0, :, cols])
        carry_ref[:, cols] = gate[tm - SUBLANES:]
        _store_last_rows(st_ref, cols, gate, nseq)
        act = (gate_c * jax.nn.sigmoid(gate_c) * val).astype(BF16)
        if n + 1 < len(chunks):
            acc = acc + _dot(act, wdn_ref[0, cols, :])
        else:
            half = tm // 2
            for rows in (slice(0, half), slice(half, tm)):
                out = acc[rows] + _dot(act[rows], wdn_ref[0, cols, :])
                o_ref[0, rows, :] = _rms(out, gfin_ref[0]) if final else out


def _ffn(x, g, w_up, conv_w, w_down, prev, g_final, layer, tm, final, nseq):
    bsz, seq, _ = x.shape
    tok = pl.BlockSpec((1, tm, D_MODEL), lambda b, i: (b, i, 0))
    hist = pl.BlockSpec((1, nseq, CONV_WIDTH - 1, D_FF), lambda b, i: (layer, b, 0, 0))
    st = pl.BlockSpec((nseq, CONV_WIDTH - 1, D_FF), lambda b, i: (b, 0, 0))
    return pl.pallas_call(
        functools.partial(_ffn_body, tm=tm, final=final, nseq=nseq),
        grid=(bsz, seq // tm),
        in_specs=[tok, _gain_spec(layer), _const_spec((1, D_MODEL, 2 * D_FF), (layer, 0, 0)),
                  _const_spec((1, CONV_WIDTH, D_FF), (layer, 0, 0)), _const_spec((1, D_FF, D_MODEL), (layer, 0, 0)),
                  hist, _gain_spec(0)],
        out_specs=[tok, st],
        out_shape=[jax.ShapeDtypeStruct(x.shape, F32),
                   jax.ShapeDtypeStruct((bsz * nseq, CONV_WIDTH - 1, D_FF), F32)],
        scratch_shapes=[pltpu.VMEM((SUBLANES, D_FF), F32)],
        compiler_params=_params(2),
        name="conv_ffn",
    )(x, g, w_up, conv_w, w_down, prev, g_final)


def _mem_flat_copies(k_hbm, v_hbm, kbuf, vbuf, sem, step, slot, nbatch):
    lyr, b = step // nbatch, step % nbatch
    return [pltpu.make_async_copy(src.at[lyr, b, :, hd, :], buf.at[slot, hd], sem.at[slot, i, hd])
            for i, (src, buf) in enumerate(((k_hbm, kbuf), (v_hbm, vbuf))) for hd in range(X_HEADS)]


def _mem_flat_body(k_hbm, v_hbm, ko_ref, vo_ref, kbuf, vbuf, sem, *, nbatch):
    s, n = pl.program_id(0), pl.num_programs(0)
    copies = functools.partial(_mem_flat_copies, k_hbm, v_hbm, kbuf, vbuf, sem, nbatch=nbatch)

    @pl.when(s == 0)
    def _():
        for c in copies(step=s, slot=0):
            c.start()

    @pl.when(s + 1 < n)
    def _():
        for c in copies(step=s + 1, slot=(s + 1) % 2):
            c.start()

    slot = s % 2
    for c in copies(step=s, slot=slot):
        c.wait()
    for hd in range(X_HEADS):
        cols = slice(hd * X_HEAD_DIM, (hd + 1) * X_HEAD_DIM)
        ko_ref[0, 0, :, cols] = kbuf[slot, hd].astype(BF16)
        vo_ref[0, 0, :, cols] = vbuf[slot, hd].astype(BF16)


def _mem_flat(mem_k, mem_v):
    depth, nbatch = mem_k.shape[:2]
    flat = pl.BlockSpec((1, 1, N_MEM, D_MODEL), lambda s: (s // nbatch, s % nbatch, 0, 0))
    return pl.pallas_call(
        functools.partial(_mem_flat_body, nbatch=nbatch),
        grid=(depth * nbatch,),
        in_specs=[pl.BlockSpec(memory_space=pl.ANY)] * 2,
        out_specs=[flat, flat],
        out_shape=[jax.ShapeDtypeStruct((depth, nbatch, N_MEM, D_MODEL), BF16)] * 2,
        scratch_shapes=[pltpu.VMEM((2, X_HEADS, N_MEM, X_HEAD_DIM), F32)] * 2
        + [pltpu.SemaphoreType.DMA((2, 2, X_HEADS))],
        compiler_params=_params(1),
        name="mem_flat",
    )(mem_k, mem_v)


def _mem_kv_body(m_ref, w_ref, k_ref, v_ref, kf_ref, vf_ref):
    kv = _dot(m_ref[0].astype(BF16), w_ref[0])
    kf_ref[0, 0] = kv[:, :D_MODEL].astype(BF16)
    vf_ref[0, 0] = kv[:, D_MODEL:].astype(BF16)
    for hd in range(X_HEADS):
        k_ref[0, 0, :, hd, :] = kv[:, hd * X_HEAD_DIM:(hd + 1) * X_HEAD_DIM]
        v_ref[0, 0, :, hd, :] = kv[:, D_MODEL + hd * X_HEAD_DIM:D_MODEL + (hd + 1) * X_HEAD_DIM]


def _mem_kv(mem, w_kv):
    bsz = mem.shape[0]
    heads = pl.BlockSpec((1, 1, N_MEM, X_HEADS, X_HEAD_DIM), lambda l, b: (l, b, 0, 0, 0))
    flat = pl.BlockSpec((1, 1, N_MEM, D_MODEL), lambda l, b: (l, b, 0, 0))
    return pl.pallas_call(
        _mem_kv_body,
        grid=(DEPTH, bsz),
        in_specs=[pl.BlockSpec((1, N_MEM, D_MODEL), lambda l, b: (b, 0, 0)),
                  pl.BlockSpec((1, D_MODEL, 2 * D_MODEL), lambda l, b: (l, 0, 0))],
        out_specs=[heads, heads, flat, flat],
        out_shape=[jax.ShapeDtypeStruct((DEPTH, bsz, N_MEM, X_HEADS, X_HEAD_DIM), F32)] * 2
        + [jax.ShapeDtypeStruct((DEPTH, bsz, N_MEM, D_MODEL), BF16)] * 2,
        compiler_params=_params(2),
        name="mem_kv",
    )(mem, w_kv)


def _trunk(x, start, past_k, past_v, ssm_re, ssm_im, conv_mix_prev, conv_ffn_prev, mem_k, mem_v, w, s5, cfg):
    bsz, seq, _ = x.shape
    tq, qg, nseq = cfg["tq"], cfg["qg"], cfg["nseq"]
    tm_in, tm_even, tm_odd, tm_ffn = cfg["tm_in"], cfg["tm_even"], cfg["tm_odd"], cfg["tm_ffn"]
    ni, chained = cfg["ni"], cfg["chained"]
    ngrp, grows = bsz // nseq, nseq * seq
    grouped = lambda t: t.reshape(ngrp, grows, t.shape[-1])
    per_seq = lambda t: t.reshape(bsz, seq, t.shape[-1])
    u, q, k, v = _inproj_even(grouped(x), w["norm_mix"], w["w_in_even"], start, seq, 0, tm_in)
    h0 = jnp.concatenate([ssm_re.reshape(bsz, S5_STATES), ssm_im.reshape(bsz, S5_STATES)], axis=1)
    if chained:
        ya, h_all = _s5_mixer(u, h0[:, None, :], s5["wg"], s5["wy"], cfg["pow2"], cfg["powr"], w["ssm_d"],
                              w["w_glu"], 0, ni, True)
        h_last = h_all[:, 0]
    else:
        assert ngrp == 1 and bsz == SUBLANES and seq == ni * S5_T
        ya, h_all = _s5_mixer(u, h0[None], s5["wg"], s5["wy"], cfg["pow2"], cfg["powr"], w["ssm_d"],
                              w["w_glu"], 0, ni, False)
        h_last = h_all[0]
    new_re = h_last[:, :S5_STATES].reshape(1, bsz, SSM_GROUPS, SSM_STATE)
    new_im = h_last[:, S5_STATES:].reshape(1, bsz, SSM_GROUPS, SSM_STATE)
    k, v = per_seq(k), per_seq(v)
    yb = _swa(per_seq(q), k, v, past_k.reshape(bsz, WINDOW, KV_WIDTH), past_v.reshape(bsz, WINDOW, KV_WIDTH),
              w["attn_sinks"], start, tq, qg)
    xg = _mix_even_xattn(grouped(x), ya, grouped(yb), w["w_out_even"], w["norm_xattn"], w["xattn_wq"],
                         mem_k, mem_v, w["xattn_wo"], 0, 0, tm_even, nseq)
    xg, cf0 = _ffn(xg, w["norm_ffn"], w["ffn_w_up"], w["ffn_conv_w"], w["ffn_w_down"], conv_ffn_prev,
                   w["norm_final"], 0, tm_ffn, False, nseq)
    xg, cm = _mix_odd_xattn(xg, w["norm_mix"], w["w_in_odd"], w["conv_mix_w"], w["w_out_odd"],
                            conv_mix_prev, w["norm_xattn"], w["xattn_wq"], mem_k, mem_v,
                            w["xattn_wo"], 1, 0, tm_odd, nseq)
    y, cf1 = _ffn(xg, w["norm_ffn"], w["ffn_w_up"], w["ffn_conv_w"], w["ffn_w_down"], conv_ffn_prev,
                  w["norm_final"], 1, tm_ffn, True, nseq)
    keep = min(seq, WINDOW)
    new_k = k[:, seq - keep:].reshape(1, bsz, keep, N_KV_HEADS, HEAD_DIM)
    new_v = v[:, seq - keep:].reshape(1, bsz, keep, N_KV_HEADS, HEAD_DIM)
    return per_seq(y), new_k, new_v, new_re, new_im, cm[None], jnp.stack([cf0, cf1])


def kernel(x_prompt, x_sample, mem_prompt, cache_win_k, cache_win_v, state_ssm_re, state_ssm_im, state_conv_mix, state_conv_ffn, cache_mem_k, cache_mem_v, norm_mix, norm_xattn, norm_ffn, norm_final, w_in_even, w_out_even, ssm_a_re, ssm_a_im, ssm_log_dt, ssm_b_re, ssm_b_im, ssm_c_re, ssm_c_im, ssm_d, w_glu, attn_sinks, w_in_odd, conv_mix_w, w_out_odd, xattn_wq, xattn_wkv, xattn_wo, ffn_w_up, ffn_conv_w, ffn_w_down):
    bp, seq_p, _ = x_prompt.shape
    bs, seq_s, _ = x_sample.shape
    gains = lambda g: g.reshape(-1, 1, D_MODEL)
    w = dict(norm_mix=gains(norm_mix), norm_xattn=gains(norm_xattn), norm_ffn=gains(norm_ffn),
             norm_final=gains(norm_final),
             w_in_even=w_in_even, w_out_even=w_out_even,
             ssm_d=ssm_d.reshape(1, SSM_WIDTH), w_glu=w_glu.astype(BF16), attn_sinks=attn_sinks.reshape(N_HEADS),
             w_in_odd=w_in_odd, conv_mix_w=conv_mix_w, w_out_odd=w_out_odd,
             xattn_wq=xattn_wq, xattn_wo=xattn_wo,
             ffn_w_up=ffn_w_up.astype(BF16), ffn_conv_w=ffn_conv_w, ffn_w_down=ffn_w_down.astype(BF16))

    gp = (SSM_GROUPS, SSM_STATE)
    a_re, a_im, log_dt = ssm_a_re.reshape(gp), ssm_a_im.reshape(gp), ssm_log_dt.reshape(gp)
    wg, wy = _s5_prep(a_re, a_im, log_dt, ssm_b_re.reshape(gp + (SSM_GROUP,)), ssm_b_im.reshape(gp + (SSM_GROUP,)),
                      ssm_c_re.reshape(SSM_GROUPS, SSM_GROUP, SSM_STATE), ssm_c_im.reshape(SSM_GROUPS, SSM_GROUP, SSM_STATE))
    s5 = dict(wg=wg, wy=wy)
    ni_p, ni_s = S5_SEG_CHUNKS, seq_s // S5_T
    padded = lambda e: e + [1.0] * (-len(e) % SUBLANES)
    pow2, powr_p, powr_s = _s5_pow_table(
        [padded([float(S5_T * ni_p << m) for m in range(3)]),
         padded([float(S5_T * (i + 1)) for i in range(ni_p)]),
         padded([float(S5_T * (i + 1)) for i in range(ni_s)])], a_re, a_im, log_dt)
    cfg_p = dict(tm_in=1024, tm_even=1024, tm_odd=1024, tm_ffn=1024, tq=512, qg=CHUNK, nseq=1, ni=ni_p, chained=True,
                 pow2=pow2, powr=powr_p)
    rows_s = bs * seq_s
    cfg_s = dict(tm_in=rows_s, tm_even=rows_s, tm_odd=rows_s, tm_ffn=rows_s, tq=seq_s, qg=seq_s, nseq=bs, ni=ni_s,
                 chained=False, pow2=pow2, powr=powr_s)

    mem_k_p, mem_v_p, mem_kf_p, mem_vf_p = _mem_kv(mem_prompt, xattn_wkv.astype(BF16))
    cw = CONV_WIDTH - 1
    zk = jnp.zeros((1, bp, WINDOW, N_KV_HEADS, HEAD_DIM), F32)
    zs = jnp.zeros((1, bp, SSM_GROUPS, SSM_STATE), F32)
    zcm = jnp.zeros((1, bp, cw, D_MODEL), F32)
    zcf = jnp.zeros((DEPTH, bp, cw, D_FF), F32)
    y_p, k_p, v_p, re_p, im_p, cm_p, cf_p = _trunk(x_prompt, 0, zk, zk, zs, zs, zcm, zcf, mem_kf_p, mem_vf_p, w, s5, cfg_p)
    mem_kf_s, mem_vf_s = _mem_flat(cache_mem_k, cache_mem_v)
    y_s, k_s, v_s, re_s, im_s, cm_s, cf_s = _trunk(x_sample, PAST_LEN, cache_win_k, cache_win_v, state_ssm_re,
                                                   state_ssm_im, state_conv_mix, state_conv_ffn, mem_kf_s,
                                                   mem_vf_s, w, s5, cfg_s)
    return (y_p, y_s, k_p, v_p, re_p, im_p, cm_p, cf_p, mem_k_p, mem_v_p, k_s, v_s, re_s, im_s, cm_s, cf_s)
```

```python
import functools
import math

import numpy as np
import jax
import jax.numpy as jnp
from jax import lax
from jax.experimental import pallas as pl
from jax.experimental.pallas import tpu as pltpu

F32 = jnp.float32
BF16 = jnp.bfloat16

D_MODEL = 1024
DEPTH = 2
PAST_LEN = 2048
CHUNK = 64
SSM_WIDTH = 512
SSM_GROUP = 16
SSM_GROUPS = 32
SSM_STATE = 64
HEAD_DIM = 64
N_HEADS = 8
N_KV_HEADS = 2
GQ = N_HEADS // N_KV_HEADS
WINDOW = 128
ROPE_THETA = 10000.0
KV_WIDTH = N_KV_HEADS * HEAD_DIM
EVEN_IN = SSM_WIDTH + N_HEADS * HEAD_DIM + 2 * KV_WIDTH
CONV_WIDTH = 3
N_MEM = 256
X_HEADS = 4
X_HEAD_DIM = 256
D_FF = 2816
EPS = 1e-6
NEG = -1e30

LANES = 128
SUBLANES = 8
S5_T = 8
S5_TILES = SSM_WIDTH // LANES
S5_TILE_STATES = (LANES // SSM_GROUP) * SSM_STATE
S5_STATES = SSM_GROUPS * SSM_STATE
SWA_STAGE_CHUNKS = 1
XATTN_PARTS = 4
S5_SEG_CHUNKS = 16
MXU_DIM = 256
FF_CHUNK = 6 * MXU_DIM
VMEM_LIMIT = 56 * 1024 * 1024


def _const_spec(shape, index=None):
    idx = tuple(index) if index is not None else (0,) * len(shape)
    return pl.BlockSpec(shape, lambda *_: idx, pipeline_mode=pl.Buffered(1))


def _gain_spec(layer):
    return _const_spec((1, 1, D_MODEL), (layer, 0, 0))


def _params(n_axes):
    return pltpu.CompilerParams(dimension_semantics=("arbitrary",) * n_axes,
                                vmem_limit_bytes=VMEM_LIMIT)


def _rms(x, g):
    return x * lax.rsqrt(jnp.mean(x * x, axis=-1, keepdims=True) + EPS) * g


def _dot(a, b):
    return jnp.dot(a, b.astype(BF16), preferred_element_type=F32)


def _dot_t(a, b):
    return lax.dot_general(a, b, (((1,), (1,)), ((), ())), preferred_element_type=F32)


def _rope(t, cos, sina, sinb):
    n = t.shape[1]
    half = HEAD_DIM // 2
    return t * cos + pltpu.roll(t, n - half, 1) * sina + pltpu.roll(t, half, 1) * sinb


def _inproj_even_body(x_ref, g_ref, w_ref, u_ref, q_ref, k_ref, v_ref, cphi_ref, sphi_ref, *, start, period):
    o0 = SSM_WIDTH
    o1 = o0 + N_HEADS * HEAD_DIM
    o2 = o1 + KV_WIDTH
    tm = x_ref.shape[1]
    half = HEAD_DIM // 2
    lane = lax.broadcasted_iota(jnp.int32, (1, LANES), 1)
    inv = jnp.exp((lane & (half - 1)).astype(F32) * (-2.0 / HEAD_DIM * math.log(ROPE_THETA)))
    first = (lane & (HEAD_DIM - 1)) < half

    @pl.when((pl.program_id(0) == 0) & (pl.program_id(1) == 0))
    def _():
        row = lax.broadcasted_iota(jnp.int32, (tm, LANES), 0)
        phi = (row & (period - 1)).astype(F32) * inv
        cphi_ref[...] = jnp.cos(phi)
        sphi_ref[...] = jnp.sin(phi)

    theta = (start + ((pl.program_id(1) * tm) & (period - 1))).astype(F32) * inv
    c_th, s_th = jnp.cos(theta), jnp.sin(theta)
    nparts = 2 if tm % (2 * LANES) == 0 else 1
    parts = [slice(p * tm // nparts, (p + 1) * tm // nparts) for p in range(nparts)]
    projs = [_dot(_rms(x_ref[0, r, :], g_ref[0]).astype(BF16), w_ref[0]) for r in parts]
    for r, proj in zip(parts, projs):
        c_ph, s_ph = cphi_ref[r, :], sphi_ref[r, :]
        cos = c_th * c_ph - s_th * s_ph
        sin = s_th * c_ph + c_th * s_ph
        sa = jnp.where(first, -sin, 0.0)
        sb = jnp.where(first, 0.0, sin)
        u_ref[0, r, :] = proj[:, :o0]
        q = [_rope(proj[:, c:c + LANES], cos, sa, sb) for c in range(o0, o1, LANES)]
        q_ref[0, r, :] = (jnp.concatenate(q, axis=1) * HEAD_DIM ** -0.5).astype(BF16)
        k_ref[0, r, :] = _rope(proj[:, o1:o2], cos, sa, sb)
        v_ref[0, r, :] = proj[:, o2:]


def _inproj_even(x, g, w_in, start, period, layer_e, tm):
    bsz, seq, _ = x.shape
    tok = lambda w: pl.BlockSpec((1, tm, w), lambda b, i: (b, i, 0))
    return pl.pallas_call(
        functools.partial(_inproj_even_body, start=start, period=period),
        grid=(bsz, seq // tm),
        in_specs=[tok(D_MODEL), _gain_spec(2 * layer_e),
                  _const_spec((1, D_MODEL, EVEN_IN), (layer_e, 0, 0))],
        out_specs=[tok(SSM_WIDTH), tok(N_HEADS * HEAD_DIM), tok(KV_WIDTH), tok(KV_WIDTH)],
        out_shape=[jax.ShapeDtypeStruct((bsz, seq, SSM_WIDTH), F32),
                   jax.ShapeDtypeStruct((bsz, seq, N_HEADS * HEAD_DIM), BF16),
                   jax.ShapeDtypeStruct((bsz, seq, KV_WIDTH), F32),
                   jax.ShapeDtypeStruct((bsz, seq, KV_WIDTH), F32)],
        scratch_shapes=[pltpu.VMEM((tm, LANES), F32)] * 2,
        compiler_params=_params(2),
        name="inproj_even",
    )(x, g, w_in)


def _s5_prep_body(lr_ref, li_ref, ld_ref, bre_ref, bim_ref, cre_ref, cim_ref, wg_ref, wy_ref):
    ts = S5_TILE_STATES
    tw = S5_T * LANES

    def times(a_r, a_i):
        return lambda p: (p[0] * a_r - p[1] * a_i, p[0] * a_i + p[1] * a_r)

    lr, li = lr_ref[0], li_ref[0]
    dt = jnp.exp(ld_ref[0])
    mag = jnp.exp(lr * dt)
    ar, ai = mag * jnp.cos(li * dt), mag * jnp.sin(li * dt)
    mul_a = times(ar, ai)
    nrm = lr * lr + li * li
    cbr = ((ar - 1.0) * lr + ai * li) / nrm
    cbi = (ai * lr - (ar - 1.0) * li) / nrm
    b_re, b_im = bre_ref[0], bim_ref[0]
    bbr = cbr * b_re - cbi * b_im
    bbi = cbr * b_im + cbi * b_re
    p = (jnp.ones_like(ar), jnp.zeros_like(ar))
    for s in reversed(range(S5_T)):
        rows = slice(s * LANES, (s + 1) * LANES)
        wg_ref[0, rows, :ts] = (p[0] * bbr - p[1] * bbi).astype(BF16)
        wg_ref[0, rows, ts:] = (p[0] * bbi + p[1] * bbr).astype(BF16)
        for t in range(s):
            wy_ref[0, rows, t * LANES:(t + 1) * LANES] = jnp.zeros((LANES, LANES), BF16)
        p = mul_a(p)
    to_col = lambda row: jnp.transpose(jnp.broadcast_to(row, (LANES, ts)))
    mul_ac = times(to_col(ar), to_col(ai))
    c_re, c_im = cre_ref[0], cim_ref[0]
    bb = jnp.concatenate([bbr, bbi], axis=1).astype(BF16)
    pc = (jnp.ones_like(c_re), jnp.zeros_like(c_re))
    for k in range(S5_T + 1):
        blk = jnp.concatenate([pc[0] * c_re - pc[1] * c_im, -(pc[0] * c_im + pc[1] * c_re)], axis=0).astype(BF16)
        if k >= 1:
            wy_ref[0, tw:, (k - 1) * LANES:k * LANES] = blk
        if k < S5_T:
            lag = _dot(bb, blk).astype(BF16)
            for s in range(S5_T - k):
                t = s + k
                wy_ref[0, s * LANES:(s + 1) * LANES, t * LANES:(t + 1) * LANES] = lag
        pc = mul_ac(pc)


def _s5_prep(a_re, a_im, log_dt, b_re, b_im, c_re, c_im):
    gpt = LANES // SSM_GROUP
    eye = jnp.eye(gpt, dtype=bool)

    def rows(p):
        return p.reshape(S5_TILES, 1, S5_TILE_STATES)

    def bmat(b):
        bt = b.reshape(S5_TILES, gpt, SSM_STATE, SSM_GROUP).transpose(0, 1, 3, 2)
        full = jnp.where(eye[None, :, None, :, None], bt[:, :, :, None, :], 0.0)
        return full.reshape(S5_TILES, LANES, S5_TILE_STATES)

    def cmat(c):
        ct = c.reshape(S5_TILES, gpt, SSM_GROUP, SSM_STATE).transpose(0, 1, 3, 2)
        full = jnp.where(eye[None, :, None, :, None], ct[:, :, :, None, :], 0.0)
        return full.reshape(S5_TILES, S5_TILE_STATES, LANES)

    row_spec = pl.BlockSpec((1, 1, S5_TILE_STATES), lambda j: (j, 0, 0))
    b_spec = pl.BlockSpec((1, LANES, S5_TILE_STATES), lambda j: (j, 0, 0))
    c_spec = pl.BlockSpec((1, S5_TILE_STATES, LANES), lambda j: (j, 0, 0))
    tw = S5_T * LANES
    return pl.pallas_call(
        _s5_prep_body,
        grid=(S5_TILES,),
        in_specs=[row_spec] * 3 + [b_spec] * 2 + [c_spec] * 2,
        out_specs=[pl.BlockSpec((1, tw, 2 * S5_TILE_STATES), lambda j: (j, 0, 0)),
                   pl.BlockSpec((1, tw + 2 * S5_TILE_STATES, tw), lambda j: (j, 0, 0))],
        out_shape=[jax.ShapeDtypeStruct((S5_TILES, tw, 2 * S5_TILE_STATES), BF16),
                   jax.ShapeDtypeStruct((S5_TILES, tw + 2 * S5_TILE_STATES, tw), BF16)],
        compiler_params=_params(1),
        name="s5_prep",
    )(rows(a_re), rows(a_im), rows(log_dt), bmat(b_re), bmat(b_im), cmat(c_re), cmat(c_im))


def _s5_pow_body(ex_ref, lr_ref, li_ref, ld_ref, *o_refs):
    dt = jnp.exp(ld_ref[...])
    zr, zi = lr_ref[...] * dt, li_ref[...] * dt
    lo = 0
    for o_ref in o_refs:
        ex = ex_ref[lo:lo + o_ref.shape[0], :]
        lo += o_ref.shape[0]
        mag = jnp.exp(ex * zr)
        o_ref[:, :S5_STATES] = mag * jnp.cos(ex * zi)
        o_ref[:, S5_STATES:] = mag * jnp.sin(ex * zi)


def _s5_pow_table(groups, a_re, a_im, log_dt):
    sizes = [len(g) for g in groups]
    ex = jnp.asarray(np.asarray(sum(groups, []), np.float32)[:, None])
    flat = lambda p: p.reshape(1, S5_STATES)
    whole = lambda n, w: pl.BlockSpec((n, w), lambda i: (0, 0))
    return pl.pallas_call(
        _s5_pow_body,
        grid=(1,),
        in_specs=[whole(sum(sizes), 1)] + [whole(1, S5_STATES)] * 3,
        out_specs=[whole(n, 2 * S5_STATES) for n in sizes],
        out_shape=[jax.ShapeDtypeStruct((n, 2 * S5_STATES), F32) for n in sizes],
        compiler_params=_params(1),
        name="s5_pow_table",
    )(ex, flat(a_re), flat(a_im), flat(log_dt))


def _s5_body(u_ref, h0_ref, wg_ref, wy_ref, pow2_ref, powr_ref, d_ref, wglu_ref,
             y_ref, hout_ref, carry_ref, us_ref, ys_ref, *, ni, chained):
    ns = S5_STATES
    ts = S5_TILE_STATES
    segtok = ni * S5_T
    pitch = _s5_pitch(ni)

    @pl.when(pl.program_id(1) == 0)
    def _():
        carry_ref[...] = h0_ref[0]

    for j in range(S5_TILES):
        for s in range(SUBLANES):
            us_ref[j, s * pitch:s * pitch + segtok] = u_ref[0, s * segtok:(s + 1) * segtok, j * LANES:(j + 1) * LANES]

    def gather(j, t):
        return jnp.concatenate([us_ref[j, pl.ds(S5_T * i + t, SUBLANES, stride=pitch), :]
                                for i in range(ni)], axis=0)

    def fma(p_r, p_i, x_r, x_i, y_r, y_i):
        return y_r + p_r * x_r - p_i * x_i, y_i + p_r * x_i + p_i * x_r

    xs, g_re, g_im = [], [], []
    for j in range(S5_TILES):
        xj = jnp.concatenate([gather(j, s) for s in range(S5_T)], axis=1).astype(BF16)
        g = _dot(xj, wg_ref[j])
        xs.append(xj)
        g_re.append(g[:, :ts])
        g_im.append(g[:, ts:])
    g_re = jnp.concatenate(g_re, axis=1)
    g_im = jnp.concatenate(g_im, axis=1)
    slab = lambda a, i: a[i * SUBLANES:(i + 1) * SUBLANES]
    power = lambda ref, r: (ref[r:r + 1, :ns], ref[r:r + 1, ns:])
    a_r, a_i = power(powr_ref, 0)
    cin = carry_ref[...]
    h_r, h_i = cin[:, :ns], cin[:, ns:]
    prev = []
    if chained:
        local = [(slab(g_re, 0), slab(g_im, 0))]
        for i in range(1, ni):
            local.append(fma(a_r, a_i, *local[-1], slab(g_re, i), slab(g_im, i)))
        sub = lax.broadcasted_iota(jnp.int32, (SUBLANES, 1), 0)
        c_r = jnp.where(sub == 0, h_r, pltpu.roll(local[-1][0], 1, 0))
        c_i = jnp.where(sub == 0, h_i, pltpu.roll(local[-1][1], 1, 0))
        for m in range(3):
            k = 1 << m
            s_r = jnp.where(sub >= k, pltpu.roll(c_r, k, 0), 0.0)
            s_i = jnp.where(sub >= k, pltpu.roll(c_i, k, 0), 0.0)
            c_r, c_i = fma(*power(pow2_ref, m), s_r, s_i, c_r, c_i)
        h_r, h_i = c_r, c_i
        for i in range(ni):
            prev.append((h_r, h_i))
            h_r, h_i = fma(*power(powr_ref, i), c_r, c_i, *local[i])
        last = jnp.concatenate([h_r[SUBLANES - 1:], h_i[SUBLANES - 1:]], axis=1)
    else:
        for i in range(ni):
            prev.append((h_r, h_i))
            h_r, h_i = fma(a_r, a_i, h_r, h_i, slab(g_re, i), slab(g_im, i))
        last = jnp.concatenate([h_r, h_i], axis=1)
    carry_ref[...] = last
    hout_ref[0] = last
    prev_re = jnp.concatenate([p[0] for p in prev], axis=0)
    prev_im = jnp.concatenate([p[1] for p in prev], axis=0)
    y_tiles = []
    for j in range(S5_TILES):
        lhs = jnp.concatenate([xs[j], prev_re[:, j * ts:(j + 1) * ts].astype(BF16),
                               prev_im[:, j * ts:(j + 1) * ts].astype(BF16)], axis=1)
        y_tiles.append(_dot(lhs, wy_ref[j]))
    d = d_ref[...]
    wglu = wglu_ref[0]
    def gelu_in(t):
        ut = jnp.concatenate([gather(j, t) for j in range(S5_TILES)], axis=1)
        yt = jnp.concatenate([y_tiles[j][:, t * LANES:(t + 1) * LANES] for j in range(S5_TILES)], axis=1)
        return jax.nn.gelu(yt + d * ut)

    def scatter(t, o):
        for j in range(S5_TILES):
            for i in range(ni):
                ys_ref[j, pl.ds(S5_T * i + t, SUBLANES, stride=pitch), :] = slab(o, i)[:, j * LANES:(j + 1) * LANES]

    g_q, z_q = {}, {}
    for step in range(S5_T + 2):
        if step < S5_T:
            g_q[step] = gelu_in(step)
        if 0 <= step - 1 < S5_T:
            z_q[step - 1] = _dot(g_q[step - 1].astype(BF16), wglu)
        if 0 <= step - 2 < S5_T:
            scatter(step - 2, g_q.pop(step - 2) * jax.nn.sigmoid(z_q.pop(step - 2)))
    for j in range(S5_TILES):
        for s in range(SUBLANES):
            y_ref[0, s * segtok:(s + 1) * segtok, j * LANES:(j + 1) * LANES] = (
                ys_ref[j, s * pitch:s * pitch + segtok].astype(BF16))


def _s5_pitch(ni):
    segtok = ni * S5_T
    return segtok + SUBLANES if (segtok // SUBLANES) % 2 == 0 else segtok


def _s5_mixer(u, h0, wg, wy, pow2, powr, d, w_glu, layer_e, ni, chained):
    nb, ntok, width = u.shape
    tb = SUBLANES * ni * S5_T
    hrows = h0.shape[1]
    blk = pl.BlockSpec((1, tb, width), lambda b, i: (b, i, 0))
    state = pl.BlockSpec((1, hrows, 2 * S5_STATES), lambda b, i: (b, 0, 0))
    return pl.pallas_call(
        functools.partial(_s5_body, ni=ni, chained=chained),
        grid=(nb, ntok // tb),
        in_specs=[blk, state, _const_spec(wg.shape), _const_spec(wy.shape),
                  _const_spec(pow2.shape), _const_spec(powr.shape), _const_spec((1, SSM_WIDTH)),
                  _const_spec((1, SSM_WIDTH, SSM_WIDTH), (layer_e, 0, 0))],
        out_specs=[blk, state],
        out_shape=[jax.ShapeDtypeStruct((nb, ntok, width), BF16),
                   jax.ShapeDtypeStruct((nb, hrows, 2 * S5_STATES), F32)],
        scratch_shapes=[pltpu.VMEM((hrows, 2 * S5_STATES), F32)]
        + [pltpu.VMEM((S5_TILES, SUBLANES * _s5_pitch(ni), LANES), F32)] * 2,
        compiler_params=_params(2),
        name="s5_mixer",
    )(u, h0, wg, wy, pow2, powr, d, w_glu)


def _swa_body(sink_ref, q_ref, kc_ref, vc_ref, kp_ref, vp_ref, pk_ref, pv_ref, o_ref, *, tq, qg, start):
    first = pl.program_id(1) == 0
    k_prev = jnp.where(first, pk_ref[0], kp_ref[0])
    v_prev = jnp.where(first, pv_ref[0], vp_ref[0])
    k_all = jnp.concatenate([k_prev, kc_ref[0]], axis=0)
    v_all = jnp.concatenate([v_prev, vc_ref[0]], axis=0)
    nk = WINDOW + qg
    lane = lax.broadcasted_iota(jnp.int32, (1, KV_WIDTH), 1)
    slot = lax.broadcasted_iota(jnp.int32, (1, nk), 1)
    qrow = lax.broadcasted_iota(jnp.int32, (2 * qg, 1), 0)
    lower = lane < HEAD_DIM
    placed = []
    for kh in range(N_KV_HEADS):
        own = lower if kh == 0 else jnp.logical_not(lower)
        kz, vz = jnp.where(own, k_all, 0.0), jnp.where(own, v_all, 1.0)
        kr, vr = pltpu.roll(kz, HEAD_DIM, 1), pltpu.roll(vz, HEAD_DIM, 1)
        k_lo, k_hi = (kz, kr) if kh == 0 else (kr, kz)
        v_lo, v_hi = (vz, vr) if kh == 0 else (vr, vz)
        placed.append(((k_lo.astype(BF16), v_lo.astype(BF16)), (k_hi.astype(BF16), v_hi.astype(BF16))))
    combos = [(kh, par) for kh in range(N_KV_HEADS) for par in range(2)]

    def scores(c):
        qc = q_ref[0, c * qg:(c + 1) * qg, :]
        out = []
        for kh, par in combos:
            base = kh * GQ * HEAD_DIM
            qs = jnp.concatenate([qc[:, base:base + LANES], qc[:, base + LANES:base + 2 * LANES]], axis=0)
            out.append(_dot_t(qs, placed[kh][par][0][c * qg:c * qg + nk]))
        return out

    def softmax(c, ss):
        n_bad = jnp.where(first, WINDOW - start - c * qg, 0) if c * qg < WINDOW - start else None
        probs, sinks = [], []
        for (kh, par), s in zip(combos, ss):
            if n_bad is not None:
                s = jnp.where(slot >= n_bad, s, NEG)
            sink = jnp.where(qrow < qg, sink_ref[kh * GQ + par], sink_ref[kh * GQ + 2 + par])
            m = jnp.maximum(jnp.max(s, axis=1, keepdims=True), sink)
            probs.append(jnp.exp(s - m).astype(BF16))
            sinks.append(jnp.exp(sink - m))
        return probs, sinks

    def finish(c, probs, sinks):
        pvs = [_dot(p, placed[kh][par][1][c * qg:c * qg + nk]) for (kh, par), p in zip(combos, probs)]
        outs = []
        for kh in range(N_KV_HEADS):
            pv_e, pv_o = pvs[2 * kh], pvs[2 * kh + 1]
            num = jnp.where(lower, pv_e, pv_o)
            den = pltpu.roll(jnp.where(lower, pv_o, pv_e), HEAD_DIM, 1) + jnp.where(lower, sinks[2 * kh], sinks[2 * kh + 1])
            acc = num / den
            outs += [acc[:qg], acc[qg:]]
        o_ref[0, c * qg:(c + 1) * qg, :] = jnp.concatenate(outs, axis=1).astype(BF16)

    nchunk = tq // qg
    per = min(SWA_STAGE_CHUNKS, nchunk)
    groups = [range(g, min(g + per, nchunk)) for g in range(0, nchunk, per)]
    s_q, p_q = {}, {}
    for step in range(len(groups) + 2):
        if step < len(groups):
            s_q[step] = [scores(c) for c in groups[step]]
        if 0 <= step - 1 < len(groups):
            p_q[step - 1] = [softmax(c, ss) for c, ss in zip(groups[step - 1], s_q.pop(step - 1))]
        if 0 <= step - 2 < len(groups):
            for c, (probs, sinks) in zip(groups[step - 2], p_q.pop(step - 2)):
                finish(c, probs, sinks)


def _swa(q, k, v, past_k, past_v, sinks, start, tq, qg):
    bsz, seq, _ = q.shape
    ntiles = seq // tq
    tok = lambda w: pl.BlockSpec((1, tq, w), lambda b, i: (b, i, 0))
    past = pl.BlockSpec((1, WINDOW, KV_WIDTH), lambda b, i: (b, 0, 0))
    if ntiles > 1:
        per = tq // WINDOW
        prev = pl.BlockSpec((1, WINDOW, KV_WIDTH), lambda b, i: (b, jnp.maximum(i * per - 1, 0), 0))
        k_prev, v_prev = k, v
    else:
        prev, k_prev, v_prev = past, past_k, past_v
    return pl.pallas_call(
        functools.partial(_swa_body, tq=tq, qg=qg, start=start),
        grid=(bsz, ntiles),
        in_specs=[pl.BlockSpec(memory_space=pltpu.SMEM), tok(N_HEADS * HEAD_DIM), tok(KV_WIDTH), tok(KV_WIDTH),
                  prev, prev, past, past],
        out_specs=tok(N_HEADS * HEAD_DIM),
        out_shape=jax.ShapeDtypeStruct((bsz, seq, N_HEADS * HEAD_DIM), BF16),
        compiler_params=_params(2),
        name="swa",
    )(sinks, q, k, v, k_prev, v_prev, past_k, past_v)


def _xattn(x1, g, wq, mk_ref, mv_ref, wo, nseq, o_ref):
    h = _rms(x1, g).astype(BF16)
    q = (_dot(h, wq) * X_HEAD_DIM ** -0.5).astype(BF16)
    nparts = nseq if nseq > 1 else XATTN_PARTS
    prows = x1.shape[0] // nparts
    heads = [slice(hd * X_HEAD_DIM, (hd + 1) * X_HEAD_DIM) for hd in range(X_HEADS)]

    def scores(p):
        r, b = slice(p * prows, (p + 1) * prows), (p if nseq > 1 else 0)
        return [_dot_t(q[r, c], mk_ref[0, b, :, c]) for c in heads]

    def softmax(ss):
        out = []
        for s in ss:
            e = jnp.exp(s - jnp.max(s, axis=1, keepdims=True))
            out.append((e.astype(BF16), jnp.sum(e, axis=1, keepdims=True)))
        return out

    def values(p, pd):
        b = p if nseq > 1 else 0
        return jnp.concatenate([_dot(e, mv_ref[0, b, :, c]) / den for c, (e, den) in zip(heads, pd)],
                               axis=1).astype(BF16)

    def finish(p, pd):
        r = slice(p * prows, (p + 1) * prows)
        o_ref[0, r, :] = x1[r] + _dot(values(p, pd), wo)

    if nseq > 1:
        pds = [softmax(ss) for ss in [scores(p) for p in range(nparts)]]
        o = jnp.concatenate([values(p, pd) for p, pd in enumerate(pds)], axis=0)
        o_ref[0] = x1 + _dot(o, wo)
        return
    s_q, p_q = {}, {}
    for step in range(nparts + 2):
        if step < nparts:
            s_q[step] = scores(step)
        if 0 <= step - 1 < nparts:
            p_q[step - 1] = softmax(s_q.pop(step - 1))
        if 0 <= step - 2 < nparts:
            finish(step - 2, p_q.pop(step - 2))


def _conv3(cur, carry, w):
    n = cur.shape[0]
    row = lax.broadcasted_iota(jnp.int32, (n, 1), 0)
    nc = carry.shape[0]
    c1, c2 = carry[nc - 1:nc], carry[nc - 2:nc - 1]
    m1 = jnp.where(row == 0, c1, pltpu.roll(cur, 1, 0))
    m2 = jnp.where(row == 0, c2, jnp.where(row == 1, c1, pltpu.roll(cur, 2, 0)))
    return w[0:1] * m2 + w[1:2] * m1 + w[2:3] * cur


def _conv3_seqs(cur, carries, w):
    nseq = len(carries)
    if nseq == 1:
        return _conv3(cur, carries[0], w)
    rows = cur.shape[0] // nseq
    body = _conv3(cur, carries[0], w)
    pieces = []
    for b in range(nseq):
        pieces.append(_conv3(cur[b * rows:b * rows + SUBLANES], carries[b], w))
        pieces.append(body[b * rows + SUBLANES:(b + 1) * rows])
    return jnp.concatenate(pieces, axis=0)


def _store_last_rows(st_ref, cols, cur, nseq):
    rows = cur.shape[0] // nseq
    for b in range(nseq):
        st_ref[b, :, cols] = cur[(b + 1) * rows - (CONV_WIDTH - 1):(b + 1) * rows]


def _mix_even_xattn_body(x_ref, ya_ref, yb_ref, wout_ref, g_ref, wq_ref, mk_ref, mv_ref, wo_ref, o_ref, *, nseq):
    x1 = x_ref[0] + _dot(jnp.concatenate([ya_ref[0], yb_ref[0]], axis=1), wout_ref[0])
    _xattn(x1, g_ref[0], wq_ref[0], mk_ref, mv_ref, wo_ref[0], nseq, o_ref)


def _mix_even_xattn(x, ya, yb, w_out, g, wq, mem_k, mem_v, wo, layer, layer_e, tm, nseq):
    bsz, seq, _ = x.shape
    tok = lambda w: pl.BlockSpec((1, tm, w), lambda b, i: (b, i, 0))
    mem = pl.BlockSpec((1, nseq, N_MEM, D_MODEL), lambda b, i: (layer, b, 0, 0))
    half = D_MODEL // 2
    sq = lambda: _const_spec((1, D_MODEL, D_MODEL), (layer, 0, 0))
    return pl.pallas_call(
        functools.partial(_mix_even_xattn_body, nseq=nseq),
        grid=(bsz, seq // tm),
        in_specs=[tok(D_MODEL), tok(half), tok(half),
                  _const_spec((1, D_MODEL, D_MODEL), (layer_e, 0, 0)),
                  _gain_spec(layer), sq(), mem, mem, sq()],
        out_specs=tok(D_MODEL),
        out_shape=jax.ShapeDtypeStruct(x.shape, F32),
        compiler_params=_params(2),
        name="mix_even_xattn",
    )(x, ya, yb, w_out, g, wq, mem_k, mem_v, wo)


def _mix_odd_xattn_body(x_ref, gm_ref, win_ref, cw_ref, wout_ref, prev_ref, g_ref, wq_ref, mk_ref, mv_ref, wo_ref,
                        o_ref, st_ref, carry_ref, *, tm, nseq):
    @pl.when(pl.program_id(1) == 0)
    def _():
        carry_ref[SUBLANES - (CONV_WIDTH - 1):, :] = prev_ref[0, 0]

    x = x_ref[0]
    proj = _dot(_rms(x, gm_ref[0]).astype(BF16), win_ref[0])
    gate_b, gate_c, z = proj[:, :D_MODEL], proj[:, D_MODEL:2 * D_MODEL], proj[:, 2 * D_MODEL:]
    cz = gate_c * z
    keep = CONV_WIDTH - 1
    carries = [carry_ref[SUBLANES - keep:, :]] if nseq == 1 else [prev_ref[0, b] for b in range(nseq)]
    z_conv = _conv3_seqs(cz, carries, cw_ref[0])
    carry_ref[...] = cz[tm - SUBLANES:]
    _store_last_rows(st_ref, slice(None), cz, nseq)
    x1 = x + _dot((gate_b * z_conv).astype(BF16), wout_ref[0])
    _xattn(x1, g_ref[0], wq_ref[0], mk_ref, mv_ref, wo_ref[0], nseq, o_ref)


def _mix_odd_xattn(x, gm, w_in, conv_w, w_out, prev, g, wq, mem_k, mem_v, wo, layer, layer_o, tm, nseq):
    bsz, seq, _ = x.shape
    tok = pl.BlockSpec((1, tm, D_MODEL), lambda b, i: (b, i, 0))
    mem = pl.BlockSpec((1, nseq, N_MEM, D_MODEL), lambda b, i: (layer, b, 0, 0))
    hist = pl.BlockSpec((1, nseq, CONV_WIDTH - 1, D_MODEL), lambda b, i: (layer_o, b, 0, 0))
    st = pl.BlockSpec((nseq, CONV_WIDTH - 1, D_MODEL), lambda b, i: (b, 0, 0))
    sq = lambda l: _const_spec((1, D_MODEL, D_MODEL), (l, 0, 0))
    return pl.pallas_call(
        functools.partial(_mix_odd_xattn_body, tm=tm, nseq=nseq),
        grid=(bsz, seq // tm),
        in_specs=[tok, _gain_spec(layer), _const_spec((1, D_MODEL, 3 * D_MODEL), (layer_o, 0, 0)),
                  _const_spec((1, CONV_WIDTH, D_MODEL), (layer_o, 0, 0)), sq(layer_o), hist,
                  _gain_spec(layer), sq(layer), mem, mem, sq(layer)],
        out_specs=[tok, st],
        out_shape=[jax.ShapeDtypeStruct(x.shape, F32),
                   jax.ShapeDtypeStruct((bsz * nseq, CONV_WIDTH - 1, D_MODEL), F32)],
        scratch_shapes=[pltpu.VMEM((SUBLANES, D_MODEL), F32)],
        compiler_params=_params(2),
        name="mix_odd_xattn",
    )(x, gm, w_in, conv_w, w_out, prev, g, wq, mem_k, mem_v, wo)


def _ffn_tile(x_ref, g_ref, wup_ref, cw_ref, wdn_ref, prev_ref, gfin_ref, o_ref, st_ref, carry_ref,
              *, tm, final, nseq, first):
    if nseq == 1:
        @pl.when(first)
        def _():
            carry_ref[SUBLANES - (CONV_WIDTH - 1):, :] = prev_ref[0, 0]

    x = x_ref[0]
    h = _rms(x, g_ref[0]).astype(BF16)
    chunks = [slice(lo, min(lo + FF_CHUNK, D_FF)) for lo in range(0, D_FF, FF_CHUNK)]

    def up(cols):
        return _dot(h, wup_ref[0, :, cols]), _dot(h, wup_ref[0, :, D_FF + cols.start:D_FF + cols.stop])

    acc = x
    nxt = up(chunks[0])
    for n, cols in enumerate(chunks):
        (gate, val), nxt = nxt, (up(chunks[n + 1]) if n + 1 < len(chunks) else None)
        carries = ([carry_ref[SUBLANES - (CONV_WIDTH - 1):, cols]] if nseq == 1
                   else [prev_ref[0, b, :, cols] for b in range(nseq)])
        gate_c = _conv3_seqs(gate, carries, cw_ref[0, :, cols])
        if nseq == 1:
            carry_ref[:, cols] = gate[tm - SUBLANES:]
        _store_last_rows(st_ref, cols, gate, nseq)
        act = (gate_c * jax.nn.sigmoid(gate_c) * val).astype(BF16)
        if n + 1 < len(chunks):
            acc = acc + _dot(act, wdn_ref[0, cols, :])
        else:
            half = tm // 2
            for rows in (slice(0, half), slice(half, tm)):
                out = acc[rows] + _dot(act[rows], wdn_ref[0, cols, :])
                o_ref[0, rows, :] = _rms(out, gfin_ref[0]) if final else out


def _ffn_body(xl_ref, xs_ref, g_ref, wup_ref, cw_ref, wdn_ref, prevl_ref, prevs_ref, gfin_ref,
              ol_ref, os_ref, stl_ref, sts_ref, carry_ref, *, tm, nt, nlong, rows_s, nseq_s, final):
    s = pl.program_id(0)
    shared = (g_ref, wup_ref, cw_ref, wdn_ref)

    @pl.when(s < nlong)
    def _():
        _ffn_tile(xl_ref, *shared, prevl_ref, gfin_ref, ol_ref, stl_ref, carry_ref,
                  tm=tm, final=final, nseq=1, first=lax.rem(s, nt) == 0)

    @pl.when(s == nlong)
    def _():
        _ffn_tile(xs_ref, *shared, prevs_ref, gfin_ref, os_ref, sts_ref, carry_ref,
                  tm=rows_s, final=final, nseq=nseq_s, first=None)


def _ffn(x_long, x_short, g, w_up, conv_w, w_down, prev_long, prev_short, g_final, layer, tm, final):
    nb, seq, _ = x_long.shape
    rows_s = x_short.shape[1]
    nseq_s = prev_short.shape[1]
    nt = seq // tm
    nlong = nb * nt
    lb = lambda s: jnp.minimum(s, nlong - 1) // nt
    li = lambda s: jnp.minimum(s, nlong - 1) % nt
    keep = CONV_WIDTH - 1
    tok_l = pl.BlockSpec((1, tm, D_MODEL), lambda s: (lb(s), li(s), 0))
    tok_s = pl.BlockSpec((1, rows_s, D_MODEL), lambda s: (0, 0, 0))
    hist_l = pl.BlockSpec((1, 1, keep, D_FF), lambda s: (layer, lb(s), 0, 0))
    hist_s = pl.BlockSpec((1, nseq_s, keep, D_FF), lambda s: (layer, 0, 0, 0))
    st_l = pl.BlockSpec((1, keep, D_FF), lambda s: (lb(s), 0, 0))
    st_s = pl.BlockSpec((nseq_s, keep, D_FF), lambda s: (0, 0, 0))
    return pl.pallas_call(
        functools.partial(_ffn_body, tm=tm, nt=nt, nlong=nlong, rows_s=rows_s, nseq_s=nseq_s, final=final),
        grid=(nlong + 1,),
        in_specs=[tok_l, tok_s, _gain_spec(layer), _const_spec((1, D_MODEL, 2 * D_FF), (layer, 0, 0)),
                  _const_spec((1, CONV_WIDTH, D_FF), (layer, 0, 0)), _const_spec((1, D_FF, D_MODEL), (layer, 0, 0)),
                  hist_l, hist_s, _gain_spec(0)],
        out_specs=[tok_l, tok_s, st_l, st_s],
        out_shape=[jax.ShapeDtypeStruct(x_long.shape, F32), jax.ShapeDtypeStruct(x_short.shape, F32),
                   jax.ShapeDtypeStruct((nb, keep, D_FF), F32), jax.ShapeDtypeStruct((nseq_s, keep, D_FF), F32)],
        scratch_shapes=[pltpu.VMEM((SUBLANES, D_FF), F32)],
        compiler_params=_params(1),
        name="conv_ffn",
    )(x_long, x_short, g, w_up, conv_w, w_down, prev_long, prev_short, g_final)


def _mem_flat_copies(k_hbm, v_hbm, kbuf, vbuf, sem, step, slot, nbatch):
    lyr, b = step // nbatch, step % nbatch
    return [pltpu.make_async_copy(src.at[lyr, b, :, hd, :], buf.at[slot, hd], sem.at[slot, i, hd])
            for i, (src, buf) in enumerate(((k_hbm, kbuf), (v_hbm, vbuf))) for hd in range(X_HEADS)]


def _mem_flat_body(k_hbm, v_hbm, ko_ref, vo_ref, kbuf, vbuf, sem, *, nbatch):
    s, n = pl.program_id(0), pl.num_programs(0)
    copies = functools.partial(_mem_flat_copies, k_hbm, v_hbm, kbuf, vbuf, sem, nbatch=nbatch)

    @pl.when(s == 0)
    def _():
        for c in copies(step=s, slot=0):
            c.start()

    @pl.when(s + 1 < n)
    def _():
        for c in copies(step=s + 1, slot=(s + 1) % 2):
            c.start()

    slot = s % 2
    for c in copies(step=s, slot=slot):
        c.wait()
    for hd in range(X_HEADS):
        cols = slice(hd * X_HEAD_DIM, (hd + 1) * X_HEAD_DIM)
        ko_ref[0, 0, :, cols] = kbuf[slot, hd].astype(BF16)
        vo_ref[0, 0, :, cols] = vbuf[slot, hd].astype(BF16)


def _mem_flat(mem_k, mem_v):
    depth, nbatch = mem_k.shape[:2]
    flat = pl.BlockSpec((1, 1, N_MEM, D_MODEL), lambda s: (s // nbatch, s % nbatch, 0, 0))
    return pl.pallas_call(
        functools.partial(_mem_flat_body, nbatch=nbatch),
        grid=(depth * nbatch,),
        in_specs=[pl.BlockSpec(memory_space=pl.ANY)] * 2,
        out_specs=[flat, flat],
        out_shape=[jax.ShapeDtypeStruct((depth, nbatch, N_MEM, D_MODEL), BF16)] * 2,
        scratch_shapes=[pltpu.VMEM((2, X_HEADS, N_MEM, X_HEAD_DIM), F32)] * 2
        + [pltpu.SemaphoreType.DMA((2, 2, X_HEADS))],
        compiler_params=_params(1),
        name="mem_flat",
    )(mem_k, mem_v)


def _mem_kv_body(m_ref, w_ref, k_ref, v_ref, kf_ref, vf_ref):
    kv = _dot(m_ref[0].astype(BF16), w_ref[0])
    kf_ref[0, 0] = kv[:, :D_MODEL].astype(BF16)
    vf_ref[0, 0] = kv[:, D_MODEL:].astype(BF16)
    for hd in range(X_HEADS):
        k_ref[0, 0, :, hd, :] = kv[:, hd * X_HEAD_DIM:(hd + 1) * X_HEAD_DIM]
        v_ref[0, 0, :, hd, :] = kv[:, D_MODEL + hd * X_HEAD_DIM:D_MODEL + (hd + 1) * X_HEAD_DIM]


def _mem_kv(mem, w_kv):
    bsz = mem.shape[0]
    heads = pl.BlockSpec((1, 1, N_MEM, X_HEADS, X_HEAD_DIM), lambda l, b: (l, b, 0, 0, 0))
    flat = pl.BlockSpec((1, 1, N_MEM, D_MODEL), lambda l, b: (l, b, 0, 0))
    return pl.pallas_call(
        _mem_kv_body,
        grid=(DEPTH, bsz),
        in_specs=[pl.BlockSpec((1, N_MEM, D_MODEL), lambda l, b: (b, 0, 0)),
                  pl.BlockSpec((1, D_MODEL, 2 * D_MODEL), lambda l, b: (l, 0, 0))],
        out_specs=[heads, heads, flat, flat],
        out_shape=[jax.ShapeDtypeStruct((DEPTH, bsz, N_MEM, X_HEADS, X_HEAD_DIM), F32)] * 2
        + [jax.ShapeDtypeStruct((DEPTH, bsz, N_MEM, D_MODEL), BF16)] * 2,
        compiler_params=_params(2),
        name="mem_kv",
    )(mem, w_kv)


def _layer0_mixer(x, start, past_k, past_v, ssm_re, ssm_im, mem_k, mem_v, w, s5, cfg):
    bsz, seq, _ = x.shape
    tq, qg, nseq = cfg["tq"], cfg["qg"], cfg["nseq"]
    tm_in, tm_even, tm_odd, tm_ffn = cfg["tm_in"], cfg["tm_even"], cfg["tm_odd"], cfg["tm_ffn"]
    ni, chained = cfg["ni"], cfg["chained"]
    ngrp, grows = bsz // nseq, nseq * seq
    grouped = lambda t: t.reshape(ngrp, grows, t.shape[-1])
    per_seq = lambda t: t.reshape(bsz, seq, t.shape[-1])
    u, q, k, v = _inproj_even(grouped(x), w["norm_mix"], w["w_in_even"], start, seq, 0, tm_in)
    h0 = jnp.concatenate([ssm_re.reshape(bsz, S5_STATES), ssm_im.reshape(bsz, S5_STATES)], axis=1)
    if chained:
        ya, h_all = _s5_mixer(u, h0[:, None, :], s5["wg"], s5["wy"], cfg["pow2"], cfg["powr"], w["ssm_d"],
                              w["w_glu"], 0, ni, True)
        h_last = h_all[:, 0]
    else:
        assert ngrp == 1 and bsz == SUBLANES and seq == ni * S5_T
        ya, h_all = _s5_mixer(u, h0[None], s5["wg"], s5["wy"], cfg["pow2"], cfg["powr"], w["ssm_d"],
                              w["w_glu"], 0, ni, False)
        h_last = h_all[0]
    new_re = h_last[:, :S5_STATES].reshape(1, bsz, SSM_GROUPS, SSM_STATE)
    new_im = h_last[:, S5_STATES:].reshape(1, bsz, SSM_GROUPS, SSM_STATE)
    k, v = per_seq(k), per_seq(v)
    yb = _swa(per_seq(q), k, v, past_k.reshape(bsz, WINDOW, KV_WIDTH), past_v.reshape(bsz, WINDOW, KV_WIDTH),
              w["attn_sinks"], start, tq, qg)
    xg = _mix_even_xattn(grouped(x), ya, grouped(yb), w["w_out_even"], w["norm_xattn"], w["xattn_wq"],
                         mem_k, mem_v, w["xattn_wo"], 0, 0, tm_even, nseq)
    keep = min(seq, WINDOW)
    new_k = k[:, seq - keep:].reshape(1, bsz, keep, N_KV_HEADS, HEAD_DIM)
    new_v = v[:, seq - keep:].reshape(1, bsz, keep, N_KV_HEADS, HEAD_DIM)
    return xg, (new_k, new_v, new_re, new_im)


def _layer1_mixer(xg, conv_mix_prev, mem_k, mem_v, w, cfg):
    return _mix_odd_xattn(xg, w["norm_mix"], w["w_in_odd"], w["conv_mix_w"], w["w_out_odd"], conv_mix_prev,
                          w["norm_xattn"], w["xattn_wq"], mem_k, mem_v, w["xattn_wo"], 1, 0, cfg["tm_odd"], cfg["nseq"])


def kernel(x_prompt, x_sample, mem_prompt, cache_win_k, cache_win_v, state_ssm_re, state_ssm_im, state_conv_mix, state_conv_ffn, cache_mem_k, cache_mem_v, norm_mix, norm_xattn, norm_ffn, norm_final, w_in_even, w_out_even, ssm_a_re, ssm_a_im, ssm_log_dt, ssm_b_re, ssm_b_im, ssm_c_re, ssm_c_im, ssm_d, w_glu, attn_sinks, w_in_odd, conv_mix_w, w_out_odd, xattn_wq, xattn_wkv, xattn_wo, ffn_w_up, ffn_conv_w, ffn_w_down):
    bp, seq_p, _ = x_prompt.shape
    bs, seq_s, _ = x_sample.shape
    gains = lambda g: g.reshape(-1, 1, D_MODEL)
    w = dict(norm_mix=gains(norm_mix), norm_xattn=gains(norm_xattn), norm_ffn=gains(norm_ffn),
             norm_final=gains(norm_final),
             w_in_even=w_in_even, w_out_even=w_out_even,
             ssm_d=ssm_d.reshape(1, SSM_WIDTH), w_glu=w_glu.astype(BF16), attn_sinks=attn_sinks.reshape(N_HEADS),
             w_in_odd=w_in_odd, conv_mix_w=conv_mix_w, w_out_odd=w_out_odd,
             xattn_wq=xattn_wq, xattn_wo=xattn_wo,
             ffn_w_up=ffn_w_up.astype(BF16), ffn_conv_w=ffn_conv_w, ffn_w_down=ffn_w_down.astype(BF16))

    gp = (SSM_GROUPS, SSM_STATE)
    a_re, a_im, log_dt = ssm_a_re.reshape(gp), ssm_a_im.reshape(gp), ssm_log_dt.reshape(gp)
    wg, wy = _s5_prep(a_re, a_im, log_dt, ssm_b_re.reshape(gp + (SSM_GROUP,)), ssm_b_im.reshape(gp + (SSM_GROUP,)),
                      ssm_c_re.reshape(SSM_GROUPS, SSM_GROUP, SSM_STATE), ssm_c_im.reshape(SSM_GROUPS, SSM_GROUP, SSM_STATE))
    s5 = dict(wg=wg, wy=wy)
    ni_p, ni_s = S5_SEG_CHUNKS, seq_s // S5_T
    padded = lambda e: e + [1.0] * (-len(e) % SUBLANES)
    pow2, powr_p, powr_s = _s5_pow_table(
        [padded([float(S5_T * ni_p << m) for m in range(3)]),
         padded([float(S5_T * (i + 1)) for i in range(ni_p)]),
         padded([float(S5_T * (i + 1)) for i in range(ni_s)])], a_re, a_im, log_dt)
    cfg_p = dict(tm_in=1024, tm_even=1024, tm_odd=1024, tm_ffn=1024, tq=512, qg=CHUNK, nseq=1, ni=ni_p, chained=True,
                 pow2=pow2, powr=powr_p)
    rows_s = bs * seq_s
    cfg_s = dict(tm_in=rows_s, tm_even=rows_s, tm_odd=rows_s, tm_ffn=rows_s, tq=seq_s, qg=seq_s, nseq=bs, ni=ni_s,
                 chained=False, pow2=pow2, powr=powr_s)

    mem_k_p, mem_v_p, mem_kf_p, mem_vf_p = _mem_kv(mem_prompt, xattn_wkv.astype(BF16))
    cw = CONV_WIDTH - 1
    zk = jnp.zeros((1, bp, WINDOW, N_KV_HEADS, HEAD_DIM), F32)
    zs = jnp.zeros((1, bp, SSM_GROUPS, SSM_STATE), F32)
    zcm = jnp.zeros((1, bp, cw, D_MODEL), F32)
    zcf = jnp.zeros((DEPTH, bp, cw, D_FF), F32)
    mem_kf_s, mem_vf_s = _mem_flat(cache_mem_k, cache_mem_v)
    xp, (k_p, v_p, re_p, im_p) = _layer0_mixer(x_prompt, 0, zk, zk, zs, zs, mem_kf_p, mem_vf_p, w, s5, cfg_p)
    xs, (k_s, v_s, re_s, im_s) = _layer0_mixer(x_sample, PAST_LEN, cache_win_k, cache_win_v, state_ssm_re,
                                               state_ssm_im, mem_kf_s, mem_vf_s, w, s5, cfg_s)
    ffn = functools.partial(_ffn, g=w["norm_ffn"], w_up=w["ffn_w_up"], conv_w=w["ffn_conv_w"], w_down=w["ffn_w_down"],
                            prev_long=zcf, prev_short=state_conv_ffn, g_final=w["norm_final"], tm=cfg_p["tm_ffn"])
    xp, xs, cf0_p, cf0_s = ffn(xp, xs, layer=0, final=False)
    xp, cm_p = _layer1_mixer(xp, zcm, mem_kf_p, mem_vf_p, w, cfg_p)
    xs, cm_s = _layer1_mixer(xs, state_conv_mix, mem_kf_s, mem_vf_s, w, cfg_s)
    y_p, y_s, cf1_p, cf1_s = ffn(xp, xs, layer=1, final=True)
    return (y_p, y_s.reshape(x_sample.shape), k_p, v_p, re_p, im_p, cm_p[None], jnp.stack([cf0_p, cf1_p]),
            mem_k_p, mem_v_p, k_s, v_s, re_s, im_s, cm_s[None], jnp.stack([cf0_s, cf1_s]))
```

```python
import functools
import math

import numpy as np
import jax
import jax.numpy as jnp
from jax import lax
from jax.experimental import pallas as pl
from jax.experimental.pallas import tpu as pltpu

F32 = jnp.float32
BF16 = jnp.bfloat16

D_MODEL = 1024
DEPTH = 2
PAST_LEN = 2048
CHUNK = 64
SSM_WIDTH = 512
SSM_GROUP = 16
SSM_GROUPS = 32
SSM_STATE = 64
HEAD_DIM = 64
N_HEADS = 8
N_KV_HEADS = 2
GQ = N_HEADS // N_KV_HEADS
WINDOW = 128
ROPE_THETA = 10000.0
KV_WIDTH = N_KV_HEADS * HEAD_DIM
EVEN_IN = SSM_WIDTH + N_HEADS * HEAD_DIM + 2 * KV_WIDTH
CONV_WIDTH = 3
N_MEM = 256
X_HEADS = 4
X_HEAD_DIM = 256
D_FF = 2816
EPS = 1e-6
NEG = -1e30

LANES = 128
SUBLANES = 8
S5_T = 8
S5_TILES = SSM_WIDTH // LANES
S5_TILE_STATES = (LANES // SSM_GROUP) * SSM_STATE
S5_STATES = SSM_GROUPS * SSM_STATE
SWA_STAGE_CHUNKS = 1
XATTN_PARTS = 4
S5_SEG_CHUNKS = 16
MXU_DIM = 256
FF_CHUNK = 6 * MXU_DIM
VMEM_LIMIT = 56 * 1024 * 1024


def _const_spec(shape, index=None):
    idx = tuple(index) if index is not None else (0,) * len(shape)
    return pl.BlockSpec(shape, lambda *_: idx, pipeline_mode=pl.Buffered(1))


def _gain_spec(layer):
    return _const_spec((1, 1, D_MODEL), (layer, 0, 0))


def _params(n_axes):
    return pltpu.CompilerParams(dimension_semantics=("arbitrary",) * n_axes,
                                vmem_limit_bytes=VMEM_LIMIT)


def _rms(x, g):
    return x * lax.rsqrt(jnp.mean(x * x, axis=-1, keepdims=True) + EPS) * g


def _dot(a, b):
    return jnp.dot(a, b.astype(BF16), preferred_element_type=F32)


def _dot_t(a, b):
    return lax.dot_general(a, b, (((1,), (1,)), ((), ())), preferred_element_type=F32)


def _rope(t, cos, sina, sinb):
    n = t.shape[1]
    half = HEAD_DIM // 2
    return t * cos + pltpu.roll(t, n - half, 1) * sina + pltpu.roll(t, half, 1) * sinb


def _inproj_even_body(x_ref, g_ref, w_ref, u_ref, q_ref, k_ref, v_ref, cphi_ref, sphi_ref, *, start, period):
    o0 = SSM_WIDTH
    o1 = o0 + N_HEADS * HEAD_DIM
    o2 = o1 + KV_WIDTH
    tm = x_ref.shape[1]
    half = HEAD_DIM // 2
    lane = lax.broadcasted_iota(jnp.int32, (1, LANES), 1)
    inv = jnp.exp((lane & (half - 1)).astype(F32) * (-2.0 / HEAD_DIM * math.log(ROPE_THETA)))
    first = (lane & (HEAD_DIM - 1)) < half

    @pl.when((pl.program_id(0) == 0) & (pl.program_id(1) == 0))
    def _():
        row = lax.broadcasted_iota(jnp.int32, (tm, LANES), 0)
        phi = (row & (period - 1)).astype(F32) * inv
        cphi_ref[...] = jnp.cos(phi)
        sphi_ref[...] = jnp.sin(phi)

    theta = (start + ((pl.program_id(1) * tm) & (period - 1))).astype(F32) * inv
    c_th, s_th = jnp.cos(theta), jnp.sin(theta)
    nparts = 2 if tm % (2 * LANES) == 0 else 1
    parts = [slice(p * tm // nparts, (p + 1) * tm // nparts) for p in range(nparts)]
    projs = [_dot(_rms(x_ref[0, r, :], g_ref[0]).astype(BF16), w_ref[0]) for r in parts]
    for r, proj in zip(parts, projs):
        c_ph, s_ph = cphi_ref[r, :], sphi_ref[r, :]
        cos = c_th * c_ph - s_th * s_ph
        sin = s_th * c_ph + c_th * s_ph
        sa = jnp.where(first, -sin, 0.0)
        sb = jnp.where(first, 0.0, sin)
        u_ref[0, r, :] = proj[:, :o0]
        q = [_rope(proj[:, c:c + LANES], cos, sa, sb) for c in range(o0, o1, LANES)]
        q_ref[0, r, :] = (jnp.concatenate(q, axis=1) * HEAD_DIM ** -0.5).astype(BF16)
        k_ref[0, r, :] = _rope(proj[:, o1:o2], cos, sa, sb)
        v_ref[0, r, :] = proj[:, o2:]


def _inproj_even(x, g, w_in, start, period, layer_e, tm):
    bsz, seq, _ = x.shape
    tok = lambda w: pl.BlockSpec((1, tm, w), lambda b, i: (b, i, 0))
    return pl.pallas_call(
        functools.partial(_inproj_even_body, start=start, period=period),
        grid=(bsz, seq // tm),
        in_specs=[tok(D_MODEL), _gain_spec(2 * layer_e),
                  _const_spec((1, D_MODEL, EVEN_IN), (layer_e, 0, 0))],
        out_specs=[tok(SSM_WIDTH), tok(N_HEADS * HEAD_DIM), tok(KV_WIDTH), tok(KV_WIDTH)],
        out_shape=[jax.ShapeDtypeStruct((bsz, seq, SSM_WIDTH), F32),
                   jax.ShapeDtypeStruct((bsz, seq, N_HEADS * HEAD_DIM), BF16),
                   jax.ShapeDtypeStruct((bsz, seq, KV_WIDTH), F32),
                   jax.ShapeDtypeStruct((bsz, seq, KV_WIDTH), F32)],
        scratch_shapes=[pltpu.VMEM((tm, LANES), F32)] * 2,
        compiler_params=_params(2),
        name="inproj_even",
    )(x, g, w_in)


def _s5_prep_body(lr_ref, li_ref, ld_ref, bre_ref, bim_ref, cre_ref, cim_ref, wg_ref, wy_ref):
    ts = S5_TILE_STATES
    tw = S5_T * LANES

    def times(a_r, a_i):
        return lambda p: (p[0] * a_r - p[1] * a_i, p[0] * a_i + p[1] * a_r)

    lr, li = lr_ref[0], li_ref[0]
    dt = jnp.exp(ld_ref[0])
    mag = jnp.exp(lr * dt)
    ar, ai = mag * jnp.cos(li * dt), mag * jnp.sin(li * dt)
    mul_a = times(ar, ai)
    nrm = lr * lr + li * li
    cbr = ((ar - 1.0) * lr + ai * li) / nrm
    cbi = (ai * lr - (ar - 1.0) * li) / nrm
    b_re, b_im = bre_ref[0], bim_ref[0]
    bbr = cbr * b_re - cbi * b_im
    bbi = cbr * b_im + cbi * b_re
    p = (jnp.ones_like(ar), jnp.zeros_like(ar))
    for s in reversed(range(S5_T)):
        rows = slice(s * LANES, (s + 1) * LANES)
        wg_ref[0, rows, :ts] = (p[0] * bbr - p[1] * bbi).astype(BF16)
        wg_ref[0, rows, ts:] = (p[0] * bbi + p[1] * bbr).astype(BF16)
        for t in range(s):
            wy_ref[0, rows, t * LANES:(t + 1) * LANES] = jnp.zeros((LANES, LANES), BF16)
        p = mul_a(p)
    to_col = lambda row: jnp.transpose(jnp.broadcast_to(row, (LANES, ts)))
    mul_ac = times(to_col(ar), to_col(ai))
    c_re, c_im = cre_ref[0], cim_ref[0]
    bb = jnp.concatenate([bbr, bbi], axis=1).astype(BF16)
    pc = (jnp.ones_like(c_re), jnp.zeros_like(c_re))
    for k in range(S5_T + 1):
        blk = jnp.concatenate([pc[0] * c_re - pc[1] * c_im, -(pc[0] * c_im + pc[1] * c_re)], axis=0).astype(BF16)
        if k >= 1:
            wy_ref[0, tw:, (k - 1) * LANES:k * LANES] = blk
        if k < S5_T:
            lag = _dot(bb, blk).astype(BF16)
            for s in range(S5_T - k):
                t = s + k
                wy_ref[0, s * LANES:(s + 1) * LANES, t * LANES:(t + 1) * LANES] = lag
        pc = mul_ac(pc)


def _s5_prep(a_re, a_im, log_dt, b_re, b_im, c_re, c_im):
    gpt = LANES // SSM_GROUP
    eye = jnp.eye(gpt, dtype=bool)

    def rows(p):
        return p.reshape(S5_TILES, 1, S5_TILE_STATES)

    def bmat(b):
        bt = b.reshape(S5_TILES, gpt, SSM_STATE, SSM_GROUP).transpose(0, 1, 3, 2)
        full = jnp.where(eye[None, :, None, :, None], bt[:, :, :, None, :], 0.0)
        return full.reshape(S5_TILES, LANES, S5_TILE_STATES)

    def cmat(c):
        ct = c.reshape(S5_TILES, gpt, SSM_GROUP, SSM_STATE).transpose(0, 1, 3, 2)
        full = jnp.where(eye[None, :, None, :, None], ct[:, :, :, None, :], 0.0)
        return full.reshape(S5_TILES, S5_TILE_STATES, LANES)

    row_spec = pl.BlockSpec((1, 1, S5_TILE_STATES), lambda j: (j, 0, 0))
    b_spec = pl.BlockSpec((1, LANES, S5_TILE_STATES), lambda j: (j, 0, 0))
    c_spec = pl.BlockSpec((1, S5_TILE_STATES, LANES), lambda j: (j, 0, 0))
    tw = S5_T * LANES
    return pl.pallas_call(
        _s5_prep_body,
        grid=(S5_TILES,),
        in_specs=[row_spec] * 3 + [b_spec] * 2 + [c_spec] * 2,
        out_specs=[pl.BlockSpec((1, tw, 2 * S5_TILE_STATES), lambda j: (j, 0, 0)),
                   pl.BlockSpec((1, tw + 2 * S5_TILE_STATES, tw), lambda j: (j, 0, 0))],
        out_shape=[jax.ShapeDtypeStruct((S5_TILES, tw, 2 * S5_TILE_STATES), BF16),
                   jax.ShapeDtypeStruct((S5_TILES, tw + 2 * S5_TILE_STATES, tw), BF16)],
        compiler_params=_params(1),
        name="s5_prep",
    )(rows(a_re), rows(a_im), rows(log_dt), bmat(b_re), bmat(b_im), cmat(c_re), cmat(c_im))


def _s5_pow_body(ex_ref, lr_ref, li_ref, ld_ref, *o_refs):
    dt = jnp.exp(ld_ref[...])
    zr, zi = lr_ref[...] * dt, li_ref[...] * dt
    lo = 0
    for o_ref in o_refs:
        ex = ex_ref[lo:lo + o_ref.shape[0], :]
        lo += o_ref.shape[0]
        mag = jnp.exp(ex * zr)
        o_ref[:, :S5_STATES] = mag * jnp.cos(ex * zi)
        o_ref[:, S5_STATES:] = mag * jnp.sin(ex * zi)


def _s5_pow_table(groups, a_re, a_im, log_dt):
    sizes = [len(g) for g in groups]
    ex = jnp.asarray(np.asarray(sum(groups, []), np.float32)[:, None])
    flat = lambda p: p.reshape(1, S5_STATES)
    whole = lambda n, w: pl.BlockSpec((n, w), lambda i: (0, 0))
    return pl.pallas_call(
        _s5_pow_body,
        grid=(1,),
        in_specs=[whole(sum(sizes), 1)] + [whole(1, S5_STATES)] * 3,
        out_specs=[whole(n, 2 * S5_STATES) for n in sizes],
        out_shape=[jax.ShapeDtypeStruct((n, 2 * S5_STATES), F32) for n in sizes],
        compiler_params=_params(1),
        name="s5_pow_table",
    )(ex, flat(a_re), flat(a_im), flat(log_dt))


def _s5_block(u_ref, h0_ref, wg_ref, wy_ref, pow2_ref, powr_ref, d_ref, wglu_ref,
              y_ref, hout_ref, carry_ref, us_ref, ys_ref, *, ni, chained, first):
    ns = S5_STATES
    ts = S5_TILE_STATES
    segtok = ni * S5_T
    pitch = _s5_pitch(ni)

    if chained:
        @pl.when(first)
        def _():
            carry_ref[...] = h0_ref[0]
    else:
        carry_ref[...] = h0_ref[0]

    for j in range(S5_TILES):
        for s in range(SUBLANES):
            us_ref[j, s * pitch:s * pitch + segtok] = u_ref[0, s * segtok:(s + 1) * segtok, j * LANES:(j + 1) * LANES]

    def gather(j, t):
        return jnp.concatenate([us_ref[j, pl.ds(S5_T * i + t, SUBLANES, stride=pitch), :]
                                for i in range(ni)], axis=0)

    def fma(p_r, p_i, x_r, x_i, y_r, y_i):
        return y_r + p_r * x_r - p_i * x_i, y_i + p_r * x_i + p_i * x_r

    xs, g_re, g_im = [], [], []
    for j in range(S5_TILES):
        xj = jnp.concatenate([gather(j, s) for s in range(S5_T)], axis=1).astype(BF16)
        g = _dot(xj, wg_ref[j])
        xs.append(xj)
        g_re.append(g[:, :ts])
        g_im.append(g[:, ts:])
    g_re = jnp.concatenate(g_re, axis=1)
    g_im = jnp.concatenate(g_im, axis=1)
    slab = lambda a, i: a[i * SUBLANES:(i + 1) * SUBLANES]
    power = lambda ref, r: (ref[r:r + 1, :ns], ref[r:r + 1, ns:])
    a_r, a_i = power(powr_ref, 0)
    cin = carry_ref[...]
    h_r, h_i = cin[:, :ns], cin[:, ns:]
    prev = []
    if chained:
        local = [(slab(g_re, 0), slab(g_im, 0))]
        for i in range(1, ni):
            local.append(fma(a_r, a_i, *local[-1], slab(g_re, i), slab(g_im, i)))
        sub = lax.broadcasted_iota(jnp.int32, (SUBLANES, 1), 0)
        c_r = jnp.where(sub == 0, h_r, pltpu.roll(local[-1][0], 1, 0))
        c_i = jnp.where(sub == 0, h_i, pltpu.roll(local[-1][1], 1, 0))
        for m in range(3):
            k = 1 << m
            s_r = jnp.where(sub >= k, pltpu.roll(c_r, k, 0), 0.0)
            s_i = jnp.where(sub >= k, pltpu.roll(c_i, k, 0), 0.0)
            c_r, c_i = fma(*power(pow2_ref, m), s_r, s_i, c_r, c_i)
        h_r, h_i = c_r, c_i
        for i in range(ni):
            prev.append((h_r, h_i))
            h_r, h_i = fma(*power(powr_ref, i), c_r, c_i, *local[i])
        last = jnp.concatenate([h_r[SUBLANES - 1:], h_i[SUBLANES - 1:]], axis=1)
    else:
        for i in range(ni):
            prev.append((h_r, h_i))
            h_r, h_i = fma(a_r, a_i, h_r, h_i, slab(g_re, i), slab(g_im, i))
        last = jnp.concatenate([h_r, h_i], axis=1)
    carry_ref[...] = last
    hout_ref[0] = last
    prev_re = jnp.concatenate([p[0] for p in prev], axis=0)
    prev_im = jnp.concatenate([p[1] for p in prev], axis=0)
    y_tiles = []
    for j in range(S5_TILES):
        lhs = jnp.concatenate([xs[j], prev_re[:, j * ts:(j + 1) * ts].astype(BF16),
                               prev_im[:, j * ts:(j + 1) * ts].astype(BF16)], axis=1)
        y_tiles.append(_dot(lhs, wy_ref[j]))
    d = d_ref[...]
    wglu = wglu_ref[0]
    def gelu_in(t):
        ut = jnp.concatenate([gather(j, t) for j in range(S5_TILES)], axis=1)
        yt = jnp.concatenate([y_tiles[j][:, t * LANES:(t + 1) * LANES] for j in range(S5_TILES)], axis=1)
        return jax.nn.gelu(yt + d * ut)

    def scatter(t, o):
        for j in range(S5_TILES):
            for i in range(ni):
                ys_ref[j, pl.ds(S5_T * i + t, SUBLANES, stride=pitch), :] = slab(o, i)[:, j * LANES:(j + 1) * LANES]

    g_q, z_q = {}, {}
    for step in range(S5_T + 2):
        if step < S5_T:
            g_q[step] = gelu_in(step)
        if 0 <= step - 1 < S5_T:
            z_q[step - 1] = _dot(g_q[step - 1].astype(BF16), wglu)
        if 0 <= step - 2 < S5_T:
            scatter(step - 2, g_q.pop(step - 2) * jax.nn.sigmoid(z_q.pop(step - 2)))
    for j in range(S5_TILES):
        for s in range(SUBLANES):
            y_ref[0, s * segtok:(s + 1) * segtok, j * LANES:(j + 1) * LANES] = (
                ys_ref[j, s * pitch:s * pitch + segtok].astype(BF16))


def _s5_pitch(ni):
    segtok = ni * S5_T
    return segtok + SUBLANES if (segtok // SUBLANES) % 2 == 0 else segtok


def _s5_body(ul_ref, ush_ref, h0l_ref, h0s_ref, wg_ref, wy_ref, pow2_ref, powrl_ref, powrs_ref, d_ref, wglu_ref,
             yl_ref, ysh_ref, houtl_ref, houts_ref, carryl_ref, carrys_ref, us_ref, ys_ref,
             *, ni_long, ni_short, nt, nlong):
    s = pl.program_id(0)
    shared = (wg_ref, wy_ref, pow2_ref)

    @pl.when(s < nlong)
    def _():
        _s5_block(ul_ref, h0l_ref, *shared, powrl_ref, d_ref, wglu_ref, yl_ref, houtl_ref, carryl_ref, us_ref, ys_ref,
                  ni=ni_long, chained=True, first=lax.rem(s, nt) == 0)

    @pl.when(s == nlong)
    def _():
        _s5_block(ush_ref, h0s_ref, *shared, powrs_ref, d_ref, wglu_ref, ysh_ref, houts_ref, carrys_ref, us_ref, ys_ref,
                  ni=ni_short, chained=False, first=None)


def _s5_mixer(u_long, u_short, h0_long, h0_short, wg, wy, pow2, powr_long, powr_short, d, w_glu, layer_e,
              ni_long, ni_short):
    nb, ntok, width = u_long.shape
    tb = SUBLANES * ni_long * S5_T
    nt = ntok // tb
    nlong = nb * nt
    assert u_short.shape[1] == SUBLANES * ni_short * S5_T and ni_short <= ni_long
    lb = lambda s: jnp.minimum(s, nlong - 1) // nt
    li = lambda s: jnp.minimum(s, nlong - 1) % nt
    blk_l = pl.BlockSpec((1, tb, width), lambda s: (lb(s), li(s), 0))
    blk_s = pl.BlockSpec(u_short.shape, lambda s: (0, 0, 0))
    st_l = pl.BlockSpec((1, 1, 2 * S5_STATES), lambda s: (lb(s), 0, 0))
    st_s = pl.BlockSpec(h0_short.shape, lambda s: (0, 0, 0))
    return pl.pallas_call(
        functools.partial(_s5_body, ni_long=ni_long, ni_short=ni_short, nt=nt, nlong=nlong),
        grid=(nlong + 1,),
        in_specs=[blk_l, blk_s, st_l, st_s, _const_spec(wg.shape), _const_spec(wy.shape),
                  _const_spec(pow2.shape), _const_spec(powr_long.shape), _const_spec(powr_short.shape),
                  _const_spec((1, SSM_WIDTH)), _const_spec((1, SSM_WIDTH, SSM_WIDTH), (layer_e, 0, 0))],
        out_specs=[blk_l, blk_s, st_l, st_s],
        out_shape=[jax.ShapeDtypeStruct(u_long.shape, BF16), jax.ShapeDtypeStruct(u_short.shape, BF16),
                   jax.ShapeDtypeStruct(h0_long.shape, F32), jax.ShapeDtypeStruct(h0_short.shape, F32)],
        scratch_shapes=[pltpu.VMEM(h0_long.shape[1:], F32), pltpu.VMEM(h0_short.shape[1:], F32)]
        + [pltpu.VMEM((S5_TILES, SUBLANES * _s5_pitch(ni_long), LANES), F32)] * 2,
        compiler_params=_params(1),
        name="s5_mixer",
    )(u_long, u_short, h0_long, h0_short, wg, wy, pow2, powr_long, powr_short, d, w_glu)


def _swa_body(sink_ref, q_ref, kc_ref, vc_ref, kp_ref, vp_ref, pk_ref, pv_ref, o_ref, *, tq, qg, start):
    first = pl.program_id(1) == 0
    k_prev = jnp.where(first, pk_ref[0], kp_ref[0])
    v_prev = jnp.where(first, pv_ref[0], vp_ref[0])
    k_all = jnp.concatenate([k_prev, kc_ref[0]], axis=0)
    v_all = jnp.concatenate([v_prev, vc_ref[0]], axis=0)
    nk = WINDOW + qg
    lane = lax.broadcasted_iota(jnp.int32, (1, KV_WIDTH), 1)
    slot = lax.broadcasted_iota(jnp.int32, (1, nk), 1)
    qrow = lax.broadcasted_iota(jnp.int32, (2 * qg, 1), 0)
    lower = lane < HEAD_DIM
    placed = []
    for kh in range(N_KV_HEADS):
        own = lower if kh == 0 else jnp.logical_not(lower)
        kz, vz = jnp.where(own, k_all, 0.0), jnp.where(own, v_all, 1.0)
        kr, vr = pltpu.roll(kz, HEAD_DIM, 1), pltpu.roll(vz, HEAD_DIM, 1)
        k_lo, k_hi = (kz, kr) if kh == 0 else (kr, kz)
        v_lo, v_hi = (vz, vr) if kh == 0 else (vr, vz)
        placed.append(((k_lo.astype(BF16), v_lo.astype(BF16)), (k_hi.astype(BF16), v_hi.astype(BF16))))
    combos = [(kh, par) for kh in range(N_KV_HEADS) for par in range(2)]

    def scores(c):
        qc = q_ref[0, c * qg:(c + 1) * qg, :]
        out = []
        for kh, par in combos:
            base = kh * GQ * HEAD_DIM
            qs = jnp.concatenate([qc[:, base:base + LANES], qc[:, base + LANES:base + 2 * LANES]], axis=0)
            out.append(_dot_t(qs, placed[kh][par][0][c * qg:c * qg + nk]))
        return out

    def softmax(c, ss):
        n_bad = jnp.where(first, WINDOW - start - c * qg, 0) if c * qg < WINDOW - start else None
        probs, sinks = [], []
        for (kh, par), s in zip(combos, ss):
            if n_bad is not None:
                s = jnp.where(slot >= n_bad, s, NEG)
            sink = jnp.where(qrow < qg, sink_ref[kh * GQ + par], sink_ref[kh * GQ + 2 + par])
            m = jnp.maximum(jnp.max(s, axis=1, keepdims=True), sink)
            probs.append(jnp.exp(s - m).astype(BF16))
            sinks.append(jnp.exp(sink - m))
        return probs, sinks

    def finish(c, probs, sinks):
        pvs = [_dot(p, placed[kh][par][1][c * qg:c * qg + nk]) for (kh, par), p in zip(combos, probs)]
        outs = []
        for kh in range(N_KV_HEADS):
            pv_e, pv_o = pvs[2 * kh], pvs[2 * kh + 1]
            num = jnp.where(lower, pv_e, pv_o)
            den = pltpu.roll(jnp.where(lower, pv_o, pv_e), HEAD_DIM, 1) + jnp.where(lower, sinks[2 * kh], sinks[2 * kh + 1])
            acc = num / den
            outs += [acc[:qg], acc[qg:]]
        o_ref[0, c * qg:(c + 1) * qg, :] = jnp.concatenate(outs, axis=1).astype(BF16)

    nchunk = tq // qg
    per = min(SWA_STAGE_CHUNKS, nchunk)
    groups = [range(g, min(g + per, nchunk)) for g in range(0, nchunk, per)]
    s_q, p_q = {}, {}
    for step in range(len(groups) + 2):
        if step < len(groups):
            s_q[step] = [scores(c) for c in groups[step]]
        if 0 <= step - 1 < len(groups):
            p_q[step - 1] = [softmax(c, ss) for c, ss in zip(groups[step - 1], s_q.pop(step - 1))]
        if 0 <= step - 2 < len(groups):
            for c, (probs, sinks) in zip(groups[step - 2], p_q.pop(step - 2)):
                finish(c, probs, sinks)


def _swa(q, k, v, past_k, past_v, sinks, start, tq, qg):
    bsz, seq, _ = q.shape
    ntiles = seq // tq
    tok = lambda w: pl.BlockSpec((1, tq, w), lambda b, i: (b, i, 0))
    past = pl.BlockSpec((1, WINDOW, KV_WIDTH), lambda b, i: (b, 0, 0))
    if ntiles > 1:
        per = tq // WINDOW
        prev = pl.BlockSpec((1, WINDOW, KV_WIDTH), lambda b, i: (b, jnp.maximum(i * per - 1, 0), 0))
        k_prev, v_prev = k, v
    else:
        prev, k_prev, v_prev = past, past_k, past_v
    return pl.pallas_call(
        functools.partial(_swa_body, tq=tq, qg=qg, start=start),
        grid=(bsz, ntiles),
        in_specs=[pl.BlockSpec(memory_space=pltpu.SMEM), tok(N_HEADS * HEAD_DIM), tok(KV_WIDTH), tok(KV_WIDTH),
                  prev, prev, past, past],
        out_specs=tok(N_HEADS * HEAD_DIM),
        out_shape=jax.ShapeDtypeStruct((bsz, seq, N_HEADS * HEAD_DIM), BF16),
        compiler_params=_params(2),
        name="swa",
    )(sinks, q, k, v, k_prev, v_prev, past_k, past_v)


def _xattn(x1, g, wq, mk_ref, mv_ref, wo, nseq, o_ref):
    h = _rms(x1, g).astype(BF16)
    q = (_dot(h, wq) * X_HEAD_DIM ** -0.5).astype(BF16)
    nparts = nseq if nseq > 1 else XATTN_PARTS
    prows = x1.shape[0] // nparts
    heads = [slice(hd * X_HEAD_DIM, (hd + 1) * X_HEAD_DIM) for hd in range(X_HEADS)]

    def scores(p):
        r, b = slice(p * prows, (p + 1) * prows), (p if nseq > 1 else 0)
        return [_dot_t(q[r, c], mk_ref[0, b, :, c]) for c in heads]

    def softmax(ss):
        out = []
        for s in ss:
            e = jnp.exp(s - jnp.max(s, axis=1, keepdims=True))
            out.append((e.astype(BF16), jnp.sum(e, axis=1, keepdims=True)))
        return out

    def values(p, pd):
        b = p if nseq > 1 else 0
        return jnp.concatenate([_dot(e, mv_ref[0, b, :, c]) / den for c, (e, den) in zip(heads, pd)],
                               axis=1).astype(BF16)

    def finish(p, pd):
        r = slice(p * prows, (p + 1) * prows)
        o_ref[0, r, :] = x1[r] + _dot(values(p, pd), wo)

    if nseq > 1:
        pds = [softmax(ss) for ss in [scores(p) for p in range(nparts)]]
        o = jnp.concatenate([values(p, pd) for p, pd in enumerate(pds)], axis=0)
        o_ref[0] = x1 + _dot(o, wo)
        return
    s_q, p_q = {}, {}
    for step in range(nparts + 2):
        if step < nparts:
            s_q[step] = scores(step)
        if 0 <= step - 1 < nparts:
            p_q[step - 1] = softmax(s_q.pop(step - 1))
        if 0 <= step - 2 < nparts:
            finish(step - 2, p_q.pop(step - 2))


def _conv3(cur, carry, w):
    n = cur.shape[0]
    row = lax.broadcasted_iota(jnp.int32, (n, 1), 0)
    nc = carry.shape[0]
    c1, c2 = carry[nc - 1:nc], carry[nc - 2:nc - 1]
    m1 = jnp.where(row == 0, c1, pltpu.roll(cur, 1, 0))
    m2 = jnp.where(row == 0, c2, jnp.where(row == 1, c1, pltpu.roll(cur, 2, 0)))
    return w[0:1] * m2 + w[1:2] * m1 + w[2:3] * cur


def _conv3_seqs(cur, carries, w):
    nseq = len(carries)
    if nseq == 1:
        return _conv3(cur, carries[0], w)
    rows = cur.shape[0] // nseq
    body = _conv3(cur, carries[0], w)
    pieces = []
    for b in range(nseq):
        pieces.append(_conv3(cur[b * rows:b * rows + SUBLANES], carries[b], w))
        pieces.append(body[b * rows + SUBLANES:(b + 1) * rows])
    return jnp.concatenate(pieces, axis=0)


def _store_last_rows(st_ref, cols, cur, nseq):
    rows = cur.shape[0] // nseq
    for b in range(nseq):
        st_ref[b, :, cols] = cur[(b + 1) * rows - (CONV_WIDTH - 1):(b + 1) * rows]


def _mix_even_tile(x_ref, ya_ref, yb_ref, wout_ref, g_ref, wq_ref, mk_ref, mv_ref, wo_ref, o_ref, *, nseq):
    x1 = x_ref[0] + _dot(jnp.concatenate([ya_ref[0], yb_ref[0]], axis=1), wout_ref[0])
    _xattn(x1, g_ref[0], wq_ref[0], mk_ref, mv_ref, wo_ref[0], nseq, o_ref)


def _mix_even_xattn_body(xl_ref, xs_ref, yal_ref, yas_ref, ybl_ref, ybs_ref, wout_ref, g_ref, wq_ref,
                         mkl_ref, mks_ref, mvl_ref, mvs_ref, wo_ref, ol_ref, os_ref, *, nlong, nseq_s):
    s = pl.program_id(0)

    @pl.when(s < nlong)
    def _():
        _mix_even_tile(xl_ref, yal_ref, ybl_ref, wout_ref, g_ref, wq_ref, mkl_ref, mvl_ref, wo_ref, ol_ref, nseq=1)

    @pl.when(s == nlong)
    def _():
        _mix_even_tile(xs_ref, yas_ref, ybs_ref, wout_ref, g_ref, wq_ref, mks_ref, mvs_ref, wo_ref, os_ref, nseq=nseq_s)


def _long_short_specs(x_long, x_short, tm):
    nb, seq, _ = x_long.shape
    nt = seq // tm
    nlong = nb * nt
    lb = lambda s: jnp.minimum(s, nlong - 1) // nt
    li = lambda s: jnp.minimum(s, nlong - 1) % nt
    tok_long = lambda w: pl.BlockSpec((1, tm, w), lambda s: (lb(s), li(s), 0))
    tok_short = lambda w: pl.BlockSpec((1, x_short.shape[1], w), lambda s: (0, 0, 0))
    return nlong, nt, lb, tok_long, tok_short


def _mix_even_xattn(x_long, x_short, ya_long, ya_short, yb_long, yb_short, w_out, g, wq, mem_long, mem_short, wo,
                    layer, layer_e, tm):
    nlong, _, lb, tok_l, tok_s = _long_short_specs(x_long, x_short, tm)
    nseq_s = mem_short[0].shape[1]
    mem_l = pl.BlockSpec((1, 1, N_MEM, D_MODEL), lambda s: (layer, lb(s), 0, 0))
    mem_s = _const_spec((1, nseq_s, N_MEM, D_MODEL), (layer, 0, 0, 0))
    half = D_MODEL // 2
    sq = lambda l: _const_spec((1, D_MODEL, D_MODEL), (l, 0, 0))
    return pl.pallas_call(
        functools.partial(_mix_even_xattn_body, nlong=nlong, nseq_s=nseq_s),
        grid=(nlong + 1,),
        in_specs=[tok_l(D_MODEL), tok_s(D_MODEL), tok_l(half), tok_s(half), tok_l(half), tok_s(half),
                  sq(layer_e), _gain_spec(layer), sq(layer), mem_l, mem_s, mem_l, mem_s, sq(layer)],
        out_specs=[tok_l(D_MODEL), tok_s(D_MODEL)],
        out_shape=[jax.ShapeDtypeStruct(x_long.shape, F32), jax.ShapeDtypeStruct(x_short.shape, F32)],
        compiler_params=_params(1),
        name="mix_even_xattn",
    )(x_long, x_short, ya_long, ya_short, yb_long, yb_short, w_out, g, wq,
      mem_long[0], mem_short[0], mem_long[1], mem_short[1], wo)


def _mix_odd_tile(x_ref, gm_ref, win_ref, cw_ref, wout_ref, prev_ref, g_ref, wq_ref, mk_ref, mv_ref, wo_ref,
                  o_ref, st_ref, carry_ref, *, tm, nseq, first):
    keep = CONV_WIDTH - 1
    if nseq == 1:
        @pl.when(first)
        def _():
            carry_ref[SUBLANES - keep:, :] = prev_ref[0, 0]

    x = x_ref[0]
    proj = _dot(_rms(x, gm_ref[0]).astype(BF16), win_ref[0])
    gate_b, gate_c, z = proj[:, :D_MODEL], proj[:, D_MODEL:2 * D_MODEL], proj[:, 2 * D_MODEL:]
    cz = gate_c * z
    carries = [carry_ref[SUBLANES - keep:, :]] if nseq == 1 else [prev_ref[0, b] for b in range(nseq)]
    z_conv = _conv3_seqs(cz, carries, cw_ref[0])
    if nseq == 1:
        carry_ref[...] = cz[tm - SUBLANES:]
    _store_last_rows(st_ref, slice(None), cz, nseq)
    x1 = x + _dot((gate_b * z_conv).astype(BF16), wout_ref[0])
    _xattn(x1, g_ref[0], wq_ref[0], mk_ref, mv_ref, wo_ref[0], nseq, o_ref)


def _mix_odd_xattn_body(xl_ref, xs_ref, gm_ref, win_ref, cw_ref, wout_ref, prevl_ref, prevs_ref, g_ref, wq_ref,
                        mkl_ref, mks_ref, mvl_ref, mvs_ref, wo_ref, ol_ref, os_ref, stl_ref, sts_ref, carry_ref,
                        *, tm, nt, nlong, rows_s, nseq_s):
    s = pl.program_id(0)
    front = (gm_ref, win_ref, cw_ref, wout_ref)

    @pl.when(s < nlong)
    def _():
        _mix_odd_tile(xl_ref, *front, prevl_ref, g_ref, wq_ref, mkl_ref, mvl_ref, wo_ref, ol_ref, stl_ref, carry_ref,
                      tm=tm, nseq=1, first=lax.rem(s, nt) == 0)

    @pl.when(s == nlong)
    def _():
        _mix_odd_tile(xs_ref, *front, prevs_ref, g_ref, wq_ref, mks_ref, mvs_ref, wo_ref, os_ref, sts_ref, carry_ref,
                      tm=rows_s, nseq=nseq_s, first=None)


def _mix_odd_xattn(x_long, x_short, gm, w_in, conv_w, w_out, prev_long, prev_short, g, wq, mem_long, mem_short, wo,
                   layer, layer_o, tm):
    nlong, nt, lb, tok_l, tok_s = _long_short_specs(x_long, x_short, tm)
    nseq_s = prev_short.shape[1]
    keep = CONV_WIDTH - 1
    mem_l = pl.BlockSpec((1, 1, N_MEM, D_MODEL), lambda s: (layer, lb(s), 0, 0))
    mem_s = _const_spec((1, nseq_s, N_MEM, D_MODEL), (layer, 0, 0, 0))
    hist_l = pl.BlockSpec((1, 1, keep, D_MODEL), lambda s: (layer_o, lb(s), 0, 0))
    hist_s = pl.BlockSpec((1, nseq_s, keep, D_MODEL), lambda s: (layer_o, 0, 0, 0))
    st_l = pl.BlockSpec((1, keep, D_MODEL), lambda s: (lb(s), 0, 0))
    st_s = pl.BlockSpec((nseq_s, keep, D_MODEL), lambda s: (0, 0, 0))
    sq = lambda l: _const_spec((1, D_MODEL, D_MODEL), (l, 0, 0))
    return pl.pallas_call(
        functools.partial(_mix_odd_xattn_body, tm=tm, nt=nt, nlong=nlong, rows_s=x_short.shape[1], nseq_s=nseq_s),
        grid=(nlong + 1,),
        in_specs=[tok_l(D_MODEL), tok_s(D_MODEL), _gain_spec(layer),
                  _const_spec((1, D_MODEL, 3 * D_MODEL), (layer_o, 0, 0)),
                  _const_spec((1, CONV_WIDTH, D_MODEL), (layer_o, 0, 0)), sq(layer_o), hist_l, hist_s,
                  _gain_spec(layer), sq(layer), mem_l, mem_s, mem_l, mem_s, sq(layer)],
        out_specs=[tok_l(D_MODEL), tok_s(D_MODEL), st_l, st_s],
        out_shape=[jax.ShapeDtypeStruct(x_long.shape, F32), jax.ShapeDtypeStruct(x_short.shape, F32),
                   jax.ShapeDtypeStruct((x_long.shape[0], keep, D_MODEL), F32),
                   jax.ShapeDtypeStruct((nseq_s, keep, D_MODEL), F32)],
        scratch_shapes=[pltpu.VMEM((SUBLANES, D_MODEL), F32)],
        compiler_params=_params(1),
        name="mix_odd_xattn",
    )(x_long, x_short, gm, w_in, conv_w, w_out, prev_long, prev_short, g, wq,
      mem_long[0], mem_short[0], mem_long[1], mem_short[1], wo)


def _ffn_tile(x_ref, g_ref, wup_ref, cw_ref, wdn_ref, prev_ref, gfin_ref, o_ref, st_ref, carry_ref,
              *, tm, final, nseq, first):
    if nseq == 1:
        @pl.when(first)
        def _():
            carry_ref[SUBLANES - (CONV_WIDTH - 1):, :] = prev_ref[0, 0]

    x = x_ref[0]
    h = _rms(x, g_ref[0]).astype(BF16)
    chunks = [slice(lo, min(lo + FF_CHUNK, D_FF)) for lo in range(0, D_FF, FF_CHUNK)]

    def up(cols):
        return _dot(h, wup_ref[0, :, cols]), _dot(h, wup_ref[0, :, D_FF + cols.start:D_FF + cols.stop])

    acc = x
    nxt = up(chunks[0])
    for n, cols in enumerate(chunks):
        (gate, val), nxt = nxt, (up(chunks[n + 1]) if n + 1 < len(chunks) else None)
        carries = ([carry_ref[SUBLANES - (CONV_WIDTH - 1):, cols]] if nseq == 1
                   else [prev_ref[0, b, :, cols] for b in range(nseq)])
        gate_c = _conv3_seqs(gate, carries, cw_ref[0, :, cols])
        if nseq == 1:
            carry_ref[:, cols] = gate[tm - SUBLANES:]
        _store_last_rows(st_ref, cols, gate, nseq)
        act = (gate_c * jax.nn.sigmoid(gate_c) * val).astype(BF16)
        if n + 1 < len(chunks):
            acc = acc + _dot(act, wdn_ref[0, cols, :])
        else:
            half = tm // 2
            for rows in (slice(0, half), slice(half, tm)):
                out = acc[rows] + _dot(act[rows], wdn_ref[0, cols, :])
                o_ref[0, rows, :] = _rms(out, gfin_ref[0]) if final else out


def _ffn_body(xl_ref, xs_ref, g_ref, wup_ref, cw_ref, wdn_ref, prevl_ref, prevs_ref, gfin_ref,
              ol_ref, os_ref, stl_ref, sts_ref, carry_ref, *, tm, nt, nlong, rows_s, nseq_s, final):
    s = pl.program_id(0)
    shared = (g_ref, wup_ref, cw_ref, wdn_ref)

    @pl.when(s < nlong)
    def _():
        _ffn_tile(xl_ref, *shared, prevl_ref, gfin_ref, ol_ref, stl_ref, carry_ref,
                  tm=tm, final=final, nseq=1, first=lax.rem(s, nt) == 0)

    @pl.when(s == nlong)
    def _():
        _ffn_tile(xs_ref, *shared, prevs_ref, gfin_ref, os_ref, sts_ref, carry_ref,
                  tm=rows_s, final=final, nseq=nseq_s, first=None)


def _ffn(x_long, x_short, g, w_up, conv_w, w_down, prev_long, prev_short, g_final, layer, tm, final):
    nb, seq, _ = x_long.shape
    rows_s = x_short.shape[1]
    nseq_s = prev_short.shape[1]
    nt = seq // tm
    nlong = nb * nt
    lb = lambda s: jnp.minimum(s, nlong - 1) // nt
    li = lambda s: jnp.minimum(s, nlong - 1) % nt
    keep = CONV_WIDTH - 1
    tok_l = pl.BlockSpec((1, tm, D_MODEL), lambda s: (lb(s), li(s), 0))
    tok_s = pl.BlockSpec((1, rows_s, D_MODEL), lambda s: (0, 0, 0))
    hist_l = pl.BlockSpec((1, 1, keep, D_FF), lambda s: (layer, lb(s), 0, 0))
    hist_s = pl.BlockSpec((1, nseq_s, keep, D_FF), lambda s: (layer, 0, 0, 0))
    st_l = pl.BlockSpec((1, keep, D_FF), lambda s: (lb(s), 0, 0))
    st_s = pl.BlockSpec((nseq_s, keep, D_FF), lambda s: (0, 0, 0))
    return pl.pallas_call(
        functools.partial(_ffn_body, tm=tm, nt=nt, nlong=nlong, rows_s=rows_s, nseq_s=nseq_s, final=final),
        grid=(nlong + 1,),
        in_specs=[tok_l, tok_s, _gain_spec(layer), _const_spec((1, D_MODEL, 2 * D_FF), (layer, 0, 0)),
                  _const_spec((1, CONV_WIDTH, D_FF), (layer, 0, 0)), _const_spec((1, D_FF, D_MODEL), (layer, 0, 0)),
                  hist_l, hist_s, _gain_spec(0)],
        out_specs=[tok_l, tok_s, st_l, st_s],
        out_shape=[jax.ShapeDtypeStruct(x_long.shape, F32), jax.ShapeDtypeStruct(x_short.shape, F32),
                   jax.ShapeDtypeStruct((nb, keep, D_FF), F32), jax.ShapeDtypeStruct((nseq_s, keep, D_FF), F32)],
        scratch_shapes=[pltpu.VMEM((SUBLANES, D_FF), F32)],
        compiler_params=_params(1),
        name="conv_ffn",
    )(x_long, x_short, g, w_up, conv_w, w_down, prev_long, prev_short, g_final)


def _mem_flat_copies(k_hbm, v_hbm, kbuf, vbuf, sem, step, slot, nbatch):
    lyr, b = step // nbatch, step % nbatch
    return [pltpu.make_async_copy(src.at[lyr, b, :, hd, :], buf.at[slot, hd], sem.at[slot, i, hd])
            for i, (src, buf) in enumerate(((k_hbm, kbuf), (v_hbm, vbuf))) for hd in range(X_HEADS)]


def _mem_flat_body(k_hbm, v_hbm, ko_ref, vo_ref, kbuf, vbuf, sem, *, nbatch):
    s, n = pl.program_id(0), pl.num_programs(0)
    copies = functools.partial(_mem_flat_copies, k_hbm, v_hbm, kbuf, vbuf, sem, nbatch=nbatch)

    @pl.when(s == 0)
    def _():
        for c in copies(step=s, slot=0):
            c.start()

    @pl.when(s + 1 < n)
    def _():
        for c in copies(step=s + 1, slot=(s + 1) % 2):
            c.start()

    slot = s % 2
    for c in copies(step=s, slot=slot):
        c.wait()
    for hd in range(X_HEADS):
        cols = slice(hd * X_HEAD_DIM, (hd + 1) * X_HEAD_DIM)
        ko_ref[0, 0, :, cols] = kbuf[slot, hd].astype(BF16)
        vo_ref[0, 0, :, cols] = vbuf[slot, hd].astype(BF16)


def _mem_flat(mem_k, mem_v):
    depth, nbatch = mem_k.shape[:2]
    flat = pl.BlockSpec((1, 1, N_MEM, D_MODEL), lambda s: (s // nbatch, s % nbatch, 0, 0))
    return pl.pallas_call(
        functools.partial(_mem_flat_body, nbatch=nbatch),
        grid=(depth * nbatch,),
        in_specs=[pl.BlockSpec(memory_space=pl.ANY)] * 2,
        out_specs=[flat, flat],
        out_shape=[jax.ShapeDtypeStruct((depth, nbatch, N_MEM, D_MODEL), BF16)] * 2,
        scratch_shapes=[pltpu.VMEM((2, X_HEADS, N_MEM, X_HEAD_DIM), F32)] * 2
        + [pltpu.SemaphoreType.DMA((2, 2, X_HEADS))],
        compiler_params=_params(1),
        name="mem_flat",
    )(mem_k, mem_v)


def _mem_kv_body(m_ref, w_ref, k_ref, v_ref, kf_ref, vf_ref):
    kv = _dot(m_ref[0].astype(BF16), w_ref[0])
    kf_ref[0, 0] = kv[:, :D_MODEL].astype(BF16)
    vf_ref[0, 0] = kv[:, D_MODEL:].astype(BF16)
    for hd in range(X_HEADS):
        k_ref[0, 0, :, hd, :] = kv[:, hd * X_HEAD_DIM:(hd + 1) * X_HEAD_DIM]
        v_ref[0, 0, :, hd, :] = kv[:, D_MODEL + hd * X_HEAD_DIM:D_MODEL + (hd + 1) * X_HEAD_DIM]


def _mem_kv(mem, w_kv):
    bsz = mem.shape[0]
    heads = pl.BlockSpec((1, 1, N_MEM, X_HEADS, X_HEAD_DIM), lambda l, b: (l, b, 0, 0, 0))
    flat = pl.BlockSpec((1, 1, N_MEM, D_MODEL), lambda l, b: (l, b, 0, 0))
    return pl.pallas_call(
        _mem_kv_body,
        grid=(DEPTH, bsz),
        in_specs=[pl.BlockSpec((1, N_MEM, D_MODEL), lambda l, b: (b, 0, 0)),
                  pl.BlockSpec((1, D_MODEL, 2 * D_MODEL), lambda l, b: (l, 0, 0))],
        out_specs=[heads, heads, flat, flat],
        out_shape=[jax.ShapeDtypeStruct((DEPTH, bsz, N_MEM, X_HEADS, X_HEAD_DIM), F32)] * 2
        + [jax.ShapeDtypeStruct((DEPTH, bsz, N_MEM, D_MODEL), BF16)] * 2,
        compiler_params=_params(2),
        name="mem_kv",
    )(mem, w_kv)


def _project_in(x, start, w, cfg):
    bsz, seq, _ = x.shape
    nseq = cfg["nseq"]
    xg = x.reshape(bsz // nseq, nseq * seq, D_MODEL)
    return (xg,) + tuple(_inproj_even(xg, w["norm_mix"], w["w_in_even"], start, seq, 0, cfg["tm_in"]))


def _window_attention(q, k, v, past_k, past_v, bsz, start, w, cfg):
    seq = q.shape[0] * q.shape[1] // bsz
    per_seq = lambda t: t.reshape(bsz, seq, t.shape[-1])
    k, v = per_seq(k), per_seq(v)
    yb = _swa(per_seq(q), k, v, past_k.reshape(bsz, WINDOW, KV_WIDTH), past_v.reshape(bsz, WINDOW, KV_WIDTH),
              w["attn_sinks"], start, cfg["tq"], cfg["qg"])
    keep = min(seq, WINDOW)
    new_k = k[:, seq - keep:].reshape(1, bsz, keep, N_KV_HEADS, HEAD_DIM)
    new_v = v[:, seq - keep:].reshape(1, bsz, keep, N_KV_HEADS, HEAD_DIM)
    return yb.reshape(q.shape), new_k, new_v


def kernel(x_prompt, x_sample, mem_prompt, cache_win_k, cache_win_v, state_ssm_re, state_ssm_im, state_conv_mix, state_conv_ffn, cache_mem_k, cache_mem_v, norm_mix, norm_xattn, norm_ffn, norm_final, w_in_even, w_out_even, ssm_a_re, ssm_a_im, ssm_log_dt, ssm_b_re, ssm_b_im, ssm_c_re, ssm_c_im, ssm_d, w_glu, attn_sinks, w_in_odd, conv_mix_w, w_out_odd, xattn_wq, xattn_wkv, xattn_wo, ffn_w_up, ffn_conv_w, ffn_w_down):
    bp, seq_p, _ = x_prompt.shape
    bs, seq_s, _ = x_sample.shape
    gains = lambda g: g.reshape(-1, 1, D_MODEL)
    w = dict(norm_mix=gains(norm_mix), norm_xattn=gains(norm_xattn), norm_ffn=gains(norm_ffn),
             norm_final=gains(norm_final),
             w_in_even=w_in_even, w_out_even=w_out_even,
             ssm_d=ssm_d.reshape(1, SSM_WIDTH), w_glu=w_glu.astype(BF16), attn_sinks=attn_sinks.reshape(N_HEADS),
             w_in_odd=w_in_odd, conv_mix_w=conv_mix_w, w_out_odd=w_out_odd,
             xattn_wq=xattn_wq, xattn_wo=xattn_wo,
             ffn_w_up=ffn_w_up.astype(BF16), ffn_conv_w=ffn_conv_w, ffn_w_down=ffn_w_down.astype(BF16))

    gp = (SSM_GROUPS, SSM_STATE)
    a_re, a_im, log_dt = ssm_a_re.reshape(gp), ssm_a_im.reshape(gp), ssm_log_dt.reshape(gp)
    wg, wy = _s5_prep(a_re, a_im, log_dt, ssm_b_re.reshape(gp + (SSM_GROUP,)), ssm_b_im.reshape(gp + (SSM_GROUP,)),
                      ssm_c_re.reshape(SSM_GROUPS, SSM_GROUP, SSM_STATE), ssm_c_im.reshape(SSM_GROUPS, SSM_GROUP, SSM_STATE))
    ni_p, ni_s = S5_SEG_CHUNKS, seq_s // S5_T
    padded = lambda e: e + [1.0] * (-len(e) % SUBLANES)
    pow2, powr_p, powr_s = _s5_pow_table(
        [padded([float(S5_T * ni_p << m) for m in range(3)]),
         padded([float(S5_T * (i + 1)) for i in range(ni_p)]),
         padded([float(S5_T * (i + 1)) for i in range(ni_s)])], a_re, a_im, log_dt)
    cfg_p = dict(tm_in=1024, tm_even=1024, tm_odd=512, tm_ffn=1024, tq=512, qg=CHUNK, nseq=1)
    cfg_s = dict(tm_in=bs * seq_s, tq=seq_s, qg=seq_s, nseq=bs)

    mem_k_p, mem_v_p, mem_kf_p, mem_vf_p = _mem_kv(mem_prompt, xattn_wkv.astype(BF16))
    cw = CONV_WIDTH - 1
    zk = jnp.zeros((1, bp, WINDOW, N_KV_HEADS, HEAD_DIM), F32)
    zs = jnp.zeros((1, bp, SSM_GROUPS, SSM_STATE), F32)
    zcm = jnp.zeros((1, bp, cw, D_MODEL), F32)
    zcf = jnp.zeros((DEPTH, bp, cw, D_FF), F32)
    mem_kf_s, mem_vf_s = _mem_flat(cache_mem_k, cache_mem_v)
    mem_p, mem_s = (mem_kf_p, mem_vf_p), (mem_kf_s, mem_vf_s)
    xp, u_p, q_p, kk_p, vv_p = _project_in(x_prompt, 0, w, cfg_p)
    xs, u_s, q_s, kk_s, vv_s = _project_in(x_sample, PAST_LEN, w, cfg_s)
    state = lambda re, im, b: jnp.concatenate([re.reshape(b, S5_STATES), im.reshape(b, S5_STATES)], axis=1)
    ya_p, ya_s, h_p, h_s = _s5_mixer(u_p, u_s, state(zs, zs, bp)[:, None, :], state(state_ssm_re, state_ssm_im, bs)[None],
                                     wg, wy, pow2, powr_p, powr_s, w["ssm_d"], w["w_glu"], 0, ni_p, ni_s)
    split = lambda h, b: (h[:, :S5_STATES].reshape(1, b, SSM_GROUPS, SSM_STATE),
                          h[:, S5_STATES:].reshape(1, b, SSM_GROUPS, SSM_STATE))
    (re_p, im_p), (re_s, im_s) = split(h_p[:, 0], bp), split(h_s[0], bs)
    yb_p, k_p, v_p = _window_attention(q_p, kk_p, vv_p, zk, zk, bp, 0, w, cfg_p)
    yb_s, k_s, v_s = _window_attention(q_s, kk_s, vv_s, cache_win_k, cache_win_v, bs, PAST_LEN, w, cfg_s)
    xp, xs = _mix_even_xattn(xp, xs, ya_p, ya_s, yb_p, yb_s, w["w_out_even"], w["norm_xattn"], w["xattn_wq"],
                             mem_p, mem_s, w["xattn_wo"], 0, 0, cfg_p["tm_even"])
    ffn = functools.partial(_ffn, g=w["norm_ffn"], w_up=w["ffn_w_up"], conv_w=w["ffn_conv_w"], w_down=w["ffn_w_down"],
                            prev_long=zcf, prev_short=state_conv_ffn, g_final=w["norm_final"], tm=cfg_p["tm_ffn"])
    xp, xs, cf0_p, cf0_s = ffn(xp, xs, layer=0, final=False)
    xp, xs, cm_p, cm_s = _mix_odd_xattn(xp, xs, w["norm_mix"], w["w_in_odd"], w["conv_mix_w"], w["w_out_odd"], zcm,
                                        state_conv_mix, w["norm_xattn"], w["xattn_wq"], mem_p, mem_s, w["xattn_wo"],
                                        1, 0, cfg_p["tm_odd"])
    y_p, y_s, cf1_p, cf1_s = ffn(xp, xs, layer=1, final=True)
    return (y_p, y_s.reshape(x_sample.shape), k_p, v_p, re_p, im_p, cm_p[None], jnp.stack([cf0_p, cf1_p]),
            mem_k_p, mem_v_p, k_s, v_s, re_s, im_s, cm_s[None], jnp.stack([cf0_s, cf1_s]))
```

```python
import functools
import math

import numpy as np
import jax
import jax.numpy as jnp
from jax import lax
from jax.experimental import pallas as pl
from jax.experimental.pallas import tpu as pltpu

F32 = jnp.float32
BF16 = jnp.bfloat16

D_MODEL = 1024
DEPTH = 2
PAST_LEN = 2048
CHUNK = 64
SSM_WIDTH = 512
SSM_GROUP = 16
SSM_GROUPS = 32
SSM_STATE = 64
HEAD_DIM = 64
N_HEADS = 8
N_KV_HEADS = 2
GQ = N_HEADS // N_KV_HEADS
WINDOW = 128
ROPE_THETA = 10000.0
KV_WIDTH = N_KV_HEADS * HEAD_DIM
EVEN_IN = SSM_WIDTH + N_HEADS * HEAD_DIM + 2 * KV_WIDTH
CONV_WIDTH = 3
N_MEM = 256
X_HEADS = 4
X_HEAD_DIM = 256
D_FF = 2816
EPS = 1e-6
NEG = -1e30

LANES = 128
SUBLANES = 8
S5_T = 8
S5_TILES = SSM_WIDTH // LANES
S5_TILE_STATES = (LANES // SSM_GROUP) * SSM_STATE
S5_STATES = SSM_GROUPS * SSM_STATE
SWA_STAGE_CHUNKS = 1
XATTN_PARTS = 4
S5_SEG_CHUNKS = 16
MXU_DIM = 256
FF_CHUNK = 6 * MXU_DIM
VMEM_LIMIT = 56 * 1024 * 1024


def _const_spec(shape, index=None):
    idx = tuple(index) if index is not None else (0,) * len(shape)
    return pl.BlockSpec(shape, lambda *_: idx, pipeline_mode=pl.Buffered(1))


def _gain_spec(layer):
    return _const_spec((1, 1, D_MODEL), (layer, 0, 0))


def _params(n_axes):
    return pltpu.CompilerParams(dimension_semantics=("arbitrary",) * n_axes,
                                vmem_limit_bytes=VMEM_LIMIT)


def _rms(x, g):
    return x * lax.rsqrt(jnp.mean(x * x, axis=-1, keepdims=True) + EPS) * g


def _dot(a, b):
    return jnp.dot(a, b.astype(BF16), preferred_element_type=F32)


def _dot_t(a, b):
    return lax.dot_general(a, b, (((1,), (1,)), ((), ())), preferred_element_type=F32)


def _rope(t, cos, sina, sinb):
    n = t.shape[1]
    half = HEAD_DIM // 2
    return t * cos + pltpu.roll(t, n - half, 1) * sina + pltpu.roll(t, half, 1) * sinb


def _inproj_even_body(x_ref, g_ref, w_ref, u_ref, q_ref, k_ref, v_ref, cphi_ref, sphi_ref, *, start, period):
    o0 = SSM_WIDTH
    o1 = o0 + N_HEADS * HEAD_DIM
    o2 = o1 + KV_WIDTH
    tm = x_ref.shape[1]
    half = HEAD_DIM // 2
    lane = lax.broadcasted_iota(jnp.int32, (1, LANES), 1)
    inv = jnp.exp((lane & (half - 1)).astype(F32) * (-2.0 / HEAD_DIM * math.log(ROPE_THETA)))
    first = (lane & (HEAD_DIM - 1)) < half

    @pl.when((pl.program_id(0) == 0) & (pl.program_id(1) == 0))
    def _():
        row = lax.broadcasted_iota(jnp.int32, (tm, LANES), 0)
        phi = (row & (period - 1)).astype(F32) * inv
        cphi_ref[...] = jnp.cos(phi)
        sphi_ref[...] = jnp.sin(phi)

    theta = (start + ((pl.program_id(1) * tm) & (period - 1))).astype(F32) * inv
    c_th, s_th = jnp.cos(theta), jnp.sin(theta)
    nparts = 2 if tm % (2 * LANES) == 0 else 1
    parts = [slice(p * tm // nparts, (p + 1) * tm // nparts) for p in range(nparts)]
    projs = [_dot(_rms(x_ref[0, r, :], g_ref[0]).astype(BF16), w_ref[0]) for r in parts]
    for r, proj in zip(parts, projs):
        c_ph, s_ph = cphi_ref[r, :], sphi_ref[r, :]
        cos = c_th * c_ph - s_th * s_ph
        sin = s_th * c_ph + c_th * s_ph
        sa = jnp.where(first, -sin, 0.0)
        sb = jnp.where(first, 0.0, sin)
        u_ref[0, r, :] = proj[:, :o0]
        q = [_rope(proj[:, c:c + LANES], cos, sa, sb) for c in range(o0, o1, LANES)]
        q_ref[0, r, :] = (jnp.concatenate(q, axis=1) * HEAD_DIM ** -0.5).astype(BF16)
        k_ref[0, r, :] = _rope(proj[:, o1:o2], cos, sa, sb)
        v_ref[0, r, :] = proj[:, o2:]


def _inproj_even(x, g, w_in, start, period, layer_e, tm):
    bsz, seq, _ = x.shape
    tok = lambda w: pl.BlockSpec((1, tm, w), lambda b, i: (b, i, 0))
    return pl.pallas_call(
        functools.partial(_inproj_even_body, start=start, period=period),
        grid=(bsz, seq // tm),
        in_specs=[tok(D_MODEL), _gain_spec(2 * layer_e),
                  _const_spec((1, D_MODEL, EVEN_IN), (layer_e, 0, 0))],
        out_specs=[tok(SSM_WIDTH), tok(N_HEADS * HEAD_DIM), tok(KV_WIDTH), tok(KV_WIDTH)],
        out_shape=[jax.ShapeDtypeStruct((bsz, seq, SSM_WIDTH), F32),
                   jax.ShapeDtypeStruct((bsz, seq, N_HEADS * HEAD_DIM), BF16),
                   jax.ShapeDtypeStruct((bsz, seq, KV_WIDTH), F32),
                   jax.ShapeDtypeStruct((bsz, seq, KV_WIDTH), F32)],
        scratch_shapes=[pltpu.VMEM((tm, LANES), F32)] * 2,
        compiler_params=_params(2),
        name="inproj_even",
    )(x, g, w_in)


def _s5_prep_body(lr_ref, li_ref, ld_ref, bre_ref, bim_ref, cre_ref, cim_ref, wg_ref, wy_ref):
    ts = S5_TILE_STATES
    tw = S5_T * LANES

    def times(a_r, a_i):
        return lambda p: (p[0] * a_r - p[1] * a_i, p[0] * a_i + p[1] * a_r)

    lr, li = lr_ref[0], li_ref[0]
    dt = jnp.exp(ld_ref[0])
    mag = jnp.exp(lr * dt)
    ar, ai = mag * jnp.cos(li * dt), mag * jnp.sin(li * dt)
    mul_a = times(ar, ai)
    nrm = lr * lr + li * li
    cbr = ((ar - 1.0) * lr + ai * li) / nrm
    cbi = (ai * lr - (ar - 1.0) * li) / nrm
    b_re, b_im = bre_ref[0], bim_ref[0]
    bbr = cbr * b_re - cbi * b_im
    bbi = cbr * b_im + cbi * b_re
    p = (jnp.ones_like(ar), jnp.zeros_like(ar))
    for s in reversed(range(S5_T)):
        rows = slice(s * LANES, (s + 1) * LANES)
        wg_ref[0, rows, :ts] = (p[0] * bbr - p[1] * bbi).astype(BF16)
        wg_ref[0, rows, ts:] = (p[0] * bbi + p[1] * bbr).astype(BF16)
        for t in range(s):
            wy_ref[0, rows, t * LANES:(t + 1) * LANES] = jnp.zeros((LANES, LANES), BF16)
        p = mul_a(p)
    to_col = lambda row: jnp.transpose(jnp.broadcast_to(row, (LANES, ts)))
    mul_ac = times(to_col(ar), to_col(ai))
    c_re, c_im = cre_ref[0], cim_ref[0]
    bb = jnp.concatenate([bbr, bbi], axis=1).astype(BF16)
    pc = (jnp.ones_like(c_re), jnp.zeros_like(c_re))
    for k in range(S5_T + 1):
        blk = jnp.concatenate([pc[0] * c_re - pc[1] * c_im, -(pc[0] * c_im + pc[1] * c_re)], axis=0).astype(BF16)
        if k >= 1:
            wy_ref[0, tw:, (k - 1) * LANES:k * LANES] = blk
        if k < S5_T:
            lag = _dot(bb, blk).astype(BF16)
            for s in range(S5_T - k):
                t = s + k
                wy_ref[0, s * LANES:(s + 1) * LANES, t * LANES:(t + 1) * LANES] = lag
        pc = mul_ac(pc)


def _s5_prep(a_re, a_im, log_dt, b_re, b_im, c_re, c_im):
    gpt = LANES // SSM_GROUP
    eye = jnp.eye(gpt, dtype=bool)

    def rows(p):
        return p.reshape(S5_TILES, 1, S5_TILE_STATES)

    def bmat(b):
        bt = b.reshape(S5_TILES, gpt, SSM_STATE, SSM_GROUP).transpose(0, 1, 3, 2)
        full = jnp.where(eye[None, :, None, :, None], bt[:, :, :, None, :], 0.0)
        return full.reshape(S5_TILES, LANES, S5_TILE_STATES)

    def cmat(c):
        ct = c.reshape(S5_TILES, gpt, SSM_GROUP, SSM_STATE).transpose(0, 1, 3, 2)
        full = jnp.where(eye[None, :, None, :, None], ct[:, :, :, None, :], 0.0)
        return full.reshape(S5_TILES, S5_TILE_STATES, LANES)

    row_spec = pl.BlockSpec((1, 1, S5_TILE_STATES), lambda j: (j, 0, 0))
    b_spec = pl.BlockSpec((1, LANES, S5_TILE_STATES), lambda j: (j, 0, 0))
    c_spec = pl.BlockSpec((1, S5_TILE_STATES, LANES), lambda j: (j, 0, 0))
    tw = S5_T * LANES
    return pl.pallas_call(
        _s5_prep_body,
        grid=(S5_TILES,),
        in_specs=[row_spec] * 3 + [b_spec] * 2 + [c_spec] * 2,
        out_specs=[pl.BlockSpec((1, tw, 2 * S5_TILE_STATES), lambda j: (j, 0, 0)),
                   pl.BlockSpec((1, tw + 2 * S5_TILE_STATES, tw), lambda j: (j, 0, 0))],
        out_shape=[jax.ShapeDtypeStruct((S5_TILES, tw, 2 * S5_TILE_STATES), BF16),
                   jax.ShapeDtypeStruct((S5_TILES, tw + 2 * S5_TILE_STATES, tw), BF16)],
        compiler_params=_params(1),
        name="s5_prep",
    )(rows(a_re), rows(a_im), rows(log_dt), bmat(b_re), bmat(b_im), cmat(c_re), cmat(c_im))


def _s5_pow_body(ex_ref, lr_ref, li_ref, ld_ref, *o_refs):
    dt = jnp.exp(ld_ref[...])
    zr, zi = lr_ref[...] * dt, li_ref[...] * dt
    lo = 0
    for o_ref in o_refs:
        ex = ex_ref[lo:lo + o_ref.shape[0], :]
        lo += o_ref.shape[0]
        mag = jnp.exp(ex * zr)
        o_ref[:, :S5_STATES] = mag * jnp.cos(ex * zi)
        o_ref[:, S5_STATES:] = mag * jnp.sin(ex * zi)


def _s5_pow_table(groups, a_re, a_im, log_dt):
    sizes = [len(g) for g in groups]
    ex = jnp.asarray(np.asarray(sum(groups, []), np.float32)[:, None])
    flat = lambda p: p.reshape(1, S5_STATES)
    whole = lambda n, w: pl.BlockSpec((n, w), lambda i: (0, 0))
    return pl.pallas_call(
        _s5_pow_body,
        grid=(1,),
        in_specs=[whole(sum(sizes), 1)] + [whole(1, S5_STATES)] * 3,
        out_specs=[whole(n, 2 * S5_STATES) for n in sizes],
        out_shape=[jax.ShapeDtypeStruct((n, 2 * S5_STATES), F32) for n in sizes],
        compiler_params=_params(1),
        name="s5_pow_table",
    )(ex, flat(a_re), flat(a_im), flat(log_dt))


def _s5_block(u_ref, h0_ref, wg_ref, wy_ref, pow2_ref, powr_ref, d_ref, wglu_ref,
              y_ref, hout_ref, carry_ref, us_ref, ys_ref, *, ni, chained, first):
    ns = S5_STATES
    ts = S5_TILE_STATES
    segtok = ni * S5_T
    pitch = _s5_pitch(ni)

    if chained:
        @pl.when(first)
        def _():
            carry_ref[...] = h0_ref[0]
    else:
        carry_ref[...] = h0_ref[0]

    for j in range(S5_TILES):
        for s in range(SUBLANES):
            us_ref[j, s * pitch:s * pitch + segtok] = u_ref[0, s * segtok:(s + 1) * segtok, j * LANES:(j + 1) * LANES]

    def gather(j, t):
        return jnp.concatenate([us_ref[j, pl.ds(S5_T * i + t, SUBLANES, stride=pitch), :]
                                for i in range(ni)], axis=0)

    def fma(p_r, p_i, x_r, x_i, y_r, y_i):
        return y_r + p_r * x_r - p_i * x_i, y_i + p_r * x_i + p_i * x_r

    xs, g_re, g_im = [], [], []
    for j in range(S5_TILES):
        xj = jnp.concatenate([gather(j, s) for s in range(S5_T)], axis=1).astype(BF16)
        g = _dot(xj, wg_ref[j])
        xs.append(xj)
        g_re.append(g[:, :ts])
        g_im.append(g[:, ts:])
    g_re = jnp.concatenate(g_re, axis=1)
    g_im = jnp.concatenate(g_im, axis=1)
    slab = lambda a, i: a[i * SUBLANES:(i + 1) * SUBLANES]
    power = lambda ref, r: (ref[r:r + 1, :ns], ref[r:r + 1, ns:])
    a_r, a_i = power(powr_ref, 0)
    cin = carry_ref[...]
    h_r, h_i = cin[:, :ns], cin[:, ns:]
    prev = []
    if chained:
        local = [(slab(g_re, 0), slab(g_im, 0))]
        for i in range(1, ni):
            local.append(fma(a_r, a_i, *local[-1], slab(g_re, i), slab(g_im, i)))
        sub = lax.broadcasted_iota(jnp.int32, (SUBLANES, 1), 0)
        c_r = jnp.where(sub == 0, h_r, pltpu.roll(local[-1][0], 1, 0))
        c_i = jnp.where(sub == 0, h_i, pltpu.roll(local[-1][1], 1, 0))
        for m in range(3):
            k = 1 << m
            s_r = jnp.where(sub >= k, pltpu.roll(c_r, k, 0), 0.0)
            s_i = jnp.where(sub >= k, pltpu.roll(c_i, k, 0), 0.0)
            c_r, c_i = fma(*power(pow2_ref, m), s_r, s_i, c_r, c_i)
        h_r, h_i = c_r, c_i
        for i in range(ni):
            prev.append((h_r, h_i))
            h_r, h_i = fma(*power(powr_ref, i), c_r, c_i, *local[i])
        last = jnp.concatenate([h_r[SUBLANES - 1:], h_i[SUBLANES - 1:]], axis=1)
    else:
        for i in range(ni):
            prev.append((h_r, h_i))
            h_r, h_i = fma(a_r, a_i, h_r, h_i, slab(g_re, i), slab(g_im, i))
        last = jnp.concatenate([h_r, h_i], axis=1)
    carry_ref[...] = last
    hout_ref[0] = last
    prev_re = jnp.concatenate([p[0] for p in prev], axis=0)
    prev_im = jnp.concatenate([p[1] for p in prev], axis=0)
    y_tiles = []
    for j in range(S5_TILES):
        lhs = jnp.concatenate([xs[j], prev_re[:, j * ts:(j + 1) * ts].astype(BF16),
                               prev_im[:, j * ts:(j + 1) * ts].astype(BF16)], axis=1)
        y_tiles.append(_dot(lhs, wy_ref[j]))
    d = d_ref[...]
    wglu = wglu_ref[0]
    def gelu_in(t):
        ut = jnp.concatenate([gather(j, t) for j in range(S5_TILES)], axis=1)
        yt = jnp.concatenate([y_tiles[j][:, t * LANES:(t + 1) * LANES] for j in range(S5_TILES)], axis=1)
        return jax.nn.gelu(yt + d * ut)

    def scatter(t, o):
        for j in range(S5_TILES):
            for i in range(ni):
                ys_ref[j, pl.ds(S5_T * i + t, SUBLANES, stride=pitch), :] = slab(o, i)[:, j * LANES:(j + 1) * LANES]

    g_q, z_q = {}, {}
    for step in range(S5_T + 2):
        if step < S5_T:
            g_q[step] = gelu_in(step)
        if 0 <= step - 1 < S5_T:
            z_q[step - 1] = _dot(g_q[step - 1].astype(BF16), wglu)
        if 0 <= step - 2 < S5_T:
            scatter(step - 2, g_q.pop(step - 2) * jax.nn.sigmoid(z_q.pop(step - 2)))
    for j in range(S5_TILES):
        for s in range(SUBLANES):
            y_ref[0, s * segtok:(s + 1) * segtok, j * LANES:(j + 1) * LANES] = (
                ys_ref[j, s * pitch:s * pitch + segtok].astype(BF16))


def _s5_pitch(ni):
    segtok = ni * S5_T
    return segtok + SUBLANES if (segtok // SUBLANES) % 2 == 0 else segtok


def _s5_body(ul_ref, ush_ref, h0l_ref, h0s_ref, wg_ref, wy_ref, pow2_ref, powrl_ref, powrs_ref, d_ref, wglu_ref,
             yl_ref, ysh_ref, houtl_ref, houts_ref, carryl_ref, carrys_ref, us_ref, ys_ref,
             *, ni_long, ni_short, nt, nlong):
    s = pl.program_id(0)
    shared = (wg_ref, wy_ref, pow2_ref)

    @pl.when(s < nlong)
    def _():
        _s5_block(ul_ref, h0l_ref, *shared, powrl_ref, d_ref, wglu_ref, yl_ref, houtl_ref, carryl_ref, us_ref, ys_ref,
                  ni=ni_long, chained=True, first=lax.rem(s, nt) == 0)

    @pl.when(s == nlong)
    def _():
        _s5_block(ush_ref, h0s_ref, *shared, powrs_ref, d_ref, wglu_ref, ysh_ref, houts_ref, carrys_ref, us_ref, ys_ref,
                  ni=ni_short, chained=False, first=None)


def _s5_mixer(u_long, u_short, h0_long, h0_short, wg, wy, pow2, powr_long, powr_short, d, w_glu, layer_e,
              ni_long, ni_short):
    nb, ntok, width = u_long.shape
    tb = SUBLANES * ni_long * S5_T
    nt = ntok // tb
    nlong = nb * nt
    assert u_short.shape[1] == SUBLANES * ni_short * S5_T and ni_short <= ni_long
    lb = lambda s: jnp.minimum(s, nlong - 1) // nt
    li = lambda s: jnp.minimum(s, nlong - 1) % nt
    blk_l = pl.BlockSpec((1, tb, width), lambda s: (lb(s), li(s), 0))
    blk_s = pl.BlockSpec(u_short.shape, lambda s: (0, 0, 0))
    st_l = pl.BlockSpec((1, 1, 2 * S5_STATES), lambda s: (lb(s), 0, 0))
    st_s = pl.BlockSpec(h0_short.shape, lambda s: (0, 0, 0))
    return pl.pallas_call(
        functools.partial(_s5_body, ni_long=ni_long, ni_short=ni_short, nt=nt, nlong=nlong),
        grid=(nlong + 1,),
        in_specs=[blk_l, blk_s, st_l, st_s, _const_spec(wg.shape), _const_spec(wy.shape),
                  _const_spec(pow2.shape), _const_spec(powr_long.shape), _const_spec(powr_short.shape),
                  _const_spec((1, SSM_WIDTH)), _const_spec((1, SSM_WIDTH, SSM_WIDTH), (layer_e, 0, 0))],
        out_specs=[blk_l, blk_s, st_l, st_s],
        out_shape=[jax.ShapeDtypeStruct(u_long.shape, BF16), jax.ShapeDtypeStruct(u_short.shape, BF16),
                   jax.ShapeDtypeStruct(h0_long.shape, F32), jax.ShapeDtypeStruct(h0_short.shape, F32)],
        scratch_shapes=[pltpu.VMEM(h0_long.shape[1:], F32), pltpu.VMEM(h0_short.shape[1:], F32)]
        + [pltpu.VMEM((S5_TILES, SUBLANES * _s5_pitch(ni_long), LANES), F32)] * 2,
        compiler_params=_params(1),
        name="s5_mixer",
    )(u_long, u_short, h0_long, h0_short, wg, wy, pow2, powr_long, powr_short, d, w_glu)


def _swa_body(sink_ref, q_ref, kc_ref, vc_ref, kp_ref, vp_ref, pk_ref, pv_ref, o_ref, *, tq, qg, start):
    first = pl.program_id(1) == 0
    k_prev = jnp.where(first, pk_ref[0], kp_ref[0])
    v_prev = jnp.where(first, pv_ref[0], vp_ref[0])
    k_all = jnp.concatenate([k_prev, kc_ref[0]], axis=0)
    v_all = jnp.concatenate([v_prev, vc_ref[0]], axis=0)
    nk = WINDOW + qg
    lane = lax.broadcasted_iota(jnp.int32, (1, KV_WIDTH), 1)
    slot = lax.broadcasted_iota(jnp.int32, (1, nk), 1)
    qrow = lax.broadcasted_iota(jnp.int32, (2 * qg, 1), 0)
    lower = lane < HEAD_DIM
    placed = []
    for kh in range(N_KV_HEADS):
        own = lower if kh == 0 else jnp.logical_not(lower)
        kz, vz = jnp.where(own, k_all, 0.0), jnp.where(own, v_all, 1.0)
        kr, vr = pltpu.roll(kz, HEAD_DIM, 1), pltpu.roll(vz, HEAD_DIM, 1)
        k_lo, k_hi = (kz, kr) if kh == 0 else (kr, kz)
        v_lo, v_hi = (vz, vr) if kh == 0 else (vr, vz)
        placed.append(((k_lo.astype(BF16), v_lo.astype(BF16)), (k_hi.astype(BF16), v_hi.astype(BF16))))
    combos = [(kh, par) for kh in range(N_KV_HEADS) for par in range(2)]

    def scores(c):
        qc = q_ref[0, c * qg:(c + 1) * qg, :]
        out = []
        for kh, par in combos:
            base = kh * GQ * HEAD_DIM
            qs = jnp.concatenate([qc[:, base:base + LANES], qc[:, base + LANES:base + 2 * LANES]], axis=0)
            out.append(_dot_t(qs, placed[kh][par][0][c * qg:c * qg + nk]))
        return out

    def softmax(c, ss):
        n_bad = jnp.where(first, WINDOW - start - c * qg, 0) if c * qg < WINDOW - start else None
        probs, sinks = [], []
        for (kh, par), s in zip(combos, ss):
            if n_bad is not None:
                s = jnp.where(slot >= n_bad, s, NEG)
            sink = jnp.where(qrow < qg, sink_ref[kh * GQ + par], sink_ref[kh * GQ + 2 + par])
            m = jnp.maximum(jnp.max(s, axis=1, keepdims=True), sink)
            probs.append(jnp.exp(s - m).astype(BF16))
            sinks.append(jnp.exp(sink - m))
        return probs, sinks

    def finish(c, probs, sinks):
        pvs = [_dot(p, placed[kh][par][1][c * qg:c * qg + nk]) for (kh, par), p in zip(combos, probs)]
        outs = []
        for kh in range(N_KV_HEADS):
            pv_e, pv_o = pvs[2 * kh], pvs[2 * kh + 1]
            num = jnp.where(lower, pv_e, pv_o)
            den = pltpu.roll(jnp.where(lower, pv_o, pv_e), HEAD_DIM, 1) + jnp.where(lower, sinks[2 * kh], sinks[2 * kh + 1])
            acc = num / den
            outs += [acc[:qg], acc[qg:]]
        o_ref[0, c * qg:(c + 1) * qg, :] = jnp.concatenate(outs, axis=1).astype(BF16)

    nchunk = tq // qg
    per = min(SWA_STAGE_CHUNKS, nchunk)
    groups = [range(g, min(g + per, nchunk)) for g in range(0, nchunk, per)]
    s_q, p_q = {}, {}
    for step in range(len(groups) + 2):
        if step < len(groups):
            s_q[step] = [scores(c) for c in groups[step]]
        if 0 <= step - 1 < len(groups):
            p_q[step - 1] = [softmax(c, ss) for c, ss in zip(groups[step - 1], s_q.pop(step - 1))]
        if 0 <= step - 2 < len(groups):
            for c, (probs, sinks) in zip(groups[step - 2], p_q.pop(step - 2)):
                finish(c, probs, sinks)


def _swa(q, k, v, past_k, past_v, sinks, start, tq, qg):
    bsz, seq, _ = q.shape
    ntiles = seq // tq
    tok = lambda w: pl.BlockSpec((1, tq, w), lambda b, i: (b, i, 0))
    past = pl.BlockSpec((1, WINDOW, KV_WIDTH), lambda b, i: (b, 0, 0))
    if ntiles > 1:
        per = tq // WINDOW
        prev = pl.BlockSpec((1, WINDOW, KV_WIDTH), lambda b, i: (b, jnp.maximum(i * per - 1, 0), 0))
        k_prev, v_prev = k, v
    else:
        prev, k_prev, v_prev = past, past_k, past_v
    return pl.pallas_call(
        functools.partial(_swa_body, tq=tq, qg=qg, start=start),
        grid=(bsz, ntiles),
        in_specs=[pl.BlockSpec(memory_space=pltpu.SMEM), tok(N_HEADS * HEAD_DIM), tok(KV_WIDTH), tok(KV_WIDTH),
                  prev, prev, past, past],
        out_specs=tok(N_HEADS * HEAD_DIM),
        out_shape=jax.ShapeDtypeStruct((bsz, seq, N_HEADS * HEAD_DIM), BF16),
        compiler_params=_params(2),
        name="swa",
    )(sinks, q, k, v, k_prev, v_prev, past_k, past_v)


def _xattn(x1, g, wq, mk_ref, mv_ref, wo, nseq, o_ref):
    h = _rms(x1, g).astype(BF16)
    q = (_dot(h, wq) * X_HEAD_DIM ** -0.5).astype(BF16)
    nparts = nseq if nseq > 1 else XATTN_PARTS
    prows = x1.shape[0] // nparts
    heads = [slice(hd * X_HEAD_DIM, (hd + 1) * X_HEAD_DIM) for hd in range(X_HEADS)]

    def scores(p):
        r, b = slice(p * prows, (p + 1) * prows), (p if nseq > 1 else 0)
        return [_dot_t(q[r, c], mk_ref[0, b, :, c]) for c in heads]

    def softmax(ss):
        out = []
        for s in ss:
            e = jnp.exp(s - jnp.max(s, axis=1, keepdims=True))
            out.append((e.astype(BF16), jnp.sum(e, axis=1, keepdims=True)))
        return out

    def values(p, pd):
        b = p if nseq > 1 else 0
        return jnp.concatenate([_dot(e, mv_ref[0, b, :, c]) / den for c, (e, den) in zip(heads, pd)],
                               axis=1).astype(BF16)

    def finish(p, pd):
        r = slice(p * prows, (p + 1) * prows)
        o_ref[0, r, :] = x1[r] + _dot(values(p, pd), wo)

    if nseq > 1:
        pds = [softmax(ss) for ss in [scores(p) for p in range(nparts)]]
        o = jnp.concatenate([values(p, pd) for p, pd in enumerate(pds)], axis=0)
        o_ref[0] = x1 + _dot(o, wo)
        return
    s_q, p_q = {}, {}
    for step in range(nparts + 2):
        if step < nparts:
            s_q[step] = scores(step)
        if 0 <= step - 1 < nparts:
            p_q[step - 1] = softmax(s_q.pop(step - 1))
        if 0 <= step - 2 < nparts:
            finish(step - 2, p_q.pop(step - 2))


def _conv3(cur, carry, w):
    n = cur.shape[0]
    row = lax.broadcasted_iota(jnp.int32, (n, 1), 0)
    nc = carry.shape[0]
    c1, c2 = carry[nc - 1:nc], carry[nc - 2:nc - 1]
    m1 = jnp.where(row == 0, c1, pltpu.roll(cur, 1, 0))
    m2 = jnp.where(row == 0, c2, jnp.where(row == 1, c1, pltpu.roll(cur, 2, 0)))
    return w[0:1] * m2 + w[1:2] * m1 + w[2:3] * cur


def _conv3_seqs(cur, carries, w):
    nseq = len(carries)
    if nseq == 1:
        return _conv3(cur, carries[0], w)
    rows = cur.shape[0] // nseq
    body = _conv3(cur, carries[0], w)
    pieces = []
    for b in range(nseq):
        pieces.append(_conv3(cur[b * rows:b * rows + SUBLANES], carries[b], w))
        pieces.append(body[b * rows + SUBLANES:(b + 1) * rows])
    return jnp.concatenate(pieces, axis=0)


def _store_last_rows(st_ref, cols, cur, nseq):
    rows = cur.shape[0] // nseq
    for b in range(nseq):
        st_ref[b, :, cols] = cur[(b + 1) * rows - (CONV_WIDTH - 1):(b + 1) * rows]


def _mix_even_tile(x_ref, ya_ref, yb_ref, wout_ref, g_ref, wq_ref, mk_ref, mv_ref, wo_ref, o_ref, *, nseq):
    x1 = x_ref[0] + _dot(jnp.concatenate([ya_ref[0], yb_ref[0]], axis=1), wout_ref[0])
    _xattn(x1, g_ref[0], wq_ref[0], mk_ref, mv_ref, wo_ref[0], nseq, o_ref)


def _mix_even_xattn_body(xl_ref, xs_ref, yal_ref, yas_ref, ybl_ref, ybs_ref, wout_ref, g_ref, wq_ref,
                         mkl_ref, mks_ref, mvl_ref, mvs_ref, wo_ref, ol_ref, os_ref, *, nlong, nseq_s):
    s = pl.program_id(0)

    @pl.when(s < nlong)
    def _():
        _mix_even_tile(xl_ref, yal_ref, ybl_ref, wout_ref, g_ref, wq_ref, mkl_ref, mvl_ref, wo_ref, ol_ref, nseq=1)

    @pl.when(s == nlong)
    def _():
        _mix_even_tile(xs_ref, yas_ref, ybs_ref, wout_ref, g_ref, wq_ref, mks_ref, mvs_ref, wo_ref, os_ref, nseq=nseq_s)


def _long_short_specs(x_long, x_short, tm):
    nb, seq, _ = x_long.shape
    nt = seq // tm
    nlong = nb * nt
    lb = lambda s: jnp.minimum(s, nlong - 1) // nt
    li = lambda s: jnp.minimum(s, nlong - 1) % nt
    tok_long = lambda w: pl.BlockSpec((1, tm, w), lambda s: (lb(s), li(s), 0))
    tok_short = lambda w: pl.BlockSpec((1, x_short.shape[1], w), lambda s: (0, 0, 0))
    return nlong, nt, lb, tok_long, tok_short


def _mix_even_xattn(x_long, x_short, ya_long, ya_short, yb_long, yb_short, w_out, g, wq, mem_long, mem_short, wo,
                    layer, layer_e, tm):
    nlong, _, lb, tok_l, tok_s = _long_short_specs(x_long, x_short, tm)
    nseq_s = mem_short[0].shape[1]
    mem_l = pl.BlockSpec((1, 1, N_MEM, D_MODEL), lambda s: (layer, lb(s), 0, 0))
    mem_s = _const_spec((1, nseq_s, N_MEM, D_MODEL), (layer, 0, 0, 0))
    half = D_MODEL // 2
    sq = lambda l: _const_spec((1, D_MODEL, D_MODEL), (l, 0, 0))
    return pl.pallas_call(
        functools.partial(_mix_even_xattn_body, nlong=nlong, nseq_s=nseq_s),
        grid=(nlong + 1,),
        in_specs=[tok_l(D_MODEL), tok_s(D_MODEL), tok_l(half), tok_s(half), tok_l(half), tok_s(half),
                  sq(layer_e), _gain_spec(layer), sq(layer), mem_l, mem_s, mem_l, mem_s, sq(layer)],
        out_specs=[tok_l(D_MODEL), tok_s(D_MODEL)],
        out_shape=[jax.ShapeDtypeStruct(x_long.shape, F32), jax.ShapeDtypeStruct(x_short.shape, F32)],
        compiler_params=_params(1),
        name="mix_even_xattn",
    )(x_long, x_short, ya_long, ya_short, yb_long, yb_short, w_out, g, wq,
      mem_long[0], mem_short[0], mem_long[1], mem_short[1], wo)


def _mix_odd_tile(x_ref, gm_ref, win_ref, cw_ref, wout_ref, prev_ref, g_ref, wq_ref, mk_ref, mv_ref, wo_ref,
                  o_ref, st_ref, carry_ref, *, tm, nseq, first):
    keep = CONV_WIDTH - 1
    if nseq == 1:
        @pl.when(first)
        def _():
            carry_ref[SUBLANES - keep:, :] = prev_ref[0, 0]

    x = x_ref[0]
    proj = _dot(_rms(x, gm_ref[0]).astype(BF16), win_ref[0])
    gate_b, gate_c, z = proj[:, :D_MODEL], proj[:, D_MODEL:2 * D_MODEL], proj[:, 2 * D_MODEL:]
    cz = gate_c * z
    carries = [carry_ref[SUBLANES - keep:, :]] if nseq == 1 else [prev_ref[0, b] for b in range(nseq)]
    z_conv = _conv3_seqs(cz, carries, cw_ref[0])
    if nseq == 1:
        carry_ref[...] = cz[tm - SUBLANES:]
    _store_last_rows(st_ref, slice(None), cz, nseq)
    x1 = x + _dot((gate_b * z_conv).astype(BF16), wout_ref[0])
    _xattn(x1, g_ref[0], wq_ref[0], mk_ref, mv_ref, wo_ref[0], nseq, o_ref)


def _mix_odd_xattn_body(*refs, tm, nseq):
    _mix_odd_tile(*refs, tm=tm, nseq=nseq, first=pl.program_id(1) == 0)


def _mix_odd_xattn(x, gm, w_in, conv_w, w_out, prev, g, wq, mem, wo, layer, layer_o, tm, nseq):
    bsz, seq, _ = x.shape
    keep = CONV_WIDTH - 1
    tok = pl.BlockSpec((1, tm, D_MODEL), lambda b, i: (b, i, 0))
    mem_spec = pl.BlockSpec((1, nseq, N_MEM, D_MODEL), lambda b, i: (layer, b, 0, 0))
    hist = pl.BlockSpec((1, nseq, keep, D_MODEL), lambda b, i: (layer_o, b, 0, 0))
    st = pl.BlockSpec((nseq, keep, D_MODEL), lambda b, i: (b, 0, 0))
    sq = lambda l: _const_spec((1, D_MODEL, D_MODEL), (l, 0, 0))
    return pl.pallas_call(
        functools.partial(_mix_odd_xattn_body, tm=tm, nseq=nseq),
        grid=(bsz, seq // tm),
        in_specs=[tok, _gain_spec(layer), _const_spec((1, D_MODEL, 3 * D_MODEL), (layer_o, 0, 0)),
                  _const_spec((1, CONV_WIDTH, D_MODEL), (layer_o, 0, 0)), sq(layer_o), hist,
                  _gain_spec(layer), sq(layer), mem_spec, mem_spec, sq(layer)],
        out_specs=[tok, st],
        out_shape=[jax.ShapeDtypeStruct(x.shape, F32), jax.ShapeDtypeStruct((bsz * nseq, keep, D_MODEL), F32)],
        scratch_shapes=[pltpu.VMEM((SUBLANES, D_MODEL), F32)],
        compiler_params=_params(2),
        name="mix_odd_xattn",
    )(x, gm, w_in, conv_w, w_out, prev, g, wq, mem[0], mem[1], wo)


def _ffn_tile(x_ref, g_ref, wup_ref, cw_ref, wdn_ref, prev_ref, gfin_ref, o_ref, st_ref, carry_ref,
              *, tm, final, nseq, first):
    if nseq == 1:
        @pl.when(first)
        def _():
            carry_ref[SUBLANES - (CONV_WIDTH - 1):, :] = prev_ref[0, 0]

    x = x_ref[0]
    h = _rms(x, g_ref[0]).astype(BF16)
    chunks = [slice(lo, min(lo + FF_CHUNK, D_FF)) for lo in range(0, D_FF, FF_CHUNK)]

    def up(cols):
        return _dot(h, wup_ref[0, :, cols]), _dot(h, wup_ref[0, :, D_FF + cols.start:D_FF + cols.stop])

    acc = x
    nxt = up(chunks[0])
    for n, cols in enumerate(chunks):
        (gate, val), nxt = nxt, (up(chunks[n + 1]) if n + 1 < len(chunks) else None)
        carries = ([carry_ref[SUBLANES - (CONV_WIDTH - 1):, cols]] if nseq == 1
                   else [prev_ref[0, b, :, cols] for b in range(nseq)])
        gate_c = _conv3_seqs(gate, carries, cw_ref[0, :, cols])
        if nseq == 1:
            carry_ref[:, cols] = gate[tm - SUBLANES:]
        _store_last_rows(st_ref, cols, gate, nseq)
        act = (gate_c * jax.nn.sigmoid(gate_c) * val).astype(BF16)
        if n + 1 < len(chunks):
            acc = acc + _dot(act, wdn_ref[0, cols, :])
        else:
            half = tm // 2
            for rows in (slice(0, half), slice(half, tm)):
                out = acc[rows] + _dot(act[rows], wdn_ref[0, cols, :])
                o_ref[0, rows, :] = _rms(out, gfin_ref[0]) if final else out


def _ffn_body(xl_ref, xs_ref, g_ref, wup_ref, cw_ref, wdn_ref, prevl_ref, prevs_ref, gfin_ref,
              ol_ref, os_ref, stl_ref, sts_ref, carry_ref, *, tm, nt, nlong, rows_s, nseq_s, final):
    s = pl.program_id(0)
    shared = (g_ref, wup_ref, cw_ref, wdn_ref)

    @pl.when(s < nlong)
    def _():
        _ffn_tile(xl_ref, *shared, prevl_ref, gfin_ref, ol_ref, stl_ref, carry_ref,
                  tm=tm, final=final, nseq=1, first=lax.rem(s, nt) == 0)

    @pl.when(s == nlong)
    def _():
        _ffn_tile(xs_ref, *shared, prevs_ref, gfin_ref, os_ref, sts_ref, carry_ref,
                  tm=rows_s, final=final, nseq=nseq_s, first=None)


def _ffn(x_long, x_short, g, w_up, conv_w, w_down, prev_long, prev_short, g_final, layer, tm, final):
    nb, seq, _ = x_long.shape
    rows_s = x_short.shape[1]
    nseq_s = prev_short.shape[1]
    nt = seq // tm
    nlong = nb * nt
    lb = lambda s: jnp.minimum(s, nlong - 1) // nt
    li = lambda s: jnp.minimum(s, nlong - 1) % nt
    keep = CONV_WIDTH - 1
    tok_l = pl.BlockSpec((1, tm, D_MODEL), lambda s: (lb(s), li(s), 0))
    tok_s = pl.BlockSpec((1, rows_s, D_MODEL), lambda s: (0, 0, 0))
    hist_l = pl.BlockSpec((1, 1, keep, D_FF), lambda s: (layer, lb(s), 0, 0))
    hist_s = pl.BlockSpec((1, nseq_s, keep, D_FF), lambda s: (layer, 0, 0, 0))
    st_l = pl.BlockSpec((1, keep, D_FF), lambda s: (lb(s), 0, 0))
    st_s = pl.BlockSpec((nseq_s, keep, D_FF), lambda s: (0, 0, 0))
    return pl.pallas_call(
        functools.partial(_ffn_body, tm=tm, nt=nt, nlong=nlong, rows_s=rows_s, nseq_s=nseq_s, final=final),
        grid=(nlong + 1,),
        in_specs=[tok_l, tok_s, _gain_spec(layer), _const_spec((1, D_MODEL, 2 * D_FF), (layer, 0, 0)),
                  _const_spec((1, CONV_WIDTH, D_FF), (layer, 0, 0)), _const_spec((1, D_FF, D_MODEL), (layer, 0, 0)),
                  hist_l, hist_s, _gain_spec(0)],
        out_specs=[tok_l, tok_s, st_l, st_s],
        out_shape=[jax.ShapeDtypeStruct(x_long.shape, F32), jax.ShapeDtypeStruct(x_short.shape, F32),
                   jax.ShapeDtypeStruct((nb, keep, D_FF), F32), jax.ShapeDtypeStruct((nseq_s, keep, D_FF), F32)],
        scratch_shapes=[pltpu.VMEM((SUBLANES, D_FF), F32)],
        compiler_params=_params(1),
        name="conv_ffn",
    )(x_long, x_short, g, w_up, conv_w, w_down, prev_long, prev_short, g_final)


def _mem_flat_copies(k_hbm, v_hbm, kbuf, vbuf, sem, step, slot, nbatch):
    lyr, b = step // nbatch, step % nbatch
    return [pltpu.make_async_copy(src.at[lyr, b, :, hd, :], buf.at[slot, hd], sem.at[slot, i, hd])
            for i, (src, buf) in enumerate(((k_hbm, kbuf), (v_hbm, vbuf))) for hd in range(X_HEADS)]


def _mem_flat_body(k_hbm, v_hbm, ko_ref, vo_ref, kbuf, vbuf, sem, *, nbatch):
    s, n = pl.program_id(0), pl.num_programs(0)
    copies = functools.partial(_mem_flat_copies, k_hbm, v_hbm, kbuf, vbuf, sem, nbatch=nbatch)

    @pl.when(s == 0)
    def _():
        for c in copies(step=s, slot=0):
            c.start()

    @pl.when(s + 1 < n)
    def _():
        for c in copies(step=s + 1, slot=(s + 1) % 2):
            c.start()

    slot = s % 2
    for c in copies(step=s, slot=slot):
        c.wait()
    for hd in range(X_HEADS):
        cols = slice(hd * X_HEAD_DIM, (hd + 1) * X_HEAD_DIM)
        ko_ref[0, 0, :, cols] = kbuf[slot, hd].astype(BF16)
        vo_ref[0, 0, :, cols] = vbuf[slot, hd].astype(BF16)


def _mem_flat(mem_k, mem_v):
    depth, nbatch = mem_k.shape[:2]
    flat = pl.BlockSpec((1, 1, N_MEM, D_MODEL), lambda s: (s // nbatch, s % nbatch, 0, 0))
    return pl.pallas_call(
        functools.partial(_mem_flat_body, nbatch=nbatch),
        grid=(depth * nbatch,),
        in_specs=[pl.BlockSpec(memory_space=pl.ANY)] * 2,
        out_specs=[flat, flat],
        out_shape=[jax.ShapeDtypeStruct((depth, nbatch, N_MEM, D_MODEL), BF16)] * 2,
        scratch_shapes=[pltpu.VMEM((2, X_HEADS, N_MEM, X_HEAD_DIM), F32)] * 2
        + [pltpu.SemaphoreType.DMA((2, 2, X_HEADS))],
        compiler_params=_params(1),
        name="mem_flat",
    )(mem_k, mem_v)


def _mem_kv_body(m_ref, w_ref, k_ref, v_ref, kf_ref, vf_ref):
    kv = _dot(m_ref[0].astype(BF16), w_ref[0])
    kf_ref[0, 0] = kv[:, :D_MODEL].astype(BF16)
    vf_ref[0, 0] = kv[:, D_MODEL:].astype(BF16)
    for hd in range(X_HEADS):
        k_ref[0, 0, :, hd, :] = kv[:, hd * X_HEAD_DIM:(hd + 1) * X_HEAD_DIM]
        v_ref[0, 0, :, hd, :] = kv[:, D_MODEL + hd * X_HEAD_DIM:D_MODEL + (hd + 1) * X_HEAD_DIM]


def _mem_kv(mem, w_kv):
    bsz = mem.shape[0]
    heads = pl.BlockSpec((1, 1, N_MEM, X_HEADS, X_HEAD_DIM), lambda l, b: (l, b, 0, 0, 0))
    flat = pl.BlockSpec((1, 1, N_MEM, D_MODEL), lambda l, b: (l, b, 0, 0))
    return pl.pallas_call(
        _mem_kv_body,
        grid=(DEPTH, bsz),
        in_specs=[pl.BlockSpec((1, N_MEM, D_MODEL), lambda l, b: (b, 0, 0)),
                  pl.BlockSpec((1, D_MODEL, 2 * D_MODEL), lambda l, b: (l, 0, 0))],
        out_specs=[heads, heads, flat, flat],
        out_shape=[jax.ShapeDtypeStruct((DEPTH, bsz, N_MEM, X_HEADS, X_HEAD_DIM), F32)] * 2
        + [jax.ShapeDtypeStruct((DEPTH, bsz, N_MEM, D_MODEL), BF16)] * 2,
        compiler_params=_params(2),
        name="mem_kv",
    )(mem, w_kv)


def _project_in(x, start, w, cfg):
    bsz, seq, _ = x.shape
    nseq = cfg["nseq"]
    xg = x.reshape(bsz // nseq, nseq * seq, D_MODEL)
    return (xg,) + tuple(_inproj_even(xg, w["norm_mix"], w["w_in_even"], start, seq, 0, cfg["tm_in"]))


def _window_attention(q, k, v, past_k, past_v, bsz, start, w, cfg):
    seq = q.shape[0] * q.shape[1] // bsz
    per_seq = lambda t: t.reshape(bsz, seq, t.shape[-1])
    k, v = per_seq(k), per_seq(v)
    yb = _swa(per_seq(q), k, v, past_k.reshape(bsz, WINDOW, KV_WIDTH), past_v.reshape(bsz, WINDOW, KV_WIDTH),
              w["attn_sinks"], start, cfg["tq"], cfg["qg"])
    keep = min(seq, WINDOW)
    new_k = k[:, seq - keep:].reshape(1, bsz, keep, N_KV_HEADS, HEAD_DIM)
    new_v = v[:, seq - keep:].reshape(1, bsz, keep, N_KV_HEADS, HEAD_DIM)
    return yb.reshape(q.shape), new_k, new_v


def kernel(x_prompt, x_sample, mem_prompt, cache_win_k, cache_win_v, state_ssm_re, state_ssm_im, state_conv_mix, state_conv_ffn, cache_mem_k, cache_mem_v, norm_mix, norm_xattn, norm_ffn, norm_final, w_in_even, w_out_even, ssm_a_re, ssm_a_im, ssm_log_dt, ssm_b_re, ssm_b_im, ssm_c_re, ssm_c_im, ssm_d, w_glu, attn_sinks, w_in_odd, conv_mix_w, w_out_odd, xattn_wq, xattn_wkv, xattn_wo, ffn_w_up, ffn_conv_w, ffn_w_down):
    bp, seq_p, _ = x_prompt.shape
    bs, seq_s, _ = x_sample.shape
    gains = lambda g: g.reshape(-1, 1, D_MODEL)
    w = dict(norm_mix=gains(norm_mix), norm_xattn=gains(norm_xattn), norm_ffn=gains(norm_ffn),
             norm_final=gains(norm_final),
             w_in_even=w_in_even, w_out_even=w_out_even,
             ssm_d=ssm_d.reshape(1, SSM_WIDTH), w_glu=w_glu.astype(BF16), attn_sinks=attn_sinks.reshape(N_HEADS),
             w_in_odd=w_in_odd, conv_mix_w=conv_mix_w, w_out_odd=w_out_odd,
             xattn_wq=xattn_wq, xattn_wo=xattn_wo,
             ffn_w_up=ffn_w_up.astype(BF16), ffn_conv_w=ffn_conv_w, ffn_w_down=ffn_w_down.astype(BF16))

    gp = (SSM_GROUPS, SSM_STATE)
    a_re, a_im, log_dt = ssm_a_re.reshape(gp), ssm_a_im.reshape(gp), ssm_log_dt.reshape(gp)
    wg, wy = _s5_prep(a_re, a_im, log_dt, ssm_b_re.reshape(gp + (SSM_GROUP,)), ssm_b_im.reshape(gp + (SSM_GROUP,)),
                      ssm_c_re.reshape(SSM_GROUPS, SSM_GROUP, SSM_STATE), ssm_c_im.reshape(SSM_GROUPS, SSM_GROUP, SSM_STATE))
    ni_p, ni_s = S5_SEG_CHUNKS, seq_s // S5_T
    padded = lambda e: e + [1.0] * (-len(e) % SUBLANES)
    pow2, powr_p, powr_s = _s5_pow_table(
        [padded([float(S5_T * ni_p << m) for m in range(3)]),
         padded([float(S5_T * (i + 1)) for i in range(ni_p)]),
         padded([float(S5_T * (i + 1)) for i in range(ni_s)])], a_re, a_im, log_dt)
    cfg_p = dict(tm_in=1024, tm_even=1024, tm_odd=1024, tm_ffn=1024, tq=512, qg=CHUNK, nseq=1)
    cfg_s = dict(tm_in=bs * seq_s, tq=seq_s, qg=seq_s, nseq=bs)

    mem_k_p, mem_v_p, mem_kf_p, mem_vf_p = _mem_kv(mem_prompt, xattn_wkv.astype(BF16))
    cw = CONV_WIDTH - 1
    zk = jnp.zeros((1, bp, WINDOW, N_KV_HEADS, HEAD_DIM), F32)
    zs = jnp.zeros((1, bp, SSM_GROUPS, SSM_STATE), F32)
    zcm = jnp.zeros((1, bp, cw, D_MODEL), F32)
    zcf = jnp.zeros((DEPTH, bp, cw, D_FF), F32)
    mem_kf_s, mem_vf_s = _mem_flat(cache_mem_k, cache_mem_v)
    mem_p, mem_s = (mem_kf_p, mem_vf_p), (mem_kf_s, mem_vf_s)
    xp, u_p, q_p, kk_p, vv_p = _project_in(x_prompt, 0, w, cfg_p)
    xs, u_s, q_s, kk_s, vv_s = _project_in(x_sample, PAST_LEN, w, cfg_s)
    state = lambda re, im, b: jnp.concatenate([re.reshape(b, S5_STATES), im.reshape(b, S5_STATES)], axis=1)
    ya_p, ya_s, h_p, h_s = _s5_mixer(u_p, u_s, state(zs, zs, bp)[:, None, :], state(state_ssm_re, state_ssm_im, bs)[None],
                                     wg, wy, pow2, powr_p, powr_s, w["ssm_d"], w["w_glu"], 0, ni_p, ni_s)
    split = lambda h, b: (h[:, :S5_STATES].reshape(1, b, SSM_GROUPS, SSM_STATE),
                          h[:, S5_STATES:].reshape(1, b, SSM_GROUPS, SSM_STATE))
    (re_p, im_p), (re_s, im_s) = split(h_p[:, 0], bp), split(h_s[0], bs)
    yb_p, k_p, v_p = _window_attention(q_p, kk_p, vv_p, zk, zk, bp, 0, w, cfg_p)
    yb_s, k_s, v_s = _window_attention(q_s, kk_s, vv_s, cache_win_k, cache_win_v, bs, PAST_LEN, w, cfg_s)
    xp, xs = _mix_even_xattn(xp, xs, ya_p, ya_s, yb_p, yb_s, w["w_out_even"], w["norm_xattn"], w["xattn_wq"],
                             mem_p, mem_s, w["xattn_wo"], 0, 0, cfg_p["tm_even"])
    ffn = functools.partial(_ffn, g=w["norm_ffn"], w_up=w["ffn_w_up"], conv_w=w["ffn_conv_w"], w_down=w["ffn_w_down"],
                            prev_long=zcf, prev_short=state_conv_ffn, g_final=w["norm_final"], tm=cfg_p["tm_ffn"])
    xp, xs, cf0_p, cf0_s = ffn(xp, xs, layer=0, final=False)
    mix_odd = functools.partial(_mix_odd_xattn, gm=w["norm_mix"], w_in=w["w_in_odd"], conv_w=w["conv_mix_w"],
                                w_out=w["w_out_odd"], g=w["norm_xattn"], wq=w["xattn_wq"], wo=w["xattn_wo"],
                                layer=1, layer_o=0)
    xp, cm_p = mix_odd(xp, prev=zcm, mem=mem_p, tm=cfg_p["tm_odd"], nseq=1)
    xs, cm_s = mix_odd(xs, prev=state_conv_mix, mem=mem_s, tm=xs.shape[1], nseq=bs)
    y_p, y_s, cf1_p, cf1_s = ffn(xp, xs, layer=1, final=True)
    return (y_p, y_s.reshape(x_sample.shape), k_p, v_p, re_p, im_p, cm_p[None], jnp.stack([cf0_p, cf1_p]),
            mem_k_p, mem_v_p, k_s, v_s, re_s, im_s, cm_s[None], jnp.stack([cf0_s, cf1_s]))
```

```python
import functools
import math

import numpy as np
import jax
import jax.numpy as jnp
from jax import lax
from jax.experimental import pallas as pl
from jax.experimental.pallas import tpu as pltpu

F32 = jnp.float32
BF16 = jnp.bfloat16

D_MODEL = 1024
DEPTH = 2
PAST_LEN = 2048
CHUNK = 64
SSM_WIDTH = 512
SSM_GROUP = 16
SSM_GROUPS = 32
SSM_STATE = 64
HEAD_DIM = 64
N_HEADS = 8
N_KV_HEADS = 2
GQ = N_HEADS // N_KV_HEADS
WINDOW = 128
ROPE_THETA = 10000.0
KV_WIDTH = N_KV_HEADS * HEAD_DIM
EVEN_IN = SSM_WIDTH + N_HEADS * HEAD_DIM + 2 * KV_WIDTH
CONV_WIDTH = 3
N_MEM = 256
X_HEADS = 4
X_HEAD_DIM = 256
D_FF = 2816
EPS = 1e-6
NEG = -1e30

LANES = 128
SUBLANES = 8
S5_T = 8
S5_TILES = SSM_WIDTH // LANES
S5_TILE_STATES = (LANES // SSM_GROUP) * SSM_STATE
S5_STATES = SSM_GROUPS * SSM_STATE
SWA_STAGE_CHUNKS = 1
XATTN_PARTS = 4
S5_SEG_CHUNKS = 16
MXU_DIM = 256
FF_CHUNK = 6 * MXU_DIM
VMEM_LIMIT = 56 * 1024 * 1024


def _const_spec(shape, index=None):
    idx = tuple(index) if index is not None else (0,) * len(shape)
    return pl.BlockSpec(shape, lambda *_: idx, pipeline_mode=pl.Buffered(1))


def _gain_spec(layer):
    return _const_spec((1, 1, D_MODEL), (layer, 0, 0))


def _params(n_axes):
    return pltpu.CompilerParams(dimension_semantics=("arbitrary",) * n_axes,
                                vmem_limit_bytes=VMEM_LIMIT)


def _rms(x, g):
    return x * lax.rsqrt(jnp.mean(x * x, axis=-1, keepdims=True) + EPS) * g


def _dot(a, b):
    return jnp.dot(a, b.astype(BF16), preferred_element_type=F32)


def _dot_t(a, b):
    return lax.dot_general(a, b, (((1,), (1,)), ((), ())), preferred_element_type=F32)


def _rope(t, cos, sina, sinb):
    n = t.shape[1]
    half = HEAD_DIM // 2
    return t * cos + pltpu.roll(t, n - half, 1) * sina + pltpu.roll(t, half, 1) * sinb


def _inproj_even_body(x_ref, g_ref, w_ref, u_ref, q_ref, k_ref, v_ref, cphi_ref, sphi_ref, *, start, period):
    o0 = SSM_WIDTH
    o1 = o0 + N_HEADS * HEAD_DIM
    o2 = o1 + KV_WIDTH
    tm = x_ref.shape[1]
    half = HEAD_DIM // 2
    lane = lax.broadcasted_iota(jnp.int32, (1, LANES), 1)
    inv = jnp.exp((lane & (half - 1)).astype(F32) * (-2.0 / HEAD_DIM * math.log(ROPE_THETA)))
    first = (lane & (HEAD_DIM - 1)) < half

    @pl.when((pl.program_id(0) == 0) & (pl.program_id(1) == 0))
    def _():
        row = lax.broadcasted_iota(jnp.int32, (tm, LANES), 0)
        phi = (row & (period - 1)).astype(F32) * inv
        cphi_ref[...] = jnp.cos(phi)
        sphi_ref[...] = jnp.sin(phi)

    theta = (start + ((pl.program_id(1) * tm) & (period - 1))).astype(F32) * inv
    c_th, s_th = jnp.cos(theta), jnp.sin(theta)
    nparts = 2 if tm % (2 * LANES) == 0 else 1
    parts = [slice(p * tm // nparts, (p + 1) * tm // nparts) for p in range(nparts)]
    projs = [_dot(_rms(x_ref[0, r, :], g_ref[0]).astype(BF16), w_ref[0]) for r in parts]
    for r, proj in zip(parts, projs):
        c_ph, s_ph = cphi_ref[r, :], sphi_ref[r, :]
        cos = c_th * c_ph - s_th * s_ph
        sin = s_th * c_ph + c_th * s_ph
        sa = jnp.where(first, -sin, 0.0)
        sb = jnp.where(first, 0.0, sin)
        u_ref[0, r, :] = proj[:, :o0]
        q = [_rope(proj[:, c:c + LANES], cos, sa, sb) for c in range(o0, o1, LANES)]
        q_ref[0, r, :] = (jnp.concatenate(q, axis=1) * HEAD_DIM ** -0.5).astype(BF16)
        k_ref[0, r, :] = _rope(proj[:, o1:o2], cos, sa, sb)
        v_ref[0, r, :] = proj[:, o2:]


def _inproj_even(x, g, w_in, start, period, layer_e, tm):
    bsz, seq, _ = x.shape
    tok = lambda w: pl.BlockSpec((1, tm, w), lambda b, i: (b, i, 0))
    return pl.pallas_call(
        functools.partial(_inproj_even_body, start=start, period=period),
        grid=(bsz, seq // tm),
        in_specs=[tok(D_MODEL), _gain_spec(2 * layer_e),
                  _const_spec((1, D_MODEL, EVEN_IN), (layer_e, 0, 0))],
        out_specs=[tok(SSM_WIDTH), tok(N_HEADS * HEAD_DIM), tok(KV_WIDTH), tok(KV_WIDTH)],
        out_shape=[jax.ShapeDtypeStruct((bsz, seq, SSM_WIDTH), F32),
                   jax.ShapeDtypeStruct((bsz, seq, N_HEADS * HEAD_DIM), BF16),
                   jax.ShapeDtypeStruct((bsz, seq, KV_WIDTH), F32),
                   jax.ShapeDtypeStruct((bsz, seq, KV_WIDTH), F32)],
        scratch_shapes=[pltpu.VMEM((tm, LANES), F32)] * 2,
        compiler_params=_params(2),
        name="inproj_even",
    )(x, g, w_in)


def _s5_prep_body(lr_ref, li_ref, ld_ref, bre_ref, bim_ref, cre_ref, cim_ref, wg_ref, wy_ref):
    ts = S5_TILE_STATES
    tw = S5_T * LANES

    def times(a_r, a_i):
        return lambda p: (p[0] * a_r - p[1] * a_i, p[0] * a_i + p[1] * a_r)

    lr, li = lr_ref[0], li_ref[0]
    dt = jnp.exp(ld_ref[0])
    mag = jnp.exp(lr * dt)
    ar, ai = mag * jnp.cos(li * dt), mag * jnp.sin(li * dt)
    mul_a = times(ar, ai)
    nrm = lr * lr + li * li
    cbr = ((ar - 1.0) * lr + ai * li) / nrm
    cbi = (ai * lr - (ar - 1.0) * li) / nrm
    b_re, b_im = bre_ref[0], bim_ref[0]
    bbr = cbr * b_re - cbi * b_im
    bbi = cbr * b_im + cbi * b_re
    p = (jnp.ones_like(ar), jnp.zeros_like(ar))
    for s in reversed(range(S5_T)):
        rows = slice(s * LANES, (s + 1) * LANES)
        wg_ref[0, rows, :ts] = (p[0] * bbr - p[1] * bbi).astype(BF16)
        wg_ref[0, rows, ts:] = (p[0] * bbi + p[1] * bbr).astype(BF16)
        for t in range(s):
            wy_ref[0, rows, t * LANES:(t + 1) * LANES] = jnp.zeros((LANES, LANES), BF16)
        p = mul_a(p)
    to_col = lambda row: jnp.transpose(jnp.broadcast_to(row, (LANES, ts)))
    mul_ac = times(to_col(ar), to_col(ai))
    c_re, c_im = cre_ref[0], cim_ref[0]
    bb = jnp.concatenate([bbr, bbi], axis=1).astype(BF16)
    pc = (jnp.ones_like(c_re), jnp.zeros_like(c_re))
    for k in range(S5_T + 1):
        blk = jnp.concatenate([pc[0] * c_re - pc[1] * c_im, -(pc[0] * c_im + pc[1] * c_re)], axis=0).astype(BF16)
        if k >= 1:
            wy_ref[0, tw:, (k - 1) * LANES:k * LANES] = blk
        if k < S5_T:
            lag = _dot(bb, blk).astype(BF16)
            for s in range(S5_T - k):
                t = s + k
                wy_ref[0, s * LANES:(s + 1) * LANES, t * LANES:(t + 1) * LANES] = lag
        pc = mul_ac(pc)


def _s5_prep(a_re, a_im, log_dt, b_re, b_im, c_re, c_im):
    gpt = LANES // SSM_GROUP
    eye = jnp.eye(gpt, dtype=bool)

    def rows(p):
        return p.reshape(S5_TILES, 1, S5_TILE_STATES)

    def bmat(b):
        bt = b.reshape(S5_TILES, gpt, SSM_STATE, SSM_GROUP).transpose(0, 1, 3, 2)
        full = jnp.where(eye[None, :, None, :, None], bt[:, :, :, None, :], 0.0)
        return full.reshape(S5_TILES, LANES, S5_TILE_STATES)

    def cmat(c):
        ct = c.reshape(S5_TILES, gpt, SSM_GROUP, SSM_STATE).transpose(0, 1, 3, 2)
        full = jnp.where(eye[None, :, None, :, None], ct[:, :, :, None, :], 0.0)
        return full.reshape(S5_TILES, S5_TILE_STATES, LANES)

    row_spec = pl.BlockSpec((1, 1, S5_TILE_STATES), lambda j: (j, 0, 0))
    b_spec = pl.BlockSpec((1, LANES, S5_TILE_STATES), lambda j: (j, 0, 0))
    c_spec = pl.BlockSpec((1, S5_TILE_STATES, LANES), lambda j: (j, 0, 0))
    tw = S5_T * LANES
    return pl.pallas_call(
        _s5_prep_body,
        grid=(S5_TILES,),
        in_specs=[row_spec] * 3 + [b_spec] * 2 + [c_spec] * 2,
        out_specs=[pl.BlockSpec((1, tw, 2 * S5_TILE_STATES), lambda j: (j, 0, 0)),
                   pl.BlockSpec((1, tw + 2 * S5_TILE_STATES, tw), lambda j: (j, 0, 0))],
        out_shape=[jax.ShapeDtypeStruct((S5_TILES, tw, 2 * S5_TILE_STATES), BF16),
                   jax.ShapeDtypeStruct((S5_TILES, tw + 2 * S5_TILE_STATES, tw), BF16)],
        compiler_params=_params(1),
        name="s5_prep",
    )(rows(a_re), rows(a_im), rows(log_dt), bmat(b_re), bmat(b_im), cmat(c_re), cmat(c_im))


def _s5_pow_body(ex_ref, lr_ref, li_ref, ld_ref, *o_refs):
    dt = jnp.exp(ld_ref[...])
    zr, zi = lr_ref[...] * dt, li_ref[...] * dt
    lo = 0
    for o_ref in o_refs:
        ex = ex_ref[lo:lo + o_ref.shape[0], :]
        lo += o_ref.shape[0]
        mag = jnp.exp(ex * zr)
        o_ref[:, :S5_STATES] = mag * jnp.cos(ex * zi)
        o_ref[:, S5_STATES:] = mag * jnp.sin(ex * zi)


def _s5_pow_table(groups, a_re, a_im, log_dt):
    sizes = [len(g) for g in groups]
    ex = jnp.asarray(np.asarray(sum(groups, []), np.float32)[:, None])
    flat = lambda p: p.reshape(1, S5_STATES)
    whole = lambda n, w: pl.BlockSpec((n, w), lambda i: (0, 0))
    return pl.pallas_call(
        _s5_pow_body,
        grid=(1,),
        in_specs=[whole(sum(sizes), 1)] + [whole(1, S5_STATES)] * 3,
        out_specs=[whole(n, 2 * S5_STATES) for n in sizes],
        out_shape=[jax.ShapeDtypeStruct((n, 2 * S5_STATES), F32) for n in sizes],
        compiler_params=_params(1),
        name="s5_pow_table",
    )(ex, flat(a_re), flat(a_im), flat(log_dt))


def _s5_block(u_ref, h0_ref, wg_ref, wy_ref, pow2_ref, powr_ref, d_ref, wglu_ref,
              y_ref, hout_ref, carry_ref, us_ref, ys_ref, *, ni, chained, first):
    ns = S5_STATES
    ts = S5_TILE_STATES
    segtok = ni * S5_T
    pitch = _s5_pitch(ni)

    if chained:
        @pl.when(first)
        def _():
            carry_ref[...] = h0_ref[0]
    else:
        carry_ref[...] = h0_ref[0]

    for j in range(S5_TILES):
        for s in range(SUBLANES):
            us_ref[j, s * pitch:s * pitch + segtok] = u_ref[0, s * segtok:(s + 1) * segtok, j * LANES:(j + 1) * LANES]

    def gather(j, t):
        return jnp.concatenate([us_ref[j, pl.ds(S5_T * i + t, SUBLANES, stride=pitch), :]
                                for i in range(ni)], axis=0)

    def fma(p_r, p_i, x_r, x_i, y_r, y_i):
        return y_r + p_r * x_r - p_i * x_i, y_i + p_r * x_i + p_i * x_r

    xs, g_re, g_im = [], [], []
    for j in range(S5_TILES):
        xj = jnp.concatenate([gather(j, s) for s in range(S5_T)], axis=1).astype(BF16)
        g = _dot(xj, wg_ref[j])
        xs.append(xj)
        g_re.append(g[:, :ts])
        g_im.append(g[:, ts:])
    g_re = jnp.concatenate(g_re, axis=1)
    g_im = jnp.concatenate(g_im, axis=1)
    slab = lambda a, i: a[i * SUBLANES:(i + 1) * SUBLANES]
    power = lambda ref, r: (ref[r:r + 1, :ns], ref[r:r + 1, ns:])
    a_r, a_i = power(powr_ref, 0)
    cin = carry_ref[...]
    h_r, h_i = cin[:, :ns], cin[:, ns:]
    prev = []
    if chained:
        local = [(slab(g_re, 0), slab(g_im, 0))]
        for i in range(1, ni):
            local.append(fma(a_r, a_i, *local[-1], slab(g_re, i), slab(g_im, i)))
        sub = lax.broadcasted_iota(jnp.int32, (SUBLANES, 1), 0)
        c_r = jnp.where(sub == 0, h_r, pltpu.roll(local[-1][0], 1, 0))
        c_i = jnp.where(sub == 0, h_i, pltpu.roll(local[-1][1], 1, 0))
        for m in range(3):
            k = 1 << m
            s_r = jnp.where(sub >= k, pltpu.roll(c_r, k, 0), 0.0)
            s_i = jnp.where(sub >= k, pltpu.roll(c_i, k, 0), 0.0)
            c_r, c_i = fma(*power(pow2_ref, m), s_r, s_i, c_r, c_i)
        h_r, h_i = c_r, c_i
        for i in range(ni):
            prev.append((h_r, h_i))
            h_r, h_i = fma(*power(powr_ref, i), c_r, c_i, *local[i])
        last = jnp.concatenate([h_r[SUBLANES - 1:], h_i[SUBLANES - 1:]], axis=1)
    else:
        for i in range(ni):
            prev.append((h_r, h_i))
            h_r, h_i = fma(a_r, a_i, h_r, h_i, slab(g_re, i), slab(g_im, i))
        last = jnp.concatenate([h_r, h_i], axis=1)
    carry_ref[...] = last
    hout_ref[0] = last
    prev_re = jnp.concatenate([p[0] for p in prev], axis=0)
    prev_im = jnp.concatenate([p[1] for p in prev], axis=0)
    y_tiles = []
    for j in range(S5_TILES):
        lhs = jnp.concatenate([xs[j], prev_re[:, j * ts:(j + 1) * ts].astype(BF16),
                               prev_im[:, j * ts:(j + 1) * ts].astype(BF16)], axis=1)
        y_tiles.append(_dot(lhs, wy_ref[j]))
    d = d_ref[...]
    wglu = wglu_ref[0]
    def gelu_in(t):
        ut = jnp.concatenate([gather(j, t) for j in range(S5_TILES)], axis=1)
        yt = jnp.concatenate([y_tiles[j][:, t * LANES:(t + 1) * LANES] for j in range(S5_TILES)], axis=1)
        return jax.nn.gelu(yt + d * ut)

    def scatter(t, o):
        for j in range(S5_TILES):
            for i in range(ni):
                ys_ref[j, pl.ds(S5_T * i + t, SUBLANES, stride=pitch), :] = slab(o, i)[:, j * LANES:(j + 1) * LANES]

    g_q, z_q = {}, {}
    for step in range(S5_T + 2):
        if step < S5_T:
            g_q[step] = gelu_in(step)
        if 0 <= step - 1 < S5_T:
            z_q[step - 1] = _dot(g_q[step - 1].astype(BF16), wglu)
        if 0 <= step - 2 < S5_T:
            scatter(step - 2, g_q.pop(step - 2) * jax.nn.sigmoid(z_q.pop(step - 2)))
    for j in range(S5_TILES):
        for s in range(SUBLANES):
            y_ref[0, s * segtok:(s + 1) * segtok, j * LANES:(j + 1) * LANES] = (
                ys_ref[j, s * pitch:s * pitch + segtok].astype(BF16))


def _s5_pitch(ni):
    segtok = ni * S5_T
    return segtok + SUBLANES if (segtok // SUBLANES) % 2 == 0 else segtok


def _s5_body(ul_ref, ush_ref, h0l_ref, h0s_ref, wg_ref, wy_ref, pow2_ref, powrl_ref, powrs_ref, d_ref, wglu_ref,
             yl_ref, ysh_ref, houtl_ref, houts_ref, carryl_ref, carrys_ref, us_ref, ys_ref,
             *, ni_long, ni_short, nt, nlong):
    s = pl.program_id(0)
    shared = (wg_ref, wy_ref, pow2_ref)

    @pl.when(s < nlong)
    def _():
        _s5_block(ul_ref, h0l_ref, *shared, powrl_ref, d_ref, wglu_ref, yl_ref, houtl_ref, carryl_ref, us_ref, ys_ref,
                  ni=ni_long, chained=True, first=lax.rem(s, nt) == 0)

    @pl.when(s == nlong)
    def _():
        _s5_block(ush_ref, h0s_ref, *shared, powrs_ref, d_ref, wglu_ref, ysh_ref, houts_ref, carrys_ref, us_ref, ys_ref,
                  ni=ni_short, chained=False, first=None)


def _s5_mixer(u_long, u_short, h0_long, h0_short, wg, wy, pow2, powr_long, powr_short, d, w_glu, layer_e,
              ni_long, ni_short):
    nb, ntok, width = u_long.shape
    tb = SUBLANES * ni_long * S5_T
    nt = ntok // tb
    nlong = nb * nt
    assert u_short.shape[1] == SUBLANES * ni_short * S5_T and ni_short <= ni_long
    lb = lambda s: jnp.minimum(s, nlong - 1) // nt
    li = lambda s: jnp.minimum(s, nlong - 1) % nt
    blk_l = pl.BlockSpec((1, tb, width), lambda s: (lb(s), li(s), 0))
    blk_s = pl.BlockSpec(u_short.shape, lambda s: (0, 0, 0))
    st_l = pl.BlockSpec((1, 1, 2 * S5_STATES), lambda s: (lb(s), 0, 0))
    st_s = pl.BlockSpec(h0_short.shape, lambda s: (0, 0, 0))
    return pl.pallas_call(
        functools.partial(_s5_body, ni_long=ni_long, ni_short=ni_short, nt=nt, nlong=nlong),
        grid=(nlong + 1,),
        in_specs=[blk_l, blk_s, st_l, st_s, _const_spec(wg.shape), _const_spec(wy.shape),
                  _const_spec(pow2.shape), _const_spec(powr_long.shape), _const_spec(powr_short.shape),
                  _const_spec((1, SSM_WIDTH)), _const_spec((1, SSM_WIDTH, SSM_WIDTH), (layer_e, 0, 0))],
        out_specs=[blk_l, blk_s, st_l, st_s],
        out_shape=[jax.ShapeDtypeStruct(u_long.shape, BF16), jax.ShapeDtypeStruct(u_short.shape, BF16),
                   jax.ShapeDtypeStruct(h0_long.shape, F32), jax.ShapeDtypeStruct(h0_short.shape, F32)],
        scratch_shapes=[pltpu.VMEM(h0_long.shape[1:], F32), pltpu.VMEM(h0_short.shape[1:], F32)]
        + [pltpu.VMEM((S5_TILES, SUBLANES * _s5_pitch(ni_long), LANES), F32)] * 2,
        compiler_params=_params(1),
        name="s5_mixer",
    )(u_long, u_short, h0_long, h0_short, wg, wy, pow2, powr_long, powr_short, d, w_glu)


def _swa_body(sink_ref, q_ref, kc_ref, vc_ref, kp_ref, vp_ref, pk_ref, pv_ref, o_ref, *, tq, qg, start):
    first = pl.program_id(1) == 0
    k_prev = jnp.where(first, pk_ref[0], kp_ref[0])
    v_prev = jnp.where(first, pv_ref[0], vp_ref[0])
    k_all = jnp.concatenate([k_prev, kc_ref[0]], axis=0)
    v_all = jnp.concatenate([v_prev, vc_ref[0]], axis=0)
    nk = WINDOW + qg
    lane = lax.broadcasted_iota(jnp.int32, (1, KV_WIDTH), 1)
    slot = lax.broadcasted_iota(jnp.int32, (1, nk), 1)
    qrow = lax.broadcasted_iota(jnp.int32, (2 * qg, 1), 0)
    lower = lane < HEAD_DIM
    placed = []
    for kh in range(N_KV_HEADS):
        own = lower if kh == 0 else jnp.logical_not(lower)
        kz, vz = jnp.where(own, k_all, 0.0), jnp.where(own, v_all, 1.0)
        kr, vr = pltpu.roll(kz, HEAD_DIM, 1), pltpu.roll(vz, HEAD_DIM, 1)
        k_lo, k_hi = (kz, kr) if kh == 0 else (kr, kz)
        v_lo, v_hi = (vz, vr) if kh == 0 else (vr, vz)
        placed.append(((k_lo.astype(BF16), v_lo.astype(BF16)), (k_hi.astype(BF16), v_hi.astype(BF16))))
    combos = [(kh, par) for kh in range(N_KV_HEADS) for par in range(2)]

    def scores(c):
        qc = q_ref[0, c * qg:(c + 1) * qg, :]
        out = []
        for kh, par in combos:
            base = kh * GQ * HEAD_DIM
            qs = jnp.concatenate([qc[:, base:base + LANES], qc[:, base + LANES:base + 2 * LANES]], axis=0)
            out.append(_dot_t(qs, placed[kh][par][0][c * qg:c * qg + nk]))
        return out

    def softmax(c, ss):
        n_bad = jnp.where(first, WINDOW - start - c * qg, 0) if c * qg < WINDOW - start else None
        probs, sinks = [], []
        for (kh, par), s in zip(combos, ss):
            if n_bad is not None:
                s = jnp.where(slot >= n_bad, s, NEG)
            sink = jnp.where(qrow < qg, sink_ref[kh * GQ + par], sink_ref[kh * GQ + 2 + par])
            m = jnp.maximum(jnp.max(s, axis=1, keepdims=True), sink)
            probs.append(jnp.exp(s - m).astype(BF16))
            sinks.append(jnp.exp(sink - m))
        return probs, sinks

    def finish(c, probs, sinks):
        pvs = [_dot(p, placed[kh][par][1][c * qg:c * qg + nk]) for (kh, par), p in zip(combos, probs)]
        outs = []
        for kh in range(N_KV_HEADS):
            pv_e, pv_o = pvs[2 * kh], pvs[2 * kh + 1]
            num = jnp.where(lower, pv_e, pv_o)
            den = pltpu.roll(jnp.where(lower, pv_o, pv_e), HEAD_DIM, 1) + jnp.where(lower, sinks[2 * kh], sinks[2 * kh + 1])
            acc = num / den
            outs += [acc[:qg], acc[qg:]]
        o_ref[0, c * qg:(c + 1) * qg, :] = jnp.concatenate(outs, axis=1).astype(BF16)

    nchunk = tq // qg
    per = min(SWA_STAGE_CHUNKS, nchunk)
    groups = [range(g, min(g + per, nchunk)) for g in range(0, nchunk, per)]
    s_q, p_q = {}, {}
    for step in range(len(groups) + 2):
        if step < len(groups):
            s_q[step] = [scores(c) for c in groups[step]]
        if 0 <= step - 1 < len(groups):
            p_q[step - 1] = [softmax(c, ss) for c, ss in zip(groups[step - 1], s_q.pop(step - 1))]
        if 0 <= step - 2 < len(groups):
            for c, (probs, sinks) in zip(groups[step - 2], p_q.pop(step - 2)):
                finish(c, probs, sinks)


def _swa(q, k, v, past_k, past_v, sinks, start, tq, qg):
    bsz, seq, _ = q.shape
    ntiles = seq // tq
    tok = lambda w: pl.BlockSpec((1, tq, w), lambda b, i: (b, i, 0))
    past = pl.BlockSpec((1, WINDOW, KV_WIDTH), lambda b, i: (b, 0, 0))
    if ntiles > 1:
        per = tq // WINDOW
        prev = pl.BlockSpec((1, WINDOW, KV_WIDTH), lambda b, i: (b, jnp.maximum(i * per - 1, 0), 0))
        k_prev, v_prev = k, v
    else:
        prev, k_prev, v_prev = past, past_k, past_v
    return pl.pallas_call(
        functools.partial(_swa_body, tq=tq, qg=qg, start=start),
        grid=(bsz, ntiles),
        in_specs=[pl.BlockSpec(memory_space=pltpu.SMEM), tok(N_HEADS * HEAD_DIM), tok(KV_WIDTH), tok(KV_WIDTH),
                  prev, prev, past, past],
        out_specs=tok(N_HEADS * HEAD_DIM),
        out_shape=jax.ShapeDtypeStruct((bsz, seq, N_HEADS * HEAD_DIM), BF16),
        compiler_params=_params(2),
        name="swa",
    )(sinks, q, k, v, k_prev, v_prev, past_k, past_v)


def _xattn(x1, g, wq, mk_ref, mv_ref, wo, nseq, o_ref):
    h = _rms(x1, g).astype(BF16)
    q = (_dot(h, wq) * X_HEAD_DIM ** -0.5).astype(BF16)
    nparts = nseq if nseq > 1 else XATTN_PARTS
    prows = x1.shape[0] // nparts
    heads = [slice(hd * X_HEAD_DIM, (hd + 1) * X_HEAD_DIM) for hd in range(X_HEADS)]

    def scores(p):
        r, b = slice(p * prows, (p + 1) * prows), (p if nseq > 1 else 0)
        return [_dot_t(q[r, c], mk_ref[0, b, :, c]) for c in heads]

    def softmax(ss):
        out = []
        for s in ss:
            e = jnp.exp(s - jnp.max(s, axis=1, keepdims=True))
            out.append((e.astype(BF16), jnp.sum(e, axis=1, keepdims=True)))
        return out

    def values(p, pd):
        b = p if nseq > 1 else 0
        return jnp.concatenate([_dot(e, mv_ref[0, b, :, c]) / den for c, (e, den) in zip(heads, pd)],
                               axis=1).astype(BF16)

    def finish(p, pd):
        r = slice(p * prows, (p + 1) * prows)
        o_ref[0, r, :] = x1[r] + _dot(values(p, pd), wo)

    if nseq > 1:
        pds = [softmax(ss) for ss in [scores(p) for p in range(nparts)]]
        o = jnp.concatenate([values(p, pd) for p, pd in enumerate(pds)], axis=0)
        o_ref[0] = x1 + _dot(o, wo)
        return
    s_q, p_q = {}, {}
    for step in range(nparts + 2):
        if step < nparts:
            s_q[step] = scores(step)
        if 0 <= step - 1 < nparts:
            p_q[step - 1] = softmax(s_q.pop(step - 1))
        if 0 <= step - 2 < nparts:
            finish(step - 2, p_q.pop(step - 2))


def _conv3(cur, carry, w):
    n = cur.shape[0]
    row = lax.broadcasted_iota(jnp.int32, (n, 1), 0)
    nc = carry.shape[0]
    c1, c2 = carry[nc - 1:nc], carry[nc - 2:nc - 1]
    m1 = jnp.where(row == 0, c1, pltpu.roll(cur, 1, 0))
    m2 = jnp.where(row == 0, c2, jnp.where(row == 1, c1, pltpu.roll(cur, 2, 0)))
    return w[0:1] * m2 + w[1:2] * m1 + w[2:3] * cur


def _conv3_seqs(cur, carries, w):
    nseq = len(carries)
    if nseq == 1:
        return _conv3(cur, carries[0], w)
    rows = cur.shape[0] // nseq
    body = _conv3(cur, carries[0], w)
    pieces = []
    for b in range(nseq):
        pieces.append(_conv3(cur[b * rows:b * rows + SUBLANES], carries[b], w))
        pieces.append(body[b * rows + SUBLANES:(b + 1) * rows])
    return jnp.concatenate(pieces, axis=0)


def _store_last_rows(st_ref, cols, cur, nseq):
    rows = cur.shape[0] // nseq
    for b in range(nseq):
        st_ref[b, :, cols] = cur[(b + 1) * rows - (CONV_WIDTH - 1):(b + 1) * rows]


def _mix_even_tile(x_ref, ya_ref, yb_ref, wout_ref, g_ref, wq_ref, mk_ref, mv_ref, wo_ref, o_ref, *, nseq):
    x1 = x_ref[0] + _dot(jnp.concatenate([ya_ref[0], yb_ref[0]], axis=1), wout_ref[0])
    _xattn(x1, g_ref[0], wq_ref[0], mk_ref, mv_ref, wo_ref[0], nseq, o_ref)


def _mix_even_xattn_body(xl_ref, xs_ref, yal_ref, yas_ref, ybl_ref, ybs_ref, wout_ref, g_ref, wq_ref,
                         mkl_ref, mks_ref, mvl_ref, mvs_ref, wo_ref, ol_ref, os_ref, *, nlong, nseq_s):
    s = pl.program_id(0)

    @pl.when(s < nlong)
    def _():
        _mix_even_tile(xl_ref, yal_ref, ybl_ref, wout_ref, g_ref, wq_ref, mkl_ref, mvl_ref, wo_ref, ol_ref, nseq=1)

    @pl.when(s == nlong)
    def _():
        _mix_even_tile(xs_ref, yas_ref, ybs_ref, wout_ref, g_ref, wq_ref, mks_ref, mvs_ref, wo_ref, os_ref, nseq=nseq_s)


def _long_short_specs(x_long, x_short, tm):
    nb, seq, _ = x_long.shape
    nt = seq // tm
    nlong = nb * nt
    lb = lambda s: jnp.minimum(s, nlong - 1) // nt
    li = lambda s: jnp.minimum(s, nlong - 1) % nt
    tok_long = lambda w: pl.BlockSpec((1, tm, w), lambda s: (lb(s), li(s), 0))
    tok_short = lambda w: pl.BlockSpec((1, x_short.shape[1], w), lambda s: (0, 0, 0))
    return nlong, nt, lb, tok_long, tok_short


def _mix_even_xattn(x_long, x_short, ya_long, ya_short, yb_long, yb_short, w_out, g, wq, mem_long, mem_short, wo,
                    layer, layer_e, tm):
    nlong, _, lb, tok_l, tok_s = _long_short_specs(x_long, x_short, tm)
    nseq_s = mem_short[0].shape[1]
    mem_l = pl.BlockSpec((1, 1, N_MEM, D_MODEL), lambda s: (layer, lb(s), 0, 0))
    mem_s = _const_spec((1, nseq_s, N_MEM, D_MODEL), (layer, 0, 0, 0))
    half = D_MODEL // 2
    sq = lambda l: _const_spec((1, D_MODEL, D_MODEL), (l, 0, 0))
    return pl.pallas_call(
        functools.partial(_mix_even_xattn_body, nlong=nlong, nseq_s=nseq_s),
        grid=(nlong + 1,),
        in_specs=[tok_l(D_MODEL), tok_s(D_MODEL), tok_l(half), tok_s(half), tok_l(half), tok_s(half),
                  sq(layer_e), _gain_spec(layer), sq(layer), mem_l, mem_s, mem_l, mem_s, sq(layer)],
        out_specs=[tok_l(D_MODEL), tok_s(D_MODEL)],
        out_shape=[jax.ShapeDtypeStruct(x_long.shape, F32), jax.ShapeDtypeStruct(x_short.shape, F32)],
        compiler_params=_params(1),
        name="mix_even_xattn",
    )(x_long, x_short, ya_long, ya_short, yb_long, yb_short, w_out, g, wq,
      mem_long[0], mem_short[0], mem_long[1], mem_short[1], wo)


def _mix_odd_tile(x_ref, gm_ref, win_ref, cw_ref, wout_ref, prev_ref, g_ref, wq_ref, mk_ref, mv_ref, wo_ref,
                  o_ref, st_ref, carry_ref, *, tm, nseq, first):
    keep = CONV_WIDTH - 1
    if nseq == 1:
        @pl.when(first)
        def _():
            carry_ref[SUBLANES - keep:, :] = prev_ref[0, 0]

    x = x_ref[0]
    proj = _dot(_rms(x, gm_ref[0]).astype(BF16), win_ref[0])
    gate_b, gate_c, z = proj[:, :D_MODEL], proj[:, D_MODEL:2 * D_MODEL], proj[:, 2 * D_MODEL:]
    cz = gate_c * z
    carries = [carry_ref[SUBLANES - keep:, :]] if nseq == 1 else [prev_ref[0, b] for b in range(nseq)]
    z_conv = _conv3_seqs(cz, carries, cw_ref[0])
    if nseq == 1:
        carry_ref[...] = cz[tm - SUBLANES:]
    _store_last_rows(st_ref, slice(None), cz, nseq)
    x1 = x + _dot((gate_b * z_conv).astype(BF16), wout_ref[0])
    _xattn(x1, g_ref[0], wq_ref[0], mk_ref, mv_ref, wo_ref[0], nseq, o_ref)


def _mix_odd_xattn_body(*refs, tm, nseq):
    _mix_odd_tile(*refs, tm=tm, nseq=nseq, first=pl.program_id(1) == 0)


def _mix_odd_xattn(x, gm, w_in, conv_w, w_out, prev, g, wq, mem, wo, layer, layer_o, tm, nseq):
    bsz, seq, _ = x.shape
    keep = CONV_WIDTH - 1
    tok = pl.BlockSpec((1, tm, D_MODEL), lambda b, i: (b, i, 0))
    mem_spec = pl.BlockSpec((1, nseq, N_MEM, D_MODEL), lambda b, i: (layer, b, 0, 0))
    hist = pl.BlockSpec((1, nseq, keep, D_MODEL), lambda b, i: (layer_o, b, 0, 0))
    st = pl.BlockSpec((nseq, keep, D_MODEL), lambda b, i: (b, 0, 0))
    sq = lambda l: _const_spec((1, D_MODEL, D_MODEL), (l, 0, 0))
    return pl.pallas_call(
        functools.partial(_mix_odd_xattn_body, tm=tm, nseq=nseq),
        grid=(bsz, seq // tm),
        in_specs=[tok, _gain_spec(layer), _const_spec((1, D_MODEL, 3 * D_MODEL), (layer_o, 0, 0)),
                  _const_spec((1, CONV_WIDTH, D_MODEL), (layer_o, 0, 0)), sq(layer_o), hist,
                  _gain_spec(layer), sq(layer), mem_spec, mem_spec, sq(layer)],
        out_specs=[tok, st],
        out_shape=[jax.ShapeDtypeStruct(x.shape, F32), jax.ShapeDtypeStruct((bsz * nseq, keep, D_MODEL), F32)],
        scratch_shapes=[pltpu.VMEM((SUBLANES, D_MODEL), F32)],
        compiler_params=_params(2),
        name="mix_odd_xattn",
    )(x, gm, w_in, conv_w, w_out, prev, g, wq, mem[0], mem[1], wo)


def _ffn_tile(x_ref, g_ref, wup_ref, cw_ref, wdn_ref, prev_ref, gfin_ref, o_ref, st_ref, carry_ref,
              *, tm, final, nseq, first):
    if nseq == 1:
        @pl.when(first)
        def _():
            carry_ref[SUBLANES - (CONV_WIDTH - 1):, :] = prev_ref[0, 0]

    x = x_ref[0]
    h = _rms(x, g_ref[0]).astype(BF16)
    chunks = [slice(lo, min(lo + FF_CHUNK, D_FF)) for lo in range(0, D_FF, FF_CHUNK)]

    def up(cols):
        return _dot(h, wup_ref[0, :, cols]), _dot(h, wup_ref[0, :, D_FF + cols.start:D_FF + cols.stop])

    acc = x
    nxt = up(chunks[0])
    for n, cols in enumerate(chunks):
        (gate, val), nxt = nxt, (up(chunks[n + 1]) if n + 1 < len(chunks) else None)
        carries = ([carry_ref[SUBLANES - (CONV_WIDTH - 1):, cols]] if nseq == 1
                   else [prev_ref[0, b, :, cols] for b in range(nseq)])
        gate_c = _conv3_seqs(gate, carries, cw_ref[0, :, cols])
        if nseq == 1:
            carry_ref[:, cols] = gate[tm - SUBLANES:]
        _store_last_rows(st_ref, cols, gate, nseq)
        act = (gate_c * jax.nn.sigmoid(gate_c) * val).astype(BF16)
        if n + 1 < len(chunks):
            acc = acc + _dot(act, wdn_ref[0, cols, :])
        else:
            half = tm // 2
            for rows in (slice(0, half), slice(half, tm)):
                out = acc[rows] + _dot(act[rows], wdn_ref[0, cols, :])
                o_ref[0, rows, :] = _rms(out, gfin_ref[0]) if final else out


def _ffn_body(xl_ref, xs_ref, g_ref, wup_ref, cw_ref, wdn_ref, prevl_ref, prevs_ref, gfin_ref,
              ol_ref, os_ref, stl_ref, sts_ref, carry_ref, *, tm, nt, nlong, rows_s, nseq_s, final):
    s = pl.program_id(0)
    shared = (g_ref, wup_ref, cw_ref, wdn_ref)

    @pl.when(s < nlong)
    def _():
        _ffn_tile(xl_ref, *shared, prevl_ref, gfin_ref, ol_ref, stl_ref, carry_ref,
                  tm=tm, final=final, nseq=1, first=lax.rem(s, nt) == 0)

    @pl.when(s == nlong)
    def _():
        _ffn_tile(xs_ref, *shared, prevs_ref, gfin_ref, os_ref, sts_ref, carry_ref,
                  tm=rows_s, final=final, nseq=nseq_s, first=None)


def _ffn(x_long, x_short, g, w_up, conv_w, w_down, prev_long, prev_short, g_final, layer, tm, final):
    nb, seq, _ = x_long.shape
    rows_s = x_short.shape[1]
    nseq_s = prev_short.shape[1]
    nt = seq // tm
    nlong = nb * nt
    lb = lambda s: jnp.minimum(s, nlong - 1) // nt
    li = lambda s: jnp.minimum(s, nlong - 1) % nt
    keep = CONV_WIDTH - 1
    tok_l = pl.BlockSpec((1, tm, D_MODEL), lambda s: (lb(s), li(s), 0))
    tok_s = pl.BlockSpec((1, rows_s, D_MODEL), lambda s: (0, 0, 0))
    hist_l = pl.BlockSpec((1, 1, keep, D_FF), lambda s: (layer, lb(s), 0, 0))
    hist_s = pl.BlockSpec((1, nseq_s, keep, D_FF), lambda s: (layer, 0, 0, 0))
    st_l = pl.BlockSpec((1, keep, D_FF), lambda s: (lb(s), 0, 0))
    st_s = pl.BlockSpec((nseq_s, keep, D_FF), lambda s: (0, 0, 0))
    return pl.pallas_call(
        functools.partial(_ffn_body, tm=tm, nt=nt, nlong=nlong, rows_s=rows_s, nseq_s=nseq_s, final=final),
        grid=(nlong + 1,),
        in_specs=[tok_l, tok_s, _gain_spec(layer), _const_spec((1, D_MODEL, 2 * D_FF), (layer, 0, 0)),
                  _const_spec((1, CONV_WIDTH, D_FF), (layer, 0, 0)), _const_spec((1, D_FF, D_MODEL), (layer, 0, 0)),
                  hist_l, hist_s, _gain_spec(0)],
        out_specs=[tok_l, tok_s, st_l, st_s],
        out_shape=[jax.ShapeDtypeStruct(x_long.shape, F32), jax.ShapeDtypeStruct(x_short.shape, F32),
                   jax.ShapeDtypeStruct((nb, keep, D_FF), F32), jax.ShapeDtypeStruct((nseq_s, keep, D_FF), F32)],
        scratch_shapes=[pltpu.VMEM((SUBLANES, D_FF), F32)],
        compiler_params=_params(1),
        name="conv_ffn",
    )(x_long, x_short, g, w_up, conv_w, w_down, prev_long, prev_short, g_final)


def _mem_flat_copies(k_hbm, v_hbm, kbuf, vbuf, sem, step, slot, nbatch):
    lyr, b = step // nbatch, step % nbatch
    return [pltpu.make_async_copy(src.at[lyr, b, :, hd, :], buf.at[slot, hd], sem.at[slot, i, hd])
            for i, (src, buf) in enumerate(((k_hbm, kbuf), (v_hbm, vbuf))) for hd in range(X_HEADS)]


def _mem_flat_body(k_hbm, v_hbm, ko_ref, vo_ref, kbuf, vbuf, sem, *, nbatch):
    s, n = pl.program_id(0), pl.num_programs(0)
    copies = functools.partial(_mem_flat_copies, k_hbm, v_hbm, kbuf, vbuf, sem, nbatch=nbatch)

    @pl.when(s == 0)
    def _():
        for c in copies(step=s, slot=0):
            c.start()

    @pl.when(s + 1 < n)
    def _():
        for c in copies(step=s + 1, slot=(s + 1) % 2):
            c.start()

    slot = s % 2
    for c in copies(step=s, slot=slot):
        c.wait()
    for hd in range(X_HEADS):
        cols = slice(hd * X_HEAD_DIM, (hd + 1) * X_HEAD_DIM)
        ko_ref[0, 0, :, cols] = kbuf[slot, hd].astype(BF16)
        vo_ref[0, 0, :, cols] = vbuf[slot, hd].astype(BF16)


def _mem_flat(mem_k, mem_v):
    depth, nbatch = mem_k.shape[:2]
    flat = pl.BlockSpec((1, 1, N_MEM, D_MODEL), lambda s: (s // nbatch, s % nbatch, 0, 0))
    return pl.pallas_call(
        functools.partial(_mem_flat_body, nbatch=nbatch),
        grid=(depth * nbatch,),
        in_specs=[pl.BlockSpec(memory_space=pl.ANY)] * 2,
        out_specs=[flat, flat],
        out_shape=[jax.ShapeDtypeStruct((depth, nbatch, N_MEM, D_MODEL), BF16)] * 2,
        scratch_shapes=[pltpu.VMEM((2, X_HEADS, N_MEM, X_HEAD_DIM), F32)] * 2
        + [pltpu.SemaphoreType.DMA((2, 2, X_HEADS))],
        compiler_params=_params(1),
        name="mem_flat",
    )(mem_k, mem_v)


def _mem_kv_body(m_ref, w_ref, k_ref, v_ref, kf_ref, vf_ref, wb_ref):
    @pl.when(pl.program_id(1) == 0)
    def _():
        wb_ref[...] = w_ref[0].astype(BF16)

    kv = _dot(m_ref[0].astype(BF16), wb_ref[...])
    kf_ref[0, 0] = kv[:, :D_MODEL].astype(BF16)
    vf_ref[0, 0] = kv[:, D_MODEL:].astype(BF16)
    for hd in range(X_HEADS):
        k_ref[0, 0, :, hd, :] = kv[:, hd * X_HEAD_DIM:(hd + 1) * X_HEAD_DIM]
        v_ref[0, 0, :, hd, :] = kv[:, D_MODEL + hd * X_HEAD_DIM:D_MODEL + (hd + 1) * X_HEAD_DIM]


def _mem_kv(mem, w_kv):
    bsz = mem.shape[0]
    heads = pl.BlockSpec((1, 1, N_MEM, X_HEADS, X_HEAD_DIM), lambda l, b: (l, b, 0, 0, 0))
    flat = pl.BlockSpec((1, 1, N_MEM, D_MODEL), lambda l, b: (l, b, 0, 0))
    return pl.pallas_call(
        _mem_kv_body,
        grid=(DEPTH, bsz),
        in_specs=[pl.BlockSpec((1, N_MEM, D_MODEL), lambda l, b: (b, 0, 0)),
                  pl.BlockSpec((1, D_MODEL, 2 * D_MODEL), lambda l, b: (l, 0, 0))],
        out_specs=[heads, heads, flat, flat],
        out_shape=[jax.ShapeDtypeStruct((DEPTH, bsz, N_MEM, X_HEADS, X_HEAD_DIM), F32)] * 2
        + [jax.ShapeDtypeStruct((DEPTH, bsz, N_MEM, D_MODEL), BF16)] * 2,
        scratch_shapes=[pltpu.VMEM((D_MODEL, 2 * D_MODEL), BF16)],
        compiler_params=_params(2),
        name="mem_kv",
    )(mem, w_kv)


def _project_in(x, start, w, cfg):
    bsz, seq, _ = x.shape
    nseq = cfg["nseq"]
    xg = x.reshape(bsz // nseq, nseq * seq, D_MODEL)
    return (xg,) + tuple(_inproj_even(xg, w["norm_mix"], w["w_in_even"], start, seq, 0, cfg["tm_in"]))


def _window_attention(q, k, v, past_k, past_v, bsz, start, w, cfg):
    seq = q.shape[0] * q.shape[1] // bsz
    per_seq = lambda t: t.reshape(bsz, seq, t.shape[-1])
    k, v = per_seq(k), per_seq(v)
    yb = _swa(per_seq(q), k, v, past_k.reshape(bsz, WINDOW, KV_WIDTH), past_v.reshape(bsz, WINDOW, KV_WIDTH),
              w["attn_sinks"], start, cfg["tq"], cfg["qg"])
    keep = min(seq, WINDOW)
    new_k = k[:, seq - keep:].reshape(1, bsz, keep, N_KV_HEADS, HEAD_DIM)
    new_v = v[:, seq - keep:].reshape(1, bsz, keep, N_KV_HEADS, HEAD_DIM)
    return yb.reshape(q.shape), new_k, new_v


def kernel(x_prompt, x_sample, mem_prompt, cache_win_k, cache_win_v, state_ssm_re, state_ssm_im, state_conv_mix, state_conv_ffn, cache_mem_k, cache_mem_v, norm_mix, norm_xattn, norm_ffn, norm_final, w_in_even, w_out_even, ssm_a_re, ssm_a_im, ssm_log_dt, ssm_b_re, ssm_b_im, ssm_c_re, ssm_c_im, ssm_d, w_glu, attn_sinks, w_in_odd, conv_mix_w, w_out_odd, xattn_wq, xattn_wkv, xattn_wo, ffn_w_up, ffn_conv_w, ffn_w_down):
    bp, seq_p, _ = x_prompt.shape
    bs, seq_s, _ = x_sample.shape
    gains = lambda g: g.reshape(-1, 1, D_MODEL)
    w = dict(norm_mix=gains(norm_mix), norm_xattn=gains(norm_xattn), norm_ffn=gains(norm_ffn),
             norm_final=gains(norm_final),
             w_in_even=w_in_even, w_out_even=w_out_even,
             ssm_d=ssm_d.reshape(1, SSM_WIDTH), w_glu=w_glu.astype(BF16), attn_sinks=attn_sinks.reshape(N_HEADS),
             w_in_odd=w_in_odd, conv_mix_w=conv_mix_w, w_out_odd=w_out_odd,
             xattn_wq=xattn_wq, xattn_wo=xattn_wo,
             ffn_w_up=ffn_w_up.astype(BF16), ffn_conv_w=ffn_conv_w, ffn_w_down=ffn_w_down.astype(BF16))

    gp = (SSM_GROUPS, SSM_STATE)
    a_re, a_im, log_dt = ssm_a_re.reshape(gp), ssm_a_im.reshape(gp), ssm_log_dt.reshape(gp)
    wg, wy = _s5_prep(a_re, a_im, log_dt, ssm_b_re.reshape(gp + (SSM_GROUP,)), ssm_b_im.reshape(gp + (SSM_GROUP,)),
                      ssm_c_re.reshape(SSM_GROUPS, SSM_GROUP, SSM_STATE), ssm_c_im.reshape(SSM_GROUPS, SSM_GROUP, SSM_STATE))
    ni_p, ni_s = S5_SEG_CHUNKS, seq_s // S5_T
    padded = lambda e: e + [1.0] * (-len(e) % SUBLANES)
    pow2, powr_p, powr_s = _s5_pow_table(
        [padded([float(S5_T * ni_p << m) for m in range(3)]),
         padded([float(S5_T * (i + 1)) for i in range(ni_p)]),
         padded([float(S5_T * (i + 1)) for i in range(ni_s)])], a_re, a_im, log_dt)
    cfg_p = dict(tm_in=1024, tm_even=1024, tm_odd=1024, tm_ffn=1024, tq=512, qg=CHUNK, nseq=1)
    cfg_s = dict(tm_in=bs * seq_s, tq=seq_s, qg=seq_s, nseq=bs)

    mem_k_p, mem_v_p, mem_kf_p, mem_vf_p = _mem_kv(mem_prompt, xattn_wkv)
    cw = CONV_WIDTH - 1
    zk = jnp.zeros((1, bp, WINDOW, N_KV_HEADS, HEAD_DIM), F32)
    zs = jnp.zeros((1, bp, SSM_GROUPS, SSM_STATE), F32)
    zcm = jnp.zeros((1, bp, cw, D_MODEL), F32)
    zcf = jnp.zeros((DEPTH, bp, cw, D_FF), F32)
    mem_kf_s, mem_vf_s = _mem_flat(cache_mem_k, cache_mem_v)
    mem_p, mem_s = (mem_kf_p, mem_vf_p), (mem_kf_s, mem_vf_s)
    xp, u_p, q_p, kk_p, vv_p = _project_in(x_prompt, 0, w, cfg_p)
    xs, u_s, q_s, kk_s, vv_s = _project_in(x_sample, PAST_LEN, w, cfg_s)
    state = lambda re, im, b: jnp.concatenate([re.reshape(b, S5_STATES), im.reshape(b, S5_STATES)], axis=1)
    ya_p, ya_s, h_p, h_s = _s5_mixer(u_p, u_s, state(zs, zs, bp)[:, None, :], state(state_ssm_re, state_ssm_im, bs)[None],
                                     wg, wy, pow2, powr_p, powr_s, w["ssm_d"], w["w_glu"], 0, ni_p, ni_s)
    split = lambda h, b: (h[:, :S5_STATES].reshape(1, b, SSM_GROUPS, SSM_STATE),
                          h[:, S5_STATES:].reshape(1, b, SSM_GROUPS, SSM_STATE))
    (re_p, im_p), (re_s, im_s) = split(h_p[:, 0], bp), split(h_s[0], bs)
    yb_p, k_p, v_p = _window_attention(q_p, kk_p, vv_p, zk, zk, bp, 0, w, cfg_p)
    yb_s, k_s, v_s = _window_attention(q_s, kk_s, vv_s, cache_win_k, cache_win_v, bs, PAST_LEN, w, cfg_s)
    xp, xs = _mix_even_xattn(xp, xs, ya_p, ya_s, yb_p, yb_s, w["w_out_even"], w["norm_xattn"], w["xattn_wq"],
                             mem_p, mem_s, w["xattn_wo"], 0, 0, cfg_p["tm_even"])
    ffn = functools.partial(_ffn, g=w["norm_ffn"], w_up=w["ffn_w_up"], conv_w=w["ffn_conv_w"], w_down=w["ffn_w_down"],
                            prev_long=zcf, prev_short=state_conv_ffn, g_final=w["norm_final"], tm=cfg_p["tm_ffn"])
    xp, xs, cf0_p, cf0_s = ffn(xp, xs, layer=0, final=False)
    mix_odd = functools.partial(_mix_odd_xattn, gm=w["norm_mix"], w_in=w["w_in_odd"], conv_w=w["conv_mix_w"],
                                w_out=w["w_out_odd"], g=w["norm_xattn"], wq=w["xattn_wq"], wo=w["xattn_wo"],
                                layer=1, layer_o=0)
    xp, cm_p = mix_odd(xp, prev=zcm, mem=mem_p, tm=cfg_p["tm_odd"], nseq=1)
    xs, cm_s = mix_odd(xs, prev=state_conv_mix, mem=mem_s, tm=xs.shape[1], nseq=bs)
    y_p, y_s, cf1_p, cf1_s = ffn(xp, xs, layer=1, final=True)
    return (y_p, y_s.reshape(x_sample.shape), k_p, v_p, re_p, im_p, cm_p[None], jnp.stack([cf0_p, cf1_p]),
            mem_k_p, mem_v_p, k_s, v_s, re_s, im_s, cm_s[None], jnp.stack([cf0_s, cf1_s]))
```

```python
import functools
import math

import numpy as np
import jax
import jax.numpy as jnp
from jax import lax
from jax.experimental import pallas as pl
from jax.experimental.pallas import tpu as pltpu

F32 = jnp.float32
BF16 = jnp.bfloat16

D_MODEL = 1024
DEPTH = 2
PAST_LEN = 2048
CHUNK = 64
SSM_WIDTH = 512
SSM_GROUP = 16
SSM_GROUPS = 32
SSM_STATE = 64
HEAD_DIM = 64
N_HEADS = 8
N_KV_HEADS = 2
GQ = N_HEADS // N_KV_HEADS
WINDOW = 128
ROPE_THETA = 10000.0
KV_WIDTH = N_KV_HEADS * HEAD_DIM
EVEN_IN = SSM_WIDTH + N_HEADS * HEAD_DIM + 2 * KV_WIDTH
CONV_WIDTH = 3
N_MEM = 256
X_HEADS = 4
X_HEAD_DIM = 256
D_FF = 2816
EPS = 1e-6
NEG = -1e30

LANES = 128
SUBLANES = 8
S5_T = 8
S5_TILES = SSM_WIDTH // LANES
S5_TILE_STATES = (LANES // SSM_GROUP) * SSM_STATE
S5_STATES = SSM_GROUPS * SSM_STATE
SWA_STAGE_CHUNKS = 1
XATTN_PARTS = 4
S5_SEG_CHUNKS = 16
MXU_DIM = 256
FF_CHUNK = 6 * MXU_DIM
VMEM_LIMIT = 56 * 1024 * 1024


def _const_spec(shape, index=None):
    idx = tuple(index) if index is not None else (0,) * len(shape)
    return pl.BlockSpec(shape, lambda *_: idx, pipeline_mode=pl.Buffered(1))


def _gain_spec(layer):
    return _const_spec((1, 1, D_MODEL), (layer, 0, 0))


def _params(n_axes):
    return pltpu.CompilerParams(dimension_semantics=("arbitrary",) * n_axes,
                                vmem_limit_bytes=VMEM_LIMIT)


def _rms(x, g):
    return x * lax.rsqrt(jnp.mean(x * x, axis=-1, keepdims=True) + EPS) * g


def _dot(a, b):
    return jnp.dot(a, b.astype(BF16), preferred_element_type=F32)


def _dot_t(a, b):
    return lax.dot_general(a, b, (((1,), (1,)), ((), ())), preferred_element_type=F32)


def _rope(t, cos, sina, sinb):
    n = t.shape[1]
    half = HEAD_DIM // 2
    return t * cos + pltpu.roll(t, n - half, 1) * sina + pltpu.roll(t, half, 1) * sinb


def _inproj_even_body(x_ref, g_ref, w_ref, u_ref, q_ref, k_ref, v_ref, cphi_ref, sphi_ref, *, start, period):
    o0 = SSM_WIDTH
    o1 = o0 + N_HEADS * HEAD_DIM
    o2 = o1 + KV_WIDTH
    tm = x_ref.shape[1]
    half = HEAD_DIM // 2
    lane = lax.broadcasted_iota(jnp.int32, (1, LANES), 1)
    inv = jnp.exp((lane & (half - 1)).astype(F32) * (-2.0 / HEAD_DIM * math.log(ROPE_THETA)))
    first = (lane & (HEAD_DIM - 1)) < half

    @pl.when((pl.program_id(0) == 0) & (pl.program_id(1) == 0))
    def _():
        row = lax.broadcasted_iota(jnp.int32, (tm, LANES), 0)
        phi = (row & (period - 1)).astype(F32) * inv
        cphi_ref[...] = jnp.cos(phi)
        sphi_ref[...] = jnp.sin(phi)

    theta = (start + ((pl.program_id(1) * tm) & (period - 1))).astype(F32) * inv
    c_th, s_th = jnp.cos(theta), jnp.sin(theta)
    nparts = 2 if tm % (2 * LANES) == 0 else 1
    parts = [slice(p * tm // nparts, (p + 1) * tm // nparts) for p in range(nparts)]
    projs = [_dot(_rms(x_ref[0, r, :], g_ref[0]).astype(BF16), w_ref[0]) for r in parts]
    for r, proj in zip(parts, projs):
        c_ph, s_ph = cphi_ref[r, :], sphi_ref[r, :]
        cos = c_th * c_ph - s_th * s_ph
        sin = s_th * c_ph + c_th * s_ph
        sa = jnp.where(first, -sin, 0.0)
        sb = jnp.where(first, 0.0, sin)
        u_ref[0, r, :] = proj[:, :o0]
        q = [_rope(proj[:, c:c + LANES], cos, sa, sb) for c in range(o0, o1, LANES)]
        q_ref[0, r, :] = (jnp.concatenate(q, axis=1) * HEAD_DIM ** -0.5).astype(BF16)
        k_ref[0, r, :] = _rope(proj[:, o1:o2], cos, sa, sb)
        v_ref[0, r, :] = proj[:, o2:]


def _inproj_even(x, g, w_in, start, period, layer_e, tm):
    bsz, seq, _ = x.shape
    tok = lambda w: pl.BlockSpec((1, tm, w), lambda b, i: (b, i, 0))
    return pl.pallas_call(
        functools.partial(_inproj_even_body, start=start, period=period),
        grid=(bsz, seq // tm),
        in_specs=[tok(D_MODEL), _gain_spec(2 * layer_e),
                  _const_spec((1, D_MODEL, EVEN_IN), (layer_e, 0, 0))],
        out_specs=[tok(SSM_WIDTH), tok(N_HEADS * HEAD_DIM), tok(KV_WIDTH), tok(KV_WIDTH)],
        out_shape=[jax.ShapeDtypeStruct((bsz, seq, SSM_WIDTH), F32),
                   jax.ShapeDtypeStruct((bsz, seq, N_HEADS * HEAD_DIM), BF16),
                   jax.ShapeDtypeStruct((bsz, seq, KV_WIDTH), F32),
                   jax.ShapeDtypeStruct((bsz, seq, KV_WIDTH), F32)],
        scratch_shapes=[pltpu.VMEM((tm, LANES), F32)] * 2,
        compiler_params=_params(2),
        name="inproj_even",
    )(x, g, w_in)


def _s5_prep_body(lr_ref, li_ref, ld_ref, bre_ref, bim_ref, cre_ref, cim_ref, wg_ref, wy_ref):
    ts = S5_TILE_STATES
    tw = S5_T * LANES

    def times(a_r, a_i):
        return lambda p: (p[0] * a_r - p[1] * a_i, p[0] * a_i + p[1] * a_r)

    lr, li = lr_ref[0], li_ref[0]
    dt = jnp.exp(ld_ref[0])
    mag = jnp.exp(lr * dt)
    ar, ai = mag * jnp.cos(li * dt), mag * jnp.sin(li * dt)
    mul_a = times(ar, ai)
    nrm = lr * lr + li * li
    cbr = ((ar - 1.0) * lr + ai * li) / nrm
    cbi = (ai * lr - (ar - 1.0) * li) / nrm
    b_re, b_im = bre_ref[0], bim_ref[0]
    bbr = cbr * b_re - cbi * b_im
    bbi = cbr * b_im + cbi * b_re
    p = (jnp.ones_like(ar), jnp.zeros_like(ar))
    for s in reversed(range(S5_T)):
        rows = slice(s * LANES, (s + 1) * LANES)
        wg_ref[0, rows, :ts] = (p[0] * bbr - p[1] * bbi).astype(BF16)
        wg_ref[0, rows, ts:] = (p[0] * bbi + p[1] * bbr).astype(BF16)
        for t in range(s):
            wy_ref[0, rows, t * LANES:(t + 1) * LANES] = jnp.zeros((LANES, LANES), BF16)
        p = mul_a(p)
    to_col = lambda row: jnp.transpose(jnp.broadcast_to(row, (LANES, ts)))
    mul_ac = times(to_col(ar), to_col(ai))
    c_re, c_im = cre_ref[0], cim_ref[0]
    bb = jnp.concatenate([bbr, bbi], axis=1).astype(BF16)
    pc = (jnp.ones_like(c_re), jnp.zeros_like(c_re))
    for k in range(S5_T + 1):
        blk = jnp.concatenate([pc[0] * c_re - pc[1] * c_im, -(pc[0] * c_im + pc[1] * c_re)], axis=0).astype(BF16)
        if k >= 1:
            wy_ref[0, tw:, (k - 1) * LANES:k * LANES] = blk
        if k < S5_T:
            lag = _dot(bb, blk).astype(BF16)
            for s in range(S5_T - k):
                t = s + k
                wy_ref[0, s * LANES:(s + 1) * LANES, t * LANES:(t + 1) * LANES] = lag
        pc = mul_ac(pc)


def _s5_prep(a_re, a_im, log_dt, b_re, b_im, c_re, c_im):
    gpt = LANES // SSM_GROUP
    eye = jnp.eye(gpt, dtype=bool)

    def rows(p):
        return p.reshape(S5_TILES, 1, S5_TILE_STATES)

    def bmat(b):
        bt = b.reshape(S5_TILES, gpt, SSM_STATE, SSM_GROUP).transpose(0, 1, 3, 2)
        full = jnp.where(eye[None, :, None, :, None], bt[:, :, :, None, :], 0.0)
        return full.reshape(S5_TILES, LANES, S5_TILE_STATES)

    def cmat(c):
        ct = c.reshape(S5_TILES, gpt, SSM_GROUP, SSM_STATE).transpose(0, 1, 3, 2)
        full = jnp.where(eye[None, :, None, :, None], ct[:, :, :, None, :], 0.0)
        return full.reshape(S5_TILES, S5_TILE_STATES, LANES)

    row_spec = pl.BlockSpec((1, 1, S5_TILE_STATES), lambda j: (j, 0, 0))
    b_spec = pl.BlockSpec((1, LANES, S5_TILE_STATES), lambda j: (j, 0, 0))
    c_spec = pl.BlockSpec((1, S5_TILE_STATES, LANES), lambda j: (j, 0, 0))
    tw = S5_T * LANES
    return pl.pallas_call(
        _s5_prep_body,
        grid=(S5_TILES,),
        in_specs=[row_spec] * 3 + [b_spec] * 2 + [c_spec] * 2,
        out_specs=[pl.BlockSpec((1, tw, 2 * S5_TILE_STATES), lambda j: (j, 0, 0)),
                   pl.BlockSpec((1, tw + 2 * S5_TILE_STATES, tw), lambda j: (j, 0, 0))],
        out_shape=[jax.ShapeDtypeStruct((S5_TILES, tw, 2 * S5_TILE_STATES), BF16),
                   jax.ShapeDtypeStruct((S5_TILES, tw + 2 * S5_TILE_STATES, tw), BF16)],
        compiler_params=_params(1),
        name="s5_prep",
    )(rows(a_re), rows(a_im), rows(log_dt), bmat(b_re), bmat(b_im), cmat(c_re), cmat(c_im))


def _s5_pow_body(ex_ref, lr_ref, li_ref, ld_ref, *o_refs):
    dt = jnp.exp(ld_ref[...])
    zr, zi = lr_ref[...] * dt, li_ref[...] * dt
    lo = 0
    for o_ref in o_refs:
        ex = ex_ref[lo:lo + o_ref.shape[0], :]
        lo += o_ref.shape[0]
        mag = jnp.exp(ex * zr)
        o_ref[:, :S5_STATES] = mag * jnp.cos(ex * zi)
        o_ref[:, S5_STATES:] = mag * jnp.sin(ex * zi)


def _s5_pow_table(groups, a_re, a_im, log_dt):
    sizes = [len(g) for g in groups]
    ex = jnp.asarray(np.asarray(sum(groups, []), np.float32)[:, None])
    flat = lambda p: p.reshape(1, S5_STATES)
    whole = lambda n, w: pl.BlockSpec((n, w), lambda i: (0, 0))
    return pl.pallas_call(
        _s5_pow_body,
        grid=(1,),
        in_specs=[whole(sum(sizes), 1)] + [whole(1, S5_STATES)] * 3,
        out_specs=[whole(n, 2 * S5_STATES) for n in sizes],
        out_shape=[jax.ShapeDtypeStruct((n, 2 * S5_STATES), F32) for n in sizes],
        compiler_params=_params(1),
        name="s5_pow_table",
    )(ex, flat(a_re), flat(a_im), flat(log_dt))


def _s5_block(u_ref, h0_ref, wg_ref, wy_ref, pow2_ref, powr_ref, d_ref, wglu_ref,
              y_ref, hout_ref, carry_ref, us_ref, ys_ref, *, ni, chained, first):
    ns = S5_STATES
    ts = S5_TILE_STATES
    segtok = ni * S5_T
    pitch = _s5_pitch(ni)

    if chained:
        @pl.when(first)
        def _():
            carry_ref[...] = h0_ref[0]
    else:
        carry_ref[...] = h0_ref[0]

    for j in range(S5_TILES):
        for s in range(SUBLANES):
            us_ref[j, s * pitch:s * pitch + segtok] = u_ref[0, s * segtok:(s + 1) * segtok, j * LANES:(j + 1) * LANES]

    def gather(j, t):
        return jnp.concatenate([us_ref[j, pl.ds(S5_T * i + t, SUBLANES, stride=pitch), :]
                                for i in range(ni)], axis=0)

    def fma(p_r, p_i, x_r, x_i, y_r, y_i):
        return y_r + p_r * x_r - p_i * x_i, y_i + p_r * x_i + p_i * x_r

    xs, g_re, g_im = [], [], []
    for j in range(S5_TILES):
        xj = jnp.concatenate([gather(j, s) for s in range(S5_T)], axis=1).astype(BF16)
        g = _dot(xj, wg_ref[j])
        xs.append(xj)
        g_re.append(g[:, :ts])
        g_im.append(g[:, ts:])
    g_re = jnp.concatenate(g_re, axis=1)
    g_im = jnp.concatenate(g_im, axis=1)
    slab = lambda a, i: a[i * SUBLANES:(i + 1) * SUBLANES]
    power = lambda ref, r: (ref[r:r + 1, :ns], ref[r:r + 1, ns:])
    a_r, a_i = power(powr_ref, 0)
    cin = carry_ref[...]
    h_r, h_i = cin[:, :ns], cin[:, ns:]
    prev = []
    if chained:
        local = [(slab(g_re, 0), slab(g_im, 0))]
        for i in range(1, ni):
            local.append(fma(a_r, a_i, *local[-1], slab(g_re, i), slab(g_im, i)))
        sub = lax.broadcasted_iota(jnp.int32, (SUBLANES, 1), 0)
        c_r = jnp.where(sub == 0, h_r, pltpu.roll(local[-1][0], 1, 0))
        c_i = jnp.where(sub == 0, h_i, pltpu.roll(local[-1][1], 1, 0))
        for m in range(3):
            k = 1 << m
            s_r = jnp.where(sub >= k, pltpu.roll(c_r, k, 0), 0.0)
            s_i = jnp.where(sub >= k, pltpu.roll(c_i, k, 0), 0.0)
            c_r, c_i = fma(*power(pow2_ref, m), s_r, s_i, c_r, c_i)
        h_r, h_i = c_r, c_i
        for i in range(ni):
            prev.append((h_r, h_i))
            h_r, h_i = fma(*power(powr_ref, i), c_r, c_i, *local[i])
        last = jnp.concatenate([h_r[SUBLANES - 1:], h_i[SUBLANES - 1:]], axis=1)
    else:
        for i in range(ni):
            prev.append((h_r, h_i))
            h_r, h_i = fma(a_r, a_i, h_r, h_i, slab(g_re, i), slab(g_im, i))
        last = jnp.concatenate([h_r, h_i], axis=1)
    carry_ref[...] = last
    hout_ref[0] = last
    prev_re = jnp.concatenate([p[0] for p in prev], axis=0)
    prev_im = jnp.concatenate([p[1] for p in prev], axis=0)
    y_tiles = []
    for j in range(S5_TILES):
        lhs = jnp.concatenate([xs[j], prev_re[:, j * ts:(j + 1) * ts].astype(BF16),
                               prev_im[:, j * ts:(j + 1) * ts].astype(BF16)], axis=1)
        y_tiles.append(_dot(lhs, wy_ref[j]))
    d = d_ref[...]
    wglu = wglu_ref[0]
    def gelu_in(t):
        ut = jnp.concatenate([gather(j, t) for j in range(S5_TILES)], axis=1)
        yt = jnp.concatenate([y_tiles[j][:, t * LANES:(t + 1) * LANES] for j in range(S5_TILES)], axis=1)
        return jax.nn.gelu(yt + d * ut)

    def scatter(t, o):
        for j in range(S5_TILES):
            for i in range(ni):
                ys_ref[j, pl.ds(S5_T * i + t, SUBLANES, stride=pitch), :] = slab(o, i)[:, j * LANES:(j + 1) * LANES]

    g_q, z_q = {}, {}
    for step in range(S5_T + 2):
        if step < S5_T:
            g_q[step] = gelu_in(step)
        if 0 <= step - 1 < S5_T:
            z_q[step - 1] = _dot(g_q[step - 1].astype(BF16), wglu)
        if 0 <= step - 2 < S5_T:
            scatter(step - 2, g_q.pop(step - 2) * jax.nn.sigmoid(z_q.pop(step - 2)))
    for j in range(S5_TILES):
        for s in range(SUBLANES):
            y_ref[0, s * segtok:(s + 1) * segtok, j * LANES:(j + 1) * LANES] = (
                ys_ref[j, s * pitch:s * pitch + segtok].astype(BF16))


def _s5_pitch(ni):
    segtok = ni * S5_T
    return segtok + SUBLANES if (segtok // SUBLANES) % 2 == 0 else segtok


def _s5_body(ul_ref, ush_ref, h0l_ref, h0s_ref, wg_ref, wy_ref, pow2_ref, powrl_ref, powrs_ref, d_ref, wglu_ref,
             yl_ref, ysh_ref, houtl_ref, houts_ref, carryl_ref, carrys_ref, us_ref, ys_ref,
             *, ni_long, ni_short, nt, nlong):
    s = pl.program_id(0)
    shared = (wg_ref, wy_ref, pow2_ref)

    @pl.when(s < nlong)
    def _():
        _s5_block(ul_ref, h0l_ref, *shared, powrl_ref, d_ref, wglu_ref, yl_ref, houtl_ref, carryl_ref, us_ref, ys_ref,
                  ni=ni_long, chained=True, first=lax.rem(s, nt) == 0)

    @pl.when(s == nlong)
    def _():
        _s5_block(ush_ref, h0s_ref, *shared, powrs_ref, d_ref, wglu_ref, ysh_ref, houts_ref, carrys_ref, us_ref, ys_ref,
                  ni=ni_short, chained=False, first=None)


def _s5_mixer(u_long, u_short, h0_long, h0_short, wg, wy, pow2, powr_long, powr_short, d, w_glu, layer_e,
              ni_long, ni_short):
    nb, ntok, width = u_long.shape
    tb = SUBLANES * ni_long * S5_T
    nt = ntok // tb
    nlong = nb * nt
    assert u_short.shape[1] == SUBLANES * ni_short * S5_T and ni_short <= ni_long
    lb = lambda s: jnp.minimum(s, nlong - 1) // nt
    li = lambda s: jnp.minimum(s, nlong - 1) % nt
    blk_l = pl.BlockSpec((1, tb, width), lambda s: (lb(s), li(s), 0))
    blk_s = pl.BlockSpec(u_short.shape, lambda s: (0, 0, 0))
    st_l = pl.BlockSpec((1, 1, 2 * S5_STATES), lambda s: (lb(s), 0, 0))
    st_s = pl.BlockSpec(h0_short.shape, lambda s: (0, 0, 0))
    return pl.pallas_call(
        functools.partial(_s5_body, ni_long=ni_long, ni_short=ni_short, nt=nt, nlong=nlong),
        grid=(nlong + 1,),
        in_specs=[blk_l, blk_s, st_l, st_s, _const_spec(wg.shape), _const_spec(wy.shape),
                  _const_spec(pow2.shape), _const_spec(powr_long.shape), _const_spec(powr_short.shape),
                  _const_spec((1, SSM_WIDTH)), _const_spec((1, SSM_WIDTH, SSM_WIDTH), (layer_e, 0, 0))],
        out_specs=[blk_l, blk_s, st_l, st_s],
        out_shape=[jax.ShapeDtypeStruct(u_long.shape, BF16), jax.ShapeDtypeStruct(u_short.shape, BF16),
                   jax.ShapeDtypeStruct(h0_long.shape, F32), jax.ShapeDtypeStruct(h0_short.shape, F32)],
        scratch_shapes=[pltpu.VMEM(h0_long.shape[1:], F32), pltpu.VMEM(h0_short.shape[1:], F32)]
        + [pltpu.VMEM((S5_TILES, SUBLANES * _s5_pitch(ni_long), LANES), F32)] * 2,
        compiler_params=_params(1),
        name="s5_mixer",
    )(u_long, u_short, h0_long, h0_short, wg, wy, pow2, powr_long, powr_short, d, w_glu)


def _swa_body(sink_ref, q_ref, kc_ref, vc_ref, kp_ref, vp_ref, pk_ref, pv_ref, o_ref, *, tq, qg, start):
    first = pl.program_id(1) == 0
    k_prev = jnp.where(first, pk_ref[0], kp_ref[0])
    v_prev = jnp.where(first, pv_ref[0], vp_ref[0])
    k_all = jnp.concatenate([k_prev, kc_ref[0]], axis=0)
    v_all = jnp.concatenate([v_prev, vc_ref[0]], axis=0)
    nk = WINDOW + qg
    lane = lax.broadcasted_iota(jnp.int32, (1, KV_WIDTH), 1)
    slot = lax.broadcasted_iota(jnp.int32, (1, nk), 1)
    qrow = lax.broadcasted_iota(jnp.int32, (2 * qg, 1), 0)
    lower = lane < HEAD_DIM
    placed = []
    for kh in range(N_KV_HEADS):
        own = lower if kh == 0 else jnp.logical_not(lower)
        kz, vz = jnp.where(own, k_all, 0.0), jnp.where(own, v_all, 1.0)
        kr, vr = pltpu.roll(kz, HEAD_DIM, 1), pltpu.roll(vz, HEAD_DIM, 1)
        k_lo, k_hi = (kz, kr) if kh == 0 else (kr, kz)
        v_lo, v_hi = (vz, vr) if kh == 0 else (vr, vz)
        placed.append(((k_lo.astype(BF16), v_lo.astype(BF16)), (k_hi.astype(BF16), v_hi.astype(BF16))))
    combos = [(kh, par) for kh in range(N_KV_HEADS) for par in range(2)]

    def scores(c):
        qc = q_ref[0, c * qg:(c + 1) * qg, :]
        out = []
        for kh, par in combos:
            base = kh * GQ * HEAD_DIM
            qs = jnp.concatenate([qc[:, base:base + LANES], qc[:, base + LANES:base + 2 * LANES]], axis=0)
            out.append(_dot_t(qs, placed[kh][par][0][c * qg:c * qg + nk]))
        return out

    def softmax(c, ss):
        n_bad = jnp.where(first, WINDOW - start - c * qg, 0) if c * qg < WINDOW - start else None
        probs, sinks = [], []
        for (kh, par), s in zip(combos, ss):
            if n_bad is not None:
                s = jnp.where(slot >= n_bad, s, NEG)
            sink = jnp.where(qrow < qg, sink_ref[kh * GQ + par], sink_ref[kh * GQ + 2 + par])
            m = jnp.maximum(jnp.max(s, axis=1, keepdims=True), sink)
            probs.append(jnp.exp(s - m).astype(BF16))
            sinks.append(jnp.exp(sink - m))
        return probs, sinks

    def finish(c, probs, sinks):
        pvs = [_dot(p, placed[kh][par][1][c * qg:c * qg + nk]) for (kh, par), p in zip(combos, probs)]
        outs = []
        for kh in range(N_KV_HEADS):
            pv_e, pv_o = pvs[2 * kh], pvs[2 * kh + 1]
            num = jnp.where(lower, pv_e, pv_o)
            den = pltpu.roll(jnp.where(lower, pv_o, pv_e), HEAD_DIM, 1) + jnp.where(lower, sinks[2 * kh], sinks[2 * kh + 1])
            acc = num / den
            outs += [acc[:qg], acc[qg:]]
        o_ref[0, c * qg:(c + 1) * qg, :] = jnp.concatenate(outs, axis=1).astype(BF16)

    nchunk = tq // qg
    per = min(SWA_STAGE_CHUNKS, nchunk)
    groups = [range(g, min(g + per, nchunk)) for g in range(0, nchunk, per)]
    s_q, p_q = {}, {}
    for step in range(len(groups) + 2):
        if step < len(groups):
            s_q[step] = [scores(c) for c in groups[step]]
        if 0 <= step - 1 < len(groups):
            p_q[step - 1] = [softmax(c, ss) for c, ss in zip(groups[step - 1], s_q.pop(step - 1))]
        if 0 <= step - 2 < len(groups):
            for c, (probs, sinks) in zip(groups[step - 2], p_q.pop(step - 2)):
                finish(c, probs, sinks)


def _swa(q, k, v, past_k, past_v, sinks, start, tq, qg):
    bsz, seq, _ = q.shape
    ntiles = seq // tq
    tok = lambda w: pl.BlockSpec((1, tq, w), lambda b, i: (b, i, 0))
    past = pl.BlockSpec((1, WINDOW, KV_WIDTH), lambda b, i: (b, 0, 0))
    if ntiles > 1:
        per = tq // WINDOW
        prev = pl.BlockSpec((1, WINDOW, KV_WIDTH), lambda b, i: (b, jnp.maximum(i * per - 1, 0), 0))
        k_prev, v_prev = k, v
    else:
        prev, k_prev, v_prev = past, past_k, past_v
    return pl.pallas_call(
        functools.partial(_swa_body, tq=tq, qg=qg, start=start),
        grid=(bsz, ntiles),
        in_specs=[pl.BlockSpec(memory_space=pltpu.SMEM), tok(N_HEADS * HEAD_DIM), tok(KV_WIDTH), tok(KV_WIDTH),
                  prev, prev, past, past],
        out_specs=tok(N_HEADS * HEAD_DIM),
        out_shape=jax.ShapeDtypeStruct((bsz, seq, N_HEADS * HEAD_DIM), BF16),
        compiler_params=_params(2),
        name="swa",
    )(sinks, q, k, v, k_prev, v_prev, past_k, past_v)


def _xattn(x1, g, wq, mk_ref, mv_ref, wo, nseq, o_ref):
    h = _rms(x1, g).astype(BF16)
    q = (_dot(h, wq) * X_HEAD_DIM ** -0.5).astype(BF16)
    nparts = nseq if nseq > 1 else XATTN_PARTS
    prows = x1.shape[0] // nparts
    heads = [slice(hd * X_HEAD_DIM, (hd + 1) * X_HEAD_DIM) for hd in range(X_HEADS)]

    def scores(p):
        r, b = slice(p * prows, (p + 1) * prows), (p if nseq > 1 else 0)
        return [_dot_t(q[r, c], mk_ref[0, b, :, c]) for c in heads]

    def softmax(ss):
        out = []
        for s in ss:
            e = jnp.exp(s - jnp.max(s, axis=1, keepdims=True))
            out.append((e.astype(BF16), jnp.sum(e, axis=1, keepdims=True)))
        return out

    def values(p, pd):
        b = p if nseq > 1 else 0
        return jnp.concatenate([_dot(e, mv_ref[0, b, :, c]) / den for c, (e, den) in zip(heads, pd)],
                               axis=1).astype(BF16)

    def finish(p, pd):
        r = slice(p * prows, (p + 1) * prows)
        o_ref[0, r, :] = x1[r] + _dot(values(p, pd), wo)

    if nseq > 1:
        pds = [softmax(ss) for ss in [scores(p) for p in range(nparts)]]
        o = jnp.concatenate([values(p, pd) for p, pd in enumerate(pds)], axis=0)
        o_ref[0] = x1 + _dot(o, wo)
        return
    s_q, p_q = {}, {}
    for step in range(nparts + 2):
        if step < nparts:
            s_q[step] = scores(step)
        if 0 <= step - 1 < nparts:
            p_q[step - 1] = softmax(s_q.pop(step - 1))
        if 0 <= step - 2 < nparts:
            finish(step - 2, p_q.pop(step - 2))


def _conv3(cur, carry, w):
    n = cur.shape[0]
    row = lax.broadcasted_iota(jnp.int32, (n, 1), 0)
    nc = carry.shape[0]
    c1, c2 = carry[nc - 1:nc], carry[nc - 2:nc - 1]
    m1 = jnp.where(row == 0, c1, pltpu.roll(cur, 1, 0))
    m2 = jnp.where(row == 0, c2, jnp.where(row == 1, c1, pltpu.roll(cur, 2, 0)))
    return w[0:1] * m2 + w[1:2] * m1 + w[2:3] * cur


def _conv3_seqs(cur, carries, w):
    nseq = len(carries)
    if nseq == 1:
        return _conv3(cur, carries[0], w)
    rows = cur.shape[0] // nseq
    body = _conv3(cur, carries[0], w)
    pieces = []
    for b in range(nseq):
        pieces.append(_conv3(cur[b * rows:b * rows + SUBLANES], carries[b], w))
        pieces.append(body[b * rows + SUBLANES:(b + 1) * rows])
    return jnp.concatenate(pieces, axis=0)


def _store_last_rows(st_ref, cols, cur, nseq):
    rows = cur.shape[0] // nseq
    for b in range(nseq):
        st_ref[b, :, cols] = cur[(b + 1) * rows - (CONV_WIDTH - 1):(b + 1) * rows]


def _mix_even_tile(x_ref, ya_ref, yb_ref, wout_ref, g_ref, wq_ref, mk_ref, mv_ref, wo_ref, o_ref, *, nseq):
    x1 = x_ref[0] + _dot(jnp.concatenate([ya_ref[0], yb_ref[0]], axis=1), wout_ref[0])
    _xattn(x1, g_ref[0], wq_ref[0], mk_ref, mv_ref, wo_ref[0], nseq, o_ref)


def _mix_even_xattn_body(xl_ref, xs_ref, yal_ref, yas_ref, ybl_ref, ybs_ref, wout_ref, g_ref, wq_ref,
                         mkl_ref, mks_ref, mvl_ref, mvs_ref, wo_ref, ol_ref, os_ref, *, nlong, nseq_s):
    s = pl.program_id(0)

    @pl.when(s < nlong)
    def _():
        _mix_even_tile(xl_ref, yal_ref, ybl_ref, wout_ref, g_ref, wq_ref, mkl_ref, mvl_ref, wo_ref, ol_ref, nseq=1)

    @pl.when(s == nlong)
    def _():
        _mix_even_tile(xs_ref, yas_ref, ybs_ref, wout_ref, g_ref, wq_ref, mks_ref, mvs_ref, wo_ref, os_ref, nseq=nseq_s)


def _long_short_specs(x_long, x_short, tm):
    nb, seq, _ = x_long.shape
    nt = seq // tm
    nlong = nb * nt
    lb = lambda s: jnp.minimum(s, nlong - 1) // nt
    li = lambda s: jnp.minimum(s, nlong - 1) % nt
    tok_long = lambda w: pl.BlockSpec((1, tm, w), lambda s: (lb(s), li(s), 0))
    tok_short = lambda w: pl.BlockSpec((1, x_short.shape[1], w), lambda s: (0, 0, 0))
    return nlong, nt, lb, tok_long, tok_short


def _mix_even_xattn(x_long, x_short, ya_long, ya_short, yb_long, yb_short, w_out, g, wq, mem_long, mem_short, wo,
                    layer, layer_e, tm):
    nlong, _, lb, tok_l, tok_s = _long_short_specs(x_long, x_short, tm)
    nseq_s = mem_short[0].shape[1]
    mem_l = pl.BlockSpec((1, 1, N_MEM, D_MODEL), lambda s: (layer, lb(s), 0, 0))
    mem_s = _const_spec((1, nseq_s, N_MEM, D_MODEL), (layer, 0, 0, 0))
    half = D_MODEL // 2
    sq = lambda l: _const_spec((1, D_MODEL, D_MODEL), (l, 0, 0))
    return pl.pallas_call(
        functools.partial(_mix_even_xattn_body, nlong=nlong, nseq_s=nseq_s),
        grid=(nlong + 1,),
        in_specs=[tok_l(D_MODEL), tok_s(D_MODEL), tok_l(half), tok_s(half), tok_l(half), tok_s(half),
                  sq(layer_e), _gain_spec(layer), sq(layer), mem_l, mem_s, mem_l, mem_s, sq(layer)],
        out_specs=[tok_l(D_MODEL), tok_s(D_MODEL)],
        out_shape=[jax.ShapeDtypeStruct(x_long.shape, F32), jax.ShapeDtypeStruct(x_short.shape, F32)],
        compiler_params=_params(1),
        name="mix_even_xattn",
    )(x_long, x_short, ya_long, ya_short, yb_long, yb_short, w_out, g, wq,
      mem_long[0], mem_short[0], mem_long[1], mem_short[1], wo)


def _mix_odd_tile(x_ref, gm_ref, win_ref, cw_ref, wout_ref, prev_ref, g_ref, wq_ref, mk_ref, mv_ref, wo_ref,
                  o_ref, st_ref, carry_ref, *, tm, nseq, first):
    keep = CONV_WIDTH - 1
    if nseq == 1:
        @pl.when(first)
        def _():
            carry_ref[SUBLANES - keep:, :] = prev_ref[0, 0]

    x = x_ref[0]
    proj = _dot(_rms(x, gm_ref[0]).astype(BF16), win_ref[0])
    gate_b, gate_c, z = proj[:, :D_MODEL], proj[:, D_MODEL:2 * D_MODEL], proj[:, 2 * D_MODEL:]
    cz = gate_c * z
    carries = [carry_ref[SUBLANES - keep:, :]] if nseq == 1 else [prev_ref[0, b] for b in range(nseq)]
    z_conv = _conv3_seqs(cz, carries, cw_ref[0])
    if nseq == 1:
        carry_ref[...] = cz[tm - SUBLANES:]
    _store_last_rows(st_ref, slice(None), cz, nseq)
    x1 = x + _dot((gate_b * z_conv).astype(BF16), wout_ref[0])
    _xattn(x1, g_ref[0], wq_ref[0], mk_ref, mv_ref, wo_ref[0], nseq, o_ref)


def _mix_odd_xattn_body(*refs, tm, nseq):
    _mix_odd_tile(*refs, tm=tm, nseq=nseq, first=pl.program_id(1) == 0)


def _mix_odd_xattn(x, gm, w_in, conv_w, w_out, prev, g, wq, mem, wo, layer, layer_o, tm, nseq):
    bsz, seq, _ = x.shape
    keep = CONV_WIDTH - 1
    tok = pl.BlockSpec((1, tm, D_MODEL), lambda b, i: (b, i, 0))
    mem_spec = pl.BlockSpec((1, nseq, N_MEM, D_MODEL), lambda b, i: (layer, b, 0, 0))
    hist = pl.BlockSpec((1, nseq, keep, D_MODEL), lambda b, i: (layer_o, b, 0, 0))
    st = pl.BlockSpec((nseq, keep, D_MODEL), lambda b, i: (b, 0, 0))
    sq = lambda l: _const_spec((1, D_MODEL, D_MODEL), (l, 0, 0))
    return pl.pallas_call(
        functools.partial(_mix_odd_xattn_body, tm=tm, nseq=nseq),
        grid=(bsz, seq // tm),
        in_specs=[tok, _gain_spec(layer), _const_spec((1, D_MODEL, 3 * D_MODEL), (layer_o, 0, 0)),
                  _const_spec((1, CONV_WIDTH, D_MODEL), (layer_o, 0, 0)), sq(layer_o), hist,
                  _gain_spec(layer), sq(layer), mem_spec, mem_spec, sq(layer)],
        out_specs=[tok, st],
        out_shape=[jax.ShapeDtypeStruct(x.shape, F32), jax.ShapeDtypeStruct((bsz * nseq, keep, D_MODEL), F32)],
        scratch_shapes=[pltpu.VMEM((SUBLANES, D_MODEL), F32)],
        compiler_params=_params(2),
        name="mix_odd_xattn",
    )(x, gm, w_in, conv_w, w_out, prev, g, wq, mem[0], mem[1], wo)


def _ffn_tile(x_ref, g_ref, wup_ref, cw_ref, wdn_ref, prev_ref, gfin_ref, o_ref, st_ref, carry_ref,
              *, tm, final, nseq, first):
    if nseq == 1:
        @pl.when(first)
        def _():
            carry_ref[SUBLANES - (CONV_WIDTH - 1):, :] = prev_ref[0, 0]

    x = x_ref[0]
    h = _rms(x, g_ref[0]).astype(BF16)
    chunks = [slice(lo, min(lo + FF_CHUNK, D_FF)) for lo in range(0, D_FF, FF_CHUNK)]

    def up(cols):
        return _dot(h, wup_ref[:, cols]), _dot(h, wup_ref[:, D_FF + cols.start:D_FF + cols.stop])

    acc = x
    nxt = up(chunks[0])
    for n, cols in enumerate(chunks):
        (gate, val), nxt = nxt, (up(chunks[n + 1]) if n + 1 < len(chunks) else None)
        carries = ([carry_ref[SUBLANES - (CONV_WIDTH - 1):, cols]] if nseq == 1
                   else [prev_ref[0, b, :, cols] for b in range(nseq)])
        gate_c = _conv3_seqs(gate, carries, cw_ref[0, :, cols])
        if nseq == 1:
            carry_ref[:, cols] = gate[tm - SUBLANES:]
        _store_last_rows(st_ref, cols, gate, nseq)
        act = (gate_c * jax.nn.sigmoid(gate_c) * val).astype(BF16)
        if n + 1 < len(chunks):
            acc = acc + _dot(act, wdn_ref[cols, :])
        else:
            half = tm // 2
            for rows in (slice(0, half), slice(half, tm)):
                out = acc[rows] + _dot(act[rows], wdn_ref[cols, :])
                o_ref[0, rows, :] = _rms(out, gfin_ref[0]) if final else out


def _ffn_stage_weights(wup_hbm, wdn_hbm, wup_ref, wdn_ref, upbuf, dnbuf, sem, layer):
    ucols, drows = upbuf.shape[2], dnbuf.shape[1]
    pieces = ([(wup_hbm.at[layer, :, c:c + ucols], upbuf, 0, wup_ref, (slice(None), slice(c, c + ucols)))
               for c in range(0, 2 * D_FF, ucols)]
              + [(wdn_hbm.at[layer, r:r + drows, :], dnbuf, 1, wdn_ref, (slice(r, r + drows), slice(None)))
                 for r in range(0, D_FF, drows)])
    copies = [pltpu.make_async_copy(src, buf.at[n % 2], sem.at[which, n % 2])
              for n, (src, buf, which, _, _) in enumerate(pieces)]
    copies[0].start()
    for n, (_, buf, _, dst, where) in enumerate(pieces):
        if n + 1 < len(pieces):
            copies[n + 1].start()
        copies[n].wait()
        dst[where] = buf[n % 2].astype(BF16)


def _ffn_body(xl_ref, xs_ref, g_ref, wup_hbm, cw_ref, wdn_hbm, prevl_ref, prevs_ref, gfin_ref,
              ol_ref, os_ref, stl_ref, sts_ref, carry_ref, wup_ref, wdn_ref, upbuf, dnbuf, sem,
              *, tm, nt, nlong, rows_s, nseq_s, final, layer):
    s = pl.program_id(0)

    @pl.when(s == 0)
    def _():
        _ffn_stage_weights(wup_hbm, wdn_hbm, wup_ref, wdn_ref, upbuf, dnbuf, sem, layer)

    shared = (g_ref, wup_ref, cw_ref, wdn_ref)

    @pl.when(s < nlong)
    def _():
        _ffn_tile(xl_ref, *shared, prevl_ref, gfin_ref, ol_ref, stl_ref, carry_ref,
                  tm=tm, final=final, nseq=1, first=lax.rem(s, nt) == 0)

    @pl.when(s == nlong)
    def _():
        _ffn_tile(xs_ref, *shared, prevs_ref, gfin_ref, os_ref, sts_ref, carry_ref,
                  tm=rows_s, final=final, nseq=nseq_s, first=None)


def _ffn(x_long, x_short, g, w_up, conv_w, w_down, prev_long, prev_short, g_final, layer, tm, final):
    nb, seq, _ = x_long.shape
    rows_s = x_short.shape[1]
    nseq_s = prev_short.shape[1]
    nt = seq // tm
    nlong = nb * nt
    lb = lambda s: jnp.minimum(s, nlong - 1) // nt
    li = lambda s: jnp.minimum(s, nlong - 1) % nt
    keep = CONV_WIDTH - 1
    tok_l = pl.BlockSpec((1, tm, D_MODEL), lambda s: (lb(s), li(s), 0))
    tok_s = pl.BlockSpec((1, rows_s, D_MODEL), lambda s: (0, 0, 0))
    hist_l = pl.BlockSpec((1, 1, keep, D_FF), lambda s: (layer, lb(s), 0, 0))
    hist_s = pl.BlockSpec((1, nseq_s, keep, D_FF), lambda s: (layer, 0, 0, 0))
    st_l = pl.BlockSpec((1, keep, D_FF), lambda s: (lb(s), 0, 0))
    st_s = pl.BlockSpec((nseq_s, keep, D_FF), lambda s: (0, 0, 0))
    return pl.pallas_call(
        functools.partial(_ffn_body, tm=tm, nt=nt, nlong=nlong, rows_s=rows_s, nseq_s=nseq_s, final=final,
                          layer=layer),
        grid=(nlong + 1,),
        in_specs=[tok_l, tok_s, _gain_spec(layer), pl.BlockSpec(memory_space=pl.ANY),
                  _const_spec((1, CONV_WIDTH, D_FF), (layer, 0, 0)), pl.BlockSpec(memory_space=pl.ANY),
                  hist_l, hist_s, _gain_spec(0)],
        out_specs=[tok_l, tok_s, st_l, st_s],
        out_shape=[jax.ShapeDtypeStruct(x_long.shape, F32), jax.ShapeDtypeStruct(x_short.shape, F32),
                   jax.ShapeDtypeStruct((nb, keep, D_FF), F32), jax.ShapeDtypeStruct((nseq_s, keep, D_FF), F32)],
        scratch_shapes=[pltpu.VMEM((SUBLANES, D_FF), F32),
                        pltpu.VMEM((D_MODEL, 2 * D_FF), BF16), pltpu.VMEM((D_FF, D_MODEL), BF16),
                        pltpu.VMEM((2, D_MODEL, 2 * MXU_DIM), F32), pltpu.VMEM((2, MXU_DIM, D_MODEL), F32),
                        pltpu.SemaphoreType.DMA((2, 2))],
        compiler_params=_params(1),
        name="conv_ffn",
    )(x_long, x_short, g, w_up, conv_w, w_down, prev_long, prev_short, g_final)


def _mem_flat_copies(k_hbm, v_hbm, kbuf, vbuf, sem, step, slot, nbatch):
    lyr, b = step // nbatch, step % nbatch
    return [pltpu.make_async_copy(src.at[lyr, b, :, hd, :], buf.at[slot, hd], sem.at[slot, i, hd])
            for i, (src, buf) in enumerate(((k_hbm, kbuf), (v_hbm, vbuf))) for hd in range(X_HEADS)]


def _mem_flat_body(k_hbm, v_hbm, ko_ref, vo_ref, kbuf, vbuf, sem, *, nbatch):
    s, n = pl.program_id(0), pl.num_programs(0)
    copies = functools.partial(_mem_flat_copies, k_hbm, v_hbm, kbuf, vbuf, sem, nbatch=nbatch)

    @pl.when(s == 0)
    def _():
        for c in copies(step=s, slot=0):
            c.start()

    @pl.when(s + 1 < n)
    def _():
        for c in copies(step=s + 1, slot=(s + 1) % 2):
            c.start()

    slot = s % 2
    for c in copies(step=s, slot=slot):
        c.wait()
    for hd in range(X_HEADS):
        cols = slice(hd * X_HEAD_DIM, (hd + 1) * X_HEAD_DIM)
        ko_ref[0, 0, :, cols] = kbuf[slot, hd].astype(BF16)
        vo_ref[0, 0, :, cols] = vbuf[slot, hd].astype(BF16)


def _mem_flat(mem_k, mem_v):
    depth, nbatch = mem_k.shape[:2]
    flat = pl.BlockSpec((1, 1, N_MEM, D_MODEL), lambda s: (s // nbatch, s % nbatch, 0, 0))
    return pl.pallas_call(
        functools.partial(_mem_flat_body, nbatch=nbatch),
        grid=(depth * nbatch,),
        in_specs=[pl.BlockSpec(memory_space=pl.ANY)] * 2,
        out_specs=[flat, flat],
        out_shape=[jax.ShapeDtypeStruct((depth, nbatch, N_MEM, D_MODEL), BF16)] * 2,
        scratch_shapes=[pltpu.VMEM((2, X_HEADS, N_MEM, X_HEAD_DIM), F32)] * 2
        + [pltpu.SemaphoreType.DMA((2, 2, X_HEADS))],
        compiler_params=_params(1),
        name="mem_flat",
    )(mem_k, mem_v)


def _mem_kv_body(m_ref, w_ref, k_ref, v_ref, kf_ref, vf_ref, wb_ref):
    @pl.when(pl.program_id(1) == 0)
    def _():
        wb_ref[...] = w_ref[0].astype(BF16)

    kv = _dot(m_ref[0].astype(BF16), wb_ref[...])
    kf_ref[0, 0] = kv[:, :D_MODEL].astype(BF16)
    vf_ref[0, 0] = kv[:, D_MODEL:].astype(BF16)
    for hd in range(X_HEADS):
        k_ref[0, 0, :, hd, :] = kv[:, hd * X_HEAD_DIM:(hd + 1) * X_HEAD_DIM]
        v_ref[0, 0, :, hd, :] = kv[:, D_MODEL + hd * X_HEAD_DIM:D_MODEL + (hd + 1) * X_HEAD_DIM]


def _mem_kv(mem, w_kv):
    bsz = mem.shape[0]
    heads = pl.BlockSpec((1, 1, N_MEM, X_HEADS, X_HEAD_DIM), lambda l, b: (l, b, 0, 0, 0))
    flat = pl.BlockSpec((1, 1, N_MEM, D_MODEL), lambda l, b: (l, b, 0, 0))
    return pl.pallas_call(
        _mem_kv_body,
        grid=(DEPTH, bsz),
        in_specs=[pl.BlockSpec((1, N_MEM, D_MODEL), lambda l, b: (b, 0, 0)),
                  pl.BlockSpec((1, D_MODEL, 2 * D_MODEL), lambda l, b: (l, 0, 0))],
        out_specs=[heads, heads, flat, flat],
        out_shape=[jax.ShapeDtypeStruct((DEPTH, bsz, N_MEM, X_HEADS, X_HEAD_DIM), F32)] * 2
        + [jax.ShapeDtypeStruct((DEPTH, bsz, N_MEM, D_MODEL), BF16)] * 2,
        scratch_shapes=[pltpu.VMEM((D_MODEL, 2 * D_MODEL), BF16)],
        compiler_params=_params(2),
        name="mem_kv",
    )(mem, w_kv)


def _project_in(x, start, w, cfg):
    bsz, seq, _ = x.shape
    nseq = cfg["nseq"]
    xg = x.reshape(bsz // nseq, nseq * seq, D_MODEL)
    return (xg,) + tuple(_inproj_even(xg, w["norm_mix"], w["w_in_even"], start, seq, 0, cfg["tm_in"]))


def _window_attention(q, k, v, past_k, past_v, bsz, start, w, cfg):
    seq = q.shape[0] * q.shape[1] // bsz
    per_seq = lambda t: t.reshape(bsz, seq, t.shape[-1])
    k, v = per_seq(k), per_seq(v)
    yb = _swa(per_seq(q), k, v, past_k.reshape(bsz, WINDOW, KV_WIDTH), past_v.reshape(bsz, WINDOW, KV_WIDTH),
              w["attn_sinks"], start, cfg["tq"], cfg["qg"])
    keep = min(seq, WINDOW)
    new_k = k[:, seq - keep:].reshape(1, bsz, keep, N_KV_HEADS, HEAD_DIM)
    new_v = v[:, seq - keep:].reshape(1, bsz, keep, N_KV_HEADS, HEAD_DIM)
    return yb.reshape(q.shape), new_k, new_v


def kernel(x_prompt, x_sample, mem_prompt, cache_win_k, cache_win_v, state_ssm_re, state_ssm_im, state_conv_mix, state_conv_ffn, cache_mem_k, cache_mem_v, norm_mix, norm_xattn, norm_ffn, norm_final, w_in_even, w_out_even, ssm_a_re, ssm_a_im, ssm_log_dt, ssm_b_re, ssm_b_im, ssm_c_re, ssm_c_im, ssm_d, w_glu, attn_sinks, w_in_odd, conv_mix_w, w_out_odd, xattn_wq, xattn_wkv, xattn_wo, ffn_w_up, ffn_conv_w, ffn_w_down):
    bp, seq_p, _ = x_prompt.shape
    bs, seq_s, _ = x_sample.shape
    gains = lambda g: g.reshape(-1, 1, D_MODEL)
    w = dict(norm_mix=gains(norm_mix), norm_xattn=gains(norm_xattn), norm_ffn=gains(norm_ffn),
             norm_final=gains(norm_final),
             w_in_even=w_in_even, w_out_even=w_out_even,
             ssm_d=ssm_d.reshape(1, SSM_WIDTH), w_glu=w_glu.astype(BF16), attn_sinks=attn_sinks.reshape(N_HEADS),
             w_in_odd=w_in_odd, conv_mix_w=conv_mix_w, w_out_odd=w_out_odd,
             xattn_wq=xattn_wq, xattn_wo=xattn_wo,
             ffn_w_up=ffn_w_up, ffn_conv_w=ffn_conv_w, ffn_w_down=ffn_w_down)

    gp = (SSM_GROUPS, SSM_STATE)
    a_re, a_im, log_dt = ssm_a_re.reshape(gp), ssm_a_im.reshape(gp), ssm_log_dt.reshape(gp)
    wg, wy = _s5_prep(a_re, a_im, log_dt, ssm_b_re.reshape(gp + (SSM_GROUP,)), ssm_b_im.reshape(gp + (SSM_GROUP,)),
                      ssm_c_re.reshape(SSM_GROUPS, SSM_GROUP, SSM_STATE), ssm_c_im.reshape(SSM_GROUPS, SSM_GROUP, SSM_STATE))
    ni_p, ni_s = S5_SEG_CHUNKS, seq_s // S5_T
    padded = lambda e: e + [1.0] * (-len(e) % SUBLANES)
    pow2, powr_p, powr_s = _s5_pow_table(
        [padded([float(S5_T * ni_p << m) for m in range(3)]),
         padded([float(S5_T * (i + 1)) for i in range(ni_p)]),
         padded([float(S5_T * (i + 1)) for i in range(ni_s)])], a_re, a_im, log_dt)
    cfg_p = dict(tm_in=1024, tm_even=1024, tm_odd=1024, tm_ffn=1024, tq=512, qg=CHUNK, nseq=1)
    cfg_s = dict(tm_in=bs * seq_s, tq=seq_s, qg=seq_s, nseq=bs)

    mem_k_p, mem_v_p, mem_kf_p, mem_vf_p = _mem_kv(mem_prompt, xattn_wkv)
    cw = CONV_WIDTH - 1
    zk = jnp.zeros((1, bp, WINDOW, N_KV_HEADS, HEAD_DIM), F32)
    zs = jnp.zeros((1, bp, SSM_GROUPS, SSM_STATE), F32)
    zcm = jnp.zeros((1, bp, cw, D_MODEL), F32)
    zcf = jnp.zeros((DEPTH, bp, cw, D_FF), F32)
    mem_kf_s, mem_vf_s = _mem_flat(cache_mem_k, cache_mem_v)
    mem_p, mem_s = (mem_kf_p, mem_vf_p), (mem_kf_s, mem_vf_s)
    xp, u_p, q_p, kk_p, vv_p = _project_in(x_prompt, 0, w, cfg_p)
    xs, u_s, q_s, kk_s, vv_s = _project_in(x_sample, PAST_LEN, w, cfg_s)
    state = lambda re, im, b: jnp.concatenate([re.reshape(b, S5_STATES), im.reshape(b, S5_STATES)], axis=1)
    ya_p, ya_s, h_p, h_s = _s5_mixer(u_p, u_s, state(zs, zs, bp)[:, None, :], state(state_ssm_re, state_ssm_im, bs)[None],
                                     wg, wy, pow2, powr_p, powr_s, w["ssm_d"], w["w_glu"], 0, ni_p, ni_s)
    split = lambda h, b: (h[:, :S5_STATES].reshape(1, b, SSM_GROUPS, SSM_STATE),
                          h[:, S5_STATES:].reshape(1, b, SSM_GROUPS, SSM_STATE))
    (re_p, im_p), (re_s, im_s) = split(h_p[:, 0], bp), split(h_s[0], bs)
    yb_p, k_p, v_p = _window_attention(q_p, kk_p, vv_p, zk, zk, bp, 0, w, cfg_p)
    yb_s, k_s, v_s = _window_attention(q_s, kk_s, vv_s, cache_win_k, cache_win_v, bs, PAST_LEN, w, cfg_s)
    xp, xs = _mix_even_xattn(xp, xs, ya_p, ya_s, yb_p, yb_s, w["w_out_even"], w["norm_xattn"], w["xattn_wq"],
                             mem_p, mem_s, w["xattn_wo"], 0, 0, cfg_p["tm_even"])
    ffn = functools.partial(_ffn, g=w["norm_ffn"], w_up=w["ffn_w_up"], conv_w=w["ffn_conv_w"], w_down=w["ffn_w_down"],
                            prev_long=zcf, prev_short=state_conv_ffn, g_final=w["norm_final"], tm=cfg_p["tm_ffn"])
    xp, xs, cf0_p, cf0_s = ffn(xp, xs, layer=0, final=False)
    mix_odd = functools.partial(_mix_odd_xattn, gm=w["norm_mix"], w_in=w["w_in_odd"], conv_w=w["conv_mix_w"],
                                w_out=w["w_out_odd"], g=w["norm_xattn"], wq=w["xattn_wq"], wo=w["xattn_wo"],
                                layer=1, layer_o=0)
    xp, cm_p = mix_odd(xp, prev=zcm, mem=mem_p, tm=cfg_p["tm_odd"], nseq=1)
    xs, cm_s = mix_odd(xs, prev=state_conv_mix, mem=mem_s, tm=xs.shape[1], nseq=bs)
    y_p, y_s, cf1_p, cf1_s = ffn(xp, xs, layer=1, final=True)
    return (y_p, y_s.reshape(x_sample.shape), k_p, v_p, re_p, im_p, cm_p[None], jnp.stack([cf0_p, cf1_p]),
            mem_k_p, mem_v_p, k_s, v_s, re_s, im_s, cm_s[None], jnp.stack([cf0_s, cf1_s]))
```

```python
import functools
import math

import numpy as np
import jax
import jax.numpy as jnp
from jax import lax
from jax.experimental import pallas as pl
from jax.experimental.pallas import tpu as pltpu

F32 = jnp.float32
BF16 = jnp.bfloat16

D_MODEL = 1024
DEPTH = 2
PAST_LEN = 2048
CHUNK = 64
SSM_WIDTH = 512
SSM_GROUP = 16
SSM_GROUPS = 32
SSM_STATE = 64
HEAD_DIM = 64
N_HEADS = 8
N_KV_HEADS = 2
GQ = N_HEADS // N_KV_HEADS
WINDOW = 128
ROPE_THETA = 10000.0
KV_WIDTH = N_KV_HEADS * HEAD_DIM
EVEN_IN = SSM_WIDTH + N_HEADS * HEAD_DIM + 2 * KV_WIDTH
CONV_WIDTH = 3
N_MEM = 256
X_HEADS = 4
X_HEAD_DIM = 256
D_FF = 2816
EPS = 1e-6
NEG = -1e30

LANES = 128
SUBLANES = 8
S5_T = 8
S5_TILES = SSM_WIDTH // LANES
S5_TILE_STATES = (LANES // SSM_GROUP) * SSM_STATE
S5_STATES = SSM_GROUPS * SSM_STATE
SWA_STAGE_CHUNKS = 1
XATTN_PARTS = 4
S5_SEG_CHUNKS = 16
MXU_DIM = 256
FF_CHUNK = 6 * MXU_DIM
VMEM_LIMIT = 56 * 1024 * 1024


def _const_spec(shape, index=None):
    idx = tuple(index) if index is not None else (0,) * len(shape)
    return pl.BlockSpec(shape, lambda *_: idx, pipeline_mode=pl.Buffered(1))


def _gain_spec(layer):
    return _const_spec((1, 1, D_MODEL), (layer, 0, 0))


def _params(n_axes):
    return pltpu.CompilerParams(dimension_semantics=("arbitrary",) * n_axes,
                                vmem_limit_bytes=VMEM_LIMIT)


def _rms(x, g):
    return x * lax.rsqrt(jnp.mean(x * x, axis=-1, keepdims=True) + EPS) * g


def _dot(a, b):
    return jnp.dot(a, b.astype(BF16), preferred_element_type=F32)


def _dot_t(a, b):
    return lax.dot_general(a, b, (((1,), (1,)), ((), ())), preferred_element_type=F32)


def _rope(t, cos, sina, sinb):
    n = t.shape[1]
    half = HEAD_DIM // 2
    return t * cos + pltpu.roll(t, n - half, 1) * sina + pltpu.roll(t, half, 1) * sinb


def _inproj_even_body(x_ref, g_ref, w_ref, u_ref, q_ref, k_ref, v_ref, cphi_ref, sphi_ref, *, start, period):
    o0 = SSM_WIDTH
    o1 = o0 + N_HEADS * HEAD_DIM
    o2 = o1 + KV_WIDTH
    tm = x_ref.shape[1]
    half = HEAD_DIM // 2
    lane = lax.broadcasted_iota(jnp.int32, (1, LANES), 1)
    inv = jnp.exp((lane & (half - 1)).astype(F32) * (-2.0 / HEAD_DIM * math.log(ROPE_THETA)))
    first = (lane & (HEAD_DIM - 1)) < half

    @pl.when((pl.program_id(0) == 0) & (pl.program_id(1) == 0))
    def _():
        row = lax.broadcasted_iota(jnp.int32, (tm, LANES), 0)
        phi = (row & (period - 1)).astype(F32) * inv
        cphi_ref[...] = jnp.cos(phi)
        sphi_ref[...] = jnp.sin(phi)

    theta = (start + ((pl.program_id(1) * tm) & (period - 1))).astype(F32) * inv
    c_th, s_th = jnp.cos(theta), jnp.sin(theta)
    nparts = 2 if tm % (2 * LANES) == 0 else 1
    parts = [slice(p * tm // nparts, (p + 1) * tm // nparts) for p in range(nparts)]
    projs = [_dot(_rms(x_ref[0, r, :], g_ref[0]).astype(BF16), w_ref[0]) for r in parts]
    for r, proj in zip(parts, projs):
        c_ph, s_ph = cphi_ref[r, :], sphi_ref[r, :]
        cos = c_th * c_ph - s_th * s_ph
        sin = s_th * c_ph + c_th * s_ph
        sa = jnp.where(first, -sin, 0.0)
        sb = jnp.where(first, 0.0, sin)
        u_ref[0, r, :] = proj[:, :o0]
        q = [_rope(proj[:, c:c + LANES], cos, sa, sb) for c in range(o0, o1, LANES)]
        q_ref[0, r, :] = (jnp.concatenate(q, axis=1) * HEAD_DIM ** -0.5).astype(BF16)
        k_ref[0, r, :] = _rope(proj[:, o1:o2], cos, sa, sb)
        v_ref[0, r, :] = proj[:, o2:]


def _inproj_even(x, g, w_in, start, period, layer_e, tm):
    bsz, seq, _ = x.shape
    tok = lambda w: pl.BlockSpec((1, tm, w), lambda b, i: (b, i, 0))
    return pl.pallas_call(
        functools.partial(_inproj_even_body, start=start, period=period),
        grid=(bsz, seq // tm),
        in_specs=[tok(D_MODEL), _gain_spec(2 * layer_e),
                  _const_spec((1, D_MODEL, EVEN_IN), (layer_e, 0, 0))],
        out_specs=[tok(SSM_WIDTH), tok(N_HEADS * HEAD_DIM), tok(KV_WIDTH), tok(KV_WIDTH)],
        out_shape=[jax.ShapeDtypeStruct((bsz, seq, SSM_WIDTH), F32),
                   jax.ShapeDtypeStruct((bsz, seq, N_HEADS * HEAD_DIM), BF16),
                   jax.ShapeDtypeStruct((bsz, seq, KV_WIDTH), F32),
                   jax.ShapeDtypeStruct((bsz, seq, KV_WIDTH), F32)],
        scratch_shapes=[pltpu.VMEM((tm, LANES), F32)] * 2,
        compiler_params=_params(2),
        name="inproj_even",
    )(x, g, w_in)


def _s5_prep_body(lr_ref, li_ref, ld_ref, bre_ref, bim_ref, cre_ref, cim_ref, wg_ref, wy_ref):
    ts = S5_TILE_STATES
    tw = S5_T * LANES

    def times(a_r, a_i):
        return lambda p: (p[0] * a_r - p[1] * a_i, p[0] * a_i + p[1] * a_r)

    lr, li = lr_ref[0], li_ref[0]
    dt = jnp.exp(ld_ref[0])
    mag = jnp.exp(lr * dt)
    ar, ai = mag * jnp.cos(li * dt), mag * jnp.sin(li * dt)
    mul_a = times(ar, ai)
    nrm = lr * lr + li * li
    cbr = ((ar - 1.0) * lr + ai * li) / nrm
    cbi = (ai * lr - (ar - 1.0) * li) / nrm
    b_re, b_im = bre_ref[0], bim_ref[0]
    bbr = cbr * b_re - cbi * b_im
    bbi = cbr * b_im + cbi * b_re
    p = (jnp.ones_like(ar), jnp.zeros_like(ar))
    for s in reversed(range(S5_T)):
        rows = slice(s * LANES, (s + 1) * LANES)
        wg_ref[0, rows, :ts] = (p[0] * bbr - p[1] * bbi).astype(BF16)
        wg_ref[0, rows, ts:] = (p[0] * bbi + p[1] * bbr).astype(BF16)
        for t in range(s):
            wy_ref[0, rows, t * LANES:(t + 1) * LANES] = jnp.zeros((LANES, LANES), BF16)
        p = mul_a(p)
    to_col = lambda row: jnp.transpose(jnp.broadcast_to(row, (LANES, ts)))
    mul_ac = times(to_col(ar), to_col(ai))
    c_re, c_im = cre_ref[0], cim_ref[0]
    bb = jnp.concatenate([bbr, bbi], axis=1).astype(BF16)
    pc = (jnp.ones_like(c_re), jnp.zeros_like(c_re))
    for k in range(S5_T + 1):
        blk = jnp.concatenate([pc[0] * c_re - pc[1] * c_im, -(pc[0] * c_im + pc[1] * c_re)], axis=0).astype(BF16)
        if k >= 1:
            wy_ref[0, tw:, (k - 1) * LANES:k * LANES] = blk
        if k < S5_T:
            lag = _dot(bb, blk).astype(BF16)
            for s in range(S5_T - k):
                t = s + k
                wy_ref[0, s * LANES:(s + 1) * LANES, t * LANES:(t + 1) * LANES] = lag
        pc = mul_ac(pc)


def _s5_prep(a_re, a_im, log_dt, b_re, b_im, c_re, c_im):
    gpt = LANES // SSM_GROUP
    eye = jnp.eye(gpt, dtype=bool)

    def rows(p):
        return p.reshape(S5_TILES, 1, S5_TILE_STATES)

    def bmat(b):
        bt = b.reshape(S5_TILES, gpt, SSM_STATE, SSM_GROUP).transpose(0, 1, 3, 2)
        full = jnp.where(eye[None, :, None, :, None], bt[:, :, :, None, :], 0.0)
        return full.reshape(S5_TILES, LANES, S5_TILE_STATES)

    def cmat(c):
        ct = c.reshape(S5_TILES, gpt, SSM_GROUP, SSM_STATE).transpose(0, 1, 3, 2)
        full = jnp.where(eye[None, :, None, :, None], ct[:, :, :, None, :], 0.0)
        return full.reshape(S5_TILES, S5_TILE_STATES, LANES)

    row_spec = pl.BlockSpec((1, 1, S5_TILE_STATES), lambda j: (j, 0, 0))
    b_spec = pl.BlockSpec((1, LANES, S5_TILE_STATES), lambda j: (j, 0, 0))
    c_spec = pl.BlockSpec((1, S5_TILE_STATES, LANES), lambda j: (j, 0, 0))
    tw = S5_T * LANES
    return pl.pallas_call(
        _s5_prep_body,
        grid=(S5_TILES,),
        in_specs=[row_spec] * 3 + [b_spec] * 2 + [c_spec] * 2,
        out_specs=[pl.BlockSpec((1, tw, 2 * S5_TILE_STATES), lambda j: (j, 0, 0)),
                   pl.BlockSpec((1, tw + 2 * S5_TILE_STATES, tw), lambda j: (j, 0, 0))],
        out_shape=[jax.ShapeDtypeStruct((S5_TILES, tw, 2 * S5_TILE_STATES), BF16),
                   jax.ShapeDtypeStruct((S5_TILES, tw + 2 * S5_TILE_STATES, tw), BF16)],
        compiler_params=_params(1),
        name="s5_prep",
    )(rows(a_re), rows(a_im), rows(log_dt), bmat(b_re), bmat(b_im), cmat(c_re), cmat(c_im))


def _s5_pow_body(ex_ref, lr_ref, li_ref, ld_ref, *o_refs):
    dt = jnp.exp(ld_ref[...])
    zr, zi = lr_ref[...] * dt, li_ref[...] * dt
    lo = 0
    for o_ref in o_refs:
        ex = ex_ref[lo:lo + o_ref.shape[0], :]
        lo += o_ref.shape[0]
        mag = jnp.exp(ex * zr)
        o_ref[:, :S5_STATES] = mag * jnp.cos(ex * zi)
        o_ref[:, S5_STATES:] = mag * jnp.sin(ex * zi)


def _s5_pow_table(groups, a_re, a_im, log_dt):
    sizes = [len(g) for g in groups]
    ex = jnp.asarray(np.asarray(sum(groups, []), np.float32)[:, None])
    flat = lambda p: p.reshape(1, S5_STATES)
    whole = lambda n, w: pl.BlockSpec((n, w), lambda i: (0, 0))
    return pl.pallas_call(
        _s5_pow_body,
        grid=(1,),
        in_specs=[whole(sum(sizes), 1)] + [whole(1, S5_STATES)] * 3,
        out_specs=[whole(n, 2 * S5_STATES) for n in sizes],
        out_shape=[jax.ShapeDtypeStruct((n, 2 * S5_STATES), F32) for n in sizes],
        compiler_params=_params(1),
        name="s5_pow_table",
    )(ex, flat(a_re), flat(a_im), flat(log_dt))


def _s5_block(u_ref, h0_ref, wg_ref, wy_ref, pow2_ref, powr_ref, d_ref, wglu_ref,
              y_ref, hout_ref, carry_ref, us_ref, ys_ref, *, ni, chained, first):
    ns = S5_STATES
    ts = S5_TILE_STATES
    segtok = ni * S5_T
    pitch = _s5_pitch(ni)

    if chained:
        @pl.when(first)
        def _():
            carry_ref[...] = h0_ref[0]
    else:
        carry_ref[...] = h0_ref[0]

    for j in range(S5_TILES):
        for s in range(SUBLANES):
            us_ref[j, s * pitch:s * pitch + segtok] = u_ref[0, s * segtok:(s + 1) * segtok, j * LANES:(j + 1) * LANES]

    def gather(j, t):
        return jnp.concatenate([us_ref[j, pl.ds(S5_T * i + t, SUBLANES, stride=pitch), :]
                                for i in range(ni)], axis=0)

    def fma(p_r, p_i, x_r, x_i, y_r, y_i):
        return y_r + p_r * x_r - p_i * x_i, y_i + p_r * x_i + p_i * x_r

    xs, g_re, g_im = [], [], []
    for j in range(S5_TILES):
        xj = jnp.concatenate([gather(j, s) for s in range(S5_T)], axis=1).astype(BF16)
        g = _dot(xj, wg_ref[j])
        xs.append(xj)
        g_re.append(g[:, :ts])
        g_im.append(g[:, ts:])
    g_re = jnp.concatenate(g_re, axis=1)
    g_im = jnp.concatenate(g_im, axis=1)
    slab = lambda a, i: a[i * SUBLANES:(i + 1) * SUBLANES]
    power = lambda ref, r: (ref[r:r + 1, :ns], ref[r:r + 1, ns:])
    a_r, a_i = power(powr_ref, 0)
    cin = carry_ref[...]
    h_r, h_i = cin[:, :ns], cin[:, ns:]
    prev = []
    if chained:
        local = [(slab(g_re, 0), slab(g_im, 0))]
        for i in range(1, ni):
            local.append(fma(a_r, a_i, *local[-1], slab(g_re, i), slab(g_im, i)))
        sub = lax.broadcasted_iota(jnp.int32, (SUBLANES, 1), 0)
        c_r = jnp.where(sub == 0, h_r, pltpu.roll(local[-1][0], 1, 0))
        c_i = jnp.where(sub == 0, h_i, pltpu.roll(local[-1][1], 1, 0))
        for m in range(3):
            k = 1 << m
            s_r = jnp.where(sub >= k, pltpu.roll(c_r, k, 0), 0.0)
            s_i = jnp.where(sub >= k, pltpu.roll(c_i, k, 0), 0.0)
            c_r, c_i = fma(*power(pow2_ref, m), s_r, s_i, c_r, c_i)
        h_r, h_i = c_r, c_i
        for i in range(ni):
            prev.append((h_r, h_i))
            h_r, h_i = fma(*power(powr_ref, i), c_r, c_i, *local[i])
        last = jnp.concatenate([h_r[SUBLANES - 1:], h_i[SUBLANES - 1:]], axis=1)
    else:
        for i in range(ni):
            prev.append((h_r, h_i))
            h_r, h_i = fma(a_r, a_i, h_r, h_i, slab(g_re, i), slab(g_im, i))
        last = jnp.concatenate([h_r, h_i], axis=1)
    carry_ref[...] = last
    hout_ref[0] = last
    prev_re = jnp.concatenate([p[0] for p in prev], axis=0)
    prev_im = jnp.concatenate([p[1] for p in prev], axis=0)
    y_tiles = []
    for j in range(S5_TILES):
        lhs = jnp.concatenate([xs[j], prev_re[:, j * ts:(j + 1) * ts].astype(BF16),
                               prev_im[:, j * ts:(j + 1) * ts].astype(BF16)], axis=1)
        y_tiles.append(_dot(lhs, wy_ref[j]))
    d = d_ref[...]
    wglu = wglu_ref[0]
    def gelu_in(t):
        ut = jnp.concatenate([gather(j, t) for j in range(S5_TILES)], axis=1)
        yt = jnp.concatenate([y_tiles[j][:, t * LANES:(t + 1) * LANES] for j in range(S5_TILES)], axis=1)
        return jax.nn.gelu(yt + d * ut)

    def scatter(t, o):
        for j in range(S5_TILES):
            for i in range(ni):
                ys_ref[j, pl.ds(S5_T * i + t, SUBLANES, stride=pitch), :] = slab(o, i)[:, j * LANES:(j + 1) * LANES]

    g_q, z_q = {}, {}
    for step in range(S5_T + 2):
        if step < S5_T:
            g_q[step] = gelu_in(step)
        if 0 <= step - 1 < S5_T:
            z_q[step - 1] = _dot(g_q[step - 1].astype(BF16), wglu)
        if 0 <= step - 2 < S5_T:
            scatter(step - 2, g_q.pop(step - 2) * jax.nn.sigmoid(z_q.pop(step - 2)))
    for j in range(S5_TILES):
        for s in range(SUBLANES):
            y_ref[0, s * segtok:(s + 1) * segtok, j * LANES:(j + 1) * LANES] = (
                ys_ref[j, s * pitch:s * pitch + segtok].astype(BF16))


def _s5_pitch(ni):
    segtok = ni * S5_T
    return segtok + SUBLANES if (segtok // SUBLANES) % 2 == 0 else segtok


def _s5_body(ul_ref, ush_ref, h0l_ref, h0s_ref, wg_ref, wy_ref, pow2_ref, powrl_ref, powrs_ref, d_ref, wglu_ref,
             yl_ref, ysh_ref, houtl_ref, houts_ref, carryl_ref, carrys_ref, us_ref, ys_ref,
             *, ni_long, ni_short, nt, nlong):
    s = pl.program_id(0)
    shared = (wg_ref, wy_ref, pow2_ref)

    @pl.when(s < nlong)
    def _():
        _s5_block(ul_ref, h0l_ref, *shared, powrl_ref, d_ref, wglu_ref, yl_ref, houtl_ref, carryl_ref, us_ref, ys_ref,
                  ni=ni_long, chained=True, first=lax.rem(s, nt) == 0)

    @pl.when(s == nlong)
    def _():
        _s5_block(ush_ref, h0s_ref, *shared, powrs_ref, d_ref, wglu_ref, ysh_ref, houts_ref, carrys_ref, us_ref, ys_ref,
                  ni=ni_short, chained=False, first=None)


def _s5_mixer(u_long, u_short, h0_long, h0_short, wg, wy, pow2, powr_long, powr_short, d, w_glu, layer_e,
              ni_long, ni_short):
    nb, ntok, width = u_long.shape
    tb = SUBLANES * ni_long * S5_T
    nt = ntok // tb
    nlong = nb * nt
    assert u_short.shape[1] == SUBLANES * ni_short * S5_T and ni_short <= ni_long
    lb = lambda s: jnp.minimum(s, nlong - 1) // nt
    li = lambda s: jnp.minimum(s, nlong - 1) % nt
    blk_l = pl.BlockSpec((1, tb, width), lambda s: (lb(s), li(s), 0))
    blk_s = pl.BlockSpec(u_short.shape, lambda s: (0, 0, 0))
    st_l = pl.BlockSpec((1, 1, 2 * S5_STATES), lambda s: (lb(s), 0, 0))
    st_s = pl.BlockSpec(h0_short.shape, lambda s: (0, 0, 0))
    return pl.pallas_call(
        functools.partial(_s5_body, ni_long=ni_long, ni_short=ni_short, nt=nt, nlong=nlong),
        grid=(nlong + 1,),
        in_specs=[blk_l, blk_s, st_l, st_s, _const_spec(wg.shape), _const_spec(wy.shape),
                  _const_spec(pow2.shape), _const_spec(powr_long.shape), _const_spec(powr_short.shape),
                  _const_spec((1, SSM_WIDTH)), _const_spec((1, SSM_WIDTH, SSM_WIDTH), (layer_e, 0, 0))],
        out_specs=[blk_l, blk_s, st_l, st_s],
        out_shape=[jax.ShapeDtypeStruct(u_long.shape, BF16), jax.ShapeDtypeStruct(u_short.shape, BF16),
                   jax.ShapeDtypeStruct(h0_long.shape, F32), jax.ShapeDtypeStruct(h0_short.shape, F32)],
        scratch_shapes=[pltpu.VMEM(h0_long.shape[1:], F32), pltpu.VMEM(h0_short.shape[1:], F32)]
        + [pltpu.VMEM((S5_TILES, SUBLANES * _s5_pitch(ni_long), LANES), F32)] * 2,
        compiler_params=_params(1),
        name="s5_mixer",
    )(u_long, u_short, h0_long, h0_short, wg, wy, pow2, powr_long, powr_short, d, w_glu)


def _swa_body(sink_ref, q_ref, kc_ref, vc_ref, kp_ref, vp_ref, pk_ref, pv_ref, o_ref, *, tq, qg, start):
    first = pl.program_id(1) == 0
    k_prev = jnp.where(first, pk_ref[0], kp_ref[0])
    v_prev = jnp.where(first, pv_ref[0], vp_ref[0])
    k_all = jnp.concatenate([k_prev, kc_ref[0]], axis=0)
    v_all = jnp.concatenate([v_prev, vc_ref[0]], axis=0)
    nk = WINDOW + qg
    lane = lax.broadcasted_iota(jnp.int32, (1, KV_WIDTH), 1)
    slot = lax.broadcasted_iota(jnp.int32, (1, nk), 1)
    qrow = lax.broadcasted_iota(jnp.int32, (2 * qg, 1), 0)
    lower = lane < HEAD_DIM
    placed = []
    for kh in range(N_KV_HEADS):
        own = lower if kh == 0 else jnp.logical_not(lower)
        kz, vz = jnp.where(own, k_all, 0.0), jnp.where(own, v_all, 1.0)
        kr, vr = pltpu.roll(kz, HEAD_DIM, 1), pltpu.roll(vz, HEAD_DIM, 1)
        k_lo, k_hi = (kz, kr) if kh == 0 else (kr, kz)
        v_lo, v_hi = (vz, vr) if kh == 0 else (vr, vz)
        placed.append(((k_lo.astype(BF16), v_lo.astype(BF16)), (k_hi.astype(BF16), v_hi.astype(BF16))))
    combos = [(kh, par) for kh in range(N_KV_HEADS) for par in range(2)]

    def scores(c):
        qc = q_ref[0, c * qg:(c + 1) * qg, :]
        out = []
        for kh, par in combos:
            base = kh * GQ * HEAD_DIM
            qs = jnp.concatenate([qc[:, base:base + LANES], qc[:, base + LANES:base + 2 * LANES]], axis=0)
            out.append(_dot_t(qs, placed[kh][par][0][c * qg:c * qg + nk]))
        return out

    def softmax(c, ss):
        n_bad = jnp.where(first, WINDOW - start - c * qg, 0) if c * qg < WINDOW - start else None
        probs, sinks = [], []
        for (kh, par), s in zip(combos, ss):
            if n_bad is not None:
                s = jnp.where(slot >= n_bad, s, NEG)
            sink = jnp.where(qrow < qg, sink_ref[kh * GQ + par], sink_ref[kh * GQ + 2 + par])
            m = jnp.maximum(jnp.max(s, axis=1, keepdims=True), sink)
            probs.append(jnp.exp(s - m).astype(BF16))
            sinks.append(jnp.exp(sink - m))
        return probs, sinks

    def finish(c, probs, sinks):
        pvs = [_dot(p, placed[kh][par][1][c * qg:c * qg + nk]) for (kh, par), p in zip(combos, probs)]
        outs = []
        for kh in range(N_KV_HEADS):
            pv_e, pv_o = pvs[2 * kh], pvs[2 * kh + 1]
            num = jnp.where(lower, pv_e, pv_o)
            den = pltpu.roll(jnp.where(lower, pv_o, pv_e), HEAD_DIM, 1) + jnp.where(lower, sinks[2 * kh], sinks[2 * kh + 1])
            acc = num / den
            outs += [acc[:qg], acc[qg:]]
        o_ref[0, c * qg:(c + 1) * qg, :] = jnp.concatenate(outs, axis=1).astype(BF16)

    nchunk = tq // qg
    per = min(SWA_STAGE_CHUNKS, nchunk)
    groups = [range(g, min(g + per, nchunk)) for g in range(0, nchunk, per)]
    s_q, p_q = {}, {}
    for step in range(len(groups) + 2):
        if step < len(groups):
            s_q[step] = [scores(c) for c in groups[step]]
        if 0 <= step - 1 < len(groups):
            p_q[step - 1] = [softmax(c, ss) for c, ss in zip(groups[step - 1], s_q.pop(step - 1))]
        if 0 <= step - 2 < len(groups):
            for c, (probs, sinks) in zip(groups[step - 2], p_q.pop(step - 2)):
                finish(c, probs, sinks)


def _swa(q, k, v, past_k, past_v, sinks, start, tq, qg):
    bsz, seq, _ = q.shape
    ntiles = seq // tq
    tok = lambda w: pl.BlockSpec((1, tq, w), lambda b, i: (b, i, 0))
    past = pl.BlockSpec((1, WINDOW, KV_WIDTH), lambda b, i: (b, 0, 0))
    if ntiles > 1:
        per = tq // WINDOW
        prev = pl.BlockSpec((1, WINDOW, KV_WIDTH), lambda b, i: (b, jnp.maximum(i * per - 1, 0), 0))
        k_prev, v_prev = k, v
    else:
        prev, k_prev, v_prev = past, past_k, past_v
    return pl.pallas_call(
        functools.partial(_swa_body, tq=tq, qg=qg, start=start),
        grid=(bsz, ntiles),
        in_specs=[pl.BlockSpec(memory_space=pltpu.SMEM), tok(N_HEADS * HEAD_DIM), tok(KV_WIDTH), tok(KV_WIDTH),
                  prev, prev, past, past],
        out_specs=tok(N_HEADS * HEAD_DIM),
        out_shape=jax.ShapeDtypeStruct((bsz, seq, N_HEADS * HEAD_DIM), BF16),
        compiler_params=_params(2),
        name="swa",
    )(sinks, q, k, v, k_prev, v_prev, past_k, past_v)


def _xattn(x1, g, wq, mk_ref, mv_ref, wo, nseq, o_ref):
    h = _rms(x1, g).astype(BF16)
    q = (_dot(h, wq) * X_HEAD_DIM ** -0.5).astype(BF16)
    nparts = nseq if nseq > 1 else XATTN_PARTS
    prows = x1.shape[0] // nparts
    heads = [slice(hd * X_HEAD_DIM, (hd + 1) * X_HEAD_DIM) for hd in range(X_HEADS)]

    def scores(p):
        r, b = slice(p * prows, (p + 1) * prows), (p if nseq > 1 else 0)
        return [_dot_t(q[r, c], mk_ref[0, b, :, c]) for c in heads]

    def softmax(ss):
        out = []
        for s in ss:
            e = jnp.exp(s - jnp.max(s, axis=1, keepdims=True))
            out.append((e.astype(BF16), jnp.sum(e, axis=1, keepdims=True)))
        return out

    def values(p, pd):
        b = p if nseq > 1 else 0
        return jnp.concatenate([_dot(e, mv_ref[0, b, :, c]) / den for c, (e, den) in zip(heads, pd)],
                               axis=1).astype(BF16)

    def finish(p, pd):
        r = slice(p * prows, (p + 1) * prows)
        o_ref[0, r, :] = x1[r] + _dot(values(p, pd), wo)

    if nseq > 1:
        pds = [softmax(ss) for ss in [scores(p) for p in range(nparts)]]
        o = jnp.concatenate([values(p, pd) for p, pd in enumerate(pds)], axis=0)
        o_ref[0] = x1 + _dot(o, wo)
        return
    s_q, p_q = {}, {}
    for step in range(nparts + 2):
        if step < nparts:
            s_q[step] = scores(step)
        if 0 <= step - 1 < nparts:
            p_q[step - 1] = softmax(s_q.pop(step - 1))
        if 0 <= step - 2 < nparts:
            finish(step - 2, p_q.pop(step - 2))


def _conv3(cur, carry, w):
    n = cur.shape[0]
    row = lax.broadcasted_iota(jnp.int32, (n, 1), 0)
    nc = carry.shape[0]
    c1, c2 = carry[nc - 1:nc], carry[nc - 2:nc - 1]
    m1 = jnp.where(row == 0, c1, pltpu.roll(cur, 1, 0))
    m2 = jnp.where(row == 0, c2, jnp.where(row == 1, c1, pltpu.roll(cur, 2, 0)))
    return w[0:1] * m2 + w[1:2] * m1 + w[2:3] * cur


def _conv3_seqs(cur, carries, w):
    nseq = len(carries)
    if nseq == 1:
        return _conv3(cur, carries[0], w)
    rows = cur.shape[0] // nseq
    body = _conv3(cur, carries[0], w)
    pieces = []
    for b in range(nseq):
        pieces.append(_conv3(cur[b * rows:b * rows + SUBLANES], carries[b], w))
        pieces.append(body[b * rows + SUBLANES:(b + 1) * rows])
    return jnp.concatenate(pieces, axis=0)


def _store_last_rows(st_ref, cols, cur, nseq):
    rows = cur.shape[0] // nseq
    for b in range(nseq):
        st_ref[b, :, cols] = cur[(b + 1) * rows - (CONV_WIDTH - 1):(b + 1) * rows]


def _mix_even_tile(x_ref, ya_ref, yb_ref, wout_ref, g_ref, wq_ref, mk_ref, mv_ref, wo_ref, o_ref, *, nseq):
    x1 = x_ref[0] + _dot(jnp.concatenate([ya_ref[0], yb_ref[0]], axis=1), wout_ref[0])
    _xattn(x1, g_ref[0], wq_ref[0], mk_ref, mv_ref, wo_ref[0], nseq, o_ref)


def _mix_even_xattn_body(xl_ref, xs_ref, yal_ref, yas_ref, ybl_ref, ybs_ref, wout_ref, g_ref, wq_ref,
                         mkl_ref, mks_ref, mvl_ref, mvs_ref, wo_ref, ol_ref, os_ref, *, nlong, nseq_s):
    s = pl.program_id(0)

    @pl.when(s < nlong)
    def _():
        _mix_even_tile(xl_ref, yal_ref, ybl_ref, wout_ref, g_ref, wq_ref, mkl_ref, mvl_ref, wo_ref, ol_ref, nseq=1)

    @pl.when(s == nlong)
    def _():
        _mix_even_tile(xs_ref, yas_ref, ybs_ref, wout_ref, g_ref, wq_ref, mks_ref, mvs_ref, wo_ref, os_ref, nseq=nseq_s)


def _long_short_specs(x_long, x_short, tm):
    nb, seq, _ = x_long.shape
    nt = seq // tm
    nlong = nb * nt
    lb = lambda s: jnp.minimum(s, nlong - 1) // nt
    li = lambda s: jnp.minimum(s, nlong - 1) % nt
    tok_long = lambda w: pl.BlockSpec((1, tm, w), lambda s: (lb(s), li(s), 0))
    tok_short = lambda w: pl.BlockSpec((1, x_short.shape[1], w), lambda s: (0, 0, 0))
    return nlong, nt, lb, tok_long, tok_short


def _mix_even_xattn(x_long, x_short, ya_long, ya_short, yb_long, yb_short, w_out, g, wq, mem_long, mem_short, wo,
                    layer, layer_e, tm):
    nlong, _, lb, tok_l, tok_s = _long_short_specs(x_long, x_short, tm)
    nseq_s = mem_short[0].shape[1]
    mem_l = pl.BlockSpec((1, 1, N_MEM, D_MODEL), lambda s: (layer, lb(s), 0, 0))
    mem_s = _const_spec((1, nseq_s, N_MEM, D_MODEL), (layer, 0, 0, 0))
    half = D_MODEL // 2
    sq = lambda l: _const_spec((1, D_MODEL, D_MODEL), (l, 0, 0))
    return pl.pallas_call(
        functools.partial(_mix_even_xattn_body, nlong=nlong, nseq_s=nseq_s),
        grid=(nlong + 1,),
        in_specs=[tok_l(D_MODEL), tok_s(D_MODEL), tok_l(half), tok_s(half), tok_l(half), tok_s(half),
                  sq(layer_e), _gain_spec(layer), sq(layer), mem_l, mem_s, mem_l, mem_s, sq(layer)],
        out_specs=[tok_l(D_MODEL), tok_s(D_MODEL)],
        out_shape=[jax.ShapeDtypeStruct(x_long.shape, F32), jax.ShapeDtypeStruct(x_short.shape, F32)],
        compiler_params=_params(1),
        name="mix_even_xattn",
    )(x_long, x_short, ya_long, ya_short, yb_long, yb_short, w_out, g, wq,
      mem_long[0], mem_short[0], mem_long[1], mem_short[1], wo)


def _mix_odd_tile(x_ref, gm_ref, win_ref, cw_ref, wout_ref, prev_ref, g_ref, wq_ref, mk_ref, mv_ref, wo_ref,
                  o_ref, st_ref, carry_ref, *, tm, nseq, first):
    keep = CONV_WIDTH - 1
    if nseq == 1:
        @pl.when(first)
        def _():
            carry_ref[SUBLANES - keep:, :] = prev_ref[0, 0]

    x = x_ref[0]
    proj = _dot(_rms(x, gm_ref[0]).astype(BF16), win_ref[0])
    gate_b, gate_c, z = proj[:, :D_MODEL], proj[:, D_MODEL:2 * D_MODEL], proj[:, 2 * D_MODEL:]
    cz = gate_c * z
    carries = [carry_ref[SUBLANES - keep:, :]] if nseq == 1 else [prev_ref[0, b] for b in range(nseq)]
    z_conv = _conv3_seqs(cz, carries, cw_ref[0])
    if nseq == 1:
        carry_ref[...] = cz[tm - SUBLANES:]
    _store_last_rows(st_ref, slice(None), cz, nseq)
    x1 = x + _dot((gate_b * z_conv).astype(BF16), wout_ref[0])
    _xattn(x1, g_ref[0], wq_ref[0], mk_ref, mv_ref, wo_ref[0], nseq, o_ref)


def _mix_odd_xattn_body(*refs, tm, nseq):
    _mix_odd_tile(*refs, tm=tm, nseq=nseq, first=pl.program_id(1) == 0)


def _mix_odd_xattn(x, gm, w_in, conv_w, w_out, prev, g, wq, mem, wo, layer, layer_o, tm, nseq):
    bsz, seq, _ = x.shape
    keep = CONV_WIDTH - 1
    tok = pl.BlockSpec((1, tm, D_MODEL), lambda b, i: (b, i, 0))
    mem_spec = pl.BlockSpec((1, nseq, N_MEM, D_MODEL), lambda b, i: (layer, b, 0, 0))
    hist = pl.BlockSpec((1, nseq, keep, D_MODEL), lambda b, i: (layer_o, b, 0, 0))
    st = pl.BlockSpec((nseq, keep, D_MODEL), lambda b, i: (b, 0, 0))
    sq = lambda l: _const_spec((1, D_MODEL, D_MODEL), (l, 0, 0))
    return pl.pallas_call(
        functools.partial(_mix_odd_xattn_body, tm=tm, nseq=nseq),
        grid=(bsz, seq // tm),
        in_specs=[tok, _gain_spec(layer), _const_spec((1, D_MODEL, 3 * D_MODEL), (layer_o, 0, 0)),
                  _const_spec((1, CONV_WIDTH, D_MODEL), (layer_o, 0, 0)), sq(layer_o), hist,
                  _gain_spec(layer), sq(layer), mem_spec, mem_spec, sq(layer)],
        out_specs=[tok, st],
        out_shape=[jax.ShapeDtypeStruct(x.shape, F32), jax.ShapeDtypeStruct((bsz * nseq, keep, D_MODEL), F32)],
        scratch_shapes=[pltpu.VMEM((SUBLANES, D_MODEL), F32)],
        compiler_params=_params(2),
        name="mix_odd_xattn",
    )(x, gm, w_in, conv_w, w_out, prev, g, wq, mem[0], mem[1], wo)


def _ffn_tile(x_ref, g_ref, wup_ref, cw_ref, wdn_ref, prev_ref, gfin_ref, o_ref, st_ref, carry_ref,
              *, tm, final, nseq, first):
    if nseq == 1:
        @pl.when(first)
        def _():
            carry_ref[SUBLANES - (CONV_WIDTH - 1):, :] = prev_ref[0, 0]

    x = x_ref[0]
    h = _rms(x, g_ref[0]).astype(BF16)
    chunks = [slice(lo, min(lo + FF_CHUNK, D_FF)) for lo in range(0, D_FF, FF_CHUNK)]

    def up(cols):
        return _dot(h, wup_ref[:, cols]), _dot(h, wup_ref[:, D_FF + cols.start:D_FF + cols.stop])

    acc = x
    nxt = up(chunks[0])
    for n, cols in enumerate(chunks):
        (gate, val), nxt = nxt, (up(chunks[n + 1]) if n + 1 < len(chunks) else None)
        carries = ([carry_ref[SUBLANES - (CONV_WIDTH - 1):, cols]] if nseq == 1
                   else [prev_ref[0, b, :, cols] for b in range(nseq)])
        gate_c = _conv3_seqs(gate, carries, cw_ref[0, :, cols])
        if nseq == 1:
            carry_ref[:, cols] = gate[tm - SUBLANES:]
        _store_last_rows(st_ref, cols, gate, nseq)
        act = (gate_c * jax.nn.sigmoid(gate_c) * val).astype(BF16)
        if n + 1 < len(chunks):
            acc = acc + _dot(act, wdn_ref[cols, :])
        else:
            half = tm // 2
            for rows in (slice(0, half), slice(half, tm)):
                out = acc[rows] + _dot(act[rows], wdn_ref[cols, :])
                o_ref[0, rows, :] = _rms(out, gfin_ref[0]) if final else out


def _ffn_stage_weights(wup_hbm, wdn_hbm, wup_ref, wdn_ref, upbuf, dnbuf, sem, layer):
    ucols, drows = upbuf.shape[2], dnbuf.shape[1]
    pieces = ([(wup_hbm.at[layer, :, c:c + ucols], upbuf, 0, wup_ref, (slice(None), slice(c, c + ucols)))
               for c in range(0, 2 * D_FF, ucols)]
              + [(wdn_hbm.at[layer, r:r + drows, :], dnbuf, 1, wdn_ref, (slice(r, r + drows), slice(None)))
                 for r in range(0, D_FF, drows)])
    copies = [pltpu.make_async_copy(src, buf.at[n % 2], sem.at[which, n % 2])
              for n, (src, buf, which, _, _) in enumerate(pieces)]
    copies[0].start()
    for n, (_, buf, _, dst, where) in enumerate(pieces):
        if n + 1 < len(pieces):
            copies[n + 1].start()
        copies[n].wait()
        dst[where] = buf[n % 2].astype(BF16)


def _ffn_body(xl_ref, xs_ref, g_ref, wup_hbm, cw_ref, wdn_hbm, prevl_ref, prevs_ref, gfin_ref,
              ol_ref, os_ref, stl_ref, sts_ref, carry_ref, wup_ref, wdn_ref, upbuf, dnbuf, sem,
              *, tm, nt, nlong, rows_s, nseq_s, final, layer):
    s = pl.program_id(0)

    @pl.when(s == 0)
    def _():
        _ffn_stage_weights(wup_hbm, wdn_hbm, wup_ref, wdn_ref, upbuf, dnbuf, sem, layer)

    shared = (g_ref, wup_ref, cw_ref, wdn_ref)

    @pl.when(s < nlong)
    def _():
        _ffn_tile(xl_ref, *shared, prevl_ref, gfin_ref, ol_ref, stl_ref, carry_ref,
                  tm=tm, final=final, nseq=1, first=lax.rem(s, nt) == 0)

    @pl.when(s == nlong)
    def _():
        _ffn_tile(xs_ref, *shared, prevs_ref, gfin_ref, os_ref, sts_ref, carry_ref,
                  tm=rows_s, final=final, nseq=nseq_s, first=None)


def _ffn(x_long, x_short, g, w_up, conv_w, w_down, prev_long, prev_short, g_final, layer, tm, final):
    nb, seq, _ = x_long.shape
    rows_s = x_short.shape[1]
    nseq_s = prev_short.shape[1]
    nt = seq // tm
    nlong = nb * nt
    lb = lambda s: jnp.minimum(s, nlong - 1) // nt
    li = lambda s: jnp.minimum(s, nlong - 1) % nt
    keep = CONV_WIDTH - 1
    tok_l = pl.BlockSpec((1, tm, D_MODEL), lambda s: (lb(s), li(s), 0))
    tok_s = pl.BlockSpec((1, rows_s, D_MODEL), lambda s: (0, 0, 0))
    hist_l = pl.BlockSpec((1, 1, keep, D_FF), lambda s: (layer, lb(s), 0, 0))
    hist_s = pl.BlockSpec((1, nseq_s, keep, D_FF), lambda s: (layer, 0, 0, 0))
    st_l = pl.BlockSpec((1, keep, D_FF), lambda s: (lb(s), 0, 0))
    st_s = pl.BlockSpec((nseq_s, keep, D_FF), lambda s: (0, 0, 0))
    return pl.pallas_call(
        functools.partial(_ffn_body, tm=tm, nt=nt, nlong=nlong, rows_s=rows_s, nseq_s=nseq_s, final=final,
                          layer=layer),
        grid=(nlong + 1,),
        in_specs=[tok_l, tok_s, _gain_spec(layer), pl.BlockSpec(memory_space=pl.ANY),
                  _const_spec((1, CONV_WIDTH, D_FF), (layer, 0, 0)), pl.BlockSpec(memory_space=pl.ANY),
                  hist_l, hist_s, _gain_spec(0)],
        out_specs=[tok_l, tok_s, st_l, st_s],
        out_shape=[jax.ShapeDtypeStruct(x_long.shape, F32), jax.ShapeDtypeStruct(x_short.shape, F32),
                   jax.ShapeDtypeStruct((nb, keep, D_FF), F32), jax.ShapeDtypeStruct((nseq_s, keep, D_FF), F32)],
        scratch_shapes=[pltpu.VMEM((SUBLANES, D_FF), F32),
                        pltpu.VMEM((D_MODEL, 2 * D_FF), BF16), pltpu.VMEM((D_FF, D_MODEL), BF16),
                        pltpu.VMEM((2, D_MODEL, 2 * MXU_DIM), F32), pltpu.VMEM((2, MXU_DIM, D_MODEL), F32),
                        pltpu.SemaphoreType.DMA((2, 2))],
        compiler_params=_params(1),
        name="conv_ffn",
    )(x_long, x_short, g, w_up, conv_w, w_down, prev_long, prev_short, g_final)


def _mem_flat_copies(k_hbm, v_hbm, kbuf, vbuf, sem, step, slot, nbatch):
    lyr, b = step // nbatch, step % nbatch
    return [pltpu.make_async_copy(src.at[lyr, b, :, hd, :], buf.at[slot, hd], sem.at[slot, i, hd])
            for i, (src, buf) in enumerate(((k_hbm, kbuf), (v_hbm, vbuf))) for hd in range(X_HEADS)]


def _mem_flat_body(k_hbm, v_hbm, ko_ref, vo_ref, kbuf, vbuf, sem, *, nbatch):
    s, n = pl.program_id(0), pl.num_programs(0)
    copies = functools.partial(_mem_flat_copies, k_hbm, v_hbm, kbuf, vbuf, sem, nbatch=nbatch)

    @pl.when(s == 0)
    def _():
        for c in copies(step=s, slot=0):
            c.start()

    @pl.when(s + 1 < n)
    def _():
        for c in copies(step=s + 1, slot=(s + 1) % 2):
            c.start()

    slot = s % 2
    for c in copies(step=s, slot=slot):
        c.wait()
    for hd in range(X_HEADS):
        cols = slice(hd * X_HEAD_DIM, (hd + 1) * X_HEAD_DIM)
        ko_ref[0, 0, :, cols] = kbuf[slot, hd].astype(BF16)
        vo_ref[0, 0, :, cols] = vbuf[slot, hd].astype(BF16)


def _mem_flat(mem_k, mem_v):
    depth, nbatch = mem_k.shape[:2]
    flat = pl.BlockSpec((1, 1, N_MEM, D_MODEL), lambda s: (s // nbatch, s % nbatch, 0, 0))
    return pl.pallas_call(
        functools.partial(_mem_flat_body, nbatch=nbatch),
        grid=(depth * nbatch,),
        in_specs=[pl.BlockSpec(memory_space=pl.ANY)] * 2,
        out_specs=[flat, flat],
        out_shape=[jax.ShapeDtypeStruct((depth, nbatch, N_MEM, D_MODEL), BF16)] * 2,
        scratch_shapes=[pltpu.VMEM((2, X_HEADS, N_MEM, X_HEAD_DIM), F32)] * 2
        + [pltpu.SemaphoreType.DMA((2, 2, X_HEADS))],
        compiler_params=_params(1),
        name="mem_flat",
    )(mem_k, mem_v)


def _mem_kv_body(m_ref, w_ref, k_ref, v_ref, kf_ref, vf_ref, wb_ref):
    @pl.when(pl.program_id(1) == 0)
    def _():
        wb_ref[...] = w_ref[0].astype(BF16)

    kv = _dot(m_ref[0].astype(BF16), wb_ref[...])
    kf_ref[0, 0] = kv[:, :D_MODEL].astype(BF16)
    vf_ref[0, 0] = kv[:, D_MODEL:].astype(BF16)
    for hd in range(X_HEADS):
        k_ref[0, 0, :, hd, :] = kv[:, hd * X_HEAD_DIM:(hd + 1) * X_HEAD_DIM]
        v_ref[0, 0, :, hd, :] = kv[:, D_MODEL + hd * X_HEAD_DIM:D_MODEL + (hd + 1) * X_HEAD_DIM]


def _mem_kv(mem, w_kv):
    bsz = mem.shape[0]
    heads = pl.BlockSpec((1, 1, N_MEM, X_HEADS, X_HEAD_DIM), lambda l, b: (l, b, 0, 0, 0))
    flat = pl.BlockSpec((1, 1, N_MEM, D_MODEL), lambda l, b: (l, b, 0, 0))
    return pl.pallas_call(
        _mem_kv_body,
        grid=(DEPTH, bsz),
        in_specs=[pl.BlockSpec((1, N_MEM, D_MODEL), lambda l, b: (b, 0, 0)),
                  pl.BlockSpec((1, D_MODEL, 2 * D_MODEL), lambda l, b: (l, 0, 0))],
        out_specs=[heads, heads, flat, flat],
        out_shape=[jax.ShapeDtypeStruct((DEPTH, bsz, N_MEM, X_HEADS, X_HEAD_DIM), F32)] * 2
        + [jax.ShapeDtypeStruct((DEPTH, bsz, N_MEM, D_MODEL), BF16)] * 2,
        scratch_shapes=[pltpu.VMEM((D_MODEL, 2 * D_MODEL), BF16)],
        compiler_params=_params(2),
        name="mem_kv",
    )(mem, w_kv)


def _project_in(x, start, w, cfg):
    bsz, seq, _ = x.shape
    nseq = cfg["nseq"]
    xg = x.reshape(bsz // nseq, nseq * seq, D_MODEL)
    return (xg,) + tuple(_inproj_even(xg, w["norm_mix"], w["w_in_even"], start, seq, 0, cfg["tm_in"]))


def _window_attention(q, k, v, past_k, past_v, bsz, start, w, cfg):
    seq = q.shape[0] * q.shape[1] // bsz
    per_seq = lambda t: t.reshape(bsz, seq, t.shape[-1])
    k, v = per_seq(k), per_seq(v)
    yb = _swa(per_seq(q), k, v, past_k.reshape(bsz, WINDOW, KV_WIDTH), past_v.reshape(bsz, WINDOW, KV_WIDTH),
              w["attn_sinks"], start, cfg["tq"], cfg["qg"])
    keep = min(seq, WINDOW)
    new_k = k[:, seq - keep:].reshape(1, bsz, keep, N_KV_HEADS, HEAD_DIM)
    new_v = v[:, seq - keep:].reshape(1, bsz, keep, N_KV_HEADS, HEAD_DIM)
    return yb.reshape(q.shape), new_k, new_v


def kernel(x_prompt, x_sample, mem_prompt, cache_win_k, cache_win_v, state_ssm_re, state_ssm_im, state_conv_mix, state_conv_ffn, cache_mem_k, cache_mem_v, norm_mix, norm_xattn, norm_ffn, norm_final, w_in_even, w_out_even, ssm_a_re, ssm_a_im, ssm_log_dt, ssm_b_re, ssm_b_im, ssm_c_re, ssm_c_im, ssm_d, w_glu, attn_sinks, w_in_odd, conv_mix_w, w_out_odd, xattn_wq, xattn_wkv, xattn_wo, ffn_w_up, ffn_conv_w, ffn_w_down):
    bp, seq_p, _ = x_prompt.shape
    bs, seq_s, _ = x_sample.shape
    gains = lambda g: g.reshape(-1, 1, D_MODEL)
    w = dict(norm_mix=gains(norm_mix), norm_xattn=gains(norm_xattn), norm_ffn=gains(norm_ffn),
             norm_final=gains(norm_final),
             w_in_even=w_in_even, w_out_even=w_out_even,
             ssm_d=ssm_d.reshape(1, SSM_WIDTH), w_glu=w_glu.astype(BF16), attn_sinks=attn_sinks.reshape(N_HEADS),
             w_in_odd=w_in_odd, conv_mix_w=conv_mix_w, w_out_odd=w_out_odd,
             xattn_wq=xattn_wq, xattn_wo=xattn_wo,
             ffn_w_up=ffn_w_up, ffn_conv_w=ffn_conv_w, ffn_w_down=ffn_w_down)

    gp = (SSM_GROUPS, SSM_STATE)
    a_re, a_im, log_dt = ssm_a_re.reshape(gp), ssm_a_im.reshape(gp), ssm_log_dt.reshape(gp)
    wg, wy = _s5_prep(a_re, a_im, log_dt, ssm_b_re.reshape(gp + (SSM_GROUP,)), ssm_b_im.reshape(gp + (SSM_GROUP,)),
                      ssm_c_re.reshape(SSM_GROUPS, SSM_GROUP, SSM_STATE), ssm_c_im.reshape(SSM_GROUPS, SSM_GROUP, SSM_STATE))
    ni_p, ni_s = S5_SEG_CHUNKS, seq_s // S5_T
    padded = lambda e: e + [1.0] * (-len(e) % SUBLANES)
    pow2, powr_p, powr_s = _s5_pow_table(
        [padded([float(S5_T * ni_p << m) for m in range(3)]),
         padded([float(S5_T * (i + 1)) for i in range(ni_p)]),
         padded([float(S5_T * (i + 1)) for i in range(ni_s)])], a_re, a_im, log_dt)
    cfg_p = dict(tm_in=2048, tm_even=1024, tm_odd=1024, tm_ffn=1024, tq=1024, qg=CHUNK, nseq=1)
    cfg_s = dict(tm_in=bs * seq_s, tq=seq_s, qg=seq_s, nseq=bs)

    mem_k_p, mem_v_p, mem_kf_p, mem_vf_p = _mem_kv(mem_prompt, xattn_wkv)
    cw = CONV_WIDTH - 1
    zk = jnp.zeros((1, bp, WINDOW, N_KV_HEADS, HEAD_DIM), F32)
    zs = jnp.zeros((1, bp, SSM_GROUPS, SSM_STATE), F32)
    zcm = jnp.zeros((1, bp, cw, D_MODEL), F32)
    zcf = jnp.zeros((DEPTH, bp, cw, D_FF), F32)
    mem_kf_s, mem_vf_s = _mem_flat(cache_mem_k, cache_mem_v)
    mem_p, mem_s = (mem_kf_p, mem_vf_p), (mem_kf_s, mem_vf_s)
    xp, u_p, q_p, kk_p, vv_p = _project_in(x_prompt, 0, w, cfg_p)
    xs, u_s, q_s, kk_s, vv_s = _project_in(x_sample, PAST_LEN, w, cfg_s)
    state = lambda re, im, b: jnp.concatenate([re.reshape(b, S5_STATES), im.reshape(b, S5_STATES)], axis=1)
    ya_p, ya_s, h_p, h_s = _s5_mixer(u_p, u_s, state(zs, zs, bp)[:, None, :], state(state_ssm_re, state_ssm_im, bs)[None],
                                     wg, wy, pow2, powr_p, powr_s, w["ssm_d"], w["w_glu"], 0, ni_p, ni_s)
    split = lambda h, b: (h[:, :S5_STATES].reshape(1, b, SSM_GROUPS, SSM_STATE),
                          h[:, S5_STATES:].reshape(1, b, SSM_GROUPS, SSM_STATE))
    (re_p, im_p), (re_s, im_s) = split(h_p[:, 0], bp), split(h_s[0], bs)
    yb_p, k_p, v_p = _window_attention(q_p, kk_p, vv_p, zk, zk, bp, 0, w, cfg_p)
    yb_s, k_s, v_s = _window_attention(q_s, kk_s, vv_s, cache_win_k, cache_win_v, bs, PAST_LEN, w, cfg_s)
    xp, xs = _mix_even_xattn(xp, xs, ya_p, ya_s, yb_p, yb_s, w["w_out_even"], w["norm_xattn"], w["xattn_wq"],
                             mem_p, mem_s, w["xattn_wo"], 0, 0, cfg_p["tm_even"])
    ffn = functools.partial(_ffn, g=w["norm_ffn"], w_up=w["ffn_w_up"], conv_w=w["ffn_conv_w"], w_down=w["ffn_w_down"],
                            prev_long=zcf, prev_short=state_conv_ffn, g_final=w["norm_final"], tm=cfg_p["tm_ffn"])
    xp, xs, cf0_p, cf0_s = ffn(xp, xs, layer=0, final=False)
    mix_odd = functools.partial(_mix_odd_xattn, gm=w["norm_mix"], w_in=w["w_in_odd"], conv_w=w["conv_mix_w"],
                                w_out=w["w_out_odd"], g=w["norm_xattn"], wq=w["xattn_wq"], wo=w["xattn_wo"],
                                layer=1, layer_o=0)
    xp, cm_p = mix_odd(xp, prev=zcm, mem=mem_p, tm=cfg_p["tm_odd"], nseq=1)
    xs, cm_s = mix_odd(xs, prev=state_conv_mix, mem=mem_s, tm=xs.shape[1], nseq=bs)
    y_p, y_s, cf1_p, cf1_s = ffn(xp, xs, layer=1, final=True)
    return (y_p, y_s.reshape(x_sample.shape), k_p, v_p, re_p, im_p, cm_p[None], jnp.stack([cf0_p, cf1_p]),
            mem_k_p, mem_v_p, k_s, v_s, re_s, im_s, cm_s[None], jnp.stack([cf0_s, cf1_s]))
```
